```python
import math
import jax, jax.numpy as jnp
from jax import lax
import numpy as np

D_MODEL = 1024
BATCH = 8
SEQ = 8192
DEPTH = 4

ATTN_HEADS_PER_GROUP = 8
ATTN_HEAD_DIM = 128
ATTN_WINDOWS = (128, 512, 2048)
ATTN_DILATIONS = (1, 4, 16)
N_ATTN_GROUPS = 3
ATTN_BLOCK = 128
ROPE_THETA = 500000.0
ROPE_DIM = ATTN_HEAD_DIM // 4
ATTN_QKV_WIDTH = N_ATTN_GROUPS * 3 * ATTN_HEADS_PER_GROUP * ATTN_HEAD_DIM
ATTN_OUT_WIDTH = ATTN_HEADS_PER_GROUP * ATTN_HEAD_DIM

SSM_EXPAND = 2
SSM_D_INNER = SSM_EXPAND * D_MODEL
SSM_HEAD_DIM = 64
SSM_HEADS = SSM_D_INNER // SSM_HEAD_DIM
SSM_STATE = 128
SSM_GROUPS = 8
SSM_CONV = 4
SSM_CHUNK = 128
SSM_CONV_DIM = SSM_D_INNER + 2 * SSM_GROUPS * SSM_STATE
SSM_IN_WIDTH = SSM_D_INNER + SSM_CONV_DIM + SSM_HEADS

D_FF = 2816
FFN_CONV = 3

NORM_EPS = 1e-5

kernel_name = "hybrid_dilated_attn_mamba2_convffn"


def rmsnorm(x, w):
    xf = x.astype(jnp.float32)
    y = xf * lax.rsqrt(jnp.mean(xf * xf, axis=-1, keepdims=True) + NORM_EPS)
    return (y * w.astype(jnp.float32)).astype(x.dtype)


def gated_group_rmsnorm(y, z, w, groups):
    g = y.astype(jnp.float32) * jax.nn.silu(z.astype(jnp.float32))
    shp = g.shape
    g = g.reshape(shp[:-1] + (groups, shp[-1] // groups))
    g = g * lax.rsqrt(jnp.mean(g * g, axis=-1, keepdims=True) + NORM_EPS)
    return g.reshape(shp) * w.astype(jnp.float32)


def causal_depthwise_conv(x, w, b):
    k = w.shape[0]
    y = lax.conv_general_dilated(
        x, w[:, None, :].astype(x.dtype), window_strides=(1,), padding=[(k - 1, 0)],
        dimension_numbers=("NWC", "WIO", "NWC"), feature_group_count=x.shape[-1])
    return y + b.astype(x.dtype)


def rope_tables(seq):
    pos = jnp.arange(seq, dtype=jnp.float32)
    inv_freq = ROPE_THETA ** (-jnp.arange(0, ROPE_DIM, 2, dtype=jnp.float32) / ROPE_DIM)
    ang = pos[:, None] * inv_freq[None, :]
    return jnp.cos(ang), jnp.sin(ang)


def apply_partial_rope(t, cos, sin):
    t = t.astype(jnp.float32)
    half = ROPE_DIM // 2
    c = cos[:, None, None, :]
    s = sin[:, None, None, :]
    x1 = t[..., :half]
    x2 = t[..., half:ROPE_DIM]
    return jnp.concatenate([x1 * c - x2 * s, x2 * c + x1 * s, t[..., ROPE_DIM:]], axis=-1)


def dilated_window_attention(q, k, v, dilation, steps):
    bsz, s, h, hd = q.shape
    length = s // dilation
    nb = -(-length // ATTN_BLOCK)
    lp = nb * ATTN_BLOCK

    def to_strided(t):
        t = t.reshape(bsz, length, dilation, h, hd).transpose(0, 2, 3, 1, 4)
        t = jnp.pad(t, ((0, 0), (0, 0), (0, 0), (0, lp - length), (0, 0)))
        return t.reshape(bsz, dilation, h, nb, ATTN_BLOCK, hd)

    def with_prev(t):
        prev = jnp.pad(t, ((0, 0), (0, 0), (0, 0), (1, 0), (0, 0), (0, 0)))[:, :, :, :-1]
        return jnp.concatenate([prev, t], axis=-2)

    qb = to_strided(q)
    kk = with_prev(to_strided(k))
    vv = with_prev(to_strided(v))
    scores = jnp.einsum("brhnqe,brhnke->brhnqk", qb, kk) * (hd ** -0.5)
    n_idx = jnp.arange(nb)[:, None, None]
    i_idx = jnp.arange(ATTN_BLOCK)[None, :, None]
    j_idx = jnp.arange(2 * ATTN_BLOCK)[None, None, :]
    delta = ATTN_BLOCK + i_idx - j_idx
    key_pos = (n_idx - 1) * ATTN_BLOCK + j_idx
    allowed = (delta >= 0) & (delta <= steps) & (key_pos >= 0)
    scores = jnp.where(allowed, scores, -jnp.inf)
    m = jnp.max(scores, axis=-1, keepdims=True)
    p = jnp.exp(scores - m)
    den = jnp.sum(p, axis=-1, keepdims=True)
    o = jnp.einsum("brhnqk,brhnke->brhnqe", p, vv) / den
    lse = (m + jnp.log(den))[..., 0]
    o = o.reshape(bsz, dilation, h, lp, hd)[:, :, :, :length]
    o = o.transpose(0, 3, 1, 2, 4).reshape(bsz, s, h, hd)
    lse = lse.reshape(bsz, dilation, h, lp)[:, :, :, :length]
    lse = lse.transpose(0, 3, 1, 2).reshape(bsz, s, h)
    return o, lse


def dilated_attention_mixer(h, w_qkv, w_o, cos, sin):
    bsz, s, _ = h.shape
    qkv = (h @ w_qkv).reshape(bsz, s, N_ATTN_GROUPS, 3, ATTN_HEADS_PER_GROUP, ATTN_HEAD_DIM)
    q = apply_partial_rope(qkv[:, :, :, 0], cos, sin)
    k = apply_partial_rope(qkv[:, :, :, 1], cos, sin)
    v = qkv[:, :, :, 2].astype(jnp.float32)
    outs, lses = [], []
    for g in range(N_ATTN_GROUPS):
        dil = ATTN_DILATIONS[g]
        o_g, lse_g = dilated_window_attention(q[:, :, g], k[:, :, g], v[:, :, g], dil,
                                              ATTN_WINDOWS[g] // dil)
        outs.append(o_g)
        lses.append(lse_g)
    wts = jax.nn.softmax(jnp.stack(lses, axis=2), axis=2)
    o = jnp.einsum("bsgh,bsghe->bshe", wts, jnp.stack(outs, axis=2))
    return o.reshape(bsz, s, ATTN_OUT_WIDTH).astype(h.dtype) @ w_o


def ssd_chunked_scan(x, dt, a, bm, cm):
    b, s, h, p = x.shape
    g, n = bm.shape[2], bm.shape[3]
    r = h // g
    c = s // SSM_CHUNK
    q = SSM_CHUNK
    x = x.reshape(b, c, q, g, r, p)
    dt = dt.reshape(b, c, q, g, r)
    bm = bm.reshape(b, c, q, g, n)
    cm = cm.reshape(b, c, q, g, n)
    a_dt = dt * a.reshape(g, r)
    a_cs = jnp.cumsum(a_dt, axis=2)
    xdt = x * dt[..., None]
    seg = a_cs[:, :, :, None] - a_cs[:, :, None, :]
    causal = jnp.tril(jnp.ones((q, q), dtype=bool))[:, :, None, None]
    lmat = jnp.exp(jnp.where(causal, seg, -jnp.inf))
    cb = jnp.einsum("bcign,bcjgn->bcijg", cm, bm)
    y_diag = jnp.einsum("bcijgr,bcjgrp->bcigrp", cb[..., None] * lmat, xdt)
    decay = jnp.exp(a_cs[:, :, -1:] - a_cs)
    states = jnp.einsum("bcjgn,bcjgr,bcjgrp->bcgrpn", bm, decay, xdt)
    chunk_decay = jnp.exp(a_cs[:, :, -1])

    def step(state, inp):
        st_c, dec_c = inp
        return state * dec_c[..., None, None] + st_c, state

    init = jnp.zeros((b, g, r, p, n), dtype=x.dtype)
    _, prev = lax.scan(step, init, (jnp.moveaxis(states, 1, 0), jnp.moveaxis(chunk_decay, 1, 0)))
    prev = jnp.moveaxis(prev, 0, 1)
    y_off = jnp.einsum("bcign,bcgrpn->bcigrp", cm, prev) * jnp.exp(a_cs)[..., None]
    return (y_diag + y_off).reshape(b, s, h, p)


def ssd_mixer(h, w_in, conv_w, conv_b, dt_bias, a_log, d_skip, norm_w, w_out):
    bsz, s, _ = h.shape
    gn = SSM_GROUPS * SSM_STATE
    zxbcdt = h @ w_in
    z = zxbcdt[..., :SSM_D_INNER]
    xbc = zxbcdt[..., SSM_D_INNER:SSM_D_INNER + SSM_CONV_DIM]
    dt_raw = zxbcdt[..., SSM_D_INNER + SSM_CONV_DIM:]
    xbc = jax.nn.silu(causal_depthwise_conv(xbc, conv_w, conv_b))
    xs = xbc[..., :SSM_D_INNER].reshape(bsz, s, SSM_HEADS, SSM_HEAD_DIM).astype(jnp.float32)
    bm = xbc[..., SSM_D_INNER:SSM_D_INNER + gn].reshape(bsz, s, SSM_GROUPS, SSM_STATE)
    cm = xbc[..., SSM_D_INNER + gn:].reshape(bsz, s, SSM_GROUPS, SSM_STATE)
    dt = jax.nn.softplus(dt_raw.astype(jnp.float32) + dt_bias.astype(jnp.float32))
    a = -jnp.exp(a_log.astype(jnp.float32))
    y = ssd_chunked_scan(xs, dt, a, bm.astype(jnp.float32), cm.astype(jnp.float32))
    y = y + d_skip.astype(jnp.float32)[:, None] * xs
    y = gated_group_rmsnorm(y.reshape(bsz, s, SSM_D_INNER), z, norm_w, SSM_GROUPS)
    return y.astype(h.dtype) @ w_out


def conv_ffn(h, w_up, conv_w, conv_b, w_down):
    u = causal_depthwise_conv(h @ w_up, conv_w, conv_b)
    gate, up = u[..., :D_FF], u[..., D_FF:]
    return (jax.nn.silu(gate) * up) @ w_down


def _fwd_setup_inputs(seed: int = 0) -> dict:
    key = jax.random.key(seed)
    ks = jax.random.split(key, 20)
    n_attn = (DEPTH + 1) // 2
    n_ssm = DEPTH // 2
    f32 = jnp.float32

    def nrm(k, shape, scale):
        return jax.random.normal(k, shape, dtype=f32) * scale

    u = jax.random.uniform(ks[7], (n_ssm, SSM_HEADS), dtype=f32)
    dt0 = jnp.exp(u * (math.log(0.1) - math.log(0.001)) + math.log(0.001))
    dt0 = jnp.maximum(dt0, 1e-4)
    dt_bias = dt0 + jnp.log(-jnp.expm1(-dt0))
    a_log = jnp.log(jax.random.uniform(ks[8], (n_ssm, SSM_HEADS), dtype=f32, minval=1.0, maxval=16.0))
    return {
        "x": nrm(ks[0], (BATCH, SEQ, D_MODEL), 1.0),
        "mix_norm_w": 1.0 + nrm(ks[1], (DEPTH, D_MODEL), 0.02),
        "attn_w_qkv": nrm(ks[2], (n_attn, D_MODEL, ATTN_QKV_WIDTH), D_MODEL ** -0.5),
        "attn_w_o": nrm(ks[3], (n_attn, ATTN_OUT_WIDTH, D_MODEL), ATTN_OUT_WIDTH ** -0.5),
        "ssm_w_in": nrm(ks[4], (n_ssm, D_MODEL, SSM_IN_WIDTH), D_MODEL ** -0.5),
        "ssm_conv_w": nrm(ks[5], (n_ssm, SSM_CONV, SSM_CONV_DIM), SSM_CONV ** -0.5),
        "ssm_conv_b": nrm(ks[6], (n_ssm, SSM_CONV_DIM), 0.01),
        "ssm_dt_bias": dt_bias,
        "ssm_a_log": a_log,
        "ssm_d": 1.0 + nrm(ks[9], (n_ssm, SSM_HEADS), 0.1),
        "ssm_norm_w": 1.0 + nrm(ks[10], (n_ssm, SSM_D_INNER), 0.02),
        "ssm_w_out": nrm(ks[11], (n_ssm, SSM_D_INNER, D_MODEL), SSM_D_INNER ** -0.5),
        "ffn_norm_w": 1.0 + nrm(ks[12], (DEPTH, D_MODEL), 0.02),
        "ffn_w_up": nrm(ks[13], (DEPTH, D_MODEL, 2 * D_FF), D_MODEL ** -0.5),
        "ffn_conv_w": nrm(ks[14], (DEPTH, FFN_CONV, 2 * D_FF), FFN_CONV ** -0.5),
        "ffn_conv_b": nrm(ks[15], (DEPTH, 2 * D_FF), 0.01),
        "ffn_w_down": nrm(ks[16], (DEPTH, D_FF, D_MODEL), D_FF ** -0.5),
        "final_norm_w": 1.0 + nrm(ks[17], (D_MODEL,), 0.02),
    }


def _fwd_reference(x, mix_norm_w, attn_w_qkv, attn_w_o, ssm_w_in, ssm_conv_w, ssm_conv_b,
              ssm_dt_bias, ssm_a_log, ssm_d, ssm_norm_w, ssm_w_out, ffn_norm_w, ffn_w_up,
              ffn_conv_w, ffn_conv_b, ffn_w_down, final_norm_w):
    cos, sin = rope_tables(x.shape[1])
    for i in range(DEPTH):
        h = rmsnorm(x, mix_norm_w[i])
        j = i // 2
        if i % 2 == 0:
            x = x + dilated_attention_mixer(h, attn_w_qkv[j], attn_w_o[j], cos, sin)
        else:
            x = x + ssd_mixer(h, ssm_w_in[j], ssm_conv_w[j], ssm_conv_b[j], ssm_dt_bias[j],
                              ssm_a_log[j], ssm_d[j], ssm_norm_w[j], ssm_w_out[j])
        h = rmsnorm(x, ffn_norm_w[i])
        x = x + conv_ffn(h, ffn_w_up[i], ffn_conv_w[i], ffn_conv_b[i], ffn_w_down[i])
    return rmsnorm(x, final_norm_w)


import jax as _jax
import jax.numpy as _jnp

TWIN_FORMAT = 'train_step'
FWD_PARAMS = ['x', 'mix_norm_w', 'attn_w_qkv', 'attn_w_o', 'ssm_w_in', 'ssm_conv_w', 'ssm_conv_b', 'ssm_dt_bias', 'ssm_a_log', 'ssm_d', 'ssm_norm_w', 'ssm_w_out', 'ffn_norm_w', 'ffn_w_up', 'ffn_conv_w', 'ffn_conv_b', 'ffn_w_down', 'final_norm_w']
TWIN_WEIGHTS = ['mix_norm_w', 'attn_w_qkv', 'attn_w_o', 'ssm_w_in', 'ssm_conv_w', 'ssm_conv_b', 'ssm_dt_bias', 'ssm_a_log', 'ssm_d', 'ssm_norm_w', 'ssm_w_out', 'ffn_norm_w', 'ffn_w_up', 'ffn_conv_w', 'ffn_conv_b', 'ffn_w_down', 'final_norm_w']
TWIN_DIFF_INPUT = 'x'
TWIN_INPUTS = ['x', 'mix_norm_w', 'attn_w_qkv', 'attn_w_o', 'ssm_w_in', 'ssm_conv_w', 'ssm_conv_b', 'ssm_dt_bias', 'ssm_a_log', 'ssm_d', 'ssm_norm_w', 'ssm_w_out', 'ffn_norm_w', 'ffn_w_up', 'ffn_conv_w', 'ffn_conv_b', 'ffn_w_down', 'final_norm_w', 'loss_target', 'm_mix_norm_w', 'm_attn_w_qkv', 'm_attn_w_o', 'm_ssm_w_in', 'm_ssm_conv_w', 'm_ssm_conv_b', 'm_ssm_dt_bias', 'm_ssm_a_log', 'm_ssm_d', 'm_ssm_norm_w', 'm_ssm_w_out', 'm_ffn_norm_w', 'm_ffn_w_up', 'm_ffn_conv_w', 'm_ffn_conv_b', 'm_ffn_w_down', 'm_final_norm_w', 'v_mix_norm_w', 'v_attn_w_qkv', 'v_attn_w_o', 'v_ssm_w_in', 'v_ssm_conv_w', 'v_ssm_conv_b', 'v_ssm_dt_bias', 'v_ssm_a_log', 'v_ssm_d', 'v_ssm_norm_w', 'v_ssm_w_out', 'v_ffn_norm_w', 'v_ffn_w_up', 'v_ffn_conv_w', 'v_ffn_conv_b', 'v_ffn_w_down', 'v_final_norm_w']
TWIN_OUTPUTS = ['loss', 'grad_x', 'grad_mix_norm_w', 'grad_attn_w_qkv', 'grad_attn_w_o', 'grad_ssm_w_in', 'grad_ssm_conv_w', 'grad_ssm_conv_b', 'grad_ssm_dt_bias', 'grad_ssm_a_log', 'grad_ssm_d', 'grad_ssm_norm_w', 'grad_ssm_w_out', 'grad_ffn_norm_w', 'grad_ffn_w_up', 'grad_ffn_conv_w', 'grad_ffn_conv_b', 'grad_ffn_w_down', 'grad_final_norm_w', 'delta_mix_norm_w', 'delta_attn_w_qkv', 'delta_attn_w_o', 'delta_ssm_w_in', 'delta_ssm_conv_w', 'delta_ssm_conv_b', 'delta_ssm_dt_bias', 'delta_ssm_a_log', 'delta_ssm_d', 'delta_ssm_norm_w', 'delta_ssm_w_out', 'delta_ffn_norm_w', 'delta_ffn_w_up', 'delta_ffn_conv_w', 'delta_ffn_conv_b', 'delta_ffn_w_down', 'delta_final_norm_w', 'new_m_mix_norm_w', 'new_m_attn_w_qkv', 'new_m_attn_w_o', 'new_m_ssm_w_in', 'new_m_ssm_conv_w', 'new_m_ssm_conv_b', 'new_m_ssm_dt_bias', 'new_m_ssm_a_log', 'new_m_ssm_d', 'new_m_ssm_norm_w', 'new_m_ssm_w_out', 'new_m_ffn_norm_w', 'new_m_ffn_w_up', 'new_m_ffn_conv_w', 'new_m_ffn_conv_b', 'new_m_ffn_w_down', 'new_m_final_norm_w', 'new_v_mix_norm_w', 'new_v_attn_w_qkv', 'new_v_attn_w_o', 'new_v_ssm_w_in', 'new_v_ssm_conv_w', 'new_v_ssm_conv_b', 'new_v_ssm_dt_bias', 'new_v_ssm_a_log', 'new_v_ssm_d', 'new_v_ssm_norm_w', 'new_v_ssm_w_out', 'new_v_ffn_norm_w', 'new_v_ffn_w_up', 'new_v_ffn_conv_w', 'new_v_ffn_conv_b', 'new_v_ffn_w_down', 'new_v_final_norm_w']
TWIN_LEAF_KINDS = {'loss': 'loss', 'grad_x': 'grad_x', 'grad_mix_norm_w': 'grad_w', 'grad_attn_w_qkv': 'grad_w', 'grad_attn_w_o': 'grad_w', 'grad_ssm_w_in': 'grad_w', 'grad_ssm_conv_w': 'grad_w', 'grad_ssm_conv_b': 'grad_w', 'grad_ssm_dt_bias': 'grad_w', 'grad_ssm_a_log': 'grad_w', 'grad_ssm_d': 'grad_w', 'grad_ssm_norm_w': 'grad_w', 'grad_ssm_w_out': 'grad_w', 'grad_ffn_norm_w': 'grad_w', 'grad_ffn_w_up': 'grad_w', 'grad_ffn_conv_w': 'grad_w', 'grad_ffn_conv_b': 'grad_w', 'grad_ffn_w_down': 'grad_w', 'grad_final_norm_w': 'grad_w', 'delta_mix_norm_w': 'delta_w', 'delta_attn_w_qkv': 'delta_w', 'delta_attn_w_o': 'delta_w', 'delta_ssm_w_in': 'delta_w', 'delta_ssm_conv_w': 'delta_w', 'delta_ssm_conv_b': 'delta_w', 'delta_ssm_dt_bias': 'delta_w', 'delta_ssm_a_log': 'delta_w', 'delta_ssm_d': 'delta_w', 'delta_ssm_norm_w': 'delta_w', 'delta_ssm_w_out': 'delta_w', 'delta_ffn_norm_w': 'delta_w', 'delta_ffn_w_up': 'delta_w', 'delta_ffn_conv_w': 'delta_w', 'delta_ffn_conv_b': 'delta_w', 'delta_ffn_w_down': 'delta_w', 'delta_final_norm_w': 'delta_w', 'new_m_mix_norm_w': 'new_m', 'new_m_attn_w_qkv': 'new_m', 'new_m_attn_w_o': 'new_m', 'new_m_ssm_w_in': 'new_m', 'new_m_ssm_conv_w': 'new_m', 'new_m_ssm_conv_b': 'new_m', 'new_m_ssm_dt_bias': 'new_m', 'new_m_ssm_a_log': 'new_m', 'new_m_ssm_d': 'new_m', 'new_m_ssm_norm_w': 'new_m', 'new_m_ssm_w_out': 'new_m', 'new_m_ffn_norm_w': 'new_m', 'new_m_ffn_w_up': 'new_m', 'new_m_ffn_conv_w': 'new_m', 'new_m_ffn_conv_b': 'new_m', 'new_m_ffn_w_down': 'new_m', 'new_m_final_norm_w': 'new_m', 'new_v_mix_norm_w': 'new_v', 'new_v_attn_w_qkv': 'new_v', 'new_v_attn_w_o': 'new_v', 'new_v_ssm_w_in': 'new_v', 'new_v_ssm_conv_w': 'new_v', 'new_v_ssm_conv_b': 'new_v', 'new_v_ssm_dt_bias': 'new_v', 'new_v_ssm_a_log': 'new_v', 'new_v_ssm_d': 'new_v', 'new_v_ssm_norm_w': 'new_v', 'new_v_ssm_w_out': 'new_v', 'new_v_ffn_norm_w': 'new_v', 'new_v_ffn_w_up': 'new_v', 'new_v_ffn_conv_w': 'new_v', 'new_v_ffn_conv_b': 'new_v', 'new_v_ffn_w_down': 'new_v', 'new_v_final_norm_w': 'new_v'}


def _forward(args):
    return _fwd_reference(*[args[k] for k in FWD_PARAMS])


def _output_shape():
    def fwd():
        inp = _fwd_setup_inputs(0)
        return _fwd_reference(*[inp[k] for k in FWD_PARAMS])
    out = _jax.eval_shape(fwd)
    return out.shape, out.dtype

N_MICROBATCH = 1
ADAM_LR = 0.001
ADAM_B1 = 0.9
ADAM_B2 = 0.999
ADAM_EPS = 1e-08
ADAM_WD = 0.01
ADAM_STEP = 10
PER_EXAMPLE_BATCH_AXIS = {'x': 0, 'loss_target': 0}
SHARED_INPUTS = []
_WEIGHT_DTYPES = {'mix_norm_w': _jnp.float32, 'attn_w_qkv': _jnp.float32, 'attn_w_o': _jnp.float32, 'ssm_w_in': _jnp.float32, 'ssm_conv_w': _jnp.float32, 'ssm_conv_b': _jnp.float32, 'ssm_dt_bias': _jnp.float32, 'ssm_a_log': _jnp.float32, 'ssm_d': _jnp.float32, 'ssm_norm_w': _jnp.float32, 'ssm_w_out': _jnp.float32, 'ffn_norm_w': _jnp.float32, 'ffn_w_up': _jnp.float32, 'ffn_conv_w': _jnp.float32, 'ffn_conv_b': _jnp.float32, 'ffn_w_down': _jnp.float32, 'final_norm_w': _jnp.float32}
MOMENT_SCALE = {'mix_norm_w': 2.053202e-01, 'attn_w_qkv': 3.618496e-02, 'attn_w_o': 7.017477e-02, 'ssm_w_in': 1.093179e-01, 'ssm_conv_w': 9.775129e-02, 'ssm_conv_b': 1.509090e-01, 'ssm_dt_bias': 2.483128e-01, 'ssm_a_log': 3.469738e-01, 'ssm_d': 7.656343e-01, 'ssm_norm_w': 1.356317e-01, 'ssm_w_out': 1.838720e-01, 'ffn_norm_w': 2.047839e-01, 'ffn_w_up': 8.470000e-02, 'ffn_conv_w': 8.600983e-02, 'ffn_conv_b': 8.871278e-02, 'ffn_w_down': 1.385139e-01, 'final_norm_w': 6.404797e+01}


def _to_microbatches(a, axis):
    t = _jnp.moveaxis(a, axis, 0)
    t = t.reshape((N_MICROBATCH, t.shape[0] // N_MICROBATCH) + t.shape[1:])
    return _jnp.moveaxis(t, 1, axis + 1)


def setup_inputs(seed: int = 0) -> dict:
    inp = _fwd_setup_inputs(seed)
    key = _jax.random.fold_in(_jax.random.key(seed), 7919)
    shape, _ = _output_shape()
    out = dict(inp)
    out["loss_target"] = _jax.random.normal(_jax.random.fold_in(key, 0), shape, _jnp.float32)
    for i, name in enumerate(TWIN_WEIGHTS):
        w = inp[name].astype(_jnp.float32)
        if MOMENT_SCALE is None:
            s = _jnp.sqrt(_jnp.mean(_jnp.square(w)) + 1e-30)
        else:
            s = MOMENT_SCALE[name]
        km, kv = _jax.random.split(_jax.random.fold_in(key, i + 1))
        out[name] = w
        out["m_" + name] = s * _jax.random.normal(km, w.shape, _jnp.float32)
        out["v_" + name] = (s * s) * _jax.random.uniform(kv, w.shape, _jnp.float32, 0.5, 1.5)
    if N_MICROBATCH > 1:
        for name, axis in PER_EXAMPLE_BATCH_AXIS.items():
            out[name] = _to_microbatches(out[name], axis)
    return {'x': out['x'], 'mix_norm_w': out['mix_norm_w'], 'attn_w_qkv': out['attn_w_qkv'], 'attn_w_o': out['attn_w_o'], 'ssm_w_in': out['ssm_w_in'], 'ssm_conv_w': out['ssm_conv_w'], 'ssm_conv_b': out['ssm_conv_b'], 'ssm_dt_bias': out['ssm_dt_bias'], 'ssm_a_log': out['ssm_a_log'], 'ssm_d': out['ssm_d'], 'ssm_norm_w': out['ssm_norm_w'], 'ssm_w_out': out['ssm_w_out'], 'ffn_norm_w': out['ffn_norm_w'], 'ffn_w_up': out['ffn_w_up'], 'ffn_conv_w': out['ffn_conv_w'], 'ffn_conv_b': out['ffn_conv_b'], 'ffn_w_down': out['ffn_w_down'], 'final_norm_w': out['final_norm_w'], 'loss_target': out['loss_target'], 'm_mix_norm_w': out['m_mix_norm_w'], 'm_attn_w_qkv': out['m_attn_w_qkv'], 'm_attn_w_o': out['m_attn_w_o'], 'm_ssm_w_in': out['m_ssm_w_in'], 'm_ssm_conv_w': out['m_ssm_conv_w'], 'm_ssm_conv_b': out['m_ssm_conv_b'], 'm_ssm_dt_bias': out['m_ssm_dt_bias'], 'm_ssm_a_log': out['m_ssm_a_log'], 'm_ssm_d': out['m_ssm_d'], 'm_ssm_norm_w': out['m_ssm_norm_w'], 'm_ssm_w_out': out['m_ssm_w_out'], 'm_ffn_norm_w': out['m_ffn_norm_w'], 'm_ffn_w_up': out['m_ffn_w_up'], 'm_ffn_conv_w': out['m_ffn_conv_w'], 'm_ffn_conv_b': out['m_ffn_conv_b'], 'm_ffn_w_down': out['m_ffn_w_down'], 'm_final_norm_w': out['m_final_norm_w'], 'v_mix_norm_w': out['v_mix_norm_w'], 'v_attn_w_qkv': out['v_attn_w_qkv'], 'v_attn_w_o': out['v_attn_w_o'], 'v_ssm_w_in': out['v_ssm_w_in'], 'v_ssm_conv_w': out['v_ssm_conv_w'], 'v_ssm_conv_b': out['v_ssm_conv_b'], 'v_ssm_dt_bias': out['v_ssm_dt_bias'], 'v_ssm_a_log': out['v_ssm_a_log'], 'v_ssm_d': out['v_ssm_d'], 'v_ssm_norm_w': out['v_ssm_norm_w'], 'v_ssm_w_out': out['v_ssm_w_out'], 'v_ffn_norm_w': out['v_ffn_norm_w'], 'v_ffn_w_up': out['v_ffn_w_up'], 'v_ffn_conv_w': out['v_ffn_conv_w'], 'v_ffn_conv_b': out['v_ffn_conv_b'], 'v_ffn_w_down': out['v_ffn_w_down'], 'v_final_norm_w': out['v_final_norm_w']}


def _loss(weights, diff, rest, loss_target):
    with _jax.named_scope("forward"):
        args = {**rest, TWIN_DIFF_INPUT: diff, **{k: w.astype(_WEIGHT_DTYPES[k]) for k, w in weights.items()}}
        y = _forward(args)
    with _jax.named_scope("loss_head"):
        err = _jnp.square(y.astype(_jnp.float32) - loss_target)
        return 0.5 * _jnp.sum(_jnp.mean(err, axis=-1)) if err.ndim else 0.5 * err


def _adamw(w, g, m, v):
    m = ADAM_B1 * m + (1.0 - ADAM_B1) * g
    v = ADAM_B2 * v + (1.0 - ADAM_B2) * _jnp.square(g)
    m_hat = m / (1.0 - ADAM_B1 ** ADAM_STEP)
    v_hat = v / (1.0 - ADAM_B2 ** ADAM_STEP)
    delta = -ADAM_LR * (m_hat / (_jnp.sqrt(v_hat) + ADAM_EPS) + ADAM_WD * w)
    return delta, m, v


def reference(x, mix_norm_w, attn_w_qkv, attn_w_o, ssm_w_in, ssm_conv_w, ssm_conv_b, ssm_dt_bias, ssm_a_log, ssm_d, ssm_norm_w, ssm_w_out, ffn_norm_w, ffn_w_up, ffn_conv_w, ffn_conv_b, ffn_w_down, final_norm_w, loss_target, m_mix_norm_w, m_attn_w_qkv, m_attn_w_o, m_ssm_w_in, m_ssm_conv_w, m_ssm_conv_b, m_ssm_dt_bias, m_ssm_a_log, m_ssm_d, m_ssm_norm_w, m_ssm_w_out, m_ffn_norm_w, m_ffn_w_up, m_ffn_conv_w, m_ffn_conv_b, m_ffn_w_down, m_final_norm_w, v_mix_norm_w, v_attn_w_qkv, v_attn_w_o, v_ssm_w_in, v_ssm_conv_w, v_ssm_conv_b, v_ssm_dt_bias, v_ssm_a_log, v_ssm_d, v_ssm_norm_w, v_ssm_w_out, v_ffn_norm_w, v_ffn_w_up, v_ffn_conv_w, v_ffn_conv_b, v_ffn_w_down, v_final_norm_w):
    given = dict(x=x, mix_norm_w=mix_norm_w, attn_w_qkv=attn_w_qkv, attn_w_o=attn_w_o, ssm_w_in=ssm_w_in, ssm_conv_w=ssm_conv_w, ssm_conv_b=ssm_conv_b, ssm_dt_bias=ssm_dt_bias, ssm_a_log=ssm_a_log, ssm_d=ssm_d, ssm_norm_w=ssm_norm_w, ssm_w_out=ssm_w_out, ffn_norm_w=ffn_norm_w, ffn_w_up=ffn_w_up, ffn_conv_w=ffn_conv_w, ffn_conv_b=ffn_conv_b, ffn_w_down=ffn_w_down, final_norm_w=final_norm_w, loss_target=loss_target, m_mix_norm_w=m_mix_norm_w, m_attn_w_qkv=m_attn_w_qkv, m_attn_w_o=m_attn_w_o, m_ssm_w_in=m_ssm_w_in, m_ssm_conv_w=m_ssm_conv_w, m_ssm_conv_b=m_ssm_conv_b, m_ssm_dt_bias=m_ssm_dt_bias, m_ssm_a_log=m_ssm_a_log, m_ssm_d=m_ssm_d, m_ssm_norm_w=m_ssm_norm_w, m_ssm_w_out=m_ssm_w_out, m_ffn_norm_w=m_ffn_norm_w, m_ffn_w_up=m_ffn_w_up, m_ffn_conv_w=m_ffn_conv_w, m_ffn_conv_b=m_ffn_conv_b, m_ffn_w_down=m_ffn_w_down, m_final_norm_w=m_final_norm_w, v_mix_norm_w=v_mix_norm_w, v_attn_w_qkv=v_attn_w_qkv, v_attn_w_o=v_attn_w_o, v_ssm_w_in=v_ssm_w_in, v_ssm_conv_w=v_ssm_conv_w, v_ssm_conv_b=v_ssm_conv_b, v_ssm_dt_bias=v_ssm_dt_bias, v_ssm_a_log=v_ssm_a_log, v_ssm_d=v_ssm_d, v_ssm_norm_w=v_ssm_norm_w, v_ssm_w_out=v_ssm_w_out, v_ffn_norm_w=v_ffn_norm_w, v_ffn_w_up=v_ffn_w_up, v_ffn_conv_w=v_ffn_conv_w, v_ffn_conv_b=v_ffn_conv_b, v_ffn_w_down=v_ffn_w_down, v_final_norm_w=v_final_norm_w)
    weights = {n: given[n] for n in TWIN_WEIGHTS}
    shared = {n: given[n] for n in SHARED_INPUTS}
    per_example = {n: given[n] for n in ['x']}
    grad_fn = _jax.value_and_grad(_loss, argnums=(0, 1))

    def one_microbatch(ex, loss_target):
        ex = dict(ex)
        diff = ex.pop(TWIN_DIFF_INPUT)
        return grad_fn(weights, diff, {**shared, **ex}, loss_target)

    if N_MICROBATCH == 1:
        loss, (grad_w, grad_x) = one_microbatch(per_example, given["loss_target"])
    else:
        def body(carry, xs):
            loss_sum, grad_sum = carry
            l_k, (gw_k, gx_k) = one_microbatch(xs[0], xs[1])
            with _jax.named_scope("update"):
                return (loss_sum + l_k, _jax.tree.map(_jnp.add, grad_sum, gw_k)), gx_k

        init = (_jnp.zeros((), _jnp.float32), _jax.tree.map(_jnp.zeros_like, weights))
        (loss, grad_w), grad_x = _jax.lax.scan(body, init, (per_example, given["loss_target"]))
    with _jax.named_scope("update"):
        delta_w, new_m, new_v = {}, {}, {}
        for n in TWIN_WEIGHTS:
            delta_w[n], new_m[n], new_v[n] = _adamw(weights[n], grad_w[n], given["m_" + n], given["v_" + n])
    return (loss, grad_x, *[grad_w[n] for n in TWIN_WEIGHTS], *[delta_w[n] for n in TWIN_WEIGHTS],
            *[new_m[n] for n in TWIN_WEIGHTS], *[new_v[n] for n in TWIN_WEIGHTS])
```

```python
import functools
import math

import jax
import jax.numpy as jnp
from jax import lax
from jax.experimental import pallas as pl
from jax.experimental.pallas import tpu as pltpu

F32 = jnp.float32
BF16 = jnp.bfloat16
N_DEV = 8
MESH_ID = pl.DeviceIdType.MESH

D_MODEL = 1024
DEPTH = 4
ATTN_HEADS = 8
ATTN_HEAD_DIM = 128
ATTN_DILATIONS = (1, 4, 16)
ATTN_STEPS = (128, 128, 128)
N_ATTN_GROUPS = 3
ATTN_BLOCK = 128
ROPE_THETA = 500000.0
ROPE_DIM = 32
ATTN_OUT_WIDTH = 1024
SSM_D_INNER = 2048
SSM_HEAD_DIM = 64
SSM_HEADS = 32
SSM_STATE = 128
SSM_GROUPS = 8
SSM_CHUNK = 128
SSM_CONV_DIM = 4096
SSM_IN_WIDTH = 6176
SSM_IN_PAD = 6272
D_FF = 2816
NORM_EPS = 1e-5
ADAM_LR = 0.001
ADAM_B1 = 0.9
ADAM_B2 = 0.999
ADAM_EPS = 1e-08
ADAM_WD = 0.01
ADAM_STEP = 10

WEIGHT_NAMES = ['mix_norm_w', 'attn_w_qkv', 'attn_w_o', 'ssm_w_in', 'ssm_conv_w', 'ssm_conv_b', 'ssm_dt_bias',
                'ssm_a_log', 'ssm_d', 'ssm_norm_w', 'ssm_w_out', 'ffn_norm_w', 'ffn_w_up', 'ffn_conv_w',
                'ffn_conv_b', 'ffn_w_down', 'final_norm_w']
SHARD_AXIS = {'mix_norm_w': None, 'attn_w_qkv': 2, 'attn_w_o': 1, 'ssm_w_in': 2, 'ssm_conv_w': 2, 'ssm_conv_b': 1,
              'ssm_dt_bias': None, 'ssm_a_log': None, 'ssm_d': None, 'ssm_norm_w': 1, 'ssm_w_out': 1,
              'ffn_norm_w': None, 'ffn_w_up': 2, 'ffn_conv_w': 2, 'ffn_conv_b': None, 'ffn_w_down': 1,
              'final_norm_w': None}
BIG = ['attn_w_qkv', 'attn_w_o', 'ssm_w_in', 'ssm_w_out', 'ffn_w_up', 'ffn_w_down']
SMALL = [n for n in WEIGHT_NAMES if n not in BIG]
SMALL_SHARDED = [n for n in SMALL if SHARD_AXIS[n] is not None]
LANES = 1024


def _my_index():
    return 4 * lax.axis_index("x") + 2 * lax.axis_index("y") + lax.axis_index("c")


def _peer(k):
    x, y, c = lax.axis_index("x"), lax.axis_index("y"), lax.axis_index("c")
    return (x ^ ((k >> 2) & 1), y ^ ((k >> 1) & 1), c ^ (k & 1))


def _exchange(src, gather, name):
    shape = src.shape[-2:]

    def body(src_ref, out_ref, send_sems, recv_sems, local_sem):
        me = _my_index()

        def piece(j):
            return src_ref if gather else src_ref.at[j]

        mine = pltpu.make_async_copy(piece(me), out_ref.at[me], local_sem)
        mine.start()
        copies = []
        for k in range(1, N_DEV):
            other = me ^ k
            cp = pltpu.make_async_remote_copy(
                src_ref=piece(other), dst_ref=out_ref.at[me], send_sem=send_sems.at[k - 1],
                recv_sem=recv_sems.at[k - 1], device_id=_peer(k), device_id_type=MESH_ID)
            cp.start()
            copies.append(cp)
        for k in range(1, N_DEV):
            other = me ^ k
            pltpu.make_async_remote_copy(
                src_ref=piece(other), dst_ref=out_ref.at[other], send_sem=send_sems.at[k - 1],
                recv_sem=recv_sems.at[k - 1], device_id=_peer(k), device_id_type=MESH_ID).wait_recv()
        for cp in copies:
            cp.wait_send()
        mine.wait()

    return pl.pallas_call(
        body, name=name,
        out_shape=jax.ShapeDtypeStruct((N_DEV,) + shape, src.dtype),
        in_specs=[pl.BlockSpec(memory_space=pl.ANY)],
        out_specs=pl.BlockSpec(memory_space=pl.ANY),
        scratch_shapes=[pltpu.SemaphoreType.DMA((N_DEV - 1,)), pltpu.SemaphoreType.DMA((N_DEV - 1,)),
                        pltpu.SemaphoreType.DMA],
    )(src)


def _pad_rows(flat, mult):
    n = flat.shape[-1]
    rows = -(-n // (LANES * mult)) * mult
    pad = rows * LANES - n
    flat = jnp.pad(flat, [(0, 0)] * (flat.ndim - 1) + [(0, pad)])
    return flat.reshape(flat.shape[:-1] + (rows, LANES))


def _pick(n, cands):
    for c in cands:
        if n % c == 0:
            return c
    return n


def _matmul(a, b, ta, tb, out_dtype, name):
    (m, k) = (a.shape[1], a.shape[0]) if ta else a.shape
    (k2, n) = (b.shape[1], b.shape[0]) if tb else b.shape
    assert k == k2, (a.shape, b.shape, ta, tb)
    tm = _pick(m, (512, 256, 128))
    tn = _pick(n, (1152, 1024, 896, 512, 384, 256, 128))
    tk = _pick(k, (1024, 896, 512, 256, 128)) if k > 2816 else k
    nk = k // tk
    dims = (((0 if ta else 1,), (1 if tb else 0,)), ((), ()))

    def body(a_ref, b_ref, o_ref, *scratch):
        part = lax.dot_general(a_ref[...].astype(BF16), b_ref[...].astype(BF16), dims,
                               preferred_element_type=F32)
        if nk == 1:
            o_ref[...] = part.astype(o_ref.dtype)
        else:
            acc_ref, = scratch
            kk = pl.program_id(2)

            @pl.when(kk == 0)
            def _():
                acc_ref[...] = part

            @pl.when(kk > 0)
            def _():
                acc_ref[...] += part

            @pl.when(kk == nk - 1)
            def _():
                o_ref[...] = acc_ref[...].astype(o_ref.dtype)

    a_spec = (pl.BlockSpec((tk, tm), lambda j, i, kk: (kk, i)) if ta
              else pl.BlockSpec((tm, tk), lambda j, i, kk: (i, kk)))
    b_spec = (pl.BlockSpec((tn, tk), lambda j, i, kk: (j, kk)) if tb
              else pl.BlockSpec((tk, tn), lambda j, i, kk: (kk, j)))
    return pl.pallas_call(
        body, name=name,
        out_shape=jax.ShapeDtypeStruct((m, n), out_dtype),
        grid=(n // tn, m // tm, nk),
        in_specs=[a_spec, b_spec],
        out_specs=pl.BlockSpec((tm, tn), lambda j, i, kk: (i, j)),
        scratch_shapes=[] if nk == 1 else [pltpu.VMEM((tm, tn), F32)],
        compiler_params=pltpu.CompilerParams(
            dimension_semantics=("parallel", "parallel", "arbitrary"), vmem_limit_bytes=56 * 1024 * 1024),
    )(a, b)


@functools.partial(jax.custom_vjp, nondiff_argnums=(2, 3))
def linear(a, w, out_dtype, name):
    return _matmul(a, w, False, False, out_dtype, name + "_fwd")


def _linear_fwd(a, w, out_dtype, name):
    return _matmul(a, w, False, False, out_dtype, name + "_fwd"), (a, w)


def _linear_bwd(out_dtype, name, res, dy):
    a, w = res
    da = _matmul(dy, w, False, True, a.dtype, name + "_da")
    dw = _matmul(a, dy, True, False, w.dtype, name + "_dw")
    return da, dw


linear.defvjp(_linear_fwd, _linear_bwd)


def rmsnorm(x, w):
    y = x * lax.rsqrt(jnp.mean(x * x, axis=-1, keepdims=True) + NORM_EPS)
    return y * w


def causal_depthwise_conv(x, w, b):
    k = w.shape[0]
    y = lax.conv_general_dilated(
        x[None], w[:, None, :], window_strides=(1,), padding=[(k - 1, 0)],
        dimension_numbers=("NWC", "WIO", "NWC"), feature_group_count=x.shape[-1])[0]
    return y + b


def rope_tables(seq):
    pos = jnp.arange(seq, dtype=F32)
    inv_freq = ROPE_THETA ** (-jnp.arange(0, ROPE_DIM, 2, dtype=F32) / ROPE_DIM)
    ang = pos[:, None] * inv_freq[None, :]
    return jnp.cos(ang), jnp.sin(ang)


def apply_partial_rope(t, cos, sin):
    half = ROPE_DIM // 2
    c = cos[:, None, None, :]
    s = sin[:, None, None, :]
    x1 = t[..., :half]
    x2 = t[..., half:ROPE_DIM]
    return jnp.concatenate([x1 * c - x2 * s, x2 * c + x1 * s, t[..., ROPE_DIM:]], axis=-1)


def dilated_window_attention(q, k, v, dilation, steps):
    s, h, hd = q.shape
    length = s // dilation
    nb = -(-length // ATTN_BLOCK)
    lp = nb * ATTN_BLOCK

    def to_strided(t):
        t = t.reshape(length, dilation, h, hd).transpose(1, 2, 0, 3)
        t = jnp.pad(t, ((0, 0), (0, 0), (0, lp - length), (0, 0)))
        return t.reshape(dilation, h, nb, ATTN_BLOCK, hd)

    def with_prev(t):
        prev = jnp.pad(t, ((0, 0), (0, 0), (1, 0), (0, 0), (0, 0)))[:, :, :-1]
        return jnp.concatenate([prev, t], axis=-2)

    qb = to_strided(q)
    kk = with_prev(to_strided(k))
    vv = with_prev(to_strided(v))
    scores = jnp.einsum("rhnqe,rhnke->rhnqk", qb, kk) * (hd ** -0.5)
    n_idx = jnp.arange(nb)[:, None, None]
    i_idx = jnp.arange(ATTN_BLOCK)[None, :, None]
    j_idx = jnp.arange(2 * ATTN_BLOCK)[None, None, :]
    delta = ATTN_BLOCK + i_idx - j_idx
    key_pos = (n_idx - 1) * ATTN_BLOCK + j_idx
    allowed = (delta >= 0) & (delta <= steps) & (key_pos >= 0)
    scores = jnp.where(allowed, scores, -jnp.inf)
    m = jnp.max(scores, axis=-1, keepdims=True)
    p = jnp.exp(scores - m)
    den = jnp.sum(p, axis=-1, keepdims=True)
    o = jnp.einsum("rhnqk,rhnke->rhnqe", p, vv) / den
    lse = (m + jnp.log(den))[..., 0]
    o = o.reshape(dilation, h, lp, hd)[:, :, :length]
    o = o.transpose(2, 0, 1, 3).reshape(s, h, hd)
    lse = lse.reshape(dilation, h, lp)[:, :, :length]
    lse = lse.transpose(2, 0, 1).reshape(s, h)
    return o, lse


def attention_mixer(h, w_qkv, w_o, cos, sin, tag):
    s = h.shape[0]
    qkv = linear(h.astype(BF16), w_qkv, F32, tag + "_qkv")
    qkv = qkv.reshape(s, N_ATTN_GROUPS, 3, ATTN_HEADS, ATTN_HEAD_DIM)
    q = apply_partial_rope(qkv[:, :, 0], cos, sin)
    k = apply_partial_rope(qkv[:, :, 1], cos, sin)
    v = qkv[:, :, 2]
    outs, lses = [], []
    for g in range(N_ATTN_GROUPS):
        o_g, lse_g = dilated_window_attention(q[:, g], k[:, g], v[:, g], ATTN_DILATIONS[g], ATTN_STEPS[g])
        outs.append(o_g)
        lses.append(lse_g)
    wts = jax.nn.softmax(jnp.stack(lses, axis=1), axis=1)
    o = jnp.einsum("sgh,sghe->she", wts, jnp.stack(outs, axis=1))
    return linear(o.reshape(s, ATTN_OUT_WIDTH).astype(BF16), w_o, F32, tag + "_o")


def ssd_chunked_scan(x, dt, a, bm, cm):
    s, h, p = x.shape
    g, n = bm.shape[1], bm.shape[2]
    r = h // g
    c = s // SSM_CHUNK
    q = SSM_CHUNK
    x = x.reshape(c, q, g, r, p)
    dt = dt.reshape(c, q, g, r)
    bm = bm.reshape(c, q, g, n)
    cm = cm.reshape(c, q, g, n)
    a_dt = dt * a.reshape(g, r)
    a_cs = jnp.cumsum(a_dt, axis=1)
    xdt = x * dt[..., None]
    seg = a_cs[:, :, None] - a_cs[:, None, :]
    causal = jnp.tril(jnp.ones((q, q), dtype=bool))[:, :, None, None]
    lmat = jnp.exp(jnp.where(causal, seg, -jnp.inf))
    cb = jnp.einsum("cign,cjgn->cijg", cm, bm)
    y_diag = jnp.einsum("cijgr,cjgrp->cigrp", cb[..., None] * lmat, xdt)
    decay = jnp.exp(a_cs[:, -1:] - a_cs)
    states = jnp.einsum("cjgn,cjgr,cjgrp->cgrpn", bm, decay, xdt)
    chunk_decay = jnp.exp(a_cs[:, -1])

    def step(state, inp):
        st_c, dec_c = inp
        return state * dec_c[..., None, None] + st_c, state

    init = jnp.zeros((g, r, p, n), dtype=x.dtype)
    _, prev = lax.scan(step, init, (states, chunk_decay))
    y_off = jnp.einsum("cign,cgrpn->cigrp", cm, prev) * jnp.exp(a_cs)[..., None]
    return (y_diag + y_off).reshape(s, h, p)


def gated_group_rmsnorm(y, z, w, groups):
    g = y * jax.nn.silu(z)
    shp = g.shape
    g = g.reshape(shp[:-1] + (groups, shp[-1] // groups))
    g = g * lax.rsqrt(jnp.mean(g * g, axis=-1, keepdims=True) + NORM_EPS)
    return g.reshape(shp) * w


def ssd_mixer(h, w_in, conv_w, conv_b, dt_bias, a_log, d_skip, norm_w, w_out, tag):
    s = h.shape[0]
    gn = SSM_GROUPS * SSM_STATE
    zxbcdt = linear(h.astype(BF16), w_in, F32, tag + "_in")
    z = zxbcdt[:, :SSM_D_INNER]
    xbc = zxbcdt[:, SSM_D_INNER:SSM_D_INNER + SSM_CONV_DIM]
    dt_raw = zxbcdt[:, SSM_D_INNER + SSM_CONV_DIM:SSM_IN_WIDTH]
    xbc = jax.nn.silu(causal_depthwise_conv(xbc, conv_w, conv_b))
    xs = xbc[:, :SSM_D_INNER].reshape(s, SSM_HEADS, SSM_HEAD_DIM)
    bm = xbc[:, SSM_D_INNER:SSM_D_INNER + gn].reshape(s, SSM_GROUPS, SSM_STATE)
    cm = xbc[:, SSM_D_INNER + gn:].reshape(s, SSM_GROUPS, SSM_STATE)
    dt = jax.nn.softplus(dt_raw + dt_bias)
    a = -jnp.exp(a_log)
    y = ssd_chunked_scan(xs, dt, a, bm, cm)
    y = y + d_skip[:, None] * xs
    y = gated_group_rmsnorm(y.reshape(s, SSM_D_INNER), z, norm_w, SSM_GROUPS)
    return linear(y.astype(BF16), w_out, F32, tag + "_out")


def conv_ffn(h, w_up, conv_w, conv_b, w_down, tag):
    u = causal_depthwise_conv(linear(h.astype(BF16), w_up, F32, tag + "_up"), conv_w, conv_b)
    gate, up = u[:, :D_FF], u[:, D_FF:]
    return linear((jax.nn.silu(gate) * up).astype(BF16), w_down, F32, tag + "_down")


VMEM_LIMIT = 56 * 1024 * 1024
SUBLANES = 8


def _params(*sem):
    return pltpu.CompilerParams(dimension_semantics=sem, vmem_limit_bytes=VMEM_LIMIT)


def _sigmoid(x):
    return 1.0 / (1.0 + jnp.exp(-x))


def _rstd(xv):
    return lax.rsqrt(jnp.mean(xv * xv, axis=-1, keepdims=True) + NORM_EPS)


def _accumulate(ref, part, first):
    @pl.when(first)
    def _():
        ref[...] = part

    @pl.when(jnp.logical_not(first))
    def _():
        ref[...] += part


def _norm_fwd(x, w, name):
    s, d = x.shape
    tm = _pick(s, (512, 256, 128))

    def body(x_ref, w_ref, h_ref):
        xv = x_ref[...]
        h_ref[...] = (xv * _rstd(xv) * w_ref[...]).astype(h_ref.dtype)

    return pl.pallas_call(
        body, name=name, out_shape=jax.ShapeDtypeStruct((s, d), BF16), grid=(s // tm,),
        in_specs=[pl.BlockSpec((tm, d), lambda i: (i, 0)), pl.BlockSpec((1, d), lambda i: (0, 0))],
        out_specs=pl.BlockSpec((tm, d), lambda i: (i, 0)), compiler_params=_params("parallel"),
    )(x, w.reshape(1, d))


def _norm_bwd(x, w, dh, name):
    s, d = x.shape
    tm = _pick(s, (512, 256, 128))

    def body(x_ref, w_ref, dh_ref, dx_ref, dw_ref):
        xv = x_ref[...]
        r = _rstd(xv)
        y = xv * r
        dhv = dh_ref[...].astype(F32)
        dy = dhv * w_ref[...]
        dx_ref[...] = r * (dy - y * jnp.mean(dy * y, axis=-1, keepdims=True))
        _accumulate(dw_ref, jnp.sum(dhv * y, axis=0, keepdims=True), pl.program_id(0) == 0)

    dx, dw = pl.pallas_call(
        body, name=name,
        out_shape=[jax.ShapeDtypeStruct((s, d), F32), jax.ShapeDtypeStruct((1, d), F32)], grid=(s // tm,),
        in_specs=[pl.BlockSpec((tm, d), lambda i: (i, 0)), pl.BlockSpec((1, d), lambda i: (0, 0)),
                  pl.BlockSpec((tm, d), lambda i: (i, 0))],
        out_specs=[pl.BlockSpec((tm, d), lambda i: (i, 0)), pl.BlockSpec((1, d), lambda i: (0, 0))],
        compiler_params=_params("arbitrary"),
    )(x, w.reshape(1, d), dh)
    return dx, dw.reshape(d)


@functools.partial(jax.custom_vjp, nondiff_argnums=(2,))
def norm(x, w, name):
    return _norm_fwd(x, w, name + "_fwd")


def _norm_vjp_fwd(x, w, name):
    return _norm_fwd(x, w, name + "_fwd"), (x, w)


def _norm_vjp_bwd(name, res, dh):
    x, w = res
    return _norm_bwd(x, w, dh, name + "_bwd")


norm.defvjp(_norm_vjp_fwd, _norm_vjp_bwd)


def loss_head(x, w, target, name):
    s, d = x.shape
    tm = _pick(s, (512, 256, 128))

    def body(x_ref, w_ref, t_ref, loss_ref, dx_ref, dw_ref):
        first = pl.program_id(0) == 0
        xv = x_ref[...]
        r = _rstd(xv)
        y = xv * r
        err = y * w_ref[...] - t_ref[...]
        part = 0.5 * jnp.sum(jnp.sum(err * err, axis=-1, keepdims=True), axis=0, keepdims=True) / d
        _accumulate(loss_ref, jnp.broadcast_to(part, loss_ref.shape), first)
        dout = err / d
        dy = dout * w_ref[...]
        dx_ref[...] = r * (dy - y * jnp.mean(dy * y, axis=-1, keepdims=True))
        _accumulate(dw_ref, jnp.sum(dout * y, axis=0, keepdims=True), first)

    row = pl.BlockSpec((tm, d), lambda i: (i, 0))
    vec = pl.BlockSpec((1, d), lambda i: (0, 0))
    loss, dx, dw = pl.pallas_call(
        body, name=name,
        out_shape=[jax.ShapeDtypeStruct((1, 128), F32), jax.ShapeDtypeStruct((s, d), F32),
                   jax.ShapeDtypeStruct((1, d), F32)],
        grid=(s // tm,), in_specs=[row, vec, row],
        out_specs=[pl.BlockSpec((1, 128), lambda i: (0, 0)), row, vec],
        compiler_params=_params("arbitrary"),
    )(x, w.reshape(1, d), target)
    return loss[0, 0], dx, dw.reshape(d)


FFN_ROWS = 256
FFN_COLS = 256


def _shift_down(cur, halo, k):
    out = pltpu.roll(cur, k, axis=0)
    row = lax.broadcasted_iota(jnp.int32, cur.shape, 0)
    for j in range(k):
        out = jnp.where(row == j, halo[SUBLANES - k + j:SUBLANES - k + j + 1, :], out)
    return out


def _shift_up(cur, nxt, k):
    n = cur.shape[0]
    out = pltpu.roll(cur, n - k, axis=0)
    row = lax.broadcasted_iota(jnp.int32, cur.shape, 0)
    for j in range(k):
        out = jnp.where(row == n - k + j, nxt[j:j + 1, :], out)
    return out


def _conv_taps(cur, halo, ntaps):
    return [_shift_down(cur, halo, ntaps - 1 - k) if k < ntaps - 1 else cur for k in range(ntaps)]


def _pad_taps(conv_w):
    return jnp.pad(conv_w, ((0, SUBLANES - conv_w.shape[0]), (0, 0)))


def _ffn_mid_fwd(u0, conv_w, conv_b, name):
    s, width = u0.shape
    half = width // 2
    tm = _pick(s, (FFN_ROWS, 128))
    per = tm // SUBLANES

    def body(w_ref, b_ref, u_ref, halo_ref, a_ref):
        keep = pl.program_id(0) > 0
        for c0 in range(0, half, FFN_COLS):
            vals = []
            for base in (c0, half + c0):
                cols = slice(base, base + FFN_COLS)
                halo = jnp.where(keep, halo_ref[:, cols], 0.0)
                taps = _conv_taps(u_ref[:, cols], halo, 3)
                vals.append(sum(w_ref[k:k + 1, cols] * taps[k] for k in range(3)) + b_ref[:, cols])
            gate, up = vals
            a_ref[:, c0:c0 + FFN_COLS] = (gate * _sigmoid(gate) * up).astype(a_ref.dtype)

    return pl.pallas_call(
        body, name=name, out_shape=jax.ShapeDtypeStruct((s, half), BF16), grid=(s // tm,),
        in_specs=[pl.BlockSpec((SUBLANES, width), lambda i: (0, 0)), pl.BlockSpec((1, width), lambda i: (0, 0)),
                  pl.BlockSpec((tm, width), lambda i: (i, 0)),
                  pl.BlockSpec((SUBLANES, width), lambda i: (jnp.maximum(i * per - 1, 0), 0))],
        out_specs=pl.BlockSpec((tm, half), lambda i: (i, 0)), compiler_params=_params("parallel"),
    )(_pad_taps(conv_w), conv_b.reshape(1, width), u0, u0)


def _ffn_mid_bwd(u0, conv_w, conv_b, da, name):
    s, width = u0.shape
    half = width // 2
    tm = _pick(s, (FFN_ROWS, 128))
    per = tm // SUBLANES
    nt = s // tm

    def body(w_ref, b_ref, u_ref, halo_ref, da_ref, du0_ref, dw_ref, db_ref, carry_ref):
        step = pl.program_id(0)
        first = step == 0
        keep = step < nt - 1
        for c0 in range(0, half, FFN_COLS):
            dav = da_ref[:, c0:c0 + FFN_COLS].astype(F32)
            taps, vals = [], []
            for base in (c0, half + c0):
                cols = slice(base, base + FFN_COLS)
                halo = jnp.where(keep, halo_ref[:, cols], 0.0)
                tp = _conv_taps(u_ref[:, cols], halo, 3)
                taps.append(tp)
                vals.append(sum(w_ref[k:k + 1, cols] * tp[k] for k in range(3)) + b_ref[:, cols])
            gate, up = vals
            sig = _sigmoid(gate)
            dus = [dav * up * sig * (1.0 + gate * (1.0 - sig)), dav * gate * sig]
            for base, tp, du in zip((c0, half + c0), taps, dus):
                cols = slice(base, base + FFN_COLS)
                nxt = jnp.where(first, 0.0, carry_ref[:, cols])
                du0 = (w_ref[2:3, cols] * du + w_ref[1:2, cols] * _shift_up(du, nxt, 1)
                       + w_ref[0:1, cols] * _shift_up(du, nxt, 2))
                du0_ref[:, cols] = du0.astype(du0_ref.dtype)
                carry_ref[:, cols] = du[0:SUBLANES, :]
                dwp = jnp.concatenate([jnp.sum(du * tp[k], axis=0, keepdims=True) for k in range(3)]
                                      + [jnp.zeros((SUBLANES - 3, FFN_COLS), F32)], axis=0)
                dbp = jnp.sum(du, axis=0, keepdims=True)

                @pl.when(first)
                def _():
                    dw_ref[:, cols] = dwp
                    db_ref[:, cols] = dbp

                @pl.when(jnp.logical_not(first))
                def _():
                    dw_ref[:, cols] += dwp
                    db_ref[:, cols] += dbp

    rev = lambda i: nt - 1 - i
    du0, dw, db = pl.pallas_call(
        body, name=name,
        out_shape=[jax.ShapeDtypeStruct((s, width), BF16), jax.ShapeDtypeStruct((SUBLANES, width), F32),
                   jax.ShapeDtypeStruct((1, width), F32)],
        grid=(nt,),
        in_specs=[pl.BlockSpec((SUBLANES, width), lambda i: (0, 0)), pl.BlockSpec((1, width), lambda i: (0, 0)),
                  pl.BlockSpec((tm, width), lambda i: (rev(i), 0)),
                  pl.BlockSpec((SUBLANES, width), lambda i: (jnp.maximum(rev(i) * per - 1, 0), 0)),
                  pl.BlockSpec((tm, half), lambda i: (rev(i), 0))],
        out_specs=[pl.BlockSpec((tm, width), lambda i: (rev(i), 0)),
                   pl.BlockSpec((SUBLANES, width), lambda i: (0, 0)), pl.BlockSpec((1, width), lambda i: (0, 0))],
        scratch_shapes=[pltpu.VMEM((SUBLANES, width), F32)],
        compiler_params=_params("arbitrary"),
    )(_pad_taps(conv_w), conv_b.reshape(1, width), u0, u0, da)
    return du0, dw[:3], db.reshape(width)


@functools.partial(jax.custom_vjp, nondiff_argnums=(3,))
def ffn_mid(u0, conv_w, conv_b, name):
    return _ffn_mid_fwd(u0, conv_w, conv_b, name + "_fwd")


def _ffn_mid_vjp_fwd(u0, conv_w, conv_b, name):
    return _ffn_mid_fwd(u0, conv_w, conv_b, name + "_fwd"), (u0, conv_w, conv_b)


def _ffn_mid_vjp_bwd(name, res, da):
    u0, conv_w, conv_b = res
    return _ffn_mid_bwd(u0, conv_w, conv_b, da, name + "_bwd")


ffn_mid.defvjp(_ffn_mid_vjp_fwd, _ffn_mid_vjp_bwd)


def conv_ffn_p(h, w_up, conv_w, conv_b, w_down, tag):
    u0 = linear(h, w_up, F32, tag + "_up")
    return linear(ffn_mid(u0, conv_w, conv_b, tag + "_mid"), w_down, F32, tag + "_down")


NEG = -1e30
HEAD_SLICES = [slice(hh * ATTN_HEAD_DIM, (hh + 1) * ATTN_HEAD_DIM) for hh in range(ATTN_HEADS)]
ATTN_SCALE = ATTN_HEAD_DIM ** -0.5
QKV_GROUP = 3 * ATTN_OUT_WIDTH


def rope_table(seq):
    cos, sin = rope_tables(seq)
    half = ROPE_DIM // 2
    ones = jnp.ones((seq, ATTN_HEAD_DIM - ROPE_DIM), F32)
    zero = lambda n: jnp.zeros((seq, n), F32)
    return jnp.concatenate([cos, cos, ones, -sin, zero(ATTN_HEAD_DIM - half),
                            zero(half), sin, zero(ATTN_HEAD_DIM - ROPE_DIM)], axis=1)


def _rope(t, tab, sign):
    half = ROPE_DIM // 2
    return t * tab[:, 0:128] + sign * (pltpu.roll(t, ATTN_HEAD_DIM - half, axis=1) * tab[:, 128:256]
                                       + pltpu.roll(t, half, axis=1) * tab[:, 256:384])


def _to_dilated(a, d):
    s = a.shape[0]
    return a if d == 1 else a.reshape(s // d, d, -1).transpose(1, 0, 2).reshape(s, -1)


def _from_dilated(a, d):
    s = a.shape[0]
    return a if d == 1 else a.reshape(d, s // d, -1).transpose(1, 0, 2).reshape(s, -1)


def _rope_qk(qkv, tab, name):
    s = qkv.shape[0]
    tm = _pick(s, (512, 256, 128))

    def body(t_ref, x_ref, o_ref):
        for hs in HEAD_SLICES:
            o_ref[:, hs] = _rope(x_ref[:, hs].astype(F32), t_ref[...], 1.0).astype(o_ref.dtype)

    blk = pl.BlockSpec((tm, ATTN_OUT_WIDTH), lambda i, c: (i, c))
    return pl.pallas_call(
        body, name=name, out_shape=jax.ShapeDtypeStruct(qkv.shape, qkv.dtype), grid=(s // tm, 2),
        in_specs=[pl.BlockSpec((tm, 384), lambda i, c: (i, 0)), blk], out_specs=blk,
        input_output_aliases={1: 0}, compiler_params=_params("parallel", "parallel"),
    )(tab, qkv)


def _dot_nt(a, b):
    return lax.dot_general(a, b, (((1,), (1,)), ((), ())), preferred_element_type=F32)


def _dot_tn(a, b):
    return lax.dot_general(a, b, (((0,), (0,)), ((), ())), preferred_element_type=F32)


def _dot(a, b):
    return jnp.dot(a, b, preferred_element_type=F32)


def _window_masks(has_prev):
    ii = lax.broadcasted_iota(jnp.int32, (ATTN_BLOCK, ATTN_BLOCK), 0)
    jj = lax.broadcasted_iota(jnp.int32, (ATTN_BLOCK, ATTN_BLOCK), 1)
    return jj <= ii, jnp.logical_and(jj >= ii, has_prev)


def _attn_group_fwd(qkv, d, name):
    s = qkv.shape[0]
    nb = s // d // ATTN_BLOCK

    def body(q_ref, kc_ref, kp_ref, vc_ref, vp_ref, o_ref, lse_ref):
        mask_c, mask_p = _window_masks(pl.program_id(1) > 0)
        lane = lax.broadcasted_iota(jnp.int32, (ATTN_BLOCK, 128), 1)
        lse_tile = jnp.zeros((ATTN_BLOCK, 128), F32)
        for hh, hs in enumerate(HEAD_SLICES):
            qh = q_ref[:, hs]
            sc = jnp.where(mask_c, _dot_nt(qh, kc_ref[:, hs]) * ATTN_SCALE, NEG)
            sp = jnp.where(mask_p, _dot_nt(qh, kp_ref[:, hs]) * ATTN_SCALE, NEG)
            m = jnp.maximum(jnp.max(sc, axis=1, keepdims=True), jnp.max(sp, axis=1, keepdims=True))
            pc = jnp.exp(sc - m)
            pp = jnp.exp(sp - m)
            den = jnp.sum(pc, axis=1, keepdims=True) + jnp.sum(pp, axis=1, keepdims=True)
            acc = _dot(pc.astype(BF16), vc_ref[:, hs]) + _dot(pp.astype(BF16), vp_ref[:, hs])
            o_ref[:, hs] = acc / den
            lse_tile = jnp.where(lane == hh, m + jnp.log(den), lse_tile)
        lse_ref[...] = lse_tile

    cur = lambda t: pl.BlockSpec((ATTN_BLOCK, ATTN_OUT_WIDTH), lambda r, n: (r * nb + n, t))
    prv = lambda t: pl.BlockSpec((ATTN_BLOCK, ATTN_OUT_WIDTH), lambda r, n: (r * nb + jnp.maximum(n - 1, 0), t))
    return pl.pallas_call(
        body, name=name,
        out_shape=[jax.ShapeDtypeStruct((s, ATTN_OUT_WIDTH), F32), jax.ShapeDtypeStruct((s, 128), F32)],
        grid=(d, nb), in_specs=[cur(0), cur(1), prv(1), cur(2), prv(2)],
        out_specs=[pl.BlockSpec((ATTN_BLOCK, ATTN_OUT_WIDTH), lambda r, n: (r * nb + n, 0)),
                   pl.BlockSpec((ATTN_BLOCK, 128), lambda r, n: (r * nb + n, 0))],
        compiler_params=_params("parallel", "parallel"),
    )(qkv, qkv, qkv, qkv, qkv)


def _attn_combine(os_, lses, name):
    s = os_[0].shape[0]
    tm = _pick(s, (256, 128))
    ng = len(os_)

    def body(*refs):
        o_refs, l_refs, (o_ref, lse_ref) = refs[:ng], refs[ng:2 * ng], refs[2 * ng:]
        lane = lax.broadcasted_iota(jnp.int32, (tm, 128), 1)
        lse_tile = jnp.zeros((tm, 128), F32)
        for hh, hs in enumerate(HEAD_SLICES):
            ls = [l_ref[:, hh:hh + 1] for l_ref in l_refs]
            m = functools.reduce(jnp.maximum, ls)
            ws = [jnp.exp(l - m) for l in ls]
            tot = functools.reduce(lambda a, b: a + b, ws)
            acc = functools.reduce(lambda a, b: a + b, [o_r[:, hs] * w for o_r, w in zip(o_refs, ws)])
            o_ref[:, hs] = (acc / tot).astype(o_ref.dtype)
            lse_tile = jnp.where(lane == hh, m + jnp.log(tot), lse_tile)
        lse_ref[...] = lse_tile

    wide = pl.BlockSpec((tm, ATTN_OUT_WIDTH), lambda i: (i, 0))
    thin = pl.BlockSpec((tm, 128), lambda i: (i, 0))
    return pl.pallas_call(
        body, name=name,
        out_shape=[jax.ShapeDtypeStruct((s, ATTN_OUT_WIDTH), BF16), jax.ShapeDtypeStruct((s, 128), F32)],
        grid=(s // tm,), in_specs=[wide] * ng + [thin] * ng, out_specs=[wide, thin],
        compiler_params=_params("parallel"),
    )(*os_, *lses)


def _attn_delta(do, o, name):
    s = do.shape[0]
    tm = _pick(s, (256, 128))

    def body(do_ref, o_ref, out_ref):
        lane = lax.broadcasted_iota(jnp.int32, (tm, 128), 1)
        tile = jnp.zeros((tm, 128), F32)
        for hh, hs in enumerate(HEAD_SLICES):
            prod = do_ref[:, hs].astype(F32) * o_ref[:, hs].astype(F32)
            tile = jnp.where(lane == hh, jnp.sum(prod, axis=1, keepdims=True), tile)
        out_ref[...] = tile

    wide = pl.BlockSpec((tm, ATTN_OUT_WIDTH), lambda i: (i, 0))
    return pl.pallas_call(
        body, name=name, out_shape=jax.ShapeDtypeStruct((s, 128), F32), grid=(s // tm,),
        in_specs=[wide, wide], out_specs=pl.BlockSpec((tm, 128), lambda i: (i, 0)),
        compiler_params=_params("parallel"),
    )(do, o)


def _attn_group_bwd(qkv, do, lse, delta, tab, d, name):
    s = qkv.shape[0]
    nb = s // d // ATTN_BLOCK

    def body(q_ref, kc_ref, kp_ref, vc_ref, vp_ref, do_ref, lse_ref, dl_ref, tq_ref, tk_ref,
             dq_ref, dk_ref, dv_ref, ck_ref, cv_ref):
        n = pl.program_id(1)

        @pl.when(n < nb)
        def _():
            mask_c, mask_p = _window_masks(n > 0)
            for hh, hs in enumerate(HEAD_SLICES):
                qh, doh = q_ref[:, hs], do_ref[:, hs]
                big_l, dl = lse_ref[:, hh:hh + 1], dl_ref[:, hh:hh + 1]
                grads = []
                for mask, k_ref, v_ref in ((mask_c, kc_ref, vc_ref), (mask_p, kp_ref, vp_ref)):
                    kh, vh = k_ref[:, hs], v_ref[:, hs]
                    sc = jnp.where(mask, _dot_nt(qh, kh) * ATTN_SCALE, NEG)
                    p = jnp.exp(sc - big_l)
                    ds = (p * (_dot_nt(doh, vh) - dl) * ATTN_SCALE).astype(BF16)
                    grads.append((_dot(ds, kh), _dot_tn(ds, qh), _dot_tn(p.astype(BF16), doh)))
                (dq_c, dk_c, dv_c), (dq_p, dk_p, dv_p) = grads
                dq_ref[:, hs] = _rope(dq_c + dq_p, tq_ref[...], -1.0).astype(dq_ref.dtype)

                @pl.when(n > 0)
                def _():
                    dk_ref[:, hs] = _rope(ck_ref[:, hs] + dk_p, tk_ref[...], -1.0).astype(dk_ref.dtype)
                    dv_ref[:, hs] = (cv_ref[:, hs] + dv_p).astype(dv_ref.dtype)

                ck_ref[:, hs] = dk_c
                cv_ref[:, hs] = dv_c

        @pl.when(n == nb)
        def _():
            for hs in HEAD_SLICES:
                dk_ref[:, hs] = _rope(ck_ref[:, hs], tk_ref[...], -1.0).astype(dk_ref.dtype)
                dv_ref[:, hs] = cv_ref[:, hs].astype(dv_ref.dtype)

    def spec(width, row, col):
        return pl.BlockSpec((ATTN_BLOCK, width), lambda r, n: (r * nb + row(n), col))

    cur = lambda n: jnp.minimum(n, nb - 1)
    prv = lambda n: jnp.maximum(jnp.minimum(n, nb - 1) - 1, 0)
    out = lambda n: jnp.maximum(n - 1, 0)
    wide = ATTN_OUT_WIDTH
    return pl.pallas_call(
        body, name=name,
        out_shape=[jax.ShapeDtypeStruct((s, wide), BF16)] * 3,
        grid=(d, nb + 1),
        in_specs=[spec(wide, cur, 0), spec(wide, cur, 1), spec(wide, prv, 1), spec(wide, cur, 2),
                  spec(wide, prv, 2), spec(wide, cur, 0), spec(128, cur, 0), spec(128, cur, 0),
                  spec(384, cur, 0), spec(384, out, 0)],
        out_specs=[spec(wide, cur, 0), spec(wide, out, 0), spec(wide, out, 0)],
        scratch_shapes=[pltpu.VMEM((ATTN_BLOCK, wide), F32), pltpu.VMEM((ATTN_BLOCK, wide), F32)],
        compiler_params=_params("parallel", "arbitrary"),
    )(qkv, qkv, qkv, qkv, qkv, do, lse, delta, tab, tab)


def _attn_core_fwd(qkvs, name):
    tab = rope_table(qkvs[0].shape[0])
    rot, os_, lses = [], [], []
    for g, (qkv, d) in enumerate(zip(qkvs, ATTN_DILATIONS)):
        qkv = _rope_qk(qkv, _to_dilated(tab, d), f"{name}_rope{g}")
        o_g, lse_g = _attn_group_fwd(qkv, d, f"{name}_fwd{g}")
        rot.append(qkv)
        os_.append(_from_dilated(o_g, d))
        lses.append(_from_dilated(lse_g, d))
    o, lse = _attn_combine(os_, lses, name + "_combine")
    return o, (tuple(rot), o, lse)


@functools.partial(jax.custom_vjp, nondiff_argnums=(1,))
def attn_core(qkvs, name):
    return _attn_core_fwd(qkvs, name)[0]


def _attn_core_vjp_fwd(qkvs, name):
    return _attn_core_fwd(qkvs, name)


def _attn_core_vjp_bwd(name, res, do):
    rot, o, lse = res
    tab = rope_table(o.shape[0])
    delta = _attn_delta(do, o, name + "_delta")
    out = []
    for g, (qkv, d) in enumerate(zip(rot, ATTN_DILATIONS)):
        parts = _attn_group_bwd(qkv, _to_dilated(do, d), _to_dilated(lse, d), _to_dilated(delta, d),
                                _to_dilated(tab, d), d, f"{name}_bwd{g}")
        out.append(jnp.concatenate(parts, axis=1))
    return (tuple(out),)


attn_core.defvjp(_attn_core_vjp_fwd, _attn_core_vjp_bwd)


def attention_mixer_p(h, w_qkv, w_o, tag):
    qkvs = tuple(linear(_to_dilated(h, d), w_qkv[:, g * QKV_GROUP:(g + 1) * QKV_GROUP], BF16, f"{tag}_qkv{g}")
                 for g, d in enumerate(ATTN_DILATIONS))
    return linear(attn_core(qkvs, tag), w_o, F32, tag + "_o")


SSM_CONV_TAPS = 4
SSM_COL_BLOCK = 2048
SSM_PAIRS = SSM_HEADS // 2
SSM_DT_BLOCK = (SSM_D_INNER + SSM_CONV_DIM) // 128


def _ssm_conv_fwd(zx, conv_w, conv_b, name):
    s = zx.shape[0]
    tm = _pick(s, (256, 128))
    per = tm // SUBLANES
    ncb = SSM_CONV_DIM // SSM_COL_BLOCK

    def body(w_ref, b_ref, x_ref, halo_ref, o_ref):
        keep = pl.program_id(1) > 0
        for c0 in range(0, SSM_COL_BLOCK, FFN_COLS):
            cols = slice(c0, c0 + FFN_COLS)
            halo = jnp.where(keep, halo_ref[:, cols], 0.0)
            taps = _conv_taps(x_ref[:, cols], halo, SSM_CONV_TAPS)
            pre = sum(w_ref[k:k + 1, cols] * taps[k] for k in range(SSM_CONV_TAPS)) + b_ref[:, cols]
            o_ref[:, cols] = pre * _sigmoid(pre)

    return pl.pallas_call(
        body, name=name, out_shape=jax.ShapeDtypeStruct((s, SSM_CONV_DIM), F32), grid=(ncb, s // tm),
        in_specs=[pl.BlockSpec((SUBLANES, SSM_COL_BLOCK), lambda j, i: (0, j)),
                  pl.BlockSpec((1, SSM_COL_BLOCK), lambda j, i: (0, j)),
                  pl.BlockSpec((tm, SSM_COL_BLOCK), lambda j, i: (i, j + 1)),
                  pl.BlockSpec((SUBLANES, SSM_COL_BLOCK), lambda j, i: (jnp.maximum(i * per - 1, 0), j + 1))],
        out_specs=pl.BlockSpec((tm, SSM_COL_BLOCK), lambda j, i: (i, j)),
        compiler_params=_params("parallel", "parallel"),
    )(_pad_taps(conv_w), conv_b.reshape(1, SSM_CONV_DIM), zx, zx)


def _ssm_conv_bwd(zx, conv_w, conv_b, dact, name):
    s = zx.shape[0]
    tm = _pick(s, (256, 128))
    per = tm // SUBLANES
    nt = s // tm
    ncb = SSM_CONV_DIM // SSM_COL_BLOCK
    nk = SSM_CONV_TAPS

    def body(w_ref, b_ref, x_ref, halo_ref, da_ref, dx_ref, dw_ref, db_ref, carry_ref):
        step = pl.program_id(1)
        first = step == 0
        keep = step < nt - 1
        for c0 in range(0, SSM_COL_BLOCK, FFN_COLS):
            cols = slice(c0, c0 + FFN_COLS)
            halo = jnp.where(keep, halo_ref[:, cols], 0.0)
            taps = _conv_taps(x_ref[:, cols], halo, nk)
            pre = sum(w_ref[k:k + 1, cols] * taps[k] for k in range(nk)) + b_ref[:, cols]
            sig = _sigmoid(pre)
            dpre = da_ref[:, cols] * sig * (1.0 + pre * (1.0 - sig))
            nxt = jnp.where(first, 0.0, carry_ref[:, cols])
            dx = w_ref[nk - 1:nk, cols] * dpre
            for k in range(nk - 1):
                dx = dx + w_ref[k:k + 1, cols] * _shift_up(dpre, nxt, nk - 1 - k)
            dx_ref[:, cols] = dx.astype(dx_ref.dtype)
            carry_ref[:, cols] = dpre[0:SUBLANES, :]
            dwp = jnp.concatenate([jnp.sum(dpre * taps[k], axis=0, keepdims=True) for k in range(nk)]
                                  + [jnp.zeros((SUBLANES - nk, FFN_COLS), F32)], axis=0)
            dbp = jnp.sum(dpre, axis=0, keepdims=True)

            @pl.when(first)
            def _():
                dw_ref[:, cols] = dwp
                db_ref[:, cols] = dbp

            @pl.when(jnp.logical_not(first))
            def _():
                dw_ref[:, cols] += dwp
                db_ref[:, cols] += dbp

    rev = lambda i: nt - 1 - i
    dx, dw, db = pl.pallas_call(
        body, name=name,
        out_shape=[jax.ShapeDtypeStruct((s, SSM_CONV_DIM), BF16), jax.ShapeDtypeStruct((SUBLANES, SSM_CONV_DIM), F32),
                   jax.ShapeDtypeStruct((1, SSM_CONV_DIM), F32)],
        grid=(ncb, nt),
        in_specs=[pl.BlockSpec((SUBLANES, SSM_COL_BLOCK), lambda j, i: (0, j)),
                  pl.BlockSpec((1, SSM_COL_BLOCK), lambda j, i: (0, j)),
                  pl.BlockSpec((tm, SSM_COL_BLOCK), lambda j, i: (rev(i), j + 1)),
                  pl.BlockSpec((SUBLANES, SSM_COL_BLOCK), lambda j, i: (jnp.maximum(rev(i) * per - 1, 0), j + 1)),
                  pl.BlockSpec((tm, SSM_COL_BLOCK), lambda j, i: (rev(i), j))],
        out_specs=[pl.BlockSpec((tm, SSM_COL_BLOCK), lambda j, i: (rev(i), j)),
                   pl.BlockSpec((SUBLANES, SSM_COL_BLOCK), lambda j, i: (0, j)),
                   pl.BlockSpec((1, SSM_COL_BLOCK), lambda j, i: (0, j))],
        scratch_shapes=[pltpu.VMEM((SUBLANES, SSM_COL_BLOCK), F32)],
        compiler_params=_params("parallel", "arbitrary"),
    )(_pad_taps(conv_w), conv_b.reshape(1, SSM_CONV_DIM), zx, zx, dact)
    return dx, dw[:nk], db.reshape(SSM_CONV_DIM)


def _ssd_chunk(xs, bms, cms, dt_raw, dtb, alog, dsk, states):
    q = SSM_CHUNK
    lane = lax.broadcasted_iota(jnp.int32, (1, 128), 1)
    row = lax.broadcasted_iota(jnp.int32, (q, 1), 0)
    ii = lax.broadcasted_iota(jnp.int32, (q, q), 0)
    jj = lax.broadcasted_iota(jnp.int32, (q, q), 1)
    tril = ii >= jj
    left = lane < SSM_HEAD_DIM
    last_row = (row == q - 1).astype(F32)

    def lanes_of(mat, h):
        pick = (lane == h).astype(F32)
        return jnp.broadcast_to(jnp.sum(mat * pick, axis=1, keepdims=True), mat.shape)

    def rows_of(mat_t, h):
        pick = (row == h).astype(F32)
        return jnp.broadcast_to(jnp.sum(mat_t * pick, axis=0, keepdims=True), mat_t.shape)

    v = dt_raw + dtb
    dt = jnp.maximum(v, 0.0) + jnp.log(1.0 + jnp.exp(-jnp.abs(v)))
    adt = dt * (-jnp.exp(alog))
    acs = jnp.dot(tril.astype(F32), adt, precision=lax.Precision.HIGHEST, preferred_element_type=F32)
    acs_t = acs.T
    ys, new_states = [], []
    for pr in range(SSM_PAIRS):
        g = pr // 2
        if pr % 2 == 0:
            cb = _dot_nt(cms[g].astype(BF16), bms[g].astype(BF16))
        cols = [lanes_of(acs, 2 * pr + e) for e in range(2)]
        dts = [lanes_of(dt, 2 * pr + e) for e in range(2)]
        rws = [rows_of(acs_t, 2 * pr + e) for e in range(2)]
        lasts = [jnp.sum(c * last_row, axis=0, keepdims=True) for c in cols]
        xdt = xs[pr] * jnp.where(left, dts[0], dts[1])
        halves = [jnp.where(left, xdt, 0.0).astype(BF16), jnp.where(left, 0.0, xdt).astype(BF16)]
        y_diag, s_new = 0.0, 0.0
        for e in range(2):
            lmat = jnp.where(tril, jnp.exp(jnp.minimum(cols[e] - rws[e], 0.0)), 0.0)
            y_diag = y_diag + _dot((cb * lmat).astype(BF16), halves[e])
            decay = jnp.exp(lasts[e] - cols[e])
            s_new = s_new + _dot_tn((bms[g] * decay).astype(BF16), halves[e])
        y_off = _dot(cms[g].astype(BF16), states[pr].astype(BF16)) * jnp.where(left, jnp.exp(cols[0]), jnp.exp(cols[1]))
        skip = jnp.where(left, lanes_of(dsk, 2 * pr), lanes_of(dsk, 2 * pr + 1))
        ys.append(y_diag + y_off + xs[pr] * skip)
        new_states.append(states[pr] * jnp.where(left, jnp.exp(lasts[0]), jnp.exp(lasts[1])) + s_new)
    return tuple(ys), tuple(new_states)


def _ssm_vec(v):
    return jnp.pad(v.reshape(1, -1), ((0, 0), (0, 128 - v.shape[0])))


def _ssd_scan_fwd(act, zx, dtb, alog, dsk, name):
    s = act.shape[0]
    nc = s // SSM_CHUNK
    ng = SSM_GROUPS

    def body(act_ref, dt_ref, dtb_ref, alog_ref, dsk_ref, y_ref, st_out_ref, st_ref):
        @pl.when(pl.program_id(0) == 0)
        def _():
            st_ref[...] = jnp.zeros_like(st_ref)

        tile = lambda k: act_ref[:, k * 128:(k + 1) * 128]
        xs = [tile(k) for k in range(SSM_PAIRS)]
        bms = [tile(SSM_PAIRS + k) for k in range(ng)]
        cms = [tile(SSM_PAIRS + ng + k) for k in range(ng)]
        states = [st_ref[k] for k in range(SSM_PAIRS)]
        st_out_ref[0] = st_ref[...]
        ys, new_states = _ssd_chunk(xs, bms, cms, dt_ref[...], dtb_ref[...], alog_ref[...], dsk_ref[...], states)
        for k in range(SSM_PAIRS):
            y_ref[:, k * 128:(k + 1) * 128] = ys[k]
            st_ref[k] = new_states[k]

    vec = pl.BlockSpec((1, 128), lambda c: (0, 0))
    return pl.pallas_call(
        body, name=name,
        out_shape=[jax.ShapeDtypeStruct((s, SSM_D_INNER), F32),
                   jax.ShapeDtypeStruct((nc, SSM_PAIRS, SSM_STATE, 128), F32)],
        grid=(nc,),
        in_specs=[pl.BlockSpec((SSM_CHUNK, SSM_CONV_DIM), lambda c: (c, 0)),
                  pl.BlockSpec((SSM_CHUNK, 128), lambda c: (c, SSM_DT_BLOCK)), vec, vec, vec],
        out_specs=[pl.BlockSpec((SSM_CHUNK, SSM_D_INNER), lambda c: (c, 0)),
                   pl.BlockSpec((1, SSM_PAIRS, SSM_STATE, 128), lambda c: (c, 0, 0, 0))],
        scratch_shapes=[pltpu.VMEM((SSM_PAIRS, SSM_STATE, 128), F32)],
        compiler_params=_params("arbitrary"),
    )(act, zx, _ssm_vec(dtb), _ssm_vec(alog), _ssm_vec(dsk))


def _ssd_scan_bwd(act, zx, dtb, alog, dsk, st_in, dy, name):
    s = act.shape[0]
    nc = s // SSM_CHUNK
    ng = SSM_GROUPS

    def body(act_ref, dt_ref, dtb_ref, alog_ref, dsk_ref, st_ref, dy_ref, dact_ref, ddt_ref, dpar_ref, dst_ref):
        first = pl.program_id(0) == 0

        @pl.when(first)
        def _():
            dst_ref[...] = jnp.zeros_like(dst_ref)

        tile = lambda k: act_ref[:, k * 128:(k + 1) * 128]
        xs = [tile(k) for k in range(SSM_PAIRS)]
        bms = [tile(SSM_PAIRS + k) for k in range(ng)]
        cms = [tile(SSM_PAIRS + ng + k) for k in range(ng)]
        states = [st_ref[0, k] for k in range(SSM_PAIRS)]
        _, pullback = jax.vjp(_ssd_chunk, xs, bms, cms, dt_ref[...], dtb_ref[...], alog_ref[...], dsk_ref[...],
                              states)
        dys = tuple(dy_ref[:, k * 128:(k + 1) * 128] for k in range(SSM_PAIRS))
        dsts = tuple(dst_ref[k] for k in range(SSM_PAIRS))
        dxs, dbms, dcms, ddt, ddtb, dalog, ddsk, dstates = pullback((dys, dsts))
        for k, t in enumerate(list(dxs) + list(dbms) + list(dcms)):
            dact_ref[:, k * 128:(k + 1) * 128] = t
        ddt_ref[...] = ddt.astype(ddt_ref.dtype)
        for k in range(SSM_PAIRS):
            dst_ref[k] = dstates[k]
        dpar = jnp.concatenate([ddtb, dalog, ddsk, jnp.zeros((SUBLANES - 3, 128), F32)], axis=0)
        _accumulate(dpar_ref, dpar, first)

    rev = lambda c: nc - 1 - c
    vec = pl.BlockSpec((1, 128), lambda c: (0, 0))
    dact, ddt, dpar = pl.pallas_call(
        body, name=name,
        out_shape=[jax.ShapeDtypeStruct((s, SSM_CONV_DIM), F32), jax.ShapeDtypeStruct((s, 128), BF16),
                   jax.ShapeDtypeStruct((SUBLANES, 128), F32)],
        grid=(nc,),
        in_specs=[pl.BlockSpec((SSM_CHUNK, SSM_CONV_DIM), lambda c: (rev(c), 0)),
                  pl.BlockSpec((SSM_CHUNK, 128), lambda c: (rev(c), SSM_DT_BLOCK)), vec, vec, vec,
                  pl.BlockSpec((1, SSM_PAIRS, SSM_STATE, 128), lambda c: (rev(c), 0, 0, 0)),
                  pl.BlockSpec((SSM_CHUNK, SSM_D_INNER), lambda c: (rev(c), 0))],
        out_specs=[pl.BlockSpec((SSM_CHUNK, SSM_CONV_DIM), lambda c: (rev(c), 0)),
                   pl.BlockSpec((SSM_CHUNK, 128), lambda c: (rev(c), 0)),
                   pl.BlockSpec((SUBLANES, 128), lambda c: (0, 0))],
        scratch_shapes=[pltpu.VMEM((SSM_PAIRS, SSM_STATE, 128), F32)],
        compiler_params=_params("arbitrary"),
    )(act, zx, _ssm_vec(dtb), _ssm_vec(alog), _ssm_vec(dsk), st_in, dy)
    return dact, ddt, dpar[0, :SSM_HEADS], dpar[1, :SSM_HEADS], dpar[2, :SSM_HEADS]


SSM_NORM_GROUP = SSM_D_INNER // SSM_GROUPS


def _gated_group(y, z, w):
    g = y * (z * _sigmoid(z))
    return g * lax.rsqrt(jnp.mean(g * g, axis=-1, keepdims=True) + NORM_EPS) * w


def _gated_norm_fwd(y, zx, w, name):
    s = y.shape[0]
    tm = _pick(s, (256, 128))

    def body(y_ref, z_ref, w_ref, o_ref):
        for c0 in range(0, SSM_D_INNER, SSM_NORM_GROUP):
            cols = slice(c0, c0 + SSM_NORM_GROUP)
            o_ref[:, cols] = _gated_group(y_ref[:, cols], z_ref[:, cols], w_ref[:, cols]).astype(o_ref.dtype)

    blk = pl.BlockSpec((tm, SSM_D_INNER), lambda i: (i, 0))
    return pl.pallas_call(
        body, name=name, out_shape=jax.ShapeDtypeStruct((s, SSM_D_INNER), BF16), grid=(s // tm,),
        in_specs=[blk, blk, pl.BlockSpec((1, SSM_D_INNER), lambda i: (0, 0))], out_specs=blk,
        compiler_params=_params("parallel"),
    )(y, zx, w.reshape(1, SSM_D_INNER))


def _gated_norm_bwd(y, zx, w, dout, name):
    s = y.shape[0]
    tm = _pick(s, (256, 128))

    def body(y_ref, z_ref, w_ref, do_ref, dy_ref, dz_ref, dw_ref):
        first = pl.program_id(0) == 0
        for c0 in range(0, SSM_D_INNER, SSM_NORM_GROUP):
            cols = slice(c0, c0 + SSM_NORM_GROUP)
            _, pullback = jax.vjp(_gated_group, y_ref[:, cols], z_ref[:, cols], w_ref[:, cols])
            dyv, dzv, dwv = pullback(do_ref[:, cols].astype(F32))
            dy_ref[:, cols] = dyv
            dz_ref[:, cols] = dzv.astype(dz_ref.dtype)

            @pl.when(first)
            def _():
                dw_ref[:, cols] = dwv

            @pl.when(jnp.logical_not(first))
            def _():
                dw_ref[:, cols] += dwv

    blk = pl.BlockSpec((tm, SSM_D_INNER), lambda i: (i, 0))
    vec = pl.BlockSpec((1, SSM_D_INNER), lambda i: (0, 0))
    dy, dz, dw = pl.pallas_call(
        body, name=name,
        out_shape=[jax.ShapeDtypeStruct((s, SSM_D_INNER), F32), jax.ShapeDtypeStruct((s, SSM_D_INNER), BF16),
                   jax.ShapeDtypeStruct((1, SSM_D_INNER), F32)],
        grid=(s // tm,), in_specs=[blk, blk, vec, blk], out_specs=[blk, blk, vec],
        compiler_params=_params("arbitrary"),
    )(y, zx, w.reshape(1, SSM_D_INNER), dout)
    return dy, dz, dw.reshape(SSM_D_INNER)


def _ssm_core_fwd(zx, conv_w, conv_b, dtb, alog, dsk, norm_w, name):
    act = _ssm_conv_fwd(zx, conv_w, conv_b, name + "_conv_fwd")
    y, st_in = _ssd_scan_fwd(act, zx, dtb, alog, dsk, name + "_scan_fwd")
    out = _gated_norm_fwd(y, zx, norm_w, name + "_gate_fwd")
    return out, (zx, conv_w, conv_b, dtb, alog, dsk, norm_w, act, y, st_in)


@functools.partial(jax.custom_vjp, nondiff_argnums=(7,))
def ssm_core(zx, conv_w, conv_b, dtb, alog, dsk, norm_w, name):
    return _ssm_core_fwd(zx, conv_w, conv_b, dtb, alog, dsk, norm_w, name)[0]


def _ssm_core_vjp_fwd(zx, conv_w, conv_b, dtb, alog, dsk, norm_w, name):
    return _ssm_core_fwd(zx, conv_w, conv_b, dtb, alog, dsk, norm_w, name)


def _ssm_core_vjp_bwd(name, res, dout):
    zx, conv_w, conv_b, dtb, alog, dsk, norm_w, act, y, st_in = res
    dy, dz, dnorm_w = _gated_norm_bwd(y, zx, norm_w, dout, name + "_gate_bwd")
    dact, ddt, ddtb, dalog, ddsk = _ssd_scan_bwd(act, zx, dtb, alog, dsk, st_in, dy, name + "_scan_bwd")
    dxbc, dconv_w, dconv_b = _ssm_conv_bwd(zx, conv_w, conv_b, dact, name + "_conv_bwd")
    dzx = jnp.concatenate([dz, dxbc, ddt], axis=1)
    return dzx, dconv_w, dconv_b, ddtb, dalog, ddsk, dnorm_w


ssm_core.defvjp(_ssm_core_vjp_fwd, _ssm_core_vjp_bwd)


def ssd_mixer_p(h, w_in, conv_w, conv_b, dtb, alog, dsk, norm_w, w_out, tag):
    zx = linear(h, w_in, F32, tag + "_in")
    return linear(ssm_core(zx, conv_w, conv_b, dtb, alog, dsk, norm_w, tag), w_out, F32, tag + "_out")


def trunk(w, x):
    for i in range(DEPTH):
        h = norm(x, w['mix_norm_w'][i], f"mixnorm{i}")
        j = i // 2
        if i % 2 == 0:
            x = x + attention_mixer_p(h, w['attn_w_qkv'][j], w['attn_w_o'][j], f"attn{j}")
        else:
            x = x + ssd_mixer_p(h, w['ssm_w_in'][j], w['ssm_conv_w'][j], w['ssm_conv_b'][j], w['ssm_dt_bias'][j],
                                w['ssm_a_log'][j], w['ssm_d'][j], w['ssm_norm_w'][j], w['ssm_w_out'][j], f"ssm{j}")
        h = norm(x, w['ffn_norm_w'][i], f"ffnnorm{i}")
        x = x + conv_ffn_p(h, w['ffn_w_up'][i], w['ffn_conv_w'][i], w['ffn_conv_b'][i], w['ffn_w_down'][i],
                           f"ffn{i}")
    return x


def local_step(w, x, target):
    final_w = w['final_norm_w']
    trunk_w = {n: a for n, a in w.items() if n != 'final_norm_w'}
    xf, pullback = jax.vjp(trunk, trunk_w, x)
    loss, dxf, dfinal = loss_head(xf, final_w, target, "loss_head")
    gw, gx = pullback(dxf)
    gw['final_norm_w'] = dfinal
    return loss, gw, gx


def _adam_math(w, g, m, v):
    m = ADAM_B1 * m + (1.0 - ADAM_B1) * g
    v = ADAM_B2 * v + (1.0 - ADAM_B2) * (g * g)
    m_hat = m / (1.0 - ADAM_B1 ** ADAM_STEP)
    v_hat = v / (1.0 - ADAM_B2 ** ADAM_STEP)
    delta = -ADAM_LR * (m_hat / (jnp.sqrt(v_hat) + ADAM_EPS) + ADAM_WD * w)
    return delta, m, v


def _adamw_sum8(pieces, w, m, v, name):
    r, c = w.shape
    tr = _pick(r, (256, 128, 64, 32, 16, 8))

    def body(p_ref, w_ref, m_ref, v_ref, g_out, d_out, m_out, v_out):
        g = p_ref[0].astype(F32)
        for j in range(1, N_DEV):
            g = g + p_ref[j].astype(F32)
        delta, mm, vv = _adam_math(w_ref[...], g, m_ref[...], v_ref[...])
        g_out[...] = g
        d_out[...] = delta
        m_out[...] = mm
        v_out[...] = vv

    blk = pl.BlockSpec((tr, c), lambda i: (i, 0))
    return pl.pallas_call(
        body, name=name,
        out_shape=[jax.ShapeDtypeStruct((r, c), F32)] * 4,
        grid=(r // tr,),
        in_specs=[pl.BlockSpec((N_DEV, tr, c), lambda i: (0, i, 0)), blk, blk, blk],
        out_specs=[blk] * 4,
        compiler_params=pltpu.CompilerParams(dimension_semantics=("parallel",)),
    )(pieces, w, m, v)


def _sum8(pieces, name):
    _, r, c = pieces.shape

    def body(p_ref, o_ref):
        g = p_ref[0]
        for j in range(1, N_DEV):
            g = g + p_ref[j]
        o_ref[...] = g

    return pl.pallas_call(
        body, name=name, out_shape=jax.ShapeDtypeStruct((r, c), F32),
        in_specs=[pl.BlockSpec(memory_space=pltpu.VMEM)], out_specs=pl.BlockSpec(memory_space=pltpu.VMEM),
    )(pieces)


def _adamw_plain(g, w, m, v, name):
    def body(g_ref, w_ref, m_ref, v_ref, d_out, m_out, v_out):
        delta, mm, vv = _adam_math(w_ref[...], g_ref[...], m_ref[...], v_ref[...])
        d_out[...] = delta
        m_out[...] = mm
        v_out[...] = vv

    vm = pl.BlockSpec(memory_space=pltpu.VMEM)
    return pl.pallas_call(
        body, name=name, out_shape=[jax.ShapeDtypeStruct(g.shape, F32)] * 3,
        in_specs=[vm] * 4, out_specs=[vm] * 3,
    )(g, w, m, v)


def _split8(full, axis):
    shp = full.shape
    t = full.reshape(shp[:axis] + (N_DEV, shp[axis] // N_DEV) + shp[axis + 1:])
    return jnp.moveaxis(t, axis, 0)


def _join8(parts, axis):
    t = jnp.moveaxis(parts, 0, axis)
    shp = t.shape
    return t.reshape(shp[:axis] + (shp[axis] * shp[axis + 1],) + shp[axis + 2:])


def _pack(arrs, lead, mult):
    flat = jnp.concatenate([a.reshape(a.shape[:lead] + (-1,)) for a in arrs], axis=-1)
    return _pad_rows(flat, mult)


def _unpack(buf, shapes, lead):
    flat = buf.reshape(buf.shape[:lead] + (-1,))
    out, off = [], 0
    for shp in shapes:
        n = math.prod(shp)
        out.append(flat[..., off:off + n].reshape(flat.shape[:lead] + tuple(shp)))
        off += n
    return out


def _own_shard(full, axis):
    size = full.shape[axis] // N_DEV
    return lax.dynamic_slice_in_dim(full, _my_index() * size, size, axis)


def kernel(x, mix_norm_w, attn_w_qkv, attn_w_o, ssm_w_in, ssm_conv_w, ssm_conv_b, ssm_dt_bias, ssm_a_log, ssm_d, ssm_norm_w, ssm_w_out, ffn_norm_w, ffn_w_up, ffn_conv_w, ffn_conv_b, ffn_w_down, final_norm_w, loss_target, m_mix_norm_w, m_attn_w_qkv, m_attn_w_o, m_ssm_w_in, m_ssm_conv_w, m_ssm_conv_b, m_ssm_dt_bias, m_ssm_a_log, m_ssm_d, m_ssm_norm_w, m_ssm_w_out, m_ffn_norm_w, m_ffn_w_up, m_ffn_conv_w, m_ffn_conv_b, m_ffn_w_down, m_final_norm_w, v_mix_norm_w, v_attn_w_qkv, v_attn_w_o, v_ssm_w_in, v_ssm_conv_w, v_ssm_conv_b, v_ssm_dt_bias, v_ssm_a_log, v_ssm_d, v_ssm_norm_w, v_ssm_w_out, v_ffn_norm_w, v_ffn_w_up, v_ffn_conv_w, v_ffn_conv_b, v_ffn_w_down, v_final_norm_w):
    args = dict(locals())
    w_sh = {n: args[n] for n in WEIGHT_NAMES}
    m_sh = {n: args["m_" + n] for n in WEIGHT_NAMES}
    v_sh = {n: args["v_" + n] for n in WEIGHT_NAMES}

    big_shapes = [w_sh[n].shape for n in BIG]
    big = _exchange(_pack([w_sh[n].astype(BF16) for n in BIG], 0, 16), True, "gather_big")
    small_shapes = [w_sh[n].shape for n in SMALL_SHARDED]
    small = _exchange(_pack([w_sh[n] for n in SMALL_SHARDED], 0, 8), True, "gather_small")
    full = {n: w_sh[n] for n in SMALL if SHARD_AXIS[n] is None}
    for n, parts in zip(BIG, _unpack(big, big_shapes, 1)):
        full[n] = _join8(parts, SHARD_AXIS[n])
    for n, parts in zip(SMALL_SHARDED, _unpack(small, small_shapes, 1)):
        full[n] = _join8(parts, SHARD_AXIS[n])
    full['ssm_w_in'] = jnp.pad(full['ssm_w_in'], ((0, 0), (0, 0), (0, SSM_IN_PAD - SSM_IN_WIDTH)))

    loss, gw, gx = local_step(full, x[0], loss_target[0])
    gw['ssm_w_in'] = gw['ssm_w_in'][:, :, :SSM_IN_WIDTH]
    loss = lax.psum(loss, ("x", "y", "c"))

    gbig = _pack([_split8(gw[n], SHARD_AXIS[n]) for n in BIG], 1, 16)
    gbig = _exchange(gbig, False, "scatter_big")
    grads, deltas, new_m, new_v = {}, {}, {}, {}
    for n, pieces in zip(BIG, _unpack(gbig, big_shapes, 1)):
        shp = w_sh[n].shape
        two_d = (shp[0] * shp[1], shp[2])
        outs = _adamw_sum8(pieces.reshape((N_DEV,) + two_d), w_sh[n].reshape(two_d), m_sh[n].reshape(two_d),
                           v_sh[n].reshape(two_d), "adamw_" + n)
        grads[n], deltas[n], new_m[n], new_v[n] = [o.reshape(shp) for o in outs]

    small_full_shapes = [gw[n].shape for n in SMALL]
    gsmall = _exchange(_pack([gw[n] for n in SMALL], 0, 8), True, "gather_small_grads")
    gsmall = _unpack(_sum8(gsmall, "sum_small_grads"), small_full_shapes, 0)
    for n, g in zip(SMALL, gsmall):
        grads[n] = g if SHARD_AXIS[n] is None else _own_shard(g, SHARD_AXIS[n])
    shapes = [w_sh[n].shape for n in SMALL]
    outs = _adamw_plain(*[_pack([d[n] for n in SMALL], 0, 8) for d in (grads, w_sh, m_sh, v_sh)], "adamw_small")
    for d, buf in zip((deltas, new_m, new_v), outs):
        for n, a in zip(SMALL, _unpack(buf, shapes, 0)):
            d[n] = a

    return (loss, gx[None], *[grads[n] for n in WEIGHT_NAMES], *[deltas[n] for n in WEIGHT_NAMES],
            *[new_m[n] for n in WEIGHT_NAMES], *[new_v[n] for n in WEIGHT_NAMES])
```

```python
import functools
import math

import jax
import jax.numpy as jnp
from jax import lax
from jax.experimental import pallas as pl
from jax.experimental.pallas import tpu as pltpu

F32 = jnp.float32
BF16 = jnp.bfloat16
N_DEV = 8
MESH_ID = pl.DeviceIdType.MESH

D_MODEL = 1024
DEPTH = 4
ATTN_HEADS = 8
ATTN_HEAD_DIM = 128
ATTN_DILATIONS = (1, 4, 16)
ATTN_STEPS = (128, 128, 128)
N_ATTN_GROUPS = 3
ATTN_BLOCK = 128
ROPE_THETA = 500000.0
ROPE_DIM = 32
ATTN_OUT_WIDTH = 1024
SSM_D_INNER = 2048
SSM_HEAD_DIM = 64
SSM_HEADS = 32
SSM_STATE = 128
SSM_GROUPS = 8
SSM_CHUNK = 128
SSM_CONV_DIM = 4096
SSM_IN_WIDTH = 6176
SSM_IN_PAD = 6272
D_FF = 2816
NORM_EPS = 1e-5
ADAM_LR = 0.001
ADAM_B1 = 0.9
ADAM_B2 = 0.999
ADAM_EPS = 1e-08
ADAM_WD = 0.01
ADAM_STEP = 10

WEIGHT_NAMES = ['mix_norm_w', 'attn_w_qkv', 'attn_w_o', 'ssm_w_in', 'ssm_conv_w', 'ssm_conv_b', 'ssm_dt_bias',
                'ssm_a_log', 'ssm_d', 'ssm_norm_w', 'ssm_w_out', 'ffn_norm_w', 'ffn_w_up', 'ffn_conv_w',
                'ffn_conv_b', 'ffn_w_down', 'final_norm_w']
SHARD_AXIS = {'mix_norm_w': None, 'attn_w_qkv': 2, 'attn_w_o': 1, 'ssm_w_in': 2, 'ssm_conv_w': 2, 'ssm_conv_b': 1,
              'ssm_dt_bias': None, 'ssm_a_log': None, 'ssm_d': None, 'ssm_norm_w': 1, 'ssm_w_out': 1,
              'ffn_norm_w': None, 'ffn_w_up': 2, 'ffn_conv_w': 2, 'ffn_conv_b': None, 'ffn_w_down': 1,
              'final_norm_w': None}
BIG = ['attn_w_qkv', 'attn_w_o', 'ssm_w_in', 'ssm_w_out', 'ffn_w_up', 'ffn_w_down']
SMALL = [n for n in WEIGHT_NAMES if n not in BIG]
SMALL_SHARDED = [n for n in SMALL if SHARD_AXIS[n] is not None]
LANES = 1024


def _my_index():
    return 4 * lax.axis_index("x") + 2 * lax.axis_index("y") + lax.axis_index("c")


def _peer(k):
    x, y, c = lax.axis_index("x"), lax.axis_index("y"), lax.axis_index("c")
    return (x ^ ((k >> 2) & 1), y ^ ((k >> 1) & 1), c ^ (k & 1))


def _exchange(src, gather, name):
    shape = src.shape if gather else src.shape[1:]

    def body(src_ref, out_ref, send_sems, recv_sems, local_sem):
        me = _my_index()

        def piece(j):
            return src_ref if gather else src_ref.at[j]

        mine = pltpu.make_async_copy(piece(me), out_ref.at[me], local_sem)
        mine.start()
        copies = []
        for k in range(1, N_DEV):
            other = me ^ k
            cp = pltpu.make_async_remote_copy(
                src_ref=piece(other), dst_ref=out_ref.at[me], send_sem=send_sems.at[k - 1],
                recv_sem=recv_sems.at[k - 1], device_id=_peer(k), device_id_type=MESH_ID)
            cp.start()
            copies.append(cp)
        for k in range(1, N_DEV):
            other = me ^ k
            pltpu.make_async_remote_copy(
                src_ref=piece(other), dst_ref=out_ref.at[other], send_sem=send_sems.at[k - 1],
                recv_sem=recv_sems.at[k - 1], device_id=_peer(k), device_id_type=MESH_ID).wait_recv()
        for cp in copies:
            cp.wait_send()
        mine.wait()

    return pl.pallas_call(
        body, name=name,
        out_shape=jax.ShapeDtypeStruct((N_DEV,) + shape, src.dtype),
        in_specs=[pl.BlockSpec(memory_space=pl.ANY)],
        out_specs=pl.BlockSpec(memory_space=pl.ANY),
        scratch_shapes=[pltpu.SemaphoreType.DMA((N_DEV - 1,)), pltpu.SemaphoreType.DMA((N_DEV - 1,)),
                        pltpu.SemaphoreType.DMA],
    )(src)


def _pad_rows(flat, mult):
    n = flat.shape[-1]
    rows = -(-n // (LANES * mult)) * mult
    pad = rows * LANES - n
    flat = jnp.pad(flat, [(0, 0)] * (flat.ndim - 1) + [(0, pad)])
    return flat.reshape(flat.shape[:-1] + (rows, LANES))


def _pick(n, cands):
    for c in cands:
        if n % c == 0:
            return c
    return n


def _matmul(a, b, ta, tb, out_dtype, name):
    (m, k) = (a.shape[1], a.shape[0]) if ta else a.shape
    (k2, n) = (b.shape[1], b.shape[0]) if tb else b.shape
    assert k == k2, (a.shape, b.shape, ta, tb)
    tm = _pick(m, (1024, 1408, 512, 256, 128))
    tn = _pick(n, (1536, 1408, 1024, 896, 512, 384, 256, 128))
    tk = _pick(k, (1024, 1408, 896, 512, 256, 128)) if k > 2816 else k
    nk = k // tk
    dims = (((0 if ta else 1,), (1 if tb else 0,)), ((), ()))

    def body(a_ref, b_ref, o_ref, *scratch):
        part = lax.dot_general(a_ref[...].astype(BF16), b_ref[...].astype(BF16), dims,
                               preferred_element_type=F32)
        if nk == 1:
            o_ref[...] = part.astype(o_ref.dtype)
        else:
            acc_ref, = scratch
            kk = pl.program_id(2)

            @pl.when(kk == 0)
            def _():
                acc_ref[...] = part

            @pl.when(kk > 0)
            def _():
                acc_ref[...] += part

            @pl.when(kk == nk - 1)
            def _():
                o_ref[...] = acc_ref[...].astype(o_ref.dtype)

    a_spec = (pl.BlockSpec((tk, tm), lambda j, i, kk: (kk, i)) if ta
              else pl.BlockSpec((tm, tk), lambda j, i, kk: (i, kk)))
    b_spec = (pl.BlockSpec((tn, tk), lambda j, i, kk: (j, kk)) if tb
              else pl.BlockSpec((tk, tn), lambda j, i, kk: (kk, j)))
    return pl.pallas_call(
        body, name=name,
        out_shape=jax.ShapeDtypeStruct((m, n), out_dtype),
        grid=(n // tn, m // tm, nk),
        in_specs=[a_spec, b_spec],
        out_specs=pl.BlockSpec((tm, tn), lambda j, i, kk: (i, j)),
        scratch_shapes=[] if nk == 1 else [pltpu.VMEM((tm, tn), F32)],
        compiler_params=pltpu.CompilerParams(
            dimension_semantics=("parallel", "parallel", "arbitrary"), vmem_limit_bytes=56 * 1024 * 1024),
    )(a, b)


@functools.partial(jax.custom_vjp, nondiff_argnums=(2, 3))
def linear(a, w, out_dtype, name):
    return _matmul(a, w, False, False, out_dtype, name + "_fwd")


def _linear_fwd(a, w, out_dtype, name):
    return _matmul(a, w, False, False, out_dtype, name + "_fwd"), (a, w)


def _linear_bwd(out_dtype, name, res, dy):
    a, w = res
    da = _matmul(dy, w, False, True, a.dtype, name + "_da")
    dw = _matmul(a, dy, True, False, w.dtype, name + "_dw")
    return da, dw


linear.defvjp(_linear_fwd, _linear_bwd)


def rmsnorm(x, w):
    y = x * lax.rsqrt(jnp.mean(x * x, axis=-1, keepdims=True) + NORM_EPS)
    return y * w


def causal_depthwise_conv(x, w, b):
    k = w.shape[0]
    y = lax.conv_general_dilated(
        x[None], w[:, None, :], window_strides=(1,), padding=[(k - 1, 0)],
        dimension_numbers=("NWC", "WIO", "NWC"), feature_group_count=x.shape[-1])[0]
    return y + b


def rope_tables(seq):
    pos = jnp.arange(seq, dtype=F32)
    inv_freq = ROPE_THETA ** (-jnp.arange(0, ROPE_DIM, 2, dtype=F32) / ROPE_DIM)
    ang = pos[:, None] * inv_freq[None, :]
    return jnp.cos(ang), jnp.sin(ang)


def apply_partial_rope(t, cos, sin):
    half = ROPE_DIM // 2
    c = cos[:, None, None, :]
    s = sin[:, None, None, :]
    x1 = t[..., :half]
    x2 = t[..., half:ROPE_DIM]
    return jnp.concatenate([x1 * c - x2 * s, x2 * c + x1 * s, t[..., ROPE_DIM:]], axis=-1)


def dilated_window_attention(q, k, v, dilation, steps):
    s, h, hd = q.shape
    length = s // dilation
    nb = -(-length // ATTN_BLOCK)
    lp = nb * ATTN_BLOCK

    def to_strided(t):
        t = t.reshape(length, dilation, h, hd).transpose(1, 2, 0, 3)
        t = jnp.pad(t, ((0, 0), (0, 0), (0, lp - length), (0, 0)))
        return t.reshape(dilation, h, nb, ATTN_BLOCK, hd)

    def with_prev(t):
        prev = jnp.pad(t, ((0, 0), (0, 0), (1, 0), (0, 0), (0, 0)))[:, :, :-1]
        return jnp.concatenate([prev, t], axis=-2)

    qb = to_strided(q)
    kk = with_prev(to_strided(k))
    vv = with_prev(to_strided(v))
    scores = jnp.einsum("rhnqe,rhnke->rhnqk", qb, kk) * (hd ** -0.5)
    n_idx = jnp.arange(nb)[:, None, None]
    i_idx = jnp.arange(ATTN_BLOCK)[None, :, None]
    j_idx = jnp.arange(2 * ATTN_BLOCK)[None, None, :]
    delta = ATTN_BLOCK + i_idx - j_idx
    key_pos = (n_idx - 1) * ATTN_BLOCK + j_idx
    allowed = (delta >= 0) & (delta <= steps) & (key_pos >= 0)
    scores = jnp.where(allowed, scores, -jnp.inf)
    m = jnp.max(scores, axis=-1, keepdims=True)
    p = jnp.exp(scores - m)
    den = jnp.sum(p, axis=-1, keepdims=True)
    o = jnp.einsum("rhnqk,rhnke->rhnqe", p, vv) / den
    lse = (m + jnp.log(den))[..., 0]
    o = o.reshape(dilation, h, lp, hd)[:, :, :length]
    o = o.transpose(2, 0, 1, 3).reshape(s, h, hd)
    lse = lse.reshape(dilation, h, lp)[:, :, :length]
    lse = lse.transpose(2, 0, 1).reshape(s, h)
    return o, lse


def attention_mixer(h, w_qkv, w_o, cos, sin, tag):
    s = h.shape[0]
    qkv = linear(h.astype(BF16), w_qkv, F32, tag + "_qkv")
    qkv = qkv.reshape(s, N_ATTN_GROUPS, 3, ATTN_HEADS, ATTN_HEAD_DIM)
    q = apply_partial_rope(qkv[:, :, 0], cos, sin)
    k = apply_partial_rope(qkv[:, :, 1], cos, sin)
    v = qkv[:, :, 2]
    outs, lses = [], []
    for g in range(N_ATTN_GROUPS):
        o_g, lse_g = dilated_window_attention(q[:, g], k[:, g], v[:, g], ATTN_DILATIONS[g], ATTN_STEPS[g])
        outs.append(o_g)
        lses.append(lse_g)
    wts = jax.nn.softmax(jnp.stack(lses, axis=1), axis=1)
    o = jnp.einsum("sgh,sghe->she", wts, jnp.stack(outs, axis=1))
    return linear(o.reshape(s, ATTN_OUT_WIDTH).astype(BF16), w_o, F32, tag + "_o")


def ssd_chunked_scan(x, dt, a, bm, cm):
    s, h, p = x.shape
    g, n = bm.shape[1], bm.shape[2]
    r = h // g
    c = s // SSM_CHUNK
    q = SSM_CHUNK
    x = x.reshape(c, q, g, r, p)
    dt = dt.reshape(c, q, g, r)
    bm = bm.reshape(c, q, g, n)
    cm = cm.reshape(c, q, g, n)
    a_dt = dt * a.reshape(g, r)
    a_cs = jnp.cumsum(a_dt, axis=1)
    xdt = x * dt[..., None]
    seg = a_cs[:, :, None] - a_cs[:, None, :]
    causal = jnp.tril(jnp.ones((q, q), dtype=bool))[:, :, None, None]
    lmat = jnp.exp(jnp.where(causal, seg, -jnp.inf))
    cb = jnp.einsum("cign,cjgn->cijg", cm, bm)
    y_diag = jnp.einsum("cijgr,cjgrp->cigrp", cb[..., None] * lmat, xdt)
    decay = jnp.exp(a_cs[:, -1:] - a_cs)
    states = jnp.einsum("cjgn,cjgr,cjgrp->cgrpn", bm, decay, xdt)
    chunk_decay = jnp.exp(a_cs[:, -1])

    def step(state, inp):
        st_c, dec_c = inp
        return state * dec_c[..., None, None] + st_c, state

    init = jnp.zeros((g, r, p, n), dtype=x.dtype)
    _, prev = lax.scan(step, init, (states, chunk_decay))
    y_off = jnp.einsum("cign,cgrpn->cigrp", cm, prev) * jnp.exp(a_cs)[..., None]
    return (y_diag + y_off).reshape(s, h, p)


def gated_group_rmsnorm(y, z, w, groups):
    g = y * jax.nn.silu(z)
    shp = g.shape
    g = g.reshape(shp[:-1] + (groups, shp[-1] // groups))
    g = g * lax.rsqrt(jnp.mean(g * g, axis=-1, keepdims=True) + NORM_EPS)
    return g.reshape(shp) * w


def ssd_mixer(h, w_in, conv_w, conv_b, dt_bias, a_log, d_skip, norm_w, w_out, tag):
    s = h.shape[0]
    gn = SSM_GROUPS * SSM_STATE
    zxbcdt = linear(h.astype(BF16), w_in, F32, tag + "_in")
    z = zxbcdt[:, :SSM_D_INNER]
    xbc = zxbcdt[:, SSM_D_INNER:SSM_D_INNER + SSM_CONV_DIM]
    dt_raw = zxbcdt[:, SSM_D_INNER + SSM_CONV_DIM:SSM_IN_WIDTH]
    xbc = jax.nn.silu(causal_depthwise_conv(xbc, conv_w, conv_b))
    xs = xbc[:, :SSM_D_INNER].reshape(s, SSM_HEADS, SSM_HEAD_DIM)
    bm = xbc[:, SSM_D_INNER:SSM_D_INNER + gn].reshape(s, SSM_GROUPS, SSM_STATE)
    cm = xbc[:, SSM_D_INNER + gn:].reshape(s, SSM_GROUPS, SSM_STATE)
    dt = jax.nn.softplus(dt_raw + dt_bias)
    a = -jnp.exp(a_log)
    y = ssd_chunked_scan(xs, dt, a, bm, cm)
    y = y + d_skip[:, None] * xs
    y = gated_group_rmsnorm(y.reshape(s, SSM_D_INNER), z, norm_w, SSM_GROUPS)
    return linear(y.astype(BF16), w_out, F32, tag + "_out")


def conv_ffn(h, w_up, conv_w, conv_b, w_down, tag):
    u = causal_depthwise_conv(linear(h.astype(BF16), w_up, F32, tag + "_up"), conv_w, conv_b)
    gate, up = u[:, :D_FF], u[:, D_FF:]
    return linear((jax.nn.silu(gate) * up).astype(BF16), w_down, F32, tag + "_down")


VMEM_LIMIT = 56 * 1024 * 1024
SUBLANES = 8


def _params(*sem):
    return pltpu.CompilerParams(dimension_semantics=sem, vmem_limit_bytes=VMEM_LIMIT)


def _sigmoid(x):
    return 1.0 / (1.0 + jnp.exp(-x))


def _rstd(xv):
    return lax.rsqrt(jnp.mean(xv * xv, axis=-1, keepdims=True) + NORM_EPS)


def _accumulate(ref, part, first):
    @pl.when(first)
    def _():
        ref[...] = part

    @pl.when(jnp.logical_not(first))
    def _():
        ref[...] += part


def _norm_fwd(x, w, name):
    s, d = x.shape
    tm = _pick(s, (512, 256, 128))

    def body(x_ref, w_ref, h_ref):
        xv = x_ref[...]
        h_ref[...] = (xv * _rstd(xv) * w_ref[...]).astype(h_ref.dtype)

    return pl.pallas_call(
        body, name=name, out_shape=jax.ShapeDtypeStruct((s, d), BF16), grid=(s // tm,),
        in_specs=[pl.BlockSpec((tm, d), lambda i: (i, 0)), pl.BlockSpec((1, d), lambda i: (0, 0))],
        out_specs=pl.BlockSpec((tm, d), lambda i: (i, 0)), compiler_params=_params("parallel"),
    )(x, w.reshape(1, d))


def _norm_bwd(x, w, dh, name):
    s, d = x.shape
    tm = _pick(s, (512, 256, 128))

    def body(x_ref, w_ref, dh_ref, dx_ref, dw_ref):
        xv = x_ref[...]
        r = _rstd(xv)
        y = xv * r
        dhv = dh_ref[...].astype(F32)
        dy = dhv * w_ref[...]
        dx_ref[...] = r * (dy - y * jnp.mean(dy * y, axis=-1, keepdims=True))
        _accumulate(dw_ref, jnp.sum(dhv * y, axis=0, keepdims=True), pl.program_id(0) == 0)

    dx, dw = pl.pallas_call(
        body, name=name,
        out_shape=[jax.ShapeDtypeStruct((s, d), F32), jax.ShapeDtypeStruct((1, d), F32)], grid=(s // tm,),
        in_specs=[pl.BlockSpec((tm, d), lambda i: (i, 0)), pl.BlockSpec((1, d), lambda i: (0, 0)),
                  pl.BlockSpec((tm, d), lambda i: (i, 0))],
        out_specs=[pl.BlockSpec((tm, d), lambda i: (i, 0)), pl.BlockSpec((1, d), lambda i: (0, 0))],
        compiler_params=_params("arbitrary"),
    )(x, w.reshape(1, d), dh)
    return dx, dw.reshape(d)


@functools.partial(jax.custom_vjp, nondiff_argnums=(2,))
def norm(x, w, name):
    return _norm_fwd(x, w, name + "_fwd")


def _norm_vjp_fwd(x, w, name):
    return _norm_fwd(x, w, name + "_fwd"), (x, w)


def _norm_vjp_bwd(name, res, dh):
    x, w = res
    return _norm_bwd(x, w, dh, name + "_bwd")


norm.defvjp(_norm_vjp_fwd, _norm_vjp_bwd)


def loss_head(x, w, target, name):
    s, d = x.shape
    tm = _pick(s, (512, 256, 128))

    def body(x_ref, w_ref, t_ref, loss_ref, dx_ref, dw_ref):
        first = pl.program_id(0) == 0
        xv = x_ref[...]
        r = _rstd(xv)
        y = xv * r
        err = y * w_ref[...] - t_ref[...]
        part = 0.5 * jnp.sum(jnp.sum(err * err, axis=-1, keepdims=True), axis=0, keepdims=True) / d
        _accumulate(loss_ref, jnp.broadcast_to(part, loss_ref.shape), first)
        dout = err / d
        dy = dout * w_ref[...]
        dx_ref[...] = r * (dy - y * jnp.mean(dy * y, axis=-1, keepdims=True))
        _accumulate(dw_ref, jnp.sum(dout * y, axis=0, keepdims=True), first)

    row = pl.BlockSpec((tm, d), lambda i: (i, 0))
    vec = pl.BlockSpec((1, d), lambda i: (0, 0))
    loss, dx, dw = pl.pallas_call(
        body, name=name,
        out_shape=[jax.ShapeDtypeStruct((1, 128), F32), jax.ShapeDtypeStruct((s, d), F32),
                   jax.ShapeDtypeStruct((1, d), F32)],
        grid=(s // tm,), in_specs=[row, vec, row],
        out_specs=[pl.BlockSpec((1, 128), lambda i: (0, 0)), row, vec],
        compiler_params=_params("arbitrary"),
    )(x, w.reshape(1, d), target)
    return loss[0, 0], dx, dw.reshape(d)


FFN_ROWS = 256
FFN_COLS = 256


def _shift_down(cur, halo, k):
    out = pltpu.roll(cur, k, axis=0)
    row = lax.broadcasted_iota(jnp.int32, cur.shape, 0)
    for j in range(k):
        out = jnp.where(row == j, halo[SUBLANES - k + j:SUBLANES - k + j + 1, :], out)
    return out


def _shift_up(cur, nxt, k):
    n = cur.shape[0]
    out = pltpu.roll(cur, n - k, axis=0)
    row = lax.broadcasted_iota(jnp.int32, cur.shape, 0)
    for j in range(k):
        out = jnp.where(row == n - k + j, nxt[j:j + 1, :], out)
    return out


def _conv_taps(cur, halo, ntaps):
    return [_shift_down(cur, halo, ntaps - 1 - k) if k < ntaps - 1 else cur for k in range(ntaps)]


def _pad_taps(conv_w):
    return jnp.pad(conv_w, ((0, SUBLANES - conv_w.shape[0]), (0, 0)))


def _ffn_mid_fwd(u0, conv_w, conv_b, name):
    s, width = u0.shape
    half = width // 2
    tm = _pick(s, (FFN_ROWS, 128))
    per = tm // SUBLANES

    def body(w_ref, b_ref, u_ref, halo_ref, a_ref):
        keep = pl.program_id(0) > 0
        for c0 in range(0, half, FFN_COLS):
            vals = []
            for base in (c0, half + c0):
                cols = slice(base, base + FFN_COLS)
                halo = jnp.where(keep, halo_ref[:, cols], 0.0)
                taps = _conv_taps(u_ref[:, cols], halo, 3)
                vals.append(sum(w_ref[k:k + 1, cols] * taps[k] for k in range(3)) + b_ref[:, cols])
            gate, up = vals
            a_ref[:, c0:c0 + FFN_COLS] = (gate * _sigmoid(gate) * up).astype(a_ref.dtype)

    return pl.pallas_call(
        body, name=name, out_shape=jax.ShapeDtypeStruct((s, half), BF16), grid=(s // tm,),
        in_specs=[pl.BlockSpec((SUBLANES, width), lambda i: (0, 0)), pl.BlockSpec((1, width), lambda i: (0, 0)),
                  pl.BlockSpec((tm, width), lambda i: (i, 0)),
                  pl.BlockSpec((SUBLANES, width), lambda i: (jnp.maximum(i * per - 1, 0), 0))],
        out_specs=pl.BlockSpec((tm, half), lambda i: (i, 0)), compiler_params=_params("parallel"),
    )(_pad_taps(conv_w), conv_b.reshape(1, width), u0, u0)


def _ffn_mid_bwd(u0, conv_w, conv_b, da, name):
    s, width = u0.shape
    half = width // 2
    tm = _pick(s, (FFN_ROWS, 128))
    per = tm // SUBLANES
    nt = s // tm

    def body(w_ref, b_ref, u_ref, halo_ref, da_ref, du0_ref, dw_ref, db_ref, carry_ref):
        step = pl.program_id(0)
        first = step == 0
        keep = step < nt - 1
        for c0 in range(0, half, FFN_COLS):
            dav = da_ref[:, c0:c0 + FFN_COLS].astype(F32)
            taps, vals = [], []
            for base in (c0, half + c0):
                cols = slice(base, base + FFN_COLS)
                halo = jnp.where(keep, halo_ref[:, cols], 0.0)
                tp = _conv_taps(u_ref[:, cols], halo, 3)
                taps.append(tp)
                vals.append(sum(w_ref[k:k + 1, cols] * tp[k] for k in range(3)) + b_ref[:, cols])
            gate, up = vals
            sig = _sigmoid(gate)
            dus = [dav * up * sig * (1.0 + gate * (1.0 - sig)), dav * gate * sig]
            for base, tp, du in zip((c0, half + c0), taps, dus):
                cols = slice(base, base + FFN_COLS)
                nxt = jnp.where(first, 0.0, carry_ref[:, cols])
                du0 = (w_ref[2:3, cols] * du + w_ref[1:2, cols] * _shift_up(du, nxt, 1)
                       + w_ref[0:1, cols] * _shift_up(du, nxt, 2))
                du0_ref[:, cols] = du0.astype(du0_ref.dtype)
                carry_ref[:, cols] = du[0:SUBLANES, :]
                dwp = jnp.concatenate([jnp.sum(du * tp[k], axis=0, keepdims=True) for k in range(3)]
                                      + [jnp.zeros((SUBLANES - 3, FFN_COLS), F32)], axis=0)
                dbp = jnp.sum(du, axis=0, keepdims=True)

                @pl.when(first)
                def _():
                    dw_ref[:, cols] = dwp
                    db_ref[:, cols] = dbp

                @pl.when(jnp.logical_not(first))
                def _():
                    dw_ref[:, cols] += dwp
                    db_ref[:, cols] += dbp

    rev = lambda i: nt - 1 - i
    du0, dw, db = pl.pallas_call(
        body, name=name,
        out_shape=[jax.ShapeDtypeStruct((s, width), BF16), jax.ShapeDtypeStruct((SUBLANES, width), F32),
                   jax.ShapeDtypeStruct((1, width), F32)],
        grid=(nt,),
        in_specs=[pl.BlockSpec((SUBLANES, width), lambda i: (0, 0)), pl.BlockSpec((1, width), lambda i: (0, 0)),
                  pl.BlockSpec((tm, width), lambda i: (rev(i), 0)),
                  pl.BlockSpec((SUBLANES, width), lambda i: (jnp.maximum(rev(i) * per - 1, 0), 0)),
                  pl.BlockSpec((tm, half), lambda i: (rev(i), 0))],
        out_specs=[pl.BlockSpec((tm, width), lambda i: (rev(i), 0)),
                   pl.BlockSpec((SUBLANES, width), lambda i: (0, 0)), pl.BlockSpec((1, width), lambda i: (0, 0))],
        scratch_shapes=[pltpu.VMEM((SUBLANES, width), F32)],
        compiler_params=_params("arbitrary"),
    )(_pad_taps(conv_w), conv_b.reshape(1, width), u0, u0, da)
    return du0, dw[:3], db.reshape(width)


@functools.partial(jax.custom_vjp, nondiff_argnums=(3,))
def ffn_mid(u0, conv_w, conv_b, name):
    return _ffn_mid_fwd(u0, conv_w, conv_b, name + "_fwd")


def _ffn_mid_vjp_fwd(u0, conv_w, conv_b, name):
    return _ffn_mid_fwd(u0, conv_w, conv_b, name + "_fwd"), (u0, conv_w, conv_b)


def _ffn_mid_vjp_bwd(name, res, da):
    u0, conv_w, conv_b = res
    return _ffn_mid_bwd(u0, conv_w, conv_b, da, name + "_bwd")


ffn_mid.defvjp(_ffn_mid_vjp_fwd, _ffn_mid_vjp_bwd)


def conv_ffn_p(h, w_up, conv_w, conv_b, w_down, tag):
    u0 = linear(h, w_up, F32, tag + "_up")
    return linear(ffn_mid(u0, conv_w, conv_b, tag + "_mid"), w_down, F32, tag + "_down")


NEG = -1e30
HEAD_SLICES = [slice(hh * ATTN_HEAD_DIM, (hh + 1) * ATTN_HEAD_DIM) for hh in range(ATTN_HEADS)]
ATTN_SCALE = ATTN_HEAD_DIM ** -0.5
QKV_GROUP = 3 * ATTN_OUT_WIDTH


def rope_table(seq):
    cos, sin = rope_tables(seq)
    half = ROPE_DIM // 2
    ones = jnp.ones((seq, ATTN_HEAD_DIM - ROPE_DIM), F32)
    zero = lambda n: jnp.zeros((seq, n), F32)
    return jnp.concatenate([cos, cos, ones, -sin, zero(ATTN_HEAD_DIM - half),
                            zero(half), sin, zero(ATTN_HEAD_DIM - ROPE_DIM)], axis=1)


def _rope(t, tab, sign):
    half = ROPE_DIM // 2
    return t * tab[:, 0:128] + sign * (pltpu.roll(t, ATTN_HEAD_DIM - half, axis=1) * tab[:, 128:256]
                                       + pltpu.roll(t, half, axis=1) * tab[:, 256:384])


def _to_dilated(a, d):
    s = a.shape[0]
    return a if d == 1 else a.reshape(s // d, d, -1).transpose(1, 0, 2).reshape(s, -1)


def _from_dilated(a, d):
    s = a.shape[0]
    return a if d == 1 else a.reshape(d, s // d, -1).transpose(1, 0, 2).reshape(s, -1)


def _rope_qk(qkv, tab, name):
    s = qkv.shape[0]
    tm = _pick(s, (512, 256, 128))

    def body(t_ref, x_ref, o_ref):
        for hs in HEAD_SLICES:
            o_ref[:, hs] = _rope(x_ref[:, hs].astype(F32), t_ref[...], 1.0).astype(o_ref.dtype)

    blk = pl.BlockSpec((tm, ATTN_OUT_WIDTH), lambda i, c: (i, c))
    return pl.pallas_call(
        body, name=name, out_shape=jax.ShapeDtypeStruct(qkv.shape, qkv.dtype), grid=(s // tm, 2),
        in_specs=[pl.BlockSpec((tm, 384), lambda i, c: (i, 0)), blk], out_specs=blk,
        input_output_aliases={1: 0}, compiler_params=_params("parallel", "parallel"),
    )(tab, qkv)


def _dot_nt(a, b):
    return lax.dot_general(a, b, (((1,), (1,)), ((), ())), preferred_element_type=F32)


def _dot_tn(a, b):
    return lax.dot_general(a, b, (((0,), (0,)), ((), ())), preferred_element_type=F32)


def _dot(a, b):
    return jnp.dot(a, b, preferred_element_type=F32)


def _window_mask(has_prev):
    ii = lax.broadcasted_iota(jnp.int32, (ATTN_BLOCK, 2 * ATTN_BLOCK), 0)
    jj = lax.broadcasted_iota(jnp.int32, (ATTN_BLOCK, 2 * ATTN_BLOCK), 1)
    in_window = jnp.logical_and(jj >= ii, jj <= ii + ATTN_BLOCK)
    return jnp.logical_and(in_window, jnp.logical_or(jj >= ATTN_BLOCK, has_prev))


def _both(prev_ref, cur_ref, hs):
    return jnp.concatenate([prev_ref[:, hs], cur_ref[:, hs]], axis=0)


def _attn_group_fwd(qkv, d, name):
    s = qkv.shape[0]
    nb = s // d // ATTN_BLOCK

    def body(q_ref, kc_ref, kp_ref, vc_ref, vp_ref, o_ref, lse_ref):
        mask = _window_mask(pl.program_id(1) > 0)
        lane = lax.broadcasted_iota(jnp.int32, (ATTN_BLOCK, 128), 1)
        lse_tile = jnp.zeros((ATTN_BLOCK, 128), F32)
        for hh, hs in enumerate(HEAD_SLICES):
            sc = jnp.where(mask, _dot_nt(q_ref[:, hs], _both(kp_ref, kc_ref, hs)) * ATTN_SCALE, NEG)
            m = jnp.max(sc, axis=1, keepdims=True)
            p = jnp.exp(sc - m)
            den = jnp.sum(p, axis=1, keepdims=True)
            o_ref[:, hs] = _dot(p.astype(BF16), _both(vp_ref, vc_ref, hs)) / den
            lse_tile = jnp.where(lane == hh, m + jnp.log(den), lse_tile)
        lse_ref[...] = lse_tile

    cur = lambda t: pl.BlockSpec((ATTN_BLOCK, ATTN_OUT_WIDTH), lambda r, n: (r * nb + n, t))
    prv = lambda t: pl.BlockSpec((ATTN_BLOCK, ATTN_OUT_WIDTH), lambda r, n: (r * nb + jnp.maximum(n - 1, 0), t))
    return pl.pallas_call(
        body, name=name,
        out_shape=[jax.ShapeDtypeStruct((s, ATTN_OUT_WIDTH), F32), jax.ShapeDtypeStruct((s, 128), F32)],
        grid=(d, nb), in_specs=[cur(0), cur(1), prv(1), cur(2), prv(2)],
        out_specs=[pl.BlockSpec((ATTN_BLOCK, ATTN_OUT_WIDTH), lambda r, n: (r * nb + n, 0)),
                   pl.BlockSpec((ATTN_BLOCK, 128), lambda r, n: (r * nb + n, 0))],
        compiler_params=_params("parallel", "parallel"),
    )(qkv, qkv, qkv, qkv, qkv)


def _attn_combine(os_, lses, name):
    s = os_[0].shape[0]
    tm = _pick(s, (256, 128))
    ng = len(os_)

    def body(*refs):
        o_refs, l_refs, (o_ref, lse_ref) = refs[:ng], refs[ng:2 * ng], refs[2 * ng:]
        lane = lax.broadcasted_iota(jnp.int32, (tm, 128), 1)
        lse_tile = jnp.zeros((tm, 128), F32)
        for hh, hs in enumerate(HEAD_SLICES):
            ls = [l_ref[:, hh:hh + 1] for l_ref in l_refs]
            m = functools.reduce(jnp.maximum, ls)
            ws = [jnp.exp(l - m) for l in ls]
            tot = functools.reduce(lambda a, b: a + b, ws)
            acc = functools.reduce(lambda a, b: a + b, [o_r[:, hs] * w for o_r, w in zip(o_refs, ws)])
            o_ref[:, hs] = (acc / tot).astype(o_ref.dtype)
            lse_tile = jnp.where(lane == hh, m + jnp.log(tot), lse_tile)
        lse_ref[...] = lse_tile

    wide = pl.BlockSpec((tm, ATTN_OUT_WIDTH), lambda i: (i, 0))
    thin = pl.BlockSpec((tm, 128), lambda i: (i, 0))
    return pl.pallas_call(
        body, name=name,
        out_shape=[jax.ShapeDtypeStruct((s, ATTN_OUT_WIDTH), BF16), jax.ShapeDtypeStruct((s, 128), F32)],
        grid=(s // tm,), in_specs=[wide] * ng + [thin] * ng, out_specs=[wide, thin],
        compiler_params=_params("parallel"),
    )(*os_, *lses)


def _attn_delta(do, o, name):
    s = do.shape[0]
    tm = _pick(s, (256, 128))

    def body(do_ref, o_ref, out_ref):
        lane = lax.broadcasted_iota(jnp.int32, (tm, 128), 1)
        tile = jnp.zeros((tm, 128), F32)
        for hh, hs in enumerate(HEAD_SLICES):
            prod = do_ref[:, hs].astype(F32) * o_ref[:, hs].astype(F32)
            tile = jnp.where(lane == hh, jnp.sum(prod, axis=1, keepdims=True), tile)
        out_ref[...] = tile

    wide = pl.BlockSpec((tm, ATTN_OUT_WIDTH), lambda i: (i, 0))
    return pl.pallas_call(
        body, name=name, out_shape=jax.ShapeDtypeStruct((s, 128), F32), grid=(s // tm,),
        in_specs=[wide, wide], out_specs=pl.BlockSpec((tm, 128), lambda i: (i, 0)),
        compiler_params=_params("parallel"),
    )(do, o)


def _attn_group_bwd(qkv, do, lse, delta, tab, d, name):
    s = qkv.shape[0]
    nb = s // d // ATTN_BLOCK

    def body(q_ref, kc_ref, kp_ref, vc_ref, vp_ref, do_ref, lse_ref, dl_ref, tq_ref, tk_ref,
             dq_ref, dk_ref, dv_ref, ck_ref, cv_ref):
        n = pl.program_id(1)

        @pl.when(n < nb)
        def _():
            mask = _window_mask(n > 0)
            for hh, hs in enumerate(HEAD_SLICES):
                qh, doh = q_ref[:, hs], do_ref[:, hs]
                kh, vh = _both(kp_ref, kc_ref, hs), _both(vp_ref, vc_ref, hs)
                sc = jnp.where(mask, _dot_nt(qh, kh) * ATTN_SCALE, NEG)
                p = jnp.exp(sc - lse_ref[:, hh:hh + 1])
                ds = (p * (_dot_nt(doh, vh) - dl_ref[:, hh:hh + 1]) * ATTN_SCALE).astype(BF16)
                dq_ref[:, hs] = _rope(_dot(ds, kh), tq_ref[...], -1.0).astype(dq_ref.dtype)
                dk = _dot_tn(ds, qh)
                dv = _dot_tn(p.astype(BF16), doh)

                @pl.when(n > 0)
                def _():
                    dk_ref[:, hs] = _rope(ck_ref[:, hs] + dk[:ATTN_BLOCK], tk_ref[...], -1.0).astype(dk_ref.dtype)
                    dv_ref[:, hs] = (cv_ref[:, hs] + dv[:ATTN_BLOCK]).astype(dv_ref.dtype)

                ck_ref[:, hs] = dk[ATTN_BLOCK:]
                cv_ref[:, hs] = dv[ATTN_BLOCK:]

        @pl.when(n == nb)
        def _():
            for hs in HEAD_SLICES:
                dk_ref[:, hs] = _rope(ck_ref[:, hs], tk_ref[...], -1.0).astype(dk_ref.dtype)
                dv_ref[:, hs] = cv_ref[:, hs].astype(dv_ref.dtype)

    def spec(width, row, col):
        return pl.BlockSpec((ATTN_BLOCK, width), lambda r, n: (r * nb + row(n), col))

    cur = lambda n: jnp.minimum(n, nb - 1)
    prv = lambda n: jnp.maximum(jnp.minimum(n, nb - 1) - 1, 0)
    out = lambda n: jnp.maximum(n - 1, 0)
    wide = ATTN_OUT_WIDTH
    return pl.pallas_call(
        body, name=name,
        out_shape=[jax.ShapeDtypeStruct((s, wide), BF16)] * 3,
        grid=(d, nb + 1),
        in_specs=[spec(wide, cur, 0), spec(wide, cur, 1), spec(wide, prv, 1), spec(wide, cur, 2),
                  spec(wide, prv, 2), spec(wide, cur, 0), spec(128, cur, 0), spec(128, cur, 0),
                  spec(384, cur, 0), spec(384, out, 0)],
        out_specs=[spec(wide, cur, 0), spec(wide, out, 0), spec(wide, out, 0)],
        scratch_shapes=[pltpu.VMEM((ATTN_BLOCK, wide), F32), pltpu.VMEM((ATTN_BLOCK, wide), F32)],
        compiler_params=_params("parallel", "arbitrary"),
    )(qkv, qkv, qkv, qkv, qkv, do, lse, delta, tab, tab)


def _attn_core_fwd(qkvs, name):
    tab = rope_table(qkvs[0].shape[0])
    rot, os_, lses = [], [], []
    for g, (qkv, d) in enumerate(zip(qkvs, ATTN_DILATIONS)):
        qkv = _rope_qk(qkv, _to_dilated(tab, d), f"{name}_rope{g}")
        o_g, lse_g = _attn_group_fwd(qkv, d, f"{name}_fwd{g}")
        rot.append(qkv)
        os_.append(_from_dilated(o_g, d))
        lses.append(_from_dilated(lse_g, d))
    o, lse = _attn_combine(os_, lses, name + "_combine")
    return o, (tuple(rot), o, lse)


@functools.partial(jax.custom_vjp, nondiff_argnums=(1,))
def attn_core(qkvs, name):
    return _attn_core_fwd(qkvs, name)[0]


def _attn_core_vjp_fwd(qkvs, name):
    return _attn_core_fwd(qkvs, name)


def _attn_core_vjp_bwd(name, res, do):
    rot, o, lse = res
    tab = rope_table(o.shape[0])
    delta = _attn_delta(do, o, name + "_delta")
    out = []
    for g, (qkv, d) in enumerate(zip(rot, ATTN_DILATIONS)):
        parts = _attn_group_bwd(qkv, _to_dilated(do, d), _to_dilated(lse, d), _to_dilated(delta, d),
                                _to_dilated(tab, d), d, f"{name}_bwd{g}")
        out.append(jnp.concatenate(parts, axis=1))
    return (tuple(out),)


attn_core.defvjp(_attn_core_vjp_fwd, _attn_core_vjp_bwd)


def attention_mixer_p(h, w_qkv, w_o, tag):
    qkvs = tuple(linear(_to_dilated(h, d), w_qkv[:, g * QKV_GROUP:(g + 1) * QKV_GROUP], BF16, f"{tag}_qkv{g}")
                 for g, d in enumerate(ATTN_DILATIONS))
    return linear(attn_core(qkvs, tag), w_o, F32, tag + "_o")


SSM_CONV_TAPS = 4
SSM_COL_BLOCK = 2048
SSM_PAIRS = SSM_HEADS // 2
SSM_DT_BLOCK = (SSM_D_INNER + SSM_CONV_DIM) // 128


def _ssm_conv_fwd(zx, conv_w, conv_b, name):
    s = zx.shape[0]
    tm = _pick(s, (256, 128))
    per = tm // SUBLANES
    ncb = SSM_CONV_DIM // SSM_COL_BLOCK

    def body(w_ref, b_ref, x_ref, halo_ref, o_ref):
        keep = pl.program_id(1) > 0
        for c0 in range(0, SSM_COL_BLOCK, FFN_COLS):
            cols = slice(c0, c0 + FFN_COLS)
            halo = jnp.where(keep, halo_ref[:, cols], 0.0)
            taps = _conv_taps(x_ref[:, cols], halo, SSM_CONV_TAPS)
            pre = sum(w_ref[k:k + 1, cols] * taps[k] for k in range(SSM_CONV_TAPS)) + b_ref[:, cols]
            o_ref[:, cols] = pre * _sigmoid(pre)

    return pl.pallas_call(
        body, name=name, out_shape=jax.ShapeDtypeStruct((s, SSM_CONV_DIM), F32), grid=(ncb, s // tm),
        in_specs=[pl.BlockSpec((SUBLANES, SSM_COL_BLOCK), lambda j, i: (0, j)),
                  pl.BlockSpec((1, SSM_COL_BLOCK), lambda j, i: (0, j)),
                  pl.BlockSpec((tm, SSM_COL_BLOCK), lambda j, i: (i, j + 1)),
                  pl.BlockSpec((SUBLANES, SSM_COL_BLOCK), lambda j, i: (jnp.maximum(i * per - 1, 0), j + 1))],
        out_specs=pl.BlockSpec((tm, SSM_COL_BLOCK), lambda j, i: (i, j)),
        compiler_params=_params("parallel", "parallel"),
    )(_pad_taps(conv_w), conv_b.reshape(1, SSM_CONV_DIM), zx, zx)


def _ssm_conv_bwd(zx, conv_w, conv_b, dact, name):
    s = zx.shape[0]
    tm = _pick(s, (256, 128))
    per = tm // SUBLANES
    nt = s // tm
    ncb = SSM_CONV_DIM // SSM_COL_BLOCK
    nk = SSM_CONV_TAPS

    def body(w_ref, b_ref, x_ref, halo_ref, da_ref, dx_ref, dw_ref, db_ref, carry_ref):
        step = pl.program_id(1)
        first = step == 0
        keep = step < nt - 1
        for c0 in range(0, SSM_COL_BLOCK, FFN_COLS):
            cols = slice(c0, c0 + FFN_COLS)
            halo = jnp.where(keep, halo_ref[:, cols], 0.0)
            taps = _conv_taps(x_ref[:, cols], halo, nk)
            pre = sum(w_ref[k:k + 1, cols] * taps[k] for k in range(nk)) + b_ref[:, cols]
            sig = _sigmoid(pre)
            dpre = da_ref[:, cols] * sig * (1.0 + pre * (1.0 - sig))
            nxt = jnp.where(first, 0.0, carry_ref[:, cols])
            dx = w_ref[nk - 1:nk, cols] * dpre
            for k in range(nk - 1):
                dx = dx + w_ref[k:k + 1, cols] * _shift_up(dpre, nxt, nk - 1 - k)
            dx_ref[:, cols] = dx.astype(dx_ref.dtype)
            carry_ref[:, cols] = dpre[0:SUBLANES, :]
            dwp = jnp.concatenate([jnp.sum(dpre * taps[k], axis=0, keepdims=True) for k in range(nk)]
                                  + [jnp.zeros((SUBLANES - nk, FFN_COLS), F32)], axis=0)
            dbp = jnp.sum(dpre, axis=0, keepdims=True)

            @pl.when(first)
            def _():
                dw_ref[:, cols] = dwp
                db_ref[:, cols] = dbp

            @pl.when(jnp.logical_not(first))
            def _():
                dw_ref[:, cols] += dwp
                db_ref[:, cols] += dbp

    rev = lambda i: nt - 1 - i
    dx, dw, db = pl.pallas_call(
        body, name=name,
        out_shape=[jax.ShapeDtypeStruct((s, SSM_CONV_DIM), BF16), jax.ShapeDtypeStruct((SUBLANES, SSM_CONV_DIM), F32),
                   jax.ShapeDtypeStruct((1, SSM_CONV_DIM), F32)],
        grid=(ncb, nt),
        in_specs=[pl.BlockSpec((SUBLANES, SSM_COL_BLOCK), lambda j, i: (0, j)),
                  pl.BlockSpec((1, SSM_COL_BLOCK), lambda j, i: (0, j)),
                  pl.BlockSpec((tm, SSM_COL_BLOCK), lambda j, i: (rev(i), j + 1)),
                  pl.BlockSpec((SUBLANES, SSM_COL_BLOCK), lambda j, i: (jnp.maximum(rev(i) * per - 1, 0), j + 1)),
                  pl.BlockSpec((tm, SSM_COL_BLOCK), lambda j, i: (rev(i), j))],
        out_specs=[pl.BlockSpec((tm, SSM_COL_BLOCK), lambda j, i: (rev(i), j)),
                   pl.BlockSpec((SUBLANES, SSM_COL_BLOCK), lambda j, i: (0, j)),
                   pl.BlockSpec((1, SSM_COL_BLOCK), lambda j, i: (0, j))],
        scratch_shapes=[pltpu.VMEM((SUBLANES, SSM_COL_BLOCK), F32)],
        compiler_params=_params("parallel", "arbitrary"),
    )(_pad_taps(conv_w), conv_b.reshape(1, SSM_CONV_DIM), zx, zx, dact)
    return dx, dw[:nk], db.reshape(SSM_CONV_DIM)


def _ssd_chunk(xs, bms, cms, dt_raw, dtb, alog, dsk, states):
    q = SSM_CHUNK
    lane = lax.broadcasted_iota(jnp.int32, (1, 128), 1)
    row = lax.broadcasted_iota(jnp.int32, (q, 1), 0)
    ii = lax.broadcasted_iota(jnp.int32, (q, q), 0)
    jj = lax.broadcasted_iota(jnp.int32, (q, q), 1)
    tril = ii >= jj
    left = lane < SSM_HEAD_DIM
    last_row = (row == q - 1).astype(F32)

    def lanes_of(mat, h):
        pick = (lane == h).astype(F32)
        return jnp.broadcast_to(jnp.sum(mat * pick, axis=1, keepdims=True), mat.shape)

    def rows_of(mat_t, h):
        pick = (row == h).astype(F32)
        return jnp.broadcast_to(jnp.sum(mat_t * pick, axis=0, keepdims=True), mat_t.shape)

    v = dt_raw + dtb
    dt = jnp.maximum(v, 0.0) + jnp.log(1.0 + jnp.exp(-jnp.abs(v)))
    adt = dt * (-jnp.exp(alog))
    acs = jnp.dot(tril.astype(F32), adt, precision=lax.Precision.HIGHEST, preferred_element_type=F32)
    acs_t = acs.T
    ys, new_states = [], []
    for pr in range(SSM_PAIRS):
        g = pr // 2
        if pr % 2 == 0:
            cb = _dot_nt(cms[g].astype(BF16), bms[g].astype(BF16))
        cols = [lanes_of(acs, 2 * pr + e) for e in range(2)]
        dts = [lanes_of(dt, 2 * pr + e) for e in range(2)]
        rws = [rows_of(acs_t, 2 * pr + e) for e in range(2)]
        lasts = [jnp.sum(c * last_row, axis=0, keepdims=True) for c in cols]
        xdt = xs[pr] * jnp.where(left, dts[0], dts[1])
        halves = [jnp.where(left, xdt, 0.0).astype(BF16), jnp.where(left, 0.0, xdt).astype(BF16)]
        y_diag, s_new = 0.0, 0.0
        for e in range(2):
            lmat = jnp.where(tril, jnp.exp(jnp.minimum(cols[e] - rws[e], 0.0)), 0.0)
            y_diag = y_diag + _dot((cb * lmat).astype(BF16), halves[e])
            decay = jnp.exp(lasts[e] - cols[e])
            s_new = s_new + _dot_tn((bms[g] * decay).astype(BF16), halves[e])
        y_off = _dot(cms[g].astype(BF16), states[pr].astype(BF16)) * jnp.where(left, jnp.exp(cols[0]), jnp.exp(cols[1]))
        skip = jnp.where(left, lanes_of(dsk, 2 * pr), lanes_of(dsk, 2 * pr + 1))
        ys.append(y_diag + y_off + xs[pr] * skip)
        new_states.append(states[pr] * jnp.where(left, jnp.exp(lasts[0]), jnp.exp(lasts[1])) + s_new)
    return tuple(ys), tuple(new_states)


def _ssm_vec(v):
    return jnp.pad(v.reshape(1, -1), ((0, 0), (0, 128 - v.shape[0])))


def _ssd_scan_fwd(act, zx, dtb, alog, dsk, name):
    s = act.shape[0]
    nc = s // SSM_CHUNK
    ng = SSM_GROUPS

    def body(act_ref, dt_ref, dtb_ref, alog_ref, dsk_ref, y_ref, st_out_ref, st_ref):
        @pl.when(pl.program_id(0) == 0)
        def _():
            st_ref[...] = jnp.zeros_like(st_ref)

        tile = lambda k: act_ref[:, k * 128:(k + 1) * 128]
        xs = [tile(k) for k in range(SSM_PAIRS)]
        bms = [tile(SSM_PAIRS + k) for k in range(ng)]
        cms = [tile(SSM_PAIRS + ng + k) for k in range(ng)]
        states = [st_ref[k] for k in range(SSM_PAIRS)]
        st_out_ref[0] = st_ref[...]
        ys, new_states = _ssd_chunk(xs, bms, cms, dt_ref[...], dtb_ref[...], alog_ref[...], dsk_ref[...], states)
        for k in range(SSM_PAIRS):
            y_ref[:, k * 128:(k + 1) * 128] = ys[k]
            st_ref[k] = new_states[k]

    vec = pl.BlockSpec((1, 128), lambda c: (0, 0))
    return pl.pallas_call(
        body, name=name,
        out_shape=[jax.ShapeDtypeStruct((s, SSM_D_INNER), F32),
                   jax.ShapeDtypeStruct((nc, SSM_PAIRS, SSM_STATE, 128), F32)],
        grid=(nc,),
        in_specs=[pl.BlockSpec((SSM_CHUNK, SSM_CONV_DIM), lambda c: (c, 0)),
                  pl.BlockSpec((SSM_CHUNK, 128), lambda c: (c, SSM_DT_BLOCK)), vec, vec, vec],
        out_specs=[pl.BlockSpec((SSM_CHUNK, SSM_D_INNER), lambda c: (c, 0)),
                   pl.BlockSpec((1, SSM_PAIRS, SSM_STATE, 128), lambda c: (c, 0, 0, 0))],
        scratch_shapes=[pltpu.VMEM((SSM_PAIRS, SSM_STATE, 128), F32)],
        compiler_params=_params("arbitrary"),
    )(act, zx, _ssm_vec(dtb), _ssm_vec(alog), _ssm_vec(dsk))


def _ssd_scan_bwd(act, zx, dtb, alog, dsk, st_in, dy, name):
    s = act.shape[0]
    nc = s // SSM_CHUNK
    ng = SSM_GROUPS

    def body(act_ref, dt_ref, dtb_ref, alog_ref, dsk_ref, st_ref, dy_ref, dact_ref, ddt_ref, dpar_ref, dst_ref):
        first = pl.program_id(0) == 0

        @pl.when(first)
        def _():
            dst_ref[...] = jnp.zeros_like(dst_ref)

        tile = lambda k: act_ref[:, k * 128:(k + 1) * 128]
        xs = [tile(k) for k in range(SSM_PAIRS)]
        bms = [tile(SSM_PAIRS + k) for k in range(ng)]
        cms = [tile(SSM_PAIRS + ng + k) for k in range(ng)]
        states = [st_ref[0, k] for k in range(SSM_PAIRS)]
        _, pullback = jax.vjp(_ssd_chunk, xs, bms, cms, dt_ref[...], dtb_ref[...], alog_ref[...], dsk_ref[...],
                              states)
        dys = tuple(dy_ref[:, k * 128:(k + 1) * 128] for k in range(SSM_PAIRS))
        dsts = tuple(dst_ref[k] for k in range(SSM_PAIRS))
        dxs, dbms, dcms, ddt, ddtb, dalog, ddsk, dstates = pullback((dys, dsts))
        for k, t in enumerate(list(dxs) + list(dbms) + list(dcms)):
            dact_ref[:, k * 128:(k + 1) * 128] = t
        ddt_ref[...] = ddt.astype(ddt_ref.dtype)
        for k in range(SSM_PAIRS):
            dst_ref[k] = dstates[k]
        dpar = jnp.concatenate([ddtb, dalog, ddsk, jnp.zeros((SUBLANES - 3, 128), F32)], axis=0)
        _accumulate(dpar_ref, dpar, first)

    rev = lambda c: nc - 1 - c
    vec = pl.BlockSpec((1, 128), lambda c: (0, 0))
    dact, ddt, dpar = pl.pallas_call(
        body, name=name,
        out_shape=[jax.ShapeDtypeStruct((s, SSM_CONV_DIM), F32), jax.ShapeDtypeStruct((s, 128), BF16),
                   jax.ShapeDtypeStruct((SUBLANES, 128), F32)],
        grid=(nc,),
        in_specs=[pl.BlockSpec((SSM_CHUNK, SSM_CONV_DIM), lambda c: (rev(c), 0)),
                  pl.BlockSpec((SSM_CHUNK, 128), lambda c: (rev(c), SSM_DT_BLOCK)), vec, vec, vec,
                  pl.BlockSpec((1, SSM_PAIRS, SSM_STATE, 128), lambda c: (rev(c), 0, 0, 0)),
                  pl.BlockSpec((SSM_CHUNK, SSM_D_INNER), lambda c: (rev(c), 0))],
        out_specs=[pl.BlockSpec((SSM_CHUNK, SSM_CONV_DIM), lambda c: (rev(c), 0)),
                   pl.BlockSpec((SSM_CHUNK, 128), lambda c: (rev(c), 0)),
                   pl.BlockSpec((SUBLANES, 128), lambda c: (0, 0))],
        scratch_shapes=[pltpu.VMEM((SSM_PAIRS, SSM_STATE, 128), F32)],
        compiler_params=_params("arbitrary"),
    )(act, zx, _ssm_vec(dtb), _ssm_vec(alog), _ssm_vec(dsk), st_in, dy)
    return dact, ddt, dpar[0, :SSM_HEADS], dpar[1, :SSM_HEADS], dpar[2, :SSM_HEADS]


SSM_NORM_GROUP = SSM_D_INNER // SSM_GROUPS


def _gated_group(y, z, w):
    g = y * (z * _sigmoid(z))
    return g * lax.rsqrt(jnp.mean(g * g, axis=-1, keepdims=True) + NORM_EPS) * w


def _gated_norm_fwd(y, zx, w, name):
    s = y.shape[0]
    tm = _pick(s, (256, 128))

    def body(y_ref, z_ref, w_ref, o_ref):
        for c0 in range(0, SSM_D_INNER, SSM_NORM_GROUP):
            cols = slice(c0, c0 + SSM_NORM_GROUP)
            o_ref[:, cols] = _gated_group(y_ref[:, cols], z_ref[:, cols], w_ref[:, cols]).astype(o_ref.dtype)

    blk = pl.BlockSpec((tm, SSM_D_INNER), lambda i: (i, 0))
    return pl.pallas_call(
        body, name=name, out_shape=jax.ShapeDtypeStruct((s, SSM_D_INNER), BF16), grid=(s // tm,),
        in_specs=[blk, blk, pl.BlockSpec((1, SSM_D_INNER), lambda i: (0, 0))], out_specs=blk,
        compiler_params=_params("parallel"),
    )(y, zx, w.reshape(1, SSM_D_INNER))


def _gated_norm_bwd(y, zx, w, dout, name):
    s = y.shape[0]
    tm = _pick(s, (256, 128))

    def body(y_ref, z_ref, w_ref, do_ref, dy_ref, dz_ref, dw_ref):
        first = pl.program_id(0) == 0
        for c0 in range(0, SSM_D_INNER, SSM_NORM_GROUP):
            cols = slice(c0, c0 + SSM_NORM_GROUP)
            _, pullback = jax.vjp(_gated_group, y_ref[:, cols], z_ref[:, cols], w_ref[:, cols])
            dyv, dzv, dwv = pullback(do_ref[:, cols].astype(F32))
            dy_ref[:, cols] = dyv
            dz_ref[:, cols] = dzv.astype(dz_ref.dtype)

            @pl.when(first)
            def _():
                dw_ref[:, cols] = dwv

            @pl.when(jnp.logical_not(first))
            def _():
                dw_ref[:, cols] += dwv

    blk = pl.BlockSpec((tm, SSM_D_INNER), lambda i: (i, 0))
    vec = pl.BlockSpec((1, SSM_D_INNER), lambda i: (0, 0))
    dy, dz, dw = pl.pallas_call(
        body, name=name,
        out_shape=[jax.ShapeDtypeStruct((s, SSM_D_INNER), F32), jax.ShapeDtypeStruct((s, SSM_D_INNER), BF16),
                   jax.ShapeDtypeStruct((1, SSM_D_INNER), F32)],
        grid=(s // tm,), in_specs=[blk, blk, vec, blk], out_specs=[blk, blk, vec],
        compiler_params=_params("arbitrary"),
    )(y, zx, w.reshape(1, SSM_D_INNER), dout)
    return dy, dz, dw.reshape(SSM_D_INNER)


def _ssm_core_fwd(zx, conv_w, conv_b, dtb, alog, dsk, norm_w, name):
    act = _ssm_conv_fwd(zx, conv_w, conv_b, name + "_conv_fwd")
    y, st_in = _ssd_scan_fwd(act, zx, dtb, alog, dsk, name + "_scan_fwd")
    out = _gated_norm_fwd(y, zx, norm_w, name + "_gate_fwd")
    return out, (zx, conv_w, conv_b, dtb, alog, dsk, norm_w, act, y, st_in)


@functools.partial(jax.custom_vjp, nondiff_argnums=(7,))
def ssm_core(zx, conv_w, conv_b, dtb, alog, dsk, norm_w, name):
    return _ssm_core_fwd(zx, conv_w, conv_b, dtb, alog, dsk, norm_w, name)[0]


def _ssm_core_vjp_fwd(zx, conv_w, conv_b, dtb, alog, dsk, norm_w, name):
    return _ssm_core_fwd(zx, conv_w, conv_b, dtb, alog, dsk, norm_w, name)


def _ssm_core_vjp_bwd(name, res, dout):
    zx, conv_w, conv_b, dtb, alog, dsk, norm_w, act, y, st_in = res
    dy, dz, dnorm_w = _gated_norm_bwd(y, zx, norm_w, dout, name + "_gate_bwd")
    dact, ddt, ddtb, dalog, ddsk = _ssd_scan_bwd(act, zx, dtb, alog, dsk, st_in, dy, name + "_scan_bwd")
    dxbc, dconv_w, dconv_b = _ssm_conv_bwd(zx, conv_w, conv_b, dact, name + "_conv_bwd")
    dzx = jnp.concatenate([dz, dxbc, ddt], axis=1)
    return dzx, dconv_w, dconv_b, ddtb, dalog, ddsk, dnorm_w


ssm_core.defvjp(_ssm_core_vjp_fwd, _ssm_core_vjp_bwd)


def ssd_mixer_p(h, w_in, conv_w, conv_b, dtb, alog, dsk, norm_w, w_out, tag):
    zx = linear(h, w_in, F32, tag + "_in")
    return linear(ssm_core(zx, conv_w, conv_b, dtb, alog, dsk, norm_w, tag), w_out, F32, tag + "_out")


def trunk(w, x):
    for i in range(DEPTH):
        h = norm(x, w['mix_norm_w'][i], f"mixnorm{i}")
        j = i // 2
        if i % 2 == 0:
            x = x + attention_mixer_p(h, w['attn_w_qkv'][j], w['attn_w_o'][j], f"attn{j}")
        else:
            x = x + ssd_mixer_p(h, w['ssm_w_in'][j], w['ssm_conv_w'][j], w['ssm_conv_b'][j], w['ssm_dt_bias'][j],
                                w['ssm_a_log'][j], w['ssm_d'][j], w['ssm_norm_w'][j], w['ssm_w_out'][j], f"ssm{j}")
        h = norm(x, w['ffn_norm_w'][i], f"ffnnorm{i}")
        x = x + conv_ffn_p(h, w['ffn_w_up'][i], w['ffn_conv_w'][i], w['ffn_conv_b'][i], w['ffn_w_down'][i],
                           f"ffn{i}")
    return x


def local_step(w, x, target):
    final_w = w['final_norm_w']
    trunk_w = {n: a for n, a in w.items() if n != 'final_norm_w'}
    xf, pullback = jax.vjp(trunk, trunk_w, x)
    loss, dxf, dfinal = loss_head(xf, final_w, target, "loss_head")
    gw, gx = pullback(dxf)
    gw['final_norm_w'] = dfinal
    return loss, gw, gx


def _adam_math(w, g, m, v):
    m = ADAM_B1 * m + (1.0 - ADAM_B1) * g
    v = ADAM_B2 * v + (1.0 - ADAM_B2) * (g * g)
    m_hat = m / (1.0 - ADAM_B1 ** ADAM_STEP)
    v_hat = v / (1.0 - ADAM_B2 ** ADAM_STEP)
    delta = -ADAM_LR * (m_hat / (jnp.sqrt(v_hat) + ADAM_EPS) + ADAM_WD * w)
    return delta, m, v


def _adamw_sum8(pieces, w, m, v, name):
    r, c = w.shape
    tr = _pick(r, (256, 128, 64, 32, 16, 8))

    def body(p_ref, w_ref, m_ref, v_ref, g_out, d_out, m_out, v_out):
        g = p_ref[0].astype(F32)
        for j in range(1, N_DEV):
            g = g + p_ref[j].astype(F32)
        delta, mm, vv = _adam_math(w_ref[...], g, m_ref[...], v_ref[...])
        g_out[...] = g
        d_out[...] = delta
        m_out[...] = mm
        v_out[...] = vv

    blk = pl.BlockSpec((tr, c), lambda i: (i, 0))
    return pl.pallas_call(
        body, name=name,
        out_shape=[jax.ShapeDtypeStruct((r, c), F32)] * 4,
        grid=(r // tr,),
        in_specs=[pl.BlockSpec((N_DEV, tr, c), lambda i: (0, i, 0)), blk, blk, blk],
        out_specs=[blk] * 4,
        compiler_params=pltpu.CompilerParams(dimension_semantics=("parallel",)),
    )(pieces, w, m, v)


def _sum8(pieces, name):
    _, r, c = pieces.shape

    def body(p_ref, o_ref):
        g = p_ref[0]
        for j in range(1, N_DEV):
            g = g + p_ref[j]
        o_ref[...] = g

    return pl.pallas_call(
        body, name=name, out_shape=jax.ShapeDtypeStruct((r, c), F32),
        in_specs=[pl.BlockSpec(memory_space=pltpu.VMEM)], out_specs=pl.BlockSpec(memory_space=pltpu.VMEM),
    )(pieces)


def _adamw_plain(g, w, m, v, name):
    def body(g_ref, w_ref, m_ref, v_ref, d_out, m_out, v_out):
        delta, mm, vv = _adam_math(w_ref[...], g_ref[...], m_ref[...], v_ref[...])
        d_out[...] = delta
        m_out[...] = mm
        v_out[...] = vv

    vm = pl.BlockSpec(memory_space=pltpu.VMEM)
    return pl.pallas_call(
        body, name=name, out_shape=[jax.ShapeDtypeStruct(g.shape, F32)] * 3,
        in_specs=[vm] * 4, out_specs=[vm] * 3,
    )(g, w, m, v)


def _split8(full, axis):
    shp = full.shape
    t = full.reshape(shp[:axis] + (N_DEV, shp[axis] // N_DEV) + shp[axis + 1:])
    return jnp.moveaxis(t, axis, 0)


def _join8(parts, axis):
    t = jnp.moveaxis(parts, 0, axis)
    shp = t.shape
    return t.reshape(shp[:axis] + (shp[axis] * shp[axis + 1],) + shp[axis + 2:])


def _pack(arrs, lead, mult):
    flat = jnp.concatenate([a.reshape(a.shape[:lead] + (-1,)) for a in arrs], axis=-1)
    return _pad_rows(flat, mult)


def _unpack(buf, shapes, lead):
    flat = buf.reshape(buf.shape[:lead] + (-1,))
    out, off = [], 0
    for shp in shapes:
        n = math.prod(shp)
        out.append(flat[..., off:off + n].reshape(flat.shape[:lead] + tuple(shp)))
        off += n
    return out


def _own_shard(full, axis):
    size = full.shape[axis] // N_DEV
    return lax.dynamic_slice_in_dim(full, _my_index() * size, size, axis)


def kernel(x, mix_norm_w, attn_w_qkv, attn_w_o, ssm_w_in, ssm_conv_w, ssm_conv_b, ssm_dt_bias, ssm_a_log, ssm_d, ssm_norm_w, ssm_w_out, ffn_norm_w, ffn_w_up, ffn_conv_w, ffn_conv_b, ffn_w_down, final_norm_w, loss_target, m_mix_norm_w, m_attn_w_qkv, m_attn_w_o, m_ssm_w_in, m_ssm_conv_w, m_ssm_conv_b, m_ssm_dt_bias, m_ssm_a_log, m_ssm_d, m_ssm_norm_w, m_ssm_w_out, m_ffn_norm_w, m_ffn_w_up, m_ffn_conv_w, m_ffn_conv_b, m_ffn_w_down, m_final_norm_w, v_mix_norm_w, v_attn_w_qkv, v_attn_w_o, v_ssm_w_in, v_ssm_conv_w, v_ssm_conv_b, v_ssm_dt_bias, v_ssm_a_log, v_ssm_d, v_ssm_norm_w, v_ssm_w_out, v_ffn_norm_w, v_ffn_w_up, v_ffn_conv_w, v_ffn_conv_b, v_ffn_w_down, v_final_norm_w):
    args = dict(locals())
    w_sh = {n: args[n] for n in WEIGHT_NAMES}
    m_sh = {n: args["m_" + n] for n in WEIGHT_NAMES}
    v_sh = {n: args["v_" + n] for n in WEIGHT_NAMES}

    small_shapes = [w_sh[n].shape for n in SMALL_SHARDED]
    small = _exchange(_pack([w_sh[n] for n in SMALL_SHARDED], 0, 8), True, "gather_small")
    full = {n: w_sh[n] for n in SMALL if SHARD_AXIS[n] is None}
    for n in BIG:
        full[n] = _join8(_exchange(w_sh[n].astype(BF16), True, "gather_" + n), SHARD_AXIS[n])
    for n, parts in zip(SMALL_SHARDED, _unpack(small, small_shapes, 1)):
        full[n] = _join8(parts, SHARD_AXIS[n])
    full['ssm_w_in'] = jnp.pad(full['ssm_w_in'], ((0, 0), (0, 0), (0, SSM_IN_PAD - SSM_IN_WIDTH)))

    loss, gw, gx = local_step(full, x[0], loss_target[0])
    gw['ssm_w_in'] = gw['ssm_w_in'][:, :, :SSM_IN_WIDTH]
    loss = lax.psum(loss, ("x", "y", "c"))

    grads, deltas, new_m, new_v = {}, {}, {}, {}
    for n in BIG:
        pieces = _exchange(_split8(gw[n], SHARD_AXIS[n]), False, "scatter_" + n)
        shp = w_sh[n].shape
        two_d = (shp[0] * shp[1], shp[2])
        outs = _adamw_sum8(pieces.reshape((N_DEV,) + two_d), w_sh[n].reshape(two_d), m_sh[n].reshape(two_d),
                           v_sh[n].reshape(two_d), "adamw_" + n)
        grads[n], deltas[n], new_m[n], new_v[n] = [o.reshape(shp) for o in outs]

    small_full_shapes = [gw[n].shape for n in SMALL]
    gsmall = _exchange(_pack([gw[n] for n in SMALL], 0, 8), True, "gather_small_grads")
    gsmall = _unpack(_sum8(gsmall, "sum_small_grads"), small_full_shapes, 0)
    for n, g in zip(SMALL, gsmall):
        grads[n] = g if SHARD_AXIS[n] is None else _own_shard(g, SHARD_AXIS[n])
    shapes = [w_sh[n].shape for n in SMALL]
    outs = _adamw_plain(*[_pack([d[n] for n in SMALL], 0, 8) for d in (grads, w_sh, m_sh, v_sh)], "adamw_small")
    for d, buf in zip((deltas, new_m, new_v), outs):
        for n, a in zip(SMALL, _unpack(buf, shapes, 0)):
            d[n] = a

    return (loss, gx[None], *[grads[n] for n in WEIGHT_NAMES], *[deltas[n] for n in WEIGHT_NAMES],
            *[new_m[n] for n in WEIGHT_NAMES], *[new_v[n] for n in WEIGHT_NAMES])
```

```python
import functools
import math

import jax
import jax.numpy as jnp
from jax import lax
from jax.experimental import pallas as pl
from jax.experimental.pallas import tpu as pltpu

F32 = jnp.float32
BF16 = jnp.bfloat16
N_DEV = 8
MESH_ID = pl.DeviceIdType.MESH

D_MODEL = 1024
DEPTH = 4
ATTN_HEADS = 8
ATTN_HEAD_DIM = 128
ATTN_DILATIONS = (1, 4, 16)
ATTN_STEPS = (128, 128, 128)
N_ATTN_GROUPS = 3
ATTN_BLOCK = 128
ROPE_THETA = 500000.0
ROPE_DIM = 32
ATTN_OUT_WIDTH = 1024
SSM_D_INNER = 2048
SSM_HEAD_DIM = 64
SSM_HEADS = 32
SSM_STATE = 128
SSM_GROUPS = 8
SSM_CHUNK = 128
SSM_CONV_DIM = 4096
SSM_IN_WIDTH = 6176
SSM_IN_PAD = 6272
D_FF = 2816
NORM_EPS = 1e-5
ADAM_LR = 0.001
ADAM_B1 = 0.9
ADAM_B2 = 0.999
ADAM_EPS = 1e-08
ADAM_WD = 0.01
ADAM_STEP = 10

WEIGHT_NAMES = ['mix_norm_w', 'attn_w_qkv', 'attn_w_o', 'ssm_w_in', 'ssm_conv_w', 'ssm_conv_b', 'ssm_dt_bias',
                'ssm_a_log', 'ssm_d', 'ssm_norm_w', 'ssm_w_out', 'ffn_norm_w', 'ffn_w_up', 'ffn_conv_w',
                'ffn_conv_b', 'ffn_w_down', 'final_norm_w']
SHARD_AXIS = {'mix_norm_w': None, 'attn_w_qkv': 2, 'attn_w_o': 1, 'ssm_w_in': 2, 'ssm_conv_w': 2, 'ssm_conv_b': 1,
              'ssm_dt_bias': None, 'ssm_a_log': None, 'ssm_d': None, 'ssm_norm_w': 1, 'ssm_w_out': 1,
              'ffn_norm_w': None, 'ffn_w_up': 2, 'ffn_conv_w': 2, 'ffn_conv_b': None, 'ffn_w_down': 1,
              'final_norm_w': None}
BIG = ['attn_w_qkv', 'attn_w_o', 'ssm_w_in', 'ssm_w_out', 'ffn_w_up', 'ffn_w_down']
SMALL = [n for n in WEIGHT_NAMES if n not in BIG]
SMALL_SHARDED = [n for n in SMALL if SHARD_AXIS[n] is not None]
LANES = 1024


def _my_index():
    return 4 * lax.axis_index("x") + 2 * lax.axis_index("y") + lax.axis_index("c")


def _peer(k):
    x, y, c = lax.axis_index("x"), lax.axis_index("y"), lax.axis_index("c")
    return (x ^ ((k >> 2) & 1), y ^ ((k >> 1) & 1), c ^ (k & 1))


def _exchange(src, gather, name):
    def body(src_ref, out_ref, send_sems, recv_sems, local_sem):
        start, wait = _exchange_copies(src_ref, out_ref, send_sems, recv_sems, local_sem, gather)
        start()
        wait()

    return pl.pallas_call(
        body, name=name,
        out_shape=_exchange_out(src, gather),
        in_specs=[pl.BlockSpec(memory_space=pl.ANY)],
        out_specs=pl.BlockSpec(memory_space=pl.ANY),
        scratch_shapes=list(EXCHANGE_SEMS),
    )(src)


EXCHANGE_SEMS = (pltpu.SemaphoreType.DMA((N_DEV - 1,)), pltpu.SemaphoreType.DMA((N_DEV - 1,)),
                 pltpu.SemaphoreType.DMA)


def _exchange_out(src, gather):
    return jax.ShapeDtypeStruct((N_DEV,) + (src.shape if gather else src.shape[1:]), src.dtype)


def _exchange_copies(src_ref, out_ref, send_sems, recv_sems, local_sem, gather):
    me = _my_index()

    def piece(j):
        return src_ref if gather else src_ref.at[j]

    def remote(k, slab):
        return pltpu.make_async_remote_copy(
            src_ref=piece(me ^ k), dst_ref=out_ref.at[slab], send_sem=send_sems.at[k - 1],
            recv_sem=recv_sems.at[k - 1], device_id=_peer(k), device_id_type=MESH_ID)

    mine = pltpu.make_async_copy(piece(me), out_ref.at[me], local_sem)
    sends = [remote(k, me) for k in range(1, N_DEV)]
    arrivals = [remote(k, me ^ k) for k in range(1, N_DEV)]

    def start():
        mine.start()
        for cp in sends:
            cp.start()

    def wait():
        for cp in arrivals:
            cp.wait_recv()
        for cp in sends:
            cp.wait_send()
        mine.wait()

    return start, wait


def _pad_rows(flat, mult):
    n = flat.shape[-1]
    rows = -(-n // (LANES * mult)) * mult
    pad = rows * LANES - n
    flat = jnp.pad(flat, [(0, 0)] * (flat.ndim - 1) + [(0, pad)])
    return flat.reshape(flat.shape[:-1] + (rows, LANES))


def _pick(n, cands):
    for c in cands:
        if n % c == 0:
            return c
    return n


def _matmul(a, b, ta, tb, out_dtype, name, rider=None):
    (m, k) = (a.shape[1], a.shape[0]) if ta else a.shape
    (k2, n) = (b.shape[1], b.shape[0]) if tb else b.shape
    assert k == k2, (a.shape, b.shape, ta, tb)
    tm = _pick(m, (1024, 1408, 512, 256, 128))
    tn = _pick(n, (1536, 1408, 1024, 896, 512, 384, 256, 128))
    tk = _pick(k, (1024, 1408, 896, 512, 256, 128)) if k > 2816 else k
    nk = k // tk
    dims = (((0 if ta else 1,), (1 if tb else 0,)), ((), ()))

    grid = (n // tn, m // tm, nk)

    def body(*refs):
        if rider is None:
            a_ref, b_ref, o_ref, *scratch = refs
        else:
            a_ref, b_ref, src_ref, o_ref, moved_ref, *scratch = refs
            start, wait = _exchange_copies(src_ref, moved_ref, *scratch[-3:], rider[1])
            scratch = scratch[:-3]
            at = [pl.program_id(ax) for ax in range(3)]
            pl.when(functools.reduce(jnp.logical_and, [p == 0 for p in at]))(start)
        part = lax.dot_general(a_ref[...].astype(BF16), b_ref[...].astype(BF16), dims,
                               preferred_element_type=F32)
        if nk == 1:
            o_ref[...] = part.astype(o_ref.dtype)
        else:
            acc_ref, = scratch
            kk = pl.program_id(2)

            @pl.when(kk == 0)
            def _():
                acc_ref[...] = part

            @pl.when(kk > 0)
            def _():
                acc_ref[...] += part

            @pl.when(kk == nk - 1)
            def _():
                o_ref[...] = acc_ref[...].astype(o_ref.dtype)
        if rider is not None:
            pl.when(functools.reduce(jnp.logical_and, [p == g - 1 for p, g in zip(at, grid)]))(wait)

    a_spec = (pl.BlockSpec((tk, tm), lambda j, i, kk: (kk, i)) if ta
              else pl.BlockSpec((tm, tk), lambda j, i, kk: (i, kk)))
    b_spec = (pl.BlockSpec((tn, tk), lambda j, i, kk: (j, kk)) if tb
              else pl.BlockSpec((tk, tn), lambda j, i, kk: (kk, j)))
    any_spec = pl.BlockSpec(memory_space=pl.ANY)
    out_shape = [jax.ShapeDtypeStruct((m, n), out_dtype)]
    out_specs = [pl.BlockSpec((tm, tn), lambda j, i, kk: (i, j))]
    scratch = [] if nk == 1 else [pltpu.VMEM((tm, tn), F32)]
    if rider is None:
        extra_in, operands, sem = [], (a, b), ("parallel", "parallel", "arbitrary")
    else:
        extra_in, operands, sem = [any_spec], (a, b, rider[0]), ("arbitrary",) * 3
        out_shape.append(_exchange_out(*rider))
        out_specs.append(any_spec)
        scratch += list(EXCHANGE_SEMS)
    outs = pl.pallas_call(
        body, name=name, out_shape=out_shape, grid=grid,
        in_specs=[a_spec, b_spec] + extra_in, out_specs=out_specs, scratch_shapes=scratch,
        compiler_params=_params(*sem),
    )(*operands)
    return outs[0] if rider is None else tuple(outs)


@functools.partial(jax.custom_vjp, nondiff_argnums=(2, 3))
def linear(a, w, out_dtype, name):
    return _matmul(a, w, False, False, out_dtype, name + "_fwd")


def _linear_fwd(a, w, out_dtype, name):
    return _matmul(a, w, False, False, out_dtype, name + "_fwd"), (a, w)


def _linear_bwd(out_dtype, name, res, dy):
    a, w = res
    da = _matmul(dy, w, False, True, a.dtype, name + "_da")
    dw = _matmul(a, dy, True, False, w.dtype, name + "_dw")
    return da, dw


linear.defvjp(_linear_fwd, _linear_bwd)


def _sum_pieces(pieces, name):
    shape = pieces.shape[1:]
    c = shape[-1]
    r = math.prod(shape[:-1])
    tr = _pick(r, (512, 256, 128, 64, 32, 16, 8))

    def body(p_ref, o_ref):
        g = p_ref[0].astype(F32)
        for j in range(1, N_DEV):
            g = g + p_ref[j].astype(F32)
        o_ref[...] = g

    return pl.pallas_call(
        body, name=name, out_shape=jax.ShapeDtypeStruct((r, c), F32), grid=(r // tr,),
        in_specs=[pl.BlockSpec((N_DEV, tr, c), lambda i: (0, i, 0))],
        out_specs=pl.BlockSpec((tr, c), lambda i: (i, 0)),
        compiler_params=_params("parallel"),
    )(pieces.reshape(N_DEV, r, c)).reshape(shape)


@functools.partial(jax.custom_vjp, nondiff_argnums=(1,))
def gather_x(shard, name):
    return _exchange(shard.astype(BF16), True, name)


def _gather_x_fwd(shard, name):
    return _exchange(shard.astype(BF16), True, name), None


def _gather_x_bwd(name, _, dparts):
    return (_sum_pieces(_exchange(dparts, False, name + "_back"), name + "_sum"),)


gather_x.defvjp(_gather_x_fwd, _gather_x_bwd)


@functools.partial(jax.custom_vjp, nondiff_argnums=(3, 4))
def linear_x(a, w, shard, out_dtype, name):
    return _matmul(a, w, False, False, out_dtype, name + "_fwd", rider=(shard.astype(BF16), True))


def _linear_x_fwd(a, w, shard, out_dtype, name):
    return _matmul(a, w, False, False, out_dtype, name + "_fwd", rider=(shard.astype(BF16), True)), (a, w)


def _linear_x_bwd(out_dtype, name, res, cts):
    a, w = res
    dy, dparts = cts
    da, moved = _matmul(dy, w, False, True, a.dtype, name + "_da", rider=(dparts, False))
    dw = _matmul(a, dy, True, False, w.dtype, name + "_dw")
    return da, dw, _sum_pieces(moved, name + "_sum")


linear_x.defvjp(_linear_x_fwd, _linear_x_bwd)


def rmsnorm(x, w):
    y = x * lax.rsqrt(jnp.mean(x * x, axis=-1, keepdims=True) + NORM_EPS)
    return y * w


def causal_depthwise_conv(x, w, b):
    k = w.shape[0]
    y = lax.conv_general_dilated(
        x[None], w[:, None, :], window_strides=(1,), padding=[(k - 1, 0)],
        dimension_numbers=("NWC", "WIO", "NWC"), feature_group_count=x.shape[-1])[0]
    return y + b


def rope_tables(seq):
    pos = jnp.arange(seq, dtype=F32)
    inv_freq = ROPE_THETA ** (-jnp.arange(0, ROPE_DIM, 2, dtype=F32) / ROPE_DIM)
    ang = pos[:, None] * inv_freq[None, :]
    return jnp.cos(ang), jnp.sin(ang)


def apply_partial_rope(t, cos, sin):
    half = ROPE_DIM // 2
    c = cos[:, None, None, :]
    s = sin[:, None, None, :]
    x1 = t[..., :half]
    x2 = t[..., half:ROPE_DIM]
    return jnp.concatenate([x1 * c - x2 * s, x2 * c + x1 * s, t[..., ROPE_DIM:]], axis=-1)


def dilated_window_attention(q, k, v, dilation, steps):
    s, h, hd = q.shape
    length = s // dilation
    nb = -(-length // ATTN_BLOCK)
    lp = nb * ATTN_BLOCK

    def to_strided(t):
        t = t.reshape(length, dilation, h, hd).transpose(1, 2, 0, 3)
        t = jnp.pad(t, ((0, 0), (0, 0), (0, lp - length), (0, 0)))
        return t.reshape(dilation, h, nb, ATTN_BLOCK, hd)

    def with_prev(t):
        prev = jnp.pad(t, ((0, 0), (0, 0), (1, 0), (0, 0), (0, 0)))[:, :, :-1]
        return jnp.concatenate([prev, t], axis=-2)

    qb = to_strided(q)
    kk = with_prev(to_strided(k))
    vv = with_prev(to_strided(v))
    scores = jnp.einsum("rhnqe,rhnke->rhnqk", qb, kk) * (hd ** -0.5)
    n_idx = jnp.arange(nb)[:, None, None]
    i_idx = jnp.arange(ATTN_BLOCK)[None, :, None]
    j_idx = jnp.arange(2 * ATTN_BLOCK)[None, None, :]
    delta = ATTN_BLOCK + i_idx - j_idx
    key_pos = (n_idx - 1) * ATTN_BLOCK + j_idx
    allowed = (delta >= 0) & (delta <= steps) & (key_pos >= 0)
    scores = jnp.where(allowed, scores, -jnp.inf)
    m = jnp.max(scores, axis=-1, keepdims=True)
    p = jnp.exp(scores - m)
    den = jnp.sum(p, axis=-1, keepdims=True)
    o = jnp.einsum("rhnqk,rhnke->rhnqe", p, vv) / den
    lse = (m + jnp.log(den))[..., 0]
    o = o.reshape(dilation, h, lp, hd)[:, :, :length]
    o = o.transpose(2, 0, 1, 3).reshape(s, h, hd)
    lse = lse.reshape(dilation, h, lp)[:, :, :length]
    lse = lse.transpose(2, 0, 1).reshape(s, h)
    return o, lse


def attention_mixer(h, w_qkv, w_o, cos, sin, tag):
    s = h.shape[0]
    qkv = linear(h.astype(BF16), w_qkv, F32, tag + "_qkv")
    qkv = qkv.reshape(s, N_ATTN_GROUPS, 3, ATTN_HEADS, ATTN_HEAD_DIM)
    q = apply_partial_rope(qkv[:, :, 0], cos, sin)
    k = apply_partial_rope(qkv[:, :, 1], cos, sin)
    v = qkv[:, :, 2]
    outs, lses = [], []
    for g in range(N_ATTN_GROUPS):
        o_g, lse_g = dilated_window_attention(q[:, g], k[:, g], v[:, g], ATTN_DILATIONS[g], ATTN_STEPS[g])
        outs.append(o_g)
        lses.append(lse_g)
    wts = jax.nn.softmax(jnp.stack(lses, axis=1), axis=1)
    o = jnp.einsum("sgh,sghe->she", wts, jnp.stack(outs, axis=1))
    return linear(o.reshape(s, ATTN_OUT_WIDTH).astype(BF16), w_o, F32, tag + "_o")


def ssd_chunked_scan(x, dt, a, bm, cm):
    s, h, p = x.shape
    g, n = bm.shape[1], bm.shape[2]
    r = h // g
    c = s // SSM_CHUNK
    q = SSM_CHUNK
    x = x.reshape(c, q, g, r, p)
    dt = dt.reshape(c, q, g, r)
    bm = bm.reshape(c, q, g, n)
    cm = cm.reshape(c, q, g, n)
    a_dt = dt * a.reshape(g, r)
    a_cs = jnp.cumsum(a_dt, axis=1)
    xdt = x * dt[..., None]
    seg = a_cs[:, :, None] - a_cs[:, None, :]
    causal = jnp.tril(jnp.ones((q, q), dtype=bool))[:, :, None, None]
    lmat = jnp.exp(jnp.where(causal, seg, -jnp.inf))
    cb = jnp.einsum("cign,cjgn->cijg", cm, bm)
    y_diag = jnp.einsum("cijgr,cjgrp->cigrp", cb[..., None] * lmat, xdt)
    decay = jnp.exp(a_cs[:, -1:] - a_cs)
    states = jnp.einsum("cjgn,cjgr,cjgrp->cgrpn", bm, decay, xdt)
    chunk_decay = jnp.exp(a_cs[:, -1])

    def step(state, inp):
        st_c, dec_c = inp
        return state * dec_c[..., None, None] + st_c, state

    init = jnp.zeros((g, r, p, n), dtype=x.dtype)
    _, prev = lax.scan(step, init, (states, chunk_decay))
    y_off = jnp.einsum("cign,cgrpn->cigrp", cm, prev) * jnp.exp(a_cs)[..., None]
    return (y_diag + y_off).reshape(s, h, p)


def gated_group_rmsnorm(y, z, w, groups):
    g = y * jax.nn.silu(z)
    shp = g.shape
    g = g.reshape(shp[:-1] + (groups, shp[-1] // groups))
    g = g * lax.rsqrt(jnp.mean(g * g, axis=-1, keepdims=True) + NORM_EPS)
    return g.reshape(shp) * w


def ssd_mixer(h, w_in, conv_w, conv_b, dt_bias, a_log, d_skip, norm_w, w_out, tag):
    s = h.shape[0]
    gn = SSM_GROUPS * SSM_STATE
    zxbcdt = linear(h.astype(BF16), w_in, F32, tag + "_in")
    z = zxbcdt[:, :SSM_D_INNER]
    xbc = zxbcdt[:, SSM_D_INNER:SSM_D_INNER + SSM_CONV_DIM]
    dt_raw = zxbcdt[:, SSM_D_INNER + SSM_CONV_DIM:SSM_IN_WIDTH]
    xbc = jax.nn.silu(causal_depthwise_conv(xbc, conv_w, conv_b))
    xs = xbc[:, :SSM_D_INNER].reshape(s, SSM_HEADS, SSM_HEAD_DIM)
    bm = xbc[:, SSM_D_INNER:SSM_D_INNER + gn].reshape(s, SSM_GROUPS, SSM_STATE)
    cm = xbc[:, SSM_D_INNER + gn:].reshape(s, SSM_GROUPS, SSM_STATE)
    dt = jax.nn.softplus(dt_raw + dt_bias)
    a = -jnp.exp(a_log)
    y = ssd_chunked_scan(xs, dt, a, bm, cm)
    y = y + d_skip[:, None] * xs
    y = gated_group_rmsnorm(y.reshape(s, SSM_D_INNER), z, norm_w, SSM_GROUPS)
    return linear(y.astype(BF16), w_out, F32, tag + "_out")


def conv_ffn(h, w_up, conv_w, conv_b, w_down, tag):
    u = causal_depthwise_conv(linear(h.astype(BF16), w_up, F32, tag + "_up"), conv_w, conv_b)
    gate, up = u[:, :D_FF], u[:, D_FF:]
    return linear((jax.nn.silu(gate) * up).astype(BF16), w_down, F32, tag + "_down")


VMEM_LIMIT = 56 * 1024 * 1024
SUBLANES = 8


def _params(*sem):
    return pltpu.CompilerParams(dimension_semantics=sem, vmem_limit_bytes=VMEM_LIMIT)


def _sigmoid(x):
    return 0.5 * jnp.tanh(0.5 * x) + 0.5


def _rstd(xv):
    return lax.rsqrt(jnp.mean(xv * xv, axis=-1, keepdims=True) + NORM_EPS)


def _accumulate(ref, part, first):
    @pl.when(first)
    def _():
        ref[...] = part

    @pl.when(jnp.logical_not(first))
    def _():
        ref[...] += part


def _norm_fwd(x, w, name):
    s, d = x.shape
    tm = _pick(s, (512, 256, 128))

    def body(x_ref, w_ref, h_ref):
        xv = x_ref[...]
        h_ref[...] = (xv * _rstd(xv) * w_ref[...]).astype(h_ref.dtype)

    return pl.pallas_call(
        body, name=name, out_shape=jax.ShapeDtypeStruct((s, d), BF16), grid=(s // tm,),
        in_specs=[pl.BlockSpec((tm, d), lambda i: (i, 0)), pl.BlockSpec((1, d), lambda i: (0, 0))],
        out_specs=pl.BlockSpec((tm, d), lambda i: (i, 0)), compiler_params=_params("parallel"),
    )(x, w.reshape(1, d))


def _norm_bwd(x, w, dh, name):
    s, d = x.shape
    tm = _pick(s, (512, 256, 128))

    def body(x_ref, w_ref, dh_ref, dx_ref, dw_ref):
        xv = x_ref[...]
        r = _rstd(xv)
        y = xv * r
        dhv = dh_ref[...].astype(F32)
        dy = dhv * w_ref[...]
        dx_ref[...] = r * (dy - y * jnp.mean(dy * y, axis=-1, keepdims=True))
        _accumulate(dw_ref, jnp.sum(dhv * y, axis=0, keepdims=True), pl.program_id(0) == 0)

    dx, dw = pl.pallas_call(
        body, name=name,
        out_shape=[jax.ShapeDtypeStruct((s, d), F32), jax.ShapeDtypeStruct((1, d), F32)], grid=(s // tm,),
        in_specs=[pl.BlockSpec((tm, d), lambda i: (i, 0)), pl.BlockSpec((1, d), lambda i: (0, 0)),
                  pl.BlockSpec((tm, d), lambda i: (i, 0))],
        out_specs=[pl.BlockSpec((tm, d), lambda i: (i, 0)), pl.BlockSpec((1, d), lambda i: (0, 0))],
        compiler_params=_params("arbitrary"),
    )(x, w.reshape(1, d), dh)
    return dx, dw.reshape(d)


@functools.partial(jax.custom_vjp, nondiff_argnums=(2,))
def norm(x, w, name):
    return _norm_fwd(x, w, name + "_fwd")


def _norm_vjp_fwd(x, w, name):
    return _norm_fwd(x, w, name + "_fwd"), (x, w)


def _norm_vjp_bwd(name, res, dh):
    x, w = res
    return _norm_bwd(x, w, dh, name + "_bwd")


norm.defvjp(_norm_vjp_fwd, _norm_vjp_bwd)


def loss_head(x, w, target, name):
    s, d = x.shape
    tm = _pick(s, (512, 256, 128))

    def body(x_ref, w_ref, t_ref, loss_ref, dx_ref, dw_ref):
        first = pl.program_id(0) == 0
        xv = x_ref[...]
        r = _rstd(xv)
        y = xv * r
        err = y * w_ref[...] - t_ref[...]
        part = 0.5 * jnp.sum(jnp.sum(err * err, axis=-1, keepdims=True), axis=0, keepdims=True) / d
        _accumulate(loss_ref, jnp.broadcast_to(part, loss_ref.shape), first)
        dout = err / d
        dy = dout * w_ref[...]
        dx_ref[...] = r * (dy - y * jnp.mean(dy * y, axis=-1, keepdims=True))
        _accumulate(dw_ref, jnp.sum(dout * y, axis=0, keepdims=True), first)

    row = pl.BlockSpec((tm, d), lambda i: (i, 0))
    vec = pl.BlockSpec((1, d), lambda i: (0, 0))
    loss, dx, dw = pl.pallas_call(
        body, name=name,
        out_shape=[jax.ShapeDtypeStruct((1, 128), F32), jax.ShapeDtypeStruct((s, d), F32),
                   jax.ShapeDtypeStruct((1, d), F32)],
        grid=(s // tm,), in_specs=[row, vec, row],
        out_specs=[pl.BlockSpec((1, 128), lambda i: (0, 0)), row, vec],
        compiler_params=_params("arbitrary"),
    )(x, w.reshape(1, d), target)
    return loss[0, 0], dx, dw.reshape(d)


FFN_ROWS = 256
FFN_COLS = 256


def _shift_down(cur, halo, k):
    out = pltpu.roll(cur, k, axis=0)
    row = lax.broadcasted_iota(jnp.int32, cur.shape, 0)
    for j in range(k):
        out = jnp.where(row == j, halo[SUBLANES - k + j:SUBLANES - k + j + 1, :], out)
    return out


def _shift_up(cur, nxt, k):
    n = cur.shape[0]
    out = pltpu.roll(cur, n - k, axis=0)
    row = lax.broadcasted_iota(jnp.int32, cur.shape, 0)
    for j in range(k):
        out = jnp.where(row == n - k + j, nxt[j:j + 1, :], out)
    return out


def _conv_taps(cur, halo, ntaps):
    return [_shift_down(cur, halo, ntaps - 1 - k) if k < ntaps - 1 else cur for k in range(ntaps)]


def _pad_taps(conv_w):
    return jnp.pad(conv_w, ((0, SUBLANES - conv_w.shape[0]), (0, 0)))


def _ffn_mid_fwd(u0, conv_w, conv_b, name):
    s, width = u0.shape
    half = width // 2
    tm = _pick(s, (FFN_ROWS, 128))
    per = tm // SUBLANES

    def body(w_ref, b_ref, u_ref, halo_ref, a_ref):
        keep = pl.program_id(0) > 0
        for c0 in range(0, half, FFN_COLS):
            vals = []
            for base in (c0, half + c0):
                cols = slice(base, base + FFN_COLS)
                halo = jnp.where(keep, halo_ref[:, cols], 0.0)
                taps = _conv_taps(u_ref[:, cols], halo, 3)
                vals.append(sum(w_ref[k:k + 1, cols] * taps[k] for k in range(3)) + b_ref[:, cols])
            gate, up = vals
            a_ref[:, c0:c0 + FFN_COLS] = (gate * _sigmoid(gate) * up).astype(a_ref.dtype)

    return pl.pallas_call(
        body, name=name, out_shape=jax.ShapeDtypeStruct((s, half), BF16), grid=(s // tm,),
        in_specs=[pl.BlockSpec((SUBLANES, width), lambda i: (0, 0)), pl.BlockSpec((1, width), lambda i: (0, 0)),
                  pl.BlockSpec((tm, width), lambda i: (i, 0)),
                  pl.BlockSpec((SUBLANES, width), lambda i: (jnp.maximum(i * per - 1, 0), 0))],
        out_specs=pl.BlockSpec((tm, half), lambda i: (i, 0)), compiler_params=_params("parallel"),
    )(_pad_taps(conv_w), conv_b.reshape(1, width), u0, u0)


def _ffn_mid_bwd(u0, conv_w, conv_b, da, name):
    s, width = u0.shape
    half = width // 2
    tm = _pick(s, (FFN_ROWS, 128))
    per = tm // SUBLANES
    nt = s // tm

    def body(w_ref, b_ref, u_ref, halo_ref, da_ref, du0_ref, dw_ref, db_ref, carry_ref):
        step = pl.program_id(0)
        first = step == 0
        keep = step < nt - 1
        for c0 in range(0, half, FFN_COLS):
            dav = da_ref[:, c0:c0 + FFN_COLS].astype(F32)
            taps, vals = [], []
            for base in (c0, half + c0):
                cols = slice(base, base + FFN_COLS)
                halo = jnp.where(keep, halo_ref[:, cols], 0.0)
                tp = _conv_taps(u_ref[:, cols], halo, 3)
                taps.append(tp)
                vals.append(sum(w_ref[k:k + 1, cols] * tp[k] for k in range(3)) + b_ref[:, cols])
            gate, up = vals
            sig = _sigmoid(gate)
            dus = [dav * up * sig * (1.0 + gate * (1.0 - sig)), dav * gate * sig]
            for base, tp, du in zip((c0, half + c0), taps, dus):
                cols = slice(base, base + FFN_COLS)
                nxt = jnp.where(first, 0.0, carry_ref[:, cols])
                du0 = (w_ref[2:3, cols] * du + w_ref[1:2, cols] * _shift_up(du, nxt, 1)
                       + w_ref[0:1, cols] * _shift_up(du, nxt, 2))
                du0_ref[:, cols] = du0.astype(du0_ref.dtype)
                carry_ref[:, cols] = du[0:SUBLANES, :]
                dwp = jnp.concatenate([jnp.sum(du * tp[k], axis=0, keepdims=True) for k in range(3)]
                                      + [jnp.zeros((SUBLANES - 3, FFN_COLS), F32)], axis=0)
                dbp = jnp.sum(du, axis=0, keepdims=True)

                @pl.when(first)
                def _():
                    dw_ref[:, cols] = dwp
                    db_ref[:, cols] = dbp

                @pl.when(jnp.logical_not(first))
                def _():
                    dw_ref[:, cols] += dwp
                    db_ref[:, cols] += dbp

    rev = lambda i: nt - 1 - i
    du0, dw, db = pl.pallas_call(
        body, name=name,
        out_shape=[jax.ShapeDtypeStruct((s, width), BF16), jax.ShapeDtypeStruct((SUBLANES, width), F32),
                   jax.ShapeDtypeStruct((1, width), F32)],
        grid=(nt,),
        in_specs=[pl.BlockSpec((SUBLANES, width), lambda i: (0, 0)), pl.BlockSpec((1, width), lambda i: (0, 0)),
                  pl.BlockSpec((tm, width), lambda i: (rev(i), 0)),
                  pl.BlockSpec((SUBLANES, width), lambda i: (jnp.maximum(rev(i) * per - 1, 0), 0)),
                  pl.BlockSpec((tm, half), lambda i: (rev(i), 0))],
        out_specs=[pl.BlockSpec((tm, width), lambda i: (rev(i), 0)),
                   pl.BlockSpec((SUBLANES, width), lambda i: (0, 0)), pl.BlockSpec((1, width), lambda i: (0, 0))],
        scratch_shapes=[pltpu.VMEM((SUBLANES, width), F32)],
        compiler_params=_params("arbitrary"),
    )(_pad_taps(conv_w), conv_b.reshape(1, width), u0, u0, da)
    return du0, dw[:3], db.reshape(width)


@functools.partial(jax.custom_vjp, nondiff_argnums=(3,))
def ffn_mid(u0, conv_w, conv_b, name):
    return _ffn_mid_fwd(u0, conv_w, conv_b, name + "_fwd")


def _ffn_mid_vjp_fwd(u0, conv_w, conv_b, name):
    return _ffn_mid_fwd(u0, conv_w, conv_b, name + "_fwd"), (u0, conv_w, conv_b)


def _ffn_mid_vjp_bwd(name, res, da):
    u0, conv_w, conv_b = res
    return _ffn_mid_bwd(u0, conv_w, conv_b, da, name + "_bwd")


ffn_mid.defvjp(_ffn_mid_vjp_fwd, _ffn_mid_vjp_bwd)


def conv_ffn_p(h, w_up, conv_w, conv_b, w_down, tag):
    u0 = linear(h, w_up, F32, tag + "_up")
    return linear(ffn_mid(u0, conv_w, conv_b, tag + "_mid"), w_down, F32, tag + "_down")


NEG = -1e30
HEAD_SLICES = [slice(hh * ATTN_HEAD_DIM, (hh + 1) * ATTN_HEAD_DIM) for hh in range(ATTN_HEADS)]
ATTN_SCALE = ATTN_HEAD_DIM ** -0.5
QKV_GROUP = 3 * ATTN_OUT_WIDTH


def rope_table(seq):
    cos, sin = rope_tables(seq)
    half = ROPE_DIM // 2
    ones = jnp.ones((seq, ATTN_HEAD_DIM - ROPE_DIM), F32)
    zero = lambda n: jnp.zeros((seq, n), F32)
    return jnp.concatenate([cos, cos, ones, -sin, zero(ATTN_HEAD_DIM - half),
                            zero(half), sin, zero(ATTN_HEAD_DIM - ROPE_DIM)], axis=1)


def _rope(t, tab, sign):
    half = ROPE_DIM // 2
    return t * tab[:, 0:128] + sign * (pltpu.roll(t, ATTN_HEAD_DIM - half, axis=1) * tab[:, 128:256]
                                       + pltpu.roll(t, half, axis=1) * tab[:, 256:384])


def _to_dilated(a, d):
    s = a.shape[0]
    return a if d == 1 else a.reshape(s // d, d, -1).transpose(1, 0, 2).reshape(s, -1)


def _from_dilated(a, d):
    s = a.shape[0]
    return a if d == 1 else a.reshape(d, s // d, -1).transpose(1, 0, 2).reshape(s, -1)


def _rope_qk(qkv, tab, name):
    s = qkv.shape[0]
    tm = _pick(s, (512, 256, 128))

    def body(t_ref, x_ref, o_ref):
        for hs in HEAD_SLICES:
            o_ref[:, hs] = _rope(x_ref[:, hs].astype(F32), t_ref[...], 1.0).astype(o_ref.dtype)

    blk = pl.BlockSpec((tm, ATTN_OUT_WIDTH), lambda i, c: (i, c))
    return pl.pallas_call(
        body, name=name, out_shape=jax.ShapeDtypeStruct(qkv.shape, qkv.dtype), grid=(s // tm, 2),
        in_specs=[pl.BlockSpec((tm, 384), lambda i, c: (i, 0)), blk], out_specs=blk,
        input_output_aliases={1: 0}, compiler_params=_params("parallel", "parallel"),
    )(tab, qkv)


def _dot_nt(a, b):
    return lax.dot_general(a, b, (((1,), (1,)), ((), ())), preferred_element_type=F32)


def _dot_tn(a, b):
    return lax.dot_general(a, b, (((0,), (0,)), ((), ())), preferred_element_type=F32)


def _dot(a, b):
    return jnp.dot(a, b, preferred_element_type=F32)


def _window_mask(has_prev):
    ii = lax.broadcasted_iota(jnp.int32, (ATTN_BLOCK, 2 * ATTN_BLOCK), 0)
    jj = lax.broadcasted_iota(jnp.int32, (ATTN_BLOCK, 2 * ATTN_BLOCK), 1)
    in_window = jnp.logical_and(jj >= ii, jj <= ii + ATTN_BLOCK)
    return jnp.logical_and(in_window, jnp.logical_or(jj >= ATTN_BLOCK, has_prev))


def _both(prev_ref, cur_ref, hs):
    return jnp.concatenate([prev_ref[:, hs], cur_ref[:, hs]], axis=0)


def _attn_group_fwd(qkv, d, name):
    s = qkv.shape[0]
    nb = s // d // ATTN_BLOCK

    def body(q_ref, kc_ref, kp_ref, vc_ref, vp_ref, o_ref, lse_ref):
        mask = _window_mask(pl.program_id(1) > 0)
        lane = lax.broadcasted_iota(jnp.int32, (ATTN_BLOCK, 128), 1)
        lse_tile = jnp.zeros((ATTN_BLOCK, 128), F32)
        for hh, hs in enumerate(HEAD_SLICES):
            sc = jnp.where(mask, _dot_nt(q_ref[:, hs], _both(kp_ref, kc_ref, hs)) * ATTN_SCALE, NEG)
            m = jnp.max(sc, axis=1, keepdims=True)
            p = jnp.exp(sc - m)
            den = jnp.sum(p, axis=1, keepdims=True)
            o_ref[:, hs] = _dot(p.astype(BF16), _both(vp_ref, vc_ref, hs)) / den
            lse_tile = jnp.where(lane == hh, m + jnp.log(den), lse_tile)
        lse_ref[...] = lse_tile

    cur = lambda t: pl.BlockSpec((ATTN_BLOCK, ATTN_OUT_WIDTH), lambda r, n: (r * nb + n, t))
    prv = lambda t: pl.BlockSpec((ATTN_BLOCK, ATTN_OUT_WIDTH), lambda r, n: (r * nb + jnp.maximum(n - 1, 0), t))
    return pl.pallas_call(
        body, name=name,
        out_shape=[jax.ShapeDtypeStruct((s, ATTN_OUT_WIDTH), F32), jax.ShapeDtypeStruct((s, 128), F32)],
        grid=(d, nb), in_specs=[cur(0), cur(1), prv(1), cur(2), prv(2)],
        out_specs=[pl.BlockSpec((ATTN_BLOCK, ATTN_OUT_WIDTH), lambda r, n: (r * nb + n, 0)),
                   pl.BlockSpec((ATTN_BLOCK, 128), lambda r, n: (r * nb + n, 0))],
        compiler_params=_params("parallel", "parallel"),
    )(qkv, qkv, qkv, qkv, qkv)


def _attn_combine(os_, lses, name):
    s = os_[0].shape[0]
    tm = _pick(s, (256, 128))
    ng = len(os_)

    def body(*refs):
        o_refs, l_refs, (o_ref, lse_ref) = refs[:ng], refs[ng:2 * ng], refs[2 * ng:]
        lane = lax.broadcasted_iota(jnp.int32, (tm, 128), 1)
        lse_tile = jnp.zeros((tm, 128), F32)
        for hh, hs in enumerate(HEAD_SLICES):
            ls = [l_ref[:, hh:hh + 1] for l_ref in l_refs]
            m = functools.reduce(jnp.maximum, ls)
            ws = [jnp.exp(l - m) for l in ls]
            tot = functools.reduce(lambda a, b: a + b, ws)
            acc = functools.reduce(lambda a, b: a + b, [o_r[:, hs] * w for o_r, w in zip(o_refs, ws)])
            o_ref[:, hs] = (acc / tot).astype(o_ref.dtype)
            lse_tile = jnp.where(lane == hh, m + jnp.log(tot), lse_tile)
        lse_ref[...] = lse_tile

    wide = pl.BlockSpec((tm, ATTN_OUT_WIDTH), lambda i: (i, 0))
    thin = pl.BlockSpec((tm, 128), lambda i: (i, 0))
    return pl.pallas_call(
        body, name=name,
        out_shape=[jax.ShapeDtypeStruct((s, ATTN_OUT_WIDTH), BF16), jax.ShapeDtypeStruct((s, 128), F32)],
        grid=(s // tm,), in_specs=[wide] * ng + [thin] * ng, out_specs=[wide, thin],
        compiler_params=_params("parallel"),
    )(*os_, *lses)


def _attn_delta(do, o, name):
    s = do.shape[0]
    tm = _pick(s, (256, 128))

    def body(do_ref, o_ref, out_ref):
        lane = lax.broadcasted_iota(jnp.int32, (tm, 128), 1)
        tile = jnp.zeros((tm, 128), F32)
        for hh, hs in enumerate(HEAD_SLICES):
            prod = do_ref[:, hs].astype(F32) * o_ref[:, hs].astype(F32)
            tile = jnp.where(lane == hh, jnp.sum(prod, axis=1, keepdims=True), tile)
        out_ref[...] = tile

    wide = pl.BlockSpec((tm, ATTN_OUT_WIDTH), lambda i: (i, 0))
    return pl.pallas_call(
        body, name=name, out_shape=jax.ShapeDtypeStruct((s, 128), F32), grid=(s // tm,),
        in_specs=[wide, wide], out_specs=pl.BlockSpec((tm, 128), lambda i: (i, 0)),
        compiler_params=_params("parallel"),
    )(do, o)


def _attn_group_bwd(qkv, do, lse, delta, tab, d, name):
    s = qkv.shape[0]
    nb = s // d // ATTN_BLOCK

    def body(q_ref, kc_ref, kp_ref, vc_ref, vp_ref, do_ref, lse_ref, dl_ref, tq_ref, tk_ref,
             dq_ref, dk_ref, dv_ref, ck_ref, cv_ref):
        n = pl.program_id(1)

        @pl.when(n == 0)
        def _():
            ck_ref[...] = jnp.zeros_like(ck_ref)
            cv_ref[...] = jnp.zeros_like(cv_ref)

        @pl.when(n < nb)
        def _():
            mask = _window_mask(n > 0)
            for hh, hs in enumerate(HEAD_SLICES):
                qh, doh = q_ref[:, hs], do_ref[:, hs]
                kh, vh = _both(kp_ref, kc_ref, hs), _both(vp_ref, vc_ref, hs)
                sc = jnp.where(mask, _dot_nt(qh, kh) * ATTN_SCALE, NEG)
                p = jnp.exp(sc - lse_ref[:, hh:hh + 1])
                ds = (p * (_dot_nt(doh, vh) - dl_ref[:, hh:hh + 1]) * ATTN_SCALE).astype(BF16)
                dq_ref[:, hs] = _rope(_dot(ds, kh), tq_ref[...], -1.0).astype(dq_ref.dtype)
                dk = _dot_tn(ds, qh)
                dv = _dot_tn(p.astype(BF16), doh)

                dk_ref[:, hs] = _rope(ck_ref[:, hs] + dk[:ATTN_BLOCK], tk_ref[...], -1.0).astype(dk_ref.dtype)
                dv_ref[:, hs] = (cv_ref[:, hs] + dv[:ATTN_BLOCK]).astype(dv_ref.dtype)
                ck_ref[:, hs] = dk[ATTN_BLOCK:]
                cv_ref[:, hs] = dv[ATTN_BLOCK:]

        @pl.when(n == nb)
        def _():
            for hs in HEAD_SLICES:
                dk_ref[:, hs] = _rope(ck_ref[:, hs], tk_ref[...], -1.0).astype(dk_ref.dtype)
                dv_ref[:, hs] = cv_ref[:, hs].astype(dv_ref.dtype)

    def spec(width, row, col):
        return pl.BlockSpec((ATTN_BLOCK, width), lambda r, n: (r * nb + row(n), col))

    cur = lambda n: jnp.minimum(n, nb - 1)
    prv = lambda n: jnp.maximum(jnp.minimum(n, nb - 1) - 1, 0)
    out = lambda n: jnp.maximum(n - 1, 0)
    wide = ATTN_OUT_WIDTH
    return pl.pallas_call(
        body, name=name,
        out_shape=[jax.ShapeDtypeStruct((s, wide), BF16)] * 3,
        grid=(d, nb + 1),
        in_specs=[spec(wide, cur, 0), spec(wide, cur, 1), spec(wide, prv, 1), spec(wide, cur, 2),
                  spec(wide, prv, 2), spec(wide, cur, 0), spec(128, cur, 0), spec(128, cur, 0),
                  spec(384, cur, 0), spec(384, out, 0)],
        out_specs=[spec(wide, cur, 0), spec(wide, out, 0), spec(wide, out, 0)],
        scratch_shapes=[pltpu.VMEM((ATTN_BLOCK, wide), F32), pltpu.VMEM((ATTN_BLOCK, wide), F32)],
        compiler_params=_params("parallel", "arbitrary"),
    )(qkv, qkv, qkv, qkv, qkv, do, lse, delta, tab, tab)


def _attn_core_fwd(qkvs, name):
    tab = rope_table(qkvs[0].shape[0])
    rot, os_, lses = [], [], []
    for g, (qkv, d) in enumerate(zip(qkvs, ATTN_DILATIONS)):
        qkv = _rope_qk(qkv, _to_dilated(tab, d), f"{name}_rope{g}")
        o_g, lse_g = _attn_group_fwd(qkv, d, f"{name}_fwd{g}")
        rot.append(qkv)
        os_.append(_from_dilated(o_g, d))
        lses.append(_from_dilated(lse_g, d))
    o, lse = _attn_combine(os_, lses, name + "_combine")
    return o, (tuple(rot), o, lse)


@functools.partial(jax.custom_vjp, nondiff_argnums=(1,))
def attn_core(qkvs, name):
    return _attn_core_fwd(qkvs, name)[0]


def _attn_core_vjp_fwd(qkvs, name):
    return _attn_core_fwd(qkvs, name)


def _attn_core_vjp_bwd(name, res, do):
    rot, o, lse = res
    tab = rope_table(o.shape[0])
    delta = _attn_delta(do, o, name + "_delta")
    out = []
    for g, (qkv, d) in enumerate(zip(rot, ATTN_DILATIONS)):
        parts = _attn_group_bwd(qkv, _to_dilated(do, d), _to_dilated(lse, d), _to_dilated(delta, d),
                                _to_dilated(tab, d), d, f"{name}_bwd{g}")
        out.append(jnp.concatenate(parts, axis=1))
    return (tuple(out),)


attn_core.defvjp(_attn_core_vjp_fwd, _attn_core_vjp_bwd)


def attention_mixer_p(h, lin, j, carries, tag):
    qkvs = tuple(lin(_to_dilated(h, d), ('attn_w_qkv', j), BF16, f"{tag}_qkv{g}",
                     cols=slice(g * QKV_GROUP, (g + 1) * QKV_GROUP), carry=carries[g])
                 for g, d in enumerate(ATTN_DILATIONS))
    return lin(attn_core(qkvs, tag), ('attn_w_o', j), F32, tag + "_o")


SSM_CONV_TAPS = 4
SSM_COL_BLOCK = 2048
SSM_PAIRS = SSM_HEADS // 2
SSM_DT_BLOCK = (SSM_D_INNER + SSM_CONV_DIM) // 128


def _ssm_conv_fwd(zx, conv_w, conv_b, name):
    s = zx.shape[0]
    tm = _pick(s, (256, 128))
    per = tm // SUBLANES
    ncb = SSM_CONV_DIM // SSM_COL_BLOCK

    def body(w_ref, b_ref, x_ref, halo_ref, o_ref):
        keep = pl.program_id(1) > 0
        for c0 in range(0, SSM_COL_BLOCK, FFN_COLS):
            cols = slice(c0, c0 + FFN_COLS)
            halo = jnp.where(keep, halo_ref[:, cols], 0.0)
            taps = _conv_taps(x_ref[:, cols], halo, SSM_CONV_TAPS)
            pre = sum(w_ref[k:k + 1, cols] * taps[k] for k in range(SSM_CONV_TAPS)) + b_ref[:, cols]
            o_ref[:, cols] = pre * _sigmoid(pre)

    return pl.pallas_call(
        body, name=name, out_shape=jax.ShapeDtypeStruct((s, SSM_CONV_DIM), F32), grid=(ncb, s // tm),
        in_specs=[pl.BlockSpec((SUBLANES, SSM_COL_BLOCK), lambda j, i: (0, j)),
                  pl.BlockSpec((1, SSM_COL_BLOCK), lambda j, i: (0, j)),
                  pl.BlockSpec((tm, SSM_COL_BLOCK), lambda j, i: (i, j + 1)),
                  pl.BlockSpec((SUBLANES, SSM_COL_BLOCK), lambda j, i: (jnp.maximum(i * per - 1, 0), j + 1))],
        out_specs=pl.BlockSpec((tm, SSM_COL_BLOCK), lambda j, i: (i, j)),
        compiler_params=_params("parallel", "parallel"),
    )(_pad_taps(conv_w), conv_b.reshape(1, SSM_CONV_DIM), zx, zx)


def _ssm_conv_bwd(zx, conv_w, conv_b, dact, name):
    s = zx.shape[0]
    tm = _pick(s, (256, 128))
    per = tm // SUBLANES
    nt = s // tm
    ncb = SSM_CONV_DIM // SSM_COL_BLOCK
    nk = SSM_CONV_TAPS

    def body(w_ref, b_ref, x_ref, halo_ref, da_ref, dx_ref, dw_ref, db_ref, carry_ref):
        step = pl.program_id(1)
        first = step == 0
        keep = step < nt - 1
        for c0 in range(0, SSM_COL_BLOCK, FFN_COLS):
            cols = slice(c0, c0 + FFN_COLS)
            halo = jnp.where(keep, halo_ref[:, cols], 0.0)
            taps = _conv_taps(x_ref[:, cols], halo, nk)
            pre = sum(w_ref[k:k + 1, cols] * taps[k] for k in range(nk)) + b_ref[:, cols]
            sig = _sigmoid(pre)
            dpre = da_ref[:, cols] * sig * (1.0 + pre * (1.0 - sig))
            nxt = jnp.where(first, 0.0, carry_ref[:, cols])
            dx = w_ref[nk - 1:nk, cols] * dpre
            for k in range(nk - 1):
                dx = dx + w_ref[k:k + 1, cols] * _shift_up(dpre, nxt, nk - 1 - k)
            dx_ref[:, cols] = dx.astype(dx_ref.dtype)
            carry_ref[:, cols] = dpre[0:SUBLANES, :]
            dwp = jnp.concatenate([jnp.sum(dpre * taps[k], axis=0, keepdims=True) for k in range(nk)]
                                  + [jnp.zeros((SUBLANES - nk, FFN_COLS), F32)], axis=0)
            dbp = jnp.sum(dpre, axis=0, keepdims=True)

            @pl.when(first)
            def _():
                dw_ref[:, cols] = dwp
                db_ref[:, cols] = dbp

            @pl.when(jnp.logical_not(first))
            def _():
                dw_ref[:, cols] += dwp
                db_ref[:, cols] += dbp

    rev = lambda i: nt - 1 - i
    dx, dw, db = pl.pallas_call(
        body, name=name,
        out_shape=[jax.ShapeDtypeStruct((s, SSM_CONV_DIM), BF16), jax.ShapeDtypeStruct((SUBLANES, SSM_CONV_DIM), F32),
                   jax.ShapeDtypeStruct((1, SSM_CONV_DIM), F32)],
        grid=(ncb, nt),
        in_specs=[pl.BlockSpec((SUBLANES, SSM_COL_BLOCK), lambda j, i: (0, j)),
                  pl.BlockSpec((1, SSM_COL_BLOCK), lambda j, i: (0, j)),
                  pl.BlockSpec((tm, SSM_COL_BLOCK), lambda j, i: (rev(i), j + 1)),
                  pl.BlockSpec((SUBLANES, SSM_COL_BLOCK), lambda j, i: (jnp.maximum(rev(i) * per - 1, 0), j + 1)),
                  pl.BlockSpec((tm, SSM_COL_BLOCK), lambda j, i: (rev(i), j))],
        out_specs=[pl.BlockSpec((tm, SSM_COL_BLOCK), lambda j, i: (rev(i), j)),
                   pl.BlockSpec((SUBLANES, SSM_COL_BLOCK), lambda j, i: (0, j)),
                   pl.BlockSpec((1, SSM_COL_BLOCK), lambda j, i: (0, j))],
        scratch_shapes=[pltpu.VMEM((SUBLANES, SSM_COL_BLOCK), F32)],
        compiler_params=_params("parallel", "arbitrary"),
    )(_pad_taps(conv_w), conv_b.reshape(1, SSM_CONV_DIM), zx, zx, dact)
    return dx, dw[:nk], db.reshape(SSM_CONV_DIM)


def _ssd_chunk(xs, bms, cms, dt_raw, dtb, alog, dsk, states):
    q = SSM_CHUNK
    lane = lax.broadcasted_iota(jnp.int32, (1, 128), 1)
    row = lax.broadcasted_iota(jnp.int32, (q, 1), 0)
    ii = lax.broadcasted_iota(jnp.int32, (q, q), 0)
    jj = lax.broadcasted_iota(jnp.int32, (q, q), 1)
    tril = ii >= jj
    left = lane < SSM_HEAD_DIM
    last_row = (row == q - 1).astype(F32)

    def lanes_of(mat, h):
        pick = (lane == h).astype(F32)
        return jnp.broadcast_to(jnp.sum(mat * pick, axis=1, keepdims=True), mat.shape)

    def rows_of(mat_t, h):
        pick = (row == h).astype(F32)
        return jnp.broadcast_to(jnp.sum(mat_t * pick, axis=0, keepdims=True), mat_t.shape)

    v = dt_raw + dtb
    dt = jnp.maximum(v, 0.0) + jnp.log(1.0 + jnp.exp(-jnp.abs(v)))
    adt = dt * (-jnp.exp(alog))
    acs = jnp.dot(tril.astype(F32), adt, precision=lax.Precision.HIGHEST, preferred_element_type=F32)
    acs_t = acs.T
    ys, new_states = [], []
    for pr in range(SSM_PAIRS):
        g = pr // 2
        if pr % 2 == 0:
            cb = _dot_nt(cms[g].astype(BF16), bms[g].astype(BF16))
        cols = [lanes_of(acs, 2 * pr + e) for e in range(2)]
        dts = [lanes_of(dt, 2 * pr + e) for e in range(2)]
        rws = [rows_of(acs_t, 2 * pr + e) for e in range(2)]
        lasts = [jnp.sum(c * last_row, axis=0, keepdims=True) for c in cols]
        xdt = xs[pr] * jnp.where(left, dts[0], dts[1])
        halves = [jnp.where(left, xdt, 0.0).astype(BF16), jnp.where(left, 0.0, xdt).astype(BF16)]
        y_diag, s_new = 0.0, 0.0
        for e in range(2):
            lmat = jnp.where(tril, jnp.exp(jnp.minimum(cols[e] - rws[e], 0.0)), 0.0)
            y_diag = y_diag + _dot((cb * lmat).astype(BF16), halves[e])
            decay = jnp.exp(lasts[e] - cols[e])
            s_new = s_new + _dot_tn((bms[g] * decay).astype(BF16), halves[e])
        y_off = _dot(cms[g].astype(BF16), states[pr].astype(BF16)) * jnp.where(left, jnp.exp(cols[0]), jnp.exp(cols[1]))
        skip = jnp.where(left, lanes_of(dsk, 2 * pr), lanes_of(dsk, 2 * pr + 1))
        ys.append(y_diag + y_off + xs[pr] * skip)
        new_states.append(states[pr] * jnp.where(left, jnp.exp(lasts[0]), jnp.exp(lasts[1])) + s_new)
    return tuple(ys), tuple(new_states)


def _ssm_vec(v):
    return jnp.pad(v.reshape(1, -1), ((0, 0), (0, 128 - v.shape[0])))


def _ssd_scan_fwd(act, zx, dtb, alog, dsk, name):
    s = act.shape[0]
    nc = s // SSM_CHUNK
    ng = SSM_GROUPS

    def body(act_ref, dt_ref, dtb_ref, alog_ref, dsk_ref, y_ref, st_out_ref, st_ref):
        @pl.when(pl.program_id(0) == 0)
        def _():
            st_ref[...] = jnp.zeros_like(st_ref)

        tile = lambda k: act_ref[:, k * 128:(k + 1) * 128]
        xs = [tile(k) for k in range(SSM_PAIRS)]
        bms = [tile(SSM_PAIRS + k) for k in range(ng)]
        cms = [tile(SSM_PAIRS + ng + k) for k in range(ng)]
        states = [st_ref[k] for k in range(SSM_PAIRS)]
        st_out_ref[0] = st_ref[...]
        ys, new_states = _ssd_chunk(xs, bms, cms, dt_ref[...], dtb_ref[...], alog_ref[...], dsk_ref[...], states)
        for k in range(SSM_PAIRS):
            y_ref[:, k * 128:(k + 1) * 128] = ys[k]
            st_ref[k] = new_states[k]

    vec = pl.BlockSpec((1, 128), lambda c: (0, 0))
    return pl.pallas_call(
        body, name=name,
        out_shape=[jax.ShapeDtypeStruct((s, SSM_D_INNER), F32),
                   jax.ShapeDtypeStruct((nc, SSM_PAIRS, SSM_STATE, 128), F32)],
        grid=(nc,),
        in_specs=[pl.BlockSpec((SSM_CHUNK, SSM_CONV_DIM), lambda c: (c, 0)),
                  pl.BlockSpec((SSM_CHUNK, 128), lambda c: (c, SSM_DT_BLOCK)), vec, vec, vec],
        out_specs=[pl.BlockSpec((SSM_CHUNK, SSM_D_INNER), lambda c: (c, 0)),
                   pl.BlockSpec((1, SSM_PAIRS, SSM_STATE, 128), lambda c: (c, 0, 0, 0))],
        scratch_shapes=[pltpu.VMEM((SSM_PAIRS, SSM_STATE, 128), F32)],
        compiler_params=_params("arbitrary"),
    )(act, zx, _ssm_vec(dtb), _ssm_vec(alog), _ssm_vec(dsk))


def _ssd_scan_bwd(act, zx, dtb, alog, dsk, st_in, dy, name):
    s = act.shape[0]
    nc = s // SSM_CHUNK
    ng = SSM_GROUPS

    def body(act_ref, dt_ref, dtb_ref, alog_ref, dsk_ref, st_ref, dy_ref, dact_ref, ddt_ref, dpar_ref, dst_ref):
        first = pl.program_id(0) == 0

        @pl.when(first)
        def _():
            dst_ref[...] = jnp.zeros_like(dst_ref)

        tile = lambda k: act_ref[:, k * 128:(k + 1) * 128]
        xs = [tile(k) for k in range(SSM_PAIRS)]
        bms = [tile(SSM_PAIRS + k) for k in range(ng)]
        cms = [tile(SSM_PAIRS + ng + k) for k in range(ng)]
        states = [st_ref[0, k] for k in range(SSM_PAIRS)]
        _, pullback = jax.vjp(_ssd_chunk, xs, bms, cms, dt_ref[...], dtb_ref[...], alog_ref[...], dsk_ref[...],
                              states)
        dys = tuple(dy_ref[:, k * 128:(k + 1) * 128] for k in range(SSM_PAIRS))
        dsts = tuple(dst_ref[k] for k in range(SSM_PAIRS))
        dxs, dbms, dcms, ddt, ddtb, dalog, ddsk, dstates = pullback((dys, dsts))
        for k, t in enumerate(list(dxs) + list(dbms) + list(dcms)):
            dact_ref[:, k * 128:(k + 1) * 128] = t
        ddt_ref[...] = ddt.astype(ddt_ref.dtype)
        for k in range(SSM_PAIRS):
            dst_ref[k] = dstates[k]
        dpar = jnp.concatenate([ddtb, dalog, ddsk, jnp.zeros((SUBLANES - 3, 128), F32)], axis=0)
        _accumulate(dpar_ref, dpar, first)

    rev = lambda c: nc - 1 - c
    vec = pl.BlockSpec((1, 128), lambda c: (0, 0))
    dact, ddt, dpar = pl.pallas_call(
        body, name=name,
        out_shape=[jax.ShapeDtypeStruct((s, SSM_CONV_DIM), F32), jax.ShapeDtypeStruct((s, 128), BF16),
                   jax.ShapeDtypeStruct((SUBLANES, 128), F32)],
        grid=(nc,),
        in_specs=[pl.BlockSpec((SSM_CHUNK, SSM_CONV_DIM), lambda c: (rev(c), 0)),
                  pl.BlockSpec((SSM_CHUNK, 128), lambda c: (rev(c), SSM_DT_BLOCK)), vec, vec, vec,
                  pl.BlockSpec((1, SSM_PAIRS, SSM_STATE, 128), lambda c: (rev(c), 0, 0, 0)),
                  pl.BlockSpec((SSM_CHUNK, SSM_D_INNER), lambda c: (rev(c), 0))],
        out_specs=[pl.BlockSpec((SSM_CHUNK, SSM_CONV_DIM), lambda c: (rev(c), 0)),
                   pl.BlockSpec((SSM_CHUNK, 128), lambda c: (rev(c), 0)),
                   pl.BlockSpec((SUBLANES, 128), lambda c: (0, 0))],
        scratch_shapes=[pltpu.VMEM((SSM_PAIRS, SSM_STATE, 128), F32)],
        compiler_params=_params("arbitrary"),
    )(act, zx, _ssm_vec(dtb), _ssm_vec(alog), _ssm_vec(dsk), st_in, dy)
    return dact, ddt, dpar[0, :SSM_HEADS], dpar[1, :SSM_HEADS], dpar[2, :SSM_HEADS]


SSM_NORM_GROUP = SSM_D_INNER // SSM_GROUPS


def _gated_group(y, z, w):
    g = y * (z * _sigmoid(z))
    return g * lax.rsqrt(jnp.mean(g * g, axis=-1, keepdims=True) + NORM_EPS) * w


def _gated_norm_fwd(y, zx, w, name):
    s = y.shape[0]
    tm = _pick(s, (256, 128))

    def body(y_ref, z_ref, w_ref, o_ref):
        for c0 in range(0, SSM_D_INNER, SSM_NORM_GROUP):
            cols = slice(c0, c0 + SSM_NORM_GROUP)
            o_ref[:, cols] = _gated_group(y_ref[:, cols], z_ref[:, cols], w_ref[:, cols]).astype(o_ref.dtype)

    blk = pl.BlockSpec((tm, SSM_D_INNER), lambda i: (i, 0))
    return pl.pallas_call(
        body, name=name, out_shape=jax.ShapeDtypeStruct((s, SSM_D_INNER), BF16), grid=(s // tm,),
        in_specs=[blk, blk, pl.BlockSpec((1, SSM_D_INNER), lambda i: (0, 0))], out_specs=blk,
        compiler_params=_params("parallel"),
    )(y, zx, w.reshape(1, SSM_D_INNER))


def _gated_norm_bwd(y, zx, w, dout, name):
    s = y.shape[0]
    tm = _pick(s, (256, 128))

    def body(y_ref, z_ref, w_ref, do_ref, dy_ref, dz_ref, dw_ref):
        first = pl.program_id(0) == 0
        for c0 in range(0, SSM_D_INNER, SSM_NORM_GROUP):
            cols = slice(c0, c0 + SSM_NORM_GROUP)
            _, pullback = jax.vjp(_gated_group, y_ref[:, cols], z_ref[:, cols], w_ref[:, cols])
            dyv, dzv, dwv = pullback(do_ref[:, cols].astype(F32))
            dy_ref[:, cols] = dyv
            dz_ref[:, cols] = dzv.astype(dz_ref.dtype)

            @pl.when(first)
            def _():
                dw_ref[:, cols] = dwv

            @pl.when(jnp.logical_not(first))
            def _():
                dw_ref[:, cols] += dwv

    blk = pl.BlockSpec((tm, SSM_D_INNER), lambda i: (i, 0))
    vec = pl.BlockSpec((1, SSM_D_INNER), lambda i: (0, 0))
    dy, dz, dw = pl.pallas_call(
        body, name=name,
        out_shape=[jax.ShapeDtypeStruct((s, SSM_D_INNER), F32), jax.ShapeDtypeStruct((s, SSM_D_INNER), BF16),
                   jax.ShapeDtypeStruct((1, SSM_D_INNER), F32)],
        grid=(s // tm,), in_specs=[blk, blk, vec, blk], out_specs=[blk, blk, vec],
        compiler_params=_params("arbitrary"),
    )(y, zx, w.reshape(1, SSM_D_INNER), dout)
    return dy, dz, dw.reshape(SSM_D_INNER)


def _ssm_core_fwd(zx, conv_w, conv_b, dtb, alog, dsk, norm_w, name):
    act = _ssm_conv_fwd(zx, conv_w, conv_b, name + "_conv_fwd")
    y, st_in = _ssd_scan_fwd(act, zx, dtb, alog, dsk, name + "_scan_fwd")
    out = _gated_norm_fwd(y, zx, norm_w, name + "_gate_fwd")
    return out, (zx, conv_w, conv_b, dtb, alog, dsk, norm_w, act, y, st_in)


@functools.partial(jax.custom_vjp, nondiff_argnums=(7,))
def ssm_core(zx, conv_w, conv_b, dtb, alog, dsk, norm_w, name):
    return _ssm_core_fwd(zx, conv_w, conv_b, dtb, alog, dsk, norm_w, name)[0]


def _ssm_core_vjp_fwd(zx, conv_w, conv_b, dtb, alog, dsk, norm_w, name):
    return _ssm_core_fwd(zx, conv_w, conv_b, dtb, alog, dsk, norm_w, name)


def _ssm_core_vjp_bwd(name, res, dout):
    zx, conv_w, conv_b, dtb, alog, dsk, norm_w, act, y, st_in = res
    dy, dz, dnorm_w = _gated_norm_bwd(y, zx, norm_w, dout, name + "_gate_bwd")
    dact, ddt, ddtb, dalog, ddsk = _ssd_scan_bwd(act, zx, dtb, alog, dsk, st_in, dy, name + "_scan_bwd")
    dxbc, dconv_w, dconv_b = _ssm_conv_bwd(zx, conv_w, conv_b, dact, name + "_conv_bwd")
    dzx = jnp.concatenate([dz, dxbc, ddt], axis=1)
    return dzx, dconv_w, dconv_b, ddtb, dalog, ddsk, dnorm_w


ssm_core.defvjp(_ssm_core_vjp_fwd, _ssm_core_vjp_bwd)


def ssd_mixer_p(h, lin, j, carries, conv_w, conv_b, dtb, alog, dsk, norm_w, tag):
    zx = lin(h, ('ssm_w_in', j), F32, tag + "_in", carry=carries[0])
    return lin(ssm_core(zx, conv_w, conv_b, dtb, alog, dsk, norm_w, tag), ('ssm_w_out', j), F32, tag + "_out",
               carry=carries[1])


def _full_weight(n, parts):
    full = _join8(parts, SHARD_AXIS[n] - 1)
    if n == 'ssm_w_in':
        full = jnp.pad(full, ((0, 0), (0, SSM_IN_PAD - SSM_IN_WIDTH)))
    return full


def trunk(w, x):
    ready = {('attn_w_qkv', 0): _full_weight('attn_w_qkv', gather_x(w['attn_w_qkv'][0], "gather_first"))}

    def lin(a, key, out_dtype, name, cols=None, carry=None):
        wt = ready[key] if cols is None else ready[key][:, cols]
        if carry is None:
            return linear(a, wt, out_dtype, name)
        y, parts = linear_x(a, wt, w[carry[0]][carry[1]], out_dtype, name)
        ready[carry] = _full_weight(carry[0], parts)
        return y

    for i in range(DEPTH):
        h = norm(x, w['mix_norm_w'][i], f"mixnorm{i}")
        j = i // 2
        ffn_next = [('ffn_w_up', i), ('ffn_w_down', i)]
        if i % 2 == 0:
            x = x + attention_mixer_p(h, lin, j, [('attn_w_o', j)] + ffn_next, f"attn{j}")
        else:
            x = x + ssd_mixer_p(h, lin, j, ffn_next, w['ssm_conv_w'][j], w['ssm_conv_b'][j], w['ssm_dt_bias'][j],
                                w['ssm_a_log'][j], w['ssm_d'][j], w['ssm_norm_w'][j], f"ssm{j}")
        if i + 1 == DEPTH:
            mixer_next = [None, None]
        elif i % 2 == 0:
            mixer_next = [('ssm_w_in', j), ('ssm_w_out', j)]
        else:
            mixer_next = [('attn_w_qkv', j + 1), None]
        h = norm(x, w['ffn_norm_w'][i], f"ffnnorm{i}")
        u0 = lin(h, ('ffn_w_up', i), F32, f"ffn{i}_up", carry=mixer_next[0])
        a = ffn_mid(u0, w['ffn_conv_w'][i], w['ffn_conv_b'][i], f"ffn{i}_mid")
        x = x + lin(a, ('ffn_w_down', i), F32, f"ffn{i}_down", carry=mixer_next[1])
    return x


def local_step(w, x, target):
    final_w = w['final_norm_w']
    trunk_w = {n: a for n, a in w.items() if n != 'final_norm_w'}
    xf, pullback = jax.vjp(trunk, trunk_w, x)
    loss, dxf, dfinal = loss_head(xf, final_w, target, "loss_head")
    gw, gx = pullback(dxf)
    gw['final_norm_w'] = dfinal
    return loss, gw, gx


def _adam_math(w, g, m, v):
    m = ADAM_B1 * m + (1.0 - ADAM_B1) * g
    v = ADAM_B2 * v + (1.0 - ADAM_B2) * (g * g)
    m_hat = m / (1.0 - ADAM_B1 ** ADAM_STEP)
    v_hat = v / (1.0 - ADAM_B2 ** ADAM_STEP)
    delta = -ADAM_LR * (m_hat / (jnp.sqrt(v_hat) + ADAM_EPS) + ADAM_WD * w)
    return delta, m, v


def _adamw_rows(g, w, m, v, name):
    r, c = w.shape
    tr = _pick(r, (256, 128, 64, 32, 16, 8))

    def body(g_ref, w_ref, m_ref, v_ref, d_out, m_out, v_out):
        delta, mm, vv = _adam_math(w_ref[...], g_ref[...], m_ref[...], v_ref[...])
        d_out[...] = delta
        m_out[...] = mm
        v_out[...] = vv

    blk = pl.BlockSpec((tr, c), lambda i: (i, 0))
    return pl.pallas_call(
        body, name=name, out_shape=[jax.ShapeDtypeStruct((r, c), F32)] * 3, grid=(r // tr,),
        in_specs=[blk] * 4, out_specs=[blk] * 3, compiler_params=_params("parallel"),
    )(g, w, m, v)


def _sum8(pieces, name):
    _, r, c = pieces.shape

    def body(p_ref, o_ref):
        g = p_ref[0]
        for j in range(1, N_DEV):
            g = g + p_ref[j]
        o_ref[...] = g

    return pl.pallas_call(
        body, name=name, out_shape=jax.ShapeDtypeStruct((r, c), F32),
        in_specs=[pl.BlockSpec(memory_space=pltpu.VMEM)], out_specs=pl.BlockSpec(memory_space=pltpu.VMEM),
    )(pieces)


def _adamw_plain(g, w, m, v, name):
    def body(g_ref, w_ref, m_ref, v_ref, d_out, m_out, v_out):
        delta, mm, vv = _adam_math(w_ref[...], g_ref[...], m_ref[...], v_ref[...])
        d_out[...] = delta
        m_out[...] = mm
        v_out[...] = vv

    vm = pl.BlockSpec(memory_space=pltpu.VMEM)
    return pl.pallas_call(
        body, name=name, out_shape=[jax.ShapeDtypeStruct(g.shape, F32)] * 3,
        in_specs=[vm] * 4, out_specs=[vm] * 3,
    )(g, w, m, v)


def _split8(full, axis):
    shp = full.shape
    t = full.reshape(shp[:axis] + (N_DEV, shp[axis] // N_DEV) + shp[axis + 1:])
    return jnp.moveaxis(t, axis, 0)


def _join8(parts, axis):
    t = jnp.moveaxis(parts, 0, axis)
    shp = t.shape
    return t.reshape(shp[:axis] + (shp[axis] * shp[axis + 1],) + shp[axis + 2:])


def _pack(arrs, lead, mult):
    flat = jnp.concatenate([a.reshape(a.shape[:lead] + (-1,)) for a in arrs], axis=-1)
    return _pad_rows(flat, mult)


def _unpack(buf, shapes, lead):
    flat = buf.reshape(buf.shape[:lead] + (-1,))
    out, off = [], 0
    for shp in shapes:
        n = math.prod(shp)
        out.append(flat[..., off:off + n].reshape(flat.shape[:lead] + tuple(shp)))
        off += n
    return out


def _own_shard(full, axis):
    size = full.shape[axis] // N_DEV
    return lax.dynamic_slice_in_dim(full, _my_index() * size, size, axis)


def kernel(x, mix_norm_w, attn_w_qkv, attn_w_o, ssm_w_in, ssm_conv_w, ssm_conv_b, ssm_dt_bias, ssm_a_log, ssm_d, ssm_norm_w, ssm_w_out, ffn_norm_w, ffn_w_up, ffn_conv_w, ffn_conv_b, ffn_w_down, final_norm_w, loss_target, m_mix_norm_w, m_attn_w_qkv, m_attn_w_o, m_ssm_w_in, m_ssm_conv_w, m_ssm_conv_b, m_ssm_dt_bias, m_ssm_a_log, m_ssm_d, m_ssm_norm_w, m_ssm_w_out, m_ffn_norm_w, m_ffn_w_up, m_ffn_conv_w, m_ffn_conv_b, m_ffn_w_down, m_final_norm_w, v_mix_norm_w, v_attn_w_qkv, v_attn_w_o, v_ssm_w_in, v_ssm_conv_w, v_ssm_conv_b, v_ssm_dt_bias, v_ssm_a_log, v_ssm_d, v_ssm_norm_w, v_ssm_w_out, v_ffn_norm_w, v_ffn_w_up, v_ffn_conv_w, v_ffn_conv_b, v_ffn_w_down, v_final_norm_w):
    args = dict(locals())
    w_sh = {n: args[n] for n in WEIGHT_NAMES}
    m_sh = {n: args["m_" + n] for n in WEIGHT_NAMES}
    v_sh = {n: args["v_" + n] for n in WEIGHT_NAMES}

    small_shapes = [w_sh[n].shape for n in SMALL_SHARDED]
    small = _exchange(_pack([w_sh[n] for n in SMALL_SHARDED], 0, 8), True, "gather_small")
    full = {n: w_sh[n] for n in SMALL if SHARD_AXIS[n] is None}
    for n, parts in zip(SMALL_SHARDED, _unpack(small, small_shapes, 1)):
        full[n] = _join8(parts, SHARD_AXIS[n])
    for n in BIG:
        full[n] = w_sh[n]

    loss, gw, gx = local_step(full, x[0], loss_target[0])
    loss = lax.psum(loss, ("x", "y", "c"))

    grads, deltas, new_m, new_v = {}, {}, {}, {}
    for n in BIG:
        shp = w_sh[n].shape
        two_d = (shp[0] * shp[1], shp[2])
        outs = _adamw_rows(*[t.reshape(two_d) for t in (gw[n], w_sh[n], m_sh[n], v_sh[n])], "adamw_" + n)
        grads[n] = gw[n]
        deltas[n], new_m[n], new_v[n] = [o.reshape(shp) for o in outs]

    small_full_shapes = [gw[n].shape for n in SMALL]
    gsmall = _exchange(_pack([gw[n] for n in SMALL], 0, 8), True, "gather_small_grads")
    gsmall = _unpack(_sum8(gsmall, "sum_small_grads"), small_full_shapes, 0)
    for n, g in zip(SMALL, gsmall):
        grads[n] = g if SHARD_AXIS[n] is None else _own_shard(g, SHARD_AXIS[n])
    shapes = [w_sh[n].shape for n in SMALL]
    outs = _adamw_plain(*[_pack([d[n] for n in SMALL], 0, 8) for d in (grads, w_sh, m_sh, v_sh)], "adamw_small")
    for d, buf in zip((deltas, new_m, new_v), outs):
        for n, a in zip(SMALL, _unpack(buf, shapes, 0)):
            d[n] = a

    return (loss, gx[None], *[grads[n] for n in WEIGHT_NAMES], *[deltas[n] for n in WEIGHT_NAMES],
            *[new_m[n] for n in WEIGHT_NAMES], *[new_v[n] for n in WEIGHT_NAMES])
```

```python
import functools
import math

import jax
import jax.numpy as jnp
from jax import lax
from jax.experimental import pallas as pl
from jax.experimental.pallas import tpu as pltpu

F32 = jnp.float32
BF16 = jnp.bfloat16
N_DEV = 8
MESH_ID = pl.DeviceIdType.MESH

D_MODEL = 1024
DEPTH = 4
ATTN_HEADS = 8
ATTN_HEAD_DIM = 128
ATTN_DILATIONS = (1, 4, 16)
ATTN_STEPS = (128, 128, 128)
N_ATTN_GROUPS = 3
ATTN_BLOCK = 128
ROPE_THETA = 500000.0
ROPE_DIM = 32
ATTN_OUT_WIDTH = 1024
SSM_D_INNER = 2048
SSM_HEAD_DIM = 64
SSM_HEADS = 32
SSM_STATE = 128
SSM_GROUPS = 8
SSM_CHUNK = 128
SSM_CONV_DIM = 4096
SSM_IN_WIDTH = 6176
SSM_IN_PAD = 6272
D_FF = 2816
NORM_EPS = 1e-5
ADAM_LR = 0.001
ADAM_B1 = 0.9
ADAM_B2 = 0.999
ADAM_EPS = 1e-08
ADAM_WD = 0.01
ADAM_STEP = 10

WEIGHT_NAMES = ['mix_norm_w', 'attn_w_qkv', 'attn_w_o', 'ssm_w_in', 'ssm_conv_w', 'ssm_conv_b', 'ssm_dt_bias',
                'ssm_a_log', 'ssm_d', 'ssm_norm_w', 'ssm_w_out', 'ffn_norm_w', 'ffn_w_up', 'ffn_conv_w',
                'ffn_conv_b', 'ffn_w_down', 'final_norm_w']
SHARD_AXIS = {'mix_norm_w': None, 'attn_w_qkv': 2, 'attn_w_o': 1, 'ssm_w_in': 2, 'ssm_conv_w': 2, 'ssm_conv_b': 1,
              'ssm_dt_bias': None, 'ssm_a_log': None, 'ssm_d': None, 'ssm_norm_w': 1, 'ssm_w_out': 1,
              'ffn_norm_w': None, 'ffn_w_up': 2, 'ffn_conv_w': 2, 'ffn_conv_b': None, 'ffn_w_down': 1,
              'final_norm_w': None}
BIG = ['attn_w_qkv', 'attn_w_o', 'ssm_w_in', 'ssm_w_out', 'ffn_w_up', 'ffn_w_down']
SMALL = [n for n in WEIGHT_NAMES if n not in BIG]
SMALL_SHARDED = [n for n in SMALL if SHARD_AXIS[n] is not None]
LANES = 1024


def _my_index():
    return 4 * lax.axis_index("x") + 2 * lax.axis_index("y") + lax.axis_index("c")


def _peer(k):
    x, y, c = lax.axis_index("x"), lax.axis_index("y"), lax.axis_index("c")
    return (x ^ ((k >> 2) & 1), y ^ ((k >> 1) & 1), c ^ (k & 1))


def _exchange(src, gather, name):
    def body(src_ref, out_ref, send_sems, recv_sems, local_sem):
        start, wait = _exchange_copies(src_ref, out_ref, send_sems, recv_sems, local_sem, gather)
        start()
        wait()

    return pl.pallas_call(
        body, name=name,
        out_shape=_exchange_out(src, gather),
        in_specs=[pl.BlockSpec(memory_space=pl.ANY)],
        out_specs=pl.BlockSpec(memory_space=pl.ANY),
        scratch_shapes=list(EXCHANGE_SEMS),
    )(src)


EXCHANGE_SEMS = (pltpu.SemaphoreType.DMA((N_DEV - 1,)), pltpu.SemaphoreType.DMA((N_DEV - 1,)),
                 pltpu.SemaphoreType.DMA)


def _exchange_out(src, gather):
    return jax.ShapeDtypeStruct((N_DEV,) + (src.shape if gather else src.shape[1:]), src.dtype)


def _exchange_copies(src_ref, out_ref, send_sems, recv_sems, local_sem, gather):
    me = _my_index()

    def piece(j):
        return src_ref if gather else src_ref.at[j]

    def remote(k, slab):
        return pltpu.make_async_remote_copy(
            src_ref=piece(me ^ k), dst_ref=out_ref.at[slab], send_sem=send_sems.at[k - 1],
            recv_sem=recv_sems.at[k - 1], device_id=_peer(k), device_id_type=MESH_ID)

    mine = pltpu.make_async_copy(piece(me), out_ref.at[me], local_sem)
    sends = [remote(k, me) for k in range(1, N_DEV)]
    arrivals = [remote(k, me ^ k) for k in range(1, N_DEV)]

    def start():
        mine.start()
        for cp in sends:
            cp.start()

    def wait():
        for cp in arrivals:
            cp.wait_recv()
        for cp in sends:
            cp.wait_send()
        mine.wait()

    return start, wait


def _pad_rows(flat, mult):
    n = flat.shape[-1]
    rows = -(-n // (LANES * mult)) * mult
    pad = rows * LANES - n
    flat = jnp.pad(flat, [(0, 0)] * (flat.ndim - 1) + [(0, pad)])
    return flat.reshape(flat.shape[:-1] + (rows, LANES))


def _pick(n, cands):
    for c in cands:
        if n % c == 0:
            return c
    return n


def _matmul(a, b, ta, tb, out_dtype, name, rider=None, res=None):
    (m, k) = (a.shape[1], a.shape[0]) if ta else a.shape
    (k2, n) = (b.shape[1], b.shape[0]) if tb else b.shape
    assert k == k2, (a.shape, b.shape, ta, tb)
    tm = _pick(m, (1024, 1408, 512, 256, 128))
    tn = _pick(n, (1536, 1408, 1024, 896, 512, 384, 256, 128))
    tk = _pick(k, (1024, 1408, 896, 512, 256, 128)) if k > 2816 else k
    nk = k // tk
    dims = (((0 if ta else 1,), (1 if tb else 0,)), ((), ()))

    grid = (n // tn, m // tm, nk)

    n_in = 2 + (res is not None) + (rider is not None)

    def body(*refs):
        ins, rest = refs[:n_in], refs[n_in:]
        a_ref, b_ref = ins[:2]
        res_ref = ins[2] if res is not None else None
        o_ref, scratch = rest[0], rest[1:]
        if rider is not None:
            start, wait = _exchange_copies(ins[-1], rest[1], *scratch[-3:], rider[1])
            scratch = scratch[1:-3]
            at = [pl.program_id(ax) for ax in range(3)]
            pl.when(functools.reduce(jnp.logical_and, [p == 0 for p in at]))(start)
        part = lax.dot_general(a_ref[...].astype(BF16), b_ref[...].astype(BF16), dims,
                               preferred_element_type=F32)

        def finish(total):
            if res_ref is not None:
                total = total + res_ref[...]
            o_ref[...] = total.astype(o_ref.dtype)

        if nk == 1:
            finish(part)
        else:
            acc_ref, = scratch
            kk = pl.program_id(2)

            @pl.when(kk == 0)
            def _():
                acc_ref[...] = part

            @pl.when(kk > 0)
            def _():
                acc_ref[...] += part

            @pl.when(kk == nk - 1)
            def _():
                finish(acc_ref[...])
        if rider is not None:
            pl.when(functools.reduce(jnp.logical_and, [p == g - 1 for p, g in zip(at, grid)]))(wait)

    a_spec = (pl.BlockSpec((tk, tm), lambda j, i, kk: (kk, i)) if ta
              else pl.BlockSpec((tm, tk), lambda j, i, kk: (i, kk)))
    b_spec = (pl.BlockSpec((tn, tk), lambda j, i, kk: (j, kk)) if tb
              else pl.BlockSpec((tk, tn), lambda j, i, kk: (kk, j)))
    any_spec = pl.BlockSpec(memory_space=pl.ANY)
    tile_spec = pl.BlockSpec((tm, tn), lambda j, i, kk: (i, j))
    in_specs, operands = [a_spec, b_spec], [a, b]
    out_shape, out_specs = [jax.ShapeDtypeStruct((m, n), out_dtype)], [tile_spec]
    scratch = [] if nk == 1 else [pltpu.VMEM((tm, tn), F32)]
    sem = ("parallel", "parallel", "arbitrary")
    if res is not None:
        in_specs.append(tile_spec)
        operands.append(res)
    if rider is not None:
        in_specs.append(any_spec)
        operands.append(rider[0])
        out_shape.append(_exchange_out(*rider))
        out_specs.append(any_spec)
        scratch += list(EXCHANGE_SEMS)
        sem = ("arbitrary",) * 3
    outs = pl.pallas_call(
        body, name=name, out_shape=out_shape, grid=grid, in_specs=in_specs, out_specs=out_specs,
        scratch_shapes=scratch, compiler_params=_params(*sem),
    )(*operands)
    return outs[0] if rider is None else tuple(outs)


@functools.partial(jax.custom_vjp, nondiff_argnums=(3, 4))
def linear(a, w, res, out_dtype, name):
    return _matmul(a, w, False, False, out_dtype, name + "_fwd", res=res)


def _linear_fwd(a, w, res, out_dtype, name):
    return _matmul(a, w, False, False, out_dtype, name + "_fwd", res=res), (a, w, res is not None)


def _linear_bwd(out_dtype, name, saved, dy):
    a, w, has_res = saved
    da = _matmul(dy, w, False, True, a.dtype, name + "_da")
    dw = _matmul(a, dy, True, False, w.dtype, name + "_dw")
    return da, dw, (dy if has_res else None)


linear.defvjp(_linear_fwd, _linear_bwd)


def _sum_pieces(pieces, name):
    shape = pieces.shape[1:]
    c = shape[-1]
    r = math.prod(shape[:-1])
    tr = _pick(r, (512, 256, 128, 64, 32, 16, 8))

    def body(p_ref, o_ref):
        g = p_ref[0].astype(F32)
        for j in range(1, N_DEV):
            g = g + p_ref[j].astype(F32)
        o_ref[...] = g

    return pl.pallas_call(
        body, name=name, out_shape=jax.ShapeDtypeStruct((r, c), F32), grid=(r // tr,),
        in_specs=[pl.BlockSpec((N_DEV, tr, c), lambda i: (0, i, 0))],
        out_specs=pl.BlockSpec((tr, c), lambda i: (i, 0)),
        compiler_params=_params("parallel"),
    )(pieces.reshape(N_DEV, r, c)).reshape(shape)


@functools.partial(jax.custom_vjp, nondiff_argnums=(1,))
def gather_x(shard, name):
    return _exchange(shard.astype(BF16), True, name)


def _gather_x_fwd(shard, name):
    return _exchange(shard.astype(BF16), True, name), None


def _gather_x_bwd(name, _, dparts):
    return (_sum_pieces(_exchange(dparts, False, name + "_back"), name + "_sum"),)


gather_x.defvjp(_gather_x_fwd, _gather_x_bwd)


@functools.partial(jax.custom_vjp, nondiff_argnums=(4, 5))
def linear_x(a, w, res, shard, out_dtype, name):
    return _matmul(a, w, False, False, out_dtype, name + "_fwd", rider=(shard.astype(BF16), True), res=res)


def _linear_x_fwd(a, w, res, shard, out_dtype, name):
    out = _matmul(a, w, False, False, out_dtype, name + "_fwd", rider=(shard.astype(BF16), True), res=res)
    return out, (a, w, res is not None)


def _linear_x_bwd(out_dtype, name, saved, cts):
    a, w, has_res = saved
    dy, dparts = cts
    half = dparts.shape[1] // 2
    da, moved_a = _matmul(dy, w, False, True, a.dtype, name + "_da", rider=(dparts[:, :half], False))
    dw, moved_b = _matmul(a, dy, True, False, w.dtype, name + "_dw", rider=(dparts[:, half:], False))
    dshard = jnp.concatenate([_sum_pieces(moved_a, name + "_sum_a"), _sum_pieces(moved_b, name + "_sum_b")])
    return da, dw, (dy if has_res else None), dshard


linear_x.defvjp(_linear_x_fwd, _linear_x_bwd)


def rmsnorm(x, w):
    y = x * lax.rsqrt(jnp.mean(x * x, axis=-1, keepdims=True) + NORM_EPS)
    return y * w


def causal_depthwise_conv(x, w, b):
    k = w.shape[0]
    y = lax.conv_general_dilated(
        x[None], w[:, None, :], window_strides=(1,), padding=[(k - 1, 0)],
        dimension_numbers=("NWC", "WIO", "NWC"), feature_group_count=x.shape[-1])[0]
    return y + b


def rope_tables(seq):
    pos = jnp.arange(seq, dtype=F32)
    inv_freq = ROPE_THETA ** (-jnp.arange(0, ROPE_DIM, 2, dtype=F32) / ROPE_DIM)
    ang = pos[:, None] * inv_freq[None, :]
    return jnp.cos(ang), jnp.sin(ang)


def apply_partial_rope(t, cos, sin):
    half = ROPE_DIM // 2
    c = cos[:, None, None, :]
    s = sin[:, None, None, :]
    x1 = t[..., :half]
    x2 = t[..., half:ROPE_DIM]
    return jnp.concatenate([x1 * c - x2 * s, x2 * c + x1 * s, t[..., ROPE_DIM:]], axis=-1)


def dilated_window_attention(q, k, v, dilation, steps):
    s, h, hd = q.shape
    length = s // dilation
    nb = -(-length // ATTN_BLOCK)
    lp = nb * ATTN_BLOCK

    def to_strided(t):
        t = t.reshape(length, dilation, h, hd).transpose(1, 2, 0, 3)
        t = jnp.pad(t, ((0, 0), (0, 0), (0, lp - length), (0, 0)))
        return t.reshape(dilation, h, nb, ATTN_BLOCK, hd)

    def with_prev(t):
        prev = jnp.pad(t, ((0, 0), (0, 0), (1, 0), (0, 0), (0, 0)))[:, :, :-1]
        return jnp.concatenate([prev, t], axis=-2)

    qb = to_strided(q)
    kk = with_prev(to_strided(k))
    vv = with_prev(to_strided(v))
    scores = jnp.einsum("rhnqe,rhnke->rhnqk", qb, kk) * (hd ** -0.5)
    n_idx = jnp.arange(nb)[:, None, None]
    i_idx = jnp.arange(ATTN_BLOCK)[None, :, None]
    j_idx = jnp.arange(2 * ATTN_BLOCK)[None, None, :]
    delta = ATTN_BLOCK + i_idx - j_idx
    key_pos = (n_idx - 1) * ATTN_BLOCK + j_idx
    allowed = (delta >= 0) & (delta <= steps) & (key_pos >= 0)
    scores = jnp.where(allowed, scores, -jnp.inf)
    m = jnp.max(scores, axis=-1, keepdims=True)
    p = jnp.exp(scores - m)
    den = jnp.sum(p, axis=-1, keepdims=True)
    o = jnp.einsum("rhnqk,rhnke->rhnqe", p, vv) / den
    lse = (m + jnp.log(den))[..., 0]
    o = o.reshape(dilation, h, lp, hd)[:, :, :length]
    o = o.transpose(2, 0, 1, 3).reshape(s, h, hd)
    lse = lse.reshape(dilation, h, lp)[:, :, :length]
    lse = lse.transpose(2, 0, 1).reshape(s, h)
    return o, lse


def attention_mixer(h, w_qkv, w_o, cos, sin, tag):
    s = h.shape[0]
    qkv = linear(h.astype(BF16), w_qkv, F32, tag + "_qkv")
    qkv = qkv.reshape(s, N_ATTN_GROUPS, 3, ATTN_HEADS, ATTN_HEAD_DIM)
    q = apply_partial_rope(qkv[:, :, 0], cos, sin)
    k = apply_partial_rope(qkv[:, :, 1], cos, sin)
    v = qkv[:, :, 2]
    outs, lses = [], []
    for g in range(N_ATTN_GROUPS):
        o_g, lse_g = dilated_window_attention(q[:, g], k[:, g], v[:, g], ATTN_DILATIONS[g], ATTN_STEPS[g])
        outs.append(o_g)
        lses.append(lse_g)
    wts = jax.nn.softmax(jnp.stack(lses, axis=1), axis=1)
    o = jnp.einsum("sgh,sghe->she", wts, jnp.stack(outs, axis=1))
    return linear(o.reshape(s, ATTN_OUT_WIDTH).astype(BF16), w_o, F32, tag + "_o")


def ssd_chunked_scan(x, dt, a, bm, cm):
    s, h, p = x.shape
    g, n = bm.shape[1], bm.shape[2]
    r = h // g
    c = s // SSM_CHUNK
    q = SSM_CHUNK
    x = x.reshape(c, q, g, r, p)
    dt = dt.reshape(c, q, g, r)
    bm = bm.reshape(c, q, g, n)
    cm = cm.reshape(c, q, g, n)
    a_dt = dt * a.reshape(g, r)
    a_cs = jnp.cumsum(a_dt, axis=1)
    xdt = x * dt[..., None]
    seg = a_cs[:, :, None] - a_cs[:, None, :]
    causal = jnp.tril(jnp.ones((q, q), dtype=bool))[:, :, None, None]
    lmat = jnp.exp(jnp.where(causal, seg, -jnp.inf))
    cb = jnp.einsum("cign,cjgn->cijg", cm, bm)
    y_diag = jnp.einsum("cijgr,cjgrp->cigrp", cb[..., None] * lmat, xdt)
    decay = jnp.exp(a_cs[:, -1:] - a_cs)
    states = jnp.einsum("cjgn,cjgr,cjgrp->cgrpn", bm, decay, xdt)
    chunk_decay = jnp.exp(a_cs[:, -1])

    def step(state, inp):
        st_c, dec_c = inp
        return state * dec_c[..., None, None] + st_c, state

    init = jnp.zeros((g, r, p, n), dtype=x.dtype)
    _, prev = lax.scan(step, init, (states, chunk_decay))
    y_off = jnp.einsum("cign,cgrpn->cigrp", cm, prev) * jnp.exp(a_cs)[..., None]
    return (y_diag + y_off).reshape(s, h, p)


def gated_group_rmsnorm(y, z, w, groups):
    g = y * jax.nn.silu(z)
    shp = g.shape
    g = g.reshape(shp[:-1] + (groups, shp[-1] // groups))
    g = g * lax.rsqrt(jnp.mean(g * g, axis=-1, keepdims=True) + NORM_EPS)
    return g.reshape(shp) * w


def ssd_mixer(h, w_in, conv_w, conv_b, dt_bias, a_log, d_skip, norm_w, w_out, tag):
    s = h.shape[0]
    gn = SSM_GROUPS * SSM_STATE
    zxbcdt = linear(h.astype(BF16), w_in, F32, tag + "_in")
    z = zxbcdt[:, :SSM_D_INNER]
    xbc = zxbcdt[:, SSM_D_INNER:SSM_D_INNER + SSM_CONV_DIM]
    dt_raw = zxbcdt[:, SSM_D_INNER + SSM_CONV_DIM:SSM_IN_WIDTH]
    xbc = jax.nn.silu(causal_depthwise_conv(xbc, conv_w, conv_b))
    xs = xbc[:, :SSM_D_INNER].reshape(s, SSM_HEADS, SSM_HEAD_DIM)
    bm = xbc[:, SSM_D_INNER:SSM_D_INNER + gn].reshape(s, SSM_GROUPS, SSM_STATE)
    cm = xbc[:, SSM_D_INNER + gn:].reshape(s, SSM_GROUPS, SSM_STATE)
    dt = jax.nn.softplus(dt_raw + dt_bias)
    a = -jnp.exp(a_log)
    y = ssd_chunked_scan(xs, dt, a, bm, cm)
    y = y + d_skip[:, None] * xs
    y = gated_group_rmsnorm(y.reshape(s, SSM_D_INNER), z, norm_w, SSM_GROUPS)
    return linear(y.astype(BF16), w_out, F32, tag + "_out")


def conv_ffn(h, w_up, conv_w, conv_b, w_down, tag):
    u = causal_depthwise_conv(linear(h.astype(BF16), w_up, F32, tag + "_up"), conv_w, conv_b)
    gate, up = u[:, :D_FF], u[:, D_FF:]
    return linear((jax.nn.silu(gate) * up).astype(BF16), w_down, F32, tag + "_down")


VMEM_LIMIT = 56 * 1024 * 1024
SUBLANES = 8


def _params(*sem):
    return pltpu.CompilerParams(dimension_semantics=sem, vmem_limit_bytes=VMEM_LIMIT)


def _sigmoid(x):
    return 0.5 * jnp.tanh(0.5 * x) + 0.5


def _rstd(xv):
    return lax.rsqrt(jnp.mean(xv * xv, axis=-1, keepdims=True) + NORM_EPS)


def _accumulate(ref, part, first):
    @pl.when(first)
    def _():
        ref[...] = part

    @pl.when(jnp.logical_not(first))
    def _():
        ref[...] += part


def _norm_fwd(x, w, name):
    s, d = x.shape
    tm = _pick(s, (512, 256, 128))

    def body(x_ref, w_ref, h_ref):
        xv = x_ref[...]
        h_ref[...] = (xv * _rstd(xv) * w_ref[...]).astype(h_ref.dtype)

    return pl.pallas_call(
        body, name=name, out_shape=jax.ShapeDtypeStruct((s, d), BF16), grid=(s // tm,),
        in_specs=[pl.BlockSpec((tm, d), lambda i: (i, 0)), pl.BlockSpec((1, d), lambda i: (0, 0))],
        out_specs=pl.BlockSpec((tm, d), lambda i: (i, 0)), compiler_params=_params("parallel"),
    )(x, w.reshape(1, d))


def _norm_bwd(x, w, dh, dskip, name):
    s, d = x.shape
    tm = _pick(s, (512, 256, 128))

    def body(x_ref, w_ref, dh_ref, ds_ref, dx_ref, dw_ref):
        xv = x_ref[...]
        r = _rstd(xv)
        y = xv * r
        dhv = dh_ref[...].astype(F32)
        dy = dhv * w_ref[...]
        dx_ref[...] = ds_ref[...] + r * (dy - y * jnp.mean(dy * y, axis=-1, keepdims=True))
        _accumulate(dw_ref, jnp.sum(dhv * y, axis=0, keepdims=True), pl.program_id(0) == 0)

    row = pl.BlockSpec((tm, d), lambda i: (i, 0))
    vec = pl.BlockSpec((1, d), lambda i: (0, 0))
    dx, dw = pl.pallas_call(
        body, name=name,
        out_shape=[jax.ShapeDtypeStruct((s, d), F32), jax.ShapeDtypeStruct((1, d), F32)], grid=(s // tm,),
        in_specs=[row, vec, row, row], out_specs=[row, vec], compiler_params=_params("arbitrary"),
    )(x, w.reshape(1, d), dh, dskip)
    return dx, dw.reshape(d)


@functools.partial(jax.custom_vjp, nondiff_argnums=(2,))
def norm(x, w, name):
    return x, _norm_fwd(x, w, name + "_fwd")


def _norm_vjp_fwd(x, w, name):
    return (x, _norm_fwd(x, w, name + "_fwd")), (x, w)


def _norm_vjp_bwd(name, saved, cts):
    x, w = saved
    dskip, dh = cts
    return _norm_bwd(x, w, dh, dskip, name + "_bwd")


norm.defvjp(_norm_vjp_fwd, _norm_vjp_bwd)


def loss_head(x, w, target, name):
    s, d = x.shape
    tm = _pick(s, (512, 256, 128))

    def body(x_ref, w_ref, t_ref, loss_ref, dx_ref, dw_ref):
        first = pl.program_id(0) == 0
        xv = x_ref[...]
        r = _rstd(xv)
        y = xv * r
        err = y * w_ref[...] - t_ref[...]
        part = 0.5 * jnp.sum(jnp.sum(err * err, axis=-1, keepdims=True), axis=0, keepdims=True) / d
        _accumulate(loss_ref, jnp.broadcast_to(part, loss_ref.shape), first)
        dout = err / d
        dy = dout * w_ref[...]
        dx_ref[...] = r * (dy - y * jnp.mean(dy * y, axis=-1, keepdims=True))
        _accumulate(dw_ref, jnp.sum(dout * y, axis=0, keepdims=True), first)

    row = pl.BlockSpec((tm, d), lambda i: (i, 0))
    vec = pl.BlockSpec((1, d), lambda i: (0, 0))
    loss, dx, dw = pl.pallas_call(
        body, name=name,
        out_shape=[jax.ShapeDtypeStruct((1, 128), F32), jax.ShapeDtypeStruct((s, d), F32),
                   jax.ShapeDtypeStruct((1, d), F32)],
        grid=(s // tm,), in_specs=[row, vec, row],
        out_specs=[pl.BlockSpec((1, 128), lambda i: (0, 0)), row, vec],
        compiler_params=_params("arbitrary"),
    )(x, w.reshape(1, d), target)
    return loss[0, 0], dx, dw.reshape(d)


FFN_ROWS = 256
FFN_COLS = 256


def _shift_down(cur, halo, k):
    out = pltpu.roll(cur, k, axis=0)
    row = lax.broadcasted_iota(jnp.int32, halo.shape, 0)
    top = out[0:SUBLANES]
    for j in range(k):
        top = jnp.where(row == j, halo[SUBLANES - k + j:SUBLANES - k + j + 1, :], top)
    return jnp.concatenate([top, out[SUBLANES:]], axis=0)


def _shift_up(cur, nxt, k):
    n = cur.shape[0]
    out = pltpu.roll(cur, n - k, axis=0)
    row = lax.broadcasted_iota(jnp.int32, nxt.shape, 0)
    bottom = out[n - SUBLANES:]
    for j in range(k):
        bottom = jnp.where(row == SUBLANES - k + j, nxt[j:j + 1, :], bottom)
    return jnp.concatenate([out[:n - SUBLANES], bottom], axis=0)


def _conv_taps(cur, halo, ntaps):
    return [_shift_down(cur, halo, ntaps - 1 - k) if k < ntaps - 1 else cur for k in range(ntaps)]


def _pad_taps(conv_w):
    return jnp.pad(conv_w, ((0, SUBLANES - conv_w.shape[0]), (0, 0)))


def _ffn_mid_fwd(u0, conv_w, conv_b, name):
    s, width = u0.shape
    half = width // 2
    tm = _pick(s, (FFN_ROWS, 128))
    per = tm // SUBLANES

    def body(w_ref, b_ref, u_ref, halo_ref, a_ref):
        keep = pl.program_id(0) > 0
        for c0 in range(0, half, FFN_COLS):
            vals = []
            for base in (c0, half + c0):
                cols = slice(base, base + FFN_COLS)
                halo = jnp.where(keep, halo_ref[:, cols], 0.0)
                taps = _conv_taps(u_ref[:, cols], halo, 3)
                vals.append(sum(w_ref[k:k + 1, cols] * taps[k] for k in range(3)) + b_ref[:, cols])
            gate, up = vals
            a_ref[:, c0:c0 + FFN_COLS] = (gate * _sigmoid(gate) * up).astype(a_ref.dtype)

    return pl.pallas_call(
        body, name=name, out_shape=jax.ShapeDtypeStruct((s, half), BF16), grid=(s // tm,),
        in_specs=[pl.BlockSpec((SUBLANES, width), lambda i: (0, 0)), pl.BlockSpec((1, width), lambda i: (0, 0)),
                  pl.BlockSpec((tm, width), lambda i: (i, 0)),
                  pl.BlockSpec((SUBLANES, width), lambda i: (jnp.maximum(i * per - 1, 0), 0))],
        out_specs=pl.BlockSpec((tm, half), lambda i: (i, 0)), compiler_params=_params("parallel"),
    )(_pad_taps(conv_w), conv_b.reshape(1, width), u0, u0)


def _ffn_mid_bwd(u0, conv_w, conv_b, da, name):
    s, width = u0.shape
    half = width // 2
    tm = _pick(s, (FFN_ROWS, 128))
    per = tm // SUBLANES
    nt = s // tm

    def body(w_ref, b_ref, u_ref, halo_ref, da_ref, du0_ref, dw_ref, db_ref, carry_ref):
        step = pl.program_id(0)
        first = step == 0
        keep = step < nt - 1
        for c0 in range(0, half, FFN_COLS):
            dav = da_ref[:, c0:c0 + FFN_COLS].astype(F32)
            taps, vals = [], []
            for base in (c0, half + c0):
                cols = slice(base, base + FFN_COLS)
                halo = jnp.where(keep, halo_ref[:, cols], 0.0)
                tp = _conv_taps(u_ref[:, cols], halo, 3)
                taps.append(tp)
                vals.append(sum(w_ref[k:k + 1, cols] * tp[k] for k in range(3)) + b_ref[:, cols])
            gate, up = vals
            sig = _sigmoid(gate)
            dus = [dav * up * sig * (1.0 + gate * (1.0 - sig)), dav * gate * sig]
            for base, tp, du in zip((c0, half + c0), taps, dus):
                cols = slice(base, base + FFN_COLS)
                nxt = jnp.where(first, 0.0, carry_ref[:, cols])
                du0 = (w_ref[2:3, cols] * du + w_ref[1:2, cols] * _shift_up(du, nxt, 1)
                       + w_ref[0:1, cols] * _shift_up(du, nxt, 2))
                du0_ref[:, cols] = du0.astype(du0_ref.dtype)
                carry_ref[:, cols] = du[0:SUBLANES, :]
                dwp = jnp.concatenate([jnp.sum(du * tp[k], axis=0, keepdims=True) for k in range(3)]
                                      + [jnp.zeros((SUBLANES - 3, FFN_COLS), F32)], axis=0)
                dbp = jnp.sum(du, axis=0, keepdims=True)

                @pl.when(first)
                def _():
                    dw_ref[:, cols] = dwp
                    db_ref[:, cols] = dbp

                @pl.when(jnp.logical_not(first))
                def _():
                    dw_ref[:, cols] += dwp
                    db_ref[:, cols] += dbp

    rev = lambda i: nt - 1 - i
    du0, dw, db = pl.pallas_call(
        body, name=name,
        out_shape=[jax.ShapeDtypeStruct((s, width), BF16), jax.ShapeDtypeStruct((SUBLANES, width), F32),
                   jax.ShapeDtypeStruct((1, width), F32)],
        grid=(nt,),
        in_specs=[pl.BlockSpec((SUBLANES, width), lambda i: (0, 0)), pl.BlockSpec((1, width), lambda i: (0, 0)),
                  pl.BlockSpec((tm, width), lambda i: (rev(i), 0)),
                  pl.BlockSpec((SUBLANES, width), lambda i: (jnp.maximum(rev(i) * per - 1, 0), 0)),
                  pl.BlockSpec((tm, half), lambda i: (rev(i), 0))],
        out_specs=[pl.BlockSpec((tm, width), lambda i: (rev(i), 0)),
                   pl.BlockSpec((SUBLANES, width), lambda i: (0, 0)), pl.BlockSpec((1, width), lambda i: (0, 0))],
        scratch_shapes=[pltpu.VMEM((SUBLANES, width), F32)],
        compiler_params=_params("arbitrary"),
    )(_pad_taps(conv_w), conv_b.reshape(1, width), u0, u0, da)
    return du0, dw[:3], db.reshape(width)


@functools.partial(jax.custom_vjp, nondiff_argnums=(3,))
def ffn_mid(u0, conv_w, conv_b, name):
    return _ffn_mid_fwd(u0, conv_w, conv_b, name + "_fwd")


def _ffn_mid_vjp_fwd(u0, conv_w, conv_b, name):
    return _ffn_mid_fwd(u0, conv_w, conv_b, name + "_fwd"), (u0, conv_w, conv_b)


def _ffn_mid_vjp_bwd(name, res, da):
    u0, conv_w, conv_b = res
    return _ffn_mid_bwd(u0, conv_w, conv_b, da, name + "_bwd")


ffn_mid.defvjp(_ffn_mid_vjp_fwd, _ffn_mid_vjp_bwd)


def conv_ffn_p(h, w_up, conv_w, conv_b, w_down, tag):
    u0 = linear(h, w_up, F32, tag + "_up")
    return linear(ffn_mid(u0, conv_w, conv_b, tag + "_mid"), w_down, F32, tag + "_down")


NEG = -1e30
HEAD_SLICES = [slice(hh * ATTN_HEAD_DIM, (hh + 1) * ATTN_HEAD_DIM) for hh in range(ATTN_HEADS)]
ATTN_SCALE = ATTN_HEAD_DIM ** -0.5
QKV_GROUP = 3 * ATTN_OUT_WIDTH


def rope_table(seq):
    cos, sin = rope_tables(seq)
    half = ROPE_DIM // 2
    ones = jnp.ones((seq, ATTN_HEAD_DIM - ROPE_DIM), F32)
    zero = lambda n: jnp.zeros((seq, n), F32)
    return jnp.concatenate([cos, cos, ones, -sin, zero(ATTN_HEAD_DIM - half),
                            zero(half), sin, zero(ATTN_HEAD_DIM - ROPE_DIM)], axis=1)


def _rope(t, tab, sign):
    half = ROPE_DIM // 2
    return t * tab[:, 0:128] + sign * (pltpu.roll(t, ATTN_HEAD_DIM - half, axis=1) * tab[:, 128:256]
                                       + pltpu.roll(t, half, axis=1) * tab[:, 256:384])


def _to_dilated(a, d):
    s = a.shape[0]
    return a if d == 1 else a.reshape(s // d, d, -1).transpose(1, 0, 2).reshape(s, -1)


def _from_dilated(a, d):
    s = a.shape[0]
    return a if d == 1 else a.reshape(d, s // d, -1).transpose(1, 0, 2).reshape(s, -1)


def _rope_qk(qkv, tab, name):
    s = qkv.shape[0]
    tm = _pick(s, (512, 256, 128))

    def body(t_ref, x_ref, o_ref):
        for hs in HEAD_SLICES:
            o_ref[:, hs] = _rope(x_ref[:, hs].astype(F32), t_ref[...], 1.0).astype(o_ref.dtype)

    blk = pl.BlockSpec((tm, ATTN_OUT_WIDTH), lambda i, c: (i, c))
    return pl.pallas_call(
        body, name=name, out_shape=jax.ShapeDtypeStruct(qkv.shape, qkv.dtype), grid=(s // tm, 2),
        in_specs=[pl.BlockSpec((tm, 384), lambda i, c: (i, 0)), blk], out_specs=blk,
        input_output_aliases={1: 0}, compiler_params=_params("parallel", "parallel"),
    )(tab, qkv)


def _dot_nt(a, b):
    return lax.dot_general(a, b, (((1,), (1,)), ((), ())), preferred_element_type=F32)


def _dot_tn(a, b):
    return lax.dot_general(a, b, (((0,), (0,)), ((), ())), preferred_element_type=F32)


def _dot(a, b):
    return jnp.dot(a, b, preferred_element_type=F32)


def _window_mask(has_prev):
    ii = lax.broadcasted_iota(jnp.int32, (ATTN_BLOCK, 2 * ATTN_BLOCK), 0)
    jj = lax.broadcasted_iota(jnp.int32, (ATTN_BLOCK, 2 * ATTN_BLOCK), 1)
    in_window = jnp.logical_and(jj >= ii, jj <= ii + ATTN_BLOCK)
    return jnp.logical_and(in_window, jnp.logical_or(jj >= ATTN_BLOCK, has_prev))


def _both(prev_ref, cur_ref, hs):
    return jnp.concatenate([prev_ref[:, hs], cur_ref[:, hs]], axis=0)


def _attn_group_fwd(qkv, d, name):
    s = qkv.shape[0]
    nb = s // d // ATTN_BLOCK

    def body(q_ref, kc_ref, kp_ref, vc_ref, vp_ref, o_ref, lse_ref):
        mask = _window_mask(pl.program_id(1) > 0)
        lane = lax.broadcasted_iota(jnp.int32, (ATTN_BLOCK, 128), 1)
        lse_tile = jnp.zeros((ATTN_BLOCK, 128), F32)
        for hh, hs in enumerate(HEAD_SLICES):
            sc = jnp.where(mask, _dot_nt(q_ref[:, hs], _both(kp_ref, kc_ref, hs)) * ATTN_SCALE, NEG)
            m = jnp.max(sc, axis=1, keepdims=True)
            p = jnp.exp(sc - m)
            den = jnp.sum(p, axis=1, keepdims=True)
            o_ref[:, hs] = _dot(p.astype(BF16), _both(vp_ref, vc_ref, hs)) / den
            lse_tile = jnp.where(lane == hh, m + jnp.log(den), lse_tile)
        lse_ref[...] = lse_tile

    cur = lambda t: pl.BlockSpec((ATTN_BLOCK, ATTN_OUT_WIDTH), lambda r, n: (r * nb + n, t))
    prv = lambda t: pl.BlockSpec((ATTN_BLOCK, ATTN_OUT_WIDTH), lambda r, n: (r * nb + jnp.maximum(n - 1, 0), t))
    return pl.pallas_call(
        body, name=name,
        out_shape=[jax.ShapeDtypeStruct((s, ATTN_OUT_WIDTH), F32), jax.ShapeDtypeStruct((s, 128), F32)],
        grid=(d, nb), in_specs=[cur(0), cur(1), prv(1), cur(2), prv(2)],
        out_specs=[pl.BlockSpec((ATTN_BLOCK, ATTN_OUT_WIDTH), lambda r, n: (r * nb + n, 0)),
                   pl.BlockSpec((ATTN_BLOCK, 128), lambda r, n: (r * nb + n, 0))],
        compiler_params=_params("parallel", "parallel"),
    )(qkv, qkv, qkv, qkv, qkv)


def _attn_combine(os_, lses, name):
    s = os_[0].shape[0]
    tm = _pick(s, (256, 128))
    ng = len(os_)

    def body(*refs):
        o_refs, l_refs, (o_ref, lse_ref) = refs[:ng], refs[ng:2 * ng], refs[2 * ng:]
        lane = lax.broadcasted_iota(jnp.int32, (tm, 128), 1)
        lse_tile = jnp.zeros((tm, 128), F32)
        for hh, hs in enumerate(HEAD_SLICES):
            ls = [l_ref[:, hh:hh + 1] for l_ref in l_refs]
            m = functools.reduce(jnp.maximum, ls)
            ws = [jnp.exp(l - m) for l in ls]
            tot = functools.reduce(lambda a, b: a + b, ws)
            acc = functools.reduce(lambda a, b: a + b, [o_r[:, hs] * w for o_r, w in zip(o_refs, ws)])
            o_ref[:, hs] = (acc / tot).astype(o_ref.dtype)
            lse_tile = jnp.where(lane == hh, m + jnp.log(tot), lse_tile)
        lse_ref[...] = lse_tile

    wide = pl.BlockSpec((tm, ATTN_OUT_WIDTH), lambda i: (i, 0))
    thin = pl.BlockSpec((tm, 128), lambda i: (i, 0))
    return pl.pallas_call(
        body, name=name,
        out_shape=[jax.ShapeDtypeStruct((s, ATTN_OUT_WIDTH), BF16), jax.ShapeDtypeStruct((s, 128), F32)],
        grid=(s // tm,), in_specs=[wide] * ng + [thin] * ng, out_specs=[wide, thin],
        compiler_params=_params("parallel"),
    )(*os_, *lses)


def _attn_delta(do, o, name):
    s = do.shape[0]
    tm = _pick(s, (256, 128))

    def body(do_ref, o_ref, out_ref):
        lane = lax.broadcasted_iota(jnp.int32, (tm, 128), 1)
        tile = jnp.zeros((tm, 128), F32)
        for hh, hs in enumerate(HEAD_SLICES):
            prod = do_ref[:, hs].astype(F32) * o_ref[:, hs].astype(F32)
            tile = jnp.where(lane == hh, jnp.sum(prod, axis=1, keepdims=True), tile)
        out_ref[...] = tile

    wide = pl.BlockSpec((tm, ATTN_OUT_WIDTH), lambda i: (i, 0))
    return pl.pallas_call(
        body, name=name, out_shape=jax.ShapeDtypeStruct((s, 128), F32), grid=(s // tm,),
        in_specs=[wide, wide], out_specs=pl.BlockSpec((tm, 128), lambda i: (i, 0)),
        compiler_params=_params("parallel"),
    )(do, o)


def _attn_group_bwd(qkv, do, lse, delta, tab, d, name):
    s = qkv.shape[0]
    nb = s // d // ATTN_BLOCK

    def body(q_ref, kc_ref, kp_ref, vc_ref, vp_ref, do_ref, lse_ref, dl_ref, tq_ref, tk_ref,
             dq_ref, dk_ref, dv_ref, ck_ref, cv_ref):
        n = pl.program_id(1)

        @pl.when(n == 0)
        def _():
            ck_ref[...] = jnp.zeros_like(ck_ref)
            cv_ref[...] = jnp.zeros_like(cv_ref)

        @pl.when(n < nb)
        def _():
            mask = _window_mask(n > 0)
            for hh, hs in enumerate(HEAD_SLICES):
                qh, doh = q_ref[:, hs], do_ref[:, hs]
                kh, vh = _both(kp_ref, kc_ref, hs), _both(vp_ref, vc_ref, hs)
                sc = jnp.where(mask, _dot_nt(qh, kh) * ATTN_SCALE, NEG)
                p = jnp.exp(sc - lse_ref[:, hh:hh + 1])
                ds = (p * (_dot_nt(doh, vh) - dl_ref[:, hh:hh + 1]) * ATTN_SCALE).astype(BF16)
                dq_ref[:, hs] = _rope(_dot(ds, kh), tq_ref[...], -1.0).astype(dq_ref.dtype)
                dk = _dot_tn(ds, qh)
                dv = _dot_tn(p.astype(BF16), doh)

                dk_ref[:, hs] = _rope(ck_ref[:, hs] + dk[:ATTN_BLOCK], tk_ref[...], -1.0).astype(dk_ref.dtype)
                dv_ref[:, hs] = (cv_ref[:, hs] + dv[:ATTN_BLOCK]).astype(dv_ref.dtype)
                ck_ref[:, hs] = dk[ATTN_BLOCK:]
                cv_ref[:, hs] = dv[ATTN_BLOCK:]

        @pl.when(n == nb)
        def _():
            for hs in HEAD_SLICES:
                dk_ref[:, hs] = _rope(ck_ref[:, hs], tk_ref[...], -1.0).astype(dk_ref.dtype)
                dv_ref[:, hs] = cv_ref[:, hs].astype(dv_ref.dtype)

    def spec(width, row, col):
        return pl.BlockSpec((ATTN_BLOCK, width), lambda r, n: (r * nb + row(n), col))

    cur = lambda n: jnp.minimum(n, nb - 1)
    prv = lambda n: jnp.maximum(jnp.minimum(n, nb - 1) - 1, 0)
    out = lambda n: jnp.maximum(n - 1, 0)
    wide = ATTN_OUT_WIDTH
    return pl.pallas_call(
        body, name=name,
        out_shape=[jax.ShapeDtypeStruct((s, wide), BF16)] * 3,
        grid=(d, nb + 1),
        in_specs=[spec(wide, cur, 0), spec(wide, cur, 1), spec(wide, prv, 1), spec(wide, cur, 2),
                  spec(wide, prv, 2), spec(wide, cur, 0), spec(128, cur, 0), spec(128, cur, 0),
                  spec(384, cur, 0), spec(384, out, 0)],
        out_specs=[spec(wide, cur, 0), spec(wide, out, 0), spec(wide, out, 0)],
        scratch_shapes=[pltpu.VMEM((ATTN_BLOCK, wide), F32), pltpu.VMEM((ATTN_BLOCK, wide), F32)],
        compiler_params=_params("parallel", "arbitrary"),
    )(qkv, qkv, qkv, qkv, qkv, do, lse, delta, tab, tab)


def _attn_core_fwd(qkvs, name):
    tab = rope_table(qkvs[0].shape[0])
    rot, os_, lses = [], [], []
    for g, (qkv, d) in enumerate(zip(qkvs, ATTN_DILATIONS)):
        qkv = _rope_qk(qkv, _to_dilated(tab, d), f"{name}_rope{g}")
        o_g, lse_g = _attn_group_fwd(qkv, d, f"{name}_fwd{g}")
        rot.append(qkv)
        os_.append(_from_dilated(o_g, d))
        lses.append(_from_dilated(lse_g, d))
    o, lse = _attn_combine(os_, lses, name + "_combine")
    return o, (tuple(rot), o, lse)


@functools.partial(jax.custom_vjp, nondiff_argnums=(1,))
def attn_core(qkvs, name):
    return _attn_core_fwd(qkvs, name)[0]


def _attn_core_vjp_fwd(qkvs, name):
    return _attn_core_fwd(qkvs, name)


def _attn_core_vjp_bwd(name, res, do):
    rot, o, lse = res
    tab = rope_table(o.shape[0])
    delta = _attn_delta(do, o, name + "_delta")
    out = []
    for g, (qkv, d) in enumerate(zip(rot, ATTN_DILATIONS)):
        parts = _attn_group_bwd(qkv, _to_dilated(do, d), _to_dilated(lse, d), _to_dilated(delta, d),
                                _to_dilated(tab, d), d, f"{name}_bwd{g}")
        out.append(jnp.concatenate(parts, axis=1))
    return (tuple(out),)


attn_core.defvjp(_attn_core_vjp_fwd, _attn_core_vjp_bwd)


def attention_mixer_p(x, h, lin, j, carries, tag):
    qkvs = tuple(lin(_to_dilated(h, d), ('attn_w_qkv', j), BF16, f"{tag}_qkv{g}",
                     cols=slice(g * QKV_GROUP, (g + 1) * QKV_GROUP), carry=carries[g])
                 for g, d in enumerate(ATTN_DILATIONS))
    return lin(attn_core(qkvs, tag), ('attn_w_o', j), F32, tag + "_o", res=x)


SSM_CONV_TAPS = 4
SSM_COL_BLOCK = 2048
SSM_PAIRS = SSM_HEADS // 2
SSM_DT_BLOCK = (SSM_D_INNER + SSM_CONV_DIM) // 128


def _ssm_conv_fwd(zx, conv_w, conv_b, name):
    s = zx.shape[0]
    tm = _pick(s, (256, 128))
    per = tm // SUBLANES
    ncb = SSM_CONV_DIM // SSM_COL_BLOCK

    def body(w_ref, b_ref, x_ref, halo_ref, o_ref):
        keep = pl.program_id(1) > 0
        for c0 in range(0, SSM_COL_BLOCK, FFN_COLS):
            cols = slice(c0, c0 + FFN_COLS)
            halo = jnp.where(keep, halo_ref[:, cols], 0.0)
            taps = _conv_taps(x_ref[:, cols], halo, SSM_CONV_TAPS)
            pre = sum(w_ref[k:k + 1, cols] * taps[k] for k in range(SSM_CONV_TAPS)) + b_ref[:, cols]
            o_ref[:, cols] = pre * _sigmoid(pre)

    return pl.pallas_call(
        body, name=name, out_shape=jax.ShapeDtypeStruct((s, SSM_CONV_DIM), F32), grid=(ncb, s // tm),
        in_specs=[pl.BlockSpec((SUBLANES, SSM_COL_BLOCK), lambda j, i: (0, j)),
                  pl.BlockSpec((1, SSM_COL_BLOCK), lambda j, i: (0, j)),
                  pl.BlockSpec((tm, SSM_COL_BLOCK), lambda j, i: (i, j + 1)),
                  pl.BlockSpec((SUBLANES, SSM_COL_BLOCK), lambda j, i: (jnp.maximum(i * per - 1, 0), j + 1))],
        out_specs=pl.BlockSpec((tm, SSM_COL_BLOCK), lambda j, i: (i, j)),
        compiler_params=_params("parallel", "parallel"),
    )(_pad_taps(conv_w), conv_b.reshape(1, SSM_CONV_DIM), zx, zx)


def _ssm_conv_bwd(zx, conv_w, conv_b, dact, name):
    s = zx.shape[0]
    tm = _pick(s, (256, 128))
    per = tm // SUBLANES
    nt = s // tm
    ncb = SSM_CONV_DIM // SSM_COL_BLOCK
    nk = SSM_CONV_TAPS

    def body(w_ref, b_ref, x_ref, halo_ref, da_ref, dx_ref, dw_ref, db_ref, carry_ref):
        step = pl.program_id(1)
        first = step == 0
        keep = step < nt - 1
        for c0 in range(0, SSM_COL_BLOCK, FFN_COLS):
            cols = slice(c0, c0 + FFN_COLS)
            halo = jnp.where(keep, halo_ref[:, cols], 0.0)
            taps = _conv_taps(x_ref[:, cols], halo, nk)
            pre = sum(w_ref[k:k + 1, cols] * taps[k] for k in range(nk)) + b_ref[:, cols]
            sig = _sigmoid(pre)
            dpre = da_ref[:, cols] * sig * (1.0 + pre * (1.0 - sig))
            nxt = jnp.where(first, 0.0, carry_ref[:, cols])
            dx = w_ref[nk - 1:nk, cols] * dpre
            for k in range(nk - 1):
                dx = dx + w_ref[k:k + 1, cols] * _shift_up(dpre, nxt, nk - 1 - k)
            dx_ref[:, cols] = dx.astype(dx_ref.dtype)
            carry_ref[:, cols] = dpre[0:SUBLANES, :]
            dwp = jnp.concatenate([jnp.sum(dpre * taps[k], axis=0, keepdims=True) for k in range(nk)]
                                  + [jnp.zeros((SUBLANES - nk, FFN_COLS), F32)], axis=0)
            dbp = jnp.sum(dpre, axis=0, keepdims=True)

            @pl.when(first)
            def _():
                dw_ref[:, cols] = dwp
                db_ref[:, cols] = dbp

            @pl.when(jnp.logical_not(first))
            def _():
                dw_ref[:, cols] += dwp
                db_ref[:, cols] += dbp

    rev = lambda i: nt - 1 - i
    dx, dw, db = pl.pallas_call(
        body, name=name,
        out_shape=[jax.ShapeDtypeStruct((s, SSM_CONV_DIM), BF16), jax.ShapeDtypeStruct((SUBLANES, SSM_CONV_DIM), F32),
                   jax.ShapeDtypeStruct((1, SSM_CONV_DIM), F32)],
        grid=(ncb, nt),
        in_specs=[pl.BlockSpec((SUBLANES, SSM_COL_BLOCK), lambda j, i: (0, j)),
                  pl.BlockSpec((1, SSM_COL_BLOCK), lambda j, i: (0, j)),
                  pl.BlockSpec((tm, SSM_COL_BLOCK), lambda j, i: (rev(i), j + 1)),
                  pl.BlockSpec((SUBLANES, SSM_COL_BLOCK), lambda j, i: (jnp.maximum(rev(i) * per - 1, 0), j + 1)),
                  pl.BlockSpec((tm, SSM_COL_BLOCK), lambda j, i: (rev(i), j))],
        out_specs=[pl.BlockSpec((tm, SSM_COL_BLOCK), lambda j, i: (rev(i), j)),
                   pl.BlockSpec((SUBLANES, SSM_COL_BLOCK), lambda j, i: (0, j)),
                   pl.BlockSpec((1, SSM_COL_BLOCK), lambda j, i: (0, j))],
        scratch_shapes=[pltpu.VMEM((SUBLANES, SSM_COL_BLOCK), F32)],
        compiler_params=_params("parallel", "arbitrary"),
    )(_pad_taps(conv_w), conv_b.reshape(1, SSM_CONV_DIM), zx, zx, dact)
    return dx, dw[:nk], db.reshape(SSM_CONV_DIM)


def _ssd_chunk(xs, bms, cms, dt_raw, dtb, alog, dsk, states):
    q = SSM_CHUNK
    lane = lax.broadcasted_iota(jnp.int32, (1, 128), 1)
    row = lax.broadcasted_iota(jnp.int32, (q, 1), 0)
    ii = lax.broadcasted_iota(jnp.int32, (q, q), 0)
    jj = lax.broadcasted_iota(jnp.int32, (q, q), 1)
    tril = ii >= jj
    left = lane < SSM_HEAD_DIM
    last_row = (row == q - 1).astype(F32)

    def lanes_of(mat, h):
        pick = (lane == h).astype(F32)
        return jnp.broadcast_to(jnp.sum(mat * pick, axis=1, keepdims=True), mat.shape)

    def rows_of(mat_t, h):
        pick = (row == h).astype(F32)
        return jnp.broadcast_to(jnp.sum(mat_t * pick, axis=0, keepdims=True), mat_t.shape)

    v = dt_raw + dtb
    dt = jnp.maximum(v, 0.0) + jnp.log(1.0 + jnp.exp(-jnp.abs(v)))
    adt = dt * (-jnp.exp(alog))
    acs = jnp.dot(tril.astype(F32), adt, precision=lax.Precision.HIGHEST, preferred_element_type=F32)
    acs_t = acs.T
    ys, new_states = [], []
    for pr in range(SSM_PAIRS):
        g = pr // 2
        if pr % 2 == 0:
            cb = _dot_nt(cms[g].astype(BF16), bms[g].astype(BF16))
        cols = [lanes_of(acs, 2 * pr + e) for e in range(2)]
        dts = [lanes_of(dt, 2 * pr + e) for e in range(2)]
        rws = [rows_of(acs_t, 2 * pr + e) for e in range(2)]
        lasts = [jnp.sum(c * last_row, axis=0, keepdims=True) for c in cols]
        xdt = xs[pr] * jnp.where(left, dts[0], dts[1])
        halves = [jnp.where(left, xdt, 0.0).astype(BF16), jnp.where(left, 0.0, xdt).astype(BF16)]
        y_diag, s_new = 0.0, 0.0
        for e in range(2):
            lmat = jnp.where(tril, jnp.exp(jnp.minimum(cols[e] - rws[e], 0.0)), 0.0)
            y_diag = y_diag + _dot((cb * lmat).astype(BF16), halves[e])
            decay = jnp.exp(lasts[e] - cols[e])
            s_new = s_new + _dot_tn((bms[g] * decay).astype(BF16), halves[e])
        y_off = _dot(cms[g].astype(BF16), states[pr].astype(BF16)) * jnp.where(left, jnp.exp(cols[0]), jnp.exp(cols[1]))
        skip = jnp.where(left, lanes_of(dsk, 2 * pr), lanes_of(dsk, 2 * pr + 1))
        ys.append(y_diag + y_off + xs[pr] * skip)
        new_states.append(states[pr] * jnp.where(left, jnp.exp(lasts[0]), jnp.exp(lasts[1])) + s_new)
    return tuple(ys), tuple(new_states)


def _ssm_vec(v):
    return jnp.pad(v.reshape(1, -1), ((0, 0), (0, 128 - v.shape[0])))


def _ssd_scan_fwd(act, zx, dtb, alog, dsk, name):
    s = act.shape[0]
    nc = s // SSM_CHUNK
    ng = SSM_GROUPS

    def body(act_ref, dt_ref, dtb_ref, alog_ref, dsk_ref, y_ref, st_out_ref, st_ref):
        @pl.when(pl.program_id(0) == 0)
        def _():
            st_ref[...] = jnp.zeros_like(st_ref)

        tile = lambda k: act_ref[:, k * 128:(k + 1) * 128]
        xs = [tile(k) for k in range(SSM_PAIRS)]
        bms = [tile(SSM_PAIRS + k) for k in range(ng)]
        cms = [tile(SSM_PAIRS + ng + k) for k in range(ng)]
        states = [st_ref[k] for k in range(SSM_PAIRS)]
        st_out_ref[0] = st_ref[...]
        ys, new_states = _ssd_chunk(xs, bms, cms, dt_ref[...], dtb_ref[...], alog_ref[...], dsk_ref[...], states)
        for k in range(SSM_PAIRS):
            y_ref[:, k * 128:(k + 1) * 128] = ys[k]
            st_ref[k] = new_states[k]

    vec = pl.BlockSpec((1, 128), lambda c: (0, 0))
    return pl.pallas_call(
        body, name=name,
        out_shape=[jax.ShapeDtypeStruct((s, SSM_D_INNER), F32),
                   jax.ShapeDtypeStruct((nc, SSM_PAIRS, SSM_STATE, 128), F32)],
        grid=(nc,),
        in_specs=[pl.BlockSpec((SSM_CHUNK, SSM_CONV_DIM), lambda c: (c, 0)),
                  pl.BlockSpec((SSM_CHUNK, 128), lambda c: (c, SSM_DT_BLOCK)), vec, vec, vec],
        out_specs=[pl.BlockSpec((SSM_CHUNK, SSM_D_INNER), lambda c: (c, 0)),
                   pl.BlockSpec((1, SSM_PAIRS, SSM_STATE, 128), lambda c: (c, 0, 0, 0))],
        scratch_shapes=[pltpu.VMEM((SSM_PAIRS, SSM_STATE, 128), F32)],
        compiler_params=_params("arbitrary"),
    )(act, zx, _ssm_vec(dtb), _ssm_vec(alog), _ssm_vec(dsk))


def _ssd_scan_bwd(act, zx, dtb, alog, dsk, st_in, dy, name):
    s = act.shape[0]
    nc = s // SSM_CHUNK
    ng = SSM_GROUPS

    def body(act_ref, dt_ref, dtb_ref, alog_ref, dsk_ref, st_ref, dy_ref, dact_ref, ddt_ref, dpar_ref, dst_ref):
        first = pl.program_id(0) == 0

        @pl.when(first)
        def _():
            dst_ref[...] = jnp.zeros_like(dst_ref)

        tile = lambda k: act_ref[:, k * 128:(k + 1) * 128]
        xs = [tile(k) for k in range(SSM_PAIRS)]
        bms = [tile(SSM_PAIRS + k) for k in range(ng)]
        cms = [tile(SSM_PAIRS + ng + k) for k in range(ng)]
        states = [st_ref[0, k] for k in range(SSM_PAIRS)]
        _, pullback = jax.vjp(_ssd_chunk, xs, bms, cms, dt_ref[...], dtb_ref[...], alog_ref[...], dsk_ref[...],
                              states)
        dys = tuple(dy_ref[:, k * 128:(k + 1) * 128] for k in range(SSM_PAIRS))
        dsts = tuple(dst_ref[k] for k in range(SSM_PAIRS))
        dxs, dbms, dcms, ddt, ddtb, dalog, ddsk, dstates = pullback((dys, dsts))
        for k, t in enumerate(list(dxs) + list(dbms) + list(dcms)):
            dact_ref[:, k * 128:(k + 1) * 128] = t
        ddt_ref[...] = ddt.astype(ddt_ref.dtype)
        for k in range(SSM_PAIRS):
            dst_ref[k] = dstates[k]
        dpar = jnp.concatenate([ddtb, dalog, ddsk, jnp.zeros((SUBLANES - 3, 128), F32)], axis=0)
        _accumulate(dpar_ref, dpar, first)

    rev = lambda c: nc - 1 - c
    vec = pl.BlockSpec((1, 128), lambda c: (0, 0))
    dact, ddt, dpar = pl.pallas_call(
        body, name=name,
        out_shape=[jax.ShapeDtypeStruct((s, SSM_CONV_DIM), F32), jax.ShapeDtypeStruct((s, 128), BF16),
                   jax.ShapeDtypeStruct((SUBLANES, 128), F32)],
        grid=(nc,),
        in_specs=[pl.BlockSpec((SSM_CHUNK, SSM_CONV_DIM), lambda c: (rev(c), 0)),
                  pl.BlockSpec((SSM_CHUNK, 128), lambda c: (rev(c), SSM_DT_BLOCK)), vec, vec, vec,
                  pl.BlockSpec((1, SSM_PAIRS, SSM_STATE, 128), lambda c: (rev(c), 0, 0, 0)),
                  pl.BlockSpec((SSM_CHUNK, SSM_D_INNER), lambda c: (rev(c), 0))],
        out_specs=[pl.BlockSpec((SSM_CHUNK, SSM_CONV_DIM), lambda c: (rev(c), 0)),
                   pl.BlockSpec((SSM_CHUNK, 128), lambda c: (rev(c), 0)),
                   pl.BlockSpec((SUBLANES, 128), lambda c: (0, 0))],
        scratch_shapes=[pltpu.VMEM((SSM_PAIRS, SSM_STATE, 128), F32)],
        compiler_params=_params("arbitrary"),
    )(act, zx, _ssm_vec(dtb), _ssm_vec(alog), _ssm_vec(dsk), st_in, dy)
    return dact, ddt, dpar[0, :SSM_HEADS], dpar[1, :SSM_HEADS], dpar[2, :SSM_HEADS]


SSM_NORM_GROUP = SSM_D_INNER // SSM_GROUPS


def _gated_group(y, z, w):
    g = y * (z * _sigmoid(z))
    return g * lax.rsqrt(jnp.mean(g * g, axis=-1, keepdims=True) + NORM_EPS) * w


def _gated_norm_fwd(y, zx, w, name):
    s = y.shape[0]
    tm = _pick(s, (256, 128))

    def body(y_ref, z_ref, w_ref, o_ref):
        for c0 in range(0, SSM_D_INNER, SSM_NORM_GROUP):
            cols = slice(c0, c0 + SSM_NORM_GROUP)
            o_ref[:, cols] = _gated_group(y_ref[:, cols], z_ref[:, cols], w_ref[:, cols]).astype(o_ref.dtype)

    blk = pl.BlockSpec((tm, SSM_D_INNER), lambda i: (i, 0))
    return pl.pallas_call(
        body, name=name, out_shape=jax.ShapeDtypeStruct((s, SSM_D_INNER), BF16), grid=(s // tm,),
        in_specs=[blk, blk, pl.BlockSpec((1, SSM_D_INNER), lambda i: (0, 0))], out_specs=blk,
        compiler_params=_params("parallel"),
    )(y, zx, w.reshape(1, SSM_D_INNER))


def _gated_norm_bwd(y, zx, w, dout, name):
    s = y.shape[0]
    tm = _pick(s, (256, 128))

    def body(y_ref, z_ref, w_ref, do_ref, dy_ref, dz_ref, dw_ref):
        first = pl.program_id(0) == 0
        for c0 in range(0, SSM_D_INNER, SSM_NORM_GROUP):
            cols = slice(c0, c0 + SSM_NORM_GROUP)
            _, pullback = jax.vjp(_gated_group, y_ref[:, cols], z_ref[:, cols], w_ref[:, cols])
            dyv, dzv, dwv = pullback(do_ref[:, cols].astype(F32))
            dy_ref[:, cols] = dyv
            dz_ref[:, cols] = dzv.astype(dz_ref.dtype)

            @pl.when(first)
            def _():
                dw_ref[:, cols] = dwv

            @pl.when(jnp.logical_not(first))
            def _():
                dw_ref[:, cols] += dwv

    blk = pl.BlockSpec((tm, SSM_D_INNER), lambda i: (i, 0))
    vec = pl.BlockSpec((1, SSM_D_INNER), lambda i: (0, 0))
    dy, dz, dw = pl.pallas_call(
        body, name=name,
        out_shape=[jax.ShapeDtypeStruct((s, SSM_D_INNER), F32), jax.ShapeDtypeStruct((s, SSM_D_INNER), BF16),
                   jax.ShapeDtypeStruct((1, SSM_D_INNER), F32)],
        grid=(s // tm,), in_specs=[blk, blk, vec, blk], out_specs=[blk, blk, vec],
        compiler_params=_params("arbitrary"),
    )(y, zx, w.reshape(1, SSM_D_INNER), dout)
    return dy, dz, dw.reshape(SSM_D_INNER)


def _ssm_core_fwd(zx, conv_w, conv_b, dtb, alog, dsk, norm_w, name):
    act = _ssm_conv_fwd(zx, conv_w, conv_b, name + "_conv_fwd")
    y, st_in = _ssd_scan_fwd(act, zx, dtb, alog, dsk, name + "_scan_fwd")
    out = _gated_norm_fwd(y, zx, norm_w, name + "_gate_fwd")
    return out, (zx, conv_w, conv_b, dtb, alog, dsk, norm_w, act, y, st_in)


@functools.partial(jax.custom_vjp, nondiff_argnums=(7,))
def ssm_core(zx, conv_w, conv_b, dtb, alog, dsk, norm_w, name):
    return _ssm_core_fwd(zx, conv_w, conv_b, dtb, alog, dsk, norm_w, name)[0]


def _ssm_core_vjp_fwd(zx, conv_w, conv_b, dtb, alog, dsk, norm_w, name):
    return _ssm_core_fwd(zx, conv_w, conv_b, dtb, alog, dsk, norm_w, name)


def _ssm_core_vjp_bwd(name, res, dout):
    zx, conv_w, conv_b, dtb, alog, dsk, norm_w, act, y, st_in = res
    dy, dz, dnorm_w = _gated_norm_bwd(y, zx, norm_w, dout, name + "_gate_bwd")
    dact, ddt, ddtb, dalog, ddsk = _ssd_scan_bwd(act, zx, dtb, alog, dsk, st_in, dy, name + "_scan_bwd")
    dxbc, dconv_w, dconv_b = _ssm_conv_bwd(zx, conv_w, conv_b, dact, name + "_conv_bwd")
    dzx = jnp.concatenate([dz, dxbc, ddt], axis=1)
    return dzx, dconv_w, dconv_b, ddtb, dalog, ddsk, dnorm_w


ssm_core.defvjp(_ssm_core_vjp_fwd, _ssm_core_vjp_bwd)


def ssd_mixer_p(x, h, lin, j, carries, conv_w, conv_b, dtb, alog, dsk, norm_w, tag):
    zx = lin(h, ('ssm_w_in', j), F32, tag + "_in", carry=carries[0])
    return lin(ssm_core(zx, conv_w, conv_b, dtb, alog, dsk, norm_w, tag), ('ssm_w_out', j), F32, tag + "_out",
               carry=carries[1], res=x)


def _full_weight(n, parts):
    full = _join8(parts, SHARD_AXIS[n] - 1)
    if n == 'ssm_w_in':
        full = jnp.pad(full, ((0, 0), (0, SSM_IN_PAD - SSM_IN_WIDTH)))
    return full


def trunk(w, x):
    ready = {('attn_w_qkv', 0): _full_weight('attn_w_qkv', gather_x(w['attn_w_qkv'][0], "gather_first"))}

    def lin(a, key, out_dtype, name, cols=None, carry=None, res=None):
        wt = ready[key] if cols is None else ready[key][:, cols]
        if carry is None:
            return linear(a, wt, res, out_dtype, name)
        y, parts = linear_x(a, wt, res, w[carry[0]][carry[1]], out_dtype, name)
        ready[carry] = _full_weight(carry[0], parts)
        return y

    for i in range(DEPTH):
        x, h = norm(x, w['mix_norm_w'][i], f"mixnorm{i}")
        j = i // 2
        ffn_next = [('ffn_w_up', i), ('ffn_w_down', i)]
        if i % 2 == 0:
            x = attention_mixer_p(x, h, lin, j, [('attn_w_o', j)] + ffn_next, f"attn{j}")
        else:
            x = ssd_mixer_p(x, h, lin, j, ffn_next, w['ssm_conv_w'][j], w['ssm_conv_b'][j], w['ssm_dt_bias'][j],
                            w['ssm_a_log'][j], w['ssm_d'][j], w['ssm_norm_w'][j], f"ssm{j}")
        if i + 1 == DEPTH:
            mixer_next = [None, None]
        elif i % 2 == 0:
            mixer_next = [('ssm_w_in', j), ('ssm_w_out', j)]
        else:
            mixer_next = [('attn_w_qkv', j + 1), None]
        x, h = norm(x, w['ffn_norm_w'][i], f"ffnnorm{i}")
        u0 = lin(h, ('ffn_w_up', i), F32, f"ffn{i}_up", carry=mixer_next[0])
        a = ffn_mid(u0, w['ffn_conv_w'][i], w['ffn_conv_b'][i], f"ffn{i}_mid")
        x = lin(a, ('ffn_w_down', i), F32, f"ffn{i}_down", carry=mixer_next[1], res=x)
    return x


def local_step(w, x, target):
    final_w = w['final_norm_w']
    trunk_w = {n: a for n, a in w.items() if n != 'final_norm_w'}
    xf, pullback = jax.vjp(trunk, trunk_w, x)
    loss, dxf, dfinal = loss_head(xf, final_w, target, "loss_head")
    gw, gx = pullback(dxf)
    gw['final_norm_w'] = dfinal
    return loss, gw, gx


def _adam_math(w, g, m, v):
    m = ADAM_B1 * m + (1.0 - ADAM_B1) * g
    v = ADAM_B2 * v + (1.0 - ADAM_B2) * (g * g)
    m_hat = m / (1.0 - ADAM_B1 ** ADAM_STEP)
    v_hat = v / (1.0 - ADAM_B2 ** ADAM_STEP)
    delta = -ADAM_LR * (m_hat / (jnp.sqrt(v_hat) + ADAM_EPS) + ADAM_WD * w)
    return delta, m, v


def _adamw_rows(g, w, m, v, name):
    r, c = w.shape
    tr = _pick(r, (256, 128, 64, 32, 16, 8))

    def body(g_ref, w_ref, m_ref, v_ref, d_out, m_out, v_out):
        delta, mm, vv = _adam_math(w_ref[...], g_ref[...], m_ref[...], v_ref[...])
        d_out[...] = delta
        m_out[...] = mm
        v_out[...] = vv

    blk = pl.BlockSpec((tr, c), lambda i: (i, 0))
    return pl.pallas_call(
        body, name=name, out_shape=[jax.ShapeDtypeStruct((r, c), F32)] * 3, grid=(r // tr,),
        in_specs=[blk] * 4, out_specs=[blk] * 3, compiler_params=_params("parallel"),
    )(g, w, m, v)


def _sum8(pieces, name):
    _, r, c = pieces.shape

    def body(p_ref, o_ref):
        g = p_ref[0]
        for j in range(1, N_DEV):
            g = g + p_ref[j]
        o_ref[...] = g

    return pl.pallas_call(
        body, name=name, out_shape=jax.ShapeDtypeStruct((r, c), F32),
        in_specs=[pl.BlockSpec(memory_space=pltpu.VMEM)], out_specs=pl.BlockSpec(memory_space=pltpu.VMEM),
    )(pieces)


def _adamw_plain(g, w, m, v, name):
    def body(g_ref, w_ref, m_ref, v_ref, d_out, m_out, v_out):
        delta, mm, vv = _adam_math(w_ref[...], g_ref[...], m_ref[...], v_ref[...])
        d_out[...] = delta
        m_out[...] = mm
        v_out[...] = vv

    vm = pl.BlockSpec(memory_space=pltpu.VMEM)
    return pl.pallas_call(
        body, name=name, out_shape=[jax.ShapeDtypeStruct(g.shape, F32)] * 3,
        in_specs=[vm] * 4, out_specs=[vm] * 3,
    )(g, w, m, v)


def _split8(full, axis):
    shp = full.shape
    t = full.reshape(shp[:axis] + (N_DEV, shp[axis] // N_DEV) + shp[axis + 1:])
    return jnp.moveaxis(t, axis, 0)


def _join8(parts, axis):
    t = jnp.moveaxis(parts, 0, axis)
    shp = t.shape
    return t.reshape(shp[:axis] + (shp[axis] * shp[axis + 1],) + shp[axis + 2:])


def _pack(arrs, lead, mult):
    flat = jnp.concatenate([a.reshape(a.shape[:lead] + (-1,)) for a in arrs], axis=-1)
    return _pad_rows(flat, mult)


def _unpack(buf, shapes, lead):
    flat = buf.reshape(buf.shape[:lead] + (-1,))
    out, off = [], 0
    for shp in shapes:
        n = math.prod(shp)
        out.append(flat[..., off:off + n].reshape(flat.shape[:lead] + tuple(shp)))
        off += n
    return out


def _own_shard(full, axis):
    size = full.shape[axis] // N_DEV
    return lax.dynamic_slice_in_dim(full, _my_index() * size, size, axis)


def kernel(x, mix_norm_w, attn_w_qkv, attn_w_o, ssm_w_in, ssm_conv_w, ssm_conv_b, ssm_dt_bias, ssm_a_log, ssm_d, ssm_norm_w, ssm_w_out, ffn_norm_w, ffn_w_up, ffn_conv_w, ffn_conv_b, ffn_w_down, final_norm_w, loss_target, m_mix_norm_w, m_attn_w_qkv, m_attn_w_o, m_ssm_w_in, m_ssm_conv_w, m_ssm_conv_b, m_ssm_dt_bias, m_ssm_a_log, m_ssm_d, m_ssm_norm_w, m_ssm_w_out, m_ffn_norm_w, m_ffn_w_up, m_ffn_conv_w, m_ffn_conv_b, m_ffn_w_down, m_final_norm_w, v_mix_norm_w, v_attn_w_qkv, v_attn_w_o, v_ssm_w_in, v_ssm_conv_w, v_ssm_conv_b, v_ssm_dt_bias, v_ssm_a_log, v_ssm_d, v_ssm_norm_w, v_ssm_w_out, v_ffn_norm_w, v_ffn_w_up, v_ffn_conv_w, v_ffn_conv_b, v_ffn_w_down, v_final_norm_w):
    args = dict(locals())
    w_sh = {n: args[n] for n in WEIGHT_NAMES}
    m_sh = {n: args["m_" + n] for n in WEIGHT_NAMES}
    v_sh = {n: args["v_" + n] for n in WEIGHT_NAMES}

    small_shapes = [w_sh[n].shape for n in SMALL_SHARDED]
    small = _exchange(_pack([w_sh[n] for n in SMALL_SHARDED], 0, 8), True, "gather_small")
    full = {n: w_sh[n] for n in SMALL if SHARD_AXIS[n] is None}
    for n, parts in zip(SMALL_SHARDED, _unpack(small, small_shapes, 1)):
        full[n] = _join8(parts, SHARD_AXIS[n])
    for n in BIG:
        full[n] = [w_sh[n][j] for j in range(w_sh[n].shape[0])]

    loss, gw, gx = local_step(full, x[0], loss_target[0])
    loss = lax.psum(loss, ("x", "y", "c"))
    for n in BIG:
        gw[n] = jnp.stack(gw[n])

    grads, deltas, new_m, new_v = {}, {}, {}, {}
    for n in BIG:
        shp = w_sh[n].shape
        two_d = (shp[0] * shp[1], shp[2])
        outs = _adamw_rows(*[t.reshape(two_d) for t in (gw[n], w_sh[n], m_sh[n], v_sh[n])], "adamw_" + n)
        grads[n] = gw[n]
        deltas[n], new_m[n], new_v[n] = [o.reshape(shp) for o in outs]

    small_full_shapes = [gw[n].shape for n in SMALL]
    gsmall = _exchange(_pack([gw[n] for n in SMALL], 0, 8), True, "gather_small_grads")
    gsmall = _unpack(_sum8(gsmall, "sum_small_grads"), small_full_shapes, 0)
    for n, g in zip(SMALL, gsmall):
        grads[n] = g if SHARD_AXIS[n] is None else _own_shard(g, SHARD_AXIS[n])
    shapes = [w_sh[n].shape for n in SMALL]
    outs = _adamw_plain(*[_pack([d[n] for n in SMALL], 0, 8) for d in (grads, w_sh, m_sh, v_sh)], "adamw_small")
    for d, buf in zip((deltas, new_m, new_v), outs):
        for n, a in zip(SMALL, _unpack(buf, shapes, 0)):
            d[n] = a

    return (loss, gx[None], *[grads[n] for n in WEIGHT_NAMES], *[deltas[n] for n in WEIGHT_NAMES],
            *[new_m[n] for n in WEIGHT_NAMES], *[new_v[n] for n in WEIGHT_NAMES])
```

```python
import functools
import math

import jax
import jax.numpy as jnp
from jax import lax
from jax.experimental import pallas as pl
from jax.experimental.pallas import tpu as pltpu

F32 = jnp.float32
BF16 = jnp.bfloat16
N_DEV = 8
MESH_ID = pl.DeviceIdType.MESH

D_MODEL = 1024
DEPTH = 4
ATTN_HEADS = 8
ATTN_HEAD_DIM = 128
ATTN_DILATIONS = (1, 4, 16)
ATTN_STEPS = (128, 128, 128)
N_ATTN_GROUPS = 3
ATTN_BLOCK = 128
ROPE_THETA = 500000.0
ROPE_DIM = 32
ATTN_OUT_WIDTH = 1024
SSM_D_INNER = 2048
SSM_HEAD_DIM = 64
SSM_HEADS = 32
SSM_STATE = 128
SSM_GROUPS = 8
SSM_CHUNK = 128
SSM_CONV_DIM = 4096
SSM_IN_WIDTH = 6176
SSM_IN_PAD = 6272
D_FF = 2816
NORM_EPS = 1e-5
ADAM_LR = 0.001
ADAM_B1 = 0.9
ADAM_B2 = 0.999
ADAM_EPS = 1e-08
ADAM_WD = 0.01
ADAM_STEP = 10

WEIGHT_NAMES = ['mix_norm_w', 'attn_w_qkv', 'attn_w_o', 'ssm_w_in', 'ssm_conv_w', 'ssm_conv_b', 'ssm_dt_bias',
                'ssm_a_log', 'ssm_d', 'ssm_norm_w', 'ssm_w_out', 'ffn_norm_w', 'ffn_w_up', 'ffn_conv_w',
                'ffn_conv_b', 'ffn_w_down', 'final_norm_w']
SHARD_AXIS = {'mix_norm_w': None, 'attn_w_qkv': 2, 'attn_w_o': 1, 'ssm_w_in': 2, 'ssm_conv_w': 2, 'ssm_conv_b': 1,
              'ssm_dt_bias': None, 'ssm_a_log': None, 'ssm_d': None, 'ssm_norm_w': 1, 'ssm_w_out': 1,
              'ffn_norm_w': None, 'ffn_w_up': 2, 'ffn_conv_w': 2, 'ffn_conv_b': None, 'ffn_w_down': 1,
              'final_norm_w': None}
BIG = ['attn_w_qkv', 'attn_w_o', 'ssm_w_in', 'ssm_w_out', 'ffn_w_up', 'ffn_w_down']
SMALL = [n for n in WEIGHT_NAMES if n not in BIG]
SMALL_SHARDED = [n for n in SMALL if SHARD_AXIS[n] is not None]
LANES = 1024


def _my_index():
    return 4 * lax.axis_index("x") + 2 * lax.axis_index("y") + lax.axis_index("c")


def _peer(k):
    x, y, c = lax.axis_index("x"), lax.axis_index("y"), lax.axis_index("c")
    return (x ^ ((k >> 2) & 1), y ^ ((k >> 1) & 1), c ^ (k & 1))


def _exchange(src, gather, name):
    def body(src_ref, out_ref, send_sems, recv_sems, local_sem):
        start, wait = _exchange_copies(src_ref, out_ref, send_sems, recv_sems, local_sem, gather)
        start()
        wait()

    return pl.pallas_call(
        body, name=name,
        out_shape=_exchange_out(src, gather),
        in_specs=[pl.BlockSpec(memory_space=pl.ANY)],
        out_specs=pl.BlockSpec(memory_space=pl.ANY),
        scratch_shapes=list(EXCHANGE_SEMS),
    )(src)


EXCHANGE_SEMS = (pltpu.SemaphoreType.DMA((N_DEV - 1,)), pltpu.SemaphoreType.DMA((N_DEV - 1,)),
                 pltpu.SemaphoreType.DMA)


def _exchange_out(src, gather):
    return jax.ShapeDtypeStruct((N_DEV,) + (src.shape if gather else src.shape[1:]), src.dtype)


def _exchange_copies(src_ref, out_ref, send_sems, recv_sems, local_sem, gather):
    me = _my_index()

    def piece(j):
        return src_ref if gather else src_ref.at[j]

    def remote(k, slab):
        return pltpu.make_async_remote_copy(
            src_ref=piece(me ^ k), dst_ref=out_ref.at[slab], send_sem=send_sems.at[k - 1],
            recv_sem=recv_sems.at[k - 1], device_id=_peer(k), device_id_type=MESH_ID)

    mine = pltpu.make_async_copy(piece(me), out_ref.at[me], local_sem)
    arrivals = {k: remote(k, me ^ k) for k in range(1, N_DEV)}
    if not gather:
        sends = [remote(k, me) for k in range(1, N_DEV)]

        def start():
            mine.start()
            for cp in sends:
                cp.start()

        def wait():
            for cp in arrivals.values():
                cp.wait_recv()
            for cp in sends:
                cp.wait_send()
            mine.wait()

        return start, wait

    far = (2, 4, 6)
    sends = [remote(k, me) for k in (1,) + far]
    passed_on = [pltpu.make_async_remote_copy(
        src_ref=out_ref.at[me ^ k], dst_ref=out_ref.at[me ^ k], send_sem=send_sems.at[k],
        recv_sem=recv_sems.at[k], device_id=_peer(1), device_id_type=MESH_ID) for k in far]

    def start():
        mine.start()
        for cp in sends:
            cp.start()

    def wait():
        for k, cp in zip(far, passed_on):
            arrivals[k].wait_recv()
            cp.start()
        for k in (1, 3, 5, 7):
            arrivals[k].wait_recv()
        for cp in sends + passed_on:
            cp.wait_send()
        mine.wait()

    return start, wait


def _pad_rows(flat, mult):
    n = flat.shape[-1]
    rows = -(-n // (LANES * mult)) * mult
    pad = rows * LANES - n
    flat = jnp.pad(flat, [(0, 0)] * (flat.ndim - 1) + [(0, pad)])
    return flat.reshape(flat.shape[:-1] + (rows, LANES))


def _pick(n, cands):
    for c in cands:
        if n % c == 0:
            return c
    return n


def _matmul(a, b, ta, tb, out_dtype, name, rider=None, res=None):
    (m, k) = (a.shape[1], a.shape[0]) if ta else a.shape
    (k2, n) = (b.shape[1], b.shape[0]) if tb else b.shape
    assert k == k2, (a.shape, b.shape, ta, tb)
    tm = _pick(m, (1024, 1408, 512, 256, 128))
    tn = _pick(n, (1536, 1408, 1024, 896, 512, 384, 256, 128))
    tk = _pick(k, (1024, 1408, 896, 512, 256, 128)) if k > 2816 else k
    nk = k // tk
    dims = (((0 if ta else 1,), (1 if tb else 0,)), ((), ()))

    grid = (n // tn, m // tm, nk)

    n_in = 2 + (res is not None) + (rider is not None)

    def body(*refs):
        ins, rest = refs[:n_in], refs[n_in:]
        a_ref, b_ref = ins[:2]
        res_ref = ins[2] if res is not None else None
        o_ref, scratch = rest[0], rest[1:]
        if rider is not None:
            start, wait = _exchange_copies(ins[-1], rest[1], *scratch[-3:], rider[1])
            scratch = scratch[1:-3]
            at = [pl.program_id(ax) for ax in range(3)]
            pl.when(functools.reduce(jnp.logical_and, [p == 0 for p in at]))(start)
        part = lax.dot_general(a_ref[...].astype(BF16), b_ref[...].astype(BF16), dims,
                               preferred_element_type=F32)

        def finish(total):
            if res_ref is not None:
                total = total + res_ref[...]
            o_ref[...] = total.astype(o_ref.dtype)

        if nk == 1:
            finish(part)
        else:
            acc_ref, = scratch
            kk = pl.program_id(2)

            @pl.when(kk == 0)
            def _():
                acc_ref[...] = part

            @pl.when(kk > 0)
            def _():
                acc_ref[...] += part

            @pl.when(kk == nk - 1)
            def _():
                finish(acc_ref[...])
        if rider is not None:
            pl.when(functools.reduce(jnp.logical_and, [p == g - 1 for p, g in zip(at, grid)]))(wait)

    a_spec = (pl.BlockSpec((tk, tm), lambda j, i, kk: (kk, i)) if ta
              else pl.BlockSpec((tm, tk), lambda j, i, kk: (i, kk)))
    b_spec = (pl.BlockSpec((tn, tk), lambda j, i, kk: (j, kk)) if tb
              else pl.BlockSpec((tk, tn), lambda j, i, kk: (kk, j)))
    any_spec = pl.BlockSpec(memory_space=pl.ANY)
    tile_spec = pl.BlockSpec((tm, tn), lambda j, i, kk: (i, j))
    in_specs, operands = [a_spec, b_spec], [a, b]
    out_shape, out_specs = [jax.ShapeDtypeStruct((m, n), out_dtype)], [tile_spec]
    scratch = [] if nk == 1 else [pltpu.VMEM((tm, tn), F32)]
    sem = ("parallel", "parallel", "arbitrary")
    if res is not None:
        in_specs.append(tile_spec)
        operands.append(res)
    if rider is not None:
        in_specs.append(any_spec)
        operands.append(rider[0])
        out_shape.append(_exchange_out(*rider))
        out_specs.append(any_spec)
        scratch += list(EXCHANGE_SEMS)
        sem = ("arbitrary",) * 3
    outs = pl.pallas_call(
        body, name=name, out_shape=out_shape, grid=grid, in_specs=in_specs, out_specs=out_specs,
        scratch_shapes=scratch, compiler_params=_params(*sem),
    )(*operands)
    return outs[0] if rider is None else tuple(outs)


@functools.partial(jax.custom_vjp, nondiff_argnums=(3, 4))
def linear(a, w, res, out_dtype, name):
    return _matmul(a, w, False, False, out_dtype, name + "_fwd", res=res)


def _linear_fwd(a, w, res, out_dtype, name):
    return _matmul(a, w, False, False, out_dtype, name + "_fwd", res=res), (a, w, res is not None)


def _linear_bwd(out_dtype, name, saved, dy):
    a, w, has_res = saved
    da = _matmul(dy, w, False, True, a.dtype, name + "_da")
    dw = _matmul(a, dy, True, False, w.dtype, name + "_dw")
    return da, dw, (dy if has_res else None)


linear.defvjp(_linear_fwd, _linear_bwd)


def _sum_pieces(pieces, name):
    shape = pieces.shape[1:]
    c = shape[-1]
    r = math.prod(shape[:-1])
    tr = _pick(r, (512, 256, 128, 64, 32, 16, 8))

    def body(p_ref, o_ref):
        g = p_ref[0].astype(F32)
        for j in range(1, N_DEV):
            g = g + p_ref[j].astype(F32)
        o_ref[...] = g

    return pl.pallas_call(
        body, name=name, out_shape=jax.ShapeDtypeStruct((r, c), F32), grid=(r // tr,),
        in_specs=[pl.BlockSpec((N_DEV, tr, c), lambda i: (0, i, 0))],
        out_specs=pl.BlockSpec((tr, c), lambda i: (i, 0)),
        compiler_params=_params("parallel"),
    )(pieces.reshape(N_DEV, r, c)).reshape(shape)


@functools.partial(jax.custom_vjp, nondiff_argnums=(1,))
def gather_x(shard, name):
    return _exchange(shard.astype(BF16), True, name)


def _gather_x_fwd(shard, name):
    return _exchange(shard.astype(BF16), True, name), None


def _gather_x_bwd(name, _, dparts):
    return (_sum_pieces(_exchange(dparts, False, name + "_back"), name + "_sum"),)


gather_x.defvjp(_gather_x_fwd, _gather_x_bwd)


@functools.partial(jax.custom_vjp, nondiff_argnums=(4, 5))
def linear_x(a, w, res, shard, out_dtype, name):
    return _matmul(a, w, False, False, out_dtype, name + "_fwd", rider=(shard.astype(BF16), True), res=res)


def _linear_x_fwd(a, w, res, shard, out_dtype, name):
    out = _matmul(a, w, False, False, out_dtype, name + "_fwd", rider=(shard.astype(BF16), True), res=res)
    return out, (a, w, res is not None)


def _linear_x_bwd(out_dtype, name, saved, cts):
    a, w, has_res = saved
    dy, dparts = cts
    half = dparts.shape[1] // 2
    da, moved_a = _matmul(dy, w, False, True, a.dtype, name + "_da", rider=(dparts[:, :half], False))
    dw, moved_b = _matmul(a, dy, True, False, w.dtype, name + "_dw", rider=(dparts[:, half:], False))
    dshard = jnp.concatenate([_sum_pieces(moved_a, name + "_sum_a"), _sum_pieces(moved_b, name + "_sum_b")])
    return da, dw, (dy if has_res else None), dshard


linear_x.defvjp(_linear_x_fwd, _linear_x_bwd)


VMEM_LIMIT = 56 * 1024 * 1024
SUBLANES = 8


def _params(*sem):
    return pltpu.CompilerParams(dimension_semantics=sem, vmem_limit_bytes=VMEM_LIMIT)


def _sigmoid(x):
    return 0.5 * jnp.tanh(0.5 * x) + 0.5


def _rstd(xv):
    return lax.rsqrt(jnp.mean(xv * xv, axis=-1, keepdims=True) + NORM_EPS)


def _accumulate(ref, part, first):
    @pl.when(first)
    def _():
        ref[...] = part

    @pl.when(jnp.logical_not(first))
    def _():
        ref[...] += part


def _norm_fwd(x, w, name):
    s, d = x.shape
    tm = _pick(s, (512, 256, 128))

    def body(x_ref, w_ref, h_ref):
        xv = x_ref[...]
        h_ref[...] = (xv * _rstd(xv) * w_ref[...]).astype(h_ref.dtype)

    return pl.pallas_call(
        body, name=name, out_shape=jax.ShapeDtypeStruct((s, d), BF16), grid=(s // tm,),
        in_specs=[pl.BlockSpec((tm, d), lambda i: (i, 0)), pl.BlockSpec((1, d), lambda i: (0, 0))],
        out_specs=pl.BlockSpec((tm, d), lambda i: (i, 0)), compiler_params=_params("parallel"),
    )(x, w.reshape(1, d))


def _norm_bwd(x, w, dh, dskip, name):
    s, d = x.shape
    tm = _pick(s, (512, 256, 128))

    def body(x_ref, w_ref, dh_ref, ds_ref, dx_ref, dw_ref):
        xv = x_ref[...]
        r = _rstd(xv)
        y = xv * r
        dhv = dh_ref[...].astype(F32)
        dy = dhv * w_ref[...]
        dx_ref[...] = ds_ref[...] + r * (dy - y * jnp.mean(dy * y, axis=-1, keepdims=True))
        _accumulate(dw_ref, jnp.sum(dhv * y, axis=0, keepdims=True), pl.program_id(0) == 0)

    row = pl.BlockSpec((tm, d), lambda i: (i, 0))
    vec = pl.BlockSpec((1, d), lambda i: (0, 0))
    dx, dw = pl.pallas_call(
        body, name=name,
        out_shape=[jax.ShapeDtypeStruct((s, d), F32), jax.ShapeDtypeStruct((1, d), F32)], grid=(s // tm,),
        in_specs=[row, vec, row, row], out_specs=[row, vec], compiler_params=_params("arbitrary"),
    )(x, w.reshape(1, d), dh, dskip)
    return dx, dw.reshape(d)


@functools.partial(jax.custom_vjp, nondiff_argnums=(2,))
def norm(x, w, name):
    return x, _norm_fwd(x, w, name + "_fwd")


def _norm_vjp_fwd(x, w, name):
    return (x, _norm_fwd(x, w, name + "_fwd")), (x, w)


def _norm_vjp_bwd(name, saved, cts):
    x, w = saved
    dskip, dh = cts
    return _norm_bwd(x, w, dh, dskip, name + "_bwd")


norm.defvjp(_norm_vjp_fwd, _norm_vjp_bwd)


def loss_head(x, w, target, name):
    s, d = x.shape
    tm = _pick(s, (512, 256, 128))

    def body(x_ref, w_ref, t_ref, loss_ref, dx_ref, dw_ref):
        first = pl.program_id(0) == 0
        xv = x_ref[...]
        r = _rstd(xv)
        y = xv * r
        err = y * w_ref[...] - t_ref[...]
        part = 0.5 * jnp.sum(jnp.sum(err * err, axis=-1, keepdims=True), axis=0, keepdims=True) / d
        _accumulate(loss_ref, jnp.broadcast_to(part, loss_ref.shape), first)
        dout = err / d
        dy = dout * w_ref[...]
        dx_ref[...] = r * (dy - y * jnp.mean(dy * y, axis=-1, keepdims=True))
        _accumulate(dw_ref, jnp.sum(dout * y, axis=0, keepdims=True), first)

    row = pl.BlockSpec((tm, d), lambda i: (i, 0))
    vec = pl.BlockSpec((1, d), lambda i: (0, 0))
    loss, dx, dw = pl.pallas_call(
        body, name=name,
        out_shape=[jax.ShapeDtypeStruct((1, 128), F32), jax.ShapeDtypeStruct((s, d), F32),
                   jax.ShapeDtypeStruct((1, d), F32)],
        grid=(s // tm,), in_specs=[row, vec, row],
        out_specs=[pl.BlockSpec((1, 128), lambda i: (0, 0)), row, vec],
        compiler_params=_params("arbitrary"),
    )(x, w.reshape(1, d), target)
    return loss[0, 0], dx, dw.reshape(d)


FFN_ROWS = 256
FFN_COLS = 256


def _shift_down(cur, halo, k):
    out = pltpu.roll(cur, k, axis=0)
    row = lax.broadcasted_iota(jnp.int32, halo.shape, 0)
    top = out[0:SUBLANES]
    for j in range(k):
        top = jnp.where(row == j, halo[SUBLANES - k + j:SUBLANES - k + j + 1, :], top)
    return jnp.concatenate([top, out[SUBLANES:]], axis=0)


def _shift_up(cur, nxt, k):
    n = cur.shape[0]
    out = pltpu.roll(cur, n - k, axis=0)
    row = lax.broadcasted_iota(jnp.int32, nxt.shape, 0)
    bottom = out[n - SUBLANES:]
    for j in range(k):
        bottom = jnp.where(row == SUBLANES - k + j, nxt[j:j + 1, :], bottom)
    return jnp.concatenate([out[:n - SUBLANES], bottom], axis=0)


def _conv_taps(cur, halo, ntaps):
    return [_shift_down(cur, halo, ntaps - 1 - k) if k < ntaps - 1 else cur for k in range(ntaps)]


def _pad_taps(conv_w):
    return jnp.pad(conv_w, ((0, SUBLANES - conv_w.shape[0]), (0, 0)))


def _ffn_mid_fwd(u0, conv_w, conv_b, name):
    s, width = u0.shape
    half = width // 2
    tm = _pick(s, (FFN_ROWS, 128))
    per = tm // SUBLANES

    def body(w_ref, b_ref, u_ref, halo_ref, a_ref):
        keep = pl.program_id(0) > 0
        for c0 in range(0, half, FFN_COLS):
            vals = []
            for base in (c0, half + c0):
                cols = slice(base, base + FFN_COLS)
                halo = jnp.where(keep, halo_ref[:, cols], 0.0)
                taps = _conv_taps(u_ref[:, cols], halo, 3)
                vals.append(sum(w_ref[k:k + 1, cols] * taps[k] for k in range(3)) + b_ref[:, cols])
            gate, up = vals
            a_ref[:, c0:c0 + FFN_COLS] = (gate * _sigmoid(gate) * up).astype(a_ref.dtype)

    return pl.pallas_call(
        body, name=name, out_shape=jax.ShapeDtypeStruct((s, half), BF16), grid=(s // tm,),
        in_specs=[pl.BlockSpec((SUBLANES, width), lambda i: (0, 0)), pl.BlockSpec((1, width), lambda i: (0, 0)),
                  pl.BlockSpec((tm, width), lambda i: (i, 0)),
                  pl.BlockSpec((SUBLANES, width), lambda i: (jnp.maximum(i * per - 1, 0), 0))],
        out_specs=pl.BlockSpec((tm, half), lambda i: (i, 0)), compiler_params=_params("parallel"),
    )(_pad_taps(conv_w), conv_b.reshape(1, width), u0, u0)


def _ffn_mid_bwd(u0, conv_w, conv_b, da, name):
    s, width = u0.shape
    half = width // 2
    tm = _pick(s, (FFN_ROWS, 128))
    per = tm // SUBLANES
    nt = s // tm

    def body(w_ref, b_ref, u_ref, halo_ref, da_ref, du0_ref, dw_ref, db_ref, carry_ref):
        step = pl.program_id(0)
        first = step == 0
        keep = step < nt - 1
        for c0 in range(0, half, FFN_COLS):
            dav = da_ref[:, c0:c0 + FFN_COLS].astype(F32)
            taps, vals = [], []
            for base in (c0, half + c0):
                cols = slice(base, base + FFN_COLS)
                halo = jnp.where(keep, halo_ref[:, cols], 0.0)
                tp = _conv_taps(u_ref[:, cols], halo, 3)
                taps.append(tp)
                vals.append(sum(w_ref[k:k + 1, cols] * tp[k] for k in range(3)) + b_ref[:, cols])
            gate, up = vals
            sig = _sigmoid(gate)
            dus = [dav * up * sig * (1.0 + gate * (1.0 - sig)), dav * gate * sig]
            for base, tp, du in zip((c0, half + c0), taps, dus):
                cols = slice(base, base + FFN_COLS)
                nxt = jnp.where(first, 0.0, carry_ref[:, cols])
                du0 = (w_ref[2:3, cols] * du + w_ref[1:2, cols] * _shift_up(du, nxt, 1)
                       + w_ref[0:1, cols] * _shift_up(du, nxt, 2))
                du0_ref[:, cols] = du0.astype(du0_ref.dtype)
                carry_ref[:, cols] = du[0:SUBLANES, :]
                dwp = jnp.concatenate([jnp.sum(du * tp[k], axis=0, keepdims=True) for k in range(3)]
                                      + [jnp.zeros((SUBLANES - 3, FFN_COLS), F32)], axis=0)
                dbp = jnp.sum(du, axis=0, keepdims=True)

                @pl.when(first)
                def _():
                    dw_ref[:, cols] = dwp
                    db_ref[:, cols] = dbp

                @pl.when(jnp.logical_not(first))
                def _():
                    dw_ref[:, cols] += dwp
                    db_ref[:, cols] += dbp

    rev = lambda i: nt - 1 - i
    du0, dw, db = pl.pallas_call(
        body, name=name,
        out_shape=[jax.ShapeDtypeStruct((s, width), BF16), jax.ShapeDtypeStruct((SUBLANES, width), F32),
                   jax.ShapeDtypeStruct((1, width), F32)],
        grid=(nt,),
        in_specs=[pl.BlockSpec((SUBLANES, width), lambda i: (0, 0)), pl.BlockSpec((1, width), lambda i: (0, 0)),
                  pl.BlockSpec((tm, width), lambda i: (rev(i), 0)),
                  pl.BlockSpec((SUBLANES, width), lambda i: (jnp.maximum(rev(i) * per - 1, 0), 0)),
                  pl.BlockSpec((tm, half), lambda i: (rev(i), 0))],
        out_specs=[pl.BlockSpec((tm, width), lambda i: (rev(i), 0)),
                   pl.BlockSpec((SUBLANES, width), lambda i: (0, 0)), pl.BlockSpec((1, width), lambda i: (0, 0))],
        scratch_shapes=[pltpu.VMEM((SUBLANES, width), F32)],
        compiler_params=_params("arbitrary"),
    )(_pad_taps(conv_w), conv_b.reshape(1, width), u0, u0, da)
    return du0, dw[:3], db.reshape(width)


@functools.partial(jax.custom_vjp, nondiff_argnums=(3,))
def ffn_mid(u0, conv_w, conv_b, name):
    return _ffn_mid_fwd(u0, conv_w, conv_b, name + "_fwd")


def _ffn_mid_vjp_fwd(u0, conv_w, conv_b, name):
    return _ffn_mid_fwd(u0, conv_w, conv_b, name + "_fwd"), (u0, conv_w, conv_b)


def _ffn_mid_vjp_bwd(name, res, da):
    u0, conv_w, conv_b = res
    return _ffn_mid_bwd(u0, conv_w, conv_b, da, name + "_bwd")


ffn_mid.defvjp(_ffn_mid_vjp_fwd, _ffn_mid_vjp_bwd)


NEG = -1e30
HEAD_SLICES = [slice(hh * ATTN_HEAD_DIM, (hh + 1) * ATTN_HEAD_DIM) for hh in range(ATTN_HEADS)]
ATTN_SCALE = ATTN_HEAD_DIM ** -0.5
QKV_GROUP = 3 * ATTN_OUT_WIDTH


def rope_table(seq, d):
    pos = (jnp.arange(seq // d, dtype=jnp.int32)[None, :] * d + jnp.arange(d, dtype=jnp.int32)[:, None])
    inv_freq = ROPE_THETA ** (-jnp.arange(0, ROPE_DIM, 2, dtype=F32) / ROPE_DIM)
    ang = pos.reshape(seq).astype(F32)[:, None] * inv_freq[None, :]
    cos, sin = jnp.cos(ang), jnp.sin(ang)
    half = ROPE_DIM // 2
    ones = jnp.ones((seq, ATTN_HEAD_DIM - ROPE_DIM), F32)
    zero = lambda n: jnp.zeros((seq, n), F32)
    return jnp.concatenate([cos, cos, ones, -sin, zero(ATTN_HEAD_DIM - half),
                            zero(half), sin, zero(ATTN_HEAD_DIM - ROPE_DIM)], axis=1)


def _rope(t, tab, sign):
    half = ROPE_DIM // 2
    return t * tab[:, 0:128] + sign * (pltpu.roll(t, ATTN_HEAD_DIM - half, axis=1) * tab[:, 128:256]
                                       + pltpu.roll(t, half, axis=1) * tab[:, 256:384])


def _to_dilated(a, d):
    s = a.shape[0]
    return a if d == 1 else a.reshape(s // d, d, -1).transpose(1, 0, 2).reshape(s, -1)


def _from_dilated(a, d):
    s = a.shape[0]
    return a if d == 1 else a.reshape(d, s // d, -1).transpose(1, 0, 2).reshape(s, -1)


def _rope_qk(qkv, tab, name):
    s = qkv.shape[0]
    tm = _pick(s, (512, 256, 128))

    def body(t_ref, x_ref, o_ref):
        for hs in HEAD_SLICES:
            o_ref[:, hs] = _rope(x_ref[:, hs].astype(F32), t_ref[...], 1.0).astype(o_ref.dtype)

    blk = pl.BlockSpec((tm, ATTN_OUT_WIDTH), lambda i, c: (i, c))
    return pl.pallas_call(
        body, name=name, out_shape=jax.ShapeDtypeStruct(qkv.shape, qkv.dtype), grid=(s // tm, 2),
        in_specs=[pl.BlockSpec((tm, 384), lambda i, c: (i, 0)), blk], out_specs=blk,
        input_output_aliases={1: 0}, compiler_params=_params("parallel", "parallel"),
    )(tab, qkv)


def _dot_nt(a, b):
    return lax.dot_general(a, b, (((1,), (1,)), ((), ())), preferred_element_type=F32)


def _dot_tn(a, b):
    return lax.dot_general(a, b, (((0,), (0,)), ((), ())), preferred_element_type=F32)


def _dot(a, b):
    return jnp.dot(a, b, preferred_element_type=F32)


def _window_mask(has_prev):
    ii = lax.broadcasted_iota(jnp.int32, (ATTN_BLOCK, 2 * ATTN_BLOCK), 0)
    jj = lax.broadcasted_iota(jnp.int32, (ATTN_BLOCK, 2 * ATTN_BLOCK), 1)
    in_window = jnp.logical_and(jj >= ii, jj <= ii + ATTN_BLOCK)
    return jnp.logical_and(in_window, jnp.logical_or(jj >= ATTN_BLOCK, has_prev))


def _both(prev_ref, cur_ref, hs):
    return jnp.concatenate([prev_ref[:, hs], cur_ref[:, hs]], axis=0)


def _attn_group_fwd(qkv, d, name):
    s = qkv.shape[0]
    nb = s // d // ATTN_BLOCK

    def body(q_ref, kc_ref, kp_ref, vc_ref, vp_ref, o_ref, lse_ref):
        mask = _window_mask(pl.program_id(1) > 0)
        lane = lax.broadcasted_iota(jnp.int32, (ATTN_BLOCK, 128), 1)
        lse_tile = jnp.zeros((ATTN_BLOCK, 128), F32)
        for hh, hs in enumerate(HEAD_SLICES):
            sc = jnp.where(mask, _dot_nt(q_ref[:, hs], _both(kp_ref, kc_ref, hs)) * ATTN_SCALE, NEG)
            m = jnp.max(sc, axis=1, keepdims=True)
            p = jnp.exp(sc - m)
            den = jnp.sum(p, axis=1, keepdims=True)
            o_ref[:, hs] = _dot(p.astype(BF16), _both(vp_ref, vc_ref, hs)) / den
            lse_tile = jnp.where(lane == hh, m + jnp.log(den), lse_tile)
        lse_ref[...] = lse_tile

    cur = lambda t: pl.BlockSpec((ATTN_BLOCK, ATTN_OUT_WIDTH), lambda r, n: (r * nb + n, t))
    prv = lambda t: pl.BlockSpec((ATTN_BLOCK, ATTN_OUT_WIDTH), lambda r, n: (r * nb + jnp.maximum(n - 1, 0), t))
    return pl.pallas_call(
        body, name=name,
        out_shape=[jax.ShapeDtypeStruct((s, ATTN_OUT_WIDTH), F32), jax.ShapeDtypeStruct((s, 128), F32)],
        grid=(d, nb), in_specs=[cur(0), cur(1), prv(1), cur(2), prv(2)],
        out_specs=[pl.BlockSpec((ATTN_BLOCK, ATTN_OUT_WIDTH), lambda r, n: (r * nb + n, 0)),
                   pl.BlockSpec((ATTN_BLOCK, 128), lambda r, n: (r * nb + n, 0))],
        compiler_params=_params("parallel", "parallel"),
    )(qkv, qkv, qkv, qkv, qkv)


def _attn_combine(os_, lses, name):
    s = os_[0].shape[0]
    tm = _pick(s, (256, 128))
    ng = len(os_)

    def body(*refs):
        o_refs, l_refs, (o_ref, lse_ref) = refs[:ng], refs[ng:2 * ng], refs[2 * ng:]
        lane = lax.broadcasted_iota(jnp.int32, (tm, 128), 1)
        lse_tile = jnp.zeros((tm, 128), F32)
        for hh, hs in enumerate(HEAD_SLICES):
            ls = [l_ref[:, hh:hh + 1] for l_ref in l_refs]
            m = functools.reduce(jnp.maximum, ls)
            ws = [jnp.exp(l - m) for l in ls]
            tot = functools.reduce(lambda a, b: a + b, ws)
            acc = functools.reduce(lambda a, b: a + b, [o_r[:, hs] * w for o_r, w in zip(o_refs, ws)])
            o_ref[:, hs] = (acc / tot).astype(o_ref.dtype)
            lse_tile = jnp.where(lane == hh, m + jnp.log(tot), lse_tile)
        lse_ref[...] = lse_tile

    wide = pl.BlockSpec((tm, ATTN_OUT_WIDTH), lambda i: (i, 0))
    thin = pl.BlockSpec((tm, 128), lambda i: (i, 0))
    return pl.pallas_call(
        body, name=name,
        out_shape=[jax.ShapeDtypeStruct((s, ATTN_OUT_WIDTH), BF16), jax.ShapeDtypeStruct((s, 128), F32)],
        grid=(s // tm,), in_specs=[wide] * ng + [thin] * ng, out_specs=[wide, thin],
        compiler_params=_params("parallel"),
    )(*os_, *lses)


def _attn_delta(do, o, name):
    s = do.shape[0]
    tm = _pick(s, (256, 128))

    def body(do_ref, o_ref, out_ref):
        lane = lax.broadcasted_iota(jnp.int32, (tm, 128), 1)
        tile = jnp.zeros((tm, 128), F32)
        for hh, hs in enumerate(HEAD_SLICES):
            prod = do_ref[:, hs].astype(F32) * o_ref[:, hs].astype(F32)
            tile = jnp.where(lane == hh, jnp.sum(prod, axis=1, keepdims=True), tile)
        out_ref[...] = tile

    wide = pl.BlockSpec((tm, ATTN_OUT_WIDTH), lambda i: (i, 0))
    return pl.pallas_call(
        body, name=name, out_shape=jax.ShapeDtypeStruct((s, 128), F32), grid=(s // tm,),
        in_specs=[wide, wide], out_specs=pl.BlockSpec((tm, 128), lambda i: (i, 0)),
        compiler_params=_params("parallel"),
    )(do, o)


def _attn_group_bwd(qkv, do, lse, delta, tab, d, name):
    s = qkv.shape[0]
    nb = s // d // ATTN_BLOCK

    def body(q_ref, kc_ref, kp_ref, vc_ref, vp_ref, do_ref, lse_ref, dl_ref, tq_ref, tk_ref,
             dq_ref, dk_ref, dv_ref, ck_ref, cv_ref):
        n = pl.program_id(1)

        @pl.when(n == 0)
        def _():
            ck_ref[...] = jnp.zeros_like(ck_ref)
            cv_ref[...] = jnp.zeros_like(cv_ref)

        @pl.when(n < nb)
        def _():
            mask = _window_mask(n > 0)
            for hh, hs in enumerate(HEAD_SLICES):
                qh, doh = q_ref[:, hs], do_ref[:, hs]
                kh, vh = _both(kp_ref, kc_ref, hs), _both(vp_ref, vc_ref, hs)
                sc = jnp.where(mask, _dot_nt(qh, kh) * ATTN_SCALE, NEG)
                p = jnp.exp(sc - lse_ref[:, hh:hh + 1])
                ds = (p * (_dot_nt(doh, vh) - dl_ref[:, hh:hh + 1]) * ATTN_SCALE).astype(BF16)
                dq_ref[:, hs] = _rope(_dot(ds, kh), tq_ref[...], -1.0).astype(dq_ref.dtype)
                dk = _dot_tn(ds, qh)
                dv = _dot_tn(p.astype(BF16), doh)

                dk_ref[:, hs] = _rope(ck_ref[:, hs] + dk[:ATTN_BLOCK], tk_ref[...], -1.0).astype(dk_ref.dtype)
                dv_ref[:, hs] = (cv_ref[:, hs] + dv[:ATTN_BLOCK]).astype(dv_ref.dtype)
                ck_ref[:, hs] = dk[ATTN_BLOCK:]
                cv_ref[:, hs] = dv[ATTN_BLOCK:]

        @pl.when(n == nb)
        def _():
            for hs in HEAD_SLICES:
                dk_ref[:, hs] = _rope(ck_ref[:, hs], tk_ref[...], -1.0).astype(dk_ref.dtype)
                dv_ref[:, hs] = cv_ref[:, hs].astype(dv_ref.dtype)

    def spec(width, row, col):
        return pl.BlockSpec((ATTN_BLOCK, width), lambda r, n: (r * nb + row(n), col))

    cur = lambda n: jnp.minimum(n, nb - 1)
    prv = lambda n: jnp.maximum(jnp.minimum(n, nb - 1) - 1, 0)
    out = lambda n: jnp.maximum(n - 1, 0)
    wide = ATTN_OUT_WIDTH
    return pl.pallas_call(
        body, name=name,
        out_shape=[jax.ShapeDtypeStruct((s, wide), BF16)] * 3,
        grid=(d, nb + 1),
        in_specs=[spec(wide, cur, 0), spec(wide, cur, 1), spec(wide, prv, 1), spec(wide, cur, 2),
                  spec(wide, prv, 2), spec(wide, cur, 0), spec(128, cur, 0), spec(128, cur, 0),
                  spec(384, cur, 0), spec(384, out, 0)],
        out_specs=[spec(wide, cur, 0), spec(wide, out, 0), spec(wide, out, 0)],
        scratch_shapes=[pltpu.VMEM((ATTN_BLOCK, wide), F32), pltpu.VMEM((ATTN_BLOCK, wide), F32)],
        compiler_params=_params("parallel", "arbitrary"),
    )(qkv, qkv, qkv, qkv, qkv, do, lse, delta, tab, tab)


def _attn_core_fwd(qkvs, name):
    rot, os_, lses = [], [], []
    for g, (qkv, d) in enumerate(zip(qkvs, ATTN_DILATIONS)):
        qkv = _rope_qk(qkv, rope_table(qkv.shape[0], d), f"{name}_rope{g}")
        o_g, lse_g = _attn_group_fwd(qkv, d, f"{name}_fwd{g}")
        rot.append(qkv)
        os_.append(_from_dilated(o_g, d))
        lses.append(_from_dilated(lse_g, d))
    o, lse = _attn_combine(os_, lses, name + "_combine")
    return o, (tuple(rot), o, lse)


@functools.partial(jax.custom_vjp, nondiff_argnums=(1,))
def attn_core(qkvs, name):
    return _attn_core_fwd(qkvs, name)[0]


def _attn_core_vjp_fwd(qkvs, name):
    return _attn_core_fwd(qkvs, name)


def _attn_core_vjp_bwd(name, res, do):
    rot, o, lse = res
    delta = _attn_delta(do, o, name + "_delta")
    out = []
    for g, (qkv, d) in enumerate(zip(rot, ATTN_DILATIONS)):
        parts = _attn_group_bwd(qkv, _to_dilated(do, d), _to_dilated(lse, d), _to_dilated(delta, d),
                                rope_table(o.shape[0], d), d, f"{name}_bwd{g}")
        out.append(jnp.concatenate(parts, axis=1))
    return (tuple(out),)


attn_core.defvjp(_attn_core_vjp_fwd, _attn_core_vjp_bwd)


def attention_mixer_p(x, h, lin, j, carries, tag):
    qkvs = tuple(lin(_to_dilated(h, d), ('attn_w_qkv', j), BF16, f"{tag}_qkv{g}",
                     cols=slice(g * QKV_GROUP, (g + 1) * QKV_GROUP), carry=carries[g])
                 for g, d in enumerate(ATTN_DILATIONS))
    return lin(attn_core(qkvs, tag), ('attn_w_o', j), F32, tag + "_o", res=x)


SSM_CONV_TAPS = 4
SSM_COL_BLOCK = 2048
SSM_PAIRS = SSM_HEADS // 2
SSM_DT_BLOCK = (SSM_D_INNER + SSM_CONV_DIM) // 128


def _ssm_conv_fwd(zx, conv_w, conv_b, name):
    s = zx.shape[0]
    tm = _pick(s, (256, 128))
    per = tm // SUBLANES
    ncb = SSM_CONV_DIM // SSM_COL_BLOCK

    def body(w_ref, b_ref, x_ref, halo_ref, o_ref):
        keep = pl.program_id(1) > 0
        for c0 in range(0, SSM_COL_BLOCK, FFN_COLS):
            cols = slice(c0, c0 + FFN_COLS)
            halo = jnp.where(keep, halo_ref[:, cols], 0.0)
            taps = _conv_taps(x_ref[:, cols], halo, SSM_CONV_TAPS)
            pre = sum(w_ref[k:k + 1, cols] * taps[k] for k in range(SSM_CONV_TAPS)) + b_ref[:, cols]
            o_ref[:, cols] = pre * _sigmoid(pre)

    return pl.pallas_call(
        body, name=name, out_shape=jax.ShapeDtypeStruct((s, SSM_CONV_DIM), F32), grid=(ncb, s // tm),
        in_specs=[pl.BlockSpec((SUBLANES, SSM_COL_BLOCK), lambda j, i: (0, j)),
                  pl.BlockSpec((1, SSM_COL_BLOCK), lambda j, i: (0, j)),
                  pl.BlockSpec((tm, SSM_COL_BLOCK), lambda j, i: (i, j + 1)),
                  pl.BlockSpec((SUBLANES, SSM_COL_BLOCK), lambda j, i: (jnp.maximum(i * per - 1, 0), j + 1))],
        out_specs=pl.BlockSpec((tm, SSM_COL_BLOCK), lambda j, i: (i, j)),
        compiler_params=_params("parallel", "parallel"),
    )(_pad_taps(conv_w), conv_b.reshape(1, SSM_CONV_DIM), zx, zx)


def _ssm_conv_bwd(zx, conv_w, conv_b, dact, name):
    s = zx.shape[0]
    tm = _pick(s, (256, 128))
    per = tm // SUBLANES
    nt = s // tm
    ncb = SSM_CONV_DIM // SSM_COL_BLOCK
    nk = SSM_CONV_TAPS

    def body(w_ref, b_ref, x_ref, halo_ref, da_ref, dx_ref, dw_ref, db_ref, carry_ref):
        step = pl.program_id(1)
        first = step == 0
        keep = step < nt - 1
        for c0 in range(0, SSM_COL_BLOCK, FFN_COLS):
            cols = slice(c0, c0 + FFN_COLS)
            halo = jnp.where(keep, halo_ref[:, cols], 0.0)
            taps = _conv_taps(x_ref[:, cols], halo, nk)
            pre = sum(w_ref[k:k + 1, cols] * taps[k] for k in range(nk)) + b_ref[:, cols]
            sig = _sigmoid(pre)
            dpre = da_ref[:, cols] * sig * (1.0 + pre * (1.0 - sig))
            nxt = jnp.where(first, 0.0, carry_ref[:, cols])
            dx = w_ref[nk - 1:nk, cols] * dpre
            for k in range(nk - 1):
                dx = dx + w_ref[k:k + 1, cols] * _shift_up(dpre, nxt, nk - 1 - k)
            dx_ref[:, cols] = dx.astype(dx_ref.dtype)
            carry_ref[:, cols] = dpre[0:SUBLANES, :]
            dwp = jnp.concatenate([jnp.sum(dpre * taps[k], axis=0, keepdims=True) for k in range(nk)]
                                  + [jnp.zeros((SUBLANES - nk, FFN_COLS), F32)], axis=0)
            dbp = jnp.sum(dpre, axis=0, keepdims=True)

            @pl.when(first)
            def _():
                dw_ref[:, cols] = dwp
                db_ref[:, cols] = dbp

            @pl.when(jnp.logical_not(first))
            def _():
                dw_ref[:, cols] += dwp
                db_ref[:, cols] += dbp

    rev = lambda i: nt - 1 - i
    dx, dw, db = pl.pallas_call(
        body, name=name,
        out_shape=[jax.ShapeDtypeStruct((s, SSM_CONV_DIM), BF16), jax.ShapeDtypeStruct((SUBLANES, SSM_CONV_DIM), F32),
                   jax.ShapeDtypeStruct((1, SSM_CONV_DIM), F32)],
        grid=(ncb, nt),
        in_specs=[pl.BlockSpec((SUBLANES, SSM_COL_BLOCK), lambda j, i: (0, j)),
                  pl.BlockSpec((1, SSM_COL_BLOCK), lambda j, i: (0, j)),
                  pl.BlockSpec((tm, SSM_COL_BLOCK), lambda j, i: (rev(i), j + 1)),
                  pl.BlockSpec((SUBLANES, SSM_COL_BLOCK), lambda j, i: (jnp.maximum(rev(i) * per - 1, 0), j + 1)),
                  pl.BlockSpec((tm, SSM_COL_BLOCK), lambda j, i: (rev(i), j))],
        out_specs=[pl.BlockSpec((tm, SSM_COL_BLOCK), lambda j, i: (rev(i), j)),
                   pl.BlockSpec((SUBLANES, SSM_COL_BLOCK), lambda j, i: (0, j)),
                   pl.BlockSpec((1, SSM_COL_BLOCK), lambda j, i: (0, j))],
        scratch_shapes=[pltpu.VMEM((SUBLANES, SSM_COL_BLOCK), F32)],
        compiler_params=_params("parallel", "arbitrary"),
    )(_pad_taps(conv_w), conv_b.reshape(1, SSM_CONV_DIM), zx, zx, dact)
    return dx, dw[:nk], db.reshape(SSM_CONV_DIM)


def _ssd_chunk(xs, bms, cms, dt_raw, dtb, alog, dsk, states):
    q = SSM_CHUNK
    lane = lax.broadcasted_iota(jnp.int32, (1, 128), 1)
    row = lax.broadcasted_iota(jnp.int32, (q, 1), 0)
    ii = lax.broadcasted_iota(jnp.int32, (q, q), 0)
    jj = lax.broadcasted_iota(jnp.int32, (q, q), 1)
    tril = ii >= jj
    left = lane < SSM_HEAD_DIM
    last_row = (row == q - 1).astype(F32)

    def lanes_of(mat, h):
        pick = (lane == h).astype(F32)
        return jnp.broadcast_to(jnp.sum(mat * pick, axis=1, keepdims=True), mat.shape)

    def rows_of(mat_t, h):
        pick = (row == h).astype(F32)
        return jnp.broadcast_to(jnp.sum(mat_t * pick, axis=0, keepdims=True), mat_t.shape)

    v = dt_raw + dtb
    dt = jnp.maximum(v, 0.0) + jnp.log(1.0 + jnp.exp(-jnp.abs(v)))
    adt = dt * (-jnp.exp(alog))
    acs = jnp.dot(tril.astype(F32), adt, precision=lax.Precision.HIGHEST, preferred_element_type=F32)
    acs_t = acs.T
    ys, new_states = [], []
    for pr in range(SSM_PAIRS):
        g = pr // 2
        if pr % 2 == 0:
            cb = _dot_nt(cms[g].astype(BF16), bms[g].astype(BF16))
        cols = [lanes_of(acs, 2 * pr + e) for e in range(2)]
        dts = [lanes_of(dt, 2 * pr + e) for e in range(2)]
        rws = [rows_of(acs_t, 2 * pr + e) for e in range(2)]
        lasts = [jnp.sum(c * last_row, axis=0, keepdims=True) for c in cols]
        xdt = xs[pr] * jnp.where(left, dts[0], dts[1])
        halves = [jnp.where(left, xdt, 0.0).astype(BF16), jnp.where(left, 0.0, xdt).astype(BF16)]
        y_diag, s_new = 0.0, 0.0
        for e in range(2):
            lmat = jnp.where(tril, jnp.exp(jnp.minimum(cols[e] - rws[e], 0.0)), 0.0)
            y_diag = y_diag + _dot((cb * lmat).astype(BF16), halves[e])
            decay = jnp.exp(lasts[e] - cols[e])
            s_new = s_new + _dot_tn((bms[g] * decay).astype(BF16), halves[e])
        y_off = _dot(cms[g].astype(BF16), states[pr].astype(BF16)) * jnp.where(left, jnp.exp(cols[0]), jnp.exp(cols[1]))
        skip = jnp.where(left, lanes_of(dsk, 2 * pr), lanes_of(dsk, 2 * pr + 1))
        ys.append(y_diag + y_off + xs[pr] * skip)
        new_states.append(states[pr] * jnp.where(left, jnp.exp(lasts[0]), jnp.exp(lasts[1])) + s_new)
    return tuple(ys), tuple(new_states)


def _ssm_vec(v):
    return jnp.pad(v.reshape(1, -1), ((0, 0), (0, 128 - v.shape[0])))


def _ssd_scan_fwd(act, zx, dtb, alog, dsk, name):
    s = act.shape[0]
    nc = s // SSM_CHUNK
    ng = SSM_GROUPS

    def body(act_ref, dt_ref, dtb_ref, alog_ref, dsk_ref, y_ref, st_out_ref, st_ref):
        @pl.when(pl.program_id(0) == 0)
        def _():
            st_ref[...] = jnp.zeros_like(st_ref)

        tile = lambda k: act_ref[:, k * 128:(k + 1) * 128]
        xs = [tile(k) for k in range(SSM_PAIRS)]
        bms = [tile(SSM_PAIRS + k) for k in range(ng)]
        cms = [tile(SSM_PAIRS + ng + k) for k in range(ng)]
        states = [st_ref[k] for k in range(SSM_PAIRS)]
        st_out_ref[0] = st_ref[...]
        ys, new_states = _ssd_chunk(xs, bms, cms, dt_ref[...], dtb_ref[...], alog_ref[...], dsk_ref[...], states)
        for k in range(SSM_PAIRS):
            y_ref[:, k * 128:(k + 1) * 128] = ys[k]
            st_ref[k] = new_states[k]

    vec = pl.BlockSpec((1, 128), lambda c: (0, 0))
    return pl.pallas_call(
        body, name=name,
        out_shape=[jax.ShapeDtypeStruct((s, SSM_D_INNER), F32),
                   jax.ShapeDtypeStruct((nc, SSM_PAIRS, SSM_STATE, 128), F32)],
        grid=(nc,),
        in_specs=[pl.BlockSpec((SSM_CHUNK, SSM_CONV_DIM), lambda c: (c, 0)),
                  pl.BlockSpec((SSM_CHUNK, 128), lambda c: (c, SSM_DT_BLOCK)), vec, vec, vec],
        out_specs=[pl.BlockSpec((SSM_CHUNK, SSM_D_INNER), lambda c: (c, 0)),
                   pl.BlockSpec((1, SSM_PAIRS, SSM_STATE, 128), lambda c: (c, 0, 0, 0))],
        scratch_shapes=[pltpu.VMEM((SSM_PAIRS, SSM_STATE, 128), F32)],
        compiler_params=_params("arbitrary"),
    )(act, zx, _ssm_vec(dtb), _ssm_vec(alog), _ssm_vec(dsk))


def _ssd_scan_bwd(act, zx, dtb, alog, dsk, st_in, dy, name):
    s = act.shape[0]
    nc = s // SSM_CHUNK
    ng = SSM_GROUPS

    def body(act_ref, dt_ref, dtb_ref, alog_ref, dsk_ref, st_ref, dy_ref, dact_ref, ddt_ref, dpar_ref, dst_ref):
        first = pl.program_id(0) == 0

        @pl.when(first)
        def _():
            dst_ref[...] = jnp.zeros_like(dst_ref)

        tile = lambda k: act_ref[:, k * 128:(k + 1) * 128]
        xs = [tile(k) for k in range(SSM_PAIRS)]
        bms = [tile(SSM_PAIRS + k) for k in range(ng)]
        cms = [tile(SSM_PAIRS + ng + k) for k in range(ng)]
        states = [st_ref[0, k] for k in range(SSM_PAIRS)]
        _, pullback = jax.vjp(_ssd_chunk, xs, bms, cms, dt_ref[...], dtb_ref[...], alog_ref[...], dsk_ref[...],
                              states)
        dys = tuple(dy_ref[:, k * 128:(k + 1) * 128] for k in range(SSM_PAIRS))
        dsts = tuple(dst_ref[k] for k in range(SSM_PAIRS))
        dxs, dbms, dcms, ddt, ddtb, dalog, ddsk, dstates = pullback((dys, dsts))
        for k, t in enumerate(list(dxs) + list(dbms) + list(dcms)):
            dact_ref[:, k * 128:(k + 1) * 128] = t
        ddt_ref[...] = ddt.astype(ddt_ref.dtype)
        for k in range(SSM_PAIRS):
            dst_ref[k] = dstates[k]
        dpar = jnp.concatenate([ddtb, dalog, ddsk, jnp.zeros((SUBLANES - 3, 128), F32)], axis=0)
        _accumulate(dpar_ref, dpar, first)

    rev = lambda c: nc - 1 - c
    vec = pl.BlockSpec((1, 128), lambda c: (0, 0))
    dact, ddt, dpar = pl.pallas_call(
        body, name=name,
        out_shape=[jax.ShapeDtypeStruct((s, SSM_CONV_DIM), F32), jax.ShapeDtypeStruct((s, 128), BF16),
                   jax.ShapeDtypeStruct((SUBLANES, 128), F32)],
        grid=(nc,),
        in_specs=[pl.BlockSpec((SSM_CHUNK, SSM_CONV_DIM), lambda c: (rev(c), 0)),
                  pl.BlockSpec((SSM_CHUNK, 128), lambda c: (rev(c), SSM_DT_BLOCK)), vec, vec, vec,
                  pl.BlockSpec((1, SSM_PAIRS, SSM_STATE, 128), lambda c: (rev(c), 0, 0, 0)),
                  pl.BlockSpec((SSM_CHUNK, SSM_D_INNER), lambda c: (rev(c), 0))],
        out_specs=[pl.BlockSpec((SSM_CHUNK, SSM_CONV_DIM), lambda c: (rev(c), 0)),
                   pl.BlockSpec((SSM_CHUNK, 128), lambda c: (rev(c), 0)),
                   pl.BlockSpec((SUBLANES, 128), lambda c: (0, 0))],
        scratch_shapes=[pltpu.VMEM((SSM_PAIRS, SSM_STATE, 128), F32)],
        compiler_params=_params("arbitrary"),
    )(act, zx, _ssm_vec(dtb), _ssm_vec(alog), _ssm_vec(dsk), st_in, dy)
    return dact, ddt, dpar[0, :SSM_HEADS], dpar[1, :SSM_HEADS], dpar[2, :SSM_HEADS]


SSM_NORM_GROUP = SSM_D_INNER // SSM_GROUPS


def _gated_group(y, z, w):
    g = y * (z * _sigmoid(z))
    return g * lax.rsqrt(jnp.mean(g * g, axis=-1, keepdims=True) + NORM_EPS) * w


def _gated_norm_fwd(y, zx, w, name):
    s = y.shape[0]
    tm = _pick(s, (256, 128))

    def body(y_ref, z_ref, w_ref, o_ref):
        for c0 in range(0, SSM_D_INNER, SSM_NORM_GROUP):
            cols = slice(c0, c0 + SSM_NORM_GROUP)
            o_ref[:, cols] = _gated_group(y_ref[:, cols], z_ref[:, cols], w_ref[:, cols]).astype(o_ref.dtype)

    blk = pl.BlockSpec((tm, SSM_D_INNER), lambda i: (i, 0))
    return pl.pallas_call(
        body, name=name, out_shape=jax.ShapeDtypeStruct((s, SSM_D_INNER), BF16), grid=(s // tm,),
        in_specs=[blk, blk, pl.BlockSpec((1, SSM_D_INNER), lambda i: (0, 0))], out_specs=blk,
        compiler_params=_params("parallel"),
    )(y, zx, w.reshape(1, SSM_D_INNER))


def _gated_norm_bwd(y, zx, w, dout, name):
    s = y.shape[0]
    tm = _pick(s, (256, 128))

    def body(y_ref, z_ref, w_ref, do_ref, dy_ref, dz_ref, dw_ref):
        first = pl.program_id(0) == 0
        for c0 in range(0, SSM_D_INNER, SSM_NORM_GROUP):
            cols = slice(c0, c0 + SSM_NORM_GROUP)
            _, pullback = jax.vjp(_gated_group, y_ref[:, cols], z_ref[:, cols], w_ref[:, cols])
            dyv, dzv, dwv = pullback(do_ref[:, cols].astype(F32))
            dy_ref[:, cols] = dyv
            dz_ref[:, cols] = dzv.astype(dz_ref.dtype)

            @pl.when(first)
            def _():
                dw_ref[:, cols] = dwv

            @pl.when(jnp.logical_not(first))
            def _():
                dw_ref[:, cols] += dwv

    blk = pl.BlockSpec((tm, SSM_D_INNER), lambda i: (i, 0))
    vec = pl.BlockSpec((1, SSM_D_INNER), lambda i: (0, 0))
    dy, dz, dw = pl.pallas_call(
        body, name=name,
        out_shape=[jax.ShapeDtypeStruct((s, SSM_D_INNER), F32), jax.ShapeDtypeStruct((s, SSM_D_INNER), BF16),
                   jax.ShapeDtypeStruct((1, SSM_D_INNER), F32)],
        grid=(s // tm,), in_specs=[blk, blk, vec, blk], out_specs=[blk, blk, vec],
        compiler_params=_params("arbitrary"),
    )(y, zx, w.reshape(1, SSM_D_INNER), dout)
    return dy, dz, dw.reshape(SSM_D_INNER)


def _ssm_core_fwd(zx, conv_w, conv_b, dtb, alog, dsk, norm_w, name):
    act = _ssm_conv_fwd(zx, conv_w, conv_b, name + "_conv_fwd")
    y, st_in = _ssd_scan_fwd(act, zx, dtb, alog, dsk, name + "_scan_fwd")
    out = _gated_norm_fwd(y, zx, norm_w, name + "_gate_fwd")
    return out, (zx, conv_w, conv_b, dtb, alog, dsk, norm_w, act, y, st_in)


@functools.partial(jax.custom_vjp, nondiff_argnums=(7,))
def ssm_core(zx, conv_w, conv_b, dtb, alog, dsk, norm_w, name):
    return _ssm_core_fwd(zx, conv_w, conv_b, dtb, alog, dsk, norm_w, name)[0]


def _ssm_core_vjp_fwd(zx, conv_w, conv_b, dtb, alog, dsk, norm_w, name):
    return _ssm_core_fwd(zx, conv_w, conv_b, dtb, alog, dsk, norm_w, name)


def _ssm_core_vjp_bwd(name, res, dout):
    zx, conv_w, conv_b, dtb, alog, dsk, norm_w, act, y, st_in = res
    dy, dz, dnorm_w = _gated_norm_bwd(y, zx, norm_w, dout, name + "_gate_bwd")
    dact, ddt, ddtb, dalog, ddsk = _ssd_scan_bwd(act, zx, dtb, alog, dsk, st_in, dy, name + "_scan_bwd")
    dxbc, dconv_w, dconv_b = _ssm_conv_bwd(zx, conv_w, conv_b, dact, name + "_conv_bwd")
    dzx = jnp.concatenate([dz, dxbc, ddt], axis=1)
    return dzx, dconv_w, dconv_b, ddtb, dalog, ddsk, dnorm_w


ssm_core.defvjp(_ssm_core_vjp_fwd, _ssm_core_vjp_bwd)


def ssd_mixer_p(x, h, lin, j, carries, conv_w, conv_b, dtb, alog, dsk, norm_w, tag):
    zx = lin(h, ('ssm_w_in', j), F32, tag + "_in", carry=carries[0])
    return lin(ssm_core(zx, conv_w, conv_b, dtb, alog, dsk, norm_w, tag), ('ssm_w_out', j), F32, tag + "_out",
               carry=carries[1], res=x)


def _full_weight(n, parts):
    full = _join8(parts, SHARD_AXIS[n] - 1)
    if n == 'ssm_w_in':
        full = jnp.pad(full, ((0, 0), (0, SSM_IN_PAD - SSM_IN_WIDTH)))
    return full


def trunk(w, x):
    ready = {('attn_w_qkv', 0): _full_weight('attn_w_qkv', gather_x(w['attn_w_qkv'][0], "gather_first"))}

    def lin(a, key, out_dtype, name, cols=None, carry=None, res=None):
        wt = ready[key] if cols is None else ready[key][:, cols]
        if carry is None:
            return linear(a, wt, res, out_dtype, name)
        y, parts = linear_x(a, wt, res, w[carry[0]][carry[1]], out_dtype, name)
        ready[carry] = _full_weight(carry[0], parts)
        return y

    for i in range(DEPTH):
        x, h = norm(x, w['mix_norm_w'][i], f"mixnorm{i}")
        j = i // 2
        ffn_next = [('ffn_w_up', i), ('ffn_w_down', i)]
        if i % 2 == 0:
            x = attention_mixer_p(x, h, lin, j, [('attn_w_o', j)] + ffn_next, f"attn{j}")
        else:
            x = ssd_mixer_p(x, h, lin, j, ffn_next, w['ssm_conv_w'][j], w['ssm_conv_b'][j], w['ssm_dt_bias'][j],
                            w['ssm_a_log'][j], w['ssm_d'][j], w['ssm_norm_w'][j], f"ssm{j}")
        if i + 1 == DEPTH:
            mixer_next = [None, None]
        elif i % 2 == 0:
            mixer_next = [('ssm_w_in', j), ('ssm_w_out', j)]
        else:
            mixer_next = [('attn_w_qkv', j + 1), None]
        x, h = norm(x, w['ffn_norm_w'][i], f"ffnnorm{i}")
        u0 = lin(h, ('ffn_w_up', i), F32, f"ffn{i}_up", carry=mixer_next[0])
        a = ffn_mid(u0, w['ffn_conv_w'][i], w['ffn_conv_b'][i], f"ffn{i}_mid")
        x = lin(a, ('ffn_w_down', i), F32, f"ffn{i}_down", carry=mixer_next[1], res=x)
    return x


def local_step(w, x, target):
    final_w = w['final_norm_w']
    trunk_w = {n: a for n, a in w.items() if n != 'final_norm_w'}
    xf, pullback = jax.vjp(trunk, trunk_w, x)
    loss, dxf, dfinal = loss_head(xf, final_w, target, "loss_head")
    gw, gx = pullback(dxf)
    gw['final_norm_w'] = dfinal
    return loss, gw, gx


def _adam_math(w, g, m, v):
    m = ADAM_B1 * m + (1.0 - ADAM_B1) * g
    v = ADAM_B2 * v + (1.0 - ADAM_B2) * (g * g)
    m_hat = m / (1.0 - ADAM_B1 ** ADAM_STEP)
    v_hat = v / (1.0 - ADAM_B2 ** ADAM_STEP)
    delta = -ADAM_LR * (m_hat / (jnp.sqrt(v_hat) + ADAM_EPS) + ADAM_WD * w)
    return delta, m, v


def _adamw_rows(g, w, m, v, name):
    r, c = w.shape
    tr = _pick(r, (256, 128, 64, 32, 16, 8))

    def body(g_ref, w_ref, m_ref, v_ref, d_out, m_out, v_out):
        delta, mm, vv = _adam_math(w_ref[...], g_ref[...], m_ref[...], v_ref[...])
        d_out[...] = delta
        m_out[...] = mm
        v_out[...] = vv

    blk = pl.BlockSpec((tr, c), lambda i: (i, 0))
    return pl.pallas_call(
        body, name=name, out_shape=[jax.ShapeDtypeStruct((r, c), F32)] * 3, grid=(r // tr,),
        in_specs=[blk] * 4, out_specs=[blk] * 3, compiler_params=_params("parallel"),
    )(g, w, m, v)


def _sum8(pieces, name):
    _, r, c = pieces.shape

    def body(p_ref, o_ref):
        g = p_ref[0]
        for j in range(1, N_DEV):
            g = g + p_ref[j]
        o_ref[...] = g

    return pl.pallas_call(
        body, name=name, out_shape=jax.ShapeDtypeStruct((r, c), F32),
        in_specs=[pl.BlockSpec(memory_space=pltpu.VMEM)], out_specs=pl.BlockSpec(memory_space=pltpu.VMEM),
    )(pieces)


def _adamw_plain(g, w, m, v, name):
    def body(g_ref, w_ref, m_ref, v_ref, d_out, m_out, v_out):
        delta, mm, vv = _adam_math(w_ref[...], g_ref[...], m_ref[...], v_ref[...])
        d_out[...] = delta
        m_out[...] = mm
        v_out[...] = vv

    vm = pl.BlockSpec(memory_space=pltpu.VMEM)
    return pl.pallas_call(
        body, name=name, out_shape=[jax.ShapeDtypeStruct(g.shape, F32)] * 3,
        in_specs=[vm] * 4, out_specs=[vm] * 3,
    )(g, w, m, v)


def _join8(parts, axis):
    t = jnp.moveaxis(parts, 0, axis)
    shp = t.shape
    return t.reshape(shp[:axis] + (shp[axis] * shp[axis + 1],) + shp[axis + 2:])


def _pack(arrs, lead, mult):
    flat = jnp.concatenate([a.reshape(a.shape[:lead] + (-1,)) for a in arrs], axis=-1)
    return _pad_rows(flat, mult)


def _unpack(buf, shapes, lead):
    flat = buf.reshape(buf.shape[:lead] + (-1,))
    out, off = [], 0
    for shp in shapes:
        n = math.prod(shp)
        out.append(flat[..., off:off + n].reshape(flat.shape[:lead] + tuple(shp)))
        off += n
    return out


def _own_shard(full, axis):
    size = full.shape[axis] // N_DEV
    return lax.dynamic_slice_in_dim(full, _my_index() * size, size, axis)


def kernel(x, mix_norm_w, attn_w_qkv, attn_w_o, ssm_w_in, ssm_conv_w, ssm_conv_b, ssm_dt_bias, ssm_a_log, ssm_d, ssm_norm_w, ssm_w_out, ffn_norm_w, ffn_w_up, ffn_conv_w, ffn_conv_b, ffn_w_down, final_norm_w, loss_target, m_mix_norm_w, m_attn_w_qkv, m_attn_w_o, m_ssm_w_in, m_ssm_conv_w, m_ssm_conv_b, m_ssm_dt_bias, m_ssm_a_log, m_ssm_d, m_ssm_norm_w, m_ssm_w_out, m_ffn_norm_w, m_ffn_w_up, m_ffn_conv_w, m_ffn_conv_b, m_ffn_w_down, m_final_norm_w, v_mix_norm_w, v_attn_w_qkv, v_attn_w_o, v_ssm_w_in, v_ssm_conv_w, v_ssm_conv_b, v_ssm_dt_bias, v_ssm_a_log, v_ssm_d, v_ssm_norm_w, v_ssm_w_out, v_ffn_norm_w, v_ffn_w_up, v_ffn_conv_w, v_ffn_conv_b, v_ffn_w_down, v_final_norm_w):
    args = dict(locals())
    w_sh = {n: args[n] for n in WEIGHT_NAMES}
    m_sh = {n: args["m_" + n] for n in WEIGHT_NAMES}
    v_sh = {n: args["v_" + n] for n in WEIGHT_NAMES}

    small_shapes = [w_sh[n].shape for n in SMALL_SHARDED]
    small = _exchange(_pack([w_sh[n] for n in SMALL_SHARDED], 0, 8), True, "gather_small")
    full = {n: w_sh[n] for n in SMALL if SHARD_AXIS[n] is None}
    for n, parts in zip(SMALL_SHARDED, _unpack(small, small_shapes, 1)):
        full[n] = _join8(parts, SHARD_AXIS[n])
    for n in BIG:
        full[n] = [w_sh[n][j] for j in range(w_sh[n].shape[0])]

    loss, gw, gx = local_step(full, x[0], loss_target[0])
    loss = lax.psum(loss, ("x", "y", "c"))
    for n in BIG:
        gw[n] = jnp.stack(gw[n])

    grads, deltas, new_m, new_v = {}, {}, {}, {}
    for n in BIG:
        shp = w_sh[n].shape
        two_d = (shp[0] * shp[1], shp[2])
        outs = _adamw_rows(*[t.reshape(two_d) for t in (gw[n], w_sh[n], m_sh[n], v_sh[n])], "adamw_" + n)
        grads[n] = gw[n]
        deltas[n], new_m[n], new_v[n] = [o.reshape(shp) for o in outs]

    small_full_shapes = [gw[n].shape for n in SMALL]
    gsmall = _exchange(_pack([gw[n] for n in SMALL], 0, 8), True, "gather_small_grads")
    gsmall = _unpack(_sum8(gsmall, "sum_small_grads"), small_full_shapes, 0)
    for n, g in zip(SMALL, gsmall):
        grads[n] = g if SHARD_AXIS[n] is None else _own_shard(g, SHARD_AXIS[n])
    shapes = [w_sh[n].shape for n in SMALL]
    outs = _adamw_plain(*[_pack([d[n] for n in SMALL], 0, 8) for d in (grads, w_sh, m_sh, v_sh)], "adamw_small")
    for d, buf in zip((deltas, new_m, new_v), outs):
        for n, a in zip(SMALL, _unpack(buf, shapes, 0)):
            d[n] = a

    return (loss, gx[None], *[grads[n] for n in WEIGHT_NAMES], *[deltas[n] for n in WEIGHT_NAMES],
            *[new_m[n] for n in WEIGHT_NAMES], *[new_v[n] for n in WEIGHT_NAMES])
```

```python
import functools
import math

import jax
import jax.numpy as jnp
from jax import lax
from jax.experimental import pallas as pl
from jax.experimental.pallas import tpu as pltpu

F32 = jnp.float32
BF16 = jnp.bfloat16
N_DEV = 8
MESH_ID = pl.DeviceIdType.MESH

D_MODEL = 1024
DEPTH = 4
ATTN_HEADS = 8
ATTN_HEAD_DIM = 128
ATTN_DILATIONS = (1, 4, 16)
ATTN_STEPS = (128, 128, 128)
N_ATTN_GROUPS = 3
ATTN_BLOCK = 128
ROPE_THETA = 500000.0
ROPE_DIM = 32
ATTN_OUT_WIDTH = 1024
SSM_D_INNER = 2048
SSM_HEAD_DIM = 64
SSM_HEADS = 32
SSM_STATE = 128
SSM_GROUPS = 8
SSM_CHUNK = 128
SSM_CONV_DIM = 4096
SSM_IN_WIDTH = 6176
SSM_IN_PAD = 6272
D_FF = 2816
NORM_EPS = 1e-5
ADAM_LR = 0.001
ADAM_B1 = 0.9
ADAM_B2 = 0.999
ADAM_EPS = 1e-08
ADAM_WD = 0.01
ADAM_STEP = 10

WEIGHT_NAMES = ['mix_norm_w', 'attn_w_qkv', 'attn_w_o', 'ssm_w_in', 'ssm_conv_w', 'ssm_conv_b', 'ssm_dt_bias',
                'ssm_a_log', 'ssm_d', 'ssm_norm_w', 'ssm_w_out', 'ffn_norm_w', 'ffn_w_up', 'ffn_conv_w',
                'ffn_conv_b', 'ffn_w_down', 'final_norm_w']
SHARD_AXIS = {'mix_norm_w': None, 'attn_w_qkv': 2, 'attn_w_o': 1, 'ssm_w_in': 2, 'ssm_conv_w': 2, 'ssm_conv_b': 1,
              'ssm_dt_bias': None, 'ssm_a_log': None, 'ssm_d': None, 'ssm_norm_w': 1, 'ssm_w_out': 1,
              'ffn_norm_w': None, 'ffn_w_up': 2, 'ffn_conv_w': 2, 'ffn_conv_b': None, 'ffn_w_down': 1,
              'final_norm_w': None}
BIG = ['attn_w_qkv', 'attn_w_o', 'ssm_w_in', 'ssm_w_out', 'ffn_w_up', 'ffn_w_down']
SMALL = [n for n in WEIGHT_NAMES if n not in BIG]
SMALL_SHARDED = [n for n in SMALL if SHARD_AXIS[n] is not None]
LANES = 1024


def _my_index():
    return 4 * lax.axis_index("x") + 2 * lax.axis_index("y") + lax.axis_index("c")


def _peer(k):
    x, y, c = lax.axis_index("x"), lax.axis_index("y"), lax.axis_index("c")
    return (x ^ ((k >> 2) & 1), y ^ ((k >> 1) & 1), c ^ (k & 1))


def _exchange(src, gather, name):
    def body(src_ref, out_ref, send_sems, recv_sems, local_sem):
        start, wait = _exchange_copies(src_ref, out_ref, send_sems, recv_sems, local_sem, gather)
        start()
        wait()

    return pl.pallas_call(
        body, name=name,
        out_shape=_exchange_out(src, gather),
        in_specs=[pl.BlockSpec(memory_space=pl.ANY)],
        out_specs=pl.BlockSpec(memory_space=pl.ANY),
        scratch_shapes=list(EXCHANGE_SEMS),
    )(src)


EXCHANGE_SEMS = (pltpu.SemaphoreType.DMA((N_DEV - 1,)), pltpu.SemaphoreType.DMA((N_DEV - 1,)),
                 pltpu.SemaphoreType.DMA)


def _exchange_out(src, gather):
    return jax.ShapeDtypeStruct((N_DEV,) + (src.shape if gather else src.shape[1:]), src.dtype)


def _exchange_copies(src_ref, out_ref, send_sems, recv_sems, local_sem, gather):
    me = _my_index()

    def piece(j):
        return src_ref if gather else src_ref.at[j]

    def remote(k, slab):
        return pltpu.make_async_remote_copy(
            src_ref=piece(me ^ k), dst_ref=out_ref.at[slab], send_sem=send_sems.at[k - 1],
            recv_sem=recv_sems.at[k - 1], device_id=_peer(k), device_id_type=MESH_ID)

    mine = pltpu.make_async_copy(piece(me), out_ref.at[me], local_sem)
    arrivals = {k: remote(k, me ^ k) for k in range(1, N_DEV)}
    if not gather:
        sends = [remote(k, me) for k in range(1, N_DEV)]

        def start():
            mine.start()
            for cp in sends:
                cp.start()

        def wait():
            for cp in arrivals.values():
                cp.wait_recv()
            for cp in sends:
                cp.wait_send()
            mine.wait()

        return start, wait

    far = (2, 4, 6)
    sends = [remote(k, me) for k in (1,) + far]
    passed_on = [pltpu.make_async_remote_copy(
        src_ref=out_ref.at[me ^ k], dst_ref=out_ref.at[me ^ k], send_sem=send_sems.at[k],
        recv_sem=recv_sems.at[k], device_id=_peer(1), device_id_type=MESH_ID) for k in far]

    def start():
        mine.start()
        for cp in sends:
            cp.start()

    def wait():
        for k, cp in zip(far, passed_on):
            arrivals[k].wait_recv()
            cp.start()
        for k in (1, 3, 5, 7):
            arrivals[k].wait_recv()
        for cp in sends + passed_on:
            cp.wait_send()
        mine.wait()

    return start, wait


def _pad_rows(flat, mult):
    n = flat.shape[-1]
    rows = -(-n // (LANES * mult)) * mult
    pad = rows * LANES - n
    flat = jnp.pad(flat, [(0, 0)] * (flat.ndim - 1) + [(0, pad)])
    return flat.reshape(flat.shape[:-1] + (rows, LANES))


TILE_TRIALS = {
    "ffn1_up_fwd": (2048, None, None), "ffn1_up_da": (2048, None, None), "ffn1_up_dw": (None, None, 2048),
    "ffn1_down_da": (2048, None, None), "ffn1_down_dw": (None, None, 2048),
    "ffn2_up_fwd": (None, 2816, None), "ffn2_up_da": (None, None, 2816), "ffn2_up_dw": (None, 2816, None),
    "ffn2_down_fwd": (None, None, None), "ffn2_down_da": (None, 2816, None), "ffn2_down_dw": (2816, None, None),
    "ffn3_up_fwd": (512, None, None), "ffn3_up_da": (512, None, None), "ffn3_up_dw": (512, None, 512),
    "ffn3_down_fwd": (512, None, None), "ffn3_down_da": (512, None, None), "ffn3_down_dw": (None, None, 512),
}


def _pick(n, cands):
    for c in cands:
        if n % c == 0:
            return c
    return n


def _matmul(a, b, ta, tb, out_dtype, name, rider=None, res=None):
    (m, k) = (a.shape[1], a.shape[0]) if ta else a.shape
    (k2, n) = (b.shape[1], b.shape[0]) if tb else b.shape
    assert k == k2, (a.shape, b.shape, ta, tb)
    tm = _pick(m, (1024, 1408, 512, 256, 128))
    tn = _pick(n, (1536, 1408, 1024, 896, 512, 384, 256, 128))
    tk = _pick(k, (1024, 1408, 896, 512, 256, 128)) if k > 2816 else k
    tm, tn, tk = [o or t for o, t in zip(TILE_TRIALS.get(name, (None,) * 3), (tm, tn, tk))]
    nk = k // tk
    dims = (((0 if ta else 1,), (1 if tb else 0,)), ((), ()))

    grid = (n // tn, m // tm, nk)

    n_in = 2 + (res is not None) + (rider is not None)

    def body(*refs):
        ins, rest = refs[:n_in], refs[n_in:]
        a_ref, b_ref = ins[:2]
        res_ref = ins[2] if res is not None else None
        o_ref, scratch = rest[0], rest[1:]
        if rider is not None:
            start, wait = _exchange_copies(ins[-1], rest[1], *scratch[-3:], rider[1])
            scratch = scratch[1:-3]
            at = [pl.program_id(ax) for ax in range(3)]
            pl.when(functools.reduce(jnp.logical_and, [p == 0 for p in at]))(start)
        part = lax.dot_general(a_ref[...].astype(BF16), b_ref[...].astype(BF16), dims,
                               preferred_element_type=F32)

        def finish(total):
            if res_ref is not None:
                total = total + res_ref[...]
            o_ref[...] = total.astype(o_ref.dtype)

        if nk == 1:
            finish(part)
        else:
            acc_ref, = scratch
            kk = pl.program_id(2)

            @pl.when(kk == 0)
            def _():
                acc_ref[...] = part

            @pl.when(kk > 0)
            def _():
                acc_ref[...] += part

            @pl.when(kk == nk - 1)
            def _():
                finish(acc_ref[...])
        if rider is not None:
            pl.when(functools.reduce(jnp.logical_and, [p == g - 1 for p, g in zip(at, grid)]))(wait)

    a_spec = (pl.BlockSpec((tk, tm), lambda j, i, kk: (kk, i)) if ta
              else pl.BlockSpec((tm, tk), lambda j, i, kk: (i, kk)))
    b_spec = (pl.BlockSpec((tn, tk), lambda j, i, kk: (j, kk)) if tb
              else pl.BlockSpec((tk, tn), lambda j, i, kk: (kk, j)))
    any_spec = pl.BlockSpec(memory_space=pl.ANY)
    tile_spec = pl.BlockSpec((tm, tn), lambda j, i, kk: (i, j))
    in_specs, operands = [a_spec, b_spec], [a, b]
    out_shape, out_specs = [jax.ShapeDtypeStruct((m, n), out_dtype)], [tile_spec]
    scratch = [] if nk == 1 else [pltpu.VMEM((tm, tn), F32)]
    sem = ("parallel", "parallel", "arbitrary")
    if res is not None:
        in_specs.append(tile_spec)
        operands.append(res)
    if rider is not None:
        in_specs.append(any_spec)
        operands.append(rider[0])
        out_shape.append(_exchange_out(*rider))
        out_specs.append(any_spec)
        scratch += list(EXCHANGE_SEMS)
        sem = ("arbitrary",) * 3
    outs = pl.pallas_call(
        body, name=name, out_shape=out_shape, grid=grid, in_specs=in_specs, out_specs=out_specs,
        scratch_shapes=scratch, compiler_params=_params(*sem),
    )(*operands)
    return outs[0] if rider is None else tuple(outs)


@functools.partial(jax.custom_vjp, nondiff_argnums=(3, 4))
def linear(a, w, res, out_dtype, name):
    return _matmul(a, w, False, False, out_dtype, name + "_fwd", res=res)


def _linear_fwd(a, w, res, out_dtype, name):
    return _matmul(a, w, False, False, out_dtype, name + "_fwd", res=res), (a, w, res is not None)


def _linear_bwd(out_dtype, name, saved, dy):
    a, w, has_res = saved
    da = _matmul(dy, w, False, True, a.dtype, name + "_da")
    dw = _matmul(a, dy, True, False, w.dtype, name + "_dw")
    return da, dw, (dy if has_res else None)


linear.defvjp(_linear_fwd, _linear_bwd)


def _sum_pieces(pieces, name):
    shape = pieces.shape[1:]
    c = shape[-1]
    r = math.prod(shape[:-1])
    tr = _pick(r, (512, 256, 128, 64, 32, 16, 8))

    def body(p_ref, o_ref):
        g = p_ref[0].astype(F32)
        for j in range(1, N_DEV):
            g = g + p_ref[j].astype(F32)
        o_ref[...] = g

    return pl.pallas_call(
        body, name=name, out_shape=jax.ShapeDtypeStruct((r, c), F32), grid=(r // tr,),
        in_specs=[pl.BlockSpec((N_DEV, tr, c), lambda i: (0, i, 0))],
        out_specs=pl.BlockSpec((tr, c), lambda i: (i, 0)),
        compiler_params=_params("parallel"),
    )(pieces.reshape(N_DEV, r, c)).reshape(shape)


@functools.partial(jax.custom_vjp, nondiff_argnums=(1,))
def gather_x(shard, name):
    return _exchange(shard.astype(BF16), True, name)


def _gather_x_fwd(shard, name):
    return _exchange(shard.astype(BF16), True, name), None


def _gather_x_bwd(name, _, dparts):
    return (_sum_pieces(_exchange(dparts, False, name + "_back"), name + "_sum"),)


gather_x.defvjp(_gather_x_fwd, _gather_x_bwd)


@functools.partial(jax.custom_vjp, nondiff_argnums=(4, 5))
def linear_x(a, w, res, shard, out_dtype, name):
    return _matmul(a, w, False, False, out_dtype, name + "_fwd", rider=(shard.astype(BF16), True), res=res)


def _linear_x_fwd(a, w, res, shard, out_dtype, name):
    out = _matmul(a, w, False, False, out_dtype, name + "_fwd", rider=(shard.astype(BF16), True), res=res)
    return out, (a, w, res is not None)


def _linear_x_bwd(out_dtype, name, saved, cts):
    a, w, has_res = saved
    dy, dparts = cts
    half = dparts.shape[1] // 2
    da, moved_a = _matmul(dy, w, False, True, a.dtype, name + "_da", rider=(dparts[:, :half], False))
    dw, moved_b = _matmul(a, dy, True, False, w.dtype, name + "_dw", rider=(dparts[:, half:], False))
    dshard = jnp.concatenate([_sum_pieces(moved_a, name + "_sum_a"), _sum_pieces(moved_b, name + "_sum_b")])
    return da, dw, (dy if has_res else None), dshard


linear_x.defvjp(_linear_x_fwd, _linear_x_bwd)


VMEM_LIMIT = 56 * 1024 * 1024
SUBLANES = 8


def _params(*sem):
    return pltpu.CompilerParams(dimension_semantics=sem, vmem_limit_bytes=VMEM_LIMIT)


def _sigmoid(x):
    return 0.5 * jnp.tanh(0.5 * x) + 0.5


def _rstd(xv):
    return lax.rsqrt(jnp.mean(xv * xv, axis=-1, keepdims=True) + NORM_EPS)


def _accumulate(ref, part, first):
    @pl.when(first)
    def _():
        ref[...] = part

    @pl.when(jnp.logical_not(first))
    def _():
        ref[...] += part


def _norm_fwd(x, w, name):
    s, d = x.shape
    tm = _pick(s, (512, 256, 128))

    def body(x_ref, w_ref, h_ref):
        xv = x_ref[...]
        h_ref[...] = (xv * _rstd(xv) * w_ref[...]).astype(h_ref.dtype)

    return pl.pallas_call(
        body, name=name, out_shape=jax.ShapeDtypeStruct((s, d), BF16), grid=(s // tm,),
        in_specs=[pl.BlockSpec((tm, d), lambda i: (i, 0)), pl.BlockSpec((1, d), lambda i: (0, 0))],
        out_specs=pl.BlockSpec((tm, d), lambda i: (i, 0)), compiler_params=_params("parallel"),
    )(x, w.reshape(1, d))


def _norm_bwd(x, w, dh, dskip, name):
    s, d = x.shape
    tm = _pick(s, (512, 256, 128))

    def body(x_ref, w_ref, dh_ref, ds_ref, dx_ref, dw_ref):
        xv = x_ref[...]
        r = _rstd(xv)
        y = xv * r
        dhv = dh_ref[...].astype(F32)
        dy = dhv * w_ref[...]
        dx_ref[...] = ds_ref[...] + r * (dy - y * jnp.mean(dy * y, axis=-1, keepdims=True))
        _accumulate(dw_ref, jnp.sum(dhv * y, axis=0, keepdims=True), pl.program_id(0) == 0)

    row = pl.BlockSpec((tm, d), lambda i: (i, 0))
    vec = pl.BlockSpec((1, d), lambda i: (0, 0))
    dx, dw = pl.pallas_call(
        body, name=name,
        out_shape=[jax.ShapeDtypeStruct((s, d), F32), jax.ShapeDtypeStruct((1, d), F32)], grid=(s // tm,),
        in_specs=[row, vec, row, row], out_specs=[row, vec], compiler_params=_params("arbitrary"),
    )(x, w.reshape(1, d), dh, dskip)
    return dx, dw.reshape(d)


@functools.partial(jax.custom_vjp, nondiff_argnums=(2,))
def norm(x, w, name):
    return x, _norm_fwd(x, w, name + "_fwd")


def _norm_vjp_fwd(x, w, name):
    return (x, _norm_fwd(x, w, name + "_fwd")), (x, w)


def _norm_vjp_bwd(name, saved, cts):
    x, w = saved
    dskip, dh = cts
    return _norm_bwd(x, w, dh, dskip, name + "_bwd")


norm.defvjp(_norm_vjp_fwd, _norm_vjp_bwd)


def loss_head(x, w, target, name):
    s, d = x.shape
    tm = _pick(s, (512, 256, 128))

    def body(x_ref, w_ref, t_ref, loss_ref, dx_ref, dw_ref):
        first = pl.program_id(0) == 0
        xv = x_ref[...]
        r = _rstd(xv)
        y = xv * r
        err = y * w_ref[...] - t_ref[...]
        part = 0.5 * jnp.sum(jnp.sum(err * err, axis=-1, keepdims=True), axis=0, keepdims=True) / d
        _accumulate(loss_ref, jnp.broadcast_to(part, loss_ref.shape), first)
        dout = err / d
        dy = dout * w_ref[...]
        dx_ref[...] = r * (dy - y * jnp.mean(dy * y, axis=-1, keepdims=True))
        _accumulate(dw_ref, jnp.sum(dout * y, axis=0, keepdims=True), first)

    row = pl.BlockSpec((tm, d), lambda i: (i, 0))
    vec = pl.BlockSpec((1, d), lambda i: (0, 0))
    loss, dx, dw = pl.pallas_call(
        body, name=name,
        out_shape=[jax.ShapeDtypeStruct((1, 128), F32), jax.ShapeDtypeStruct((s, d), F32),
                   jax.ShapeDtypeStruct((1, d), F32)],
        grid=(s // tm,), in_specs=[row, vec, row],
        out_specs=[pl.BlockSpec((1, 128), lambda i: (0, 0)), row, vec],
        compiler_params=_params("arbitrary"),
    )(x, w.reshape(1, d), target)
    return loss[0, 0], dx, dw.reshape(d)


FFN_ROWS = 256
FFN_COLS = 256


def _shift_down(cur, halo, k):
    out = pltpu.roll(cur, k, axis=0)
    row = lax.broadcasted_iota(jnp.int32, halo.shape, 0)
    top = out[0:SUBLANES]
    for j in range(k):
        top = jnp.where(row == j, halo[SUBLANES - k + j:SUBLANES - k + j + 1, :], top)
    return jnp.concatenate([top, out[SUBLANES:]], axis=0)


def _shift_up(cur, nxt, k):
    n = cur.shape[0]
    out = pltpu.roll(cur, n - k, axis=0)
    row = lax.broadcasted_iota(jnp.int32, nxt.shape, 0)
    bottom = out[n - SUBLANES:]
    for j in range(k):
        bottom = jnp.where(row == SUBLANES - k + j, nxt[j:j + 1, :], bottom)
    return jnp.concatenate([out[:n - SUBLANES], bottom], axis=0)


def _conv_taps(cur, halo, ntaps):
    return [_shift_down(cur, halo, ntaps - 1 - k) if k < ntaps - 1 else cur for k in range(ntaps)]


def _pad_taps(conv_w):
    return jnp.pad(conv_w, ((0, SUBLANES - conv_w.shape[0]), (0, 0)))


def _ffn_mid_fwd(u0, conv_w, conv_b, name):
    s, width = u0.shape
    half = width // 2
    tm = _pick(s, (FFN_ROWS, 128))
    per = tm // SUBLANES

    def body(w_ref, b_ref, u_ref, halo_ref, a_ref):
        keep = pl.program_id(0) > 0
        for c0 in range(0, half, FFN_COLS):
            vals = []
            for base in (c0, half + c0):
                cols = slice(base, base + FFN_COLS)
                halo = jnp.where(keep, halo_ref[:, cols], 0.0)
                taps = _conv_taps(u_ref[:, cols], halo, 3)
                vals.append(sum(w_ref[k:k + 1, cols] * taps[k] for k in range(3)) + b_ref[:, cols])
            gate, up = vals
            a_ref[:, c0:c0 + FFN_COLS] = (gate * _sigmoid(gate) * up).astype(a_ref.dtype)

    return pl.pallas_call(
        body, name=name, out_shape=jax.ShapeDtypeStruct((s, half), BF16), grid=(s // tm,),
        in_specs=[pl.BlockSpec((SUBLANES, width), lambda i: (0, 0)), pl.BlockSpec((1, width), lambda i: (0, 0)),
                  pl.BlockSpec((tm, width), lambda i: (i, 0)),
                  pl.BlockSpec((SUBLANES, width), lambda i: (jnp.maximum(i * per - 1, 0), 0))],
        out_specs=pl.BlockSpec((tm, half), lambda i: (i, 0)), compiler_params=_params("parallel"),
    )(_pad_taps(conv_w), conv_b.reshape(1, width), u0, u0)


def _ffn_mid_bwd(u0, conv_w, conv_b, da, name):
    s, width = u0.shape
    half = width // 2
    tm = _pick(s, (FFN_ROWS, 128))
    per = tm // SUBLANES
    nt = s // tm

    def body(w_ref, b_ref, u_ref, halo_ref, da_ref, du0_ref, dw_ref, db_ref, carry_ref):
        step = pl.program_id(0)
        first = step == 0
        keep = step < nt - 1
        for c0 in range(0, half, FFN_COLS):
            dav = da_ref[:, c0:c0 + FFN_COLS].astype(F32)
            taps, vals = [], []
            for base in (c0, half + c0):
                cols = slice(base, base + FFN_COLS)
                halo = jnp.where(keep, halo_ref[:, cols], 0.0)
                tp = _conv_taps(u_ref[:, cols], halo, 3)
                taps.append(tp)
                vals.append(sum(w_ref[k:k + 1, cols] * tp[k] for k in range(3)) + b_ref[:, cols])
            gate, up = vals
            sig = _sigmoid(gate)
            dus = [dav * up * sig * (1.0 + gate * (1.0 - sig)), dav * gate * sig]
            for base, tp, du in zip((c0, half + c0), taps, dus):
                cols = slice(base, base + FFN_COLS)
                nxt = jnp.where(first, 0.0, carry_ref[:, cols])
                du0 = (w_ref[2:3, cols] * du + w_ref[1:2, cols] * _shift_up(du, nxt, 1)
                       + w_ref[0:1, cols] * _shift_up(du, nxt, 2))
                du0_ref[:, cols] = du0.astype(du0_ref.dtype)
                carry_ref[:, cols] = du[0:SUBLANES, :]
                dwp = jnp.concatenate([jnp.sum(du * tp[k], axis=0, keepdims=True) for k in range(3)]
                                      + [jnp.zeros((SUBLANES - 3, FFN_COLS), F32)], axis=0)
                dbp = jnp.sum(du, axis=0, keepdims=True)

                @pl.when(first)
                def _():
                    dw_ref[:, cols] = dwp
                    db_ref[:, cols] = dbp

                @pl.when(jnp.logical_not(first))
                def _():
                    dw_ref[:, cols] += dwp
                    db_ref[:, cols] += dbp

    rev = lambda i: nt - 1 - i
    du0, dw, db = pl.pallas_call(
        body, name=name,
        out_shape=[jax.ShapeDtypeStruct((s, width), BF16), jax.ShapeDtypeStruct((SUBLANES, width), F32),
                   jax.ShapeDtypeStruct((1, width), F32)],
        grid=(nt,),
        in_specs=[pl.BlockSpec((SUBLANES, width), lambda i: (0, 0)), pl.BlockSpec((1, width), lambda i: (0, 0)),
                  pl.BlockSpec((tm, width), lambda i: (rev(i), 0)),
                  pl.BlockSpec((SUBLANES, width), lambda i: (jnp.maximum(rev(i) * per - 1, 0), 0)),
                  pl.BlockSpec((tm, half), lambda i: (rev(i), 0))],
        out_specs=[pl.BlockSpec((tm, width), lambda i: (rev(i), 0)),
                   pl.BlockSpec((SUBLANES, width), lambda i: (0, 0)), pl.BlockSpec((1, width), lambda i: (0, 0))],
        scratch_shapes=[pltpu.VMEM((SUBLANES, width), F32)],
        compiler_params=_params("arbitrary"),
    )(_pad_taps(conv_w), conv_b.reshape(1, width), u0, u0, da)
    return du0, dw[:3], db.reshape(width)


@functools.partial(jax.custom_vjp, nondiff_argnums=(3,))
def ffn_mid(u0, conv_w, conv_b, name):
    return _ffn_mid_fwd(u0, conv_w, conv_b, name + "_fwd")


def _ffn_mid_vjp_fwd(u0, conv_w, conv_b, name):
    return _ffn_mid_fwd(u0, conv_w, conv_b, name + "_fwd"), (u0, conv_w, conv_b)


def _ffn_mid_vjp_bwd(name, res, da):
    u0, conv_w, conv_b = res
    return _ffn_mid_bwd(u0, conv_w, conv_b, da, name + "_bwd")


ffn_mid.defvjp(_ffn_mid_vjp_fwd, _ffn_mid_vjp_bwd)


NEG = -1e30
HEAD_SLICES = [slice(hh * ATTN_HEAD_DIM, (hh + 1) * ATTN_HEAD_DIM) for hh in range(ATTN_HEADS)]
ATTN_SCALE = ATTN_HEAD_DIM ** -0.5
QKV_GROUP = 3 * ATTN_OUT_WIDTH


def rope_table(seq, d):
    pos = (jnp.arange(seq // d, dtype=jnp.int32)[None, :] * d + jnp.arange(d, dtype=jnp.int32)[:, None])
    inv_freq = ROPE_THETA ** (-jnp.arange(0, ROPE_DIM, 2, dtype=F32) / ROPE_DIM)
    ang = pos.reshape(seq).astype(F32)[:, None] * inv_freq[None, :]
    cos, sin = jnp.cos(ang), jnp.sin(ang)
    half = ROPE_DIM // 2
    ones = jnp.ones((seq, ATTN_HEAD_DIM - ROPE_DIM), F32)
    zero = lambda n: jnp.zeros((seq, n), F32)
    return jnp.concatenate([cos, cos, ones, -sin, zero(ATTN_HEAD_DIM - half),
                            zero(half), sin, zero(ATTN_HEAD_DIM - ROPE_DIM)], axis=1)


def _rope(t, tab, sign):
    half = ROPE_DIM // 2
    return t * tab[:, 0:128] + sign * (pltpu.roll(t, ATTN_HEAD_DIM - half, axis=1) * tab[:, 128:256]
                                       + pltpu.roll(t, half, axis=1) * tab[:, 256:384])


def _to_dilated(a, d):
    s = a.shape[0]
    return a if d == 1 else a.reshape(s // d, d, -1).transpose(1, 0, 2).reshape(s, -1)


def _from_dilated(a, d):
    s = a.shape[0]
    return a if d == 1 else a.reshape(d, s // d, -1).transpose(1, 0, 2).reshape(s, -1)


def _rope_qk(qkv, tab, name):
    s = qkv.shape[0]
    tm = _pick(s, (512, 256, 128))

    def body(t_ref, x_ref, o_ref):
        for hs in HEAD_SLICES:
            o_ref[:, hs] = _rope(x_ref[:, hs].astype(F32), t_ref[...], 1.0).astype(o_ref.dtype)

    blk = pl.BlockSpec((tm, ATTN_OUT_WIDTH), lambda i, c: (i, c))
    return pl.pallas_call(
        body, name=name, out_shape=jax.ShapeDtypeStruct(qkv.shape, qkv.dtype), grid=(s // tm, 2),
        in_specs=[pl.BlockSpec((tm, 384), lambda i, c: (i, 0)), blk], out_specs=blk,
        input_output_aliases={1: 0}, compiler_params=_params("parallel", "parallel"),
    )(tab, qkv)


def _dot_nt(a, b):
    return lax.dot_general(a, b, (((1,), (1,)), ((), ())), preferred_element_type=F32)


def _dot_tn(a, b):
    return lax.dot_general(a, b, (((0,), (0,)), ((), ())), preferred_element_type=F32)


def _dot(a, b):
    return jnp.dot(a, b, preferred_element_type=F32)


def _window_mask(has_prev):
    ii = lax.broadcasted_iota(jnp.int32, (ATTN_BLOCK, 2 * ATTN_BLOCK), 0)
    jj = lax.broadcasted_iota(jnp.int32, (ATTN_BLOCK, 2 * ATTN_BLOCK), 1)
    in_window = jnp.logical_and(jj >= ii, jj <= ii + ATTN_BLOCK)
    return jnp.logical_and(in_window, jnp.logical_or(jj >= ATTN_BLOCK, has_prev))


def _both(prev_ref, cur_ref, hs):
    return jnp.concatenate([prev_ref[:, hs], cur_ref[:, hs]], axis=0)


def _attn_group_fwd(qkv, d, name):
    s = qkv.shape[0]
    nb = s // d // ATTN_BLOCK

    def body(q_ref, kc_ref, kp_ref, vc_ref, vp_ref, o_ref, lse_ref):
        mask = _window_mask(pl.program_id(1) > 0)
        lane = lax.broadcasted_iota(jnp.int32, (ATTN_BLOCK, 128), 1)
        lse_tile = jnp.zeros((ATTN_BLOCK, 128), F32)
        for hh, hs in enumerate(HEAD_SLICES):
            sc = jnp.where(mask, _dot_nt(q_ref[:, hs], _both(kp_ref, kc_ref, hs)) * ATTN_SCALE, NEG)
            m = jnp.max(sc, axis=1, keepdims=True)
            p = jnp.exp(sc - m)
            den = jnp.sum(p, axis=1, keepdims=True)
            o_ref[:, hs] = _dot(p.astype(BF16), _both(vp_ref, vc_ref, hs)) / den
            lse_tile = jnp.where(lane == hh, m + jnp.log(den), lse_tile)
        lse_ref[...] = lse_tile

    cur = lambda t: pl.BlockSpec((ATTN_BLOCK, ATTN_OUT_WIDTH), lambda r, n: (r * nb + n, t))
    prv = lambda t: pl.BlockSpec((ATTN_BLOCK, ATTN_OUT_WIDTH), lambda r, n: (r * nb + jnp.maximum(n - 1, 0), t))
    return pl.pallas_call(
        body, name=name,
        out_shape=[jax.ShapeDtypeStruct((s, ATTN_OUT_WIDTH), F32), jax.ShapeDtypeStruct((s, 128), F32)],
        grid=(d, nb), in_specs=[cur(0), cur(1), prv(1), cur(2), prv(2)],
        out_specs=[pl.BlockSpec((ATTN_BLOCK, ATTN_OUT_WIDTH), lambda r, n: (r * nb + n, 0)),
                   pl.BlockSpec((ATTN_BLOCK, 128), lambda r, n: (r * nb + n, 0))],
        compiler_params=_params("parallel", "parallel"),
    )(qkv, qkv, qkv, qkv, qkv)


def _attn_combine(os_, lses, name):
    s = os_[0].shape[0]
    tm = _pick(s, (256, 128))
    ng = len(os_)

    def body(*refs):
        o_refs, l_refs, (o_ref, lse_ref) = refs[:ng], refs[ng:2 * ng], refs[2 * ng:]
        lane = lax.broadcasted_iota(jnp.int32, (tm, 128), 1)
        lse_tile = jnp.zeros((tm, 128), F32)
        for hh, hs in enumerate(HEAD_SLICES):
            ls = [l_ref[:, hh:hh + 1] for l_ref in l_refs]
            m = functools.reduce(jnp.maximum, ls)
            ws = [jnp.exp(l - m) for l in ls]
            tot = functools.reduce(lambda a, b: a + b, ws)
            acc = functools.reduce(lambda a, b: a + b, [o_r[:, hs] * w for o_r, w in zip(o_refs, ws)])
            o_ref[:, hs] = (acc / tot).astype(o_ref.dtype)
            lse_tile = jnp.where(lane == hh, m + jnp.log(tot), lse_tile)
        lse_ref[...] = lse_tile

    wide = pl.BlockSpec((tm, ATTN_OUT_WIDTH), lambda i: (i, 0))
    thin = pl.BlockSpec((tm, 128), lambda i: (i, 0))
    return pl.pallas_call(
        body, name=name,
        out_shape=[jax.ShapeDtypeStruct((s, ATTN_OUT_WIDTH), BF16), jax.ShapeDtypeStruct((s, 128), F32)],
        grid=(s // tm,), in_specs=[wide] * ng + [thin] * ng, out_specs=[wide, thin],
        compiler_params=_params("parallel"),
    )(*os_, *lses)


def _attn_delta(do, o, name):
    s = do.shape[0]
    tm = _pick(s, (256, 128))

    def body(do_ref, o_ref, out_ref):
        lane = lax.broadcasted_iota(jnp.int32, (tm, 128), 1)
        tile = jnp.zeros((tm, 128), F32)
        for hh, hs in enumerate(HEAD_SLICES):
            prod = do_ref[:, hs].astype(F32) * o_ref[:, hs].astype(F32)
            tile = jnp.where(lane == hh, jnp.sum(prod, axis=1, keepdims=True), tile)
        out_ref[...] = tile

    wide = pl.BlockSpec((tm, ATTN_OUT_WIDTH), lambda i: (i, 0))
    return pl.pallas_call(
        body, name=name, out_shape=jax.ShapeDtypeStruct((s, 128), F32), grid=(s // tm,),
        in_specs=[wide, wide], out_specs=pl.BlockSpec((tm, 128), lambda i: (i, 0)),
        compiler_params=_params("parallel"),
    )(do, o)


def _attn_group_bwd(qkv, do, lse, delta, tab, d, name):
    s = qkv.shape[0]
    nb = s // d // ATTN_BLOCK
    wide = ATTN_OUT_WIDTH

    def body(q_ref, qn_ref, k_ref, v_ref, do_ref, don_ref, lse_ref, lsen_ref, dl_ref, dln_ref, tab_ref,
             out_ref, carry_ref):
        n = pl.program_id(1)

        @pl.when(n == 0)
        def _():
            carry_ref[...] = jnp.zeros_like(carry_ref)

        rows = lax.broadcasted_iota(jnp.int32, (2 * ATTN_BLOCK, ATTN_BLOCK), 0)
        keys = lax.broadcasted_iota(jnp.int32, (2 * ATTN_BLOCK, ATTN_BLOCK), 1)
        own = jnp.logical_and(rows < ATTN_BLOCK, keys <= rows)
        nxt = jnp.logical_and(rows >= ATTN_BLOCK, jnp.logical_and(keys >= rows - ATTN_BLOCK, n + 1 < nb))
        mask = jnp.logical_or(own, nxt)
        both = lambda a_ref, b_ref, cols: jnp.concatenate([a_ref[:, cols], b_ref[:, cols]], axis=0)
        for hh, hs in enumerate(HEAD_SLICES):
            one = slice(hh, hh + 1)
            q2, do2 = both(q_ref, qn_ref, hs), both(do_ref, don_ref, hs)
            kh, vh = k_ref[:, hs], v_ref[:, hs]
            sc = jnp.where(mask, _dot_nt(q2, kh) * ATTN_SCALE, NEG)
            p = jnp.exp(sc - both(lse_ref, lsen_ref, one))
            ds = (p * (_dot_nt(do2, vh) - both(dl_ref, dln_ref, one)) * ATTN_SCALE).astype(BF16)
            dq2 = _dot(ds, kh)
            dq = carry_ref[:, hs] + dq2[:ATTN_BLOCK]
            carry_ref[:, hs] = dq2[ATTN_BLOCK:]
            out_ref[:, hs] = _rope(dq, tab_ref[...], -1.0).astype(out_ref.dtype)
            out_ref[:, wide + hh * ATTN_HEAD_DIM:wide + (hh + 1) * ATTN_HEAD_DIM] = _rope(
                _dot_tn(ds, q2), tab_ref[...], -1.0).astype(out_ref.dtype)
            out_ref[:, 2 * wide + hh * ATTN_HEAD_DIM:2 * wide + (hh + 1) * ATTN_HEAD_DIM] = _dot_tn(
                p.astype(BF16), do2).astype(out_ref.dtype)

    def spec(width, row, col):
        return pl.BlockSpec((ATTN_BLOCK, width), lambda r, n: (r * nb + row(n), col))

    cur = lambda n: n
    nxt_block = lambda n: jnp.minimum(n + 1, nb - 1)
    return pl.pallas_call(
        body, name=name, out_shape=jax.ShapeDtypeStruct((s, QKV_GROUP), BF16), grid=(d, nb),
        in_specs=[spec(wide, cur, 0), spec(wide, nxt_block, 0), spec(wide, cur, 1), spec(wide, cur, 2),
                  spec(wide, cur, 0), spec(wide, nxt_block, 0), spec(128, cur, 0), spec(128, nxt_block, 0),
                  spec(128, cur, 0), spec(128, nxt_block, 0), spec(384, cur, 0)],
        out_specs=spec(QKV_GROUP, cur, 0),
        scratch_shapes=[pltpu.VMEM((ATTN_BLOCK, wide), F32)],
        compiler_params=_params("parallel", "arbitrary"),
    )(qkv, qkv, qkv, qkv, do, do, lse, lse, delta, delta, tab)


def _attn_core_fwd(qkvs, name):
    rot, os_, lses = [], [], []
    for g, (qkv, d) in enumerate(zip(qkvs, ATTN_DILATIONS)):
        qkv = _rope_qk(qkv, rope_table(qkv.shape[0], d), f"{name}_rope{g}")
        o_g, lse_g = _attn_group_fwd(qkv, d, f"{name}_fwd{g}")
        rot.append(qkv)
        os_.append(_from_dilated(o_g, d))
        lses.append(_from_dilated(lse_g, d))
    o, lse = _attn_combine(os_, lses, name + "_combine")
    return o, (tuple(rot), o, lse)


@functools.partial(jax.custom_vjp, nondiff_argnums=(1,))
def attn_core(qkvs, name):
    return _attn_core_fwd(qkvs, name)[0]


def _attn_core_vjp_fwd(qkvs, name):
    return _attn_core_fwd(qkvs, name)


def _attn_core_vjp_bwd(name, res, do):
    rot, o, lse = res
    delta = _attn_delta(do, o, name + "_delta")
    out = []
    for g, (qkv, d) in enumerate(zip(rot, ATTN_DILATIONS)):
        out.append(_attn_group_bwd(qkv, _to_dilated(do, d), _to_dilated(lse, d), _to_dilated(delta, d),
                                   rope_table(o.shape[0], d), d, f"{name}_bwd{g}"))
    return (tuple(out),)


attn_core.defvjp(_attn_core_vjp_fwd, _attn_core_vjp_bwd)


def attention_mixer_p(x, h, lin, j, carries, tag):
    qkvs = tuple(lin(_to_dilated(h, d), ('attn_w_qkv', j), BF16, f"{tag}_qkv{g}",
                     cols=slice(g * QKV_GROUP, (g + 1) * QKV_GROUP), carry=carries[g])
                 for g, d in enumerate(ATTN_DILATIONS))
    return lin(attn_core(qkvs, tag), ('attn_w_o', j), F32, tag + "_o", res=x)


SSM_CONV_TAPS = 4
SSM_COL_BLOCK = 2048
SSM_PAIRS = SSM_HEADS // 2
SSM_DT_BLOCK = (SSM_D_INNER + SSM_CONV_DIM) // 128


def _ssm_conv_fwd(zx, conv_w, conv_b, name):
    s = zx.shape[0]
    tm = _pick(s, (256, 128))
    per = tm // SUBLANES
    ncb = SSM_CONV_DIM // SSM_COL_BLOCK

    def body(w_ref, b_ref, x_ref, halo_ref, o_ref):
        keep = pl.program_id(1) > 0
        for c0 in range(0, SSM_COL_BLOCK, FFN_COLS):
            cols = slice(c0, c0 + FFN_COLS)
            halo = jnp.where(keep, halo_ref[:, cols], 0.0)
            taps = _conv_taps(x_ref[:, cols], halo, SSM_CONV_TAPS)
            pre = sum(w_ref[k:k + 1, cols] * taps[k] for k in range(SSM_CONV_TAPS)) + b_ref[:, cols]
            o_ref[:, cols] = pre * _sigmoid(pre)

    return pl.pallas_call(
        body, name=name, out_shape=jax.ShapeDtypeStruct((s, SSM_CONV_DIM), F32), grid=(ncb, s // tm),
        in_specs=[pl.BlockSpec((SUBLANES, SSM_COL_BLOCK), lambda j, i: (0, j)),
                  pl.BlockSpec((1, SSM_COL_BLOCK), lambda j, i: (0, j)),
                  pl.BlockSpec((tm, SSM_COL_BLOCK), lambda j, i: (i, j + 1)),
                  pl.BlockSpec((SUBLANES, SSM_COL_BLOCK), lambda j, i: (jnp.maximum(i * per - 1, 0), j + 1))],
        out_specs=pl.BlockSpec((tm, SSM_COL_BLOCK), lambda j, i: (i, j)),
        compiler_params=_params("parallel", "parallel"),
    )(_pad_taps(conv_w), conv_b.reshape(1, SSM_CONV_DIM), zx, zx)


def _ssm_conv_bwd(zx, conv_w, conv_b, dact, name):
    s = zx.shape[0]
    tm = _pick(s, (256, 128))
    per = tm // SUBLANES
    nt = s // tm
    ncb = SSM_CONV_DIM // SSM_COL_BLOCK
    nk = SSM_CONV_TAPS

    def body(w_ref, b_ref, x_ref, halo_ref, da_ref, dx_ref, dw_ref, db_ref, carry_ref):
        step = pl.program_id(1)
        first = step == 0
        keep = step < nt - 1
        for c0 in range(0, SSM_COL_BLOCK, FFN_COLS):
            cols = slice(c0, c0 + FFN_COLS)
            halo = jnp.where(keep, halo_ref[:, cols], 0.0)
            taps = _conv_taps(x_ref[:, cols], halo, nk)
            pre = sum(w_ref[k:k + 1, cols] * taps[k] for k in range(nk)) + b_ref[:, cols]
            sig = _sigmoid(pre)
            dpre = da_ref[:, cols] * sig * (1.0 + pre * (1.0 - sig))
            nxt = jnp.where(first, 0.0, carry_ref[:, cols])
            dx = w_ref[nk - 1:nk, cols] * dpre
            for k in range(nk - 1):
                dx = dx + w_ref[k:k + 1, cols] * _shift_up(dpre, nxt, nk - 1 - k)
            dx_ref[:, cols] = dx.astype(dx_ref.dtype)
            carry_ref[:, cols] = dpre[0:SUBLANES, :]
            dwp = jnp.concatenate([jnp.sum(dpre * taps[k], axis=0, keepdims=True) for k in range(nk)]
                                  + [jnp.zeros((SUBLANES - nk, FFN_COLS), F32)], axis=0)
            dbp = jnp.sum(dpre, axis=0, keepdims=True)

            @pl.when(first)
            def _():
                dw_ref[:, cols] = dwp
                db_ref[:, cols] = dbp

            @pl.when(jnp.logical_not(first))
            def _():
                dw_ref[:, cols] += dwp
                db_ref[:, cols] += dbp

    rev = lambda i: nt - 1 - i
    dx, dw, db = pl.pallas_call(
        body, name=name,
        out_shape=[jax.ShapeDtypeStruct((s, SSM_CONV_DIM), BF16), jax.ShapeDtypeStruct((SUBLANES, SSM_CONV_DIM), F32),
                   jax.ShapeDtypeStruct((1, SSM_CONV_DIM), F32)],
        grid=(ncb, nt),
        in_specs=[pl.BlockSpec((SUBLANES, SSM_COL_BLOCK), lambda j, i: (0, j)),
                  pl.BlockSpec((1, SSM_COL_BLOCK), lambda j, i: (0, j)),
                  pl.BlockSpec((tm, SSM_COL_BLOCK), lambda j, i: (rev(i), j + 1)),
                  pl.BlockSpec((SUBLANES, SSM_COL_BLOCK), lambda j, i: (jnp.maximum(rev(i) * per - 1, 0), j + 1)),
                  pl.BlockSpec((tm, SSM_COL_BLOCK), lambda j, i: (rev(i), j))],
        out_specs=[pl.BlockSpec((tm, SSM_COL_BLOCK), lambda j, i: (rev(i), j)),
                   pl.BlockSpec((SUBLANES, SSM_COL_BLOCK), lambda j, i: (0, j)),
                   pl.BlockSpec((1, SSM_COL_BLOCK), lambda j, i: (0, j))],
        scratch_shapes=[pltpu.VMEM((SUBLANES, SSM_COL_BLOCK), F32)],
        compiler_params=_params("parallel", "arbitrary"),
    )(_pad_taps(conv_w), conv_b.reshape(1, SSM_CONV_DIM), zx, zx, dact)
    return dx, dw[:nk], db.reshape(SSM_CONV_DIM)


def _ssd_chunk(xs, bms, cms, dt_raw, dtb, alog, dsk, states):
    q = SSM_CHUNK
    lane = lax.broadcasted_iota(jnp.int32, (1, 128), 1)
    row = lax.broadcasted_iota(jnp.int32, (q, 1), 0)
    ii = lax.broadcasted_iota(jnp.int32, (q, q), 0)
    jj = lax.broadcasted_iota(jnp.int32, (q, q), 1)
    tril = ii >= jj
    left = lane < SSM_HEAD_DIM
    last_row = (row == q - 1).astype(F32)

    def lanes_of(mat, h):
        pick = (lane == h).astype(F32)
        return jnp.broadcast_to(jnp.sum(mat * pick, axis=1, keepdims=True), mat.shape)

    def rows_of(mat_t, h):
        pick = (row == h).astype(F32)
        return jnp.broadcast_to(jnp.sum(mat_t * pick, axis=0, keepdims=True), mat_t.shape)

    v = dt_raw + dtb
    dt = jnp.maximum(v, 0.0) + jnp.log(1.0 + jnp.exp(-jnp.abs(v)))
    adt = dt * (-jnp.exp(alog))
    acs = jnp.dot(tril.astype(F32), adt, precision=lax.Precision.HIGHEST, preferred_element_type=F32)
    acs_t = acs.T
    ys, new_states = [], []
    for pr in range(SSM_PAIRS):
        g = pr // 2
        if pr % 2 == 0:
            cb = _dot_nt(cms[g].astype(BF16), bms[g].astype(BF16))
        cols = [lanes_of(acs, 2 * pr + e) for e in range(2)]
        dts = [lanes_of(dt, 2 * pr + e) for e in range(2)]
        rws = [rows_of(acs_t, 2 * pr + e) for e in range(2)]
        lasts = [jnp.sum(c * last_row, axis=0, keepdims=True) for c in cols]
        xdt = xs[pr] * jnp.where(left, dts[0], dts[1])
        halves = [jnp.where(left, xdt, 0.0).astype(BF16), jnp.where(left, 0.0, xdt).astype(BF16)]
        y_diag, s_new = 0.0, 0.0
        for e in range(2):
            lmat = jnp.where(tril, jnp.exp(jnp.minimum(cols[e] - rws[e], 0.0)), 0.0)
            y_diag = y_diag + _dot((cb * lmat).astype(BF16), halves[e])
            decay = jnp.exp(lasts[e] - cols[e])
            s_new = s_new + _dot_tn((bms[g] * decay).astype(BF16), halves[e])
        y_off = _dot(cms[g].astype(BF16), states[pr].astype(BF16)) * jnp.where(left, jnp.exp(cols[0]), jnp.exp(cols[1]))
        skip = jnp.where(left, lanes_of(dsk, 2 * pr), lanes_of(dsk, 2 * pr + 1))
        ys.append(y_diag + y_off + xs[pr] * skip)
        new_states.append(states[pr] * jnp.where(left, jnp.exp(lasts[0]), jnp.exp(lasts[1])) + s_new)
    return tuple(ys), tuple(new_states)


def _ssm_vec(v):
    return jnp.pad(v.reshape(1, -1), ((0, 0), (0, 128 - v.shape[0])))


def _ssd_scan_fwd(act, zx, dtb, alog, dsk, name):
    s = act.shape[0]
    nc = s // SSM_CHUNK
    ng = SSM_GROUPS

    def body(act_ref, dt_ref, dtb_ref, alog_ref, dsk_ref, y_ref, st_out_ref, st_ref):
        @pl.when(pl.program_id(0) == 0)
        def _():
            st_ref[...] = jnp.zeros_like(st_ref)

        tile = lambda k: act_ref[:, k * 128:(k + 1) * 128]
        xs = [tile(k) for k in range(SSM_PAIRS)]
        bms = [tile(SSM_PAIRS + k) for k in range(ng)]
        cms = [tile(SSM_PAIRS + ng + k) for k in range(ng)]
        states = [st_ref[k] for k in range(SSM_PAIRS)]
        st_out_ref[0] = st_ref[...]
        ys, new_states = _ssd_chunk(xs, bms, cms, dt_ref[...], dtb_ref[...], alog_ref[...], dsk_ref[...], states)
        for k in range(SSM_PAIRS):
            y_ref[:, k * 128:(k + 1) * 128] = ys[k]
            st_ref[k] = new_states[k]

    vec = pl.BlockSpec((1, 128), lambda c: (0, 0))
    return pl.pallas_call(
        body, name=name,
        out_shape=[jax.ShapeDtypeStruct((s, SSM_D_INNER), F32),
                   jax.ShapeDtypeStruct((nc, SSM_PAIRS, SSM_STATE, 128), F32)],
        grid=(nc,),
        in_specs=[pl.BlockSpec((SSM_CHUNK, SSM_CONV_DIM), lambda c: (c, 0)),
                  pl.BlockSpec((SSM_CHUNK, 128), lambda c: (c, SSM_DT_BLOCK)), vec, vec, vec],
        out_specs=[pl.BlockSpec((SSM_CHUNK, SSM_D_INNER), lambda c: (c, 0)),
                   pl.BlockSpec((1, SSM_PAIRS, SSM_STATE, 128), lambda c: (c, 0, 0, 0))],
        scratch_shapes=[pltpu.VMEM((SSM_PAIRS, SSM_STATE, 128), F32)],
        compiler_params=_params("arbitrary"),
    )(act, zx, _ssm_vec(dtb), _ssm_vec(alog), _ssm_vec(dsk))


def _ssd_scan_bwd(act, zx, dtb, alog, dsk, st_in, dy, name):
    s = act.shape[0]
    nc = s // SSM_CHUNK
    ng = SSM_GROUPS

    def body(act_ref, dt_ref, dtb_ref, alog_ref, dsk_ref, st_ref, dy_ref, dact_ref, ddt_ref, dpar_ref, dst_ref):
        first = pl.program_id(0) == 0

        @pl.when(first)
        def _():
            dst_ref[...] = jnp.zeros_like(dst_ref)

        tile = lambda k: act_ref[:, k * 128:(k + 1) * 128]
        xs = [tile(k) for k in range(SSM_PAIRS)]
        bms = [tile(SSM_PAIRS + k) for k in range(ng)]
        cms = [tile(SSM_PAIRS + ng + k) for k in range(ng)]
        states = [st_ref[0, k] for k in range(SSM_PAIRS)]
        _, pullback = jax.vjp(_ssd_chunk, xs, bms, cms, dt_ref[...], dtb_ref[...], alog_ref[...], dsk_ref[...],
                              states)
        dys = tuple(dy_ref[:, k * 128:(k + 1) * 128] for k in range(SSM_PAIRS))
        dsts = tuple(dst_ref[k] for k in range(SSM_PAIRS))
        dxs, dbms, dcms, ddt, ddtb, dalog, ddsk, dstates = pullback((dys, dsts))
        for k, t in enumerate(list(dxs) + list(dbms) + list(dcms)):
            dact_ref[:, k * 128:(k + 1) * 128] = t
        ddt_ref[...] = ddt.astype(ddt_ref.dtype)
        for k in range(SSM_PAIRS):
            dst_ref[k] = dstates[k]
        dpar = jnp.concatenate([ddtb, dalog, ddsk, jnp.zeros((SUBLANES - 3, 128), F32)], axis=0)
        _accumulate(dpar_ref, dpar, first)

    rev = lambda c: nc - 1 - c
    vec = pl.BlockSpec((1, 128), lambda c: (0, 0))
    dact, ddt, dpar = pl.pallas_call(
        body, name=name,
        out_shape=[jax.ShapeDtypeStruct((s, SSM_CONV_DIM), F32), jax.ShapeDtypeStruct((s, 128), BF16),
                   jax.ShapeDtypeStruct((SUBLANES, 128), F32)],
        grid=(nc,),
        in_specs=[pl.BlockSpec((SSM_CHUNK, SSM_CONV_DIM), lambda c: (rev(c), 0)),
                  pl.BlockSpec((SSM_CHUNK, 128), lambda c: (rev(c), SSM_DT_BLOCK)), vec, vec, vec,
                  pl.BlockSpec((1, SSM_PAIRS, SSM_STATE, 128), lambda c: (rev(c), 0, 0, 0)),
                  pl.BlockSpec((SSM_CHUNK, SSM_D_INNER), lambda c: (rev(c), 0))],
        out_specs=[pl.BlockSpec((SSM_CHUNK, SSM_CONV_DIM), lambda c: (rev(c), 0)),
                   pl.BlockSpec((SSM_CHUNK, 128), lambda c: (rev(c), 0)),
                   pl.BlockSpec((SUBLANES, 128), lambda c: (0, 0))],
        scratch_shapes=[pltpu.VMEM((SSM_PAIRS, SSM_STATE, 128), F32)],
        compiler_params=_params("arbitrary"),
    )(act, zx, _ssm_vec(dtb), _ssm_vec(alog), _ssm_vec(dsk), st_in, dy)
    return dact, ddt, dpar[0, :SSM_HEADS], dpar[1, :SSM_HEADS], dpar[2, :SSM_HEADS]


SSM_NORM_GROUP = SSM_D_INNER // SSM_GROUPS


def _gated_group(y, z, w):
    g = y * (z * _sigmoid(z))
    return g * lax.rsqrt(jnp.mean(g * g, axis=-1, keepdims=True) + NORM_EPS) * w


def _gated_norm_fwd(y, zx, w, name):
    s = y.shape[0]
    tm = _pick(s, (256, 128))

    def body(y_ref, z_ref, w_ref, o_ref):
        for c0 in range(0, SSM_D_INNER, SSM_NORM_GROUP):
            cols = slice(c0, c0 + SSM_NORM_GROUP)
            o_ref[:, cols] = _gated_group(y_ref[:, cols], z_ref[:, cols], w_ref[:, cols]).astype(o_ref.dtype)

    blk = pl.BlockSpec((tm, SSM_D_INNER), lambda i: (i, 0))
    return pl.pallas_call(
        body, name=name, out_shape=jax.ShapeDtypeStruct((s, SSM_D_INNER), BF16), grid=(s // tm,),
        in_specs=[blk, blk, pl.BlockSpec((1, SSM_D_INNER), lambda i: (0, 0))], out_specs=blk,
        compiler_params=_params("parallel"),
    )(y, zx, w.reshape(1, SSM_D_INNER))


def _gated_norm_bwd(y, zx, w, dout, name):
    s = y.shape[0]
    tm = _pick(s, (256, 128))

    def body(y_ref, z_ref, w_ref, do_ref, dy_ref, dz_ref, dw_ref):
        first = pl.program_id(0) == 0
        for c0 in range(0, SSM_D_INNER, SSM_NORM_GROUP):
            cols = slice(c0, c0 + SSM_NORM_GROUP)
            _, pullback = jax.vjp(_gated_group, y_ref[:, cols], z_ref[:, cols], w_ref[:, cols])
            dyv, dzv, dwv = pullback(do_ref[:, cols].astype(F32))
            dy_ref[:, cols] = dyv
            dz_ref[:, cols] = dzv.astype(dz_ref.dtype)

            @pl.when(first)
            def _():
                dw_ref[:, cols] = dwv

            @pl.when(jnp.logical_not(first))
            def _():
                dw_ref[:, cols] += dwv

    blk = pl.BlockSpec((tm, SSM_D_INNER), lambda i: (i, 0))
    vec = pl.BlockSpec((1, SSM_D_INNER), lambda i: (0, 0))
    dy, dz, dw = pl.pallas_call(
        body, name=name,
        out_shape=[jax.ShapeDtypeStruct((s, SSM_D_INNER), F32), jax.ShapeDtypeStruct((s, SSM_D_INNER), BF16),
                   jax.ShapeDtypeStruct((1, SSM_D_INNER), F32)],
        grid=(s // tm,), in_specs=[blk, blk, vec, blk], out_specs=[blk, blk, vec],
        compiler_params=_params("arbitrary"),
    )(y, zx, w.reshape(1, SSM_D_INNER), dout)
    return dy, dz, dw.reshape(SSM_D_INNER)


def _ssm_core_fwd(zx, conv_w, conv_b, dtb, alog, dsk, norm_w, name):
    act = _ssm_conv_fwd(zx, conv_w, conv_b, name + "_conv_fwd")
    y, st_in = _ssd_scan_fwd(act, zx, dtb, alog, dsk, name + "_scan_fwd")
    out = _gated_norm_fwd(y, zx, norm_w, name + "_gate_fwd")
    return out, (zx, conv_w, conv_b, dtb, alog, dsk, norm_w, act, y, st_in)


@functools.partial(jax.custom_vjp, nondiff_argnums=(7,))
def ssm_core(zx, conv_w, conv_b, dtb, alog, dsk, norm_w, name):
    return _ssm_core_fwd(zx, conv_w, conv_b, dtb, alog, dsk, norm_w, name)[0]


def _ssm_core_vjp_fwd(zx, conv_w, conv_b, dtb, alog, dsk, norm_w, name):
    return _ssm_core_fwd(zx, conv_w, conv_b, dtb, alog, dsk, norm_w, name)


def _ssm_core_vjp_bwd(name, res, dout):
    zx, conv_w, conv_b, dtb, alog, dsk, norm_w, act, y, st_in = res
    dy, dz, dnorm_w = _gated_norm_bwd(y, zx, norm_w, dout, name + "_gate_bwd")
    dact, ddt, ddtb, dalog, ddsk = _ssd_scan_bwd(act, zx, dtb, alog, dsk, st_in, dy, name + "_scan_bwd")
    dxbc, dconv_w, dconv_b = _ssm_conv_bwd(zx, conv_w, conv_b, dact, name + "_conv_bwd")
    dzx = jnp.concatenate([dz, dxbc, ddt], axis=1)
    return dzx, dconv_w, dconv_b, ddtb, dalog, ddsk, dnorm_w


ssm_core.defvjp(_ssm_core_vjp_fwd, _ssm_core_vjp_bwd)


def ssd_mixer_p(x, h, lin, j, carries, conv_w, conv_b, dtb, alog, dsk, norm_w, tag):
    zx = lin(h, ('ssm_w_in', j), F32, tag + "_in", carry=carries[0])
    return lin(ssm_core(zx, conv_w, conv_b, dtb, alog, dsk, norm_w, tag), ('ssm_w_out', j), F32, tag + "_out",
               carry=carries[1], res=x)


def _full_weight(n, parts):
    full = _join8(parts, SHARD_AXIS[n] - 1)
    if n == 'ssm_w_in':
        full = jnp.pad(full, ((0, 0), (0, SSM_IN_PAD - SSM_IN_WIDTH)))
    return full


def trunk(w, x):
    ready = {('attn_w_qkv', 0): _full_weight('attn_w_qkv', gather_x(w['attn_w_qkv'][0], "gather_first"))}

    def lin(a, key, out_dtype, name, cols=None, carry=None, res=None):
        wt = ready[key] if cols is None else ready[key][:, cols]
        if carry is None:
            return linear(a, wt, res, out_dtype, name)
        y, parts = linear_x(a, wt, res, w[carry[0]][carry[1]], out_dtype, name)
        ready[carry] = _full_weight(carry[0], parts)
        return y

    for i in range(DEPTH):
        x, h = norm(x, w['mix_norm_w'][i], f"mixnorm{i}")
        j = i // 2
        ffn_next = [('ffn_w_up', i), ('ffn_w_down', i)]
        if i % 2 == 0:
            x = attention_mixer_p(x, h, lin, j, [('attn_w_o', j)] + ffn_next, f"attn{j}")
        else:
            x = ssd_mixer_p(x, h, lin, j, ffn_next, w['ssm_conv_w'][j], w['ssm_conv_b'][j], w['ssm_dt_bias'][j],
                            w['ssm_a_log'][j], w['ssm_d'][j], w['ssm_norm_w'][j], f"ssm{j}")
        if i + 1 == DEPTH:
            mixer_next = [None, None]
        elif i % 2 == 0:
            mixer_next = [('ssm_w_in', j), ('ssm_w_out', j)]
        else:
            mixer_next = [('attn_w_qkv', j + 1), None]
        x, h = norm(x, w['ffn_norm_w'][i], f"ffnnorm{i}")
        u0 = lin(h, ('ffn_w_up', i), F32, f"ffn{i}_up", carry=mixer_next[0])
        a = ffn_mid(u0, w['ffn_conv_w'][i], w['ffn_conv_b'][i], f"ffn{i}_mid")
        x = lin(a, ('ffn_w_down', i), F32, f"ffn{i}_down", carry=mixer_next[1], res=x)
    return x


def local_step(w, x, target):
    final_w = w['final_norm_w']
    trunk_w = {n: a for n, a in w.items() if n != 'final_norm_w'}
    xf, pullback = jax.vjp(trunk, trunk_w, x)
    loss, dxf, dfinal = loss_head(xf, final_w, target, "loss_head")
    gw, gx = pullback(dxf)
    gw['final_norm_w'] = dfinal
    return loss, gw, gx


def _adam_math(w, g, m, v):
    m = ADAM_B1 * m + (1.0 - ADAM_B1) * g
    v = ADAM_B2 * v + (1.0 - ADAM_B2) * (g * g)
    m_hat = m / (1.0 - ADAM_B1 ** ADAM_STEP)
    v_hat = v / (1.0 - ADAM_B2 ** ADAM_STEP)
    delta = -ADAM_LR * (m_hat / (jnp.sqrt(v_hat) + ADAM_EPS) + ADAM_WD * w)
    return delta, m, v


def _adamw_rows(g, w, m, v, name):
    r, c = w.shape
    tr = _pick(r, (256, 128, 64, 32, 16, 8))

    def body(g_ref, w_ref, m_ref, v_ref, d_out, m_out, v_out):
        delta, mm, vv = _adam_math(w_ref[...], g_ref[...], m_ref[...], v_ref[...])
        d_out[...] = delta
        m_out[...] = mm
        v_out[...] = vv

    blk = pl.BlockSpec((tr, c), lambda i: (i, 0))
    return pl.pallas_call(
        body, name=name, out_shape=[jax.ShapeDtypeStruct((r, c), F32)] * 3, grid=(r // tr,),
        in_specs=[blk] * 4, out_specs=[blk] * 3, compiler_params=_params("parallel"),
    )(g, w, m, v)


def _sum8(pieces, name):
    _, r, c = pieces.shape

    def body(p_ref, o_ref):
        g = p_ref[0]
        for j in range(1, N_DEV):
            g = g + p_ref[j]
        o_ref[...] = g

    return pl.pallas_call(
        body, name=name, out_shape=jax.ShapeDtypeStruct((r, c), F32),
        in_specs=[pl.BlockSpec(memory_space=pltpu.VMEM)], out_specs=pl.BlockSpec(memory_space=pltpu.VMEM),
    )(pieces)


def _adamw_plain(g, w, m, v, name):
    def body(g_ref, w_ref, m_ref, v_ref, d_out, m_out, v_out):
        delta, mm, vv = _adam_math(w_ref[...], g_ref[...], m_ref[...], v_ref[...])
        d_out[...] = delta
        m_out[...] = mm
        v_out[...] = vv

    vm = pl.BlockSpec(memory_space=pltpu.VMEM)
    return pl.pallas_call(
        body, name=name, out_shape=[jax.ShapeDtypeStruct(g.shape, F32)] * 3,
        in_specs=[vm] * 4, out_specs=[vm] * 3,
    )(g, w, m, v)


def _join8(parts, axis):
    t = jnp.moveaxis(parts, 0, axis)
    shp = t.shape
    return t.reshape(shp[:axis] + (shp[axis] * shp[axis + 1],) + shp[axis + 2:])


def _pack(arrs, lead, mult):
    flat = jnp.concatenate([a.reshape(a.shape[:lead] + (-1,)) for a in arrs], axis=-1)
    return _pad_rows(flat, mult)


def _unpack(buf, shapes, lead):
    flat = buf.reshape(buf.shape[:lead] + (-1,))
    out, off = [], 0
    for shp in shapes:
        n = math.prod(shp)
        out.append(flat[..., off:off + n].reshape(flat.shape[:lead] + tuple(shp)))
        off += n
    return out


def _own_shard(full, axis):
    size = full.shape[axis] // N_DEV
    return lax.dynamic_slice_in_dim(full, _my_index() * size, size, axis)


def kernel(x, mix_norm_w, attn_w_qkv, attn_w_o, ssm_w_in, ssm_conv_w, ssm_conv_b, ssm_dt_bias, ssm_a_log, ssm_d, ssm_norm_w, ssm_w_out, ffn_norm_w, ffn_w_up, ffn_conv_w, ffn_conv_b, ffn_w_down, final_norm_w, loss_target, m_mix_norm_w, m_attn_w_qkv, m_attn_w_o, m_ssm_w_in, m_ssm_conv_w, m_ssm_conv_b, m_ssm_dt_bias, m_ssm_a_log, m_ssm_d, m_ssm_norm_w, m_ssm_w_out, m_ffn_norm_w, m_ffn_w_up, m_ffn_conv_w, m_ffn_conv_b, m_ffn_w_down, m_final_norm_w, v_mix_norm_w, v_attn_w_qkv, v_attn_w_o, v_ssm_w_in, v_ssm_conv_w, v_ssm_conv_b, v_ssm_dt_bias, v_ssm_a_log, v_ssm_d, v_ssm_norm_w, v_ssm_w_out, v_ffn_norm_w, v_ffn_w_up, v_ffn_conv_w, v_ffn_conv_b, v_ffn_w_down, v_final_norm_w):
    args = dict(locals())
    w_sh = {n: args[n] for n in WEIGHT_NAMES}
    m_sh = {n: args["m_" + n] for n in WEIGHT_NAMES}
    v_sh = {n: args["v_" + n] for n in WEIGHT_NAMES}

    small_shapes = [w_sh[n].shape for n in SMALL_SHARDED]
    small = _exchange(_pack([w_sh[n] for n in SMALL_SHARDED], 0, 8), True, "gather_small")
    full = {n: w_sh[n] for n in SMALL if SHARD_AXIS[n] is None}
    for n, parts in zip(SMALL_SHARDED, _unpack(small, small_shapes, 1)):
        full[n] = _join8(parts, SHARD_AXIS[n])
    for n in BIG:
        full[n] = [w_sh[n][j] for j in range(w_sh[n].shape[0])]

    loss, gw, gx = local_step(full, x[0], loss_target[0])
    loss = lax.psum(loss, ("x", "y", "c"))
    for n in BIG:
        gw[n] = jnp.stack(gw[n])

    grads, deltas, new_m, new_v = {}, {}, {}, {}
    for n in BIG:
        shp = w_sh[n].shape
        two_d = (shp[0] * shp[1], shp[2])
        outs = _adamw_rows(*[t.reshape(two_d) for t in (gw[n], w_sh[n], m_sh[n], v_sh[n])], "adamw_" + n)
        grads[n] = gw[n]
        deltas[n], new_m[n], new_v[n] = [o.reshape(shp) for o in outs]

    small_full_shapes = [gw[n].shape for n in SMALL]
    gsmall = _exchange(_pack([gw[n] for n in SMALL], 0, 8), True, "gather_small_grads")
    gsmall = _unpack(_sum8(gsmall, "sum_small_grads"), small_full_shapes, 0)
    for n, g in zip(SMALL, gsmall):
        grads[n] = g if SHARD_AXIS[n] is None else _own_shard(g, SHARD_AXIS[n])
    shapes = [w_sh[n].shape for n in SMALL]
    outs = _adamw_plain(*[_pack([d[n] for n in SMALL], 0, 8) for d in (grads, w_sh, m_sh, v_sh)], "adamw_small")
    for d, buf in zip((deltas, new_m, new_v), outs):
        for n, a in zip(SMALL, _unpack(buf, shapes, 0)):
            d[n] = a

    return (loss, gx[None], *[grads[n] for n in WEIGHT_NAMES], *[deltas[n] for n in WEIGHT_NAMES],
            *[new_m[n] for n in WEIGHT_NAMES], *[new_v[n] for n in WEIGHT_NAMES])
```

```python
import functools
import math

import jax
import jax.numpy as jnp
from jax import lax
from jax.experimental import pallas as pl
from jax.experimental.pallas import tpu as pltpu

F32 = jnp.float32
BF16 = jnp.bfloat16
N_DEV = 8
MESH_ID = pl.DeviceIdType.MESH

D_MODEL = 1024
DEPTH = 4
ATTN_HEADS = 8
ATTN_HEAD_DIM = 128
ATTN_DILATIONS = (1, 4, 16)
ATTN_STEPS = (128, 128, 128)
N_ATTN_GROUPS = 3
ATTN_BLOCK = 128
ROPE_THETA = 500000.0
ROPE_DIM = 32
ATTN_OUT_WIDTH = 1024
SSM_D_INNER = 2048
SSM_HEAD_DIM = 64
SSM_HEADS = 32
SSM_STATE = 128
SSM_GROUPS = 8
SSM_CHUNK = 128
SSM_CONV_DIM = 4096
SSM_IN_WIDTH = 6176
SSM_IN_PAD = 6272
D_FF = 2816
NORM_EPS = 1e-5
ADAM_LR = 0.001
ADAM_B1 = 0.9
ADAM_B2 = 0.999
ADAM_EPS = 1e-08
ADAM_WD = 0.01
ADAM_STEP = 10

WEIGHT_NAMES = ['mix_norm_w', 'attn_w_qkv', 'attn_w_o', 'ssm_w_in', 'ssm_conv_w', 'ssm_conv_b', 'ssm_dt_bias',
                'ssm_a_log', 'ssm_d', 'ssm_norm_w', 'ssm_w_out', 'ffn_norm_w', 'ffn_w_up', 'ffn_conv_w',
                'ffn_conv_b', 'ffn_w_down', 'final_norm_w']
SHARD_AXIS = {'mix_norm_w': None, 'attn_w_qkv': 2, 'attn_w_o': 1, 'ssm_w_in': 2, 'ssm_conv_w': 2, 'ssm_conv_b': 1,
              'ssm_dt_bias': None, 'ssm_a_log': None, 'ssm_d': None, 'ssm_norm_w': 1, 'ssm_w_out': 1,
              'ffn_norm_w': None, 'ffn_w_up': 2, 'ffn_conv_w': 2, 'ffn_conv_b': None, 'ffn_w_down': 1,
              'final_norm_w': None}
BIG = ['attn_w_qkv', 'attn_w_o', 'ssm_w_in', 'ssm_w_out', 'ffn_w_up', 'ffn_w_down']
SMALL = [n for n in WEIGHT_NAMES if n not in BIG]
SMALL_SHARDED = [n for n in SMALL if SHARD_AXIS[n] is not None]
LANES = 1024


def _my_index():
    return 4 * lax.axis_index("x") + 2 * lax.axis_index("y") + lax.axis_index("c")


def _peer(k):
    x, y, c = lax.axis_index("x"), lax.axis_index("y"), lax.axis_index("c")
    return (x ^ ((k >> 2) & 1), y ^ ((k >> 1) & 1), c ^ (k & 1))


def _exchange(src, gather, name):
    def body(src_ref, out_ref, send_sems, recv_sems, local_sem):
        start, wait = _exchange_copies(src_ref, out_ref, send_sems, recv_sems, local_sem, gather)
        start()
        wait()

    return pl.pallas_call(
        body, name=name,
        out_shape=_exchange_out(src, gather),
        in_specs=[pl.BlockSpec(memory_space=pl.ANY)],
        out_specs=pl.BlockSpec(memory_space=pl.ANY),
        scratch_shapes=list(EXCHANGE_SEMS),
    )(src)


EXCHANGE_SEMS = (pltpu.SemaphoreType.DMA((N_DEV - 1,)), pltpu.SemaphoreType.DMA((N_DEV - 1,)),
                 pltpu.SemaphoreType.DMA)


def _exchange_out(src, gather):
    return jax.ShapeDtypeStruct((N_DEV,) + (src.shape if gather else src.shape[1:]), src.dtype)


def _exchange_copies(src_ref, out_ref, send_sems, recv_sems, local_sem, gather):
    me = _my_index()

    def piece(j):
        return src_ref if gather else src_ref.at[j]

    def remote(k, slab):
        return pltpu.make_async_remote_copy(
            src_ref=piece(me ^ k), dst_ref=out_ref.at[slab], send_sem=send_sems.at[k - 1],
            recv_sem=recv_sems.at[k - 1], device_id=_peer(k), device_id_type=MESH_ID)

    mine = pltpu.make_async_copy(piece(me), out_ref.at[me], local_sem)
    arrivals = {k: remote(k, me ^ k) for k in range(1, N_DEV)}
    if not gather:
        sends = [remote(k, me) for k in range(1, N_DEV)]

        def start():
            mine.start()
            for cp in sends:
                cp.start()

        def wait():
            for cp in arrivals.values():
                cp.wait_recv()
            for cp in sends:
                cp.wait_send()
            mine.wait()

        return start, wait

    far = (2, 4, 6)
    sends = [remote(k, me) for k in (1,) + far]
    passed_on = [pltpu.make_async_remote_copy(
        src_ref=out_ref.at[me ^ k], dst_ref=out_ref.at[me ^ k], send_sem=send_sems.at[k],
        recv_sem=recv_sems.at[k], device_id=_peer(1), device_id_type=MESH_ID) for k in far]

    def start():
        mine.start()
        for cp in sends:
            cp.start()

    def wait():
        for k, cp in zip(far, passed_on):
            arrivals[k].wait_recv()
            cp.start()
        for k in (1, 3, 5, 7):
            arrivals[k].wait_recv()
        for cp in sends + passed_on:
            cp.wait_send()
        mine.wait()

    return start, wait


def _pad_rows(flat, mult):
    n = flat.shape[-1]
    rows = -(-n // (LANES * mult)) * mult
    pad = rows * LANES - n
    flat = jnp.pad(flat, [(0, 0)] * (flat.ndim - 1) + [(0, pad)])
    return flat.reshape(flat.shape[:-1] + (rows, LANES))


MATMUL_VMEM_BUDGET = 48 * 1024 * 1024
MATMUL_TM = (1024, 1408, 512, 256, 128)
MATMUL_TN = (3072, 2816, 2048, 1536, 1408, 1024, 896, 512, 384, 256, 128)
MATMUL_TK = (3072, 2816, 2048, 1408, 1024, 896)
MATMUL_MIN_TK = 896


def _pick(n, cands):
    for c in cands:
        if n % c == 0:
            return c
    return n


def _matmul_tiles(m, n, k, a_bytes, b_bytes, out_bytes, has_res):
    tm = _pick(m, MATMUL_TM)

    def need(tn, tk):
        blocks = 2 * (tm * tk * a_bytes + tk * tn * b_bytes + tm * tn * out_bytes + has_res * tm * tn * 4)
        temps = tm * tn * 4 * (1 + (tk < k)) + (a_bytes > 2) * tm * tk * 2 + (b_bytes > 2) * tk * tn * 2
        return blocks + temps

    for tn in [c for c in MATMUL_TN if n % c == 0] + [n]:
        for tk in [k] + [c for c in MATMUL_TK if k % c == 0 and MATMUL_MIN_TK <= c < k]:
            if need(tn, tk) <= MATMUL_VMEM_BUDGET:
                return tm, tn, tk
    raise ValueError(f"no matmul tiling fits VMEM for {(m, n, k)}")


def _matmul(a, b, ta, tb, out_dtype, name, rider=None, res=None):
    (m, k) = (a.shape[1], a.shape[0]) if ta else a.shape
    (k2, n) = (b.shape[1], b.shape[0]) if tb else b.shape
    assert k == k2, (a.shape, b.shape, ta, tb)
    tm, tn, tk = _matmul_tiles(m, n, k, a.dtype.itemsize, b.dtype.itemsize, jnp.dtype(out_dtype).itemsize,
                               res is not None)
    nk = k // tk
    dims = (((0 if ta else 1,), (1 if tb else 0,)), ((), ()))

    grid = (n // tn, m // tm, nk)

    n_in = 2 + (res is not None) + (rider is not None)

    def body(*refs):
        ins, rest = refs[:n_in], refs[n_in:]
        a_ref, b_ref = ins[:2]
        res_ref = ins[2] if res is not None else None
        o_ref, scratch = rest[0], rest[1:]
        if rider is not None:
            start, wait = _exchange_copies(ins[-1], rest[1], *scratch[-3:], rider[1])
            scratch = scratch[1:-3]
            at = [pl.program_id(ax) for ax in range(3)]
            pl.when(functools.reduce(jnp.logical_and, [p == 0 for p in at]))(start)
        part = lax.dot_general(a_ref[...].astype(BF16), b_ref[...].astype(BF16), dims,
                               preferred_element_type=F32)

        def finish(total):
            if res_ref is not None:
                total = total + res_ref[...]
            o_ref[...] = total.astype(o_ref.dtype)

        if nk == 1:
            finish(part)
        else:
            acc_ref, = scratch
            kk = pl.program_id(2)

            @pl.when(kk == 0)
            def _():
                acc_ref[...] = part

            @pl.when(kk > 0)
            def _():
                acc_ref[...] += part

            @pl.when(kk == nk - 1)
            def _():
                finish(acc_ref[...])
        if rider is not None:
            pl.when(functools.reduce(jnp.logical_and, [p == g - 1 for p, g in zip(at, grid)]))(wait)

    a_spec = (pl.BlockSpec((tk, tm), lambda j, i, kk: (kk, i)) if ta
              else pl.BlockSpec((tm, tk), lambda j, i, kk: (i, kk)))
    b_spec = (pl.BlockSpec((tn, tk), lambda j, i, kk: (j, kk)) if tb
              else pl.BlockSpec((tk, tn), lambda j, i, kk: (kk, j)))
    any_spec = pl.BlockSpec(memory_space=pl.ANY)
    tile_spec = pl.BlockSpec((tm, tn), lambda j, i, kk: (i, j))
    in_specs, operands = [a_spec, b_spec], [a, b]
    out_shape, out_specs = [jax.ShapeDtypeStruct((m, n), out_dtype)], [tile_spec]
    scratch = [] if nk == 1 else [pltpu.VMEM((tm, tn), F32)]
    sem = ("parallel", "parallel", "arbitrary")
    if res is not None:
        in_specs.append(tile_spec)
        operands.append(res)
    if rider is not None:
        in_specs.append(any_spec)
        operands.append(rider[0])
        out_shape.append(_exchange_out(*rider))
        out_specs.append(any_spec)
        scratch += list(EXCHANGE_SEMS)
        sem = ("arbitrary",) * 3
    outs = pl.pallas_call(
        body, name=name, out_shape=out_shape, grid=grid, in_specs=in_specs, out_specs=out_specs,
        scratch_shapes=scratch, compiler_params=_params(*sem),
    )(*operands)
    return outs[0] if rider is None else tuple(outs)


@functools.partial(jax.custom_vjp, nondiff_argnums=(3, 4))
def linear(a, w, res, out_dtype, name):
    return _matmul(a, w, False, False, out_dtype, name + "_fwd", res=res)


def _linear_fwd(a, w, res, out_dtype, name):
    return _matmul(a, w, False, False, out_dtype, name + "_fwd", res=res), (a, w, res is not None)


def _linear_bwd(out_dtype, name, saved, dy):
    a, w, has_res = saved
    da = _matmul(dy, w, False, True, a.dtype, name + "_da")
    dw = _matmul(a, dy, True, False, w.dtype, name + "_dw")
    return da, dw, (dy if has_res else None)


linear.defvjp(_linear_fwd, _linear_bwd)


def _sum_pieces(pieces, name):
    shape = pieces.shape[1:]
    c = shape[-1]
    r = math.prod(shape[:-1])
    tr = _pick(r, (512, 256, 128, 64, 32, 16, 8))

    def body(p_ref, o_ref):
        g = p_ref[0].astype(F32)
        for j in range(1, N_DEV):
            g = g + p_ref[j].astype(F32)
        o_ref[...] = g

    return pl.pallas_call(
        body, name=name, out_shape=jax.ShapeDtypeStruct((r, c), F32), grid=(r // tr,),
        in_specs=[pl.BlockSpec((N_DEV, tr, c), lambda i: (0, i, 0))],
        out_specs=pl.BlockSpec((tr, c), lambda i: (i, 0)),
        compiler_params=_params("parallel"),
    )(pieces.reshape(N_DEV, r, c)).reshape(shape)


@functools.partial(jax.custom_vjp, nondiff_argnums=(1,))
def gather_x(shard, name):
    return _exchange(shard.astype(BF16), True, name)


def _gather_x_fwd(shard, name):
    return _exchange(shard.astype(BF16), True, name), None


def _gather_x_bwd(name, _, dparts):
    return (_sum_pieces(_exchange(dparts, False, name + "_back"), name + "_sum"),)


gather_x.defvjp(_gather_x_fwd, _gather_x_bwd)


@functools.partial(jax.custom_vjp, nondiff_argnums=(4, 5))
def linear_x(a, w, res, shard, out_dtype, name):
    return _matmul(a, w, False, False, out_dtype, name + "_fwd", rider=(shard.astype(BF16), True), res=res)


def _linear_x_fwd(a, w, res, shard, out_dtype, name):
    out = _matmul(a, w, False, False, out_dtype, name + "_fwd", rider=(shard.astype(BF16), True), res=res)
    return out, (a, w, res is not None)


def _linear_x_bwd(out_dtype, name, saved, cts):
    a, w, has_res = saved
    dy, dparts = cts
    half = dparts.shape[1] // 2
    da, moved_a = _matmul(dy, w, False, True, a.dtype, name + "_da", rider=(dparts[:, :half], False))
    dw, moved_b = _matmul(a, dy, True, False, w.dtype, name + "_dw", rider=(dparts[:, half:], False))
    dshard = jnp.concatenate([_sum_pieces(moved_a, name + "_sum_a"), _sum_pieces(moved_b, name + "_sum_b")])
    return da, dw, (dy if has_res else None), dshard


linear_x.defvjp(_linear_x_fwd, _linear_x_bwd)


VMEM_LIMIT = 56 * 1024 * 1024
SUBLANES = 8


def _params(*sem):
    return pltpu.CompilerParams(dimension_semantics=sem, vmem_limit_bytes=VMEM_LIMIT)


def _sigmoid(x):
    return 0.5 * jnp.tanh(0.5 * x) + 0.5


def _rstd(xv):
    return lax.rsqrt(jnp.mean(xv * xv, axis=-1, keepdims=True) + NORM_EPS)


def _accumulate(ref, part, first):
    @pl.when(first)
    def _():
        ref[...] = part

    @pl.when(jnp.logical_not(first))
    def _():
        ref[...] += part


def _norm_fwd(x, w, name):
    s, d = x.shape
    tm = _pick(s, (512, 256, 128))

    def body(x_ref, w_ref, h_ref):
        xv = x_ref[...]
        h_ref[...] = (xv * _rstd(xv) * w_ref[...]).astype(h_ref.dtype)

    return pl.pallas_call(
        body, name=name, out_shape=jax.ShapeDtypeStruct((s, d), BF16), grid=(s // tm,),
        in_specs=[pl.BlockSpec((tm, d), lambda i: (i, 0)), pl.BlockSpec((1, d), lambda i: (0, 0))],
        out_specs=pl.BlockSpec((tm, d), lambda i: (i, 0)), compiler_params=_params("parallel"),
    )(x, w.reshape(1, d))


def _norm_bwd(x, w, dh, dskip, name):
    s, d = x.shape
    tm = _pick(s, (512, 256, 128))

    def body(x_ref, w_ref, dh_ref, ds_ref, dx_ref, dw_ref):
        xv = x_ref[...]
        r = _rstd(xv)
        y = xv * r
        dhv = dh_ref[...].astype(F32)
        dy = dhv * w_ref[...]
        dx_ref[...] = ds_ref[...] + r * (dy - y * jnp.mean(dy * y, axis=-1, keepdims=True))
        _accumulate(dw_ref, jnp.sum(dhv * y, axis=0, keepdims=True), pl.program_id(0) == 0)

    row = pl.BlockSpec((tm, d), lambda i: (i, 0))
    vec = pl.BlockSpec((1, d), lambda i: (0, 0))
    dx, dw = pl.pallas_call(
        body, name=name,
        out_shape=[jax.ShapeDtypeStruct((s, d), F32), jax.ShapeDtypeStruct((1, d), F32)], grid=(s // tm,),
        in_specs=[row, vec, row, row], out_specs=[row, vec], compiler_params=_params("arbitrary"),
    )(x, w.reshape(1, d), dh, dskip)
    return dx, dw.reshape(d)


@functools.partial(jax.custom_vjp, nondiff_argnums=(2,))
def norm(x, w, name):
    return x, _norm_fwd(x, w, name + "_fwd")


def _norm_vjp_fwd(x, w, name):
    return (x, _norm_fwd(x, w, name + "_fwd")), (x, w)


def _norm_vjp_bwd(name, saved, cts):
    x, w = saved
    dskip, dh = cts
    return _norm_bwd(x, w, dh, dskip, name + "_bwd")


norm.defvjp(_norm_vjp_fwd, _norm_vjp_bwd)


def loss_head(x, w, target, name):
    s, d = x.shape
    tm = _pick(s, (512, 256, 128))

    def body(x_ref, w_ref, t_ref, loss_ref, dx_ref, dw_ref):
        first = pl.program_id(0) == 0
        xv = x_ref[...]
        r = _rstd(xv)
        y = xv * r
        err = y * w_ref[...] - t_ref[...]
        part = 0.5 * jnp.sum(jnp.sum(err * err, axis=-1, keepdims=True), axis=0, keepdims=True) / d
        _accumulate(loss_ref, jnp.broadcast_to(part, loss_ref.shape), first)
        dout = err / d
        dy = dout * w_ref[...]
        dx_ref[...] = r * (dy - y * jnp.mean(dy * y, axis=-1, keepdims=True))
        _accumulate(dw_ref, jnp.sum(dout * y, axis=0, keepdims=True), first)

    row = pl.BlockSpec((tm, d), lambda i: (i, 0))
    vec = pl.BlockSpec((1, d), lambda i: (0, 0))
    loss, dx, dw = pl.pallas_call(
        body, name=name,
        out_shape=[jax.ShapeDtypeStruct((1, 128), F32), jax.ShapeDtypeStruct((s, d), F32),
                   jax.ShapeDtypeStruct((1, d), F32)],
        grid=(s // tm,), in_specs=[row, vec, row],
        out_specs=[pl.BlockSpec((1, 128), lambda i: (0, 0)), row, vec],
        compiler_params=_params("arbitrary"),
    )(x, w.reshape(1, d), target)
    return loss[0, 0], dx, dw.reshape(d)


FFN_ROWS = 256
FFN_COLS = 256


def _shift_down(cur, halo, k):
    out = pltpu.roll(cur, k, axis=0)
    row = lax.broadcasted_iota(jnp.int32, halo.shape, 0)
    top = out[0:SUBLANES]
    for j in range(k):
        top = jnp.where(row == j, halo[SUBLANES - k + j:SUBLANES - k + j + 1, :], top)
    return jnp.concatenate([top, out[SUBLANES:]], axis=0)


def _shift_up(cur, nxt, k):
    n = cur.shape[0]
    out = pltpu.roll(cur, n - k, axis=0)
    row = lax.broadcasted_iota(jnp.int32, nxt.shape, 0)
    bottom = out[n - SUBLANES:]
    for j in range(k):
        bottom = jnp.where(row == SUBLANES - k + j, nxt[j:j + 1, :], bottom)
    return jnp.concatenate([out[:n - SUBLANES], bottom], axis=0)


def _conv_taps(cur, halo, ntaps):
    return [_shift_down(cur, halo, ntaps - 1 - k) if k < ntaps - 1 else cur for k in range(ntaps)]


def _pad_taps(conv_w):
    return jnp.pad(conv_w, ((0, SUBLANES - conv_w.shape[0]), (0, 0)))


def _ffn_mid_fwd(u0, conv_w, conv_b, name):
    s, width = u0.shape
    half = width // 2
    tm = _pick(s, (FFN_ROWS, 128))
    per = tm // SUBLANES

    def body(w_ref, b_ref, u_ref, halo_ref, a_ref, pre_ref):
        keep = pl.program_id(0) > 0
        for c0 in range(0, half, FFN_COLS):
            vals = []
            for base in (c0, half + c0):
                cols = slice(base, base + FFN_COLS)
                halo = jnp.where(keep, halo_ref[:, cols], 0.0)
                taps = _conv_taps(u_ref[:, cols], halo, 3)
                vals.append(sum(w_ref[k:k + 1, cols] * taps[k] for k in range(3)) + b_ref[:, cols])
                pre_ref[:, cols] = vals[-1]
            gate, up = vals
            a_ref[:, c0:c0 + FFN_COLS] = (gate * _sigmoid(gate) * up).astype(a_ref.dtype)

    return pl.pallas_call(
        body, name=name,
        out_shape=[jax.ShapeDtypeStruct((s, half), BF16), jax.ShapeDtypeStruct((s, width), F32)], grid=(s // tm,),
        in_specs=[pl.BlockSpec((SUBLANES, width), lambda i: (0, 0)), pl.BlockSpec((1, width), lambda i: (0, 0)),
                  pl.BlockSpec((tm, width), lambda i: (i, 0)),
                  pl.BlockSpec((SUBLANES, width), lambda i: (jnp.maximum(i * per - 1, 0), 0))],
        out_specs=[pl.BlockSpec((tm, half), lambda i: (i, 0)), pl.BlockSpec((tm, width), lambda i: (i, 0))],
        compiler_params=_params("parallel"),
    )(_pad_taps(conv_w), conv_b.reshape(1, width), u0, u0)


def _ffn_mid_bwd(u0, pre, conv_w, da, name):
    s, width = u0.shape
    half = width // 2
    tm = _pick(s, (FFN_ROWS, 128))
    nt = s // tm

    def body(w_ref, u_ref, pre_ref, da_ref, du0_ref, dw_ref, db_ref, carry_ref):
        first = pl.program_id(0) == 0
        for c0 in range(0, half, FFN_COLS):
            dav = da_ref[:, c0:c0 + FFN_COLS].astype(F32)
            gate, up = pre_ref[:, c0:c0 + FFN_COLS], pre_ref[:, half + c0:half + c0 + FFN_COLS]
            sig = _sigmoid(gate)
            dus = [dav * up * sig * (1.0 + gate * (1.0 - sig)), dav * gate * sig]
            for base, du in zip((c0, half + c0), dus):
                cols = slice(base, base + FFN_COLS)
                nxt = jnp.where(first, 0.0, carry_ref[:, cols])
                ahead = [_shift_up(du, nxt, 2), _shift_up(du, nxt, 1), du]
                du0 = sum(w_ref[k:k + 1, cols] * ahead[k] for k in range(3))
                du0_ref[:, cols] = du0.astype(du0_ref.dtype)
                carry_ref[:, cols] = du[0:SUBLANES, :]
                x = u_ref[:, cols]
                dwp = jnp.concatenate([jnp.sum(ahead[k] * x, axis=0, keepdims=True) for k in range(3)]
                                      + [jnp.zeros((SUBLANES - 3, FFN_COLS), F32)], axis=0)
                dbp = jnp.sum(du, axis=0, keepdims=True)

                @pl.when(first)
                def _():
                    dw_ref[:, cols] = dwp
                    db_ref[:, cols] = dbp

                @pl.when(jnp.logical_not(first))
                def _():
                    dw_ref[:, cols] += dwp
                    db_ref[:, cols] += dbp

    rev = lambda i: nt - 1 - i
    du0, dw, db = pl.pallas_call(
        body, name=name,
        out_shape=[jax.ShapeDtypeStruct((s, width), BF16), jax.ShapeDtypeStruct((SUBLANES, width), F32),
                   jax.ShapeDtypeStruct((1, width), F32)],
        grid=(nt,),
        in_specs=[pl.BlockSpec((SUBLANES, width), lambda i: (0, 0)),
                  pl.BlockSpec((tm, width), lambda i: (rev(i), 0)), pl.BlockSpec((tm, width), lambda i: (rev(i), 0)),
                  pl.BlockSpec((tm, half), lambda i: (rev(i), 0))],
        out_specs=[pl.BlockSpec((tm, width), lambda i: (rev(i), 0)),
                   pl.BlockSpec((SUBLANES, width), lambda i: (0, 0)), pl.BlockSpec((1, width), lambda i: (0, 0))],
        scratch_shapes=[pltpu.VMEM((SUBLANES, width), F32)],
        compiler_params=_params("arbitrary"),
    )(_pad_taps(conv_w), u0, pre, da)
    return du0, dw[:3], db.reshape(width)


@functools.partial(jax.custom_vjp, nondiff_argnums=(3,))
def ffn_mid(u0, conv_w, conv_b, name):
    return _ffn_mid_fwd(u0, conv_w, conv_b, name + "_fwd")[0]


def _ffn_mid_vjp_fwd(u0, conv_w, conv_b, name):
    a, pre = _ffn_mid_fwd(u0, conv_w, conv_b, name + "_fwd")
    return a, (u0, pre, conv_w)


def _ffn_mid_vjp_bwd(name, res, da):
    u0, pre, conv_w = res
    return _ffn_mid_bwd(u0, pre, conv_w, da, name + "_bwd")


ffn_mid.defvjp(_ffn_mid_vjp_fwd, _ffn_mid_vjp_bwd)


NEG = -1e30
HEAD_SLICES = [slice(hh * ATTN_HEAD_DIM, (hh + 1) * ATTN_HEAD_DIM) for hh in range(ATTN_HEADS)]
ATTN_SCALE = ATTN_HEAD_DIM ** -0.5
QKV_GROUP = 3 * ATTN_OUT_WIDTH


def rope_table(seq, d):
    pos = (jnp.arange(seq // d, dtype=jnp.int32)[None, :] * d + jnp.arange(d, dtype=jnp.int32)[:, None])
    inv_freq = ROPE_THETA ** (-jnp.arange(0, ROPE_DIM, 2, dtype=F32) / ROPE_DIM)
    ang = pos.reshape(seq).astype(F32)[:, None] * inv_freq[None, :]
    cos, sin = jnp.cos(ang), jnp.sin(ang)
    half = ROPE_DIM // 2
    ones = jnp.ones((seq, ATTN_HEAD_DIM - ROPE_DIM), F32)
    zero = lambda n: jnp.zeros((seq, n), F32)
    return jnp.concatenate([cos, cos, ones, -sin, zero(ATTN_HEAD_DIM - half),
                            zero(half), sin, zero(ATTN_HEAD_DIM - ROPE_DIM)], axis=1)


def _rope(t, tab, sign):
    half = ROPE_DIM // 2
    return t * tab[:, 0:128] + sign * (pltpu.roll(t, ATTN_HEAD_DIM - half, axis=1) * tab[:, 128:256]
                                       + pltpu.roll(t, half, axis=1) * tab[:, 256:384])


def _to_dilated(a, d):
    s = a.shape[0]
    return a if d == 1 else a.reshape(s // d, d, -1).transpose(1, 0, 2).reshape(s, -1)


def _from_dilated(a, d):
    s = a.shape[0]
    return a if d == 1 else a.reshape(d, s // d, -1).transpose(1, 0, 2).reshape(s, -1)


def _rope_qk(qkv, tab, name):
    s = qkv.shape[0]
    tm = _pick(s, (512, 256, 128))

    def body(t_ref, x_ref, o_ref):
        for hs in HEAD_SLICES:
            o_ref[:, hs] = _rope(x_ref[:, hs].astype(F32), t_ref[...], 1.0).astype(o_ref.dtype)

    blk = pl.BlockSpec((tm, ATTN_OUT_WIDTH), lambda i, c: (i, c))
    return pl.pallas_call(
        body, name=name, out_shape=jax.ShapeDtypeStruct(qkv.shape, qkv.dtype), grid=(s // tm, 2),
        in_specs=[pl.BlockSpec((tm, 384), lambda i, c: (i, 0)), blk], out_specs=blk,
        input_output_aliases={1: 0}, compiler_params=_params("parallel", "parallel"),
    )(tab, qkv)


def _dot_nt(a, b):
    return lax.dot_general(a, b, (((1,), (1,)), ((), ())), preferred_element_type=F32)


def _dot_tn(a, b):
    return lax.dot_general(a, b, (((0,), (0,)), ((), ())), preferred_element_type=F32)


def _dot(a, b):
    return jnp.dot(a, b, preferred_element_type=F32)


def _window_mask(has_prev):
    ii = lax.broadcasted_iota(jnp.int32, (ATTN_BLOCK, 2 * ATTN_BLOCK), 0)
    jj = lax.broadcasted_iota(jnp.int32, (ATTN_BLOCK, 2 * ATTN_BLOCK), 1)
    in_window = jnp.logical_and(jj >= ii, jj <= ii + ATTN_BLOCK)
    return jnp.logical_and(in_window, jnp.logical_or(jj >= ATTN_BLOCK, has_prev))


def _both(prev_ref, cur_ref, hs):
    return jnp.concatenate([prev_ref[:, hs], cur_ref[:, hs]], axis=0)


def _attn_group_fwd(qkv, d, name):
    s = qkv.shape[0]
    nb = s // d // ATTN_BLOCK

    def body(q_ref, kc_ref, kp_ref, vc_ref, vp_ref, o_ref, lse_ref):
        mask = _window_mask(pl.program_id(1) > 0)
        lane = lax.broadcasted_iota(jnp.int32, (ATTN_BLOCK, 128), 1)
        lse_tile = jnp.zeros((ATTN_BLOCK, 128), F32)
        for hh, hs in enumerate(HEAD_SLICES):
            sc = jnp.where(mask, _dot_nt(q_ref[:, hs], _both(kp_ref, kc_ref, hs)) * ATTN_SCALE, NEG)
            m = jnp.max(sc, axis=1, keepdims=True)
            p = jnp.exp(sc - m)
            den = jnp.sum(p, axis=1, keepdims=True)
            o_ref[:, hs] = _dot(p.astype(BF16), _both(vp_ref, vc_ref, hs)) / den
            lse_tile = jnp.where(lane == hh, m + jnp.log(den), lse_tile)
        lse_ref[...] = lse_tile

    cur = lambda t: pl.BlockSpec((ATTN_BLOCK, ATTN_OUT_WIDTH), lambda r, n: (r * nb + n, t))
    prv = lambda t: pl.BlockSpec((ATTN_BLOCK, ATTN_OUT_WIDTH), lambda r, n: (r * nb + jnp.maximum(n - 1, 0), t))
    return pl.pallas_call(
        body, name=name,
        out_shape=[jax.ShapeDtypeStruct((s, ATTN_OUT_WIDTH), F32), jax.ShapeDtypeStruct((s, 128), F32)],
        grid=(d, nb), in_specs=[cur(0), cur(1), prv(1), cur(2), prv(2)],
        out_specs=[pl.BlockSpec((ATTN_BLOCK, ATTN_OUT_WIDTH), lambda r, n: (r * nb + n, 0)),
                   pl.BlockSpec((ATTN_BLOCK, 128), lambda r, n: (r * nb + n, 0))],
        compiler_params=_params("parallel", "parallel"),
    )(qkv, qkv, qkv, qkv, qkv)


def _attn_combine(os_, lses, name):
    s = os_[0].shape[0]
    tm = _pick(s, (256, 128))
    ng = len(os_)

    def body(*refs):
        o_refs, l_refs, (o_ref, lse_ref) = refs[:ng], refs[ng:2 * ng], refs[2 * ng:]
        lane = lax.broadcasted_iota(jnp.int32, (tm, 128), 1)
        lse_tile = jnp.zeros((tm, 128), F32)
        for hh, hs in enumerate(HEAD_SLICES):
            ls = [l_ref[:, hh:hh + 1] for l_ref in l_refs]
            m = functools.reduce(jnp.maximum, ls)
            ws = [jnp.exp(l - m) for l in ls]
            tot = functools.reduce(lambda a, b: a + b, ws)
            acc = functools.reduce(lambda a, b: a + b, [o_r[:, hs] * w for o_r, w in zip(o_refs, ws)])
            o_ref[:, hs] = (acc / tot).astype(o_ref.dtype)
            lse_tile = jnp.where(lane == hh, m + jnp.log(tot), lse_tile)
        lse_ref[...] = lse_tile

    wide = pl.BlockSpec((tm, ATTN_OUT_WIDTH), lambda i: (i, 0))
    thin = pl.BlockSpec((tm, 128), lambda i: (i, 0))
    return pl.pallas_call(
        body, name=name,
        out_shape=[jax.ShapeDtypeStruct((s, ATTN_OUT_WIDTH), BF16), jax.ShapeDtypeStruct((s, 128), F32)],
        grid=(s // tm,), in_specs=[wide] * ng + [thin] * ng, out_specs=[wide, thin],
        compiler_params=_params("parallel"),
    )(*os_, *lses)


def _attn_delta(do, o, name):
    s = do.shape[0]
    tm = _pick(s, (256, 128))

    def body(do_ref, o_ref, out_ref):
        lane = lax.broadcasted_iota(jnp.int32, (tm, 128), 1)
        tile = jnp.zeros((tm, 128), F32)
        for hh, hs in enumerate(HEAD_SLICES):
            prod = do_ref[:, hs].astype(F32) * o_ref[:, hs].astype(F32)
            tile = jnp.where(lane == hh, jnp.sum(prod, axis=1, keepdims=True), tile)
        out_ref[...] = tile

    wide = pl.BlockSpec((tm, ATTN_OUT_WIDTH), lambda i: (i, 0))
    return pl.pallas_call(
        body, name=name, out_shape=jax.ShapeDtypeStruct((s, 128), F32), grid=(s // tm,),
        in_specs=[wide, wide], out_specs=pl.BlockSpec((tm, 128), lambda i: (i, 0)),
        compiler_params=_params("parallel"),
    )(do, o)


def _attn_group_bwd(qkv, do, lse, delta, tab, d, name):
    s = qkv.shape[0]
    nb = s // d // ATTN_BLOCK
    wide = ATTN_OUT_WIDTH

    def body(q_ref, qn_ref, k_ref, v_ref, do_ref, don_ref, lse_ref, lsen_ref, dl_ref, dln_ref, tab_ref,
             out_ref, carry_ref):
        n = pl.program_id(1)

        @pl.when(n == 0)
        def _():
            carry_ref[...] = jnp.zeros_like(carry_ref)

        rows = lax.broadcasted_iota(jnp.int32, (2 * ATTN_BLOCK, ATTN_BLOCK), 0)
        keys = lax.broadcasted_iota(jnp.int32, (2 * ATTN_BLOCK, ATTN_BLOCK), 1)
        own = jnp.logical_and(rows < ATTN_BLOCK, keys <= rows)
        nxt = jnp.logical_and(rows >= ATTN_BLOCK, jnp.logical_and(keys >= rows - ATTN_BLOCK, n + 1 < nb))
        mask = jnp.logical_or(own, nxt)
        both = lambda a_ref, b_ref, cols: jnp.concatenate([a_ref[:, cols], b_ref[:, cols]], axis=0)
        for hh, hs in enumerate(HEAD_SLICES):
            one = slice(hh, hh + 1)
            q2, do2 = both(q_ref, qn_ref, hs), both(do_ref, don_ref, hs)
            kh, vh = k_ref[:, hs], v_ref[:, hs]
            sc = jnp.where(mask, _dot_nt(q2, kh) * ATTN_SCALE, NEG)
            p = jnp.exp(sc - both(lse_ref, lsen_ref, one))
            ds = (p * (_dot_nt(do2, vh) - both(dl_ref, dln_ref, one)) * ATTN_SCALE).astype(BF16)
            dq2 = _dot(ds, kh)
            dq = carry_ref[:, hs] + dq2[:ATTN_BLOCK]
            carry_ref[:, hs] = dq2[ATTN_BLOCK:]
            out_ref[:, hs] = _rope(dq, tab_ref[...], -1.0).astype(out_ref.dtype)
            out_ref[:, wide + hh * ATTN_HEAD_DIM:wide + (hh + 1) * ATTN_HEAD_DIM] = _rope(
                _dot_tn(ds, q2), tab_ref[...], -1.0).astype(out_ref.dtype)
            out_ref[:, 2 * wide + hh * ATTN_HEAD_DIM:2 * wide + (hh + 1) * ATTN_HEAD_DIM] = _dot_tn(
                p.astype(BF16), do2).astype(out_ref.dtype)

    def spec(width, row, col):
        return pl.BlockSpec((ATTN_BLOCK, width), lambda r, n: (r * nb + row(n), col))

    cur = lambda n: n
    nxt_block = lambda n: jnp.minimum(n + 1, nb - 1)
    return pl.pallas_call(
        body, name=name, out_shape=jax.ShapeDtypeStruct((s, QKV_GROUP), BF16), grid=(d, nb),
        in_specs=[spec(wide, cur, 0), spec(wide, nxt_block, 0), spec(wide, cur, 1), spec(wide, cur, 2),
                  spec(wide, cur, 0), spec(wide, nxt_block, 0), spec(128, cur, 0), spec(128, nxt_block, 0),
                  spec(128, cur, 0), spec(128, nxt_block, 0), spec(384, cur, 0)],
        out_specs=spec(QKV_GROUP, cur, 0),
        scratch_shapes=[pltpu.VMEM((ATTN_BLOCK, wide), F32)],
        compiler_params=_params("parallel", "arbitrary"),
    )(qkv, qkv, qkv, qkv, do, do, lse, lse, delta, delta, tab)


def _attn_core_fwd(qkvs, name):
    rot, os_, lses = [], [], []
    for g, (qkv, d) in enumerate(zip(qkvs, ATTN_DILATIONS)):
        qkv = _rope_qk(qkv, rope_table(qkv.shape[0], d), f"{name}_rope{g}")
        o_g, lse_g = _attn_group_fwd(qkv, d, f"{name}_fwd{g}")
        rot.append(qkv)
        os_.append(_from_dilated(o_g, d))
        lses.append(_from_dilated(lse_g, d))
    o, lse = _attn_combine(os_, lses, name + "_combine")
    return o, (tuple(rot), o, lse)


@functools.partial(jax.custom_vjp, nondiff_argnums=(1,))
def attn_core(qkvs, name):
    return _attn_core_fwd(qkvs, name)[0]


def _attn_core_vjp_fwd(qkvs, name):
    return _attn_core_fwd(qkvs, name)


def _attn_core_vjp_bwd(name, res, do):
    rot, o, lse = res
    delta = _attn_delta(do, o, name + "_delta")
    out = []
    for g, (qkv, d) in enumerate(zip(rot, ATTN_DILATIONS)):
        out.append(_attn_group_bwd(qkv, _to_dilated(do, d), _to_dilated(lse, d), _to_dilated(delta, d),
                                   rope_table(o.shape[0], d), d, f"{name}_bwd{g}"))
    return (tuple(out),)


attn_core.defvjp(_attn_core_vjp_fwd, _attn_core_vjp_bwd)


def attention_mixer_p(x, h, lin, j, carries, tag):
    qkvs = tuple(lin(_to_dilated(h, d), ('attn_w_qkv', j), BF16, f"{tag}_qkv{g}",
                     cols=slice(g * QKV_GROUP, (g + 1) * QKV_GROUP), carry=carries[g])
                 for g, d in enumerate(ATTN_DILATIONS))
    return lin(attn_core(qkvs, tag), ('attn_w_o', j), F32, tag + "_o", res=x)


SSM_CONV_TAPS = 4
SSM_COL_BLOCK = 2048
SSM_PAIRS = SSM_HEADS // 2
SSM_DT_BLOCK = (SSM_D_INNER + SSM_CONV_DIM) // 128


def _ssm_conv_fwd(zx, conv_w, conv_b, name):
    s = zx.shape[0]
    tm = _pick(s, (256, 128))
    per = tm // SUBLANES
    ncb = SSM_CONV_DIM // SSM_COL_BLOCK

    def body(w_ref, b_ref, x_ref, halo_ref, o_ref, pre_ref):
        keep = pl.program_id(1) > 0
        for c0 in range(0, SSM_COL_BLOCK, FFN_COLS):
            cols = slice(c0, c0 + FFN_COLS)
            halo = jnp.where(keep, halo_ref[:, cols], 0.0)
            taps = _conv_taps(x_ref[:, cols], halo, SSM_CONV_TAPS)
            pre = sum(w_ref[k:k + 1, cols] * taps[k] for k in range(SSM_CONV_TAPS)) + b_ref[:, cols]
            pre_ref[:, cols] = pre
            o_ref[:, cols] = pre * _sigmoid(pre)

    blk = pl.BlockSpec((tm, SSM_COL_BLOCK), lambda j, i: (i, j))
    return pl.pallas_call(
        body, name=name, out_shape=[jax.ShapeDtypeStruct((s, SSM_CONV_DIM), F32)] * 2, grid=(ncb, s // tm),
        in_specs=[pl.BlockSpec((SUBLANES, SSM_COL_BLOCK), lambda j, i: (0, j)),
                  pl.BlockSpec((1, SSM_COL_BLOCK), lambda j, i: (0, j)),
                  pl.BlockSpec((tm, SSM_COL_BLOCK), lambda j, i: (i, j + 1)),
                  pl.BlockSpec((SUBLANES, SSM_COL_BLOCK), lambda j, i: (jnp.maximum(i * per - 1, 0), j + 1))],
        out_specs=[blk, blk], compiler_params=_params("parallel", "parallel"),
    )(_pad_taps(conv_w), conv_b.reshape(1, SSM_CONV_DIM), zx, zx)


def _ssm_conv_bwd(zx, pre, conv_w, dact, name):
    s = zx.shape[0]
    tm = _pick(s, (256, 128))
    nt = s // tm
    ncb = SSM_CONV_DIM // SSM_COL_BLOCK
    nk = SSM_CONV_TAPS

    def body(w_ref, x_ref, pre_ref, da_ref, dx_ref, dw_ref, db_ref, carry_ref):
        first = pl.program_id(1) == 0
        for c0 in range(0, SSM_COL_BLOCK, FFN_COLS):
            cols = slice(c0, c0 + FFN_COLS)
            prev = pre_ref[:, cols]
            sig = _sigmoid(prev)
            dpre = da_ref[:, cols] * sig * (1.0 + prev * (1.0 - sig))
            nxt = jnp.where(first, 0.0, carry_ref[:, cols])
            ahead = [_shift_up(dpre, nxt, nk - 1 - k) for k in range(nk - 1)] + [dpre]
            dx_ref[:, cols] = sum(w_ref[k:k + 1, cols] * ahead[k] for k in range(nk)).astype(dx_ref.dtype)
            carry_ref[:, cols] = dpre[0:SUBLANES, :]
            x = x_ref[:, cols]
            dwp = jnp.concatenate([jnp.sum(ahead[k] * x, axis=0, keepdims=True) for k in range(nk)]
                                  + [jnp.zeros((SUBLANES - nk, FFN_COLS), F32)], axis=0)
            dbp = jnp.sum(dpre, axis=0, keepdims=True)

            @pl.when(first)
            def _():
                dw_ref[:, cols] = dwp
                db_ref[:, cols] = dbp

            @pl.when(jnp.logical_not(first))
            def _():
                dw_ref[:, cols] += dwp
                db_ref[:, cols] += dbp

    rev = lambda i: nt - 1 - i
    dx, dw, db = pl.pallas_call(
        body, name=name,
        out_shape=[jax.ShapeDtypeStruct((s, SSM_CONV_DIM), BF16), jax.ShapeDtypeStruct((SUBLANES, SSM_CONV_DIM), F32),
                   jax.ShapeDtypeStruct((1, SSM_CONV_DIM), F32)],
        grid=(ncb, nt),
        in_specs=[pl.BlockSpec((SUBLANES, SSM_COL_BLOCK), lambda j, i: (0, j)),
                  pl.BlockSpec((tm, SSM_COL_BLOCK), lambda j, i: (rev(i), j + 1)),
                  pl.BlockSpec((tm, SSM_COL_BLOCK), lambda j, i: (rev(i), j)),
                  pl.BlockSpec((tm, SSM_COL_BLOCK), lambda j, i: (rev(i), j))],
        out_specs=[pl.BlockSpec((tm, SSM_COL_BLOCK), lambda j, i: (rev(i), j)),
                   pl.BlockSpec((SUBLANES, SSM_COL_BLOCK), lambda j, i: (0, j)),
                   pl.BlockSpec((1, SSM_COL_BLOCK), lambda j, i: (0, j))],
        scratch_shapes=[pltpu.VMEM((SUBLANES, SSM_COL_BLOCK), F32)],
        compiler_params=_params("parallel", "arbitrary"),
    )(_pad_taps(conv_w), zx, pre, dact)
    return dx, dw[:nk], db.reshape(SSM_CONV_DIM)


def _ssd_chunk(xs, bms, cms, dt_raw, dtb, alog, dsk, states):
    q = SSM_CHUNK
    lane = lax.broadcasted_iota(jnp.int32, (1, 128), 1)
    row = lax.broadcasted_iota(jnp.int32, (q, 1), 0)
    ii = lax.broadcasted_iota(jnp.int32, (q, q), 0)
    jj = lax.broadcasted_iota(jnp.int32, (q, q), 1)
    tril = ii >= jj
    left = lane < SSM_HEAD_DIM
    last_row = (row == q - 1).astype(F32)

    def lanes_of(mat, h):
        pick = (lane == h).astype(F32)
        return jnp.broadcast_to(jnp.sum(mat * pick, axis=1, keepdims=True), mat.shape)

    def rows_of(mat_t, h):
        pick = (row == h).astype(F32)
        return jnp.broadcast_to(jnp.sum(mat_t * pick, axis=0, keepdims=True), mat_t.shape)

    v = dt_raw + dtb
    dt = jnp.maximum(v, 0.0) + jnp.log(1.0 + jnp.exp(-jnp.abs(v)))
    adt = dt * (-jnp.exp(alog))
    acs = jnp.dot(tril.astype(F32), adt, precision=lax.Precision.HIGHEST, preferred_element_type=F32)
    acs_t = acs.T
    ys, new_states = [], []
    for pr in range(SSM_PAIRS):
        g = pr // 2
        if pr % 2 == 0:
            cb = _dot_nt(cms[g].astype(BF16), bms[g].astype(BF16))
        cols = [lanes_of(acs, 2 * pr + e) for e in range(2)]
        dts = [lanes_of(dt, 2 * pr + e) for e in range(2)]
        rws = [rows_of(acs_t, 2 * pr + e) for e in range(2)]
        lasts = [jnp.sum(c * last_row, axis=0, keepdims=True) for c in cols]
        xdt = xs[pr] * jnp.where(left, dts[0], dts[1])
        halves = [jnp.where(left, xdt, 0.0).astype(BF16), jnp.where(left, 0.0, xdt).astype(BF16)]
        y_diag, s_new = 0.0, 0.0
        for e in range(2):
            lmat = jnp.where(tril, jnp.exp(jnp.minimum(cols[e] - rws[e], 0.0)), 0.0)
            y_diag = y_diag + _dot((cb * lmat).astype(BF16), halves[e])
            decay = jnp.exp(lasts[e] - cols[e])
            s_new = s_new + _dot_tn((bms[g] * decay).astype(BF16), halves[e])
        y_off = _dot(cms[g].astype(BF16), states[pr].astype(BF16)) * jnp.where(left, jnp.exp(cols[0]), jnp.exp(cols[1]))
        skip = jnp.where(left, lanes_of(dsk, 2 * pr), lanes_of(dsk, 2 * pr + 1))
        ys.append(y_diag + y_off + xs[pr] * skip)
        new_states.append(states[pr] * jnp.where(left, jnp.exp(lasts[0]), jnp.exp(lasts[1])) + s_new)
    return tuple(ys), tuple(new_states)


def _ssm_vec(v):
    return jnp.pad(v.reshape(1, -1), ((0, 0), (0, 128 - v.shape[0])))


def _ssd_scan_fwd(act, zx, dtb, alog, dsk, name):
    s = act.shape[0]
    nc = s // SSM_CHUNK
    ng = SSM_GROUPS

    def body(act_ref, dt_ref, dtb_ref, alog_ref, dsk_ref, y_ref, st_out_ref, st_ref):
        @pl.when(pl.program_id(0) == 0)
        def _():
            st_ref[...] = jnp.zeros_like(st_ref)

        tile = lambda k: act_ref[:, k * 128:(k + 1) * 128]
        xs = [tile(k) for k in range(SSM_PAIRS)]
        bms = [tile(SSM_PAIRS + k) for k in range(ng)]
        cms = [tile(SSM_PAIRS + ng + k) for k in range(ng)]
        states = [st_ref[k] for k in range(SSM_PAIRS)]
        st_out_ref[0] = st_ref[...]
        ys, new_states = _ssd_chunk(xs, bms, cms, dt_ref[...], dtb_ref[...], alog_ref[...], dsk_ref[...], states)
        for k in range(SSM_PAIRS):
            y_ref[:, k * 128:(k + 1) * 128] = ys[k]
            st_ref[k] = new_states[k]

    vec = pl.BlockSpec((1, 128), lambda c: (0, 0))
    return pl.pallas_call(
        body, name=name,
        out_shape=[jax.ShapeDtypeStruct((s, SSM_D_INNER), F32),
                   jax.ShapeDtypeStruct((nc, SSM_PAIRS, SSM_STATE, 128), F32)],
        grid=(nc,),
        in_specs=[pl.BlockSpec((SSM_CHUNK, SSM_CONV_DIM), lambda c: (c, 0)),
                  pl.BlockSpec((SSM_CHUNK, 128), lambda c: (c, SSM_DT_BLOCK)), vec, vec, vec],
        out_specs=[pl.BlockSpec((SSM_CHUNK, SSM_D_INNER), lambda c: (c, 0)),
                   pl.BlockSpec((1, SSM_PAIRS, SSM_STATE, 128), lambda c: (c, 0, 0, 0))],
        scratch_shapes=[pltpu.VMEM((SSM_PAIRS, SSM_STATE, 128), F32)],
        compiler_params=_params("arbitrary"),
    )(act, zx, _ssm_vec(dtb), _ssm_vec(alog), _ssm_vec(dsk))


def _ssd_scan_bwd(act, zx, dtb, alog, dsk, st_in, dy, name):
    s = act.shape[0]
    nc = s // SSM_CHUNK
    ng = SSM_GROUPS

    def body(act_ref, dt_ref, dtb_ref, alog_ref, dsk_ref, st_ref, dy_ref, dact_ref, ddt_ref, dpar_ref, dst_ref):
        first = pl.program_id(0) == 0

        @pl.when(first)
        def _():
            dst_ref[...] = jnp.zeros_like(dst_ref)

        tile = lambda k: act_ref[:, k * 128:(k + 1) * 128]
        xs = [tile(k) for k in range(SSM_PAIRS)]
        bms = [tile(SSM_PAIRS + k) for k in range(ng)]
        cms = [tile(SSM_PAIRS + ng + k) for k in range(ng)]
        states = [st_ref[0, k] for k in range(SSM_PAIRS)]
        _, pullback = jax.vjp(_ssd_chunk, xs, bms, cms, dt_ref[...], dtb_ref[...], alog_ref[...], dsk_ref[...],
                              states)
        dys = tuple(dy_ref[:, k * 128:(k + 1) * 128] for k in range(SSM_PAIRS))
        dsts = tuple(dst_ref[k] for k in range(SSM_PAIRS))
        dxs, dbms, dcms, ddt, ddtb, dalog, ddsk, dstates = pullback((dys, dsts))
        for k, t in enumerate(list(dxs) + list(dbms) + list(dcms)):
            dact_ref[:, k * 128:(k + 1) * 128] = t
        ddt_ref[...] = ddt.astype(ddt_ref.dtype)
        for k in range(SSM_PAIRS):
            dst_ref[k] = dstates[k]
        dpar = jnp.concatenate([ddtb, dalog, ddsk, jnp.zeros((SUBLANES - 3, 128), F32)], axis=0)
        _accumulate(dpar_ref, dpar, first)

    rev = lambda c: nc - 1 - c
    vec = pl.BlockSpec((1, 128), lambda c: (0, 0))
    dact, ddt, dpar = pl.pallas_call(
        body, name=name,
        out_shape=[jax.ShapeDtypeStruct((s, SSM_CONV_DIM), F32), jax.ShapeDtypeStruct((s, 128), BF16),
                   jax.ShapeDtypeStruct((SUBLANES, 128), F32)],
        grid=(nc,),
        in_specs=[pl.BlockSpec((SSM_CHUNK, SSM_CONV_DIM), lambda c: (rev(c), 0)),
                  pl.BlockSpec((SSM_CHUNK, 128), lambda c: (rev(c), SSM_DT_BLOCK)), vec, vec, vec,
                  pl.BlockSpec((1, SSM_PAIRS, SSM_STATE, 128), lambda c: (rev(c), 0, 0, 0)),
                  pl.BlockSpec((SSM_CHUNK, SSM_D_INNER), lambda c: (rev(c), 0))],
        out_specs=[pl.BlockSpec((SSM_CHUNK, SSM_CONV_DIM), lambda c: (rev(c), 0)),
                   pl.BlockSpec((SSM_CHUNK, 128), lambda c: (rev(c), 0)),
                   pl.BlockSpec((SUBLANES, 128), lambda c: (0, 0))],
        scratch_shapes=[pltpu.VMEM((SSM_PAIRS, SSM_STATE, 128), F32)],
        compiler_params=_params("arbitrary"),
    )(act, zx, _ssm_vec(dtb), _ssm_vec(alog), _ssm_vec(dsk), st_in, dy)
    return dact, ddt, dpar[0, :SSM_HEADS], dpar[1, :SSM_HEADS], dpar[2, :SSM_HEADS]


SSM_NORM_GROUP = SSM_D_INNER // SSM_GROUPS


def _gated_group(y, z, w):
    g = y * (z * _sigmoid(z))
    return g * lax.rsqrt(jnp.mean(g * g, axis=-1, keepdims=True) + NORM_EPS) * w


def _gated_norm_fwd(y, zx, w, name):
    s = y.shape[0]
    tm = _pick(s, (256, 128))

    def body(y_ref, z_ref, w_ref, o_ref):
        for c0 in range(0, SSM_D_INNER, SSM_NORM_GROUP):
            cols = slice(c0, c0 + SSM_NORM_GROUP)
            o_ref[:, cols] = _gated_group(y_ref[:, cols], z_ref[:, cols], w_ref[:, cols]).astype(o_ref.dtype)

    blk = pl.BlockSpec((tm, SSM_D_INNER), lambda i: (i, 0))
    return pl.pallas_call(
        body, name=name, out_shape=jax.ShapeDtypeStruct((s, SSM_D_INNER), BF16), grid=(s // tm,),
        in_specs=[blk, blk, pl.BlockSpec((1, SSM_D_INNER), lambda i: (0, 0))], out_specs=blk,
        compiler_params=_params("parallel"),
    )(y, zx, w.reshape(1, SSM_D_INNER))


def _gated_norm_bwd(y, zx, w, dout, name):
    s = y.shape[0]
    tm = _pick(s, (256, 128))

    def body(y_ref, z_ref, w_ref, do_ref, dy_ref, dz_ref, dw_ref):
        first = pl.program_id(0) == 0
        for c0 in range(0, SSM_D_INNER, SSM_NORM_GROUP):
            cols = slice(c0, c0 + SSM_NORM_GROUP)
            _, pullback = jax.vjp(_gated_group, y_ref[:, cols], z_ref[:, cols], w_ref[:, cols])
            dyv, dzv, dwv = pullback(do_ref[:, cols].astype(F32))
            dy_ref[:, cols] = dyv
            dz_ref[:, cols] = dzv.astype(dz_ref.dtype)

            @pl.when(first)
            def _():
                dw_ref[:, cols] = dwv

            @pl.when(jnp.logical_not(first))
            def _():
                dw_ref[:, cols] += dwv

    blk = pl.BlockSpec((tm, SSM_D_INNER), lambda i: (i, 0))
    vec = pl.BlockSpec((1, SSM_D_INNER), lambda i: (0, 0))
    dy, dz, dw = pl.pallas_call(
        body, name=name,
        out_shape=[jax.ShapeDtypeStruct((s, SSM_D_INNER), F32), jax.ShapeDtypeStruct((s, SSM_D_INNER), BF16),
                   jax.ShapeDtypeStruct((1, SSM_D_INNER), F32)],
        grid=(s // tm,), in_specs=[blk, blk, vec, blk], out_specs=[blk, blk, vec],
        compiler_params=_params("arbitrary"),
    )(y, zx, w.reshape(1, SSM_D_INNER), dout)
    return dy, dz, dw.reshape(SSM_D_INNER)


def _ssm_core_fwd(zx, conv_w, conv_b, dtb, alog, dsk, norm_w, name):
    act, pre = _ssm_conv_fwd(zx, conv_w, conv_b, name + "_conv_fwd")
    y, st_in = _ssd_scan_fwd(act, zx, dtb, alog, dsk, name + "_scan_fwd")
    out = _gated_norm_fwd(y, zx, norm_w, name + "_gate_fwd")
    return out, (zx, conv_w, pre, dtb, alog, dsk, norm_w, act, y, st_in)


@functools.partial(jax.custom_vjp, nondiff_argnums=(7,))
def ssm_core(zx, conv_w, conv_b, dtb, alog, dsk, norm_w, name):
    return _ssm_core_fwd(zx, conv_w, conv_b, dtb, alog, dsk, norm_w, name)[0]


def _ssm_core_vjp_fwd(zx, conv_w, conv_b, dtb, alog, dsk, norm_w, name):
    return _ssm_core_fwd(zx, conv_w, conv_b, dtb, alog, dsk, norm_w, name)


def _ssm_core_vjp_bwd(name, res, dout):
    zx, conv_w, pre, dtb, alog, dsk, norm_w, act, y, st_in = res
    dy, dz, dnorm_w = _gated_norm_bwd(y, zx, norm_w, dout, name + "_gate_bwd")
    dact, ddt, ddtb, dalog, ddsk = _ssd_scan_bwd(act, zx, dtb, alog, dsk, st_in, dy, name + "_scan_bwd")
    dxbc, dconv_w, dconv_b = _ssm_conv_bwd(zx, pre, conv_w, dact, name + "_conv_bwd")
    dzx = jnp.concatenate([dz, dxbc, ddt], axis=1)
    return dzx, dconv_w, dconv_b, ddtb, dalog, ddsk, dnorm_w


ssm_core.defvjp(_ssm_core_vjp_fwd, _ssm_core_vjp_bwd)


def ssd_mixer_p(x, h, lin, j, carries, conv_w, conv_b, dtb, alog, dsk, norm_w, tag):
    zx = lin(h, ('ssm_w_in', j), F32, tag + "_in", carry=carries[0])
    return lin(ssm_core(zx, conv_w, conv_b, dtb, alog, dsk, norm_w, tag), ('ssm_w_out', j), F32, tag + "_out",
               carry=carries[1], res=x)


def _full_weight(n, parts):
    full = _join8(parts, SHARD_AXIS[n] - 1)
    if n == 'ssm_w_in':
        full = jnp.pad(full, ((0, 0), (0, SSM_IN_PAD - SSM_IN_WIDTH)))
    return full


def trunk(w, x):
    ready = {('attn_w_qkv', 0): _full_weight('attn_w_qkv', gather_x(w['attn_w_qkv'][0], "gather_first"))}

    def lin(a, key, out_dtype, name, cols=None, carry=None, res=None):
        wt = ready[key] if cols is None else ready[key][:, cols]
        if carry is None:
            return linear(a, wt, res, out_dtype, name)
        y, parts = linear_x(a, wt, res, w[carry[0]][carry[1]], out_dtype, name)
        ready[carry] = _full_weight(carry[0], parts)
        return y

    for i in range(DEPTH):
        x, h = norm(x, w['mix_norm_w'][i], f"mixnorm{i}")
        j = i // 2
        ffn_next = [('ffn_w_up', i), ('ffn_w_down', i)]
        if i % 2 == 0:
            x = attention_mixer_p(x, h, lin, j, [('attn_w_o', j)] + ffn_next, f"attn{j}")
        else:
            x = ssd_mixer_p(x, h, lin, j, ffn_next, w['ssm_conv_w'][j], w['ssm_conv_b'][j], w['ssm_dt_bias'][j],
                            w['ssm_a_log'][j], w['ssm_d'][j], w['ssm_norm_w'][j], f"ssm{j}")
        if i + 1 == DEPTH:
            mixer_next = [None, None]
        elif i % 2 == 0:
            mixer_next = [('ssm_w_in', j), ('ssm_w_out', j)]
        else:
            mixer_next = [('attn_w_qkv', j + 1), None]
        x, h = norm(x, w['ffn_norm_w'][i], f"ffnnorm{i}")
        u0 = lin(h, ('ffn_w_up', i), F32, f"ffn{i}_up", carry=mixer_next[0])
        a = ffn_mid(u0, w['ffn_conv_w'][i], w['ffn_conv_b'][i], f"ffn{i}_mid")
        x = lin(a, ('ffn_w_down', i), F32, f"ffn{i}_down", carry=mixer_next[1], res=x)
    return x


def local_step(w, x, target):
    final_w = w['final_norm_w']
    trunk_w = {n: a for n, a in w.items() if n != 'final_norm_w'}
    xf, pullback = jax.vjp(trunk, trunk_w, x)
    loss, dxf, dfinal = loss_head(xf, final_w, target, "loss_head")
    gw, gx = pullback(dxf)
    gw['final_norm_w'] = dfinal
    return loss, gw, gx


def _adam_math(w, g, m, v):
    m = ADAM_B1 * m + (1.0 - ADAM_B1) * g
    v = ADAM_B2 * v + (1.0 - ADAM_B2) * (g * g)
    m_hat = m / (1.0 - ADAM_B1 ** ADAM_STEP)
    v_hat = v / (1.0 - ADAM_B2 ** ADAM_STEP)
    delta = -ADAM_LR * (m_hat / (jnp.sqrt(v_hat) + ADAM_EPS) + ADAM_WD * w)
    return delta, m, v


def _adamw_rows(g, w, m, v, name):
    r, c = w.shape
    tr = _pick(r, (256, 128, 64, 32, 16, 8))

    def body(g_ref, w_ref, m_ref, v_ref, d_out, m_out, v_out):
        delta, mm, vv = _adam_math(w_ref[...], g_ref[...], m_ref[...], v_ref[...])
        d_out[...] = delta
        m_out[...] = mm
        v_out[...] = vv

    blk = pl.BlockSpec((tr, c), lambda i: (i, 0))
    return pl.pallas_call(
        body, name=name, out_shape=[jax.ShapeDtypeStruct((r, c), F32)] * 3, grid=(r // tr,),
        in_specs=[blk] * 4, out_specs=[blk] * 3, compiler_params=_params("parallel"),
    )(g, w, m, v)


def _sum8(pieces, name):
    _, r, c = pieces.shape

    def body(p_ref, o_ref):
        g = p_ref[0]
        for j in range(1, N_DEV):
            g = g + p_ref[j]
        o_ref[...] = g

    return pl.pallas_call(
        body, name=name, out_shape=jax.ShapeDtypeStruct((r, c), F32),
        in_specs=[pl.BlockSpec(memory_space=pltpu.VMEM)], out_specs=pl.BlockSpec(memory_space=pltpu.VMEM),
    )(pieces)


def _adamw_plain(g, w, m, v, name):
    def body(g_ref, w_ref, m_ref, v_ref, d_out, m_out, v_out):
        delta, mm, vv = _adam_math(w_ref[...], g_ref[...], m_ref[...], v_ref[...])
        d_out[...] = delta
        m_out[...] = mm
        v_out[...] = vv

    vm = pl.BlockSpec(memory_space=pltpu.VMEM)
    return pl.pallas_call(
        body, name=name, out_shape=[jax.ShapeDtypeStruct(g.shape, F32)] * 3,
        in_specs=[vm] * 4, out_specs=[vm] * 3,
    )(g, w, m, v)


def _join8(parts, axis):
    t = jnp.moveaxis(parts, 0, axis)
    shp = t.shape
    return t.reshape(shp[:axis] + (shp[axis] * shp[axis + 1],) + shp[axis + 2:])


def _pack(arrs, lead, mult):
    flat = jnp.concatenate([a.reshape(a.shape[:lead] + (-1,)) for a in arrs], axis=-1)
    return _pad_rows(flat, mult)


def _unpack(buf, shapes, lead):
    flat = buf.reshape(buf.shape[:lead] + (-1,))
    out, off = [], 0
    for shp in shapes:
        n = math.prod(shp)
        out.append(flat[..., off:off + n].reshape(flat.shape[:lead] + tuple(shp)))
        off += n
    return out


def _own_shard(full, axis):
    size = full.shape[axis] // N_DEV
    return lax.dynamic_slice_in_dim(full, _my_index() * size, size, axis)


def kernel(x, mix_norm_w, attn_w_qkv, attn_w_o, ssm_w_in, ssm_conv_w, ssm_conv_b, ssm_dt_bias, ssm_a_log, ssm_d, ssm_norm_w, ssm_w_out, ffn_norm_w, ffn_w_up, ffn_conv_w, ffn_conv_b, ffn_w_down, final_norm_w, loss_target, m_mix_norm_w, m_attn_w_qkv, m_attn_w_o, m_ssm_w_in, m_ssm_conv_w, m_ssm_conv_b, m_ssm_dt_bias, m_ssm_a_log, m_ssm_d, m_ssm_norm_w, m_ssm_w_out, m_ffn_norm_w, m_ffn_w_up, m_ffn_conv_w, m_ffn_conv_b, m_ffn_w_down, m_final_norm_w, v_mix_norm_w, v_attn_w_qkv, v_attn_w_o, v_ssm_w_in, v_ssm_conv_w, v_ssm_conv_b, v_ssm_dt_bias, v_ssm_a_log, v_ssm_d, v_ssm_norm_w, v_ssm_w_out, v_ffn_norm_w, v_ffn_w_up, v_ffn_conv_w, v_ffn_conv_b, v_ffn_w_down, v_final_norm_w):
    args = dict(locals())
    w_sh = {n: args[n] for n in WEIGHT_NAMES}
    m_sh = {n: args["m_" + n] for n in WEIGHT_NAMES}
    v_sh = {n: args["v_" + n] for n in WEIGHT_NAMES}

    small_shapes = [w_sh[n].shape for n in SMALL_SHARDED]
    small = _exchange(_pack([w_sh[n] for n in SMALL_SHARDED], 0, 8), True, "gather_small")
    full = {n: w_sh[n] for n in SMALL if SHARD_AXIS[n] is None}
    for n, parts in zip(SMALL_SHARDED, _unpack(small, small_shapes, 1)):
        full[n] = _join8(parts, SHARD_AXIS[n])
    for n in BIG:
        full[n] = [w_sh[n][j] for j in range(w_sh[n].shape[0])]

    loss, gw, gx = local_step(full, x[0], loss_target[0])
    loss = lax.psum(loss, ("x", "y", "c"))
    for n in BIG:
        gw[n] = jnp.stack(gw[n])

    grads, deltas, new_m, new_v = {}, {}, {}, {}
    for n in BIG:
        shp = w_sh[n].shape
        two_d = (shp[0] * shp[1], shp[2])
        outs = _adamw_rows(*[t.reshape(two_d) for t in (gw[n], w_sh[n], m_sh[n], v_sh[n])], "adamw_" + n)
        grads[n] = gw[n]
        deltas[n], new_m[n], new_v[n] = [o.reshape(shp) for o in outs]

    small_full_shapes = [gw[n].shape for n in SMALL]
    gsmall = _exchange(_pack([gw[n] for n in SMALL], 0, 8), True, "gather_small_grads")
    gsmall = _unpack(_sum8(gsmall, "sum_small_grads"), small_full_shapes, 0)
    for n, g in zip(SMALL, gsmall):
        grads[n] = g if SHARD_AXIS[n] is None else _own_shard(g, SHARD_AXIS[n])
    shapes = [w_sh[n].shape for n in SMALL]
    outs = _adamw_plain(*[_pack([d[n] for n in SMALL], 0, 8) for d in (grads, w_sh, m_sh, v_sh)], "adamw_small")
    for d, buf in zip((deltas, new_m, new_v), outs):
        for n, a in zip(SMALL, _unpack(buf, shapes, 0)):
            d[n] = a

    return (loss, gx[None], *[grads[n] for n in WEIGHT_NAMES], *[deltas[n] for n in WEIGHT_NAMES],
            *[new_m[n] for n in WEIGHT_NAMES], *[new_v[n] for n in WEIGHT_NAMES])
```

```python
import functools
import math

import jax
import jax.numpy as jnp
from jax import lax
from jax.experimental import pallas as pl
from jax.experimental.pallas import tpu as pltpu

F32 = jnp.float32
BF16 = jnp.bfloat16
N_DEV = 8
MESH_ID = pl.DeviceIdType.MESH

D_MODEL = 1024
DEPTH = 4
ATTN_HEADS = 8
ATTN_HEAD_DIM = 128
ATTN_DILATIONS = (1, 4, 16)
ATTN_STEPS = (128, 128, 128)
N_ATTN_GROUPS = 3
ATTN_BLOCK = 128
ROPE_THETA = 500000.0
ROPE_DIM = 32
ATTN_OUT_WIDTH = 1024
SSM_D_INNER = 2048
SSM_HEAD_DIM = 64
SSM_HEADS = 32
SSM_STATE = 128
SSM_GROUPS = 8
SSM_CHUNK = 128
SSM_CONV_DIM = 4096
SSM_IN_WIDTH = 6176
SSM_IN_PAD = 6272
D_FF = 2816
NORM_EPS = 1e-5
ADAM_LR = 0.001
ADAM_B1 = 0.9
ADAM_B2 = 0.999
ADAM_EPS = 1e-08
ADAM_WD = 0.01
ADAM_STEP = 10

WEIGHT_NAMES = ['mix_norm_w', 'attn_w_qkv', 'attn_w_o', 'ssm_w_in', 'ssm_conv_w', 'ssm_conv_b', 'ssm_dt_bias',
                'ssm_a_log', 'ssm_d', 'ssm_norm_w', 'ssm_w_out', 'ffn_norm_w', 'ffn_w_up', 'ffn_conv_w',
                'ffn_conv_b', 'ffn_w_down', 'final_norm_w']
SHARD_AXIS = {'mix_norm_w': None, 'attn_w_qkv': 2, 'attn_w_o': 1, 'ssm_w_in': 2, 'ssm_conv_w': 2, 'ssm_conv_b': 1,
              'ssm_dt_bias': None, 'ssm_a_log': None, 'ssm_d': None, 'ssm_norm_w': 1, 'ssm_w_out': 1,
              'ffn_norm_w': None, 'ffn_w_up': 2, 'ffn_conv_w': 2, 'ffn_conv_b': None, 'ffn_w_down': 1,
              'final_norm_w': None}
BIG = ['attn_w_qkv', 'attn_w_o', 'ssm_w_in', 'ssm_w_out', 'ffn_w_up', 'ffn_w_down']
SMALL = [n for n in WEIGHT_NAMES if n not in BIG]
SMALL_SHARDED = [n for n in SMALL if SHARD_AXIS[n] is not None]
LANES = 1024


def _my_index():
    return 4 * lax.axis_index("x") + 2 * lax.axis_index("y") + lax.axis_index("c")


def _peer(k):
    x, y, c = lax.axis_index("x"), lax.axis_index("y"), lax.axis_index("c")
    return (x ^ ((k >> 2) & 1), y ^ ((k >> 1) & 1), c ^ (k & 1))


def _exchange(src, gather, name):
    def body(src_ref, out_ref, send_sems, recv_sems, local_sem):
        start, wait = _exchange_copies(src_ref, out_ref, send_sems, recv_sems, local_sem, gather)
        start()
        wait()

    return pl.pallas_call(
        body, name=name,
        out_shape=_exchange_out(src, gather),
        in_specs=[pl.BlockSpec(memory_space=pl.ANY)],
        out_specs=pl.BlockSpec(memory_space=pl.ANY),
        scratch_shapes=list(EXCHANGE_SEMS),
    )(src)


EXCHANGE_SEMS = (pltpu.SemaphoreType.DMA((N_DEV - 1,)), pltpu.SemaphoreType.DMA((N_DEV - 1,)),
                 pltpu.SemaphoreType.DMA)


def _exchange_out(src, gather):
    return jax.ShapeDtypeStruct((N_DEV,) + (src.shape if gather else src.shape[1:]), src.dtype)


def _exchange_copies(src_ref, out_ref, send_sems, recv_sems, local_sem, gather):
    me = _my_index()

    def piece(j):
        return src_ref if gather else src_ref.at[j]

    def remote(k, slab):
        return pltpu.make_async_remote_copy(
            src_ref=piece(me ^ k), dst_ref=out_ref.at[slab], send_sem=send_sems.at[k - 1],
            recv_sem=recv_sems.at[k - 1], device_id=_peer(k), device_id_type=MESH_ID)

    mine = pltpu.make_async_copy(piece(me), out_ref.at[me], local_sem)
    arrivals = {k: remote(k, me ^ k) for k in range(1, N_DEV)}
    if not gather:
        sends = [remote(k, me) for k in range(1, N_DEV)]

        def start():
            mine.start()
            for cp in sends:
                cp.start()

        def wait():
            for cp in arrivals.values():
                cp.wait_recv()
            for cp in sends:
                cp.wait_send()
            mine.wait()

        return start, wait

    far = (2, 4, 6)
    sends = [remote(k, me) for k in (1,) + far]
    passed_on = [pltpu.make_async_remote_copy(
        src_ref=out_ref.at[me ^ k], dst_ref=out_ref.at[me ^ k], send_sem=send_sems.at[k],
        recv_sem=recv_sems.at[k], device_id=_peer(1), device_id_type=MESH_ID) for k in far]

    def start():
        mine.start()
        for cp in sends:
            cp.start()

    def wait():
        for k, cp in zip(far, passed_on):
            arrivals[k].wait_recv()
            cp.start()
        for k in (1, 3, 5, 7):
            arrivals[k].wait_recv()
        for cp in sends + passed_on:
            cp.wait_send()
        mine.wait()

    return start, wait


def _pad_rows(flat, mult):
    n = flat.shape[-1]
    rows = -(-n // (LANES * mult)) * mult
    pad = rows * LANES - n
    flat = jnp.pad(flat, [(0, 0)] * (flat.ndim - 1) + [(0, pad)])
    return flat.reshape(flat.shape[:-1] + (rows, LANES))


MATMUL_VMEM_BUDGET = 48 * 1024 * 1024
MATMUL_TM = (1024, 1408, 512, 256, 128)
MATMUL_TN = (3072, 2816, 2048, 1536, 1408, 1024, 896, 512, 384, 256, 128)
MATMUL_TK = (3072, 2816, 2048, 1408, 1024, 896)
MATMUL_MIN_TK = 896


def _pick(n, cands):
    for c in cands:
        if n % c == 0:
            return c
    return n


def _matmul_tiles(m, n, k, a_bytes, b_bytes, out_bytes, has_res):
    tm = _pick(m, MATMUL_TM)

    def need(tn, tk):
        blocks = 2 * (tm * tk * a_bytes + tk * tn * b_bytes + tm * tn * out_bytes + has_res * tm * tn * 4)
        temps = tm * tn * 4 * (1 + (tk < k)) + (a_bytes > 2) * tm * tk * 2 + (b_bytes > 2) * tk * tn * 2
        return blocks + temps

    for tn in [c for c in MATMUL_TN if n % c == 0] + [n]:
        for tk in [k] + [c for c in MATMUL_TK if k % c == 0 and MATMUL_MIN_TK <= c < k]:
            if need(tn, tk) <= MATMUL_VMEM_BUDGET:
                return tm, tn, tk
    raise ValueError(f"no matmul tiling fits VMEM for {(m, n, k)}")


def _matmul(a, b, ta, tb, out_dtype, name, rider=None, res=None, rope=None):
    (m, k) = (a.shape[1], a.shape[0]) if ta else a.shape
    (k2, n) = (b.shape[1], b.shape[0]) if tb else b.shape
    assert k == k2, (a.shape, b.shape, ta, tb)
    tm, tn, tk = _matmul_tiles(m, n, k, a.dtype.itemsize, b.dtype.itemsize, jnp.dtype(out_dtype).itemsize,
                               res is not None)
    nk = k // tk
    dims = (((0 if ta else 1,), (1 if tb else 0,)), ((), ()))

    grid = (n // tn, m // tm, nk)

    n_in = 2 + (res is not None) + (rope is not None) + (rider is not None)
    if rope is not None:
        assert res is None and tn == n == QKV_GROUP, (tn, n)

    def body(*refs):
        ins, rest = refs[:n_in], refs[n_in:]
        a_ref, b_ref = ins[:2]
        res_ref = ins[2] if res is not None else None
        rope_ref = ins[2] if rope is not None else None
        o_ref, scratch = rest[0], rest[1:]
        if rider is not None:
            start, wait = _exchange_copies(ins[-1], rest[1], *scratch[-3:], rider[1])
            scratch = scratch[1:-3]
            at = [pl.program_id(ax) for ax in range(3)]
            pl.when(functools.reduce(jnp.logical_and, [p == 0 for p in at]))(start)
        part = lax.dot_general(a_ref[...].astype(BF16), b_ref[...].astype(BF16), dims,
                               preferred_element_type=F32)

        def finish(total):
            if rope_ref is not None:
                for c0 in range(0, tn, ATTN_HEAD_DIM):
                    head = total[:, c0:c0 + ATTN_HEAD_DIM]
                    if c0 < 2 * ATTN_OUT_WIDTH:
                        head = _rope(head, rope_ref[...], 1.0)
                    o_ref[:, c0:c0 + ATTN_HEAD_DIM] = head.astype(o_ref.dtype)
                return
            if res_ref is not None:
                total = total + res_ref[...]
            o_ref[...] = total.astype(o_ref.dtype)

        if nk == 1:
            finish(part)
        else:
            acc_ref, = scratch
            kk = pl.program_id(2)

            @pl.when(kk == 0)
            def _():
                acc_ref[...] = part

            @pl.when(kk > 0)
            def _():
                acc_ref[...] += part

            @pl.when(kk == nk - 1)
            def _():
                finish(acc_ref[...])
        if rider is not None:
            pl.when(functools.reduce(jnp.logical_and, [p == g - 1 for p, g in zip(at, grid)]))(wait)

    a_spec = (pl.BlockSpec((tk, tm), lambda j, i, kk: (kk, i)) if ta
              else pl.BlockSpec((tm, tk), lambda j, i, kk: (i, kk)))
    b_spec = (pl.BlockSpec((tn, tk), lambda j, i, kk: (j, kk)) if tb
              else pl.BlockSpec((tk, tn), lambda j, i, kk: (kk, j)))
    any_spec = pl.BlockSpec(memory_space=pl.ANY)
    tile_spec = pl.BlockSpec((tm, tn), lambda j, i, kk: (i, j))
    in_specs, operands = [a_spec, b_spec], [a, b]
    out_shape, out_specs = [jax.ShapeDtypeStruct((m, n), out_dtype)], [tile_spec]
    scratch = [] if nk == 1 else [pltpu.VMEM((tm, tn), F32)]
    sem = ("parallel", "parallel", "arbitrary")
    if res is not None:
        in_specs.append(tile_spec)
        operands.append(res)
    if rope is not None:
        in_specs.append(pl.BlockSpec((tm, rope.shape[1]), lambda j, i, kk: (i, 0)))
        operands.append(rope)
    if rider is not None:
        in_specs.append(any_spec)
        operands.append(rider[0])
        out_shape.append(_exchange_out(*rider))
        out_specs.append(any_spec)
        scratch += list(EXCHANGE_SEMS)
        sem = ("arbitrary",) * 3
    outs = pl.pallas_call(
        body, name=name, out_shape=out_shape, grid=grid, in_specs=in_specs, out_specs=out_specs,
        scratch_shapes=scratch, compiler_params=_params(*sem),
    )(*operands)
    return outs[0] if rider is None else tuple(outs)


@functools.partial(jax.custom_vjp, nondiff_argnums=(3, 4))
def linear(a, w, res, out_dtype, name):
    return _matmul(a, w, False, False, out_dtype, name + "_fwd", res=res)


def _linear_fwd(a, w, res, out_dtype, name):
    return _matmul(a, w, False, False, out_dtype, name + "_fwd", res=res), (a, w, res is not None)


def _linear_bwd(out_dtype, name, saved, dy):
    a, w, has_res = saved
    da = _matmul(dy, w, False, True, a.dtype, name + "_da")
    dw = _matmul(a, dy, True, False, w.dtype, name + "_dw")
    return da, dw, (dy if has_res else None)


linear.defvjp(_linear_fwd, _linear_bwd)


def _sum_pieces(pieces, name):
    shape = pieces.shape[1:]
    c = shape[-1]
    r = math.prod(shape[:-1])
    tr = _pick(r, (512, 256, 128, 64, 32, 16, 8))

    def body(p_ref, o_ref):
        g = p_ref[0].astype(F32)
        for j in range(1, N_DEV):
            g = g + p_ref[j].astype(F32)
        o_ref[...] = g

    return pl.pallas_call(
        body, name=name, out_shape=jax.ShapeDtypeStruct((r, c), F32), grid=(r // tr,),
        in_specs=[pl.BlockSpec((N_DEV, tr, c), lambda i: (0, i, 0))],
        out_specs=pl.BlockSpec((tr, c), lambda i: (i, 0)),
        compiler_params=_params("parallel"),
    )(pieces.reshape(N_DEV, r, c)).reshape(shape)


@functools.partial(jax.custom_vjp, nondiff_argnums=(1,))
def gather_x(shard, name):
    return _exchange(shard.astype(BF16), True, name)


def _gather_x_fwd(shard, name):
    return _exchange(shard.astype(BF16), True, name), None


def _gather_x_bwd(name, _, dparts):
    return (_sum_pieces(_exchange(dparts, False, name + "_back"), name + "_sum"),)


gather_x.defvjp(_gather_x_fwd, _gather_x_bwd)


@functools.partial(jax.custom_vjp, nondiff_argnums=(5, 6))
def linear_x(a, w, res, rope, shard, out_dtype, name):
    return _matmul(a, w, False, False, out_dtype, name + "_fwd", rider=(shard.astype(BF16), True), res=res,
                   rope=rope)


def _linear_x_fwd(a, w, res, rope, shard, out_dtype, name):
    out = _matmul(a, w, False, False, out_dtype, name + "_fwd", rider=(shard.astype(BF16), True), res=res,
                  rope=rope)
    return out, (a, w, res is not None, rope)


def _linear_x_bwd(out_dtype, name, saved, cts):
    a, w, has_res, rope = saved
    dy, dparts = cts
    half = dparts.shape[1] // 2
    da, moved_a = _matmul(dy, w, False, True, a.dtype, name + "_da", rider=(dparts[:, :half], False))
    dw, moved_b = _matmul(a, dy, True, False, w.dtype, name + "_dw", rider=(dparts[:, half:], False))
    dshard = jnp.concatenate([_sum_pieces(moved_a, name + "_sum_a"), _sum_pieces(moved_b, name + "_sum_b")])
    return da, dw, (dy if has_res else None), (None if rope is None else jnp.zeros_like(rope)), dshard


linear_x.defvjp(_linear_x_fwd, _linear_x_bwd)


VMEM_LIMIT = 56 * 1024 * 1024
SUBLANES = 8


def _params(*sem):
    return pltpu.CompilerParams(dimension_semantics=sem, vmem_limit_bytes=VMEM_LIMIT)


def _sigmoid(x):
    return 0.5 * jnp.tanh(0.5 * x) + 0.5


def _rstd(xv):
    return lax.rsqrt(jnp.mean(xv * xv, axis=-1, keepdims=True) + NORM_EPS)


def _accumulate(ref, part, first):
    @pl.when(first)
    def _():
        ref[...] = part

    @pl.when(jnp.logical_not(first))
    def _():
        ref[...] += part


def _norm_fwd(x, w, name):
    s, d = x.shape
    tm = _pick(s, (512, 256, 128))

    def body(x_ref, w_ref, h_ref):
        xv = x_ref[...]
        h_ref[...] = (xv * _rstd(xv) * w_ref[...]).astype(h_ref.dtype)

    return pl.pallas_call(
        body, name=name, out_shape=jax.ShapeDtypeStruct((s, d), BF16), grid=(s // tm,),
        in_specs=[pl.BlockSpec((tm, d), lambda i: (i, 0)), pl.BlockSpec((1, d), lambda i: (0, 0))],
        out_specs=pl.BlockSpec((tm, d), lambda i: (i, 0)), compiler_params=_params("parallel"),
    )(x, w.reshape(1, d))


def _norm_bwd(x, w, dh, dskip, name):
    s, d = x.shape
    tm = _pick(s, (512, 256, 128))

    def body(x_ref, w_ref, dh_ref, ds_ref, dx_ref, dw_ref):
        xv = x_ref[...]
        r = _rstd(xv)
        y = xv * r
        dhv = dh_ref[...].astype(F32)
        dy = dhv * w_ref[...]
        dx_ref[...] = ds_ref[...] + r * (dy - y * jnp.mean(dy * y, axis=-1, keepdims=True))
        _accumulate(dw_ref, jnp.sum(dhv * y, axis=0, keepdims=True), pl.program_id(0) == 0)

    row = pl.BlockSpec((tm, d), lambda i: (i, 0))
    vec = pl.BlockSpec((1, d), lambda i: (0, 0))
    dx, dw = pl.pallas_call(
        body, name=name,
        out_shape=[jax.ShapeDtypeStruct((s, d), F32), jax.ShapeDtypeStruct((1, d), F32)], grid=(s // tm,),
        in_specs=[row, vec, row, row], out_specs=[row, vec], compiler_params=_params("arbitrary"),
    )(x, w.reshape(1, d), dh, dskip)
    return dx, dw.reshape(d)


@functools.partial(jax.custom_vjp, nondiff_argnums=(2,))
def norm(x, w, name):
    return x, _norm_fwd(x, w, name + "_fwd")


def _norm_vjp_fwd(x, w, name):
    return (x, _norm_fwd(x, w, name + "_fwd")), (x, w)


def _norm_vjp_bwd(name, saved, cts):
    x, w = saved
    dskip, dh = cts
    return _norm_bwd(x, w, dh, dskip, name + "_bwd")


norm.defvjp(_norm_vjp_fwd, _norm_vjp_bwd)


def loss_head(x, w, target, name):
    s, d = x.shape
    tm = _pick(s, (512, 256, 128))

    def body(x_ref, w_ref, t_ref, loss_ref, dx_ref, dw_ref):
        first = pl.program_id(0) == 0
        xv = x_ref[...]
        r = _rstd(xv)
        y = xv * r
        err = y * w_ref[...] - t_ref[...]
        part = 0.5 * jnp.sum(jnp.sum(err * err, axis=-1, keepdims=True), axis=0, keepdims=True) / d
        _accumulate(loss_ref, jnp.broadcast_to(part, loss_ref.shape), first)
        dout = err / d
        dy = dout * w_ref[...]
        dx_ref[...] = r * (dy - y * jnp.mean(dy * y, axis=-1, keepdims=True))
        _accumulate(dw_ref, jnp.sum(dout * y, axis=0, keepdims=True), first)

    row = pl.BlockSpec((tm, d), lambda i: (i, 0))
    vec = pl.BlockSpec((1, d), lambda i: (0, 0))
    loss, dx, dw = pl.pallas_call(
        body, name=name,
        out_shape=[jax.ShapeDtypeStruct((1, 128), F32), jax.ShapeDtypeStruct((s, d), F32),
                   jax.ShapeDtypeStruct((1, d), F32)],
        grid=(s // tm,), in_specs=[row, vec, row],
        out_specs=[pl.BlockSpec((1, 128), lambda i: (0, 0)), row, vec],
        compiler_params=_params("arbitrary"),
    )(x, w.reshape(1, d), target)
    return loss[0, 0], dx, dw.reshape(d)


FFN_ROWS = 256
FFN_COLS = 256


def _shift_down(cur, halo, k):
    out = pltpu.roll(cur, k, axis=0)
    row = lax.broadcasted_iota(jnp.int32, halo.shape, 0)
    top = out[0:SUBLANES]
    for j in range(k):
        top = jnp.where(row == j, halo[SUBLANES - k + j:SUBLANES - k + j + 1, :], top)
    return jnp.concatenate([top, out[SUBLANES:]], axis=0)


def _shift_up(cur, nxt, k):
    n = cur.shape[0]
    out = pltpu.roll(cur, n - k, axis=0)
    row = lax.broadcasted_iota(jnp.int32, nxt.shape, 0)
    bottom = out[n - SUBLANES:]
    for j in range(k):
        bottom = jnp.where(row == SUBLANES - k + j, nxt[j:j + 1, :], bottom)
    return jnp.concatenate([out[:n - SUBLANES], bottom], axis=0)


def _conv_taps(cur, halo, ntaps):
    return [_shift_down(cur, halo, ntaps - 1 - k) if k < ntaps - 1 else cur for k in range(ntaps)]


def _pad_taps(conv_w):
    return jnp.pad(conv_w, ((0, SUBLANES - conv_w.shape[0]), (0, 0)))


def _ffn_mid_fwd(u0, conv_w, conv_b, name):
    s, width = u0.shape
    half = width // 2
    tm = _pick(s, (FFN_ROWS, 128))
    per = tm // SUBLANES

    def body(w_ref, b_ref, u_ref, halo_ref, a_ref, pre_ref):
        keep = pl.program_id(0) > 0
        for c0 in range(0, half, FFN_COLS):
            vals = []
            for base in (c0, half + c0):
                cols = slice(base, base + FFN_COLS)
                halo = jnp.where(keep, halo_ref[:, cols], 0.0)
                taps = _conv_taps(u_ref[:, cols], halo, 3)
                vals.append(sum(w_ref[k:k + 1, cols] * taps[k] for k in range(3)) + b_ref[:, cols])
                pre_ref[:, cols] = vals[-1].astype(pre_ref.dtype)
            gate, up = vals
            a_ref[:, c0:c0 + FFN_COLS] = (gate * _sigmoid(gate) * up).astype(a_ref.dtype)

    return pl.pallas_call(
        body, name=name,
        out_shape=[jax.ShapeDtypeStruct((s, half), BF16), jax.ShapeDtypeStruct((s, width), BF16)], grid=(s // tm,),
        in_specs=[pl.BlockSpec((SUBLANES, width), lambda i: (0, 0)), pl.BlockSpec((1, width), lambda i: (0, 0)),
                  pl.BlockSpec((tm, width), lambda i: (i, 0)),
                  pl.BlockSpec((SUBLANES, width), lambda i: (jnp.maximum(i * per - 1, 0), 0))],
        out_specs=[pl.BlockSpec((tm, half), lambda i: (i, 0)), pl.BlockSpec((tm, width), lambda i: (i, 0))],
        compiler_params=_params("parallel"),
    )(_pad_taps(conv_w), conv_b.reshape(1, width), u0, u0)


def _ffn_mid_bwd(u0, pre, conv_w, da, name):
    s, width = u0.shape
    half = width // 2
    tm = _pick(s, (FFN_ROWS, 128))
    nt = s // tm

    def body(w_ref, u_ref, pre_ref, da_ref, du0_ref, dw_ref, db_ref, carry_ref):
        first = pl.program_id(0) == 0
        for c0 in range(0, half, FFN_COLS):
            dav = da_ref[:, c0:c0 + FFN_COLS].astype(F32)
            gate = pre_ref[:, c0:c0 + FFN_COLS].astype(F32)
            up = pre_ref[:, half + c0:half + c0 + FFN_COLS].astype(F32)
            sig = _sigmoid(gate)
            dus = [dav * up * sig * (1.0 + gate * (1.0 - sig)), dav * gate * sig]
            for base, du in zip((c0, half + c0), dus):
                cols = slice(base, base + FFN_COLS)
                nxt = jnp.where(first, 0.0, carry_ref[:, cols])
                ahead = [_shift_up(du, nxt, 2), _shift_up(du, nxt, 1), du]
                du0 = sum(w_ref[k:k + 1, cols] * ahead[k] for k in range(3))
                du0_ref[:, cols] = du0.astype(du0_ref.dtype)
                carry_ref[:, cols] = du[0:SUBLANES, :]
                x = u_ref[:, cols]
                dwp = jnp.concatenate([jnp.sum(ahead[k] * x, axis=0, keepdims=True) for k in range(3)]
                                      + [jnp.zeros((SUBLANES - 3, FFN_COLS), F32)], axis=0)
                dbp = jnp.sum(du, axis=0, keepdims=True)

                @pl.when(first)
                def _():
                    dw_ref[:, cols] = dwp
                    db_ref[:, cols] = dbp

                @pl.when(jnp.logical_not(first))
                def _():
                    dw_ref[:, cols] += dwp
                    db_ref[:, cols] += dbp

    rev = lambda i: nt - 1 - i
    du0, dw, db = pl.pallas_call(
        body, name=name,
        out_shape=[jax.ShapeDtypeStruct((s, width), BF16), jax.ShapeDtypeStruct((SUBLANES, width), F32),
                   jax.ShapeDtypeStruct((1, width), F32)],
        grid=(nt,),
        in_specs=[pl.BlockSpec((SUBLANES, width), lambda i: (0, 0)),
                  pl.BlockSpec((tm, width), lambda i: (rev(i), 0)), pl.BlockSpec((tm, width), lambda i: (rev(i), 0)),
                  pl.BlockSpec((tm, half), lambda i: (rev(i), 0))],
        out_specs=[pl.BlockSpec((tm, width), lambda i: (rev(i), 0)),
                   pl.BlockSpec((SUBLANES, width), lambda i: (0, 0)), pl.BlockSpec((1, width), lambda i: (0, 0))],
        scratch_shapes=[pltpu.VMEM((SUBLANES, width), F32)],
        compiler_params=_params("arbitrary"),
    )(_pad_taps(conv_w), u0, pre, da)
    return du0, dw[:3], db.reshape(width)


@functools.partial(jax.custom_vjp, nondiff_argnums=(3,))
def ffn_mid(u0, conv_w, conv_b, name):
    return _ffn_mid_fwd(u0, conv_w, conv_b, name + "_fwd")[0]


def _ffn_mid_vjp_fwd(u0, conv_w, conv_b, name):
    a, pre = _ffn_mid_fwd(u0, conv_w, conv_b, name + "_fwd")
    return a, (u0, pre, conv_w)


def _ffn_mid_vjp_bwd(name, res, da):
    u0, pre, conv_w = res
    return _ffn_mid_bwd(u0, pre, conv_w, da, name + "_bwd")


ffn_mid.defvjp(_ffn_mid_vjp_fwd, _ffn_mid_vjp_bwd)


NEG = -1e30
HEAD_SLICES = [slice(hh * ATTN_HEAD_DIM, (hh + 1) * ATTN_HEAD_DIM) for hh in range(ATTN_HEADS)]
ATTN_SCALE = ATTN_HEAD_DIM ** -0.5
QKV_GROUP = 3 * ATTN_OUT_WIDTH


def rope_table(seq, d):
    pos = (jnp.arange(seq // d, dtype=jnp.int32)[None, :] * d + jnp.arange(d, dtype=jnp.int32)[:, None])
    inv_freq = ROPE_THETA ** (-jnp.arange(0, ROPE_DIM, 2, dtype=F32) / ROPE_DIM)
    ang = pos.reshape(seq).astype(F32)[:, None] * inv_freq[None, :]
    cos, sin = jnp.cos(ang), jnp.sin(ang)
    half = ROPE_DIM // 2
    ones = jnp.ones((seq, ATTN_HEAD_DIM - ROPE_DIM), F32)
    zero = lambda n: jnp.zeros((seq, n), F32)
    return jnp.concatenate([cos, cos, ones, -sin, zero(ATTN_HEAD_DIM - half),
                            zero(half), sin, zero(ATTN_HEAD_DIM - ROPE_DIM)], axis=1)


def _rope(t, tab, sign):
    half = ROPE_DIM // 2
    return t * tab[:, 0:128] + sign * (pltpu.roll(t, ATTN_HEAD_DIM - half, axis=1) * tab[:, 128:256]
                                       + pltpu.roll(t, half, axis=1) * tab[:, 256:384])


def _to_dilated(a, d):
    s = a.shape[0]
    return a if d == 1 else a.reshape(s // d, d, -1).transpose(1, 0, 2).reshape(s, -1)


def _from_dilated(a, d):
    s = a.shape[0]
    return a if d == 1 else a.reshape(d, s // d, -1).transpose(1, 0, 2).reshape(s, -1)


def _dot_nt(a, b):
    return lax.dot_general(a, b, (((1,), (1,)), ((), ())), preferred_element_type=F32)


def _dot_tn(a, b):
    return lax.dot_general(a, b, (((0,), (0,)), ((), ())), preferred_element_type=F32)


def _dot(a, b):
    return jnp.dot(a, b, preferred_element_type=F32)


def _window_mask(has_prev):
    ii = lax.broadcasted_iota(jnp.int32, (ATTN_BLOCK, 2 * ATTN_BLOCK), 0)
    jj = lax.broadcasted_iota(jnp.int32, (ATTN_BLOCK, 2 * ATTN_BLOCK), 1)
    in_window = jnp.logical_and(jj >= ii, jj <= ii + ATTN_BLOCK)
    return jnp.logical_and(in_window, jnp.logical_or(jj >= ATTN_BLOCK, has_prev))


def _both(prev_ref, cur_ref, hs):
    return jnp.concatenate([prev_ref[:, hs], cur_ref[:, hs]], axis=0)


def _attn_group_fwd(qkv, d, name):
    s = qkv.shape[0]
    nb = s // d // ATTN_BLOCK

    def body(q_ref, kc_ref, kp_ref, vc_ref, vp_ref, o_ref, lse_ref):
        mask = _window_mask(pl.program_id(1) > 0)
        lane = lax.broadcasted_iota(jnp.int32, (ATTN_BLOCK, 128), 1)
        lse_tile = jnp.zeros((ATTN_BLOCK, 128), F32)
        for hh, hs in enumerate(HEAD_SLICES):
            sc = jnp.where(mask, _dot_nt(q_ref[:, hs], _both(kp_ref, kc_ref, hs)) * ATTN_SCALE, NEG)
            m = jnp.max(sc, axis=1, keepdims=True)
            p = jnp.exp(sc - m)
            den = jnp.sum(p, axis=1, keepdims=True)
            o_ref[:, hs] = _dot(p.astype(BF16), _both(vp_ref, vc_ref, hs)) / den
            lse_tile = jnp.where(lane == hh, m + jnp.log(den), lse_tile)
        lse_ref[...] = lse_tile

    cur = lambda t: pl.BlockSpec((ATTN_BLOCK, ATTN_OUT_WIDTH), lambda r, n: (r * nb + n, t))
    prv = lambda t: pl.BlockSpec((ATTN_BLOCK, ATTN_OUT_WIDTH), lambda r, n: (r * nb + jnp.maximum(n - 1, 0), t))
    return pl.pallas_call(
        body, name=name,
        out_shape=[jax.ShapeDtypeStruct((s, ATTN_OUT_WIDTH), F32), jax.ShapeDtypeStruct((s, 128), F32)],
        grid=(d, nb), in_specs=[cur(0), cur(1), prv(1), cur(2), prv(2)],
        out_specs=[pl.BlockSpec((ATTN_BLOCK, ATTN_OUT_WIDTH), lambda r, n: (r * nb + n, 0)),
                   pl.BlockSpec((ATTN_BLOCK, 128), lambda r, n: (r * nb + n, 0))],
        compiler_params=_params("parallel", "parallel"),
    )(qkv, qkv, qkv, qkv, qkv)


def _attn_combine(os_, lses, name):
    s = os_[0].shape[0]
    tm = _pick(s, (256, 128))
    ng = len(os_)

    def body(*refs):
        o_refs, l_refs, (o_ref, lse_ref) = refs[:ng], refs[ng:2 * ng], refs[2 * ng:]
        lane = lax.broadcasted_iota(jnp.int32, (tm, 128), 1)
        lse_tile = jnp.zeros((tm, 128), F32)
        for hh, hs in enumerate(HEAD_SLICES):
            ls = [l_ref[:, hh:hh + 1] for l_ref in l_refs]
            m = functools.reduce(jnp.maximum, ls)
            ws = [jnp.exp(l - m) for l in ls]
            tot = functools.reduce(lambda a, b: a + b, ws)
            acc = functools.reduce(lambda a, b: a + b, [o_r[:, hs] * w for o_r, w in zip(o_refs, ws)])
            o_ref[:, hs] = (acc / tot).astype(o_ref.dtype)
            lse_tile = jnp.where(lane == hh, m + jnp.log(tot), lse_tile)
        lse_ref[...] = lse_tile

    wide = pl.BlockSpec((tm, ATTN_OUT_WIDTH), lambda i: (i, 0))
    thin = pl.BlockSpec((tm, 128), lambda i: (i, 0))
    return pl.pallas_call(
        body, name=name,
        out_shape=[jax.ShapeDtypeStruct((s, ATTN_OUT_WIDTH), BF16), jax.ShapeDtypeStruct((s, 128), F32)],
        grid=(s // tm,), in_specs=[wide] * ng + [thin] * ng, out_specs=[wide, thin],
        compiler_params=_params("parallel"),
    )(*os_, *lses)


def _attn_delta(do, o, name):
    s = do.shape[0]
    tm = _pick(s, (256, 128))

    def body(do_ref, o_ref, out_ref):
        lane = lax.broadcasted_iota(jnp.int32, (tm, 128), 1)
        tile = jnp.zeros((tm, 128), F32)
        for hh, hs in enumerate(HEAD_SLICES):
            prod = do_ref[:, hs].astype(F32) * o_ref[:, hs].astype(F32)
            tile = jnp.where(lane == hh, jnp.sum(prod, axis=1, keepdims=True), tile)
        out_ref[...] = tile

    wide = pl.BlockSpec((tm, ATTN_OUT_WIDTH), lambda i: (i, 0))
    return pl.pallas_call(
        body, name=name, out_shape=jax.ShapeDtypeStruct((s, 128), F32), grid=(s // tm,),
        in_specs=[wide, wide], out_specs=pl.BlockSpec((tm, 128), lambda i: (i, 0)),
        compiler_params=_params("parallel"),
    )(do, o)


def _attn_group_bwd(qkv, do, lse, delta, tab, d, name):
    s = qkv.shape[0]
    nb = s // d // ATTN_BLOCK
    wide = ATTN_OUT_WIDTH

    def body(q_ref, qn_ref, k_ref, v_ref, do_ref, don_ref, lse_ref, lsen_ref, dl_ref, dln_ref, tab_ref,
             out_ref, carry_ref):
        n = pl.program_id(1)

        @pl.when(n == 0)
        def _():
            carry_ref[...] = jnp.zeros_like(carry_ref)

        rows = lax.broadcasted_iota(jnp.int32, (2 * ATTN_BLOCK, ATTN_BLOCK), 0)
        keys = lax.broadcasted_iota(jnp.int32, (2 * ATTN_BLOCK, ATTN_BLOCK), 1)
        own = jnp.logical_and(rows < ATTN_BLOCK, keys <= rows)
        nxt = jnp.logical_and(rows >= ATTN_BLOCK, jnp.logical_and(keys >= rows - ATTN_BLOCK, n + 1 < nb))
        mask = jnp.logical_or(own, nxt)
        both = lambda a_ref, b_ref, cols: jnp.concatenate([a_ref[:, cols], b_ref[:, cols]], axis=0)
        for hh, hs in enumerate(HEAD_SLICES):
            one = slice(hh, hh + 1)
            q2, do2 = both(q_ref, qn_ref, hs), both(do_ref, don_ref, hs)
            kh, vh = k_ref[:, hs], v_ref[:, hs]
            sc = jnp.where(mask, _dot_nt(q2, kh) * ATTN_SCALE, NEG)
            p = jnp.exp(sc - both(lse_ref, lsen_ref, one))
            ds = (p * (_dot_nt(do2, vh) - both(dl_ref, dln_ref, one)) * ATTN_SCALE).astype(BF16)
            dq2 = _dot(ds, kh)
            dq = carry_ref[:, hs] + dq2[:ATTN_BLOCK]
            carry_ref[:, hs] = dq2[ATTN_BLOCK:]
            out_ref[:, hs] = _rope(dq, tab_ref[...], -1.0).astype(out_ref.dtype)
            out_ref[:, wide + hh * ATTN_HEAD_DIM:wide + (hh + 1) * ATTN_HEAD_DIM] = _rope(
                _dot_tn(ds, q2), tab_ref[...], -1.0).astype(out_ref.dtype)
            out_ref[:, 2 * wide + hh * ATTN_HEAD_DIM:2 * wide + (hh + 1) * ATTN_HEAD_DIM] = _dot_tn(
                p.astype(BF16), do2).astype(out_ref.dtype)

    def spec(width, row, col):
        return pl.BlockSpec((ATTN_BLOCK, width), lambda r, n: (r * nb + row(n), col))

    cur = lambda n: n
    nxt_block = lambda n: jnp.minimum(n + 1, nb - 1)
    return pl.pallas_call(
        body, name=name, out_shape=jax.ShapeDtypeStruct((s, QKV_GROUP), BF16), grid=(d, nb),
        in_specs=[spec(wide, cur, 0), spec(wide, nxt_block, 0), spec(wide, cur, 1), spec(wide, cur, 2),
                  spec(wide, cur, 0), spec(wide, nxt_block, 0), spec(128, cur, 0), spec(128, nxt_block, 0),
                  spec(128, cur, 0), spec(128, nxt_block, 0), spec(384, cur, 0)],
        out_specs=spec(QKV_GROUP, cur, 0),
        scratch_shapes=[pltpu.VMEM((ATTN_BLOCK, wide), F32)],
        compiler_params=_params("parallel", "arbitrary"),
    )(qkv, qkv, qkv, qkv, do, do, lse, lse, delta, delta, tab)


def _attn_core_fwd(qkvs, name):
    os_, lses = [], []
    for g, (qkv, d) in enumerate(zip(qkvs, ATTN_DILATIONS)):
        o_g, lse_g = _attn_group_fwd(qkv, d, f"{name}_fwd{g}")
        os_.append(_from_dilated(o_g, d))
        lses.append(_from_dilated(lse_g, d))
    o, lse = _attn_combine(os_, lses, name + "_combine")
    return o, (tuple(qkvs), o, lse)


@functools.partial(jax.custom_vjp, nondiff_argnums=(1,))
def attn_core(qkvs, name):
    return _attn_core_fwd(qkvs, name)[0]


def _attn_core_vjp_fwd(qkvs, name):
    return _attn_core_fwd(qkvs, name)


def _attn_core_vjp_bwd(name, res, do):
    rot, o, lse = res
    delta = _attn_delta(do, o, name + "_delta")
    out = []
    for g, (qkv, d) in enumerate(zip(rot, ATTN_DILATIONS)):
        out.append(_attn_group_bwd(qkv, _to_dilated(do, d), _to_dilated(lse, d), _to_dilated(delta, d),
                                   rope_table(o.shape[0], d), d, f"{name}_bwd{g}"))
    return (tuple(out),)


attn_core.defvjp(_attn_core_vjp_fwd, _attn_core_vjp_bwd)


def attention_mixer_p(x, h, lin, j, carries, tag):
    qkvs = tuple(lin(_to_dilated(h, d), ('attn_w_qkv', j), BF16, f"{tag}_qkv{g}",
                     cols=slice(g * QKV_GROUP, (g + 1) * QKV_GROUP), carry=carries[g],
                     rope=rope_table(h.shape[0], d))
                 for g, d in enumerate(ATTN_DILATIONS))
    return lin(attn_core(qkvs, tag), ('attn_w_o', j), F32, tag + "_o", res=x)


SSM_CONV_TAPS = 4
SSM_COL_BLOCK = 2048
SSM_PAIRS = SSM_HEADS // 2
SSM_DT_BLOCK = (SSM_D_INNER + SSM_CONV_DIM) // 128


def _ssm_conv_fwd(zx, conv_w, conv_b, name):
    s = zx.shape[0]
    tm = _pick(s, (256, 128))
    per = tm // SUBLANES
    ncb = SSM_CONV_DIM // SSM_COL_BLOCK

    def body(w_ref, b_ref, x_ref, halo_ref, o_ref, pre_ref):
        keep = pl.program_id(1) > 0
        for c0 in range(0, SSM_COL_BLOCK, FFN_COLS):
            cols = slice(c0, c0 + FFN_COLS)
            halo = jnp.where(keep, halo_ref[:, cols], 0.0)
            taps = _conv_taps(x_ref[:, cols], halo, SSM_CONV_TAPS)
            pre = sum(w_ref[k:k + 1, cols] * taps[k] for k in range(SSM_CONV_TAPS)) + b_ref[:, cols]
            pre_ref[:, cols] = pre.astype(pre_ref.dtype)
            o_ref[:, cols] = pre * _sigmoid(pre)

    blk = pl.BlockSpec((tm, SSM_COL_BLOCK), lambda j, i: (i, j))
    return pl.pallas_call(
        body, name=name,
        out_shape=[jax.ShapeDtypeStruct((s, SSM_CONV_DIM), F32), jax.ShapeDtypeStruct((s, SSM_CONV_DIM), BF16)],
        grid=(ncb, s // tm),
        in_specs=[pl.BlockSpec((SUBLANES, SSM_COL_BLOCK), lambda j, i: (0, j)),
                  pl.BlockSpec((1, SSM_COL_BLOCK), lambda j, i: (0, j)),
                  pl.BlockSpec((tm, SSM_COL_BLOCK), lambda j, i: (i, j + 1)),
                  pl.BlockSpec((SUBLANES, SSM_COL_BLOCK), lambda j, i: (jnp.maximum(i * per - 1, 0), j + 1))],
        out_specs=[blk, blk], compiler_params=_params("parallel", "parallel"),
    )(_pad_taps(conv_w), conv_b.reshape(1, SSM_CONV_DIM), zx, zx)


def _ssm_conv_bwd(zx, pre, conv_w, dact, name):
    s = zx.shape[0]
    tm = _pick(s, (256, 128))
    nt = s // tm
    ncb = SSM_CONV_DIM // SSM_COL_BLOCK
    nk = SSM_CONV_TAPS

    def body(w_ref, x_ref, pre_ref, da_ref, dx_ref, dw_ref, db_ref, carry_ref):
        first = pl.program_id(1) == 0
        for c0 in range(0, SSM_COL_BLOCK, FFN_COLS):
            cols = slice(c0, c0 + FFN_COLS)
            prev = pre_ref[:, cols].astype(F32)
            sig = _sigmoid(prev)
            dpre = da_ref[:, cols] * sig * (1.0 + prev * (1.0 - sig))
            nxt = jnp.where(first, 0.0, carry_ref[:, cols])
            ahead = [_shift_up(dpre, nxt, nk - 1 - k) for k in range(nk - 1)] + [dpre]
            dx_ref[:, cols] = sum(w_ref[k:k + 1, cols] * ahead[k] for k in range(nk)).astype(dx_ref.dtype)
            carry_ref[:, cols] = dpre[0:SUBLANES, :]
            x = x_ref[:, cols]
            dwp = jnp.concatenate([jnp.sum(ahead[k] * x, axis=0, keepdims=True) for k in range(nk)]
                                  + [jnp.zeros((SUBLANES - nk, FFN_COLS), F32)], axis=0)
            dbp = jnp.sum(dpre, axis=0, keepdims=True)

            @pl.when(first)
            def _():
                dw_ref[:, cols] = dwp
                db_ref[:, cols] = dbp

            @pl.when(jnp.logical_not(first))
            def _():
                dw_ref[:, cols] += dwp
                db_ref[:, cols] += dbp

    rev = lambda i: nt - 1 - i
    dx, dw, db = pl.pallas_call(
        body, name=name,
        out_shape=[jax.ShapeDtypeStruct((s, SSM_CONV_DIM), BF16), jax.ShapeDtypeStruct((SUBLANES, SSM_CONV_DIM), F32),
                   jax.ShapeDtypeStruct((1, SSM_CONV_DIM), F32)],
        grid=(ncb, nt),
        in_specs=[pl.BlockSpec((SUBLANES, SSM_COL_BLOCK), lambda j, i: (0, j)),
                  pl.BlockSpec((tm, SSM_COL_BLOCK), lambda j, i: (rev(i), j + 1)),
                  pl.BlockSpec((tm, SSM_COL_BLOCK), lambda j, i: (rev(i), j)),
                  pl.BlockSpec((tm, SSM_COL_BLOCK), lambda j, i: (rev(i), j))],
        out_specs=[pl.BlockSpec((tm, SSM_COL_BLOCK), lambda j, i: (rev(i), j)),
                   pl.BlockSpec((SUBLANES, SSM_COL_BLOCK), lambda j, i: (0, j)),
                   pl.BlockSpec((1, SSM_COL_BLOCK), lambda j, i: (0, j))],
        scratch_shapes=[pltpu.VMEM((SUBLANES, SSM_COL_BLOCK), F32)],
        compiler_params=_params("parallel", "arbitrary"),
    )(_pad_taps(conv_w), zx, pre, dact)
    return dx, dw[:nk], db.reshape(SSM_CONV_DIM)


def _ssd_chunk(xs, bms, cms, dt_raw, dtb, alog, dsk, states):
    q = SSM_CHUNK
    lane = lax.broadcasted_iota(jnp.int32, (1, 128), 1)
    row = lax.broadcasted_iota(jnp.int32, (q, 1), 0)
    ii = lax.broadcasted_iota(jnp.int32, (q, q), 0)
    jj = lax.broadcasted_iota(jnp.int32, (q, q), 1)
    tril = ii >= jj
    left = lane < SSM_HEAD_DIM
    last_row = (row == q - 1).astype(F32)

    def lanes_of(mat, h):
        pick = (lane == h).astype(F32)
        return jnp.broadcast_to(jnp.sum(mat * pick, axis=1, keepdims=True), mat.shape)

    def rows_of(mat_t, h):
        pick = (row == h).astype(F32)
        return jnp.broadcast_to(jnp.sum(mat_t * pick, axis=0, keepdims=True), mat_t.shape)

    v = dt_raw + dtb
    dt = jnp.maximum(v, 0.0) + jnp.log(1.0 + jnp.exp(-jnp.abs(v)))
    adt = dt * (-jnp.exp(alog))
    acs = jnp.dot(tril.astype(F32), adt, precision=lax.Precision.HIGHEST, preferred_element_type=F32)
    acs_t = acs.T
    ys, new_states = [], []
    for pr in range(SSM_PAIRS):
        g = pr // 2
        if pr % 2 == 0:
            cb = _dot_nt(cms[g].astype(BF16), bms[g].astype(BF16))
        cols = [lanes_of(acs, 2 * pr + e) for e in range(2)]
        dts = [lanes_of(dt, 2 * pr + e) for e in range(2)]
        rws = [rows_of(acs_t, 2 * pr + e) for e in range(2)]
        lasts = [jnp.sum(c * last_row, axis=0, keepdims=True) for c in cols]
        xdt = xs[pr] * jnp.where(left, dts[0], dts[1])
        halves = [jnp.where(left, xdt, 0.0).astype(BF16), jnp.where(left, 0.0, xdt).astype(BF16)]
        y_diag, s_new = 0.0, 0.0
        for e in range(2):
            lmat = jnp.where(tril, jnp.exp(jnp.minimum(cols[e] - rws[e], 0.0)), 0.0)
            y_diag = y_diag + _dot((cb * lmat).astype(BF16), halves[e])
            decay = jnp.exp(lasts[e] - cols[e])
            s_new = s_new + _dot_tn((bms[g] * decay).astype(BF16), halves[e])
        y_off = _dot(cms[g].astype(BF16), states[pr].astype(BF16)) * jnp.where(left, jnp.exp(cols[0]), jnp.exp(cols[1]))
        skip = jnp.where(left, lanes_of(dsk, 2 * pr), lanes_of(dsk, 2 * pr + 1))
        ys.append(y_diag + y_off + xs[pr] * skip)
        new_states.append(states[pr] * jnp.where(left, jnp.exp(lasts[0]), jnp.exp(lasts[1])) + s_new)
    return tuple(ys), tuple(new_states)


def _ssm_vec(v):
    return jnp.pad(v.reshape(1, -1), ((0, 0), (0, 128 - v.shape[0])))


def _ssd_scan_fwd(act, zx, dtb, alog, dsk, name):
    s = act.shape[0]
    nc = s // SSM_CHUNK
    ng = SSM_GROUPS

    def body(act_ref, dt_ref, dtb_ref, alog_ref, dsk_ref, y_ref, st_out_ref, st_ref):
        @pl.when(pl.program_id(0) == 0)
        def _():
            st_ref[...] = jnp.zeros_like(st_ref)

        tile = lambda k: act_ref[:, k * 128:(k + 1) * 128]
        xs = [tile(k) for k in range(SSM_PAIRS)]
        bms = [tile(SSM_PAIRS + k) for k in range(ng)]
        cms = [tile(SSM_PAIRS + ng + k) for k in range(ng)]
        states = [st_ref[k] for k in range(SSM_PAIRS)]
        st_out_ref[0] = st_ref[...]
        ys, new_states = _ssd_chunk(xs, bms, cms, dt_ref[...], dtb_ref[...], alog_ref[...], dsk_ref[...], states)
        for k in range(SSM_PAIRS):
            y_ref[:, k * 128:(k + 1) * 128] = ys[k]
            st_ref[k] = new_states[k]

    vec = pl.BlockSpec((1, 128), lambda c: (0, 0))
    return pl.pallas_call(
        body, name=name,
        out_shape=[jax.ShapeDtypeStruct((s, SSM_D_INNER), F32),
                   jax.ShapeDtypeStruct((nc, SSM_PAIRS, SSM_STATE, 128), F32)],
        grid=(nc,),
        in_specs=[pl.BlockSpec((SSM_CHUNK, SSM_CONV_DIM), lambda c: (c, 0)),
                  pl.BlockSpec((SSM_CHUNK, 128), lambda c: (c, SSM_DT_BLOCK)), vec, vec, vec],
        out_specs=[pl.BlockSpec((SSM_CHUNK, SSM_D_INNER), lambda c: (c, 0)),
                   pl.BlockSpec((1, SSM_PAIRS, SSM_STATE, 128), lambda c: (c, 0, 0, 0))],
        scratch_shapes=[pltpu.VMEM((SSM_PAIRS, SSM_STATE, 128), F32)],
        compiler_params=_params("arbitrary"),
    )(act, zx, _ssm_vec(dtb), _ssm_vec(alog), _ssm_vec(dsk))


def _ssd_scan_bwd(act, zx, dtb, alog, dsk, st_in, dy, name):
    s = act.shape[0]
    nc = s // SSM_CHUNK
    ng = SSM_GROUPS

    def body(act_ref, dt_ref, dtb_ref, alog_ref, dsk_ref, st_ref, dy_ref, dact_ref, ddt_ref, dpar_ref, dst_ref):
        first = pl.program_id(0) == 0

        @pl.when(first)
        def _():
            dst_ref[...] = jnp.zeros_like(dst_ref)

        tile = lambda k: act_ref[:, k * 128:(k + 1) * 128]
        xs = [tile(k) for k in range(SSM_PAIRS)]
        bms = [tile(SSM_PAIRS + k) for k in range(ng)]
        cms = [tile(SSM_PAIRS + ng + k) for k in range(ng)]
        states = [st_ref[0, k] for k in range(SSM_PAIRS)]
        _, pullback = jax.vjp(_ssd_chunk, xs, bms, cms, dt_ref[...], dtb_ref[...], alog_ref[...], dsk_ref[...],
                              states)
        dys = tuple(dy_ref[:, k * 128:(k + 1) * 128] for k in range(SSM_PAIRS))
        dsts = tuple(dst_ref[k] for k in range(SSM_PAIRS))
        dxs, dbms, dcms, ddt, ddtb, dalog, ddsk, dstates = pullback((dys, dsts))
        for k, t in enumerate(list(dxs) + list(dbms) + list(dcms)):
            dact_ref[:, k * 128:(k + 1) * 128] = t
        ddt_ref[...] = ddt.astype(ddt_ref.dtype)
        for k in range(SSM_PAIRS):
            dst_ref[k] = dstates[k]
        dpar = jnp.concatenate([ddtb, dalog, ddsk, jnp.zeros((SUBLANES - 3, 128), F32)], axis=0)
        _accumulate(dpar_ref, dpar, first)

    rev = lambda c: nc - 1 - c
    vec = pl.BlockSpec((1, 128), lambda c: (0, 0))
    dact, ddt, dpar = pl.pallas_call(
        body, name=name,
        out_shape=[jax.ShapeDtypeStruct((s, SSM_CONV_DIM), F32), jax.ShapeDtypeStruct((s, 128), BF16),
                   jax.ShapeDtypeStruct((SUBLANES, 128), F32)],
        grid=(nc,),
        in_specs=[pl.BlockSpec((SSM_CHUNK, SSM_CONV_DIM), lambda c: (rev(c), 0)),
                  pl.BlockSpec((SSM_CHUNK, 128), lambda c: (rev(c), SSM_DT_BLOCK)), vec, vec, vec,
                  pl.BlockSpec((1, SSM_PAIRS, SSM_STATE, 128), lambda c: (rev(c), 0, 0, 0)),
                  pl.BlockSpec((SSM_CHUNK, SSM_D_INNER), lambda c: (rev(c), 0))],
        out_specs=[pl.BlockSpec((SSM_CHUNK, SSM_CONV_DIM), lambda c: (rev(c), 0)),
                   pl.BlockSpec((SSM_CHUNK, 128), lambda c: (rev(c), 0)),
                   pl.BlockSpec((SUBLANES, 128), lambda c: (0, 0))],
        scratch_shapes=[pltpu.VMEM((SSM_PAIRS, SSM_STATE, 128), F32)],
        compiler_params=_params("arbitrary"),
    )(act, zx, _ssm_vec(dtb), _ssm_vec(alog), _ssm_vec(dsk), st_in, dy)
    return dact, ddt, dpar[0, :SSM_HEADS], dpar[1, :SSM_HEADS], dpar[2, :SSM_HEADS]


SSM_NORM_GROUP = SSM_D_INNER // SSM_GROUPS


def _gated_group(y, z, w):
    g = y * (z * _sigmoid(z))
    return g * lax.rsqrt(jnp.mean(g * g, axis=-1, keepdims=True) + NORM_EPS) * w


def _gated_norm_fwd(y, zx, w, name):
    s = y.shape[0]
    tm = _pick(s, (256, 128))

    def body(y_ref, z_ref, w_ref, o_ref):
        for c0 in range(0, SSM_D_INNER, SSM_NORM_GROUP):
            cols = slice(c0, c0 + SSM_NORM_GROUP)
            o_ref[:, cols] = _gated_group(y_ref[:, cols], z_ref[:, cols], w_ref[:, cols]).astype(o_ref.dtype)

    blk = pl.BlockSpec((tm, SSM_D_INNER), lambda i: (i, 0))
    return pl.pallas_call(
        body, name=name, out_shape=jax.ShapeDtypeStruct((s, SSM_D_INNER), BF16), grid=(s // tm,),
        in_specs=[blk, blk, pl.BlockSpec((1, SSM_D_INNER), lambda i: (0, 0))], out_specs=blk,
        compiler_params=_params("parallel"),
    )(y, zx, w.reshape(1, SSM_D_INNER))


def _gated_norm_bwd(y, zx, w, dout, name):
    s = y.shape[0]
    tm = _pick(s, (256, 128))

    def body(y_ref, z_ref, w_ref, do_ref, dy_ref, dz_ref, dw_ref):
        first = pl.program_id(0) == 0
        for c0 in range(0, SSM_D_INNER, SSM_NORM_GROUP):
            cols = slice(c0, c0 + SSM_NORM_GROUP)
            _, pullback = jax.vjp(_gated_group, y_ref[:, cols], z_ref[:, cols], w_ref[:, cols])
            dyv, dzv, dwv = pullback(do_ref[:, cols].astype(F32))
            dy_ref[:, cols] = dyv
            dz_ref[:, cols] = dzv.astype(dz_ref.dtype)

            @pl.when(first)
            def _():
                dw_ref[:, cols] = dwv

            @pl.when(jnp.logical_not(first))
            def _():
                dw_ref[:, cols] += dwv

    blk = pl.BlockSpec((tm, SSM_D_INNER), lambda i: (i, 0))
    vec = pl.BlockSpec((1, SSM_D_INNER), lambda i: (0, 0))
    dy, dz, dw = pl.pallas_call(
        body, name=name,
        out_shape=[jax.ShapeDtypeStruct((s, SSM_D_INNER), F32), jax.ShapeDtypeStruct((s, SSM_D_INNER), BF16),
                   jax.ShapeDtypeStruct((1, SSM_D_INNER), F32)],
        grid=(s // tm,), in_specs=[blk, blk, vec, blk], out_specs=[blk, blk, vec],
        compiler_params=_params("arbitrary"),
    )(y, zx, w.reshape(1, SSM_D_INNER), dout)
    return dy, dz, dw.reshape(SSM_D_INNER)


def _ssm_core_fwd(zx, conv_w, conv_b, dtb, alog, dsk, norm_w, name):
    act, pre = _ssm_conv_fwd(zx, conv_w, conv_b, name + "_conv_fwd")
    y, st_in = _ssd_scan_fwd(act, zx, dtb, alog, dsk, name + "_scan_fwd")
    out = _gated_norm_fwd(y, zx, norm_w, name + "_gate_fwd")
    return out, (zx, conv_w, pre, dtb, alog, dsk, norm_w, act, y, st_in)


@functools.partial(jax.custom_vjp, nondiff_argnums=(7,))
def ssm_core(zx, conv_w, conv_b, dtb, alog, dsk, norm_w, name):
    return _ssm_core_fwd(zx, conv_w, conv_b, dtb, alog, dsk, norm_w, name)[0]


def _ssm_core_vjp_fwd(zx, conv_w, conv_b, dtb, alog, dsk, norm_w, name):
    return _ssm_core_fwd(zx, conv_w, conv_b, dtb, alog, dsk, norm_w, name)


def _ssm_core_vjp_bwd(name, res, dout):
    zx, conv_w, pre, dtb, alog, dsk, norm_w, act, y, st_in = res
    dy, dz, dnorm_w = _gated_norm_bwd(y, zx, norm_w, dout, name + "_gate_bwd")
    dact, ddt, ddtb, dalog, ddsk = _ssd_scan_bwd(act, zx, dtb, alog, dsk, st_in, dy, name + "_scan_bwd")
    dxbc, dconv_w, dconv_b = _ssm_conv_bwd(zx, pre, conv_w, dact, name + "_conv_bwd")
    dzx = jnp.concatenate([dz, dxbc, ddt], axis=1)
    return dzx, dconv_w, dconv_b, ddtb, dalog, ddsk, dnorm_w


ssm_core.defvjp(_ssm_core_vjp_fwd, _ssm_core_vjp_bwd)


def ssd_mixer_p(x, h, lin, j, carries, conv_w, conv_b, dtb, alog, dsk, norm_w, tag):
    zx = lin(h, ('ssm_w_in', j), F32, tag + "_in", carry=carries[0])
    return lin(ssm_core(zx, conv_w, conv_b, dtb, alog, dsk, norm_w, tag), ('ssm_w_out', j), F32, tag + "_out",
               carry=carries[1], res=x)


def _full_weight(n, parts):
    full = _join8(parts, SHARD_AXIS[n] - 1)
    if n == 'ssm_w_in':
        full = jnp.pad(full, ((0, 0), (0, SSM_IN_PAD - SSM_IN_WIDTH)))
    return full


def trunk(w, x):
    ready = {('attn_w_qkv', 0): _full_weight('attn_w_qkv', gather_x(w['attn_w_qkv'][0], "gather_first"))}

    def lin(a, key, out_dtype, name, cols=None, carry=None, res=None, rope=None):
        wt = ready[key] if cols is None else ready[key][:, cols]
        if carry is None:
            assert rope is None
            return linear(a, wt, res, out_dtype, name)
        y, parts = linear_x(a, wt, res, rope, w[carry[0]][carry[1]], out_dtype, name)
        ready[carry] = _full_weight(carry[0], parts)
        return y

    for i in range(DEPTH):
        x, h = norm(x, w['mix_norm_w'][i], f"mixnorm{i}")
        j = i // 2
        ffn_next = [('ffn_w_up', i), ('ffn_w_down', i)]
        if i % 2 == 0:
            x = attention_mixer_p(x, h, lin, j, [('attn_w_o', j)] + ffn_next, f"attn{j}")
        else:
            x = ssd_mixer_p(x, h, lin, j, ffn_next, w['ssm_conv_w'][j], w['ssm_conv_b'][j], w['ssm_dt_bias'][j],
                            w['ssm_a_log'][j], w['ssm_d'][j], w['ssm_norm_w'][j], f"ssm{j}")
        if i + 1 == DEPTH:
            mixer_next = [None, None]
        elif i % 2 == 0:
            mixer_next = [('ssm_w_in', j), ('ssm_w_out', j)]
        else:
            mixer_next = [('attn_w_qkv', j + 1), None]
        x, h = norm(x, w['ffn_norm_w'][i], f"ffnnorm{i}")
        u0 = lin(h, ('ffn_w_up', i), F32, f"ffn{i}_up", carry=mixer_next[0])
        a = ffn_mid(u0, w['ffn_conv_w'][i], w['ffn_conv_b'][i], f"ffn{i}_mid")
        x = lin(a, ('ffn_w_down', i), F32, f"ffn{i}_down", carry=mixer_next[1], res=x)
    return x


def local_step(w, x, target):
    final_w = w['final_norm_w']
    trunk_w = {n: a for n, a in w.items() if n != 'final_norm_w'}
    xf, pullback = jax.vjp(trunk, trunk_w, x)
    loss, dxf, dfinal = loss_head(xf, final_w, target, "loss_head")
    gw, gx = pullback(dxf)
    gw['final_norm_w'] = dfinal
    return loss, gw, gx


def _adam_math(w, g, m, v):
    m = ADAM_B1 * m + (1.0 - ADAM_B1) * g
    v = ADAM_B2 * v + (1.0 - ADAM_B2) * (g * g)
    m_hat = m / (1.0 - ADAM_B1 ** ADAM_STEP)
    v_hat = v / (1.0 - ADAM_B2 ** ADAM_STEP)
    delta = -ADAM_LR * (m_hat / (jnp.sqrt(v_hat) + ADAM_EPS) + ADAM_WD * w)
    return delta, m, v


def _adamw_rows(g, w, m, v, name):
    r, c = w.shape
    tr = _pick(r, (256, 128, 64, 32, 16, 8))

    def body(g_ref, w_ref, m_ref, v_ref, d_out, m_out, v_out):
        delta, mm, vv = _adam_math(w_ref[...], g_ref[...], m_ref[...], v_ref[...])
        d_out[...] = delta
        m_out[...] = mm
        v_out[...] = vv

    blk = pl.BlockSpec((tr, c), lambda i: (i, 0))
    return pl.pallas_call(
        body, name=name, out_shape=[jax.ShapeDtypeStruct((r, c), F32)] * 3, grid=(r // tr,),
        in_specs=[blk] * 4, out_specs=[blk] * 3, compiler_params=_params("parallel"),
    )(g, w, m, v)


def _sum8(pieces, name):
    _, r, c = pieces.shape

    def body(p_ref, o_ref):
        g = p_ref[0]
        for j in range(1, N_DEV):
            g = g + p_ref[j]
        o_ref[...] = g

    return pl.pallas_call(
        body, name=name, out_shape=jax.ShapeDtypeStruct((r, c), F32),
        in_specs=[pl.BlockSpec(memory_space=pltpu.VMEM)], out_specs=pl.BlockSpec(memory_space=pltpu.VMEM),
    )(pieces)


def _adamw_plain(g, w, m, v, name):
    def body(g_ref, w_ref, m_ref, v_ref, d_out, m_out, v_out):
        delta, mm, vv = _adam_math(w_ref[...], g_ref[...], m_ref[...], v_ref[...])
        d_out[...] = delta
        m_out[...] = mm
        v_out[...] = vv

    vm = pl.BlockSpec(memory_space=pltpu.VMEM)
    return pl.pallas_call(
        body, name=name, out_shape=[jax.ShapeDtypeStruct(g.shape, F32)] * 3,
        in_specs=[vm] * 4, out_specs=[vm] * 3,
    )(g, w, m, v)


def _join8(parts, axis):
    t = jnp.moveaxis(parts, 0, axis)
    shp = t.shape
    return t.reshape(shp[:axis] + (shp[axis] * shp[axis + 1],) + shp[axis + 2:])


def _pack(arrs, lead, mult):
    flat = jnp.concatenate([a.reshape(a.shape[:lead] + (-1,)) for a in arrs], axis=-1)
    return _pad_rows(flat, mult)


def _unpack(buf, shapes, lead):
    flat = buf.reshape(buf.shape[:lead] + (-1,))
    out, off = [], 0
    for shp in shapes:
        n = math.prod(shp)
        out.append(flat[..., off:off + n].reshape(flat.shape[:lead] + tuple(shp)))
        off += n
    return out


def _own_shard(full, axis):
    size = full.shape[axis] // N_DEV
    return lax.dynamic_slice_in_dim(full, _my_index() * size, size, axis)


def kernel(x, mix_norm_w, attn_w_qkv, attn_w_o, ssm_w_in, ssm_conv_w, ssm_conv_b, ssm_dt_bias, ssm_a_log, ssm_d, ssm_norm_w, ssm_w_out, ffn_norm_w, ffn_w_up, ffn_conv_w, ffn_conv_b, ffn_w_down, final_norm_w, loss_target, m_mix_norm_w, m_attn_w_qkv, m_attn_w_o, m_ssm_w_in, m_ssm_conv_w, m_ssm_conv_b, m_ssm_dt_bias, m_ssm_a_log, m_ssm_d, m_ssm_norm_w, m_ssm_w_out, m_ffn_norm_w, m_ffn_w_up, m_ffn_conv_w, m_ffn_conv_b, m_ffn_w_down, m_final_norm_w, v_mix_norm_w, v_attn_w_qkv, v_attn_w_o, v_ssm_w_in, v_ssm_conv_w, v_ssm_conv_b, v_ssm_dt_bias, v_ssm_a_log, v_ssm_d, v_ssm_norm_w, v_ssm_w_out, v_ffn_norm_w, v_ffn_w_up, v_ffn_conv_w, v_ffn_conv_b, v_ffn_w_down, v_final_norm_w):
    args = dict(locals())
    w_sh = {n: args[n] for n in WEIGHT_NAMES}
    m_sh = {n: args["m_" + n] for n in WEIGHT_NAMES}
    v_sh = {n: args["v_" + n] for n in WEIGHT_NAMES}

    small_shapes = [w_sh[n].shape for n in SMALL_SHARDED]
    small = _exchange(_pack([w_sh[n] for n in SMALL_SHARDED], 0, 8), True, "gather_small")
    full = {n: w_sh[n] for n in SMALL if SHARD_AXIS[n] is None}
    for n, parts in zip(SMALL_SHARDED, _unpack(small, small_shapes, 1)):
        full[n] = _join8(parts, SHARD_AXIS[n])
    for n in BIG:
        full[n] = [w_sh[n][j] for j in range(w_sh[n].shape[0])]

    loss, gw, gx = local_step(full, x[0], loss_target[0])
    loss = lax.psum(loss, ("x", "y", "c"))
    for n in BIG:
        gw[n] = jnp.stack(gw[n])

    grads, deltas, new_m, new_v = {}, {}, {}, {}
    for n in BIG:
        shp = w_sh[n].shape
        two_d = (shp[0] * shp[1], shp[2])
        outs = _adamw_rows(*[t.reshape(two_d) for t in (gw[n], w_sh[n], m_sh[n], v_sh[n])], "adamw_" + n)
        grads[n] = gw[n]
        deltas[n], new_m[n], new_v[n] = [o.reshape(shp) for o in outs]

    small_full_shapes = [gw[n].shape for n in SMALL]
    gsmall = _exchange(_pack([gw[n] for n in SMALL], 0, 8), True, "gather_small_grads")
    gsmall = _unpack(_sum8(gsmall, "sum_small_grads"), small_full_shapes, 0)
    for n, g in zip(SMALL, gsmall):
        grads[n] = g if SHARD_AXIS[n] is None else _own_shard(g, SHARD_AXIS[n])
    shapes = [w_sh[n].shape for n in SMALL]
    outs = _adamw_plain(*[_pack([d[n] for n in SMALL], 0, 8) for d in (grads, w_sh, m_sh, v_sh)], "adamw_small")
    for d, buf in zip((deltas, new_m, new_v), outs):
        for n, a in zip(SMALL, _unpack(buf, shapes, 0)):
            d[n] = a

    return (loss, gx[None], *[grads[n] for n in WEIGHT_NAMES], *[deltas[n] for n in WEIGHT_NAMES],
            *[new_m[n] for n in WEIGHT_NAMES], *[new_v[n] for n in WEIGHT_NAMES])
```

```python
import functools
import math

import jax
import jax.numpy as jnp
from jax import lax
from jax.experimental import pallas as pl
from jax.experimental.pallas import tpu as pltpu

F32 = jnp.float32
BF16 = jnp.bfloat16
N_DEV = 8
MESH_ID = pl.DeviceIdType.MESH

D_MODEL = 1024
DEPTH = 4
ATTN_HEADS = 8
ATTN_HEAD_DIM = 128
ATTN_DILATIONS = (1, 4, 16)
ATTN_STEPS = (128, 128, 128)
N_ATTN_GROUPS = 3
ATTN_BLOCK = 128
ROPE_THETA = 500000.0
ROPE_DIM = 32
ATTN_OUT_WIDTH = 1024
SSM_D_INNER = 2048
SSM_HEAD_DIM = 64
SSM_HEADS = 32
SSM_STATE = 128
SSM_GROUPS = 8
SSM_CHUNK = 128
SSM_CONV_DIM = 4096
SSM_IN_WIDTH = 6176
SSM_IN_PAD = 6272
D_FF = 2816
NORM_EPS = 1e-5
ADAM_LR = 0.001
ADAM_B1 = 0.9
ADAM_B2 = 0.999
ADAM_EPS = 1e-08
ADAM_WD = 0.01
ADAM_STEP = 10

WEIGHT_NAMES = ['mix_norm_w', 'attn_w_qkv', 'attn_w_o', 'ssm_w_in', 'ssm_conv_w', 'ssm_conv_b', 'ssm_dt_bias',
                'ssm_a_log', 'ssm_d', 'ssm_norm_w', 'ssm_w_out', 'ffn_norm_w', 'ffn_w_up', 'ffn_conv_w',
                'ffn_conv_b', 'ffn_w_down', 'final_norm_w']
SHARD_AXIS = {'mix_norm_w': None, 'attn_w_qkv': 2, 'attn_w_o': 1, 'ssm_w_in': 2, 'ssm_conv_w': 2, 'ssm_conv_b': 1,
              'ssm_dt_bias': None, 'ssm_a_log': None, 'ssm_d': None, 'ssm_norm_w': 1, 'ssm_w_out': 1,
              'ffn_norm_w': None, 'ffn_w_up': 2, 'ffn_conv_w': 2, 'ffn_conv_b': None, 'ffn_w_down': 1,
              'final_norm_w': None}
BIG = ['attn_w_qkv', 'attn_w_o', 'ssm_w_in', 'ssm_w_out', 'ffn_w_up', 'ffn_w_down']
SMALL = [n for n in WEIGHT_NAMES if n not in BIG]
SMALL_SHARDED = [n for n in SMALL if SHARD_AXIS[n] is not None]
LANES = 1024


def _my_index():
    return 4 * lax.axis_index("x") + 2 * lax.axis_index("y") + lax.axis_index("c")


def _peer(k):
    x, y, c = lax.axis_index("x"), lax.axis_index("y"), lax.axis_index("c")
    return (x ^ ((k >> 2) & 1), y ^ ((k >> 1) & 1), c ^ (k & 1))


def _exchange(src, gather, name):
    def body(src_ref, out_ref, send_sems, recv_sems, local_sem):
        start, wait = _exchange_copies(src_ref, out_ref, send_sems, recv_sems, local_sem, gather)
        start()
        wait()

    return pl.pallas_call(
        body, name=name,
        out_shape=_exchange_out(src, gather),
        in_specs=[pl.BlockSpec(memory_space=pl.ANY)],
        out_specs=pl.BlockSpec(memory_space=pl.ANY),
        scratch_shapes=list(EXCHANGE_SEMS),
    )(src)


EXCHANGE_SEMS = (pltpu.SemaphoreType.DMA((N_DEV - 1,)), pltpu.SemaphoreType.DMA((N_DEV - 1,)),
                 pltpu.SemaphoreType.DMA)


def _exchange_out(src, gather):
    return jax.ShapeDtypeStruct((N_DEV,) + (src.shape if gather else src.shape[1:]), src.dtype)


def _exchange_copies(src_ref, out_ref, send_sems, recv_sems, local_sem, gather):
    me = _my_index()

    def piece(j):
        return src_ref if gather else src_ref.at[j]

    def remote(k, slab):
        return pltpu.make_async_remote_copy(
            src_ref=piece(me ^ k), dst_ref=out_ref.at[slab], send_sem=send_sems.at[k - 1],
            recv_sem=recv_sems.at[k - 1], device_id=_peer(k), device_id_type=MESH_ID)

    mine = pltpu.make_async_copy(piece(me), out_ref.at[me], local_sem)
    arrivals = {k: remote(k, me ^ k) for k in range(1, N_DEV)}
    if not gather:
        sends = [remote(k, me) for k in range(1, N_DEV)]

        def start():
            mine.start()
            for cp in sends:
                cp.start()

        def wait():
            for cp in arrivals.values():
                cp.wait_recv()
            for cp in sends:
                cp.wait_send()
            mine.wait()

        return start, wait

    far = (2, 4, 6)
    sends = [remote(k, me) for k in (1,) + far]
    passed_on = [pltpu.make_async_remote_copy(
        src_ref=out_ref.at[me ^ k], dst_ref=out_ref.at[me ^ k], send_sem=send_sems.at[k],
        recv_sem=recv_sems.at[k], device_id=_peer(1), device_id_type=MESH_ID) for k in far]

    def start():
        mine.start()
        for cp in sends:
            cp.start()

    def wait():
        for k, cp in zip(far, passed_on):
            arrivals[k].wait_recv()
            cp.start()
        for k in (1, 3, 5, 7):
            arrivals[k].wait_recv()
        for cp in sends + passed_on:
            cp.wait_send()
        mine.wait()

    return start, wait


def _pad_rows(flat, mult):
    n = flat.shape[-1]
    rows = -(-n // (LANES * mult)) * mult
    pad = rows * LANES - n
    flat = jnp.pad(flat, [(0, 0)] * (flat.ndim - 1) + [(0, pad)])
    return flat.reshape(flat.shape[:-1] + (rows, LANES))


MATMUL_VMEM_BUDGET = 48 * 1024 * 1024
MATMUL_TM = (1024, 1408, 512, 256, 128)
MATMUL_TN = (3072, 2816, 2048, 1536, 1408, 1024, 896, 512, 384, 256, 128)
MATMUL_TK = (3072, 2816, 2048, 1408, 1024, 896)
MATMUL_MIN_TK = 896


def _pick(n, cands):
    for c in cands:
        if n % c == 0:
            return c
    return n


def _matmul_tiles(m, n, k, a_bytes, b_bytes, out_bytes, has_res):
    tm = _pick(m, MATMUL_TM)

    def need(tn, tk):
        blocks = 2 * (tm * tk * a_bytes + tk * tn * b_bytes + tm * tn * out_bytes + has_res * tm * tn * 4)
        temps = tm * tn * 4 * (1 + (tk < k)) + (a_bytes > 2) * tm * tk * 2 + (b_bytes > 2) * tk * tn * 2
        return blocks + temps

    for tn in [c for c in MATMUL_TN if n % c == 0] + [n]:
        for tk in [k] + [c for c in MATMUL_TK if k % c == 0 and MATMUL_MIN_TK <= c < k]:
            if need(tn, tk) <= MATMUL_VMEM_BUDGET:
                return tm, tn, tk
    raise ValueError(f"no matmul tiling fits VMEM for {(m, n, k)}")


def _matmul(a, b, ta, tb, out_dtype, name, rider=None, res=None, rope=None):
    (m, k) = (a.shape[1], a.shape[0]) if ta else a.shape
    (k2, n) = (b.shape[1], b.shape[0]) if tb else b.shape
    assert k == k2, (a.shape, b.shape, ta, tb)
    tm, tn, tk = _matmul_tiles(m, n, k, a.dtype.itemsize, b.dtype.itemsize, jnp.dtype(out_dtype).itemsize,
                               res is not None)
    nk = k // tk
    dims = (((0 if ta else 1,), (1 if tb else 0,)), ((), ()))

    grid = (n // tn, m // tm, nk)

    n_in = 2 + (res is not None) + (rope is not None) + (rider is not None)
    if rope is not None:
        assert res is None and tn == n == QKV_GROUP, (tn, n)

    def body(*refs):
        ins, rest = refs[:n_in], refs[n_in:]
        a_ref, b_ref = ins[:2]
        res_ref = ins[2] if res is not None else None
        rope_ref = ins[2] if rope is not None else None
        o_ref, scratch = rest[0], rest[1:]
        if rider is not None:
            start, wait = _exchange_copies(ins[-1], rest[1], *scratch[-3:], rider[1])
            scratch = scratch[1:-3]
            at = [pl.program_id(ax) for ax in range(3)]
            pl.when(functools.reduce(jnp.logical_and, [p == 0 for p in at]))(start)
        part = lax.dot_general(a_ref[...].astype(BF16), b_ref[...].astype(BF16), dims,
                               preferred_element_type=F32)

        def finish(total):
            if rope_ref is not None:
                for c0 in range(0, tn, ATTN_HEAD_DIM):
                    head = total[:, c0:c0 + ATTN_HEAD_DIM]
                    if c0 < 2 * ATTN_OUT_WIDTH:
                        head = _rope(head, rope_ref[...], 1.0)
                    o_ref[:, c0:c0 + ATTN_HEAD_DIM] = head.astype(o_ref.dtype)
                return
            if res_ref is not None:
                total = total + res_ref[...]
            o_ref[...] = total.astype(o_ref.dtype)

        if nk == 1:
            finish(part)
        else:
            acc_ref, = scratch
            kk = pl.program_id(2)

            @pl.when(kk == 0)
            def _():
                acc_ref[...] = part

            @pl.when(kk > 0)
            def _():
                acc_ref[...] += part

            @pl.when(kk == nk - 1)
            def _():
                finish(acc_ref[...])
        if rider is not None:
            pl.when(functools.reduce(jnp.logical_and, [p == g - 1 for p, g in zip(at, grid)]))(wait)

    a_spec = (pl.BlockSpec((tk, tm), lambda j, i, kk: (kk, i)) if ta
              else pl.BlockSpec((tm, tk), lambda j, i, kk: (i, kk)))
    b_spec = (pl.BlockSpec((tn, tk), lambda j, i, kk: (j, kk)) if tb
              else pl.BlockSpec((tk, tn), lambda j, i, kk: (kk, j)))
    any_spec = pl.BlockSpec(memory_space=pl.ANY)
    tile_spec = pl.BlockSpec((tm, tn), lambda j, i, kk: (i, j))
    in_specs, operands = [a_spec, b_spec], [a, b]
    out_shape, out_specs = [jax.ShapeDtypeStruct((m, n), out_dtype)], [tile_spec]
    scratch = [] if nk == 1 else [pltpu.VMEM((tm, tn), F32)]
    sem = ("parallel", "parallel", "arbitrary")
    if res is not None:
        in_specs.append(tile_spec)
        operands.append(res)
    if rope is not None:
        in_specs.append(pl.BlockSpec((tm, rope.shape[1]), lambda j, i, kk: (i, 0)))
        operands.append(rope)
    if rider is not None:
        in_specs.append(any_spec)
        operands.append(rider[0])
        out_shape.append(_exchange_out(*rider))
        out_specs.append(any_spec)
        scratch += list(EXCHANGE_SEMS)
        sem = ("arbitrary",) * 3
    outs = pl.pallas_call(
        body, name=name, out_shape=out_shape, grid=grid, in_specs=in_specs, out_specs=out_specs,
        scratch_shapes=scratch, compiler_params=_params(*sem),
    )(*operands)
    return outs[0] if rider is None else tuple(outs)


@functools.partial(jax.custom_vjp, nondiff_argnums=(3, 4))
def linear(a, w, res, out_dtype, name):
    return _matmul(a, w, False, False, out_dtype, name + "_fwd", res=res)


def _linear_fwd(a, w, res, out_dtype, name):
    return _matmul(a, w, False, False, out_dtype, name + "_fwd", res=res), (a, w, res is not None)


def _linear_bwd(out_dtype, name, saved, dy):
    a, w, has_res = saved
    da = _matmul(dy, w, False, True, a.dtype, name + "_da")
    dw = _matmul(a, dy, True, False, w.dtype, name + "_dw")
    return da, dw, (dy if has_res else None)


linear.defvjp(_linear_fwd, _linear_bwd)


def _sum_pieces(pieces, name):
    shape = pieces.shape[1:]
    c = shape[-1]
    r = math.prod(shape[:-1])
    tr = _pick(r, (512, 256, 128, 64, 32, 16, 8))

    def body(p_ref, o_ref):
        g = p_ref[0].astype(F32)
        for j in range(1, N_DEV):
            g = g + p_ref[j].astype(F32)
        o_ref[...] = g

    return pl.pallas_call(
        body, name=name, out_shape=jax.ShapeDtypeStruct((r, c), F32), grid=(r // tr,),
        in_specs=[pl.BlockSpec((N_DEV, tr, c), lambda i: (0, i, 0))],
        out_specs=pl.BlockSpec((tr, c), lambda i: (i, 0)),
        compiler_params=_params("parallel"),
    )(pieces.reshape(N_DEV, r, c)).reshape(shape)


@functools.partial(jax.custom_vjp, nondiff_argnums=(5, 6))
def linear_x(a, w, res, rope, shard, out_dtype, name):
    return _matmul(a, w, False, False, out_dtype, name + "_fwd", rider=(shard.astype(BF16), True), res=res,
                   rope=rope)


def _linear_x_fwd(a, w, res, rope, shard, out_dtype, name):
    out = _matmul(a, w, False, False, out_dtype, name + "_fwd", rider=(shard.astype(BF16), True), res=res,
                  rope=rope)
    return out, (a, w, res is not None, rope)


def _linear_x_bwd(out_dtype, name, saved, cts):
    a, w, has_res, rope = saved
    dy, dparts = cts
    half = dparts.shape[1] // 2
    da, moved_a = _matmul(dy, w, False, True, a.dtype, name + "_da", rider=(dparts[:, :half], False))
    dw, moved_b = _matmul(a, dy, True, False, w.dtype, name + "_dw", rider=(dparts[:, half:], False))
    dshard = jnp.concatenate([_sum_pieces(moved_a, name + "_sum_a"), _sum_pieces(moved_b, name + "_sum_b")])
    return da, dw, (dy if has_res else None), (None if rope is None else jnp.zeros_like(rope)), dshard


linear_x.defvjp(_linear_x_fwd, _linear_x_bwd)

FIRST_ROW_SPLIT = (0, 384, 704, 1024)


def _qkv_first_fwd(hs, shard, ropes, carried, name):
    parts = _exchange(shard.astype(BF16), True, name + "_gather")
    wfull = _join8(parts, 1)
    ys, moved = [], []
    for g, (h, rope, carry) in enumerate(zip(hs, ropes, carried)):
        y, m = _matmul(h, wfull[:, g * QKV_GROUP:(g + 1) * QKV_GROUP], False, False, BF16, f"{name}{g}_fwd",
                       rider=(carry.astype(BF16), True), rope=rope)
        ys.append(y)
        moved.append(m)
    return (tuple(ys), tuple(moved)), (hs, wfull, ropes)


@functools.partial(jax.custom_vjp, nondiff_argnums=(4,))
def qkv_first(hs, shard, ropes, carried, name):
    return _qkv_first_fwd(hs, shard, ropes, carried, name)[0]


def _qkv_first_vjp_fwd(hs, shard, ropes, carried, name):
    return _qkv_first_fwd(hs, shard, ropes, carried, name)


def _qkv_first_vjp_bwd(name, saved, cts):
    hs, wfull, ropes = saved
    dys, dmoved = cts
    dws, dcarried = [], []
    for g in range(len(hs)):
        dw, back = _matmul(hs[g], dys[g], True, False, BF16, f"{name}{g}_dw", rider=(dmoved[g], False))
        dws.append(dw)
        dcarried.append(_sum_pieces(back, f"{name}{g}_carried_sum"))
    w_shape = wfull.shape
    dparts = jnp.moveaxis(jnp.concatenate(dws, axis=1).reshape(w_shape[0], N_DEV, w_shape[1] // N_DEV), 1, 0)
    das, dshard = [], []
    for g in range(len(hs)):
        rows = slice(FIRST_ROW_SPLIT[g], FIRST_ROW_SPLIT[g + 1])
        da, got = _matmul(dys[g], wfull[:, g * QKV_GROUP:(g + 1) * QKV_GROUP], False, True, BF16, f"{name}{g}_da",
                          rider=(dparts[:, rows], False))
        das.append(da)
        dshard.append(_sum_pieces(got, f"{name}{g}_own_sum"))
    return tuple(das), jnp.concatenate(dshard), tuple(jnp.zeros_like(r) for r in ropes), tuple(dcarried)


qkv_first.defvjp(_qkv_first_vjp_fwd, _qkv_first_vjp_bwd)


VMEM_LIMIT = 56 * 1024 * 1024
SUBLANES = 8


def _params(*sem):
    return pltpu.CompilerParams(dimension_semantics=sem, vmem_limit_bytes=VMEM_LIMIT)


def _sigmoid(x):
    return 0.5 * jnp.tanh(0.5 * x) + 0.5


def _rstd(xv):
    return lax.rsqrt(jnp.mean(xv * xv, axis=-1, keepdims=True) + NORM_EPS)


def _accumulate(ref, part, first):
    @pl.when(first)
    def _():
        ref[...] = part

    @pl.when(jnp.logical_not(first))
    def _():
        ref[...] += part


def _norm_fwd(x, w, name):
    s, d = x.shape
    tm = _pick(s, (512, 256, 128))

    def body(x_ref, w_ref, h_ref):
        xv = x_ref[...]
        h_ref[...] = (xv * _rstd(xv) * w_ref[...]).astype(h_ref.dtype)

    return pl.pallas_call(
        body, name=name, out_shape=jax.ShapeDtypeStruct((s, d), BF16), grid=(s // tm,),
        in_specs=[pl.BlockSpec((tm, d), lambda i: (i, 0)), pl.BlockSpec((1, d), lambda i: (0, 0))],
        out_specs=pl.BlockSpec((tm, d), lambda i: (i, 0)), compiler_params=_params("parallel"),
    )(x, w.reshape(1, d))


def _norm_bwd(x, w, dh, dskip, name):
    s, d = x.shape
    tm = _pick(s, (512, 256, 128))

    def body(x_ref, w_ref, dh_ref, ds_ref, dx_ref, dw_ref):
        xv = x_ref[...]
        r = _rstd(xv)
        y = xv * r
        dhv = dh_ref[...].astype(F32)
        dy = dhv * w_ref[...]
        dx_ref[...] = ds_ref[...] + r * (dy - y * jnp.mean(dy * y, axis=-1, keepdims=True))
        _accumulate(dw_ref, jnp.sum(dhv * y, axis=0, keepdims=True), pl.program_id(0) == 0)

    row = pl.BlockSpec((tm, d), lambda i: (i, 0))
    vec = pl.BlockSpec((1, d), lambda i: (0, 0))
    dx, dw = pl.pallas_call(
        body, name=name,
        out_shape=[jax.ShapeDtypeStruct((s, d), F32), jax.ShapeDtypeStruct((1, d), F32)], grid=(s // tm,),
        in_specs=[row, vec, row, row], out_specs=[row, vec], compiler_params=_params("arbitrary"),
    )(x, w.reshape(1, d), dh, dskip)
    return dx, dw.reshape(d)


@functools.partial(jax.custom_vjp, nondiff_argnums=(2,))
def norm(x, w, name):
    return x, _norm_fwd(x, w, name + "_fwd")


def _norm_vjp_fwd(x, w, name):
    return (x, _norm_fwd(x, w, name + "_fwd")), (x, w)


def _norm_vjp_bwd(name, saved, cts):
    x, w = saved
    dskip, dh = cts
    return _norm_bwd(x, w, dh, dskip, name + "_bwd")


norm.defvjp(_norm_vjp_fwd, _norm_vjp_bwd)


def loss_head(x, w, target, name):
    s, d = x.shape
    tm = _pick(s, (512, 256, 128))

    def body(x_ref, w_ref, t_ref, loss_ref, dx_ref, dw_ref):
        first = pl.program_id(0) == 0
        xv = x_ref[...]
        r = _rstd(xv)
        y = xv * r
        err = y * w_ref[...] - t_ref[...]
        part = 0.5 * jnp.sum(jnp.sum(err * err, axis=-1, keepdims=True), axis=0, keepdims=True) / d
        _accumulate(loss_ref, jnp.broadcast_to(part, loss_ref.shape), first)
        dout = err / d
        dy = dout * w_ref[...]
        dx_ref[...] = r * (dy - y * jnp.mean(dy * y, axis=-1, keepdims=True))
        _accumulate(dw_ref, jnp.sum(dout * y, axis=0, keepdims=True), first)

    row = pl.BlockSpec((tm, d), lambda i: (i, 0))
    vec = pl.BlockSpec((1, d), lambda i: (0, 0))
    loss, dx, dw = pl.pallas_call(
        body, name=name,
        out_shape=[jax.ShapeDtypeStruct((1, 128), F32), jax.ShapeDtypeStruct((s, d), F32),
                   jax.ShapeDtypeStruct((1, d), F32)],
        grid=(s // tm,), in_specs=[row, vec, row],
        out_specs=[pl.BlockSpec((1, 128), lambda i: (0, 0)), row, vec],
        compiler_params=_params("arbitrary"),
    )(x, w.reshape(1, d), target)
    return loss[0, 0], dx, dw.reshape(d)


FFN_ROWS = 256
FFN_COLS = 256


def _shift_down(cur, halo, k):
    out = pltpu.roll(cur, k, axis=0)
    row = lax.broadcasted_iota(jnp.int32, halo.shape, 0)
    top = out[0:SUBLANES]
    for j in range(k):
        top = jnp.where(row == j, halo[SUBLANES - k + j:SUBLANES - k + j + 1, :], top)
    return jnp.concatenate([top, out[SUBLANES:]], axis=0)


def _shift_up(cur, nxt, k):
    n = cur.shape[0]
    out = pltpu.roll(cur, n - k, axis=0)
    row = lax.broadcasted_iota(jnp.int32, nxt.shape, 0)
    bottom = out[n - SUBLANES:]
    for j in range(k):
        bottom = jnp.where(row == SUBLANES - k + j, nxt[j:j + 1, :], bottom)
    return jnp.concatenate([out[:n - SUBLANES], bottom], axis=0)


def _conv_taps(cur, halo, ntaps):
    return [_shift_down(cur, halo, ntaps - 1 - k) if k < ntaps - 1 else cur for k in range(ntaps)]


def _pad_taps(conv_w):
    return jnp.pad(conv_w, ((0, SUBLANES - conv_w.shape[0]), (0, 0)))


def _ffn_mid_fwd(u0, conv_w, conv_b, name):
    s, width = u0.shape
    half = width // 2
    tm = _pick(s, (FFN_ROWS, 128))
    per = tm // SUBLANES

    def body(w_ref, b_ref, u_ref, halo_ref, a_ref, pre_ref):
        keep = pl.program_id(0) > 0
        for c0 in range(0, half, FFN_COLS):
            vals = []
            for base in (c0, half + c0):
                cols = slice(base, base + FFN_COLS)
                halo = jnp.where(keep, halo_ref[:, cols], 0.0)
                taps = _conv_taps(u_ref[:, cols], halo, 3)
                vals.append(sum(w_ref[k:k + 1, cols] * taps[k] for k in range(3)) + b_ref[:, cols])
                pre_ref[:, cols] = vals[-1].astype(pre_ref.dtype)
            gate, up = vals
            a_ref[:, c0:c0 + FFN_COLS] = (gate * _sigmoid(gate) * up).astype(a_ref.dtype)

    return pl.pallas_call(
        body, name=name,
        out_shape=[jax.ShapeDtypeStruct((s, half), BF16), jax.ShapeDtypeStruct((s, width), BF16)], grid=(s // tm,),
        in_specs=[pl.BlockSpec((SUBLANES, width), lambda i: (0, 0)), pl.BlockSpec((1, width), lambda i: (0, 0)),
                  pl.BlockSpec((tm, width), lambda i: (i, 0)),
                  pl.BlockSpec((SUBLANES, width), lambda i: (jnp.maximum(i * per - 1, 0), 0))],
        out_specs=[pl.BlockSpec((tm, half), lambda i: (i, 0)), pl.BlockSpec((tm, width), lambda i: (i, 0))],
        compiler_params=_params("parallel"),
    )(_pad_taps(conv_w), conv_b.reshape(1, width), u0, u0)


def _ffn_mid_bwd(u0, pre, conv_w, da, name):
    s, width = u0.shape
    half = width // 2
    tm = _pick(s, (FFN_ROWS, 128))
    nt = s // tm

    def body(w_ref, u_ref, pre_ref, da_ref, du0_ref, dw_ref, db_ref, carry_ref):
        first = pl.program_id(0) == 0
        for c0 in range(0, half, FFN_COLS):
            dav = da_ref[:, c0:c0 + FFN_COLS].astype(F32)
            gate = pre_ref[:, c0:c0 + FFN_COLS].astype(F32)
            up = pre_ref[:, half + c0:half + c0 + FFN_COLS].astype(F32)
            sig = _sigmoid(gate)
            dus = [dav * up * sig * (1.0 + gate * (1.0 - sig)), dav * gate * sig]
            for base, du in zip((c0, half + c0), dus):
                cols = slice(base, base + FFN_COLS)
                nxt = jnp.where(first, 0.0, carry_ref[:, cols])
                ahead = [_shift_up(du, nxt, 2), _shift_up(du, nxt, 1), du]
                du0 = sum(w_ref[k:k + 1, cols] * ahead[k] for k in range(3))
                du0_ref[:, cols] = du0.astype(du0_ref.dtype)
                carry_ref[:, cols] = du[0:SUBLANES, :]
                x = u_ref[:, cols]
                dwp = jnp.concatenate([jnp.sum(ahead[k] * x, axis=0, keepdims=True) for k in range(3)]
                                      + [jnp.zeros((SUBLANES - 3, FFN_COLS), F32)], axis=0)
                dbp = jnp.sum(du, axis=0, keepdims=True)

                @pl.when(first)
                def _():
                    dw_ref[:, cols] = dwp
                    db_ref[:, cols] = dbp

                @pl.when(jnp.logical_not(first))
                def _():
                    dw_ref[:, cols] += dwp
                    db_ref[:, cols] += dbp

    rev = lambda i: nt - 1 - i
    du0, dw, db = pl.pallas_call(
        body, name=name,
        out_shape=[jax.ShapeDtypeStruct((s, width), BF16), jax.ShapeDtypeStruct((SUBLANES, width), F32),
                   jax.ShapeDtypeStruct((1, width), F32)],
        grid=(nt,),
        in_specs=[pl.BlockSpec((SUBLANES, width), lambda i: (0, 0)),
                  pl.BlockSpec((tm, width), lambda i: (rev(i), 0)), pl.BlockSpec((tm, width), lambda i: (rev(i), 0)),
                  pl.BlockSpec((tm, half), lambda i: (rev(i), 0))],
        out_specs=[pl.BlockSpec((tm, width), lambda i: (rev(i), 0)),
                   pl.BlockSpec((SUBLANES, width), lambda i: (0, 0)), pl.BlockSpec((1, width), lambda i: (0, 0))],
        scratch_shapes=[pltpu.VMEM((SUBLANES, width), F32)],
        compiler_params=_params("arbitrary"),
    )(_pad_taps(conv_w), u0, pre, da)
    return du0, dw[:3], db.reshape(width)


@functools.partial(jax.custom_vjp, nondiff_argnums=(3,))
def ffn_mid(u0, conv_w, conv_b, name):
    return _ffn_mid_fwd(u0, conv_w, conv_b, name + "_fwd")[0]


def _ffn_mid_vjp_fwd(u0, conv_w, conv_b, name):
    a, pre = _ffn_mid_fwd(u0, conv_w, conv_b, name + "_fwd")
    return a, (u0, pre, conv_w)


def _ffn_mid_vjp_bwd(name, res, da):
    u0, pre, conv_w = res
    return _ffn_mid_bwd(u0, pre, conv_w, da, name + "_bwd")


ffn_mid.defvjp(_ffn_mid_vjp_fwd, _ffn_mid_vjp_bwd)


NEG = -1e30
HEAD_SLICES = [slice(hh * ATTN_HEAD_DIM, (hh + 1) * ATTN_HEAD_DIM) for hh in range(ATTN_HEADS)]
ATTN_SCALE = ATTN_HEAD_DIM ** -0.5
QKV_GROUP = 3 * ATTN_OUT_WIDTH


def rope_table(seq, d):
    pos = (jnp.arange(seq // d, dtype=jnp.int32)[None, :] * d + jnp.arange(d, dtype=jnp.int32)[:, None])
    inv_freq = ROPE_THETA ** (-jnp.arange(0, ROPE_DIM, 2, dtype=F32) / ROPE_DIM)
    ang = pos.reshape(seq).astype(F32)[:, None] * inv_freq[None, :]
    cos, sin = jnp.cos(ang), jnp.sin(ang)
    half = ROPE_DIM // 2
    ones = jnp.ones((seq, ATTN_HEAD_DIM - ROPE_DIM), F32)
    zero = lambda n: jnp.zeros((seq, n), F32)
    return jnp.concatenate([cos, cos, ones, -sin, zero(ATTN_HEAD_DIM - half),
                            zero(half), sin, zero(ATTN_HEAD_DIM - ROPE_DIM)], axis=1)


def _rope(t, tab, sign):
    half = ROPE_DIM // 2
    return t * tab[:, 0:128] + sign * (pltpu.roll(t, ATTN_HEAD_DIM - half, axis=1) * tab[:, 128:256]
                                       + pltpu.roll(t, half, axis=1) * tab[:, 256:384])


def _to_dilated(a, d):
    s = a.shape[0]
    return a if d == 1 else a.reshape(s // d, d, -1).transpose(1, 0, 2).reshape(s, -1)


def _from_dilated(a, d):
    s = a.shape[0]
    return a if d == 1 else a.reshape(d, s // d, -1).transpose(1, 0, 2).reshape(s, -1)


def _dot_nt(a, b):
    return lax.dot_general(a, b, (((1,), (1,)), ((), ())), preferred_element_type=F32)


def _dot_tn(a, b):
    return lax.dot_general(a, b, (((0,), (0,)), ((), ())), preferred_element_type=F32)


def _dot(a, b):
    return jnp.dot(a, b, preferred_element_type=F32)


def _window_mask(has_prev):
    ii = lax.broadcasted_iota(jnp.int32, (ATTN_BLOCK, 2 * ATTN_BLOCK), 0)
    jj = lax.broadcasted_iota(jnp.int32, (ATTN_BLOCK, 2 * ATTN_BLOCK), 1)
    in_window = jnp.logical_and(jj >= ii, jj <= ii + ATTN_BLOCK)
    return jnp.logical_and(in_window, jnp.logical_or(jj >= ATTN_BLOCK, has_prev))


def _both(prev_ref, cur_ref, hs):
    return jnp.concatenate([prev_ref[:, hs], cur_ref[:, hs]], axis=0)


def _attn_group_fwd(qkv, d, name):
    s = qkv.shape[0]
    nb = s // d // ATTN_BLOCK

    def body(q_ref, kc_ref, kp_ref, vc_ref, vp_ref, o_ref, lse_ref):
        mask = _window_mask(pl.program_id(1) > 0)
        lane = lax.broadcasted_iota(jnp.int32, (ATTN_BLOCK, 128), 1)
        lse_tile = jnp.zeros((ATTN_BLOCK, 128), F32)
        for hh, hs in enumerate(HEAD_SLICES):
            sc = jnp.where(mask, _dot_nt(q_ref[:, hs], _both(kp_ref, kc_ref, hs)) * ATTN_SCALE, NEG)
            m = jnp.max(sc, axis=1, keepdims=True)
            p = jnp.exp(sc - m)
            den = jnp.sum(p, axis=1, keepdims=True)
            o_ref[:, hs] = _dot(p.astype(BF16), _both(vp_ref, vc_ref, hs)) / den
            lse_tile = jnp.where(lane == hh, m + jnp.log(den), lse_tile)
        lse_ref[...] = lse_tile

    cur = lambda t: pl.BlockSpec((ATTN_BLOCK, ATTN_OUT_WIDTH), lambda r, n: (r * nb + n, t))
    prv = lambda t: pl.BlockSpec((ATTN_BLOCK, ATTN_OUT_WIDTH), lambda r, n: (r * nb + jnp.maximum(n - 1, 0), t))
    return pl.pallas_call(
        body, name=name,
        out_shape=[jax.ShapeDtypeStruct((s, ATTN_OUT_WIDTH), F32), jax.ShapeDtypeStruct((s, 128), F32)],
        grid=(d, nb), in_specs=[cur(0), cur(1), prv(1), cur(2), prv(2)],
        out_specs=[pl.BlockSpec((ATTN_BLOCK, ATTN_OUT_WIDTH), lambda r, n: (r * nb + n, 0)),
                   pl.BlockSpec((ATTN_BLOCK, 128), lambda r, n: (r * nb + n, 0))],
        compiler_params=_params("parallel", "parallel"),
    )(qkv, qkv, qkv, qkv, qkv)


def _attn_combine(os_, lses, name):
    s = os_[0].shape[0]
    tm = _pick(s, (256, 128))
    ng = len(os_)

    def body(*refs):
        o_refs, l_refs, (o_ref, lse_ref) = refs[:ng], refs[ng:2 * ng], refs[2 * ng:]
        lane = lax.broadcasted_iota(jnp.int32, (tm, 128), 1)
        lse_tile = jnp.zeros((tm, 128), F32)
        for hh, hs in enumerate(HEAD_SLICES):
            ls = [l_ref[:, hh:hh + 1] for l_ref in l_refs]
            m = functools.reduce(jnp.maximum, ls)
            ws = [jnp.exp(l - m) for l in ls]
            tot = functools.reduce(lambda a, b: a + b, ws)
            acc = functools.reduce(lambda a, b: a + b, [o_r[:, hs] * w for o_r, w in zip(o_refs, ws)])
            o_ref[:, hs] = (acc / tot).astype(o_ref.dtype)
            lse_tile = jnp.where(lane == hh, m + jnp.log(tot), lse_tile)
        lse_ref[...] = lse_tile

    wide = pl.BlockSpec((tm, ATTN_OUT_WIDTH), lambda i: (i, 0))
    thin = pl.BlockSpec((tm, 128), lambda i: (i, 0))
    return pl.pallas_call(
        body, name=name,
        out_shape=[jax.ShapeDtypeStruct((s, ATTN_OUT_WIDTH), BF16), jax.ShapeDtypeStruct((s, 128), F32)],
        grid=(s // tm,), in_specs=[wide] * ng + [thin] * ng, out_specs=[wide, thin],
        compiler_params=_params("parallel"),
    )(*os_, *lses)


def _attn_delta(do, o, name):
    s = do.shape[0]
    tm = _pick(s, (256, 128))

    def body(do_ref, o_ref, out_ref):
        lane = lax.broadcasted_iota(jnp.int32, (tm, 128), 1)
        tile = jnp.zeros((tm, 128), F32)
        for hh, hs in enumerate(HEAD_SLICES):
            prod = do_ref[:, hs].astype(F32) * o_ref[:, hs].astype(F32)
            tile = jnp.where(lane == hh, jnp.sum(prod, axis=1, keepdims=True), tile)
        out_ref[...] = tile

    wide = pl.BlockSpec((tm, ATTN_OUT_WIDTH), lambda i: (i, 0))
    return pl.pallas_call(
        body, name=name, out_shape=jax.ShapeDtypeStruct((s, 128), F32), grid=(s // tm,),
        in_specs=[wide, wide], out_specs=pl.BlockSpec((tm, 128), lambda i: (i, 0)),
        compiler_params=_params("parallel"),
    )(do, o)


def _attn_group_bwd(qkv, do, lse, delta, tab, d, name):
    s = qkv.shape[0]
    nb = s // d // ATTN_BLOCK
    wide = ATTN_OUT_WIDTH

    def body(q_ref, qn_ref, k_ref, v_ref, do_ref, don_ref, lse_ref, lsen_ref, dl_ref, dln_ref, tab_ref,
             out_ref, carry_ref):
        n = pl.program_id(1)

        @pl.when(n == 0)
        def _():
            carry_ref[...] = jnp.zeros_like(carry_ref)

        rows = lax.broadcasted_iota(jnp.int32, (2 * ATTN_BLOCK, ATTN_BLOCK), 0)
        keys = lax.broadcasted_iota(jnp.int32, (2 * ATTN_BLOCK, ATTN_BLOCK), 1)
        own = jnp.logical_and(rows < ATTN_BLOCK, keys <= rows)
        nxt = jnp.logical_and(rows >= ATTN_BLOCK, jnp.logical_and(keys >= rows - ATTN_BLOCK, n + 1 < nb))
        mask = jnp.logical_or(own, nxt)
        both = lambda a_ref, b_ref, cols: jnp.concatenate([a_ref[:, cols], b_ref[:, cols]], axis=0)
        for hh, hs in enumerate(HEAD_SLICES):
            one = slice(hh, hh + 1)
            q2, do2 = both(q_ref, qn_ref, hs), both(do_ref, don_ref, hs)
            kh, vh = k_ref[:, hs], v_ref[:, hs]
            sc = jnp.where(mask, _dot_nt(q2, kh) * ATTN_SCALE, NEG)
            p = jnp.exp(sc - both(lse_ref, lsen_ref, one))
            ds = (p * (_dot_nt(do2, vh) - both(dl_ref, dln_ref, one)) * ATTN_SCALE).astype(BF16)
            dq2 = _dot(ds, kh)
            dq = carry_ref[:, hs] + dq2[:ATTN_BLOCK]
            carry_ref[:, hs] = dq2[ATTN_BLOCK:]
            out_ref[:, hs] = _rope(dq, tab_ref[...], -1.0).astype(out_ref.dtype)
            out_ref[:, wide + hh * ATTN_HEAD_DIM:wide + (hh + 1) * ATTN_HEAD_DIM] = _rope(
                _dot_tn(ds, q2), tab_ref[...], -1.0).astype(out_ref.dtype)
            out_ref[:, 2 * wide + hh * ATTN_HEAD_DIM:2 * wide + (hh + 1) * ATTN_HEAD_DIM] = _dot_tn(
                p.astype(BF16), do2).astype(out_ref.dtype)

    def spec(width, row, col):
        return pl.BlockSpec((ATTN_BLOCK, width), lambda r, n: (r * nb + row(n), col))

    cur = lambda n: n
    nxt_block = lambda n: jnp.minimum(n + 1, nb - 1)
    return pl.pallas_call(
        body, name=name, out_shape=jax.ShapeDtypeStruct((s, QKV_GROUP), BF16), grid=(d, nb),
        in_specs=[spec(wide, cur, 0), spec(wide, nxt_block, 0), spec(wide, cur, 1), spec(wide, cur, 2),
                  spec(wide, cur, 0), spec(wide, nxt_block, 0), spec(128, cur, 0), spec(128, nxt_block, 0),
                  spec(128, cur, 0), spec(128, nxt_block, 0), spec(384, cur, 0)],
        out_specs=spec(QKV_GROUP, cur, 0),
        scratch_shapes=[pltpu.VMEM((ATTN_BLOCK, wide), F32)],
        compiler_params=_params("parallel", "arbitrary"),
    )(qkv, qkv, qkv, qkv, do, do, lse, lse, delta, delta, tab)


def _attn_core_fwd(qkvs, name):
    os_, lses = [], []
    for g, (qkv, d) in enumerate(zip(qkvs, ATTN_DILATIONS)):
        o_g, lse_g = _attn_group_fwd(qkv, d, f"{name}_fwd{g}")
        os_.append(_from_dilated(o_g, d))
        lses.append(_from_dilated(lse_g, d))
    o, lse = _attn_combine(os_, lses, name + "_combine")
    return o, (tuple(qkvs), o, lse)


@functools.partial(jax.custom_vjp, nondiff_argnums=(1,))
def attn_core(qkvs, name):
    return _attn_core_fwd(qkvs, name)[0]


def _attn_core_vjp_fwd(qkvs, name):
    return _attn_core_fwd(qkvs, name)


def _attn_core_vjp_bwd(name, res, do):
    rot, o, lse = res
    delta = _attn_delta(do, o, name + "_delta")
    out = []
    for g, (qkv, d) in enumerate(zip(rot, ATTN_DILATIONS)):
        out.append(_attn_group_bwd(qkv, _to_dilated(do, d), _to_dilated(lse, d), _to_dilated(delta, d),
                                   rope_table(o.shape[0], d), d, f"{name}_bwd{g}"))
    return (tuple(out),)


attn_core.defvjp(_attn_core_vjp_fwd, _attn_core_vjp_bwd)


def attention_mixer_p(x, h, lin, j, carries, tag):
    qkvs = tuple(lin(_to_dilated(h, d), ('attn_w_qkv', j), BF16, f"{tag}_qkv{g}",
                     cols=slice(g * QKV_GROUP, (g + 1) * QKV_GROUP), carry=carries[g],
                     rope=rope_table(h.shape[0], d))
                 for g, d in enumerate(ATTN_DILATIONS))
    return lin(attn_core(qkvs, tag), ('attn_w_o', j), F32, tag + "_o", res=x)


def first_attention_mixer(x, h, lin, w, ready, carries, tag):
    s = h.shape[0]
    qkvs, moved = qkv_first(tuple(_to_dilated(h, d) for d in ATTN_DILATIONS), w['attn_w_qkv'][0],
                            tuple(rope_table(s, d) for d in ATTN_DILATIONS),
                            tuple(w[n][j] for n, j in carries), tag + "_qkv")
    for (n, j), parts in zip(carries, moved):
        ready[(n, j)] = _full_weight(n, parts)
    return lin(attn_core(qkvs, tag), ('attn_w_o', 0), F32, tag + "_o", res=x)


SSM_CONV_TAPS = 4
SSM_COL_BLOCK = 2048
SSM_PAIRS = SSM_HEADS // 2
SSM_DT_BLOCK = (SSM_D_INNER + SSM_CONV_DIM) // 128


def _ssm_conv_fwd(zx, conv_w, conv_b, name):
    s = zx.shape[0]
    tm = _pick(s, (256, 128))
    per = tm // SUBLANES
    ncb = SSM_CONV_DIM // SSM_COL_BLOCK

    def body(w_ref, b_ref, x_ref, halo_ref, o_ref, pre_ref):
        keep = pl.program_id(1) > 0
        for c0 in range(0, SSM_COL_BLOCK, FFN_COLS):
            cols = slice(c0, c0 + FFN_COLS)
            halo = jnp.where(keep, halo_ref[:, cols], 0.0)
            taps = _conv_taps(x_ref[:, cols], halo, SSM_CONV_TAPS)
            pre = sum(w_ref[k:k + 1, cols] * taps[k] for k in range(SSM_CONV_TAPS)) + b_ref[:, cols]
            pre_ref[:, cols] = pre.astype(pre_ref.dtype)
            o_ref[:, cols] = pre * _sigmoid(pre)

    blk = pl.BlockSpec((tm, SSM_COL_BLOCK), lambda j, i: (i, j))
    return pl.pallas_call(
        body, name=name,
        out_shape=[jax.ShapeDtypeStruct((s, SSM_CONV_DIM), F32), jax.ShapeDtypeStruct((s, SSM_CONV_DIM), BF16)],
        grid=(ncb, s // tm),
        in_specs=[pl.BlockSpec((SUBLANES, SSM_COL_BLOCK), lambda j, i: (0, j)),
                  pl.BlockSpec((1, SSM_COL_BLOCK), lambda j, i: (0, j)),
                  pl.BlockSpec((tm, SSM_COL_BLOCK), lambda j, i: (i, j + 1)),
                  pl.BlockSpec((SUBLANES, SSM_COL_BLOCK), lambda j, i: (jnp.maximum(i * per - 1, 0), j + 1))],
        out_specs=[blk, blk], compiler_params=_params("parallel", "parallel"),
    )(_pad_taps(conv_w), conv_b.reshape(1, SSM_CONV_DIM), zx, zx)


def _ssm_conv_bwd(zx, pre, conv_w, dact, name):
    s = zx.shape[0]
    tm = _pick(s, (256, 128))
    nt = s // tm
    ncb = SSM_CONV_DIM // SSM_COL_BLOCK
    nk = SSM_CONV_TAPS

    def body(w_ref, x_ref, pre_ref, da_ref, dx_ref, dw_ref, db_ref, carry_ref):
        first = pl.program_id(1) == 0
        for c0 in range(0, SSM_COL_BLOCK, FFN_COLS):
            cols = slice(c0, c0 + FFN_COLS)
            prev = pre_ref[:, cols].astype(F32)
            sig = _sigmoid(prev)
            dpre = da_ref[:, cols] * sig * (1.0 + prev * (1.0 - sig))
            nxt = jnp.where(first, 0.0, carry_ref[:, cols])
            ahead = [_shift_up(dpre, nxt, nk - 1 - k) for k in range(nk - 1)] + [dpre]
            dx_ref[:, cols] = sum(w_ref[k:k + 1, cols] * ahead[k] for k in range(nk)).astype(dx_ref.dtype)
            carry_ref[:, cols] = dpre[0:SUBLANES, :]
            x = x_ref[:, cols]
            dwp = jnp.concatenate([jnp.sum(ahead[k] * x, axis=0, keepdims=True) for k in range(nk)]
                                  + [jnp.zeros((SUBLANES - nk, FFN_COLS), F32)], axis=0)
            dbp = jnp.sum(dpre, axis=0, keepdims=True)

            @pl.when(first)
            def _():
                dw_ref[:, cols] = dwp
                db_ref[:, cols] = dbp

            @pl.when(jnp.logical_not(first))
            def _():
                dw_ref[:, cols] += dwp
                db_ref[:, cols] += dbp

    rev = lambda i: nt - 1 - i
    dx, dw, db = pl.pallas_call(
        body, name=name,
        out_shape=[jax.ShapeDtypeStruct((s, SSM_CONV_DIM), BF16), jax.ShapeDtypeStruct((SUBLANES, SSM_CONV_DIM), F32),
                   jax.ShapeDtypeStruct((1, SSM_CONV_DIM), F32)],
        grid=(ncb, nt),
        in_specs=[pl.BlockSpec((SUBLANES, SSM_COL_BLOCK), lambda j, i: (0, j)),
                  pl.BlockSpec((tm, SSM_COL_BLOCK), lambda j, i: (rev(i), j + 1)),
                  pl.BlockSpec((tm, SSM_COL_BLOCK), lambda j, i: (rev(i), j)),
                  pl.BlockSpec((tm, SSM_COL_BLOCK), lambda j, i: (rev(i), j))],
        out_specs=[pl.BlockSpec((tm, SSM_COL_BLOCK), lambda j, i: (rev(i), j)),
                   pl.BlockSpec((SUBLANES, SSM_COL_BLOCK), lambda j, i: (0, j)),
                   pl.BlockSpec((1, SSM_COL_BLOCK), lambda j, i: (0, j))],
        scratch_shapes=[pltpu.VMEM((SUBLANES, SSM_COL_BLOCK), F32)],
        compiler_params=_params("parallel", "arbitrary"),
    )(_pad_taps(conv_w), zx, pre, dact)
    return dx, dw[:nk], db.reshape(SSM_CONV_DIM)


def _ssd_chunk(xs, bms, cms, dt_raw, dtb, alog, dsk, states):
    q = SSM_CHUNK
    lane = lax.broadcasted_iota(jnp.int32, (1, 128), 1)
    row = lax.broadcasted_iota(jnp.int32, (q, 1), 0)
    ii = lax.broadcasted_iota(jnp.int32, (q, q), 0)
    jj = lax.broadcasted_iota(jnp.int32, (q, q), 1)
    tril = ii >= jj
    left = lane < SSM_HEAD_DIM
    last_row = (row == q - 1).astype(F32)

    def lanes_of(mat, h):
        pick = (lane == h).astype(F32)
        return jnp.broadcast_to(jnp.sum(mat * pick, axis=1, keepdims=True), mat.shape)

    def rows_of(mat_t, h):
        pick = (row == h).astype(F32)
        return jnp.broadcast_to(jnp.sum(mat_t * pick, axis=0, keepdims=True), mat_t.shape)

    v = dt_raw + dtb
    dt = jnp.maximum(v, 0.0) + jnp.log(1.0 + jnp.exp(-jnp.abs(v)))
    adt = dt * (-jnp.exp(alog))
    acs = jnp.dot(tril.astype(F32), adt, precision=lax.Precision.HIGHEST, preferred_element_type=F32)
    acs_t = acs.T
    ys, new_states = [], []
    for pr in range(SSM_PAIRS):
        g = pr // 2
        if pr % 2 == 0:
            cb = _dot_nt(cms[g].astype(BF16), bms[g].astype(BF16))
        cols = [lanes_of(acs, 2 * pr + e) for e in range(2)]
        dts = [lanes_of(dt, 2 * pr + e) for e in range(2)]
        rws = [rows_of(acs_t, 2 * pr + e) for e in range(2)]
        lasts = [jnp.sum(c * last_row, axis=0, keepdims=True) for c in cols]
        xdt = xs[pr] * jnp.where(left, dts[0], dts[1])
        halves = [jnp.where(left, xdt, 0.0).astype(BF16), jnp.where(left, 0.0, xdt).astype(BF16)]
        y_diag, s_new = 0.0, 0.0
        for e in range(2):
            lmat = jnp.where(tril, jnp.exp(jnp.minimum(cols[e] - rws[e], 0.0)), 0.0)
            y_diag = y_diag + _dot((cb * lmat).astype(BF16), halves[e])
            decay = jnp.exp(lasts[e] - cols[e])
            s_new = s_new + _dot_tn((bms[g] * decay).astype(BF16), halves[e])
        y_off = _dot(cms[g].astype(BF16), states[pr].astype(BF16)) * jnp.where(left, jnp.exp(cols[0]), jnp.exp(cols[1]))
        skip = jnp.where(left, lanes_of(dsk, 2 * pr), lanes_of(dsk, 2 * pr + 1))
        ys.append(y_diag + y_off + xs[pr] * skip)
        new_states.append(states[pr] * jnp.where(left, jnp.exp(lasts[0]), jnp.exp(lasts[1])) + s_new)
    return tuple(ys), tuple(new_states)


def _ssm_vec(v):
    return jnp.pad(v.reshape(1, -1), ((0, 0), (0, 128 - v.shape[0])))


def _ssd_scan_fwd(act, zx, dtb, alog, dsk, name):
    s = act.shape[0]
    nc = s // SSM_CHUNK
    ng = SSM_GROUPS

    def body(act_ref, dt_ref, dtb_ref, alog_ref, dsk_ref, y_ref, st_out_ref, st_ref):
        @pl.when(pl.program_id(0) == 0)
        def _():
            st_ref[...] = jnp.zeros_like(st_ref)

        tile = lambda k: act_ref[:, k * 128:(k + 1) * 128]
        xs = [tile(k) for k in range(SSM_PAIRS)]
        bms = [tile(SSM_PAIRS + k) for k in range(ng)]
        cms = [tile(SSM_PAIRS + ng + k) for k in range(ng)]
        states = [st_ref[k] for k in range(SSM_PAIRS)]
        st_out_ref[0] = st_ref[...]
        ys, new_states = _ssd_chunk(xs, bms, cms, dt_ref[...], dtb_ref[...], alog_ref[...], dsk_ref[...], states)
        for k in range(SSM_PAIRS):
            y_ref[:, k * 128:(k + 1) * 128] = ys[k]
            st_ref[k] = new_states[k]

    vec = pl.BlockSpec((1, 128), lambda c: (0, 0))
    return pl.pallas_call(
        body, name=name,
        out_shape=[jax.ShapeDtypeStruct((s, SSM_D_INNER), F32),
                   jax.ShapeDtypeStruct((nc, SSM_PAIRS, SSM_STATE, 128), F32)],
        grid=(nc,),
        in_specs=[pl.BlockSpec((SSM_CHUNK, SSM_CONV_DIM), lambda c: (c, 0)),
                  pl.BlockSpec((SSM_CHUNK, 128), lambda c: (c, SSM_DT_BLOCK)), vec, vec, vec],
        out_specs=[pl.BlockSpec((SSM_CHUNK, SSM_D_INNER), lambda c: (c, 0)),
                   pl.BlockSpec((1, SSM_PAIRS, SSM_STATE, 128), lambda c: (c, 0, 0, 0))],
        scratch_shapes=[pltpu.VMEM((SSM_PAIRS, SSM_STATE, 128), F32)],
        compiler_params=_params("arbitrary"),
    )(act, zx, _ssm_vec(dtb), _ssm_vec(alog), _ssm_vec(dsk))


def _ssd_scan_bwd(act, zx, dtb, alog, dsk, st_in, dy, name):
    s = act.shape[0]
    nc = s // SSM_CHUNK
    ng = SSM_GROUPS

    def body(act_ref, dt_ref, dtb_ref, alog_ref, dsk_ref, st_ref, dy_ref, dact_ref, ddt_ref, dpar_ref, dst_ref):
        first = pl.program_id(0) == 0

        @pl.when(first)
        def _():
            dst_ref[...] = jnp.zeros_like(dst_ref)

        tile = lambda k: act_ref[:, k * 128:(k + 1) * 128]
        xs = [tile(k) for k in range(SSM_PAIRS)]
        bms = [tile(SSM_PAIRS + k) for k in range(ng)]
        cms = [tile(SSM_PAIRS + ng + k) for k in range(ng)]
        states = [st_ref[0, k] for k in range(SSM_PAIRS)]
        _, pullback = jax.vjp(_ssd_chunk, xs, bms, cms, dt_ref[...], dtb_ref[...], alog_ref[...], dsk_ref[...],
                              states)
        dys = tuple(dy_ref[:, k * 128:(k + 1) * 128] for k in range(SSM_PAIRS))
        dsts = tuple(dst_ref[k] for k in range(SSM_PAIRS))
        dxs, dbms, dcms, ddt, ddtb, dalog, ddsk, dstates = pullback((dys, dsts))
        for k, t in enumerate(list(dxs) + list(dbms) + list(dcms)):
            dact_ref[:, k * 128:(k + 1) * 128] = t
        ddt_ref[...] = ddt.astype(ddt_ref.dtype)
        for k in range(SSM_PAIRS):
            dst_ref[k] = dstates[k]
        dpar = jnp.concatenate([ddtb, dalog, ddsk, jnp.zeros((SUBLANES - 3, 128), F32)], axis=0)
        _accumulate(dpar_ref, dpar, first)

    rev = lambda c: nc - 1 - c
    vec = pl.BlockSpec((1, 128), lambda c: (0, 0))
    dact, ddt, dpar = pl.pallas_call(
        body, name=name,
        out_shape=[jax.ShapeDtypeStruct((s, SSM_CONV_DIM), F32), jax.ShapeDtypeStruct((s, 128), BF16),
                   jax.ShapeDtypeStruct((SUBLANES, 128), F32)],
        grid=(nc,),
        in_specs=[pl.BlockSpec((SSM_CHUNK, SSM_CONV_DIM), lambda c: (rev(c), 0)),
                  pl.BlockSpec((SSM_CHUNK, 128), lambda c: (rev(c), SSM_DT_BLOCK)), vec, vec, vec,
                  pl.BlockSpec((1, SSM_PAIRS, SSM_STATE, 128), lambda c: (rev(c), 0, 0, 0)),
                  pl.BlockSpec((SSM_CHUNK, SSM_D_INNER), lambda c: (rev(c), 0))],
        out_specs=[pl.BlockSpec((SSM_CHUNK, SSM_CONV_DIM), lambda c: (rev(c), 0)),
                   pl.BlockSpec((SSM_CHUNK, 128), lambda c: (rev(c), 0)),
                   pl.BlockSpec((SUBLANES, 128), lambda c: (0, 0))],
        scratch_shapes=[pltpu.VMEM((SSM_PAIRS, SSM_STATE, 128), F32)],
        compiler_params=_params("arbitrary"),
    )(act, zx, _ssm_vec(dtb), _ssm_vec(alog), _ssm_vec(dsk), st_in, dy)
    return dact, ddt, dpar[0, :SSM_HEADS], dpar[1, :SSM_HEADS], dpar[2, :SSM_HEADS]


SSM_NORM_GROUP = SSM_D_INNER // SSM_GROUPS


def _gated_group(y, z, w):
    g = y * (z * _sigmoid(z))
    return g * lax.rsqrt(jnp.mean(g * g, axis=-1, keepdims=True) + NORM_EPS) * w


def _gated_norm_fwd(y, zx, w, name):
    s = y.shape[0]
    tm = _pick(s, (256, 128))

    def body(y_ref, z_ref, w_ref, o_ref):
        for c0 in range(0, SSM_D_INNER, SSM_NORM_GROUP):
            cols = slice(c0, c0 + SSM_NORM_GROUP)
            o_ref[:, cols] = _gated_group(y_ref[:, cols], z_ref[:, cols], w_ref[:, cols]).astype(o_ref.dtype)

    blk = pl.BlockSpec((tm, SSM_D_INNER), lambda i: (i, 0))
    return pl.pallas_call(
        body, name=name, out_shape=jax.ShapeDtypeStruct((s, SSM_D_INNER), BF16), grid=(s // tm,),
        in_specs=[blk, blk, pl.BlockSpec((1, SSM_D_INNER), lambda i: (0, 0))], out_specs=blk,
        compiler_params=_params("parallel"),
    )(y, zx, w.reshape(1, SSM_D_INNER))


def _gated_norm_bwd(y, zx, w, dout, name):
    s = y.shape[0]
    tm = _pick(s, (256, 128))

    def body(y_ref, z_ref, w_ref, do_ref, dy_ref, dz_ref, dw_ref):
        first = pl.program_id(0) == 0
        for c0 in range(0, SSM_D_INNER, SSM_NORM_GROUP):
            cols = slice(c0, c0 + SSM_NORM_GROUP)
            _, pullback = jax.vjp(_gated_group, y_ref[:, cols], z_ref[:, cols], w_ref[:, cols])
            dyv, dzv, dwv = pullback(do_ref[:, cols].astype(F32))
            dy_ref[:, cols] = dyv
            dz_ref[:, cols] = dzv.astype(dz_ref.dtype)

            @pl.when(first)
            def _():
                dw_ref[:, cols] = dwv

            @pl.when(jnp.logical_not(first))
            def _():
                dw_ref[:, cols] += dwv

    blk = pl.BlockSpec((tm, SSM_D_INNER), lambda i: (i, 0))
    vec = pl.BlockSpec((1, SSM_D_INNER), lambda i: (0, 0))
    dy, dz, dw = pl.pallas_call(
        body, name=name,
        out_shape=[jax.ShapeDtypeStruct((s, SSM_D_INNER), F32), jax.ShapeDtypeStruct((s, SSM_D_INNER), BF16),
                   jax.ShapeDtypeStruct((1, SSM_D_INNER), F32)],
        grid=(s // tm,), in_specs=[blk, blk, vec, blk], out_specs=[blk, blk, vec],
        compiler_params=_params("arbitrary"),
    )(y, zx, w.reshape(1, SSM_D_INNER), dout)
    return dy, dz, dw.reshape(SSM_D_INNER)


def _ssm_core_fwd(zx, conv_w, conv_b, dtb, alog, dsk, norm_w, name):
    act, pre = _ssm_conv_fwd(zx, conv_w, conv_b, name + "_conv_fwd")
    y, st_in = _ssd_scan_fwd(act, zx, dtb, alog, dsk, name + "_scan_fwd")
    out = _gated_norm_fwd(y, zx, norm_w, name + "_gate_fwd")
    return out, (zx, conv_w, pre, dtb, alog, dsk, norm_w, act, y, st_in)


@functools.partial(jax.custom_vjp, nondiff_argnums=(7,))
def ssm_core(zx, conv_w, conv_b, dtb, alog, dsk, norm_w, name):
    return _ssm_core_fwd(zx, conv_w, conv_b, dtb, alog, dsk, norm_w, name)[0]


def _ssm_core_vjp_fwd(zx, conv_w, conv_b, dtb, alog, dsk, norm_w, name):
    return _ssm_core_fwd(zx, conv_w, conv_b, dtb, alog, dsk, norm_w, name)


def _ssm_core_vjp_bwd(name, res, dout):
    zx, conv_w, pre, dtb, alog, dsk, norm_w, act, y, st_in = res
    dy, dz, dnorm_w = _gated_norm_bwd(y, zx, norm_w, dout, name + "_gate_bwd")
    dact, ddt, ddtb, dalog, ddsk = _ssd_scan_bwd(act, zx, dtb, alog, dsk, st_in, dy, name + "_scan_bwd")
    dxbc, dconv_w, dconv_b = _ssm_conv_bwd(zx, pre, conv_w, dact, name + "_conv_bwd")
    dzx = jnp.concatenate([dz, dxbc, ddt], axis=1)
    return dzx, dconv_w, dconv_b, ddtb, dalog, ddsk, dnorm_w


ssm_core.defvjp(_ssm_core_vjp_fwd, _ssm_core_vjp_bwd)


def ssd_mixer_p(x, h, lin, j, carries, conv_w, conv_b, dtb, alog, dsk, norm_w, tag):
    zx = lin(h, ('ssm_w_in', j), F32, tag + "_in", carry=carries[0])
    return lin(ssm_core(zx, conv_w, conv_b, dtb, alog, dsk, norm_w, tag), ('ssm_w_out', j), F32, tag + "_out",
               carry=carries[1], res=x)


def _full_weight(n, parts):
    full = _join8(parts, SHARD_AXIS[n] - 1)
    if n == 'ssm_w_in':
        full = jnp.pad(full, ((0, 0), (0, SSM_IN_PAD - SSM_IN_WIDTH)))
    return full


def trunk(w, x):
    ready = {}

    def lin(a, key, out_dtype, name, cols=None, carry=None, res=None, rope=None):
        wt = ready[key] if cols is None else ready[key][:, cols]
        if carry is None:
            assert rope is None
            return linear(a, wt, res, out_dtype, name)
        y, parts = linear_x(a, wt, res, rope, w[carry[0]][carry[1]], out_dtype, name)
        ready[carry] = _full_weight(carry[0], parts)
        return y

    for i in range(DEPTH):
        x, h = norm(x, w['mix_norm_w'][i], f"mixnorm{i}")
        j = i // 2
        ffn_next = [('ffn_w_up', i), ('ffn_w_down', i)]
        if i == 0:
            x = first_attention_mixer(x, h, lin, w, ready, [('attn_w_o', 0)] + ffn_next, "attn0")
        elif i % 2 == 0:
            x = attention_mixer_p(x, h, lin, j, [('attn_w_o', j)] + ffn_next, f"attn{j}")
        else:
            x = ssd_mixer_p(x, h, lin, j, ffn_next, w['ssm_conv_w'][j], w['ssm_conv_b'][j], w['ssm_dt_bias'][j],
                            w['ssm_a_log'][j], w['ssm_d'][j], w['ssm_norm_w'][j], f"ssm{j}")
        if i + 1 == DEPTH:
            mixer_next = [None, None]
        elif i % 2 == 0:
            mixer_next = [('ssm_w_in', j), ('ssm_w_out', j)]
        else:
            mixer_next = [('attn_w_qkv', j + 1), None]
        x, h = norm(x, w['ffn_norm_w'][i], f"ffnnorm{i}")
        u0 = lin(h, ('ffn_w_up', i), F32, f"ffn{i}_up", carry=mixer_next[0])
        a = ffn_mid(u0, w['ffn_conv_w'][i], w['ffn_conv_b'][i], f"ffn{i}_mid")
        x = lin(a, ('ffn_w_down', i), F32, f"ffn{i}_down", carry=mixer_next[1], res=x)
    return x


def local_step(w, x, target):
    final_w = w['final_norm_w']
    trunk_w = {n: a for n, a in w.items() if n != 'final_norm_w'}
    xf, pullback = jax.vjp(trunk, trunk_w, x)
    loss, dxf, dfinal = loss_head(xf, final_w, target, "loss_head")
    gw, gx = pullback(dxf)
    gw['final_norm_w'] = dfinal
    return loss, gw, gx


def _adam_math(w, g, m, v):
    m = ADAM_B1 * m + (1.0 - ADAM_B1) * g
    v = ADAM_B2 * v + (1.0 - ADAM_B2) * (g * g)
    m_hat = m / (1.0 - ADAM_B1 ** ADAM_STEP)
    v_hat = v / (1.0 - ADAM_B2 ** ADAM_STEP)
    delta = -ADAM_LR * (m_hat / (jnp.sqrt(v_hat) + ADAM_EPS) + ADAM_WD * w)
    return delta, m, v


def _adamw_rows(g, w, m, v, name):
    r, c = w.shape
    tr = _pick(r, (256, 128, 64, 32, 16, 8))

    def body(g_ref, w_ref, m_ref, v_ref, d_out, m_out, v_out):
        delta, mm, vv = _adam_math(w_ref[...], g_ref[...], m_ref[...], v_ref[...])
        d_out[...] = delta
        m_out[...] = mm
        v_out[...] = vv

    blk = pl.BlockSpec((tr, c), lambda i: (i, 0))
    return pl.pallas_call(
        body, name=name, out_shape=[jax.ShapeDtypeStruct((r, c), F32)] * 3, grid=(r // tr,),
        in_specs=[blk] * 4, out_specs=[blk] * 3, compiler_params=_params("parallel"),
    )(g, w, m, v)


def _sum8(pieces, name):
    _, r, c = pieces.shape

    def body(p_ref, o_ref):
        g = p_ref[0]
        for j in range(1, N_DEV):
            g = g + p_ref[j]
        o_ref[...] = g

    return pl.pallas_call(
        body, name=name, out_shape=jax.ShapeDtypeStruct((r, c), F32),
        in_specs=[pl.BlockSpec(memory_space=pltpu.VMEM)], out_specs=pl.BlockSpec(memory_space=pltpu.VMEM),
    )(pieces)


def _adamw_plain(g, w, m, v, name):
    def body(g_ref, w_ref, m_ref, v_ref, d_out, m_out, v_out):
        delta, mm, vv = _adam_math(w_ref[...], g_ref[...], m_ref[...], v_ref[...])
        d_out[...] = delta
        m_out[...] = mm
        v_out[...] = vv

    vm = pl.BlockSpec(memory_space=pltpu.VMEM)
    return pl.pallas_call(
        body, name=name, out_shape=[jax.ShapeDtypeStruct(g.shape, F32)] * 3,
        in_specs=[vm] * 4, out_specs=[vm] * 3,
    )(g, w, m, v)


def _join8(parts, axis):
    t = jnp.moveaxis(parts, 0, axis)
    shp = t.shape
    return t.reshape(shp[:axis] + (shp[axis] * shp[axis + 1],) + shp[axis + 2:])


def _pack(arrs, lead, mult):
    flat = jnp.concatenate([a.reshape(a.shape[:lead] + (-1,)) for a in arrs], axis=-1)
    return _pad_rows(flat, mult)


def _unpack(buf, shapes, lead):
    flat = buf.reshape(buf.shape[:lead] + (-1,))
    out, off = [], 0
    for shp in shapes:
        n = math.prod(shp)
        out.append(flat[..., off:off + n].reshape(flat.shape[:lead] + tuple(shp)))
        off += n
    return out


def _own_shard(full, axis):
    size = full.shape[axis] // N_DEV
    return lax.dynamic_slice_in_dim(full, _my_index() * size, size, axis)


def kernel(x, mix_norm_w, attn_w_qkv, attn_w_o, ssm_w_in, ssm_conv_w, ssm_conv_b, ssm_dt_bias, ssm_a_log, ssm_d, ssm_norm_w, ssm_w_out, ffn_norm_w, ffn_w_up, ffn_conv_w, ffn_conv_b, ffn_w_down, final_norm_w, loss_target, m_mix_norm_w, m_attn_w_qkv, m_attn_w_o, m_ssm_w_in, m_ssm_conv_w, m_ssm_conv_b, m_ssm_dt_bias, m_ssm_a_log, m_ssm_d, m_ssm_norm_w, m_ssm_w_out, m_ffn_norm_w, m_ffn_w_up, m_ffn_conv_w, m_ffn_conv_b, m_ffn_w_down, m_final_norm_w, v_mix_norm_w, v_attn_w_qkv, v_attn_w_o, v_ssm_w_in, v_ssm_conv_w, v_ssm_conv_b, v_ssm_dt_bias, v_ssm_a_log, v_ssm_d, v_ssm_norm_w, v_ssm_w_out, v_ffn_norm_w, v_ffn_w_up, v_ffn_conv_w, v_ffn_conv_b, v_ffn_w_down, v_final_norm_w):
    args = dict(locals())
    w_sh = {n: args[n] for n in WEIGHT_NAMES}
    m_sh = {n: args["m_" + n] for n in WEIGHT_NAMES}
    v_sh = {n: args["v_" + n] for n in WEIGHT_NAMES}

    small_shapes = [w_sh[n].shape for n in SMALL_SHARDED]
    small = _exchange(_pack([w_sh[n] for n in SMALL_SHARDED], 0, 8), True, "gather_small")
    full = {n: w_sh[n] for n in SMALL if SHARD_AXIS[n] is None}
    for n, parts in zip(SMALL_SHARDED, _unpack(small, small_shapes, 1)):
        full[n] = _join8(parts, SHARD_AXIS[n])
    for n in BIG:
        full[n] = [w_sh[n][j] for j in range(w_sh[n].shape[0])]

    loss, gw, gx = local_step(full, x[0], loss_target[0])
    loss = lax.psum(loss, ("x", "y", "c"))
    for n in BIG:
        gw[n] = jnp.stack(gw[n])

    grads, deltas, new_m, new_v = {}, {}, {}, {}
    for n in BIG:
        shp = w_sh[n].shape
        two_d = (shp[0] * shp[1], shp[2])
        outs = _adamw_rows(*[t.reshape(two_d) for t in (gw[n], w_sh[n], m_sh[n], v_sh[n])], "adamw_" + n)
        grads[n] = gw[n]
        deltas[n], new_m[n], new_v[n] = [o.reshape(shp) for o in outs]

    small_full_shapes = [gw[n].shape for n in SMALL]
    gsmall = _exchange(_pack([gw[n] for n in SMALL], 0, 8), True, "gather_small_grads")
    gsmall = _unpack(_sum8(gsmall, "sum_small_grads"), small_full_shapes, 0)
    for n, g in zip(SMALL, gsmall):
        grads[n] = g if SHARD_AXIS[n] is None else _own_shard(g, SHARD_AXIS[n])
    shapes = [w_sh[n].shape for n in SMALL]
    outs = _adamw_plain(*[_pack([d[n] for n in SMALL], 0, 8) for d in (grads, w_sh, m_sh, v_sh)], "adamw_small")
    for d, buf in zip((deltas, new_m, new_v), outs):
        for n, a in zip(SMALL, _unpack(buf, shapes, 0)):
            d[n] = a

    return (loss, gx[None], *[grads[n] for n in WEIGHT_NAMES], *[deltas[n] for n in WEIGHT_NAMES],
            *[new_m[n] for n in WEIGHT_NAMES], *[new_v[n] for n in WEIGHT_NAMES])
```

```python
import functools
import math

import jax
import jax.numpy as jnp
from jax import lax
from jax.experimental import pallas as pl
from jax.experimental.pallas import tpu as pltpu

F32 = jnp.float32
BF16 = jnp.bfloat16
N_DEV = 8
MESH_ID = pl.DeviceIdType.MESH

DEPTH = 4
ATTN_HEADS = 8
ATTN_HEAD_DIM = 128
ATTN_DILATIONS = (1, 4, 16)
ATTN_BLOCK = 128
ROPE_THETA = 500000.0
ROPE_DIM = 32
ATTN_OUT_WIDTH = 1024
SSM_D_INNER = 2048
SSM_HEAD_DIM = 64
SSM_HEADS = 32
SSM_STATE = 128
SSM_GROUPS = 8
SSM_CHUNK = 128
SSM_CONV_DIM = 4096
SSM_IN_WIDTH = 6176
SSM_IN_PAD = 6400
NORM_EPS = 1e-5
ADAM_LR = 0.001
ADAM_B1 = 0.9
ADAM_B2 = 0.999
ADAM_EPS = 1e-08
ADAM_WD = 0.01
ADAM_STEP = 10

WEIGHT_NAMES = ['mix_norm_w', 'attn_w_qkv', 'attn_w_o', 'ssm_w_in', 'ssm_conv_w', 'ssm_conv_b', 'ssm_dt_bias',
                'ssm_a_log', 'ssm_d', 'ssm_norm_w', 'ssm_w_out', 'ffn_norm_w', 'ffn_w_up', 'ffn_conv_w',
                'ffn_conv_b', 'ffn_w_down', 'final_norm_w']
SHARD_AXIS = {'mix_norm_w': None, 'attn_w_qkv': 2, 'attn_w_o': 1, 'ssm_w_in': 2, 'ssm_conv_w': 2, 'ssm_conv_b': 1,
              'ssm_dt_bias': None, 'ssm_a_log': None, 'ssm_d': None, 'ssm_norm_w': 1, 'ssm_w_out': 1,
              'ffn_norm_w': None, 'ffn_w_up': 2, 'ffn_conv_w': 2, 'ffn_conv_b': None, 'ffn_w_down': 1,
              'final_norm_w': None}
BIG = ['attn_w_qkv', 'attn_w_o', 'ssm_w_in', 'ssm_w_out', 'ffn_w_up', 'ffn_w_down']
SMALL = [n for n in WEIGHT_NAMES if n not in BIG]
SMALL_SHARDED = [n for n in SMALL if SHARD_AXIS[n] is not None]
LANES = 1024


def _my_index():
    return 4 * lax.axis_index("x") + 2 * lax.axis_index("y") + lax.axis_index("c")


def _peer(k):
    x, y, c = lax.axis_index("x"), lax.axis_index("y"), lax.axis_index("c")
    return (x ^ ((k >> 2) & 1), y ^ ((k >> 1) & 1), c ^ (k & 1))


def _exchange(src, gather, name):
    def body(src_ref, out_ref, send_sems, recv_sems, local_sem):
        start, wait = _exchange_copies(src_ref, out_ref, send_sems, recv_sems, local_sem, gather)
        start()
        wait()

    return pl.pallas_call(
        body, name=name,
        out_shape=_exchange_out(src, gather),
        in_specs=[pl.BlockSpec(memory_space=pl.ANY)],
        out_specs=pl.BlockSpec(memory_space=pl.ANY),
        scratch_shapes=list(EXCHANGE_SEMS),
    )(src)


EXCHANGE_SEMS = (pltpu.SemaphoreType.DMA((N_DEV - 1,)), pltpu.SemaphoreType.DMA((N_DEV - 1,)),
                 pltpu.SemaphoreType.DMA)


def _exchange_out(src, gather):
    return jax.ShapeDtypeStruct((N_DEV,) + (src.shape if gather else src.shape[1:]), src.dtype)


def _exchange_copies(src_ref, out_ref, send_sems, recv_sems, local_sem, gather):
    me = _my_index()

    def piece(j):
        return src_ref if gather else src_ref.at[j]

    def remote(k, slab):
        return pltpu.make_async_remote_copy(
            src_ref=piece(me ^ k), dst_ref=out_ref.at[slab], send_sem=send_sems.at[k - 1],
            recv_sem=recv_sems.at[k - 1], device_id=_peer(k), device_id_type=MESH_ID)

    mine = pltpu.make_async_copy(piece(me), out_ref.at[me], local_sem)
    arrivals = {k: remote(k, me ^ k) for k in range(1, N_DEV)}
    if not gather:
        sends = [remote(k, me) for k in range(1, N_DEV)]

        def start():
            mine.start()
            for cp in sends:
                cp.start()

        def wait():
            for cp in arrivals.values():
                cp.wait_recv()
            for cp in sends:
                cp.wait_send()
            mine.wait()

        return start, wait

    far = (2, 4, 6)
    sends = [remote(k, me) for k in (1,) + far]
    passed_on = [pltpu.make_async_remote_copy(
        src_ref=out_ref.at[me ^ k], dst_ref=out_ref.at[me ^ k], send_sem=send_sems.at[k],
        recv_sem=recv_sems.at[k], device_id=_peer(1), device_id_type=MESH_ID) for k in far]

    def start():
        mine.start()
        for cp in sends:
            cp.start()

    def wait():
        for k, cp in zip(far, passed_on):
            arrivals[k].wait_recv()
            cp.start()
        for k in (1, 3, 5, 7):
            arrivals[k].wait_recv()
        for cp in sends + passed_on:
            cp.wait_send()
        mine.wait()

    return start, wait


def _pad_rows(flat, mult):
    n = flat.shape[-1]
    rows = -(-n // (LANES * mult)) * mult
    pad = rows * LANES - n
    flat = jnp.pad(flat, [(0, 0)] * (flat.ndim - 1) + [(0, pad)])
    return flat.reshape(flat.shape[:-1] + (rows, LANES))


MATMUL_VMEM_BUDGET = 48 * 1024 * 1024
MATMUL_TM = (1024, 1408, 512, 256, 128)
MATMUL_TN = (3200, 3072, 2816, 2048, 1536, 1408, 1280, 1024, 896, 512, 384, 256, 128)
MATMUL_TK = (3200, 3072, 2816, 2048, 1408, 1280, 1024, 896)
MATMUL_MIN_TK = 896


def _pick(n, cands):
    for c in cands:
        if n % c == 0:
            return c
    return n


def _matmul_tiles(m, n, k, a_bytes, b_bytes, out_bytes, has_res):
    tm = _pick(m, MATMUL_TM)

    def need(tn, tk):
        blocks = 2 * (tm * tk * a_bytes + tk * tn * b_bytes + tm * tn * out_bytes + has_res * tm * tn * 4)
        temps = tm * tn * 4 * (1 + (tk < k)) + (a_bytes > 2) * tm * tk * 2 + (b_bytes > 2) * tk * tn * 2
        return blocks + temps

    for tn in [c for c in MATMUL_TN if n % c == 0] + [n]:
        for tk in [k] + [c for c in MATMUL_TK if k % c == 0 and MATMUL_MIN_TK <= c < k]:
            if need(tn, tk) <= MATMUL_VMEM_BUDGET:
                return tm, tn, tk
    raise ValueError(f"no matmul tiling fits VMEM for {(m, n, k)}")


def _matmul(a, b, ta, tb, out_dtype, name, rider=None, res=None, rope=None):
    (m, k) = (a.shape[1], a.shape[0]) if ta else a.shape
    (k2, n) = (b.shape[1], b.shape[0]) if tb else b.shape
    assert k == k2, (a.shape, b.shape, ta, tb)
    tm, tn, tk = _matmul_tiles(m, n, k, a.dtype.itemsize, b.dtype.itemsize, jnp.dtype(out_dtype).itemsize,
                               res is not None)
    nk = k // tk
    dims = (((0 if ta else 1,), (1 if tb else 0,)), ((), ()))

    grid = (n // tn, m // tm, nk)

    n_in = 2 + (res is not None) + (rope is not None) + (rider is not None)
    if rope is not None:
        assert res is None and tn == n == QKV_GROUP, (tn, n)

    def body(*refs):
        ins, rest = refs[:n_in], refs[n_in:]
        a_ref, b_ref = ins[:2]
        res_ref = ins[2] if res is not None else None
        rope_ref = ins[2] if rope is not None else None
        o_ref, scratch = rest[0], rest[1:]
        if rider is not None:
            start, wait = _exchange_copies(ins[-1], rest[1], *scratch[-3:], rider[1])
            scratch = scratch[1:-3]
            at = [pl.program_id(ax) for ax in range(3)]
            pl.when(functools.reduce(jnp.logical_and, [p == 0 for p in at]))(start)
        part = lax.dot_general(a_ref[...].astype(BF16), b_ref[...].astype(BF16), dims,
                               preferred_element_type=F32)

        def finish(total):
            if rope_ref is not None:
                for c0 in range(0, tn, ATTN_HEAD_DIM):
                    head = total[:, c0:c0 + ATTN_HEAD_DIM]
                    if c0 < 2 * ATTN_OUT_WIDTH:
                        head = _rope(head, rope_ref[...], 1.0)
                    o_ref[:, c0:c0 + ATTN_HEAD_DIM] = head.astype(o_ref.dtype)
                return
            if res_ref is not None:
                total = total + res_ref[...]
            o_ref[...] = total.astype(o_ref.dtype)

        if nk == 1:
            finish(part)
        else:
            acc_ref, = scratch
            kk = pl.program_id(2)

            @pl.when(kk == 0)
            def _():
                acc_ref[...] = part

            @pl.when(kk > 0)
            def _():
                acc_ref[...] += part

            @pl.when(kk == nk - 1)
            def _():
                finish(acc_ref[...])
        if rider is not None:
            pl.when(functools.reduce(jnp.logical_and, [p == g - 1 for p, g in zip(at, grid)]))(wait)

    a_spec = (pl.BlockSpec((tk, tm), lambda j, i, kk: (kk, i)) if ta
              else pl.BlockSpec((tm, tk), lambda j, i, kk: (i, kk)))
    b_spec = (pl.BlockSpec((tn, tk), lambda j, i, kk: (j, kk)) if tb
              else pl.BlockSpec((tk, tn), lambda j, i, kk: (kk, j)))
    any_spec = pl.BlockSpec(memory_space=pl.ANY)
    tile_spec = pl.BlockSpec((tm, tn), lambda j, i, kk: (i, j))
    in_specs, operands = [a_spec, b_spec], [a, b]
    out_shape, out_specs = [jax.ShapeDtypeStruct((m, n), out_dtype)], [tile_spec]
    scratch = [] if nk == 1 else [pltpu.VMEM((tm, tn), F32)]
    sem = ("parallel", "parallel", "arbitrary")
    if res is not None:
        in_specs.append(tile_spec)
        operands.append(res)
    if rope is not None:
        in_specs.append(pl.BlockSpec((tm, rope.shape[1]), lambda j, i, kk: (i, 0)))
        operands.append(rope)
    if rider is not None:
        in_specs.append(any_spec)
        operands.append(rider[0])
        out_shape.append(_exchange_out(*rider))
        out_specs.append(any_spec)
        scratch += list(EXCHANGE_SEMS)
        sem = ("arbitrary",) * 3
    outs = pl.pallas_call(
        body, name=name, out_shape=out_shape, grid=grid, in_specs=in_specs, out_specs=out_specs,
        scratch_shapes=scratch, compiler_params=_params(*sem),
    )(*operands)
    return outs[0] if rider is None else tuple(outs)


@functools.partial(jax.custom_vjp, nondiff_argnums=(3, 4))
def linear(a, w, res, out_dtype, name):
    return _matmul(a, w, False, False, out_dtype, name + "_fwd", res=res)


def _linear_fwd(a, w, res, out_dtype, name):
    return _matmul(a, w, False, False, out_dtype, name + "_fwd", res=res), (a, w, res is not None)


def _linear_bwd(out_dtype, name, saved, dy):
    a, w, has_res = saved
    da = _matmul(dy, w, False, True, a.dtype, name + "_da")
    dw = _matmul(a, dy, True, False, w.dtype, name + "_dw")
    return da, dw, (dy if has_res else None)


linear.defvjp(_linear_fwd, _linear_bwd)


def _sum_pieces(pieces, name):
    shape = pieces.shape[1:]
    c = shape[-1]
    r = math.prod(shape[:-1])
    tr = _pick(r, (512, 256, 128, 64, 32, 16, 8))

    def body(p_ref, o_ref):
        g = p_ref[0].astype(F32)
        for j in range(1, N_DEV):
            g = g + p_ref[j].astype(F32)
        o_ref[...] = g

    return pl.pallas_call(
        body, name=name, out_shape=jax.ShapeDtypeStruct((r, c), F32), grid=(r // tr,),
        in_specs=[pl.BlockSpec((N_DEV, tr, c), lambda i: (0, i, 0))],
        out_specs=pl.BlockSpec((tr, c), lambda i: (i, 0)),
        compiler_params=_params("parallel"),
    )(pieces.reshape(N_DEV, r, c)).reshape(shape)


@functools.partial(jax.custom_vjp, nondiff_argnums=(5, 6))
def linear_x(a, w, res, rope, shard, out_dtype, name):
    return _matmul(a, w, False, False, out_dtype, name + "_fwd", rider=(shard.astype(BF16), True), res=res,
                   rope=rope)


def _linear_x_fwd(a, w, res, rope, shard, out_dtype, name):
    out = _matmul(a, w, False, False, out_dtype, name + "_fwd", rider=(shard.astype(BF16), True), res=res,
                  rope=rope)
    return out, (a, w, res is not None, rope)


def _linear_x_bwd(out_dtype, name, saved, cts):
    a, w, has_res, rope = saved
    dy, dparts = cts
    half = dparts.shape[1] // 2
    da, moved_a = _matmul(dy, w, False, True, a.dtype, name + "_da", rider=(dparts[:, :half], False))
    dw, moved_b = _matmul(a, dy, True, False, w.dtype, name + "_dw", rider=(dparts[:, half:], False))
    dshard = jnp.concatenate([_sum_pieces(moved_a, name + "_sum_a"), _sum_pieces(moved_b, name + "_sum_b")])
    return da, dw, (dy if has_res else None), (None if rope is None else jnp.zeros_like(rope)), dshard


linear_x.defvjp(_linear_x_fwd, _linear_x_bwd)

FIRST_ROW_SPLIT = (0, 384, 704, 1024)


def _qkv_first_fwd(hs, shard, ropes, carried, name):
    parts = _exchange(shard.astype(BF16), True, name + "_gather")
    wfull = _join8(parts, 1)
    ys, moved = [], []
    for g, (h, rope, carry) in enumerate(zip(hs, ropes, carried)):
        y, m = _matmul(h, wfull[:, g * QKV_GROUP:(g + 1) * QKV_GROUP], False, False, BF16, f"{name}{g}_fwd",
                       rider=(carry.astype(BF16), True), rope=rope)
        ys.append(y)
        moved.append(m)
    return (tuple(ys), tuple(moved)), (hs, wfull, ropes)


@functools.partial(jax.custom_vjp, nondiff_argnums=(4,))
def qkv_first(hs, shard, ropes, carried, name):
    return _qkv_first_fwd(hs, shard, ropes, carried, name)[0]


def _qkv_first_vjp_fwd(hs, shard, ropes, carried, name):
    return _qkv_first_fwd(hs, shard, ropes, carried, name)


def _qkv_first_vjp_bwd(name, saved, cts):
    hs, wfull, ropes = saved
    dys, dmoved = cts
    dws, dcarried = [], []
    for g in range(len(hs)):
        dw, back = _matmul(hs[g], dys[g], True, False, BF16, f"{name}{g}_dw", rider=(dmoved[g], False))
        dws.append(dw)
        dcarried.append(_sum_pieces(back, f"{name}{g}_carried_sum"))
    w_shape = wfull.shape
    dparts = jnp.moveaxis(jnp.concatenate(dws, axis=1).reshape(w_shape[0], N_DEV, w_shape[1] // N_DEV), 1, 0)
    das, dshard = [], []
    for g in range(len(hs)):
        rows = slice(FIRST_ROW_SPLIT[g], FIRST_ROW_SPLIT[g + 1])
        da, got = _matmul(dys[g], wfull[:, g * QKV_GROUP:(g + 1) * QKV_GROUP], False, True, BF16, f"{name}{g}_da",
                          rider=(dparts[:, rows], False))
        das.append(da)
        dshard.append(_sum_pieces(got, f"{name}{g}_own_sum"))
    return tuple(das), jnp.concatenate(dshard), tuple(jnp.zeros_like(r) for r in ropes), tuple(dcarried)


qkv_first.defvjp(_qkv_first_vjp_fwd, _qkv_first_vjp_bwd)


VMEM_LIMIT = 56 * 1024 * 1024
SUBLANES = 8


def _params(*sem):
    return pltpu.CompilerParams(dimension_semantics=sem, vmem_limit_bytes=VMEM_LIMIT)


def _sigmoid(x):
    return 0.5 * jnp.tanh(0.5 * x) + 0.5


def _rstd(xv):
    return lax.rsqrt(jnp.mean(xv * xv, axis=-1, keepdims=True) + NORM_EPS)


def _accumulate(ref, part, first):
    @pl.when(first)
    def _():
        ref[...] = part

    @pl.when(jnp.logical_not(first))
    def _():
        ref[...] += part


def _norm_fwd(x, w, name):
    s, d = x.shape
    tm = _pick(s, (512, 256, 128))

    def body(x_ref, w_ref, h_ref):
        xv = x_ref[...]
        h_ref[...] = (xv * _rstd(xv) * w_ref[...]).astype(h_ref.dtype)

    return pl.pallas_call(
        body, name=name, out_shape=jax.ShapeDtypeStruct((s, d), BF16), grid=(s // tm,),
        in_specs=[pl.BlockSpec((tm, d), lambda i: (i, 0)), pl.BlockSpec((1, d), lambda i: (0, 0))],
        out_specs=pl.BlockSpec((tm, d), lambda i: (i, 0)), compiler_params=_params("parallel"),
    )(x, w.reshape(1, d))


def _norm_bwd(x, w, dh, dskip, name):
    s, d = x.shape
    tm = _pick(s, (512, 256, 128))

    def body(x_ref, w_ref, dh_ref, ds_ref, dx_ref, dw_ref):
        xv = x_ref[...]
        r = _rstd(xv)
        y = xv * r
        dhv = dh_ref[...].astype(F32)
        dy = dhv * w_ref[...]
        dx_ref[...] = ds_ref[...] + r * (dy - y * jnp.mean(dy * y, axis=-1, keepdims=True))
        _accumulate(dw_ref, jnp.sum(dhv * y, axis=0, keepdims=True), pl.program_id(0) == 0)

    row = pl.BlockSpec((tm, d), lambda i: (i, 0))
    vec = pl.BlockSpec((1, d), lambda i: (0, 0))
    dx, dw = pl.pallas_call(
        body, name=name,
        out_shape=[jax.ShapeDtypeStruct((s, d), F32), jax.ShapeDtypeStruct((1, d), F32)], grid=(s // tm,),
        in_specs=[row, vec, row, row], out_specs=[row, vec], compiler_params=_params("arbitrary"),
    )(x, w.reshape(1, d), dh, dskip)
    return dx, dw.reshape(d)


@functools.partial(jax.custom_vjp, nondiff_argnums=(2,))
def norm(x, w, name):
    return x, _norm_fwd(x, w, name + "_fwd")


def _norm_vjp_fwd(x, w, name):
    return (x, _norm_fwd(x, w, name + "_fwd")), (x, w)


def _norm_vjp_bwd(name, saved, cts):
    x, w = saved
    dskip, dh = cts
    return _norm_bwd(x, w, dh, dskip, name + "_bwd")


norm.defvjp(_norm_vjp_fwd, _norm_vjp_bwd)


def loss_head(x, w, target, name):
    s, d = x.shape
    tm = _pick(s, (512, 256, 128))

    def body(x_ref, w_ref, t_ref, loss_ref, dx_ref, dw_ref):
        first = pl.program_id(0) == 0
        xv = x_ref[...]
        r = _rstd(xv)
        y = xv * r
        err = y * w_ref[...] - t_ref[...]
        part = 0.5 * jnp.sum(jnp.sum(err * err, axis=-1, keepdims=True), axis=0, keepdims=True) / d
        _accumulate(loss_ref, jnp.broadcast_to(part, loss_ref.shape), first)
        dout = err / d
        dy = dout * w_ref[...]
        dx_ref[...] = r * (dy - y * jnp.mean(dy * y, axis=-1, keepdims=True))
        _accumulate(dw_ref, jnp.sum(dout * y, axis=0, keepdims=True), first)

    row = pl.BlockSpec((tm, d), lambda i: (i, 0))
    vec = pl.BlockSpec((1, d), lambda i: (0, 0))
    loss, dx, dw = pl.pallas_call(
        body, name=name,
        out_shape=[jax.ShapeDtypeStruct((1, 128), F32), jax.ShapeDtypeStruct((s, d), F32),
                   jax.ShapeDtypeStruct((1, d), F32)],
        grid=(s // tm,), in_specs=[row, vec, row],
        out_specs=[pl.BlockSpec((1, 128), lambda i: (0, 0)), row, vec],
        compiler_params=_params("arbitrary"),
    )(x, w.reshape(1, d), target)
    return loss[0, 0], dx, dw.reshape(d)


FFN_ROWS = 256
FFN_COLS = 256


def _shift_down(cur, halo, k):
    out = pltpu.roll(cur, k, axis=0)
    row = lax.broadcasted_iota(jnp.int32, halo.shape, 0)
    top = out[0:SUBLANES]
    for j in range(k):
        top = jnp.where(row == j, halo[SUBLANES - k + j:SUBLANES - k + j + 1, :], top)
    return jnp.concatenate([top, out[SUBLANES:]], axis=0)


def _shift_up(cur, nxt, k):
    n = cur.shape[0]
    out = pltpu.roll(cur, n - k, axis=0)
    row = lax.broadcasted_iota(jnp.int32, nxt.shape, 0)
    bottom = out[n - SUBLANES:]
    for j in range(k):
        bottom = jnp.where(row == SUBLANES - k + j, nxt[j:j + 1, :], bottom)
    return jnp.concatenate([out[:n - SUBLANES], bottom], axis=0)


def _conv_taps(cur, halo, ntaps):
    return [_shift_down(cur, halo, ntaps - 1 - k) if k < ntaps - 1 else cur for k in range(ntaps)]


def _pad_taps(conv_w):
    return jnp.pad(conv_w, ((0, SUBLANES - conv_w.shape[0]), (0, 0)))


def _ffn_mid_fwd(u0, conv_w, conv_b, name):
    s, width = u0.shape
    half = width // 2
    tm = _pick(s, (FFN_ROWS, 128))
    per = tm // SUBLANES

    def body(w_ref, b_ref, u_ref, halo_ref, a_ref, pre_ref):
        keep = pl.program_id(0) > 0
        for c0 in range(0, half, FFN_COLS):
            vals = []
            for base in (c0, half + c0):
                cols = slice(base, base + FFN_COLS)
                halo = jnp.where(keep, halo_ref[:, cols], 0.0)
                taps = _conv_taps(u_ref[:, cols], halo, 3)
                vals.append(sum(w_ref[k:k + 1, cols] * taps[k] for k in range(3)) + b_ref[:, cols])
                pre_ref[:, cols] = vals[-1].astype(pre_ref.dtype)
            gate, up = vals
            a_ref[:, c0:c0 + FFN_COLS] = (gate * _sigmoid(gate) * up).astype(a_ref.dtype)

    return pl.pallas_call(
        body, name=name,
        out_shape=[jax.ShapeDtypeStruct((s, half), BF16), jax.ShapeDtypeStruct((s, width), BF16)], grid=(s // tm,),
        in_specs=[pl.BlockSpec((SUBLANES, width), lambda i: (0, 0)), pl.BlockSpec((1, width), lambda i: (0, 0)),
                  pl.BlockSpec((tm, width), lambda i: (i, 0)),
                  pl.BlockSpec((SUBLANES, width), lambda i: (jnp.maximum(i * per - 1, 0), 0))],
        out_specs=[pl.BlockSpec((tm, half), lambda i: (i, 0)), pl.BlockSpec((tm, width), lambda i: (i, 0))],
        compiler_params=_params("parallel"),
    )(_pad_taps(conv_w), conv_b.reshape(1, width), u0, u0)


def _ffn_mid_bwd(u0, pre, conv_w, da, name):
    s, width = u0.shape
    half = width // 2
    tm = _pick(s, (FFN_ROWS, 128))
    nt = s // tm

    def body(w_ref, u_ref, pre_ref, da_ref, du0_ref, dw_ref, db_ref, carry_ref):
        first = pl.program_id(0) == 0
        for c0 in range(0, half, FFN_COLS):
            dav = da_ref[:, c0:c0 + FFN_COLS].astype(F32)
            gate = pre_ref[:, c0:c0 + FFN_COLS].astype(F32)
            up = pre_ref[:, half + c0:half + c0 + FFN_COLS].astype(F32)
            sig = _sigmoid(gate)
            dus = [dav * up * sig * (1.0 + gate * (1.0 - sig)), dav * gate * sig]
            for base, du in zip((c0, half + c0), dus):
                cols = slice(base, base + FFN_COLS)
                nxt = jnp.where(first, 0.0, carry_ref[:, cols])
                ahead = [_shift_up(du, nxt, 2), _shift_up(du, nxt, 1), du]
                du0 = sum(w_ref[k:k + 1, cols] * ahead[k] for k in range(3))
                du0_ref[:, cols] = du0.astype(du0_ref.dtype)
                carry_ref[:, cols] = du[0:SUBLANES, :]
                x = u_ref[:, cols]
                dwp = jnp.concatenate([jnp.sum(ahead[k] * x, axis=0, keepdims=True) for k in range(3)]
                                      + [jnp.zeros((SUBLANES - 3, FFN_COLS), F32)], axis=0)
                dbp = jnp.sum(du, axis=0, keepdims=True)

                @pl.when(first)
                def _():
                    dw_ref[:, cols] = dwp
                    db_ref[:, cols] = dbp

                @pl.when(jnp.logical_not(first))
                def _():
                    dw_ref[:, cols] += dwp
                    db_ref[:, cols] += dbp

    rev = lambda i: nt - 1 - i
    du0, dw, db = pl.pallas_call(
        body, name=name,
        out_shape=[jax.ShapeDtypeStruct((s, width), BF16), jax.ShapeDtypeStruct((SUBLANES, width), F32),
                   jax.ShapeDtypeStruct((1, width), F32)],
        grid=(nt,),
        in_specs=[pl.BlockSpec((SUBLANES, width), lambda i: (0, 0)),
                  pl.BlockSpec((tm, width), lambda i: (rev(i), 0)), pl.BlockSpec((tm, width), lambda i: (rev(i), 0)),
                  pl.BlockSpec((tm, half), lambda i: (rev(i), 0))],
        out_specs=[pl.BlockSpec((tm, width), lambda i: (rev(i), 0)),
                   pl.BlockSpec((SUBLANES, width), lambda i: (0, 0)), pl.BlockSpec((1, width), lambda i: (0, 0))],
        scratch_shapes=[pltpu.VMEM((SUBLANES, width), F32)],
        compiler_params=_params("arbitrary"),
    )(_pad_taps(conv_w), u0, pre, da)
    return du0, dw[:3], db.reshape(width)


@functools.partial(jax.custom_vjp, nondiff_argnums=(3,))
def ffn_mid(u0, conv_w, conv_b, name):
    return _ffn_mid_fwd(u0, conv_w, conv_b, name + "_fwd")[0]


def _ffn_mid_vjp_fwd(u0, conv_w, conv_b, name):
    a, pre = _ffn_mid_fwd(u0, conv_w, conv_b, name + "_fwd")
    return a, (u0, pre, conv_w)


def _ffn_mid_vjp_bwd(name, res, da):
    u0, pre, conv_w = res
    return _ffn_mid_bwd(u0, pre, conv_w, da, name + "_bwd")


ffn_mid.defvjp(_ffn_mid_vjp_fwd, _ffn_mid_vjp_bwd)


NEG = -1e30
HEAD_SLICES = [slice(hh * ATTN_HEAD_DIM, (hh + 1) * ATTN_HEAD_DIM) for hh in range(ATTN_HEADS)]
ATTN_SCALE = ATTN_HEAD_DIM ** -0.5
QKV_GROUP = 3 * ATTN_OUT_WIDTH


def rope_table(seq, d):
    pos = (jnp.arange(seq // d, dtype=jnp.int32)[None, :] * d + jnp.arange(d, dtype=jnp.int32)[:, None])
    inv_freq = ROPE_THETA ** (-jnp.arange(0, ROPE_DIM, 2, dtype=F32) / ROPE_DIM)
    ang = pos.reshape(seq).astype(F32)[:, None] * inv_freq[None, :]
    cos, sin = jnp.cos(ang), jnp.sin(ang)
    half = ROPE_DIM // 2
    ones = jnp.ones((seq, ATTN_HEAD_DIM - ROPE_DIM), F32)
    zero = lambda n: jnp.zeros((seq, n), F32)
    return jnp.concatenate([cos, cos, ones, -sin, zero(ATTN_HEAD_DIM - half),
                            zero(half), sin, zero(ATTN_HEAD_DIM - ROPE_DIM)], axis=1)


def _rope(t, tab, sign):
    half = ROPE_DIM // 2
    return t * tab[:, 0:128] + sign * (pltpu.roll(t, ATTN_HEAD_DIM - half, axis=1) * tab[:, 128:256]
                                       + pltpu.roll(t, half, axis=1) * tab[:, 256:384])


def _to_dilated(a, d):
    s = a.shape[0]
    return a if d == 1 else a.reshape(s // d, d, -1).transpose(1, 0, 2).reshape(s, -1)


def _from_dilated(a, d):
    s = a.shape[0]
    return a if d == 1 else a.reshape(d, s // d, -1).transpose(1, 0, 2).reshape(s, -1)


def _dot_nt(a, b):
    return lax.dot_general(a, b, (((1,), (1,)), ((), ())), preferred_element_type=F32)


def _dot_tn(a, b):
    return lax.dot_general(a, b, (((0,), (0,)), ((), ())), preferred_element_type=F32)


def _dot(a, b):
    return jnp.dot(a, b, preferred_element_type=F32)


def _window_mask(has_prev):
    ii = lax.broadcasted_iota(jnp.int32, (ATTN_BLOCK, 2 * ATTN_BLOCK), 0)
    jj = lax.broadcasted_iota(jnp.int32, (ATTN_BLOCK, 2 * ATTN_BLOCK), 1)
    in_window = jnp.logical_and(jj >= ii, jj <= ii + ATTN_BLOCK)
    return jnp.logical_and(in_window, jnp.logical_or(jj >= ATTN_BLOCK, has_prev))


def _both(prev_ref, cur_ref, hs):
    return jnp.concatenate([prev_ref[:, hs], cur_ref[:, hs]], axis=0)


def _attn_group_fwd(qkv, d, name):
    s = qkv.shape[0]
    nb = s // d // ATTN_BLOCK

    def body(q_ref, kc_ref, kp_ref, vc_ref, vp_ref, o_ref, lse_ref):
        mask = _window_mask(pl.program_id(1) > 0)
        lane = lax.broadcasted_iota(jnp.int32, (ATTN_BLOCK, 128), 1)
        lse_tile = jnp.zeros((ATTN_BLOCK, 128), F32)
        for hh, hs in enumerate(HEAD_SLICES):
            sc = jnp.where(mask, _dot_nt(q_ref[:, hs], _both(kp_ref, kc_ref, hs)) * ATTN_SCALE, NEG)
            m = jnp.max(sc, axis=1, keepdims=True)
            p = jnp.exp(sc - m)
            den = jnp.sum(p, axis=1, keepdims=True)
            o_ref[:, hs] = _dot(p.astype(BF16), _both(vp_ref, vc_ref, hs)) / den
            lse_tile = jnp.where(lane == hh, m + jnp.log(den), lse_tile)
        lse_ref[...] = lse_tile

    cur = lambda t: pl.BlockSpec((ATTN_BLOCK, ATTN_OUT_WIDTH), lambda r, n: (r * nb + n, t))
    prv = lambda t: pl.BlockSpec((ATTN_BLOCK, ATTN_OUT_WIDTH), lambda r, n: (r * nb + jnp.maximum(n - 1, 0), t))
    return pl.pallas_call(
        body, name=name,
        out_shape=[jax.ShapeDtypeStruct((s, ATTN_OUT_WIDTH), F32), jax.ShapeDtypeStruct((s, 128), F32)],
        grid=(d, nb), in_specs=[cur(0), cur(1), prv(1), cur(2), prv(2)],
        out_specs=[pl.BlockSpec((ATTN_BLOCK, ATTN_OUT_WIDTH), lambda r, n: (r * nb + n, 0)),
                   pl.BlockSpec((ATTN_BLOCK, 128), lambda r, n: (r * nb + n, 0))],
        compiler_params=_params("parallel", "parallel"),
    )(qkv, qkv, qkv, qkv, qkv)


def _attn_combine(os_, lses, name):
    s = os_[0].shape[0]
    tm = _pick(s, (256, 128))
    ng = len(os_)

    def body(*refs):
        o_refs, l_refs, (o_ref, lse_ref) = refs[:ng], refs[ng:2 * ng], refs[2 * ng:]
        lane = lax.broadcasted_iota(jnp.int32, (tm, 128), 1)
        lse_tile = jnp.zeros((tm, 128), F32)
        for hh, hs in enumerate(HEAD_SLICES):
            ls = [l_ref[:, hh:hh + 1] for l_ref in l_refs]
            m = functools.reduce(jnp.maximum, ls)
            ws = [jnp.exp(l - m) for l in ls]
            tot = functools.reduce(lambda a, b: a + b, ws)
            acc = functools.reduce(lambda a, b: a + b, [o_r[:, hs] * w for o_r, w in zip(o_refs, ws)])
            o_ref[:, hs] = (acc / tot).astype(o_ref.dtype)
            lse_tile = jnp.where(lane == hh, m + jnp.log(tot), lse_tile)
        lse_ref[...] = lse_tile

    wide = pl.BlockSpec((tm, ATTN_OUT_WIDTH), lambda i: (i, 0))
    thin = pl.BlockSpec((tm, 128), lambda i: (i, 0))
    return pl.pallas_call(
        body, name=name,
        out_shape=[jax.ShapeDtypeStruct((s, ATTN_OUT_WIDTH), BF16), jax.ShapeDtypeStruct((s, 128), F32)],
        grid=(s // tm,), in_specs=[wide] * ng + [thin] * ng, out_specs=[wide, thin],
        compiler_params=_params("parallel"),
    )(*os_, *lses)


def _attn_delta(do, o, name):
    s = do.shape[0]
    tm = _pick(s, (256, 128))

    def body(do_ref, o_ref, out_ref):
        lane = lax.broadcasted_iota(jnp.int32, (tm, 128), 1)
        tile = jnp.zeros((tm, 128), F32)
        for hh, hs in enumerate(HEAD_SLICES):
            prod = do_ref[:, hs].astype(F32) * o_ref[:, hs].astype(F32)
            tile = jnp.where(lane == hh, jnp.sum(prod, axis=1, keepdims=True), tile)
        out_ref[...] = tile

    wide = pl.BlockSpec((tm, ATTN_OUT_WIDTH), lambda i: (i, 0))
    return pl.pallas_call(
        body, name=name, out_shape=jax.ShapeDtypeStruct((s, 128), F32), grid=(s // tm,),
        in_specs=[wide, wide], out_specs=pl.BlockSpec((tm, 128), lambda i: (i, 0)),
        compiler_params=_params("parallel"),
    )(do, o)


def _attn_group_bwd(qkv, do, lse, delta, tab, d, name):
    s = qkv.shape[0]
    nb = s // d // ATTN_BLOCK
    wide = ATTN_OUT_WIDTH

    def body(q_ref, qn_ref, k_ref, v_ref, do_ref, don_ref, lse_ref, lsen_ref, dl_ref, dln_ref, tab_ref,
             out_ref, carry_ref):
        n = pl.program_id(1)

        @pl.when(n == 0)
        def _():
            carry_ref[...] = jnp.zeros_like(carry_ref)

        rows = lax.broadcasted_iota(jnp.int32, (2 * ATTN_BLOCK, ATTN_BLOCK), 0)
        keys = lax.broadcasted_iota(jnp.int32, (2 * ATTN_BLOCK, ATTN_BLOCK), 1)
        own = jnp.logical_and(rows < ATTN_BLOCK, keys <= rows)
        nxt = jnp.logical_and(rows >= ATTN_BLOCK, jnp.logical_and(keys >= rows - ATTN_BLOCK, n + 1 < nb))
        mask = jnp.logical_or(own, nxt)
        both = lambda a_ref, b_ref, cols: jnp.concatenate([a_ref[:, cols], b_ref[:, cols]], axis=0)
        for hh, hs in enumerate(HEAD_SLICES):
            one = slice(hh, hh + 1)
            q2, do2 = both(q_ref, qn_ref, hs), both(do_ref, don_ref, hs)
            kh, vh = k_ref[:, hs], v_ref[:, hs]
            sc = jnp.where(mask, _dot_nt(q2, kh) * ATTN_SCALE, NEG)
            p = jnp.exp(sc - both(lse_ref, lsen_ref, one))
            ds = (p * (_dot_nt(do2, vh) - both(dl_ref, dln_ref, one)) * ATTN_SCALE).astype(BF16)
            dq2 = _dot(ds, kh)
            dq = carry_ref[:, hs] + dq2[:ATTN_BLOCK]
            carry_ref[:, hs] = dq2[ATTN_BLOCK:]
            out_ref[:, hs] = _rope(dq, tab_ref[...], -1.0).astype(out_ref.dtype)
            out_ref[:, wide + hh * ATTN_HEAD_DIM:wide + (hh + 1) * ATTN_HEAD_DIM] = _rope(
                _dot_tn(ds, q2), tab_ref[...], -1.0).astype(out_ref.dtype)
            out_ref[:, 2 * wide + hh * ATTN_HEAD_DIM:2 * wide + (hh + 1) * ATTN_HEAD_DIM] = _dot_tn(
                p.astype(BF16), do2).astype(out_ref.dtype)

    def spec(width, row, col):
        return pl.BlockSpec((ATTN_BLOCK, width), lambda r, n: (r * nb + row(n), col))

    cur = lambda n: n
    nxt_block = lambda n: jnp.minimum(n + 1, nb - 1)
    return pl.pallas_call(
        body, name=name, out_shape=jax.ShapeDtypeStruct((s, QKV_GROUP), BF16), grid=(d, nb),
        in_specs=[spec(wide, cur, 0), spec(wide, nxt_block, 0), spec(wide, cur, 1), spec(wide, cur, 2),
                  spec(wide, cur, 0), spec(wide, nxt_block, 0), spec(128, cur, 0), spec(128, nxt_block, 0),
                  spec(128, cur, 0), spec(128, nxt_block, 0), spec(384, cur, 0)],
        out_specs=spec(QKV_GROUP, cur, 0),
        scratch_shapes=[pltpu.VMEM((ATTN_BLOCK, wide), F32)],
        compiler_params=_params("parallel", "arbitrary"),
    )(qkv, qkv, qkv, qkv, do, do, lse, lse, delta, delta, tab)


def _attn_core_fwd(qkvs, name):
    os_, lses = [], []
    for g, (qkv, d) in enumerate(zip(qkvs, ATTN_DILATIONS)):
        o_g, lse_g = _attn_group_fwd(qkv, d, f"{name}_fwd{g}")
        os_.append(_from_dilated(o_g, d))
        lses.append(_from_dilated(lse_g, d))
    o, lse = _attn_combine(os_, lses, name + "_combine")
    return o, (tuple(qkvs), o, lse)


@functools.partial(jax.custom_vjp, nondiff_argnums=(1,))
def attn_core(qkvs, name):
    return _attn_core_fwd(qkvs, name)[0]


def _attn_core_vjp_fwd(qkvs, name):
    return _attn_core_fwd(qkvs, name)


def _attn_core_vjp_bwd(name, res, do):
    rot, o, lse = res
    delta = _attn_delta(do, o, name + "_delta")
    out = []
    for g, (qkv, d) in enumerate(zip(rot, ATTN_DILATIONS)):
        out.append(_attn_group_bwd(qkv, _to_dilated(do, d), _to_dilated(lse, d), _to_dilated(delta, d),
                                   rope_table(o.shape[0], d), d, f"{name}_bwd{g}"))
    return (tuple(out),)


attn_core.defvjp(_attn_core_vjp_fwd, _attn_core_vjp_bwd)


def attention_mixer_p(x, h, lin, j, carries, tag):
    qkvs = tuple(lin(_to_dilated(h, d), ('attn_w_qkv', j), BF16, f"{tag}_qkv{g}",
                     cols=slice(g * QKV_GROUP, (g + 1) * QKV_GROUP), carry=carries[g],
                     rope=rope_table(h.shape[0], d))
                 for g, d in enumerate(ATTN_DILATIONS))
    return lin(attn_core(qkvs, tag), ('attn_w_o', j), F32, tag + "_o", res=x)


def first_attention_mixer(x, h, lin, w, ready, carries, tag):
    s = h.shape[0]
    qkvs, moved = qkv_first(tuple(_to_dilated(h, d) for d in ATTN_DILATIONS), w['attn_w_qkv'][0],
                            tuple(rope_table(s, d) for d in ATTN_DILATIONS),
                            tuple(w[n][j] for n, j in carries), tag + "_qkv")
    for (n, j), parts in zip(carries, moved):
        ready[(n, j)] = _full_weight(n, parts)
    return lin(attn_core(qkvs, tag), ('attn_w_o', 0), F32, tag + "_o", res=x)


SSM_CONV_TAPS = 4
SSM_COL_BLOCK = 2048
SSM_PAIRS = SSM_HEADS // 2
SSM_DT_BLOCK = (SSM_D_INNER + SSM_CONV_DIM) // 128


def _ssm_conv_fwd(zx, conv_w, conv_b, name):
    s = zx.shape[0]
    tm = _pick(s, (256, 128))
    per = tm // SUBLANES
    ncb = SSM_CONV_DIM // SSM_COL_BLOCK

    def body(w_ref, b_ref, x_ref, halo_ref, o_ref, pre_ref):
        keep = pl.program_id(1) > 0
        for c0 in range(0, SSM_COL_BLOCK, FFN_COLS):
            cols = slice(c0, c0 + FFN_COLS)
            halo = jnp.where(keep, halo_ref[:, cols], 0.0)
            taps = _conv_taps(x_ref[:, cols], halo, SSM_CONV_TAPS)
            pre = sum(w_ref[k:k + 1, cols] * taps[k] for k in range(SSM_CONV_TAPS)) + b_ref[:, cols]
            pre_ref[:, cols] = pre.astype(pre_ref.dtype)
            o_ref[:, cols] = pre * _sigmoid(pre)

    blk = pl.BlockSpec((tm, SSM_COL_BLOCK), lambda j, i: (i, j))
    return pl.pallas_call(
        body, name=name,
        out_shape=[jax.ShapeDtypeStruct((s, SSM_CONV_DIM), F32), jax.ShapeDtypeStruct((s, SSM_CONV_DIM), BF16)],
        grid=(ncb, s // tm),
        in_specs=[pl.BlockSpec((SUBLANES, SSM_COL_BLOCK), lambda j, i: (0, j)),
                  pl.BlockSpec((1, SSM_COL_BLOCK), lambda j, i: (0, j)),
                  pl.BlockSpec((tm, SSM_COL_BLOCK), lambda j, i: (i, j + 1)),
                  pl.BlockSpec((SUBLANES, SSM_COL_BLOCK), lambda j, i: (jnp.maximum(i * per - 1, 0), j + 1))],
        out_specs=[blk, blk], compiler_params=_params("parallel", "parallel"),
    )(_pad_taps(conv_w), conv_b.reshape(1, SSM_CONV_DIM), zx, zx)


def _ssm_conv_bwd(zx, pre, conv_w, dact, name):
    s = zx.shape[0]
    tm = _pick(s, (256, 128))
    nt = s // tm
    ncb = SSM_CONV_DIM // SSM_COL_BLOCK
    nk = SSM_CONV_TAPS

    def body(w_ref, x_ref, pre_ref, da_ref, dx_ref, dw_ref, db_ref, carry_ref):
        first = pl.program_id(1) == 0
        for c0 in range(0, SSM_COL_BLOCK, FFN_COLS):
            cols = slice(c0, c0 + FFN_COLS)
            prev = pre_ref[:, cols].astype(F32)
            sig = _sigmoid(prev)
            dpre = da_ref[:, cols] * sig * (1.0 + prev * (1.0 - sig))
            nxt = jnp.where(first, 0.0, carry_ref[:, cols])
            ahead = [_shift_up(dpre, nxt, nk - 1 - k) for k in range(nk - 1)] + [dpre]
            dx_ref[:, cols] = sum(w_ref[k:k + 1, cols] * ahead[k] for k in range(nk)).astype(dx_ref.dtype)
            carry_ref[:, cols] = dpre[0:SUBLANES, :]
            x = x_ref[:, cols]
            dwp = jnp.concatenate([jnp.sum(ahead[k] * x, axis=0, keepdims=True) for k in range(nk)]
                                  + [jnp.zeros((SUBLANES - nk, FFN_COLS), F32)], axis=0)
            dbp = jnp.sum(dpre, axis=0, keepdims=True)

            @pl.when(first)
            def _():
                dw_ref[:, cols] = dwp
                db_ref[:, cols] = dbp

            @pl.when(jnp.logical_not(first))
            def _():
                dw_ref[:, cols] += dwp
                db_ref[:, cols] += dbp

    rev = lambda i: nt - 1 - i
    dx, dw, db = pl.pallas_call(
        body, name=name,
        out_shape=[jax.ShapeDtypeStruct((s, SSM_CONV_DIM), BF16), jax.ShapeDtypeStruct((SUBLANES, SSM_CONV_DIM), F32),
                   jax.ShapeDtypeStruct((1, SSM_CONV_DIM), F32)],
        grid=(ncb, nt),
        in_specs=[pl.BlockSpec((SUBLANES, SSM_COL_BLOCK), lambda j, i: (0, j)),
                  pl.BlockSpec((tm, SSM_COL_BLOCK), lambda j, i: (rev(i), j + 1)),
                  pl.BlockSpec((tm, SSM_COL_BLOCK), lambda j, i: (rev(i), j)),
                  pl.BlockSpec((tm, SSM_COL_BLOCK), lambda j, i: (rev(i), j))],
        out_specs=[pl.BlockSpec((tm, SSM_COL_BLOCK), lambda j, i: (rev(i), j)),
                   pl.BlockSpec((SUBLANES, SSM_COL_BLOCK), lambda j, i: (0, j)),
                   pl.BlockSpec((1, SSM_COL_BLOCK), lambda j, i: (0, j))],
        scratch_shapes=[pltpu.VMEM((SUBLANES, SSM_COL_BLOCK), F32)],
        compiler_params=_params("parallel", "arbitrary"),
    )(_pad_taps(conv_w), zx, pre, dact)
    return dx, dw[:nk], db.reshape(SSM_CONV_DIM)


def _ssd_chunk(xs, bms, cms, dt_raw, dtb, alog, dsk, states):
    q = SSM_CHUNK
    lane = lax.broadcasted_iota(jnp.int32, (1, 128), 1)
    row = lax.broadcasted_iota(jnp.int32, (q, 1), 0)
    ii = lax.broadcasted_iota(jnp.int32, (q, q), 0)
    jj = lax.broadcasted_iota(jnp.int32, (q, q), 1)
    tril = ii >= jj
    left = lane < SSM_HEAD_DIM
    last_row = (row == q - 1).astype(F32)

    def lanes_of(mat, h):
        pick = (lane == h).astype(F32)
        return jnp.broadcast_to(jnp.sum(mat * pick, axis=1, keepdims=True), mat.shape)

    def rows_of(mat_t, h):
        pick = (row == h).astype(F32)
        return jnp.broadcast_to(jnp.sum(mat_t * pick, axis=0, keepdims=True), mat_t.shape)

    v = dt_raw + dtb
    dt = jnp.maximum(v, 0.0) + jnp.log(1.0 + jnp.exp(-jnp.abs(v)))
    adt = dt * (-jnp.exp(alog))
    acs = jnp.dot(tril.astype(F32), adt, precision=lax.Precision.HIGHEST, preferred_element_type=F32)
    acs_t = acs.T
    ys, new_states = [], []
    for pr in range(SSM_PAIRS):
        g = pr // 2
        if pr % 2 == 0:
            cb = _dot_nt(cms[g].astype(BF16), bms[g].astype(BF16))
        cols = [lanes_of(acs, 2 * pr + e) for e in range(2)]
        dts = [lanes_of(dt, 2 * pr + e) for e in range(2)]
        rws = [rows_of(acs_t, 2 * pr + e) for e in range(2)]
        lasts = [jnp.sum(c * last_row, axis=0, keepdims=True) for c in cols]
        xdt = xs[pr] * jnp.where(left, dts[0], dts[1])
        halves = [jnp.where(left, xdt, 0.0).astype(BF16), jnp.where(left, 0.0, xdt).astype(BF16)]
        y_diag, s_new = 0.0, 0.0
        for e in range(2):
            lmat = jnp.where(tril, jnp.exp(jnp.minimum(cols[e] - rws[e], 0.0)), 0.0)
            y_diag = y_diag + _dot((cb * lmat).astype(BF16), halves[e])
            decay = jnp.exp(lasts[e] - cols[e])
            s_new = s_new + _dot_tn((bms[g] * decay).astype(BF16), halves[e])
        y_off = _dot(cms[g].astype(BF16), states[pr].astype(BF16)) * jnp.where(left, jnp.exp(cols[0]), jnp.exp(cols[1]))
        skip = jnp.where(left, lanes_of(dsk, 2 * pr), lanes_of(dsk, 2 * pr + 1))
        ys.append(y_diag + y_off + xs[pr] * skip)
        new_states.append(states[pr] * jnp.where(left, jnp.exp(lasts[0]), jnp.exp(lasts[1])) + s_new)
    return tuple(ys), tuple(new_states)


def _ssm_vec(v):
    return jnp.pad(v.reshape(1, -1), ((0, 0), (0, 128 - v.shape[0])))


def _ssd_scan_fwd(act, zx, dtb, alog, dsk, name):
    s = act.shape[0]
    nc = s // SSM_CHUNK
    ng = SSM_GROUPS

    def body(act_ref, dt_ref, dtb_ref, alog_ref, dsk_ref, y_ref, st_out_ref, st_ref):
        @pl.when(pl.program_id(0) == 0)
        def _():
            st_ref[...] = jnp.zeros_like(st_ref)

        tile = lambda k: act_ref[:, k * 128:(k + 1) * 128]
        xs = [tile(k) for k in range(SSM_PAIRS)]
        bms = [tile(SSM_PAIRS + k) for k in range(ng)]
        cms = [tile(SSM_PAIRS + ng + k) for k in range(ng)]
        states = [st_ref[k] for k in range(SSM_PAIRS)]
        st_out_ref[0] = st_ref[...]
        ys, new_states = _ssd_chunk(xs, bms, cms, dt_ref[...], dtb_ref[...], alog_ref[...], dsk_ref[...], states)
        for k in range(SSM_PAIRS):
            y_ref[:, k * 128:(k + 1) * 128] = ys[k]
            st_ref[k] = new_states[k]

    vec = pl.BlockSpec((1, 128), lambda c: (0, 0))
    return pl.pallas_call(
        body, name=name,
        out_shape=[jax.ShapeDtypeStruct((s, SSM_D_INNER), F32),
                   jax.ShapeDtypeStruct((nc, SSM_PAIRS, SSM_STATE, 128), F32)],
        grid=(nc,),
        in_specs=[pl.BlockSpec((SSM_CHUNK, SSM_CONV_DIM), lambda c: (c, 0)),
                  pl.BlockSpec((SSM_CHUNK, 128), lambda c: (c, SSM_DT_BLOCK)), vec, vec, vec],
        out_specs=[pl.BlockSpec((SSM_CHUNK, SSM_D_INNER), lambda c: (c, 0)),
                   pl.BlockSpec((1, SSM_PAIRS, SSM_STATE, 128), lambda c: (c, 0, 0, 0))],
        scratch_shapes=[pltpu.VMEM((SSM_PAIRS, SSM_STATE, 128), F32)],
        compiler_params=_params("arbitrary"),
    )(act, zx, _ssm_vec(dtb), _ssm_vec(alog), _ssm_vec(dsk))


def _ssd_scan_bwd(act, zx, dtb, alog, dsk, st_in, dy, name):
    s = act.shape[0]
    nc = s // SSM_CHUNK
    ng = SSM_GROUPS

    def body(act_ref, dt_ref, dtb_ref, alog_ref, dsk_ref, st_ref, dy_ref, dact_ref, ddt_ref, dpar_ref, dst_ref):
        first = pl.program_id(0) == 0

        @pl.when(first)
        def _():
            dst_ref[...] = jnp.zeros_like(dst_ref)

        tile = lambda k: act_ref[:, k * 128:(k + 1) * 128]
        xs = [tile(k) for k in range(SSM_PAIRS)]
        bms = [tile(SSM_PAIRS + k) for k in range(ng)]
        cms = [tile(SSM_PAIRS + ng + k) for k in range(ng)]
        states = [st_ref[0, k] for k in range(SSM_PAIRS)]
        _, pullback = jax.vjp(_ssd_chunk, xs, bms, cms, dt_ref[...], dtb_ref[...], alog_ref[...], dsk_ref[...],
                              states)
        dys = tuple(dy_ref[:, k * 128:(k + 1) * 128] for k in range(SSM_PAIRS))
        dsts = tuple(dst_ref[k] for k in range(SSM_PAIRS))
        dxs, dbms, dcms, ddt, ddtb, dalog, ddsk, dstates = pullback((dys, dsts))
        for k, t in enumerate(list(dxs) + list(dbms) + list(dcms)):
            dact_ref[:, k * 128:(k + 1) * 128] = t
        ddt_ref[...] = ddt.astype(ddt_ref.dtype)
        for k in range(SSM_PAIRS):
            dst_ref[k] = dstates[k]
        dpar = jnp.concatenate([ddtb, dalog, ddsk, jnp.zeros((SUBLANES - 3, 128), F32)], axis=0)
        _accumulate(dpar_ref, dpar, first)

    rev = lambda c: nc - 1 - c
    vec = pl.BlockSpec((1, 128), lambda c: (0, 0))
    dact, ddt, dpar = pl.pallas_call(
        body, name=name,
        out_shape=[jax.ShapeDtypeStruct((s, SSM_CONV_DIM), F32), jax.ShapeDtypeStruct((s, 128), BF16),
                   jax.ShapeDtypeStruct((SUBLANES, 128), F32)],
        grid=(nc,),
        in_specs=[pl.BlockSpec((SSM_CHUNK, SSM_CONV_DIM), lambda c: (rev(c), 0)),
                  pl.BlockSpec((SSM_CHUNK, 128), lambda c: (rev(c), SSM_DT_BLOCK)), vec, vec, vec,
                  pl.BlockSpec((1, SSM_PAIRS, SSM_STATE, 128), lambda c: (rev(c), 0, 0, 0)),
                  pl.BlockSpec((SSM_CHUNK, SSM_D_INNER), lambda c: (rev(c), 0))],
        out_specs=[pl.BlockSpec((SSM_CHUNK, SSM_CONV_DIM), lambda c: (rev(c), 0)),
                   pl.BlockSpec((SSM_CHUNK, 128), lambda c: (rev(c), 0)),
                   pl.BlockSpec((SUBLANES, 128), lambda c: (0, 0))],
        scratch_shapes=[pltpu.VMEM((SSM_PAIRS, SSM_STATE, 128), F32)],
        compiler_params=_params("arbitrary"),
    )(act, zx, _ssm_vec(dtb), _ssm_vec(alog), _ssm_vec(dsk), st_in, dy)
    return dact, ddt, dpar[0, :SSM_HEADS], dpar[1, :SSM_HEADS], dpar[2, :SSM_HEADS]


SSM_NORM_GROUP = SSM_D_INNER // SSM_GROUPS


def _gated_group(y, z, w):
    g = y * (z * _sigmoid(z))
    return g * lax.rsqrt(jnp.mean(g * g, axis=-1, keepdims=True) + NORM_EPS) * w


def _gated_norm_fwd(y, zx, w, name):
    s = y.shape[0]
    tm = _pick(s, (256, 128))

    def body(y_ref, z_ref, w_ref, o_ref):
        for c0 in range(0, SSM_D_INNER, SSM_NORM_GROUP):
            cols = slice(c0, c0 + SSM_NORM_GROUP)
            o_ref[:, cols] = _gated_group(y_ref[:, cols], z_ref[:, cols], w_ref[:, cols]).astype(o_ref.dtype)

    blk = pl.BlockSpec((tm, SSM_D_INNER), lambda i: (i, 0))
    return pl.pallas_call(
        body, name=name, out_shape=jax.ShapeDtypeStruct((s, SSM_D_INNER), BF16), grid=(s // tm,),
        in_specs=[blk, blk, pl.BlockSpec((1, SSM_D_INNER), lambda i: (0, 0))], out_specs=blk,
        compiler_params=_params("parallel"),
    )(y, zx, w.reshape(1, SSM_D_INNER))


def _gated_norm_bwd(y, zx, w, dout, name):
    s = y.shape[0]
    tm = _pick(s, (256, 128))

    def body(y_ref, z_ref, w_ref, do_ref, dy_ref, dz_ref, dw_ref):
        first = pl.program_id(0) == 0
        for c0 in range(0, SSM_D_INNER, SSM_NORM_GROUP):
            cols = slice(c0, c0 + SSM_NORM_GROUP)
            _, pullback = jax.vjp(_gated_group, y_ref[:, cols], z_ref[:, cols], w_ref[:, cols])
            dyv, dzv, dwv = pullback(do_ref[:, cols].astype(F32))
            dy_ref[:, cols] = dyv
            dz_ref[:, cols] = dzv.astype(dz_ref.dtype)

            @pl.when(first)
            def _():
                dw_ref[:, cols] = dwv

            @pl.when(jnp.logical_not(first))
            def _():
                dw_ref[:, cols] += dwv

    blk = pl.BlockSpec((tm, SSM_D_INNER), lambda i: (i, 0))
    vec = pl.BlockSpec((1, SSM_D_INNER), lambda i: (0, 0))
    dy, dz, dw = pl.pallas_call(
        body, name=name,
        out_shape=[jax.ShapeDtypeStruct((s, SSM_D_INNER), F32), jax.ShapeDtypeStruct((s, SSM_D_INNER), BF16),
                   jax.ShapeDtypeStruct((1, SSM_D_INNER), F32)],
        grid=(s // tm,), in_specs=[blk, blk, vec, blk], out_specs=[blk, blk, vec],
        compiler_params=_params("arbitrary"),
    )(y, zx, w.reshape(1, SSM_D_INNER), dout)
    return dy, dz, dw.reshape(SSM_D_INNER)


def _ssm_core_fwd(zx, conv_w, conv_b, dtb, alog, dsk, norm_w, name):
    act, pre = _ssm_conv_fwd(zx, conv_w, conv_b, name + "_conv_fwd")
    y, st_in = _ssd_scan_fwd(act, zx, dtb, alog, dsk, name + "_scan_fwd")
    out = _gated_norm_fwd(y, zx, norm_w, name + "_gate_fwd")
    return out, (zx, conv_w, pre, dtb, alog, dsk, norm_w, act, y, st_in)


@functools.partial(jax.custom_vjp, nondiff_argnums=(7,))
def ssm_core(zx, conv_w, conv_b, dtb, alog, dsk, norm_w, name):
    return _ssm_core_fwd(zx, conv_w, conv_b, dtb, alog, dsk, norm_w, name)[0]


def _ssm_core_vjp_fwd(zx, conv_w, conv_b, dtb, alog, dsk, norm_w, name):
    return _ssm_core_fwd(zx, conv_w, conv_b, dtb, alog, dsk, norm_w, name)


def _ssm_core_vjp_bwd(name, res, dout):
    zx, conv_w, pre, dtb, alog, dsk, norm_w, act, y, st_in = res
    dy, dz, dnorm_w = _gated_norm_bwd(y, zx, norm_w, dout, name + "_gate_bwd")
    dact, ddt, ddtb, dalog, ddsk = _ssd_scan_bwd(act, zx, dtb, alog, dsk, st_in, dy, name + "_scan_bwd")
    dxbc, dconv_w, dconv_b = _ssm_conv_bwd(zx, pre, conv_w, dact, name + "_conv_bwd")
    pad = jnp.zeros((zx.shape[0], zx.shape[1] - dz.shape[1] - dxbc.shape[1] - ddt.shape[1]), dz.dtype)
    dzx = jnp.concatenate([dz, dxbc, ddt, pad], axis=1)
    return dzx, dconv_w, dconv_b, ddtb, dalog, ddsk, dnorm_w


ssm_core.defvjp(_ssm_core_vjp_fwd, _ssm_core_vjp_bwd)


def ssd_mixer_p(x, h, lin, j, carries, conv_w, conv_b, dtb, alog, dsk, norm_w, tag):
    zx = lin(h, ('ssm_w_in', j), F32, tag + "_in", carry=carries[0])
    return lin(ssm_core(zx, conv_w, conv_b, dtb, alog, dsk, norm_w, tag), ('ssm_w_out', j), F32, tag + "_out",
               carry=carries[1], res=x)


def _full_weight(n, parts):
    full = _join8(parts, SHARD_AXIS[n] - 1)
    if n == 'ssm_w_in':
        full = jnp.pad(full, ((0, 0), (0, SSM_IN_PAD - SSM_IN_WIDTH)))
    return full


def trunk(w, x):
    ready = {}

    def lin(a, key, out_dtype, name, cols=None, carry=None, res=None, rope=None):
        wt = ready[key] if cols is None else ready[key][:, cols]
        if carry is None:
            assert rope is None
            return linear(a, wt, res, out_dtype, name)
        y, parts = linear_x(a, wt, res, rope, w[carry[0]][carry[1]], out_dtype, name)
        ready[carry] = _full_weight(carry[0], parts)
        return y

    for i in range(DEPTH):
        x, h = norm(x, w['mix_norm_w'][i], f"mixnorm{i}")
        j = i // 2
        ffn_next = [('ffn_w_up', i), ('ffn_w_down', i)]
        if i == 0:
            x = first_attention_mixer(x, h, lin, w, ready, [('attn_w_o', 0)] + ffn_next, "attn0")
        elif i % 2 == 0:
            x = attention_mixer_p(x, h, lin, j, [('attn_w_o', j)] + ffn_next, f"attn{j}")
        else:
            x = ssd_mixer_p(x, h, lin, j, ffn_next, w['ssm_conv_w'][j], w['ssm_conv_b'][j], w['ssm_dt_bias'][j],
                            w['ssm_a_log'][j], w['ssm_d'][j], w['ssm_norm_w'][j], f"ssm{j}")
        if i + 1 == DEPTH:
            mixer_next = [None, None]
        elif i % 2 == 0:
            mixer_next = [('ssm_w_in', j), ('ssm_w_out', j)]
        else:
            mixer_next = [('attn_w_qkv', j + 1), None]
        x, h = norm(x, w['ffn_norm_w'][i], f"ffnnorm{i}")
        u0 = lin(h, ('ffn_w_up', i), F32, f"ffn{i}_up", carry=mixer_next[0])
        a = ffn_mid(u0, w['ffn_conv_w'][i], w['ffn_conv_b'][i], f"ffn{i}_mid")
        x = lin(a, ('ffn_w_down', i), F32, f"ffn{i}_down", carry=mixer_next[1], res=x)
    return x


def local_step(w, x, target):
    final_w = w['final_norm_w']
    trunk_w = {n: a for n, a in w.items() if n != 'final_norm_w'}
    xf, pullback = jax.vjp(trunk, trunk_w, x)
    loss, dxf, dfinal = loss_head(xf, final_w, target, "loss_head")
    gw, gx = pullback(dxf)
    gw['final_norm_w'] = dfinal
    return loss, gw, gx


def _adam_math(w, g, m, v):
    m = ADAM_B1 * m + (1.0 - ADAM_B1) * g
    v = ADAM_B2 * v + (1.0 - ADAM_B2) * (g * g)
    m_hat = m / (1.0 - ADAM_B1 ** ADAM_STEP)
    v_hat = v / (1.0 - ADAM_B2 ** ADAM_STEP)
    delta = -ADAM_LR * (m_hat / (jnp.sqrt(v_hat) + ADAM_EPS) + ADAM_WD * w)
    return delta, m, v


def _adamw_rows(g, w, m, v, name):
    r, c = w.shape
    tr = _pick(r, (256, 128, 64, 32, 16, 8))

    def body(g_ref, w_ref, m_ref, v_ref, d_out, m_out, v_out):
        delta, mm, vv = _adam_math(w_ref[...], g_ref[...], m_ref[...], v_ref[...])
        d_out[...] = delta
        m_out[...] = mm
        v_out[...] = vv

    blk = pl.BlockSpec((tr, c), lambda i: (i, 0))
    return pl.pallas_call(
        body, name=name, out_shape=[jax.ShapeDtypeStruct((r, c), F32)] * 3, grid=(r // tr,),
        in_specs=[blk] * 4, out_specs=[blk] * 3, compiler_params=_params("parallel"),
    )(g, w, m, v)


def _sum8(pieces, name):
    _, r, c = pieces.shape

    def body(p_ref, o_ref):
        g = p_ref[0]
        for j in range(1, N_DEV):
            g = g + p_ref[j]
        o_ref[...] = g

    return pl.pallas_call(
        body, name=name, out_shape=jax.ShapeDtypeStruct((r, c), F32),
        in_specs=[pl.BlockSpec(memory_space=pltpu.VMEM)], out_specs=pl.BlockSpec(memory_space=pltpu.VMEM),
    )(pieces)


def _adamw_plain(g, w, m, v, name):
    def body(g_ref, w_ref, m_ref, v_ref, d_out, m_out, v_out):
        delta, mm, vv = _adam_math(w_ref[...], g_ref[...], m_ref[...], v_ref[...])
        d_out[...] = delta
        m_out[...] = mm
        v_out[...] = vv

    vm = pl.BlockSpec(memory_space=pltpu.VMEM)
    return pl.pallas_call(
        body, name=name, out_shape=[jax.ShapeDtypeStruct(g.shape, F32)] * 3,
        in_specs=[vm] * 4, out_specs=[vm] * 3,
    )(g, w, m, v)


def _join8(parts, axis):
    t = jnp.moveaxis(parts, 0, axis)
    shp = t.shape
    return t.reshape(shp[:axis] + (shp[axis] * shp[axis + 1],) + shp[axis + 2:])


def _pack(arrs, lead, mult):
    flat = jnp.concatenate([a.reshape(a.shape[:lead] + (-1,)) for a in arrs], axis=-1)
    return _pad_rows(flat, mult)


def _unpack(buf, shapes, lead):
    flat = buf.reshape(buf.shape[:lead] + (-1,))
    out, off = [], 0
    for shp in shapes:
        n = math.prod(shp)
        out.append(flat[..., off:off + n].reshape(flat.shape[:lead] + tuple(shp)))
        off += n
    return out


def _own_shard(full, axis):
    size = full.shape[axis] // N_DEV
    return lax.dynamic_slice_in_dim(full, _my_index() * size, size, axis)


def kernel(x, mix_norm_w, attn_w_qkv, attn_w_o, ssm_w_in, ssm_conv_w, ssm_conv_b, ssm_dt_bias, ssm_a_log, ssm_d, ssm_norm_w, ssm_w_out, ffn_norm_w, ffn_w_up, ffn_conv_w, ffn_conv_b, ffn_w_down, final_norm_w, loss_target, m_mix_norm_w, m_attn_w_qkv, m_attn_w_o, m_ssm_w_in, m_ssm_conv_w, m_ssm_conv_b, m_ssm_dt_bias, m_ssm_a_log, m_ssm_d, m_ssm_norm_w, m_ssm_w_out, m_ffn_norm_w, m_ffn_w_up, m_ffn_conv_w, m_ffn_conv_b, m_ffn_w_down, m_final_norm_w, v_mix_norm_w, v_attn_w_qkv, v_attn_w_o, v_ssm_w_in, v_ssm_conv_w, v_ssm_conv_b, v_ssm_dt_bias, v_ssm_a_log, v_ssm_d, v_ssm_norm_w, v_ssm_w_out, v_ffn_norm_w, v_ffn_w_up, v_ffn_conv_w, v_ffn_conv_b, v_ffn_w_down, v_final_norm_w):
    args = dict(locals())
    w_sh = {n: args[n] for n in WEIGHT_NAMES}
    m_sh = {n: args["m_" + n] for n in WEIGHT_NAMES}
    v_sh = {n: args["v_" + n] for n in WEIGHT_NAMES}

    small_shapes = [w_sh[n].shape for n in SMALL_SHARDED]
    small = _exchange(_pack([w_sh[n] for n in SMALL_SHARDED], 0, 8), True, "gather_small")
    full = {n: w_sh[n] for n in SMALL if SHARD_AXIS[n] is None}
    for n, parts in zip(SMALL_SHARDED, _unpack(small, small_shapes, 1)):
        full[n] = _join8(parts, SHARD_AXIS[n])
    for n in BIG:
        full[n] = [w_sh[n][j] for j in range(w_sh[n].shape[0])]

    loss, gw, gx = local_step(full, x[0], loss_target[0])
    loss = lax.psum(loss, ("x", "y", "c"))
    for n in BIG:
        gw[n] = jnp.stack(gw[n])

    grads, deltas, new_m, new_v = {}, {}, {}, {}
    for n in BIG:
        shp = w_sh[n].shape
        two_d = (shp[0] * shp[1], shp[2])
        outs = _adamw_rows(*[t.reshape(two_d) for t in (gw[n], w_sh[n], m_sh[n], v_sh[n])], "adamw_" + n)
        grads[n] = gw[n]
        deltas[n], new_m[n], new_v[n] = [o.reshape(shp) for o in outs]

    small_full_shapes = [gw[n].shape for n in SMALL]
    gsmall = _exchange(_pack([gw[n] for n in SMALL], 0, 8), True, "gather_small_grads")
    gsmall = _unpack(_sum8(gsmall, "sum_small_grads"), small_full_shapes, 0)
    for n, g in zip(SMALL, gsmall):
        grads[n] = g if SHARD_AXIS[n] is None else _own_shard(g, SHARD_AXIS[n])
    shapes = [w_sh[n].shape for n in SMALL]
    outs = _adamw_plain(*[_pack([d[n] for n in SMALL], 0, 8) for d in (grads, w_sh, m_sh, v_sh)], "adamw_small")
    for d, buf in zip((deltas, new_m, new_v), outs):
        for n, a in zip(SMALL, _unpack(buf, shapes, 0)):
            d[n] = a

    return (loss, gx[None], *[grads[n] for n in WEIGHT_NAMES], *[deltas[n] for n in WEIGHT_NAMES],
            *[new_m[n] for n in WEIGHT_NAMES], *[new_v[n] for n in WEIGHT_NAMES])
```

```python
import functools
import math

import jax
import jax.numpy as jnp
from jax import lax
from jax.experimental import pallas as pl
from jax.experimental.pallas import tpu as pltpu

F32 = jnp.float32
BF16 = jnp.bfloat16
N_DEV = 8
MESH_ID = pl.DeviceIdType.MESH

DEPTH = 4
ATTN_HEADS = 8
ATTN_HEAD_DIM = 128
ATTN_DILATIONS = (1, 4, 16)
ATTN_BLOCK = 128
ROPE_THETA = 500000.0
ROPE_DIM = 32
ATTN_OUT_WIDTH = 1024
SSM_D_INNER = 2048
SSM_HEAD_DIM = 64
SSM_HEADS = 32
SSM_STATE = 128
SSM_GROUPS = 8
SSM_CHUNK = 128
SSM_CONV_DIM = 4096
SSM_IN_WIDTH = 6176
SSM_IN_PAD = 6400
NORM_EPS = 1e-5
ADAM_LR = 0.001
ADAM_B1 = 0.9
ADAM_B2 = 0.999
ADAM_EPS = 1e-08
ADAM_WD = 0.01
ADAM_STEP = 10

WEIGHT_NAMES = ['mix_norm_w', 'attn_w_qkv', 'attn_w_o', 'ssm_w_in', 'ssm_conv_w', 'ssm_conv_b', 'ssm_dt_bias',
                'ssm_a_log', 'ssm_d', 'ssm_norm_w', 'ssm_w_out', 'ffn_norm_w', 'ffn_w_up', 'ffn_conv_w',
                'ffn_conv_b', 'ffn_w_down', 'final_norm_w']
SHARD_AXIS = {'mix_norm_w': None, 'attn_w_qkv': 2, 'attn_w_o': 1, 'ssm_w_in': 2, 'ssm_conv_w': 2, 'ssm_conv_b': 1,
              'ssm_dt_bias': None, 'ssm_a_log': None, 'ssm_d': None, 'ssm_norm_w': 1, 'ssm_w_out': 1,
              'ffn_norm_w': None, 'ffn_w_up': 2, 'ffn_conv_w': 2, 'ffn_conv_b': None, 'ffn_w_down': 1,
              'final_norm_w': None}
BIG = ['attn_w_qkv', 'attn_w_o', 'ssm_w_in', 'ssm_w_out', 'ffn_w_up', 'ffn_w_down']
SMALL = [n for n in WEIGHT_NAMES if n not in BIG]
SMALL_SHARDED = [n for n in SMALL if SHARD_AXIS[n] is not None]
LANES = 1024


def _my_index():
    return 4 * lax.axis_index("x") + 2 * lax.axis_index("y") + lax.axis_index("c")


def _peer(k):
    x, y, c = lax.axis_index("x"), lax.axis_index("y"), lax.axis_index("c")
    return (x ^ ((k >> 2) & 1), y ^ ((k >> 1) & 1), c ^ (k & 1))


def _exchange(src, gather, name):
    def body(src_ref, out_ref, send_sems, recv_sems, local_sem):
        start, wait = _exchange_copies(src_ref, out_ref, send_sems, recv_sems, local_sem, gather)
        start()
        wait()

    return pl.pallas_call(
        body, name=name,
        out_shape=_exchange_out(src, gather),
        in_specs=[pl.BlockSpec(memory_space=pl.ANY)],
        out_specs=pl.BlockSpec(memory_space=pl.ANY),
        scratch_shapes=list(EXCHANGE_SEMS),
    )(src)


EXCHANGE_SEMS = (pltpu.SemaphoreType.DMA((N_DEV - 1,)), pltpu.SemaphoreType.DMA((N_DEV - 1,)),
                 pltpu.SemaphoreType.DMA)


def _exchange_out(src, gather):
    return jax.ShapeDtypeStruct((N_DEV,) + (src.shape if gather else src.shape[1:]), src.dtype)


def _exchange_copies(src_ref, out_ref, send_sems, recv_sems, local_sem, gather):
    me = _my_index()

    def piece(j):
        return src_ref if gather else src_ref.at[j]

    def remote(k, slab):
        return pltpu.make_async_remote_copy(
            src_ref=piece(me ^ k), dst_ref=out_ref.at[slab], send_sem=send_sems.at[k - 1],
            recv_sem=recv_sems.at[k - 1], device_id=_peer(k), device_id_type=MESH_ID)

    mine = pltpu.make_async_copy(piece(me), out_ref.at[me], local_sem)
    arrivals = {k: remote(k, me ^ k) for k in range(1, N_DEV)}
    if not gather:
        sends = [remote(k, me) for k in range(1, N_DEV)]

        def start():
            mine.start()
            for cp in sends:
                cp.start()

        def wait():
            for cp in arrivals.values():
                cp.wait_recv()
            for cp in sends:
                cp.wait_send()
            mine.wait()

        return start, wait

    far = (2, 4, 6)
    sends = [remote(k, me) for k in (1,) + far]
    passed_on = [pltpu.make_async_remote_copy(
        src_ref=out_ref.at[me ^ k], dst_ref=out_ref.at[me ^ k], send_sem=send_sems.at[k],
        recv_sem=recv_sems.at[k], device_id=_peer(1), device_id_type=MESH_ID) for k in far]

    def start():
        mine.start()
        for cp in sends:
            cp.start()

    def wait():
        for k, cp in zip(far, passed_on):
            arrivals[k].wait_recv()
            cp.start()
        for k in (1, 3, 5, 7):
            arrivals[k].wait_recv()
        for cp in sends + passed_on:
            cp.wait_send()
        mine.wait()

    return start, wait


def _pad_rows(flat, mult):
    n = flat.shape[-1]
    rows = -(-n // (LANES * mult)) * mult
    pad = rows * LANES - n
    flat = jnp.pad(flat, [(0, 0)] * (flat.ndim - 1) + [(0, pad)])
    return flat.reshape(flat.shape[:-1] + (rows, LANES))


MATMUL_VMEM_BUDGET = 48 * 1024 * 1024
MATMUL_TM = (1024, 1408, 512, 256, 128)
MATMUL_TN = (3200, 3072, 2816, 2048, 1536, 1408, 1280, 1024, 896, 512, 384, 256, 128)
MATMUL_TK = (3200, 3072, 2816, 2048, 1408, 1280, 1024, 896)
MATMUL_MIN_TK = 896


def _pick(n, cands):
    for c in cands:
        if n % c == 0:
            return c
    return n


def _matmul_tiles(m, n, k, a_bytes, b_bytes, out_bytes, has_res):
    tm = _pick(m, MATMUL_TM)

    def need(tn, tk):
        blocks = 2 * (tm * tk * a_bytes + tk * tn * b_bytes + tm * tn * out_bytes + has_res * tm * tn * 4)
        temps = tm * tn * 4 * (1 + (tk < k)) + (a_bytes > 2) * tm * tk * 2 + (b_bytes > 2) * tk * tn * 2
        return blocks + temps

    for tn in [c for c in MATMUL_TN if n % c == 0] + [n]:
        for tk in [k] + [c for c in MATMUL_TK if k % c == 0 and MATMUL_MIN_TK <= c < k]:
            if need(tn, tk) <= MATMUL_VMEM_BUDGET:
                return tm, tn, tk
    raise ValueError(f"no matmul tiling fits VMEM for {(m, n, k)}")


def _matmul(a, b, ta, tb, out_dtype, name, rider=None, res=None, rope=None):
    (m, k) = (a.shape[1], a.shape[0]) if ta else a.shape
    (k2, n) = (b.shape[1], b.shape[0]) if tb else b.shape
    assert k == k2, (a.shape, b.shape, ta, tb)
    tm, tn, tk = _matmul_tiles(m, n, k, a.dtype.itemsize, b.dtype.itemsize, jnp.dtype(out_dtype).itemsize,
                               res is not None)
    nk = k // tk
    dims = (((0 if ta else 1,), (1 if tb else 0,)), ((), ()))

    grid = (n // tn, m // tm, nk)

    n_in = 2 + (res is not None) + (rope is not None) + (rider is not None)
    if rope is not None:
        assert res is None and tn == n == QKV_GROUP, (tn, n)

    def body(*refs):
        ins, rest = refs[:n_in], refs[n_in:]
        a_ref, b_ref = ins[:2]
        res_ref = ins[2] if res is not None else None
        rope_ref = ins[2] if rope is not None else None
        o_ref, scratch = rest[0], rest[1:]
        if rider is not None:
            start, wait = _exchange_copies(ins[-1], rest[1], *scratch[-3:], rider[1])
            scratch = scratch[1:-3]
            at = [pl.program_id(ax) for ax in range(3)]
            pl.when(functools.reduce(jnp.logical_and, [p == 0 for p in at]))(start)
        part = lax.dot_general(a_ref[...].astype(BF16), b_ref[...].astype(BF16), dims,
                               preferred_element_type=F32)

        def finish(total):
            if rope_ref is not None:
                for c0 in range(0, tn, ATTN_HEAD_DIM):
                    head = total[:, c0:c0 + ATTN_HEAD_DIM]
                    if c0 < 2 * ATTN_OUT_WIDTH:
                        head = _rope(head, rope_ref[...], 1.0)
                    o_ref[:, c0:c0 + ATTN_HEAD_DIM] = head.astype(o_ref.dtype)
                return
            if res_ref is not None:
                total = total + res_ref[...]
            o_ref[...] = total.astype(o_ref.dtype)

        if nk == 1:
            finish(part)
        else:
            acc_ref, = scratch
            kk = pl.program_id(2)

            @pl.when(kk == 0)
            def _():
                acc_ref[...] = part

            @pl.when(kk > 0)
            def _():
                acc_ref[...] += part

            @pl.when(kk == nk - 1)
            def _():
                finish(acc_ref[...])
        if rider is not None:
            pl.when(functools.reduce(jnp.logical_and, [p == g - 1 for p, g in zip(at, grid)]))(wait)

    a_spec = (pl.BlockSpec((tk, tm), lambda j, i, kk: (kk, i)) if ta
              else pl.BlockSpec((tm, tk), lambda j, i, kk: (i, kk)))
    b_spec = (pl.BlockSpec((tn, tk), lambda j, i, kk: (j, kk)) if tb
              else pl.BlockSpec((tk, tn), lambda j, i, kk: (kk, j)))
    any_spec = pl.BlockSpec(memory_space=pl.ANY)
    tile_spec = pl.BlockSpec((tm, tn), lambda j, i, kk: (i, j))
    in_specs, operands = [a_spec, b_spec], [a, b]
    out_shape, out_specs = [jax.ShapeDtypeStruct((m, n), out_dtype)], [tile_spec]
    scratch = [] if nk == 1 else [pltpu.VMEM((tm, tn), F32)]
    sem = ("parallel", "parallel", "arbitrary")
    if res is not None:
        in_specs.append(tile_spec)
        operands.append(res)
    if rope is not None:
        in_specs.append(pl.BlockSpec((tm, rope.shape[1]), lambda j, i, kk: (i, 0)))
        operands.append(rope)
    if rider is not None:
        in_specs.append(any_spec)
        operands.append(rider[0])
        out_shape.append(_exchange_out(*rider))
        out_specs.append(any_spec)
        scratch += list(EXCHANGE_SEMS)
        sem = ("arbitrary",) * 3
    outs = pl.pallas_call(
        body, name=name, out_shape=out_shape, grid=grid, in_specs=in_specs, out_specs=out_specs,
        scratch_shapes=scratch, compiler_params=_params(*sem),
    )(*operands)
    return outs[0] if rider is None else tuple(outs)


@functools.partial(jax.custom_vjp, nondiff_argnums=(3, 4))
def linear(a, w, res, out_dtype, name):
    return _matmul(a, w, False, False, out_dtype, name + "_fwd", res=res)


def _linear_fwd(a, w, res, out_dtype, name):
    return _matmul(a, w, False, False, out_dtype, name + "_fwd", res=res), (a, w, res is not None)


def _linear_bwd(out_dtype, name, saved, dy):
    a, w, has_res = saved
    da = _matmul(dy, w, False, True, a.dtype, name + "_da")
    dw = _matmul(a, dy, True, False, w.dtype, name + "_dw")
    return da, dw, (dy if has_res else None)


linear.defvjp(_linear_fwd, _linear_bwd)


def _sum_pieces(pieces, name):
    shape = pieces.shape[1:]
    c = shape[-1]
    r = math.prod(shape[:-1])
    tr = _pick(r, (512, 256, 128, 64, 32, 16, 8))

    def body(p_ref, o_ref):
        g = p_ref[0].astype(F32)
        for j in range(1, N_DEV):
            g = g + p_ref[j].astype(F32)
        o_ref[...] = g

    return pl.pallas_call(
        body, name=name, out_shape=jax.ShapeDtypeStruct((r, c), F32), grid=(r // tr,),
        in_specs=[pl.BlockSpec((N_DEV, tr, c), lambda i: (0, i, 0))],
        out_specs=pl.BlockSpec((tr, c), lambda i: (i, 0)),
        compiler_params=_params("parallel"),
    )(pieces.reshape(N_DEV, r, c)).reshape(shape)


@functools.partial(jax.custom_vjp, nondiff_argnums=(5, 6))
def linear_x(a, w, res, rope, shard, out_dtype, name):
    return _matmul(a, w, False, False, out_dtype, name + "_fwd", rider=(shard.astype(BF16), True), res=res,
                   rope=rope)


def _linear_x_fwd(a, w, res, rope, shard, out_dtype, name):
    out = _matmul(a, w, False, False, out_dtype, name + "_fwd", rider=(shard.astype(BF16), True), res=res,
                  rope=rope)
    return out, (a, w, res is not None, rope)


def _linear_x_bwd(out_dtype, name, saved, cts):
    a, w, has_res, rope = saved
    dy, dparts = cts
    half = dparts.shape[1] // 2
    da, moved_a = _matmul(dy, w, False, True, a.dtype, name + "_da", rider=(dparts[:, :half], False))
    dw, moved_b = _matmul(a, dy, True, False, w.dtype, name + "_dw", rider=(dparts[:, half:], False))
    dshard = jnp.concatenate([_sum_pieces(moved_a, name + "_sum_a"), _sum_pieces(moved_b, name + "_sum_b")])
    return da, dw, (dy if has_res else None), (None if rope is None else jnp.zeros_like(rope)), dshard


linear_x.defvjp(_linear_x_fwd, _linear_x_bwd)

FIRST_ROW_SPLIT = (0, 384, 704, 1024)


def _qkv_first_fwd(hs, shard, ropes, carried, name):
    parts = _exchange(shard.astype(BF16), True, name + "_gather")
    wfull = _join8(parts, 1)
    ys, moved = [], []
    for g, (h, rope, carry) in enumerate(zip(hs, ropes, carried)):
        y, m = _matmul(h, wfull[:, g * QKV_GROUP:(g + 1) * QKV_GROUP], False, False, BF16, f"{name}{g}_fwd",
                       rider=(carry.astype(BF16), True), rope=rope)
        ys.append(y)
        moved.append(m)
    return (tuple(ys), tuple(moved)), (hs, wfull, ropes)


@functools.partial(jax.custom_vjp, nondiff_argnums=(4,))
def qkv_first(hs, shard, ropes, carried, name):
    return _qkv_first_fwd(hs, shard, ropes, carried, name)[0]


def _qkv_first_vjp_fwd(hs, shard, ropes, carried, name):
    return _qkv_first_fwd(hs, shard, ropes, carried, name)


def _qkv_first_vjp_bwd(name, saved, cts):
    hs, wfull, ropes = saved
    dys, dmoved = cts
    dws, dcarried = [], []
    for g in range(len(hs)):
        dw, back = _matmul(hs[g], dys[g], True, False, BF16, f"{name}{g}_dw", rider=(dmoved[g], False))
        dws.append(dw)
        dcarried.append(_sum_pieces(back, f"{name}{g}_carried_sum"))
    w_shape = wfull.shape
    dparts = jnp.moveaxis(jnp.concatenate(dws, axis=1).reshape(w_shape[0], N_DEV, w_shape[1] // N_DEV), 1, 0)
    das, dshard = [], []
    for g in range(len(hs)):
        rows = slice(FIRST_ROW_SPLIT[g], FIRST_ROW_SPLIT[g + 1])
        da, got = _matmul(dys[g], wfull[:, g * QKV_GROUP:(g + 1) * QKV_GROUP], False, True, BF16, f"{name}{g}_da",
                          rider=(dparts[:, rows], False))
        das.append(da)
        dshard.append(_sum_pieces(got, f"{name}{g}_own_sum"))
    return tuple(das), jnp.concatenate(dshard), tuple(jnp.zeros_like(r) for r in ropes), tuple(dcarried)


qkv_first.defvjp(_qkv_first_vjp_fwd, _qkv_first_vjp_bwd)


VMEM_LIMIT = 56 * 1024 * 1024
SUBLANES = 8


def _params(*sem):
    return pltpu.CompilerParams(dimension_semantics=sem, vmem_limit_bytes=VMEM_LIMIT)


def _sigmoid(x):
    return 0.5 * jnp.tanh(0.5 * x) + 0.5


def _rstd(xv):
    return lax.rsqrt(jnp.mean(xv * xv, axis=-1, keepdims=True) + NORM_EPS)


def _accumulate(ref, part, first):
    @pl.when(first)
    def _():
        ref[...] = part

    @pl.when(jnp.logical_not(first))
    def _():
        ref[...] += part


def _norm_fwd(x, w, name):
    s, d = x.shape
    tm = _pick(s, (512, 256, 128))

    def body(x_ref, w_ref, h_ref):
        xv = x_ref[...]
        h_ref[...] = (xv * _rstd(xv) * w_ref[...]).astype(h_ref.dtype)

    return pl.pallas_call(
        body, name=name, out_shape=jax.ShapeDtypeStruct((s, d), BF16), grid=(s // tm,),
        in_specs=[pl.BlockSpec((tm, d), lambda i: (i, 0)), pl.BlockSpec((1, d), lambda i: (0, 0))],
        out_specs=pl.BlockSpec((tm, d), lambda i: (i, 0)), compiler_params=_params("parallel"),
    )(x, w.reshape(1, d))


def _norm_bwd(x, w, dh, dskip, name):
    s, d = x.shape
    tm = _pick(s, (512, 256, 128))

    def body(x_ref, w_ref, dh_ref, ds_ref, dx_ref, dw_ref):
        xv = x_ref[...]
        r = _rstd(xv)
        y = xv * r
        dhv = dh_ref[...].astype(F32)
        dy = dhv * w_ref[...]
        dx_ref[...] = ds_ref[...] + r * (dy - y * jnp.mean(dy * y, axis=-1, keepdims=True))
        _accumulate(dw_ref, jnp.sum(dhv * y, axis=0, keepdims=True), pl.program_id(0) == 0)

    row = pl.BlockSpec((tm, d), lambda i: (i, 0))
    vec = pl.BlockSpec((1, d), lambda i: (0, 0))
    dx, dw = pl.pallas_call(
        body, name=name,
        out_shape=[jax.ShapeDtypeStruct((s, d), F32), jax.ShapeDtypeStruct((1, d), F32)], grid=(s // tm,),
        in_specs=[row, vec, row, row], out_specs=[row, vec], compiler_params=_params("arbitrary"),
    )(x, w.reshape(1, d), dh, dskip)
    return dx, dw.reshape(d)


@functools.partial(jax.custom_vjp, nondiff_argnums=(2,))
def norm(x, w, name):
    return x, _norm_fwd(x, w, name + "_fwd")


def _norm_vjp_fwd(x, w, name):
    return (x, _norm_fwd(x, w, name + "_fwd")), (x, w)


def _norm_vjp_bwd(name, saved, cts):
    x, w = saved
    dskip, dh = cts
    return _norm_bwd(x, w, dh, dskip, name + "_bwd")


norm.defvjp(_norm_vjp_fwd, _norm_vjp_bwd)


def loss_head(x, w, target, name):
    s, d = x.shape
    tm = _pick(s, (512, 256, 128))

    def body(x_ref, w_ref, t_ref, loss_ref, dx_ref, dw_ref):
        first = pl.program_id(0) == 0
        xv = x_ref[...]
        r = _rstd(xv)
        y = xv * r
        err = y * w_ref[...] - t_ref[...]
        part = 0.5 * jnp.sum(jnp.sum(err * err, axis=-1, keepdims=True), axis=0, keepdims=True) / d
        _accumulate(loss_ref, jnp.broadcast_to(part, loss_ref.shape), first)
        dout = err / d
        dy = dout * w_ref[...]
        dx_ref[...] = r * (dy - y * jnp.mean(dy * y, axis=-1, keepdims=True))
        _accumulate(dw_ref, jnp.sum(dout * y, axis=0, keepdims=True), first)

    row = pl.BlockSpec((tm, d), lambda i: (i, 0))
    vec = pl.BlockSpec((1, d), lambda i: (0, 0))
    loss, dx, dw = pl.pallas_call(
        body, name=name,
        out_shape=[jax.ShapeDtypeStruct((1, 128), F32), jax.ShapeDtypeStruct((s, d), F32),
                   jax.ShapeDtypeStruct((1, d), F32)],
        grid=(s // tm,), in_specs=[row, vec, row],
        out_specs=[pl.BlockSpec((1, 128), lambda i: (0, 0)), row, vec],
        compiler_params=_params("arbitrary"),
    )(x, w.reshape(1, d), target)
    return loss[0, 0], dx, dw.reshape(d)


FFN_ROWS = 256
FFN_COLS = 256


def _shift_down(cur, halo, k):
    out = pltpu.roll(cur, k, axis=0)
    row = lax.broadcasted_iota(jnp.int32, halo.shape, 0)
    top = out[0:SUBLANES]
    for j in range(k):
        top = jnp.where(row == j, halo[SUBLANES - k + j:SUBLANES - k + j + 1, :], top)
    return jnp.concatenate([top, out[SUBLANES:]], axis=0)


def _shift_up(cur, nxt, k):
    n = cur.shape[0]
    out = pltpu.roll(cur, n - k, axis=0)
    row = lax.broadcasted_iota(jnp.int32, nxt.shape, 0)
    bottom = out[n - SUBLANES:]
    for j in range(k):
        bottom = jnp.where(row == SUBLANES - k + j, nxt[j:j + 1, :], bottom)
    return jnp.concatenate([out[:n - SUBLANES], bottom], axis=0)


def _conv_taps(cur, halo, ntaps):
    return [_shift_down(cur, halo, ntaps - 1 - k) if k < ntaps - 1 else cur for k in range(ntaps)]


def _pad_taps(conv_w):
    return jnp.pad(conv_w, ((0, SUBLANES - conv_w.shape[0]), (0, 0)))


def _ffn_mid_fwd(u0, conv_w, conv_b, name):
    s, width = u0.shape
    half = width // 2
    tm = _pick(s, (FFN_ROWS, 128))
    per = tm // SUBLANES

    def body(w_ref, b_ref, u_ref, halo_ref, a_ref, pre_ref):
        keep = pl.program_id(0) > 0
        for c0 in range(0, half, FFN_COLS):
            vals = []
            for base in (c0, half + c0):
                cols = slice(base, base + FFN_COLS)
                halo = jnp.where(keep, halo_ref[:, cols], 0.0)
                taps = _conv_taps(u_ref[:, cols], halo, 3)
                vals.append(sum(w_ref[k:k + 1, cols] * taps[k] for k in range(3)) + b_ref[:, cols])
                pre_ref[:, cols] = vals[-1].astype(pre_ref.dtype)
            gate, up = vals
            a_ref[:, c0:c0 + FFN_COLS] = (gate * _sigmoid(gate) * up).astype(a_ref.dtype)

    return pl.pallas_call(
        body, name=name,
        out_shape=[jax.ShapeDtypeStruct((s, half), BF16), jax.ShapeDtypeStruct((s, width), BF16)], grid=(s // tm,),
        in_specs=[pl.BlockSpec((SUBLANES, width), lambda i: (0, 0)), pl.BlockSpec((1, width), lambda i: (0, 0)),
                  pl.BlockSpec((tm, width), lambda i: (i, 0)),
                  pl.BlockSpec((SUBLANES, width), lambda i: (jnp.maximum(i * per - 1, 0), 0))],
        out_specs=[pl.BlockSpec((tm, half), lambda i: (i, 0)), pl.BlockSpec((tm, width), lambda i: (i, 0))],
        compiler_params=_params("parallel"),
    )(_pad_taps(conv_w), conv_b.reshape(1, width), u0, u0)


def _ffn_mid_bwd(u0, pre, conv_w, da, name):
    s, width = u0.shape
    half = width // 2
    tm = _pick(s, (FFN_ROWS, 128))
    nt = s // tm

    def body(w_ref, u_ref, pre_ref, da_ref, du0_ref, dw_ref, db_ref, carry_ref):
        first = pl.program_id(0) == 0
        for c0 in range(0, half, FFN_COLS):
            dav = da_ref[:, c0:c0 + FFN_COLS].astype(F32)
            gate = pre_ref[:, c0:c0 + FFN_COLS].astype(F32)
            up = pre_ref[:, half + c0:half + c0 + FFN_COLS].astype(F32)
            sig = _sigmoid(gate)
            dus = [dav * up * sig * (1.0 + gate * (1.0 - sig)), dav * gate * sig]
            for base, du in zip((c0, half + c0), dus):
                cols = slice(base, base + FFN_COLS)
                nxt = jnp.where(first, 0.0, carry_ref[:, cols])
                ahead = [_shift_up(du, nxt, 2), _shift_up(du, nxt, 1), du]
                du0 = sum(w_ref[k:k + 1, cols] * ahead[k] for k in range(3))
                du0_ref[:, cols] = du0.astype(du0_ref.dtype)
                carry_ref[:, cols] = du[0:SUBLANES, :]
                x = u_ref[:, cols]
                dwp = jnp.concatenate([jnp.sum(ahead[k] * x, axis=0, keepdims=True) for k in range(3)]
                                      + [jnp.zeros((SUBLANES - 3, FFN_COLS), F32)], axis=0)
                dbp = jnp.sum(du, axis=0, keepdims=True)

                @pl.when(first)
                def _():
                    dw_ref[:, cols] = dwp
                    db_ref[:, cols] = dbp

                @pl.when(jnp.logical_not(first))
                def _():
                    dw_ref[:, cols] += dwp
                    db_ref[:, cols] += dbp

    rev = lambda i: nt - 1 - i
    du0, dw, db = pl.pallas_call(
        body, name=name,
        out_shape=[jax.ShapeDtypeStruct((s, width), BF16), jax.ShapeDtypeStruct((SUBLANES, width), F32),
                   jax.ShapeDtypeStruct((1, width), F32)],
        grid=(nt,),
        in_specs=[pl.BlockSpec((SUBLANES, width), lambda i: (0, 0)),
                  pl.BlockSpec((tm, width), lambda i: (rev(i), 0)), pl.BlockSpec((tm, width), lambda i: (rev(i), 0)),
                  pl.BlockSpec((tm, half), lambda i: (rev(i), 0))],
        out_specs=[pl.BlockSpec((tm, width), lambda i: (rev(i), 0)),
                   pl.BlockSpec((SUBLANES, width), lambda i: (0, 0)), pl.BlockSpec((1, width), lambda i: (0, 0))],
        scratch_shapes=[pltpu.VMEM((SUBLANES, width), F32)],
        compiler_params=_params("arbitrary"),
    )(_pad_taps(conv_w), u0, pre, da)
    return du0, dw[:3], db.reshape(width)


@functools.partial(jax.custom_vjp, nondiff_argnums=(3,))
def ffn_mid(u0, conv_w, conv_b, name):
    return _ffn_mid_fwd(u0, conv_w, conv_b, name + "_fwd")[0]


def _ffn_mid_vjp_fwd(u0, conv_w, conv_b, name):
    a, pre = _ffn_mid_fwd(u0, conv_w, conv_b, name + "_fwd")
    return a, (u0, pre, conv_w)


def _ffn_mid_vjp_bwd(name, res, da):
    u0, pre, conv_w = res
    return _ffn_mid_bwd(u0, pre, conv_w, da, name + "_bwd")


ffn_mid.defvjp(_ffn_mid_vjp_fwd, _ffn_mid_vjp_bwd)


NEG = -1e30
HEAD_SLICES = [slice(hh * ATTN_HEAD_DIM, (hh + 1) * ATTN_HEAD_DIM) for hh in range(ATTN_HEADS)]
ATTN_SCALE = ATTN_HEAD_DIM ** -0.5
QKV_GROUP = 3 * ATTN_OUT_WIDTH


def rope_table(seq, d):
    pos = (jnp.arange(seq // d, dtype=jnp.int32)[None, :] * d + jnp.arange(d, dtype=jnp.int32)[:, None])
    inv_freq = ROPE_THETA ** (-jnp.arange(0, ROPE_DIM, 2, dtype=F32) / ROPE_DIM)
    ang = pos.reshape(seq).astype(F32)[:, None] * inv_freq[None, :]
    cos, sin = jnp.cos(ang), jnp.sin(ang)
    half = ROPE_DIM // 2
    ones = jnp.ones((seq, ATTN_HEAD_DIM - ROPE_DIM), F32)
    zero = lambda n: jnp.zeros((seq, n), F32)
    return jnp.concatenate([cos, cos, ones, -sin, zero(ATTN_HEAD_DIM - half),
                            zero(half), sin, zero(ATTN_HEAD_DIM - ROPE_DIM)], axis=1)


def _rope(t, tab, sign):
    half = ROPE_DIM // 2
    return t * tab[:, 0:128] + sign * (pltpu.roll(t, ATTN_HEAD_DIM - half, axis=1) * tab[:, 128:256]
                                       + pltpu.roll(t, half, axis=1) * tab[:, 256:384])


def _to_dilated(a, d):
    s = a.shape[0]
    return a if d == 1 else a.reshape(s // d, d, -1).transpose(1, 0, 2).reshape(s, -1)


def _from_dilated(a, d):
    s = a.shape[0]
    return a if d == 1 else a.reshape(d, s // d, -1).transpose(1, 0, 2).reshape(s, -1)


def _dot_nt(a, b):
    return lax.dot_general(a, b, (((1,), (1,)), ((), ())), preferred_element_type=F32)


def _dot_tn(a, b):
    return lax.dot_general(a, b, (((0,), (0,)), ((), ())), preferred_element_type=F32)


def _dot(a, b):
    return jnp.dot(a, b, preferred_element_type=F32)


def _window_mask(has_prev):
    ii = lax.broadcasted_iota(jnp.int32, (ATTN_BLOCK, 2 * ATTN_BLOCK), 0)
    jj = lax.broadcasted_iota(jnp.int32, (ATTN_BLOCK, 2 * ATTN_BLOCK), 1)
    in_window = jnp.logical_and(jj >= ii, jj <= ii + ATTN_BLOCK)
    return jnp.logical_and(in_window, jnp.logical_or(jj >= ATTN_BLOCK, has_prev))


def _both(prev_ref, cur_ref, hs):
    return jnp.concatenate([prev_ref[:, hs], cur_ref[:, hs]], axis=0)


def _attn_group_fwd(qkv, d, name):
    s = qkv.shape[0]
    nb = s // d // ATTN_BLOCK

    def body(q_ref, kc_ref, kp_ref, vc_ref, vp_ref, o_ref, lse_ref):
        mask = _window_mask(pl.program_id(1) > 0)
        lane = lax.broadcasted_iota(jnp.int32, (ATTN_BLOCK, 128), 1)
        lse_tile = jnp.zeros((ATTN_BLOCK, 128), F32)
        for hh, hs in enumerate(HEAD_SLICES):
            sc = jnp.where(mask, _dot_nt(q_ref[:, hs], _both(kp_ref, kc_ref, hs)) * ATTN_SCALE, NEG)
            m = jnp.max(sc, axis=1, keepdims=True)
            p = jnp.exp(sc - m)
            den = jnp.sum(p, axis=1, keepdims=True)
            o_ref[:, hs] = (_dot(p.astype(BF16), _both(vp_ref, vc_ref, hs)) / den).astype(o_ref.dtype)
            lse_tile = jnp.where(lane == hh, m + jnp.log(den), lse_tile)
        lse_ref[...] = lse_tile

    cur = lambda t: pl.BlockSpec((ATTN_BLOCK, ATTN_OUT_WIDTH), lambda r, n: (r * nb + n, t))
    prv = lambda t: pl.BlockSpec((ATTN_BLOCK, ATTN_OUT_WIDTH), lambda r, n: (r * nb + jnp.maximum(n - 1, 0), t))
    return pl.pallas_call(
        body, name=name,
        out_shape=[jax.ShapeDtypeStruct((s, ATTN_OUT_WIDTH), BF16), jax.ShapeDtypeStruct((s, 128), F32)],
        grid=(d, nb), in_specs=[cur(0), cur(1), prv(1), cur(2), prv(2)],
        out_specs=[pl.BlockSpec((ATTN_BLOCK, ATTN_OUT_WIDTH), lambda r, n: (r * nb + n, 0)),
                   pl.BlockSpec((ATTN_BLOCK, 128), lambda r, n: (r * nb + n, 0))],
        compiler_params=_params("parallel", "parallel"),
    )(qkv, qkv, qkv, qkv, qkv)


def _attn_combine(os_, lses, name):
    s = os_[0].shape[0]
    tm = _pick(s, (256, 128))
    ng = len(os_)

    def body(*refs):
        o_refs, l_refs, (o_ref, lse_ref) = refs[:ng], refs[ng:2 * ng], refs[2 * ng:]
        lane = lax.broadcasted_iota(jnp.int32, (tm, 128), 1)
        lse_tile = jnp.zeros((tm, 128), F32)
        for hh, hs in enumerate(HEAD_SLICES):
            ls = [l_ref[:, hh:hh + 1] for l_ref in l_refs]
            m = functools.reduce(jnp.maximum, ls)
            ws = [jnp.exp(l - m) for l in ls]
            tot = functools.reduce(lambda a, b: a + b, ws)
            acc = functools.reduce(lambda a, b: a + b, [o_r[:, hs].astype(F32) * w for o_r, w in zip(o_refs, ws)])
            o_ref[:, hs] = (acc / tot).astype(o_ref.dtype)
            lse_tile = jnp.where(lane == hh, m + jnp.log(tot), lse_tile)
        lse_ref[...] = lse_tile

    wide = pl.BlockSpec((tm, ATTN_OUT_WIDTH), lambda i: (i, 0))
    thin = pl.BlockSpec((tm, 128), lambda i: (i, 0))
    return pl.pallas_call(
        body, name=name,
        out_shape=[jax.ShapeDtypeStruct((s, ATTN_OUT_WIDTH), BF16), jax.ShapeDtypeStruct((s, 128), F32)],
        grid=(s // tm,), in_specs=[wide] * ng + [thin] * ng, out_specs=[wide, thin],
        compiler_params=_params("parallel"),
    )(*os_, *lses)


def _attn_delta(do, o, name):
    s = do.shape[0]
    tm = _pick(s, (256, 128))

    def body(do_ref, o_ref, out_ref):
        lane = lax.broadcasted_iota(jnp.int32, (tm, 128), 1)
        tile = jnp.zeros((tm, 128), F32)
        for hh, hs in enumerate(HEAD_SLICES):
            prod = do_ref[:, hs].astype(F32) * o_ref[:, hs].astype(F32)
            tile = jnp.where(lane == hh, jnp.sum(prod, axis=1, keepdims=True), tile)
        out_ref[...] = tile

    wide = pl.BlockSpec((tm, ATTN_OUT_WIDTH), lambda i: (i, 0))
    return pl.pallas_call(
        body, name=name, out_shape=jax.ShapeDtypeStruct((s, 128), F32), grid=(s // tm,),
        in_specs=[wide, wide], out_specs=pl.BlockSpec((tm, 128), lambda i: (i, 0)),
        compiler_params=_params("parallel"),
    )(do, o)


def _attn_group_bwd(qkv, do, lse, delta, tab, d, name):
    s = qkv.shape[0]
    nb = s // d // ATTN_BLOCK
    wide = ATTN_OUT_WIDTH

    def body(q_ref, qn_ref, k_ref, v_ref, do_ref, don_ref, lse_ref, lsen_ref, dl_ref, dln_ref, tab_ref,
             out_ref, carry_ref):
        n = pl.program_id(1)

        @pl.when(n == 0)
        def _():
            carry_ref[...] = jnp.zeros_like(carry_ref)

        rows = lax.broadcasted_iota(jnp.int32, (2 * ATTN_BLOCK, ATTN_BLOCK), 0)
        keys = lax.broadcasted_iota(jnp.int32, (2 * ATTN_BLOCK, ATTN_BLOCK), 1)
        own = jnp.logical_and(rows < ATTN_BLOCK, keys <= rows)
        nxt = jnp.logical_and(rows >= ATTN_BLOCK, jnp.logical_and(keys >= rows - ATTN_BLOCK, n + 1 < nb))
        mask = jnp.logical_or(own, nxt)
        both = lambda a_ref, b_ref, cols: jnp.concatenate([a_ref[:, cols], b_ref[:, cols]], axis=0)
        for hh, hs in enumerate(HEAD_SLICES):
            one = slice(hh, hh + 1)
            q2, do2 = both(q_ref, qn_ref, hs), both(do_ref, don_ref, hs)
            kh, vh = k_ref[:, hs], v_ref[:, hs]
            sc = jnp.where(mask, _dot_nt(q2, kh) * ATTN_SCALE, NEG)
            p = jnp.exp(sc - both(lse_ref, lsen_ref, one))
            ds = (p * (_dot_nt(do2, vh) - both(dl_ref, dln_ref, one)) * ATTN_SCALE).astype(BF16)
            dq2 = _dot(ds, kh)
            dq = carry_ref[:, hs] + dq2[:ATTN_BLOCK]
            carry_ref[:, hs] = dq2[ATTN_BLOCK:]
            out_ref[:, hs] = _rope(dq, tab_ref[...], -1.0).astype(out_ref.dtype)
            out_ref[:, wide + hh * ATTN_HEAD_DIM:wide + (hh + 1) * ATTN_HEAD_DIM] = _rope(
                _dot_tn(ds, q2), tab_ref[...], -1.0).astype(out_ref.dtype)
            out_ref[:, 2 * wide + hh * ATTN_HEAD_DIM:2 * wide + (hh + 1) * ATTN_HEAD_DIM] = _dot_tn(
                p.astype(BF16), do2).astype(out_ref.dtype)

    def spec(width, row, col):
        return pl.BlockSpec((ATTN_BLOCK, width), lambda r, n: (r * nb + row(n), col))

    cur = lambda n: n
    nxt_block = lambda n: jnp.minimum(n + 1, nb - 1)
    return pl.pallas_call(
        body, name=name, out_shape=jax.ShapeDtypeStruct((s, QKV_GROUP), BF16), grid=(d, nb),
        in_specs=[spec(wide, cur, 0), spec(wide, nxt_block, 0), spec(wide, cur, 1), spec(wide, cur, 2),
                  spec(wide, cur, 0), spec(wide, nxt_block, 0), spec(128, cur, 0), spec(128, nxt_block, 0),
                  spec(128, cur, 0), spec(128, nxt_block, 0), spec(384, cur, 0)],
        out_specs=spec(QKV_GROUP, cur, 0),
        scratch_shapes=[pltpu.VMEM((ATTN_BLOCK, wide), F32)],
        compiler_params=_params("parallel", "arbitrary"),
    )(qkv, qkv, qkv, qkv, do, do, lse, lse, delta, delta, tab)


def _attn_core_fwd(qkvs, name):
    os_, lses = [], []
    for g, (qkv, d) in enumerate(zip(qkvs, ATTN_DILATIONS)):
        o_g, lse_g = _attn_group_fwd(qkv, d, f"{name}_fwd{g}")
        os_.append(_from_dilated(o_g, d))
        lses.append(_from_dilated(lse_g, d))
    o, lse = _attn_combine(os_, lses, name + "_combine")
    return o, (tuple(qkvs), o, lse)


@functools.partial(jax.custom_vjp, nondiff_argnums=(1,))
def attn_core(qkvs, name):
    return _attn_core_fwd(qkvs, name)[0]


def _attn_core_vjp_fwd(qkvs, name):
    return _attn_core_fwd(qkvs, name)


def _attn_core_vjp_bwd(name, res, do):
    rot, o, lse = res
    delta = _attn_delta(do, o, name + "_delta")
    out = []
    for g, (qkv, d) in enumerate(zip(rot, ATTN_DILATIONS)):
        out.append(_attn_group_bwd(qkv, _to_dilated(do, d), _to_dilated(lse, d), _to_dilated(delta, d),
                                   rope_table(o.shape[0], d), d, f"{name}_bwd{g}"))
    return (tuple(out),)


attn_core.defvjp(_attn_core_vjp_fwd, _attn_core_vjp_bwd)


def attention_mixer_p(x, h, lin, j, carries, tag):
    qkvs = tuple(lin(_to_dilated(h, d), ('attn_w_qkv', j), BF16, f"{tag}_qkv{g}",
                     cols=slice(g * QKV_GROUP, (g + 1) * QKV_GROUP), carry=carries[g],
                     rope=rope_table(h.shape[0], d))
                 for g, d in enumerate(ATTN_DILATIONS))
    return lin(attn_core(qkvs, tag), ('attn_w_o', j), F32, tag + "_o", res=x)


def first_attention_mixer(x, h, lin, w, ready, carries, tag):
    s = h.shape[0]
    qkvs, moved = qkv_first(tuple(_to_dilated(h, d) for d in ATTN_DILATIONS), w['attn_w_qkv'][0],
                            tuple(rope_table(s, d) for d in ATTN_DILATIONS),
                            tuple(w[n][j] for n, j in carries), tag + "_qkv")
    for (n, j), parts in zip(carries, moved):
        ready[(n, j)] = _full_weight(n, parts)
    return lin(attn_core(qkvs, tag), ('attn_w_o', 0), F32, tag + "_o", res=x)


SSM_CONV_TAPS = 4
SSM_COL_BLOCK = 2048
SSM_PAIRS = SSM_HEADS // 2
SSM_DT_BLOCK = (SSM_D_INNER + SSM_CONV_DIM) // 128


def _ssm_conv_fwd(zx, conv_w, conv_b, name):
    s = zx.shape[0]
    tm = _pick(s, (256, 128))
    per = tm // SUBLANES
    ncb = SSM_CONV_DIM // SSM_COL_BLOCK

    def body(w_ref, b_ref, x_ref, halo_ref, o_ref, pre_ref):
        keep = pl.program_id(1) > 0
        for c0 in range(0, SSM_COL_BLOCK, FFN_COLS):
            cols = slice(c0, c0 + FFN_COLS)
            halo = jnp.where(keep, halo_ref[:, cols], 0.0)
            taps = _conv_taps(x_ref[:, cols], halo, SSM_CONV_TAPS)
            pre = sum(w_ref[k:k + 1, cols] * taps[k] for k in range(SSM_CONV_TAPS)) + b_ref[:, cols]
            pre_ref[:, cols] = pre.astype(pre_ref.dtype)
            o_ref[:, cols] = pre * _sigmoid(pre)

    blk = pl.BlockSpec((tm, SSM_COL_BLOCK), lambda j, i: (i, j))
    return pl.pallas_call(
        body, name=name,
        out_shape=[jax.ShapeDtypeStruct((s, SSM_CONV_DIM), F32), jax.ShapeDtypeStruct((s, SSM_CONV_DIM), BF16)],
        grid=(ncb, s // tm),
        in_specs=[pl.BlockSpec((SUBLANES, SSM_COL_BLOCK), lambda j, i: (0, j)),
                  pl.BlockSpec((1, SSM_COL_BLOCK), lambda j, i: (0, j)),
                  pl.BlockSpec((tm, SSM_COL_BLOCK), lambda j, i: (i, j + 1)),
                  pl.BlockSpec((SUBLANES, SSM_COL_BLOCK), lambda j, i: (jnp.maximum(i * per - 1, 0), j + 1))],
        out_specs=[blk, blk], compiler_params=_params("parallel", "parallel"),
    )(_pad_taps(conv_w), conv_b.reshape(1, SSM_CONV_DIM), zx, zx)


def _ssm_conv_bwd(zx, pre, conv_w, dact, dzx, name):
    s = zx.shape[0]
    tm = _pick(s, (256, 128))
    nt = s // tm
    ncb = SSM_CONV_DIM // SSM_COL_BLOCK
    nk = SSM_CONV_TAPS

    def body(w_ref, x_ref, pre_ref, da_ref, _, dx_ref, dw_ref, db_ref, carry_ref):
        first = pl.program_id(1) == 0
        for c0 in range(0, SSM_COL_BLOCK, FFN_COLS):
            cols = slice(c0, c0 + FFN_COLS)
            prev = pre_ref[:, cols].astype(F32)
            sig = _sigmoid(prev)
            dpre = da_ref[:, cols] * sig * (1.0 + prev * (1.0 - sig))
            nxt = jnp.where(first, 0.0, carry_ref[:, cols])
            ahead = [_shift_up(dpre, nxt, nk - 1 - k) for k in range(nk - 1)] + [dpre]
            dx_ref[:, cols] = sum(w_ref[k:k + 1, cols] * ahead[k] for k in range(nk)).astype(dx_ref.dtype)
            carry_ref[:, cols] = dpre[0:SUBLANES, :]
            x = x_ref[:, cols]
            dwp = jnp.concatenate([jnp.sum(ahead[k] * x, axis=0, keepdims=True) for k in range(nk)]
                                  + [jnp.zeros((SUBLANES - nk, FFN_COLS), F32)], axis=0)
            dbp = jnp.sum(dpre, axis=0, keepdims=True)

            @pl.when(first)
            def _():
                dw_ref[:, cols] = dwp
                db_ref[:, cols] = dbp

            @pl.when(jnp.logical_not(first))
            def _():
                dw_ref[:, cols] += dwp
                db_ref[:, cols] += dbp

    rev = lambda i: nt - 1 - i
    dzx, dw, db = pl.pallas_call(
        body, name=name,
        out_shape=[jax.ShapeDtypeStruct(dzx.shape, dzx.dtype), jax.ShapeDtypeStruct((SUBLANES, SSM_CONV_DIM), F32),
                   jax.ShapeDtypeStruct((1, SSM_CONV_DIM), F32)],
        grid=(ncb, nt),
        in_specs=[pl.BlockSpec((SUBLANES, SSM_COL_BLOCK), lambda j, i: (0, j)),
                  pl.BlockSpec((tm, SSM_COL_BLOCK), lambda j, i: (rev(i), j + 1)),
                  pl.BlockSpec((tm, SSM_COL_BLOCK), lambda j, i: (rev(i), j)),
                  pl.BlockSpec((tm, SSM_COL_BLOCK), lambda j, i: (rev(i), j)),
                  pl.BlockSpec(memory_space=pl.ANY)],
        out_specs=[pl.BlockSpec((tm, SSM_COL_BLOCK), lambda j, i: (rev(i), j + 1)),
                   pl.BlockSpec((SUBLANES, SSM_COL_BLOCK), lambda j, i: (0, j)),
                   pl.BlockSpec((1, SSM_COL_BLOCK), lambda j, i: (0, j))],
        scratch_shapes=[pltpu.VMEM((SUBLANES, SSM_COL_BLOCK), F32)],
        input_output_aliases={4: 0},
        compiler_params=_params("parallel", "arbitrary"),
    )(_pad_taps(conv_w), zx, pre, dact, dzx)
    return dzx, dw[:nk], db.reshape(SSM_CONV_DIM)


def _ssd_chunk(xs, bms, cms, dt_raw, dtb, alog, dsk, states):
    q = SSM_CHUNK
    lane = lax.broadcasted_iota(jnp.int32, (1, 128), 1)
    row = lax.broadcasted_iota(jnp.int32, (q, 1), 0)
    ii = lax.broadcasted_iota(jnp.int32, (q, q), 0)
    jj = lax.broadcasted_iota(jnp.int32, (q, q), 1)
    tril = ii >= jj
    left = lane < SSM_HEAD_DIM
    last_row = (row == q - 1).astype(F32)

    def lanes_of(mat, h):
        pick = (lane == h).astype(F32)
        return jnp.broadcast_to(jnp.sum(mat * pick, axis=1, keepdims=True), mat.shape)

    def rows_of(mat_t, h):
        pick = (row == h).astype(F32)
        return jnp.broadcast_to(jnp.sum(mat_t * pick, axis=0, keepdims=True), mat_t.shape)

    v = dt_raw + dtb
    dt = jnp.maximum(v, 0.0) + jnp.log(1.0 + jnp.exp(-jnp.abs(v)))
    adt = dt * (-jnp.exp(alog))
    acs = jnp.dot(tril.astype(F32), adt, precision=lax.Precision.HIGHEST, preferred_element_type=F32)
    acs_t = acs.T
    ys, new_states = [], []
    for pr in range(SSM_PAIRS):
        g = pr // 2
        if pr % 2 == 0:
            cb = _dot_nt(cms[g].astype(BF16), bms[g].astype(BF16))
        cols = [lanes_of(acs, 2 * pr + e) for e in range(2)]
        dts = [lanes_of(dt, 2 * pr + e) for e in range(2)]
        rws = [rows_of(acs_t, 2 * pr + e) for e in range(2)]
        lasts = [jnp.sum(c * last_row, axis=0, keepdims=True) for c in cols]
        xdt = xs[pr] * jnp.where(left, dts[0], dts[1])
        halves = [jnp.where(left, xdt, 0.0).astype(BF16), jnp.where(left, 0.0, xdt).astype(BF16)]
        y_diag, s_new = 0.0, 0.0
        for e in range(2):
            lmat = jnp.where(tril, jnp.exp(jnp.minimum(cols[e] - rws[e], 0.0)), 0.0)
            y_diag = y_diag + _dot((cb * lmat).astype(BF16), halves[e])
            decay = jnp.exp(lasts[e] - cols[e])
            s_new = s_new + _dot_tn((bms[g] * decay).astype(BF16), halves[e])
        y_off = _dot(cms[g].astype(BF16), states[pr].astype(BF16)) * jnp.where(left, jnp.exp(cols[0]), jnp.exp(cols[1]))
        skip = jnp.where(left, lanes_of(dsk, 2 * pr), lanes_of(dsk, 2 * pr + 1))
        ys.append(y_diag + y_off + xs[pr] * skip)
        new_states.append(states[pr] * jnp.where(left, jnp.exp(lasts[0]), jnp.exp(lasts[1])) + s_new)
    return tuple(ys), tuple(new_states)


def _ssm_vec(v):
    return jnp.pad(v.reshape(1, -1), ((0, 0), (0, 128 - v.shape[0])))


def _ssd_scan_fwd(act, zx, dtb, alog, dsk, name):
    s = act.shape[0]
    nc = s // SSM_CHUNK
    ng = SSM_GROUPS

    def body(act_ref, dt_ref, dtb_ref, alog_ref, dsk_ref, y_ref, st_out_ref, st_ref):
        @pl.when(pl.program_id(0) == 0)
        def _():
            st_ref[...] = jnp.zeros_like(st_ref)

        tile = lambda k: act_ref[:, k * 128:(k + 1) * 128]
        xs = [tile(k) for k in range(SSM_PAIRS)]
        bms = [tile(SSM_PAIRS + k) for k in range(ng)]
        cms = [tile(SSM_PAIRS + ng + k) for k in range(ng)]
        states = [st_ref[k] for k in range(SSM_PAIRS)]
        st_out_ref[0] = st_ref[...]
        ys, new_states = _ssd_chunk(xs, bms, cms, dt_ref[...], dtb_ref[...], alog_ref[...], dsk_ref[...], states)
        for k in range(SSM_PAIRS):
            y_ref[:, k * 128:(k + 1) * 128] = ys[k]
            st_ref[k] = new_states[k]

    vec = pl.BlockSpec((1, 128), lambda c: (0, 0))
    return pl.pallas_call(
        body, name=name,
        out_shape=[jax.ShapeDtypeStruct((s, SSM_D_INNER), F32),
                   jax.ShapeDtypeStruct((nc, SSM_PAIRS, SSM_STATE, 128), F32)],
        grid=(nc,),
        in_specs=[pl.BlockSpec((SSM_CHUNK, SSM_CONV_DIM), lambda c: (c, 0)),
                  pl.BlockSpec((SSM_CHUNK, 128), lambda c: (c, SSM_DT_BLOCK)), vec, vec, vec],
        out_specs=[pl.BlockSpec((SSM_CHUNK, SSM_D_INNER), lambda c: (c, 0)),
                   pl.BlockSpec((1, SSM_PAIRS, SSM_STATE, 128), lambda c: (c, 0, 0, 0))],
        scratch_shapes=[pltpu.VMEM((SSM_PAIRS, SSM_STATE, 128), F32)],
        compiler_params=_params("arbitrary"),
    )(act, zx, _ssm_vec(dtb), _ssm_vec(alog), _ssm_vec(dsk))


def _ssd_scan_bwd(act, zx, dtb, alog, dsk, st_in, dy, dzx, name):
    s = act.shape[0]
    nc = s // SSM_CHUNK
    ng = SSM_GROUPS
    dt_col = SSM_D_INNER + SSM_CONV_DIM
    tail = zx.shape[1] - dt_col
    assert tail % 128 == 0 and dt_col % tail == 0, tail

    def body(act_ref, dt_ref, dtb_ref, alog_ref, dsk_ref, st_ref, dy_ref, _, dact_ref, ddt_ref, dpar_ref, dst_ref):
        first = pl.program_id(0) == 0

        @pl.when(first)
        def _():
            dst_ref[...] = jnp.zeros_like(dst_ref)

        tile = lambda k: act_ref[:, k * 128:(k + 1) * 128]
        xs = [tile(k) for k in range(SSM_PAIRS)]
        bms = [tile(SSM_PAIRS + k) for k in range(ng)]
        cms = [tile(SSM_PAIRS + ng + k) for k in range(ng)]
        states = [st_ref[0, k] for k in range(SSM_PAIRS)]
        _, pullback = jax.vjp(_ssd_chunk, xs, bms, cms, dt_ref[...], dtb_ref[...], alog_ref[...], dsk_ref[...],
                              states)
        dys = tuple(dy_ref[:, k * 128:(k + 1) * 128] for k in range(SSM_PAIRS))
        dsts = tuple(dst_ref[k] for k in range(SSM_PAIRS))
        dxs, dbms, dcms, ddt, ddtb, dalog, ddsk, dstates = pullback((dys, dsts))
        for k, t in enumerate(list(dxs) + list(dbms) + list(dcms)):
            dact_ref[:, k * 128:(k + 1) * 128] = t
        ddt_ref[:, 0:128] = ddt.astype(ddt_ref.dtype)
        if tail > 128:
            ddt_ref[:, 128:] = jnp.zeros((SSM_CHUNK, tail - 128), ddt_ref.dtype)
        for k in range(SSM_PAIRS):
            dst_ref[k] = dstates[k]
        dpar = jnp.concatenate([ddtb, dalog, ddsk, jnp.zeros((SUBLANES - 3, 128), F32)], axis=0)
        _accumulate(dpar_ref, dpar, first)

    rev = lambda c: nc - 1 - c
    vec = pl.BlockSpec((1, 128), lambda c: (0, 0))
    dact, dzx, dpar = pl.pallas_call(
        body, name=name,
        out_shape=[jax.ShapeDtypeStruct((s, SSM_CONV_DIM), F32), jax.ShapeDtypeStruct(dzx.shape, dzx.dtype),
                   jax.ShapeDtypeStruct((SUBLANES, 128), F32)],
        grid=(nc,),
        in_specs=[pl.BlockSpec((SSM_CHUNK, SSM_CONV_DIM), lambda c: (rev(c), 0)),
                  pl.BlockSpec((SSM_CHUNK, 128), lambda c: (rev(c), SSM_DT_BLOCK)), vec, vec, vec,
                  pl.BlockSpec((1, SSM_PAIRS, SSM_STATE, 128), lambda c: (rev(c), 0, 0, 0)),
                  pl.BlockSpec((SSM_CHUNK, SSM_D_INNER), lambda c: (rev(c), 0)),
                  pl.BlockSpec(memory_space=pl.ANY)],
        out_specs=[pl.BlockSpec((SSM_CHUNK, SSM_CONV_DIM), lambda c: (rev(c), 0)),
                   pl.BlockSpec((SSM_CHUNK, tail), lambda c: (rev(c), dt_col // tail)),
                   pl.BlockSpec((SUBLANES, 128), lambda c: (0, 0))],
        scratch_shapes=[pltpu.VMEM((SSM_PAIRS, SSM_STATE, 128), F32)],
        input_output_aliases={7: 1},
        compiler_params=_params("arbitrary"),
    )(act, zx, _ssm_vec(dtb), _ssm_vec(alog), _ssm_vec(dsk), st_in, dy, dzx)
    return dact, dzx, dpar[0, :SSM_HEADS], dpar[1, :SSM_HEADS], dpar[2, :SSM_HEADS]


SSM_NORM_GROUP = SSM_D_INNER // SSM_GROUPS


def _gated_group(y, z, w):
    g = y * (z * _sigmoid(z))
    return g * lax.rsqrt(jnp.mean(g * g, axis=-1, keepdims=True) + NORM_EPS) * w


def _gated_norm_fwd(y, zx, w, name):
    s = y.shape[0]
    tm = _pick(s, (256, 128))

    def body(y_ref, z_ref, w_ref, o_ref):
        for c0 in range(0, SSM_D_INNER, SSM_NORM_GROUP):
            cols = slice(c0, c0 + SSM_NORM_GROUP)
            o_ref[:, cols] = _gated_group(y_ref[:, cols], z_ref[:, cols], w_ref[:, cols]).astype(o_ref.dtype)

    blk = pl.BlockSpec((tm, SSM_D_INNER), lambda i: (i, 0))
    return pl.pallas_call(
        body, name=name, out_shape=jax.ShapeDtypeStruct((s, SSM_D_INNER), BF16), grid=(s // tm,),
        in_specs=[blk, blk, pl.BlockSpec((1, SSM_D_INNER), lambda i: (0, 0))], out_specs=blk,
        compiler_params=_params("parallel"),
    )(y, zx, w.reshape(1, SSM_D_INNER))


def _gated_norm_bwd(y, zx, w, dout, name):
    s = y.shape[0]
    tm = _pick(s, (256, 128))

    def body(y_ref, z_ref, w_ref, do_ref, dy_ref, dz_ref, dw_ref):
        first = pl.program_id(0) == 0
        for c0 in range(0, SSM_D_INNER, SSM_NORM_GROUP):
            cols = slice(c0, c0 + SSM_NORM_GROUP)
            _, pullback = jax.vjp(_gated_group, y_ref[:, cols], z_ref[:, cols], w_ref[:, cols])
            dyv, dzv, dwv = pullback(do_ref[:, cols].astype(F32))
            dy_ref[:, cols] = dyv
            dz_ref[:, cols] = dzv.astype(dz_ref.dtype)

            @pl.when(first)
            def _():
                dw_ref[:, cols] = dwv

            @pl.when(jnp.logical_not(first))
            def _():
                dw_ref[:, cols] += dwv

    blk = pl.BlockSpec((tm, SSM_D_INNER), lambda i: (i, 0))
    vec = pl.BlockSpec((1, SSM_D_INNER), lambda i: (0, 0))
    dy, dz, dw = pl.pallas_call(
        body, name=name,
        out_shape=[jax.ShapeDtypeStruct((s, SSM_D_INNER), F32), jax.ShapeDtypeStruct((s, zx.shape[1]), BF16),
                   jax.ShapeDtypeStruct((1, SSM_D_INNER), F32)],
        grid=(s // tm,), in_specs=[blk, blk, vec, blk], out_specs=[blk, blk, vec],
        compiler_params=_params("arbitrary"),
    )(y, zx, w.reshape(1, SSM_D_INNER), dout)
    return dy, dz, dw.reshape(SSM_D_INNER)


def _ssm_core_fwd(zx, conv_w, conv_b, dtb, alog, dsk, norm_w, name):
    act, pre = _ssm_conv_fwd(zx, conv_w, conv_b, name + "_conv_fwd")
    y, st_in = _ssd_scan_fwd(act, zx, dtb, alog, dsk, name + "_scan_fwd")
    out = _gated_norm_fwd(y, zx, norm_w, name + "_gate_fwd")
    return out, (zx, conv_w, pre, dtb, alog, dsk, norm_w, act, y, st_in)


@functools.partial(jax.custom_vjp, nondiff_argnums=(7,))
def ssm_core(zx, conv_w, conv_b, dtb, alog, dsk, norm_w, name):
    return _ssm_core_fwd(zx, conv_w, conv_b, dtb, alog, dsk, norm_w, name)[0]


def _ssm_core_vjp_fwd(zx, conv_w, conv_b, dtb, alog, dsk, norm_w, name):
    return _ssm_core_fwd(zx, conv_w, conv_b, dtb, alog, dsk, norm_w, name)


def _ssm_core_vjp_bwd(name, res, dout):
    zx, conv_w, pre, dtb, alog, dsk, norm_w, act, y, st_in = res
    dy, dzx, dnorm_w = _gated_norm_bwd(y, zx, norm_w, dout, name + "_gate_bwd")
    dact, dzx, ddtb, dalog, ddsk = _ssd_scan_bwd(act, zx, dtb, alog, dsk, st_in, dy, dzx, name + "_scan_bwd")
    dzx, dconv_w, dconv_b = _ssm_conv_bwd(zx, pre, conv_w, dact, dzx, name + "_conv_bwd")
    return dzx, dconv_w, dconv_b, ddtb, dalog, ddsk, dnorm_w


ssm_core.defvjp(_ssm_core_vjp_fwd, _ssm_core_vjp_bwd)


def ssd_mixer_p(x, h, lin, j, carries, conv_w, conv_b, dtb, alog, dsk, norm_w, tag):
    zx = lin(h, ('ssm_w_in', j), F32, tag + "_in", carry=carries[0])
    return lin(ssm_core(zx, conv_w, conv_b, dtb, alog, dsk, norm_w, tag), ('ssm_w_out', j), F32, tag + "_out",
               carry=carries[1], res=x)


def _full_weight(n, parts):
    full = _join8(parts, SHARD_AXIS[n] - 1)
    if n == 'ssm_w_in':
        full = jnp.pad(full, ((0, 0), (0, SSM_IN_PAD - SSM_IN_WIDTH)))
    return full


def trunk(w, x):
    ready = {}

    def lin(a, key, out_dtype, name, cols=None, carry=None, res=None, rope=None):
        wt = ready[key] if cols is None else ready[key][:, cols]
        if carry is None:
            assert rope is None
            return linear(a, wt, res, out_dtype, name)
        y, parts = linear_x(a, wt, res, rope, w[carry[0]][carry[1]], out_dtype, name)
        ready[carry] = _full_weight(carry[0], parts)
        return y

    for i in range(DEPTH):
        x, h = norm(x, w['mix_norm_w'][i], f"mixnorm{i}")
        j = i // 2
        ffn_next = [('ffn_w_up', i), ('ffn_w_down', i)]
        if i == 0:
            x = first_attention_mixer(x, h, lin, w, ready, [('attn_w_o', 0)] + ffn_next, "attn0")
        elif i % 2 == 0:
            x = attention_mixer_p(x, h, lin, j, [('attn_w_o', j)] + ffn_next, f"attn{j}")
        else:
            x = ssd_mixer_p(x, h, lin, j, ffn_next, w['ssm_conv_w'][j], w['ssm_conv_b'][j], w['ssm_dt_bias'][j],
                            w['ssm_a_log'][j], w['ssm_d'][j], w['ssm_norm_w'][j], f"ssm{j}")
        if i + 1 == DEPTH:
            mixer_next = [None, None]
        elif i % 2 == 0:
            mixer_next = [('ssm_w_in', j), ('ssm_w_out', j)]
        else:
            mixer_next = [('attn_w_qkv', j + 1), None]
        x, h = norm(x, w['ffn_norm_w'][i], f"ffnnorm{i}")
        u0 = lin(h, ('ffn_w_up', i), F32, f"ffn{i}_up", carry=mixer_next[0])
        a = ffn_mid(u0, w['ffn_conv_w'][i], w['ffn_conv_b'][i], f"ffn{i}_mid")
        x = lin(a, ('ffn_w_down', i), F32, f"ffn{i}_down", carry=mixer_next[1], res=x)
    return x


def local_step(w, x, target):
    final_w = w['final_norm_w']
    trunk_w = {n: a for n, a in w.items() if n != 'final_norm_w'}
    xf, pullback = jax.vjp(trunk, trunk_w, x)
    loss, dxf, dfinal = loss_head(xf, final_w, target, "loss_head")
    gw, gx = pullback(dxf)
    gw['final_norm_w'] = dfinal
    return loss, gw, gx


def _adam_math(w, g, m, v):
    m = ADAM_B1 * m + (1.0 - ADAM_B1) * g
    v = ADAM_B2 * v + (1.0 - ADAM_B2) * (g * g)
    m_hat = m / (1.0 - ADAM_B1 ** ADAM_STEP)
    v_hat = v / (1.0 - ADAM_B2 ** ADAM_STEP)
    delta = -ADAM_LR * (m_hat / (jnp.sqrt(v_hat) + ADAM_EPS) + ADAM_WD * w)
    return delta, m, v


def _adamw_rows(g, w, m, v, name):
    r, c = w.shape
    tr = _pick(r, (256, 128, 64, 32, 16, 8))

    def body(g_ref, w_ref, m_ref, v_ref, d_out, m_out, v_out):
        delta, mm, vv = _adam_math(w_ref[...], g_ref[...], m_ref[...], v_ref[...])
        d_out[...] = delta
        m_out[...] = mm
        v_out[...] = vv

    blk = pl.BlockSpec((tr, c), lambda i: (i, 0))
    return pl.pallas_call(
        body, name=name, out_shape=[jax.ShapeDtypeStruct((r, c), F32)] * 3, grid=(r // tr,),
        in_specs=[blk] * 4, out_specs=[blk] * 3, compiler_params=_params("parallel"),
    )(g, w, m, v)


def _sum8(pieces, name):
    _, r, c = pieces.shape

    def body(p_ref, o_ref):
        g = p_ref[0]
        for j in range(1, N_DEV):
            g = g + p_ref[j]
        o_ref[...] = g

    return pl.pallas_call(
        body, name=name, out_shape=jax.ShapeDtypeStruct((r, c), F32),
        in_specs=[pl.BlockSpec(memory_space=pltpu.VMEM)], out_specs=pl.BlockSpec(memory_space=pltpu.VMEM),
    )(pieces)


def _adamw_plain(g, w, m, v, name):
    def body(g_ref, w_ref, m_ref, v_ref, d_out, m_out, v_out):
        delta, mm, vv = _adam_math(w_ref[...], g_ref[...], m_ref[...], v_ref[...])
        d_out[...] = delta
        m_out[...] = mm
        v_out[...] = vv

    vm = pl.BlockSpec(memory_space=pltpu.VMEM)
    return pl.pallas_call(
        body, name=name, out_shape=[jax.ShapeDtypeStruct(g.shape, F32)] * 3,
        in_specs=[vm] * 4, out_specs=[vm] * 3,
    )(g, w, m, v)


def _join8(parts, axis):
    t = jnp.moveaxis(parts, 0, axis)
    shp = t.shape
    return t.reshape(shp[:axis] + (shp[axis] * shp[axis + 1],) + shp[axis + 2:])


def _pack(arrs, lead, mult):
    flat = jnp.concatenate([a.reshape(a.shape[:lead] + (-1,)) for a in arrs], axis=-1)
    return _pad_rows(flat, mult)


def _unpack(buf, shapes, lead):
    flat = buf.reshape(buf.shape[:lead] + (-1,))
    out, off = [], 0
    for shp in shapes:
        n = math.prod(shp)
        out.append(flat[..., off:off + n].reshape(flat.shape[:lead] + tuple(shp)))
        off += n
    return out


def _own_shard(full, axis):
    size = full.shape[axis] // N_DEV
    return lax.dynamic_slice_in_dim(full, _my_index() * size, size, axis)


def kernel(x, mix_norm_w, attn_w_qkv, attn_w_o, ssm_w_in, ssm_conv_w, ssm_conv_b, ssm_dt_bias, ssm_a_log, ssm_d, ssm_norm_w, ssm_w_out, ffn_norm_w, ffn_w_up, ffn_conv_w, ffn_conv_b, ffn_w_down, final_norm_w, loss_target, m_mix_norm_w, m_attn_w_qkv, m_attn_w_o, m_ssm_w_in, m_ssm_conv_w, m_ssm_conv_b, m_ssm_dt_bias, m_ssm_a_log, m_ssm_d, m_ssm_norm_w, m_ssm_w_out, m_ffn_norm_w, m_ffn_w_up, m_ffn_conv_w, m_ffn_conv_b, m_ffn_w_down, m_final_norm_w, v_mix_norm_w, v_attn_w_qkv, v_attn_w_o, v_ssm_w_in, v_ssm_conv_w, v_ssm_conv_b, v_ssm_dt_bias, v_ssm_a_log, v_ssm_d, v_ssm_norm_w, v_ssm_w_out, v_ffn_norm_w, v_ffn_w_up, v_ffn_conv_w, v_ffn_conv_b, v_ffn_w_down, v_final_norm_w):
    args = dict(locals())
    w_sh = {n: args[n] for n in WEIGHT_NAMES}
    m_sh = {n: args["m_" + n] for n in WEIGHT_NAMES}
    v_sh = {n: args["v_" + n] for n in WEIGHT_NAMES}

    small_shapes = [w_sh[n].shape for n in SMALL_SHARDED]
    small = _exchange(_pack([w_sh[n] for n in SMALL_SHARDED], 0, 8), True, "gather_small")
    full = {n: w_sh[n] for n in SMALL if SHARD_AXIS[n] is None}
    for n, parts in zip(SMALL_SHARDED, _unpack(small, small_shapes, 1)):
        full[n] = _join8(parts, SHARD_AXIS[n])
    for n in BIG:
        full[n] = [w_sh[n][j] for j in range(w_sh[n].shape[0])]

    loss, gw, gx = local_step(full, x[0], loss_target[0])
    loss = lax.psum(loss, ("x", "y", "c"))
    for n in BIG:
        gw[n] = jnp.stack(gw[n])

    grads, deltas, new_m, new_v = {}, {}, {}, {}
    for n in BIG:
        shp = w_sh[n].shape
        two_d = (shp[0] * shp[1], shp[2])
        outs = _adamw_rows(*[t.reshape(two_d) for t in (gw[n], w_sh[n], m_sh[n], v_sh[n])], "adamw_" + n)
        grads[n] = gw[n]
        deltas[n], new_m[n], new_v[n] = [o.reshape(shp) for o in outs]

    small_full_shapes = [gw[n].shape for n in SMALL]
    gsmall = _exchange(_pack([gw[n] for n in SMALL], 0, 8), True, "gather_small_grads")
    gsmall = _unpack(_sum8(gsmall, "sum_small_grads"), small_full_shapes, 0)
    for n, g in zip(SMALL, gsmall):
        grads[n] = g if SHARD_AXIS[n] is None else _own_shard(g, SHARD_AXIS[n])
    shapes = [w_sh[n].shape for n in SMALL]
    outs = _adamw_plain(*[_pack([d[n] for n in SMALL], 0, 8) for d in (grads, w_sh, m_sh, v_sh)], "adamw_small")
    for d, buf in zip((deltas, new_m, new_v), outs):
        for n, a in zip(SMALL, _unpack(buf, shapes, 0)):
            d[n] = a

    return (loss, gx[None], *[grads[n] for n in WEIGHT_NAMES], *[deltas[n] for n in WEIGHT_NAMES],
            *[new_m[n] for n in WEIGHT_NAMES], *[new_v[n] for n in WEIGHT_NAMES])
```

```python
import functools
import math

import jax
import jax.numpy as jnp
from jax import lax
from jax.experimental import pallas as pl
from jax.experimental.pallas import tpu as pltpu

F32 = jnp.float32
BF16 = jnp.bfloat16
N_DEV = 8
MESH_ID = pl.DeviceIdType.MESH

DEPTH = 4
ATTN_HEADS = 8
ATTN_HEAD_DIM = 128
ATTN_DILATIONS = (1, 4, 16)
ATTN_BLOCK = 128
ROPE_THETA = 500000.0
ROPE_DIM = 32
ATTN_OUT_WIDTH = 1024
SSM_D_INNER = 2048
SSM_HEAD_DIM = 64
SSM_HEADS = 32
SSM_STATE = 128
SSM_GROUPS = 8
SSM_CHUNK = 128
SSM_CONV_DIM = 4096
SSM_IN_WIDTH = 6176
SSM_IN_PAD = 6400
NORM_EPS = 1e-5
ADAM_LR = 0.001
ADAM_B1 = 0.9
ADAM_B2 = 0.999
ADAM_EPS = 1e-08
ADAM_WD = 0.01
ADAM_STEP = 10

WEIGHT_NAMES = ['mix_norm_w', 'attn_w_qkv', 'attn_w_o', 'ssm_w_in', 'ssm_conv_w', 'ssm_conv_b', 'ssm_dt_bias',
                'ssm_a_log', 'ssm_d', 'ssm_norm_w', 'ssm_w_out', 'ffn_norm_w', 'ffn_w_up', 'ffn_conv_w',
                'ffn_conv_b', 'ffn_w_down', 'final_norm_w']
SHARD_AXIS = {'mix_norm_w': None, 'attn_w_qkv': 2, 'attn_w_o': 1, 'ssm_w_in': 2, 'ssm_conv_w': 2, 'ssm_conv_b': 1,
              'ssm_dt_bias': None, 'ssm_a_log': None, 'ssm_d': None, 'ssm_norm_w': 1, 'ssm_w_out': 1,
              'ffn_norm_w': None, 'ffn_w_up': 2, 'ffn_conv_w': 2, 'ffn_conv_b': None, 'ffn_w_down': 1,
              'final_norm_w': None}
BIG = ['attn_w_qkv', 'attn_w_o', 'ssm_w_in', 'ssm_w_out', 'ffn_w_up', 'ffn_w_down']
SMALL = [n for n in WEIGHT_NAMES if n not in BIG]
SMALL_SHARDED = [n for n in SMALL if SHARD_AXIS[n] is not None]
LANES = 1024


def _my_index():
    return 4 * lax.axis_index("x") + 2 * lax.axis_index("y") + lax.axis_index("c")


def _peer(k):
    x, y, c = lax.axis_index("x"), lax.axis_index("y"), lax.axis_index("c")
    return (x ^ ((k >> 2) & 1), y ^ ((k >> 1) & 1), c ^ (k & 1))


def _exchange(src, gather, name):
    def body(src_ref, out_ref, send_sems, recv_sems, local_sem):
        start, wait = _exchange_copies(src_ref, out_ref, send_sems, recv_sems, local_sem, gather)
        start()
        wait()

    return pl.pallas_call(
        body, name=name,
        out_shape=_exchange_out(src, gather),
        in_specs=[pl.BlockSpec(memory_space=pl.ANY)],
        out_specs=pl.BlockSpec(memory_space=pl.ANY),
        scratch_shapes=list(EXCHANGE_SEMS),
    )(src)


EXCHANGE_SEMS = (pltpu.SemaphoreType.DMA((N_DEV - 1,)), pltpu.SemaphoreType.DMA((N_DEV - 1,)),
                 pltpu.SemaphoreType.DMA)


def _exchange_out(src, gather):
    return jax.ShapeDtypeStruct((N_DEV,) + (src.shape if gather else src.shape[1:]), src.dtype)


def _exchange_copies(src_ref, out_ref, send_sems, recv_sems, local_sem, gather):
    me = _my_index()

    def piece(j):
        return src_ref if gather else src_ref.at[j]

    def remote(k, slab):
        return pltpu.make_async_remote_copy(
            src_ref=piece(me ^ k), dst_ref=out_ref.at[slab], send_sem=send_sems.at[k - 1],
            recv_sem=recv_sems.at[k - 1], device_id=_peer(k), device_id_type=MESH_ID)

    mine = pltpu.make_async_copy(piece(me), out_ref.at[me], local_sem)
    arrivals = {k: remote(k, me ^ k) for k in range(1, N_DEV)}
    if not gather:
        sends = [remote(k, me) for k in range(1, N_DEV)]

        def start():
            mine.start()
            for cp in sends:
                cp.start()

        def wait():
            for cp in arrivals.values():
                cp.wait_recv()
            for cp in sends:
                cp.wait_send()
            mine.wait()

        return start, wait

    far = (2, 4, 6)
    sends = [remote(k, me) for k in (1,) + far]
    passed_on = [pltpu.make_async_remote_copy(
        src_ref=out_ref.at[me ^ k], dst_ref=out_ref.at[me ^ k], send_sem=send_sems.at[k],
        recv_sem=recv_sems.at[k], device_id=_peer(1), device_id_type=MESH_ID) for k in far]

    def start():
        mine.start()
        for cp in sends:
            cp.start()

    def wait():
        for k, cp in zip(far, passed_on):
            arrivals[k].wait_recv()
            cp.start()
        for k in (1, 3, 5, 7):
            arrivals[k].wait_recv()
        for cp in sends + passed_on:
            cp.wait_send()
        mine.wait()

    return start, wait


def _pad_rows(flat, mult):
    n = flat.shape[-1]
    rows = -(-n // (LANES * mult)) * mult
    pad = rows * LANES - n
    flat = jnp.pad(flat, [(0, 0)] * (flat.ndim - 1) + [(0, pad)])
    return flat.reshape(flat.shape[:-1] + (rows, LANES))


MATMUL_VMEM_BUDGET = 48 * 1024 * 1024
MATMUL_TM = (1024, 1408, 512, 256, 128)
MATMUL_TN = (3200, 3072, 2816, 2048, 1536, 1408, 1280, 1024, 896, 512, 384, 256, 128)
MATMUL_TK = (3200, 3072, 2816, 2048, 1408, 1280, 1024, 896)
MATMUL_MIN_TK = 896


def _pick(n, cands):
    for c in cands:
        if n % c == 0:
            return c
    return n


def _matmul_tiles(m, n, k, a_bytes, b_bytes, out_bytes, has_res):
    tm = _pick(m, MATMUL_TM)

    def need(tn, tk):
        blocks = 2 * (tm * tk * a_bytes + tk * tn * b_bytes + tm * tn * out_bytes + has_res * tm * tn * 4)
        temps = tm * tn * 4 * (1 + (tk < k)) + (a_bytes > 2) * tm * tk * 2 + (b_bytes > 2) * tk * tn * 2
        return blocks + temps

    for tn in [c for c in MATMUL_TN if n % c == 0] + [n]:
        for tk in [k] + [c for c in MATMUL_TK if k % c == 0 and MATMUL_MIN_TK <= c < k]:
            if need(tn, tk) <= MATMUL_VMEM_BUDGET:
                return tm, tn, tk
    raise ValueError(f"no matmul tiling fits VMEM for {(m, n, k)}")


def _matmul(a, b, ta, tb, out_dtype, name, rider=None, res=None, rope=None):
    (m, k) = (a.shape[1], a.shape[0]) if ta else a.shape
    (k2, n) = (b.shape[1], b.shape[0]) if tb else b.shape
    assert k == k2, (a.shape, b.shape, ta, tb)
    tm, tn, tk = _matmul_tiles(m, n, k, a.dtype.itemsize, b.dtype.itemsize, jnp.dtype(out_dtype).itemsize,
                               res is not None)
    nk = k // tk
    dims = (((0 if ta else 1,), (1 if tb else 0,)), ((), ()))

    grid = (n // tn, m // tm, nk)

    n_in = 2 + (res is not None) + (rope is not None) + (rider is not None)
    if rope is not None:
        assert res is None and tn == n == QKV_GROUP, (tn, n)

    def body(*refs):
        ins, rest = refs[:n_in], refs[n_in:]
        a_ref, b_ref = ins[:2]
        res_ref = ins[2] if res is not None else None
        rope_ref = ins[2] if rope is not None else None
        o_ref, scratch = rest[0], rest[1:]
        if rider is not None:
            start, wait = _exchange_copies(ins[-1], rest[1], *scratch[-3:], rider[1])
            scratch = scratch[1:-3]
            at = [pl.program_id(ax) for ax in range(3)]
            pl.when(functools.reduce(jnp.logical_and, [p == 0 for p in at]))(start)
        part = lax.dot_general(a_ref[...].astype(BF16), b_ref[...].astype(BF16), dims,
                               preferred_element_type=F32)

        def finish(total):
            if rope_ref is not None:
                for c0 in range(0, tn, ATTN_HEAD_DIM):
                    head = total[:, c0:c0 + ATTN_HEAD_DIM]
                    if c0 < 2 * ATTN_OUT_WIDTH:
                        head = _rope(head, rope_ref[...], 1.0)
                    o_ref[:, c0:c0 + ATTN_HEAD_DIM] = head.astype(o_ref.dtype)
                return
            if res_ref is not None:
                total = total + res_ref[...]
            o_ref[...] = total.astype(o_ref.dtype)

        if nk == 1:
            finish(part)
        else:
            acc_ref, = scratch
            kk = pl.program_id(2)

            @pl.when(kk == 0)
            def _():
                acc_ref[...] = part

            @pl.when(kk > 0)
            def _():
                acc_ref[...] += part

            @pl.when(kk == nk - 1)
            def _():
                finish(acc_ref[...])
        if rider is not None:
            pl.when(functools.reduce(jnp.logical_and, [p == g - 1 for p, g in zip(at, grid)]))(wait)

    a_spec = (pl.BlockSpec((tk, tm), lambda j, i, kk: (kk, i)) if ta
              else pl.BlockSpec((tm, tk), lambda j, i, kk: (i, kk)))
    b_spec = (pl.BlockSpec((tn, tk), lambda j, i, kk: (j, kk)) if tb
              else pl.BlockSpec((tk, tn), lambda j, i, kk: (kk, j)))
    any_spec = pl.BlockSpec(memory_space=pl.ANY)
    tile_spec = pl.BlockSpec((tm, tn), lambda j, i, kk: (i, j))
    in_specs, operands = [a_spec, b_spec], [a, b]
    out_shape, out_specs = [jax.ShapeDtypeStruct((m, n), out_dtype)], [tile_spec]
    scratch = [] if nk == 1 else [pltpu.VMEM((tm, tn), F32)]
    sem = ("parallel", "parallel", "arbitrary")
    if res is not None:
        in_specs.append(tile_spec)
        operands.append(res)
    if rope is not None:
        in_specs.append(pl.BlockSpec((tm, rope.shape[1]), lambda j, i, kk: (i, 0)))
        operands.append(rope)
    if rider is not None:
        in_specs.append(any_spec)
        operands.append(rider[0])
        out_shape.append(_exchange_out(*rider))
        out_specs.append(any_spec)
        scratch += list(EXCHANGE_SEMS)
        sem = ("arbitrary",) * 3
    outs = pl.pallas_call(
        body, name=name, out_shape=out_shape, grid=grid, in_specs=in_specs, out_specs=out_specs,
        scratch_shapes=scratch, compiler_params=_params(*sem),
    )(*operands)
    return outs[0] if rider is None else tuple(outs)


@functools.partial(jax.custom_vjp, nondiff_argnums=(3, 4))
def linear(a, w, res, out_dtype, name):
    return _matmul(a, w, False, False, out_dtype, name + "_fwd", res=res)


def _linear_fwd(a, w, res, out_dtype, name):
    return _matmul(a, w, False, False, out_dtype, name + "_fwd", res=res), (a, w, res is not None)


def _linear_bwd(out_dtype, name, saved, dy):
    a, w, has_res = saved
    da = _matmul(dy, w, False, True, a.dtype, name + "_da")
    dw = _matmul(a, dy, True, False, w.dtype, name + "_dw")
    return da, dw, (dy if has_res else None)


linear.defvjp(_linear_fwd, _linear_bwd)


def _sum_pieces(pieces, name):
    shape = pieces.shape[1:]
    c = shape[-1]
    r = math.prod(shape[:-1])
    tr = _pick(r, (512, 256, 128, 64, 32, 16, 8))

    def body(p_ref, o_ref):
        g = p_ref[0].astype(F32)
        for j in range(1, N_DEV):
            g = g + p_ref[j].astype(F32)
        o_ref[...] = g

    return pl.pallas_call(
        body, name=name, out_shape=jax.ShapeDtypeStruct((r, c), F32), grid=(r // tr,),
        in_specs=[pl.BlockSpec((N_DEV, tr, c), lambda i: (0, i, 0))],
        out_specs=pl.BlockSpec((tr, c), lambda i: (i, 0)),
        compiler_params=_params("parallel"),
    )(pieces.reshape(N_DEV, r, c)).reshape(shape)


@functools.partial(jax.custom_vjp, nondiff_argnums=(5, 6))
def linear_x(a, w, res, rope, shard, out_dtype, name):
    return _matmul(a, w, False, False, out_dtype, name + "_fwd", rider=(shard.astype(BF16), True), res=res,
                   rope=rope)


def _linear_x_fwd(a, w, res, rope, shard, out_dtype, name):
    out = _matmul(a, w, False, False, out_dtype, name + "_fwd", rider=(shard.astype(BF16), True), res=res,
                  rope=rope)
    return out, (a, w, res is not None, rope)


def _linear_x_bwd(out_dtype, name, saved, cts):
    a, w, has_res, rope = saved
    dy, dparts = cts
    half = dparts.shape[1] // 2
    da, moved_a = _matmul(dy, w, False, True, a.dtype, name + "_da", rider=(dparts[:, :half], False))
    dw, moved_b = _matmul(a, dy, True, False, w.dtype, name + "_dw", rider=(dparts[:, half:], False))
    dshard = jnp.concatenate([_sum_pieces(moved_a, name + "_sum_a"), _sum_pieces(moved_b, name + "_sum_b")])
    return da, dw, (dy if has_res else None), (None if rope is None else jnp.zeros_like(rope)), dshard


linear_x.defvjp(_linear_x_fwd, _linear_x_bwd)

FIRST_ROW_SPLIT = (0, 384, 704, 1024)


def _qkv_first_fwd(hs, shard, ropes, carried, name):
    parts = _exchange(shard.astype(BF16), True, name + "_gather")
    wfull = _join8(parts, 1)
    ys, moved = [], []
    for g, (h, rope, carry) in enumerate(zip(hs, ropes, carried)):
        y, m = _matmul(h, wfull[:, g * QKV_GROUP:(g + 1) * QKV_GROUP], False, False, BF16, f"{name}{g}_fwd",
                       rider=(carry.astype(BF16), True), rope=rope)
        ys.append(y)
        moved.append(m)
    return (tuple(ys), tuple(moved)), (hs, wfull, ropes)


@functools.partial(jax.custom_vjp, nondiff_argnums=(4,))
def qkv_first(hs, shard, ropes, carried, name):
    return _qkv_first_fwd(hs, shard, ropes, carried, name)[0]


def _qkv_first_vjp_fwd(hs, shard, ropes, carried, name):
    return _qkv_first_fwd(hs, shard, ropes, carried, name)


def _qkv_first_vjp_bwd(name, saved, cts):
    hs, wfull, ropes = saved
    dys, dmoved = cts
    dws, dcarried = [], []
    for g in range(len(hs)):
        dw, back = _matmul(hs[g], dys[g], True, False, BF16, f"{name}{g}_dw", rider=(dmoved[g], False))
        dws.append(dw)
        dcarried.append(_sum_pieces(back, f"{name}{g}_carried_sum"))
    w_shape = wfull.shape
    dparts = jnp.moveaxis(jnp.concatenate(dws, axis=1).reshape(w_shape[0], N_DEV, w_shape[1] // N_DEV), 1, 0)
    das, dshard = [], []
    for g in range(len(hs)):
        rows = slice(FIRST_ROW_SPLIT[g], FIRST_ROW_SPLIT[g + 1])
        da, got = _matmul(dys[g], wfull[:, g * QKV_GROUP:(g + 1) * QKV_GROUP], False, True, BF16, f"{name}{g}_da",
                          rider=(dparts[:, rows], False))
        das.append(da)
        dshard.append(_sum_pieces(got, f"{name}{g}_own_sum"))
    return tuple(das), jnp.concatenate(dshard), tuple(jnp.zeros_like(r) for r in ropes), tuple(dcarried)


qkv_first.defvjp(_qkv_first_vjp_fwd, _qkv_first_vjp_bwd)


VMEM_LIMIT = 56 * 1024 * 1024
SUBLANES = 8


def _params(*sem):
    return pltpu.CompilerParams(dimension_semantics=sem, vmem_limit_bytes=VMEM_LIMIT)


def _sigmoid(x):
    return 0.5 * jnp.tanh(0.5 * x) + 0.5


def _rstd(xv):
    return lax.rsqrt(jnp.mean(xv * xv, axis=-1, keepdims=True) + NORM_EPS)


def _accumulate(ref, part, first):
    @pl.when(first)
    def _():
        ref[...] = part

    @pl.when(jnp.logical_not(first))
    def _():
        ref[...] += part


def _norm_fwd(x, w, name):
    s, d = x.shape
    tm = _pick(s, (1024, 512, 256, 128))

    def body(x_ref, w_ref, h_ref):
        xv = x_ref[...]
        h_ref[...] = (xv * _rstd(xv) * w_ref[...]).astype(h_ref.dtype)

    return pl.pallas_call(
        body, name=name, out_shape=jax.ShapeDtypeStruct((s, d), BF16), grid=(s // tm,),
        in_specs=[pl.BlockSpec((tm, d), lambda i: (i, 0)), pl.BlockSpec((1, d), lambda i: (0, 0))],
        out_specs=pl.BlockSpec((tm, d), lambda i: (i, 0)), compiler_params=_params("parallel"),
    )(x, w.reshape(1, d))


def _norm_bwd(x, w, dh, dskip, name):
    s, d = x.shape
    tm = _pick(s, (1024, 512, 256, 128))

    def body(x_ref, w_ref, dh_ref, ds_ref, dx_ref, dw_ref):
        xv = x_ref[...]
        r = _rstd(xv)
        y = xv * r
        dhv = dh_ref[...].astype(F32)
        dy = dhv * w_ref[...]
        dx_ref[...] = ds_ref[...] + r * (dy - y * jnp.mean(dy * y, axis=-1, keepdims=True))
        _accumulate(dw_ref, jnp.sum(dhv * y, axis=0, keepdims=True), pl.program_id(0) == 0)

    row = pl.BlockSpec((tm, d), lambda i: (i, 0))
    vec = pl.BlockSpec((1, d), lambda i: (0, 0))
    dx, dw = pl.pallas_call(
        body, name=name,
        out_shape=[jax.ShapeDtypeStruct((s, d), F32), jax.ShapeDtypeStruct((1, d), F32)], grid=(s // tm,),
        in_specs=[row, vec, row, row], out_specs=[row, vec], compiler_params=_params("arbitrary"),
    )(x, w.reshape(1, d), dh, dskip)
    return dx, dw.reshape(d)


@functools.partial(jax.custom_vjp, nondiff_argnums=(2,))
def norm(x, w, name):
    return x, _norm_fwd(x, w, name + "_fwd")


def _norm_vjp_fwd(x, w, name):
    return (x, _norm_fwd(x, w, name + "_fwd")), (x, w)


def _norm_vjp_bwd(name, saved, cts):
    x, w = saved
    dskip, dh = cts
    return _norm_bwd(x, w, dh, dskip, name + "_bwd")


norm.defvjp(_norm_vjp_fwd, _norm_vjp_bwd)


def loss_head(x, w, target, name):
    s, d = x.shape
    tm = _pick(s, (1024, 512, 256, 128))

    def body(x_ref, w_ref, t_ref, loss_ref, dx_ref, dw_ref):
        first = pl.program_id(0) == 0
        xv = x_ref[...]
        r = _rstd(xv)
        y = xv * r
        err = y * w_ref[...] - t_ref[...]
        part = 0.5 * jnp.sum(jnp.sum(err * err, axis=-1, keepdims=True), axis=0, keepdims=True) / d
        _accumulate(loss_ref, jnp.broadcast_to(part, loss_ref.shape), first)
        dout = err / d
        dy = dout * w_ref[...]
        dx_ref[...] = r * (dy - y * jnp.mean(dy * y, axis=-1, keepdims=True))
        _accumulate(dw_ref, jnp.sum(dout * y, axis=0, keepdims=True), first)

    row = pl.BlockSpec((tm, d), lambda i: (i, 0))
    vec = pl.BlockSpec((1, d), lambda i: (0, 0))
    loss, dx, dw = pl.pallas_call(
        body, name=name,
        out_shape=[jax.ShapeDtypeStruct((1, 128), F32), jax.ShapeDtypeStruct((s, d), F32),
                   jax.ShapeDtypeStruct((1, d), F32)],
        grid=(s // tm,), in_specs=[row, vec, row],
        out_specs=[pl.BlockSpec((1, 128), lambda i: (0, 0)), row, vec],
        compiler_params=_params("arbitrary"),
    )(x, w.reshape(1, d), target)
    return loss[0, 0], dx, dw.reshape(d)


FFN_ROWS = 256
FFN_COLS = 256


def _shift_down(cur, halo, k):
    out = pltpu.roll(cur, k, axis=0)
    row = lax.broadcasted_iota(jnp.int32, halo.shape, 0)
    top = out[0:SUBLANES]
    for j in range(k):
        top = jnp.where(row == j, halo[SUBLANES - k + j:SUBLANES - k + j + 1, :], top)
    return jnp.concatenate([top, out[SUBLANES:]], axis=0)


def _shift_up(cur, nxt, k):
    n = cur.shape[0]
    out = pltpu.roll(cur, n - k, axis=0)
    row = lax.broadcasted_iota(jnp.int32, nxt.shape, 0)
    bottom = out[n - SUBLANES:]
    for j in range(k):
        bottom = jnp.where(row == SUBLANES - k + j, nxt[j:j + 1, :], bottom)
    return jnp.concatenate([out[:n - SUBLANES], bottom], axis=0)


def _conv_taps(cur, halo, ntaps):
    return [_shift_down(cur, halo, ntaps - 1 - k) if k < ntaps - 1 else cur for k in range(ntaps)]


def _pad_taps(conv_w):
    return jnp.pad(conv_w, ((0, SUBLANES - conv_w.shape[0]), (0, 0)))


def _ffn_mid_fwd(u0, conv_w, conv_b, name):
    s, width = u0.shape
    half = width // 2
    tm = _pick(s, (FFN_ROWS, 128))
    per = tm // SUBLANES

    def body(w_ref, b_ref, u_ref, halo_ref, a_ref, pre_ref):
        keep = pl.program_id(0) > 0
        for c0 in range(0, half, FFN_COLS):
            vals = []
            for base in (c0, half + c0):
                cols = slice(base, base + FFN_COLS)
                halo = jnp.where(keep, halo_ref[:, cols], 0.0)
                taps = _conv_taps(u_ref[:, cols], halo, 3)
                vals.append(sum(w_ref[k:k + 1, cols] * taps[k] for k in range(3)) + b_ref[:, cols])
                pre_ref[:, cols] = vals[-1].astype(pre_ref.dtype)
            gate, up = vals
            a_ref[:, c0:c0 + FFN_COLS] = (gate * _sigmoid(gate) * up).astype(a_ref.dtype)

    return pl.pallas_call(
        body, name=name,
        out_shape=[jax.ShapeDtypeStruct((s, half), BF16), jax.ShapeDtypeStruct((s, width), BF16)], grid=(s // tm,),
        in_specs=[pl.BlockSpec((SUBLANES, width), lambda i: (0, 0)), pl.BlockSpec((1, width), lambda i: (0, 0)),
                  pl.BlockSpec((tm, width), lambda i: (i, 0)),
                  pl.BlockSpec((SUBLANES, width), lambda i: (jnp.maximum(i * per - 1, 0), 0))],
        out_specs=[pl.BlockSpec((tm, half), lambda i: (i, 0)), pl.BlockSpec((tm, width), lambda i: (i, 0))],
        compiler_params=_params("parallel"),
    )(_pad_taps(conv_w), conv_b.reshape(1, width), u0, u0)


def _ffn_mid_bwd(u0, pre, conv_w, da, name):
    s, width = u0.shape
    half = width // 2
    tm = _pick(s, (FFN_ROWS, 128))
    nt = s // tm

    def body(w_ref, u_ref, pre_ref, da_ref, du0_ref, dw_ref, db_ref, carry_ref):
        first = pl.program_id(0) == 0
        for c0 in range(0, half, FFN_COLS):
            dav = da_ref[:, c0:c0 + FFN_COLS].astype(F32)
            gate = pre_ref[:, c0:c0 + FFN_COLS].astype(F32)
            up = pre_ref[:, half + c0:half + c0 + FFN_COLS].astype(F32)
            sig = _sigmoid(gate)
            dus = [dav * up * sig * (1.0 + gate * (1.0 - sig)), dav * gate * sig]
            for base, du in zip((c0, half + c0), dus):
                cols = slice(base, base + FFN_COLS)
                nxt = jnp.where(first, 0.0, carry_ref[:, cols])
                ahead = [_shift_up(du, nxt, 2), _shift_up(du, nxt, 1), du]
                du0 = sum(w_ref[k:k + 1, cols] * ahead[k] for k in range(3))
                du0_ref[:, cols] = du0.astype(du0_ref.dtype)
                carry_ref[:, cols] = du[0:SUBLANES, :]
                x = u_ref[:, cols]
                dwp = jnp.concatenate([jnp.sum(ahead[k] * x, axis=0, keepdims=True) for k in range(3)]
                                      + [jnp.zeros((SUBLANES - 3, FFN_COLS), F32)], axis=0)
                dbp = jnp.sum(du, axis=0, keepdims=True)

                @pl.when(first)
                def _():
                    dw_ref[:, cols] = dwp
                    db_ref[:, cols] = dbp

                @pl.when(jnp.logical_not(first))
                def _():
                    dw_ref[:, cols] += dwp
                    db_ref[:, cols] += dbp

    rev = lambda i: nt - 1 - i
    du0, dw, db = pl.pallas_call(
        body, name=name,
        out_shape=[jax.ShapeDtypeStruct((s, width), BF16), jax.ShapeDtypeStruct((SUBLANES, width), F32),
                   jax.ShapeDtypeStruct((1, width), F32)],
        grid=(nt,),
        in_specs=[pl.BlockSpec((SUBLANES, width), lambda i: (0, 0)),
                  pl.BlockSpec((tm, width), lambda i: (rev(i), 0)), pl.BlockSpec((tm, width), lambda i: (rev(i), 0)),
                  pl.BlockSpec((tm, half), lambda i: (rev(i), 0))],
        out_specs=[pl.BlockSpec((tm, width), lambda i: (rev(i), 0)),
                   pl.BlockSpec((SUBLANES, width), lambda i: (0, 0)), pl.BlockSpec((1, width), lambda i: (0, 0))],
        scratch_shapes=[pltpu.VMEM((SUBLANES, width), F32)],
        compiler_params=_params("arbitrary"),
    )(_pad_taps(conv_w), u0, pre, da)
    return du0, dw[:3], db.reshape(width)


@functools.partial(jax.custom_vjp, nondiff_argnums=(3,))
def ffn_mid(u0, conv_w, conv_b, name):
    return _ffn_mid_fwd(u0, conv_w, conv_b, name + "_fwd")[0]


def _ffn_mid_vjp_fwd(u0, conv_w, conv_b, name):
    a, pre = _ffn_mid_fwd(u0, conv_w, conv_b, name + "_fwd")
    return a, (u0, pre, conv_w)


def _ffn_mid_vjp_bwd(name, res, da):
    u0, pre, conv_w = res
    return _ffn_mid_bwd(u0, pre, conv_w, da, name + "_bwd")


ffn_mid.defvjp(_ffn_mid_vjp_fwd, _ffn_mid_vjp_bwd)


NEG = -1e30
HEAD_SLICES = [slice(hh * ATTN_HEAD_DIM, (hh + 1) * ATTN_HEAD_DIM) for hh in range(ATTN_HEADS)]
ATTN_SCALE = ATTN_HEAD_DIM ** -0.5
QKV_GROUP = 3 * ATTN_OUT_WIDTH


def rope_table(seq, d):
    pos = (jnp.arange(seq // d, dtype=jnp.int32)[None, :] * d + jnp.arange(d, dtype=jnp.int32)[:, None])
    inv_freq = ROPE_THETA ** (-jnp.arange(0, ROPE_DIM, 2, dtype=F32) / ROPE_DIM)
    ang = pos.reshape(seq).astype(F32)[:, None] * inv_freq[None, :]
    cos, sin = jnp.cos(ang), jnp.sin(ang)
    half = ROPE_DIM // 2
    ones = jnp.ones((seq, ATTN_HEAD_DIM - ROPE_DIM), F32)
    zero = lambda n: jnp.zeros((seq, n), F32)
    return jnp.concatenate([cos, cos, ones, -sin, zero(ATTN_HEAD_DIM - half),
                            zero(half), sin, zero(ATTN_HEAD_DIM - ROPE_DIM)], axis=1)


def _rope(t, tab, sign):
    half = ROPE_DIM // 2
    return t * tab[:, 0:128] + sign * (pltpu.roll(t, ATTN_HEAD_DIM - half, axis=1) * tab[:, 128:256]
                                       + pltpu.roll(t, half, axis=1) * tab[:, 256:384])


def _to_dilated(a, d):
    s = a.shape[0]
    return a if d == 1 else a.reshape(s // d, d, -1).transpose(1, 0, 2).reshape(s, -1)


def _from_dilated(a, d):
    s = a.shape[0]
    return a if d == 1 else a.reshape(d, s // d, -1).transpose(1, 0, 2).reshape(s, -1)


def _dot_nt(a, b):
    return lax.dot_general(a, b, (((1,), (1,)), ((), ())), preferred_element_type=F32)


def _dot_tn(a, b):
    return lax.dot_general(a, b, (((0,), (0,)), ((), ())), preferred_element_type=F32)


def _dot(a, b):
    return jnp.dot(a, b, preferred_element_type=F32)


def _window_mask(has_prev):
    ii = lax.broadcasted_iota(jnp.int32, (ATTN_BLOCK, 2 * ATTN_BLOCK), 0)
    jj = lax.broadcasted_iota(jnp.int32, (ATTN_BLOCK, 2 * ATTN_BLOCK), 1)
    in_window = jnp.logical_and(jj >= ii, jj <= ii + ATTN_BLOCK)
    return jnp.logical_and(in_window, jnp.logical_or(jj >= ATTN_BLOCK, has_prev))


def _both(prev_ref, cur_ref, hs):
    return jnp.concatenate([prev_ref[:, hs], cur_ref[:, hs]], axis=0)


def _attn_group_fwd(qkv, d, name):
    s = qkv.shape[0]
    nb = s // d // ATTN_BLOCK

    def body(q_ref, kc_ref, kp_ref, vc_ref, vp_ref, o_ref, lse_ref):
        mask = _window_mask(pl.program_id(1) > 0)
        lane = lax.broadcasted_iota(jnp.int32, (ATTN_BLOCK, 128), 1)
        lse_tile = jnp.zeros((ATTN_BLOCK, 128), F32)
        for hh, hs in enumerate(HEAD_SLICES):
            sc = jnp.where(mask, _dot_nt(q_ref[:, hs], _both(kp_ref, kc_ref, hs)) * ATTN_SCALE, NEG)
            m = jnp.max(sc, axis=1, keepdims=True)
            p = jnp.exp(sc - m)
            den = jnp.sum(p, axis=1, keepdims=True)
            o_ref[:, hs] = (_dot(p.astype(BF16), _both(vp_ref, vc_ref, hs)) / den).astype(o_ref.dtype)
            lse_tile = jnp.where(lane == hh, m + jnp.log(den), lse_tile)
        lse_ref[...] = lse_tile

    cur = lambda t: pl.BlockSpec((ATTN_BLOCK, ATTN_OUT_WIDTH), lambda r, n: (r * nb + n, t))
    prv = lambda t: pl.BlockSpec((ATTN_BLOCK, ATTN_OUT_WIDTH), lambda r, n: (r * nb + jnp.maximum(n - 1, 0), t))
    return pl.pallas_call(
        body, name=name,
        out_shape=[jax.ShapeDtypeStruct((s, ATTN_OUT_WIDTH), BF16), jax.ShapeDtypeStruct((s, 128), F32)],
        grid=(d, nb), in_specs=[cur(0), cur(1), prv(1), cur(2), prv(2)],
        out_specs=[pl.BlockSpec((ATTN_BLOCK, ATTN_OUT_WIDTH), lambda r, n: (r * nb + n, 0)),
                   pl.BlockSpec((ATTN_BLOCK, 128), lambda r, n: (r * nb + n, 0))],
        compiler_params=_params("parallel", "parallel"),
    )(qkv, qkv, qkv, qkv, qkv)


def _attn_combine(os_, lses, name):
    s = os_[0].shape[0]
    tm = _pick(s, (512, 256, 128))
    ng = len(os_)

    def body(*refs):
        o_refs, l_refs, (o_ref, lse_ref) = refs[:ng], refs[ng:2 * ng], refs[2 * ng:]
        lane = lax.broadcasted_iota(jnp.int32, (tm, 128), 1)
        lse_tile = jnp.zeros((tm, 128), F32)
        for hh, hs in enumerate(HEAD_SLICES):
            ls = [l_ref[:, hh:hh + 1] for l_ref in l_refs]
            m = functools.reduce(jnp.maximum, ls)
            ws = [jnp.exp(l - m) for l in ls]
            tot = functools.reduce(lambda a, b: a + b, ws)
            acc = functools.reduce(lambda a, b: a + b, [o_r[:, hs].astype(F32) * w for o_r, w in zip(o_refs, ws)])
            o_ref[:, hs] = (acc / tot).astype(o_ref.dtype)
            lse_tile = jnp.where(lane == hh, m + jnp.log(tot), lse_tile)
        lse_ref[...] = lse_tile

    wide = pl.BlockSpec((tm, ATTN_OUT_WIDTH), lambda i: (i, 0))
    thin = pl.BlockSpec((tm, 128), lambda i: (i, 0))
    return pl.pallas_call(
        body, name=name,
        out_shape=[jax.ShapeDtypeStruct((s, ATTN_OUT_WIDTH), BF16), jax.ShapeDtypeStruct((s, 128), F32)],
        grid=(s // tm,), in_specs=[wide] * ng + [thin] * ng, out_specs=[wide, thin],
        compiler_params=_params("parallel"),
    )(*os_, *lses)


def _attn_delta(do, o, name):
    s = do.shape[0]
    tm = _pick(s, (512, 256, 128))

    def body(do_ref, o_ref, out_ref):
        lane = lax.broadcasted_iota(jnp.int32, (tm, 128), 1)
        tile = jnp.zeros((tm, 128), F32)
        for hh, hs in enumerate(HEAD_SLICES):
            prod = do_ref[:, hs].astype(F32) * o_ref[:, hs].astype(F32)
            tile = jnp.where(lane == hh, jnp.sum(prod, axis=1, keepdims=True), tile)
        out_ref[...] = tile

    wide = pl.BlockSpec((tm, ATTN_OUT_WIDTH), lambda i: (i, 0))
    return pl.pallas_call(
        body, name=name, out_shape=jax.ShapeDtypeStruct((s, 128), F32), grid=(s // tm,),
        in_specs=[wide, wide], out_specs=pl.BlockSpec((tm, 128), lambda i: (i, 0)),
        compiler_params=_params("parallel"),
    )(do, o)


def _attn_group_bwd(qkv, do, lse, delta, tab, d, name):
    s = qkv.shape[0]
    nb = s // d // ATTN_BLOCK
    wide = ATTN_OUT_WIDTH

    def body(q_ref, qn_ref, k_ref, v_ref, do_ref, don_ref, lse_ref, lsen_ref, dl_ref, dln_ref, tab_ref,
             out_ref, carry_ref):
        n = pl.program_id(1)

        @pl.when(n == 0)
        def _():
            carry_ref[...] = jnp.zeros_like(carry_ref)

        rows = lax.broadcasted_iota(jnp.int32, (2 * ATTN_BLOCK, ATTN_BLOCK), 0)
        keys = lax.broadcasted_iota(jnp.int32, (2 * ATTN_BLOCK, ATTN_BLOCK), 1)
        own = jnp.logical_and(rows < ATTN_BLOCK, keys <= rows)
        nxt = jnp.logical_and(rows >= ATTN_BLOCK, jnp.logical_and(keys >= rows - ATTN_BLOCK, n + 1 < nb))
        mask = jnp.logical_or(own, nxt)
        both = lambda a_ref, b_ref, cols: jnp.concatenate([a_ref[:, cols], b_ref[:, cols]], axis=0)
        for hh, hs in enumerate(HEAD_SLICES):
            one = slice(hh, hh + 1)
            q2, do2 = both(q_ref, qn_ref, hs), both(do_ref, don_ref, hs)
            kh, vh = k_ref[:, hs], v_ref[:, hs]
            sc = jnp.where(mask, _dot_nt(q2, kh) * ATTN_SCALE, NEG)
            p = jnp.exp(sc - both(lse_ref, lsen_ref, one))
            ds = (p * (_dot_nt(do2, vh) - both(dl_ref, dln_ref, one)) * ATTN_SCALE).astype(BF16)
            dq2 = _dot(ds, kh)
            dq = carry_ref[:, hs] + dq2[:ATTN_BLOCK]
            carry_ref[:, hs] = dq2[ATTN_BLOCK:]
            out_ref[:, hs] = _rope(dq, tab_ref[...], -1.0).astype(out_ref.dtype)
            out_ref[:, wide + hh * ATTN_HEAD_DIM:wide + (hh + 1) * ATTN_HEAD_DIM] = _rope(
                _dot_tn(ds, q2), tab_ref[...], -1.0).astype(out_ref.dtype)
            out_ref[:, 2 * wide + hh * ATTN_HEAD_DIM:2 * wide + (hh + 1) * ATTN_HEAD_DIM] = _dot_tn(
                p.astype(BF16), do2).astype(out_ref.dtype)

    def spec(width, row, col):
        return pl.BlockSpec((ATTN_BLOCK, width), lambda r, n: (r * nb + row(n), col))

    cur = lambda n: n
    nxt_block = lambda n: jnp.minimum(n + 1, nb - 1)
    return pl.pallas_call(
        body, name=name, out_shape=jax.ShapeDtypeStruct((s, QKV_GROUP), BF16), grid=(d, nb),
        in_specs=[spec(wide, cur, 0), spec(wide, nxt_block, 0), spec(wide, cur, 1), spec(wide, cur, 2),
                  spec(wide, cur, 0), spec(wide, nxt_block, 0), spec(128, cur, 0), spec(128, nxt_block, 0),
                  spec(128, cur, 0), spec(128, nxt_block, 0), spec(384, cur, 0)],
        out_specs=spec(QKV_GROUP, cur, 0),
        scratch_shapes=[pltpu.VMEM((ATTN_BLOCK, wide), F32)],
        compiler_params=_params("parallel", "arbitrary"),
    )(qkv, qkv, qkv, qkv, do, do, lse, lse, delta, delta, tab)


def _attn_core_fwd(qkvs, name):
    os_, lses = [], []
    for g, (qkv, d) in enumerate(zip(qkvs, ATTN_DILATIONS)):
        o_g, lse_g = _attn_group_fwd(qkv, d, f"{name}_fwd{g}")
        os_.append(_from_dilated(o_g, d))
        lses.append(_from_dilated(lse_g, d))
    o, lse = _attn_combine(os_, lses, name + "_combine")
    return o, (tuple(qkvs), o, lse)


@functools.partial(jax.custom_vjp, nondiff_argnums=(1,))
def attn_core(qkvs, name):
    return _attn_core_fwd(qkvs, name)[0]


def _attn_core_vjp_fwd(qkvs, name):
    return _attn_core_fwd(qkvs, name)


def _attn_core_vjp_bwd(name, res, do):
    rot, o, lse = res
    delta = _attn_delta(do, o, name + "_delta")
    out = []
    for g, (qkv, d) in enumerate(zip(rot, ATTN_DILATIONS)):
        out.append(_attn_group_bwd(qkv, _to_dilated(do, d), _to_dilated(lse, d), _to_dilated(delta, d),
                                   rope_table(o.shape[0], d), d, f"{name}_bwd{g}"))
    return (tuple(out),)


attn_core.defvjp(_attn_core_vjp_fwd, _attn_core_vjp_bwd)


def attention_mixer_p(x, h, lin, j, carries, tag):
    qkvs = tuple(lin(_to_dilated(h, d), ('attn_w_qkv', j), BF16, f"{tag}_qkv{g}",
                     cols=slice(g * QKV_GROUP, (g + 1) * QKV_GROUP), carry=carries[g],
                     rope=rope_table(h.shape[0], d))
                 for g, d in enumerate(ATTN_DILATIONS))
    return lin(attn_core(qkvs, tag), ('attn_w_o', j), F32, tag + "_o", res=x)


def first_attention_mixer(x, h, lin, w, ready, carries, tag):
    s = h.shape[0]
    qkvs, moved = qkv_first(tuple(_to_dilated(h, d) for d in ATTN_DILATIONS), w['attn_w_qkv'][0],
                            tuple(rope_table(s, d) for d in ATTN_DILATIONS),
                            tuple(w[n][j] for n, j in carries), tag + "_qkv")
    for (n, j), parts in zip(carries, moved):
        ready[(n, j)] = _full_weight(n, parts)
    return lin(attn_core(qkvs, tag), ('attn_w_o', 0), F32, tag + "_o", res=x)


SSM_CONV_TAPS = 4
SSM_COL_BLOCK = 2048
SSM_PAIRS = SSM_HEADS // 2
SSM_DT_BLOCK = (SSM_D_INNER + SSM_CONV_DIM) // 128


def _ssm_conv_fwd(zx, conv_w, conv_b, name):
    s = zx.shape[0]
    tm = _pick(s, (512, 256, 128))
    per = tm // SUBLANES
    ncb = SSM_CONV_DIM // SSM_COL_BLOCK

    def body(w_ref, b_ref, x_ref, halo_ref, o_ref, pre_ref):
        keep = pl.program_id(1) > 0
        for c0 in range(0, SSM_COL_BLOCK, FFN_COLS):
            cols = slice(c0, c0 + FFN_COLS)
            halo = jnp.where(keep, halo_ref[:, cols], 0.0)
            taps = _conv_taps(x_ref[:, cols], halo, SSM_CONV_TAPS)
            pre = sum(w_ref[k:k + 1, cols] * taps[k] for k in range(SSM_CONV_TAPS)) + b_ref[:, cols]
            pre_ref[:, cols] = pre.astype(pre_ref.dtype)
            o_ref[:, cols] = pre * _sigmoid(pre)

    blk = pl.BlockSpec((tm, SSM_COL_BLOCK), lambda j, i: (i, j))
    return pl.pallas_call(
        body, name=name,
        out_shape=[jax.ShapeDtypeStruct((s, SSM_CONV_DIM), F32), jax.ShapeDtypeStruct((s, SSM_CONV_DIM), BF16)],
        grid=(ncb, s // tm),
        in_specs=[pl.BlockSpec((SUBLANES, SSM_COL_BLOCK), lambda j, i: (0, j)),
                  pl.BlockSpec((1, SSM_COL_BLOCK), lambda j, i: (0, j)),
                  pl.BlockSpec((tm, SSM_COL_BLOCK), lambda j, i: (i, j + 1)),
                  pl.BlockSpec((SUBLANES, SSM_COL_BLOCK), lambda j, i: (jnp.maximum(i * per - 1, 0), j + 1))],
        out_specs=[blk, blk], compiler_params=_params("parallel", "parallel"),
    )(_pad_taps(conv_w), conv_b.reshape(1, SSM_CONV_DIM), zx, zx)


def _ssm_conv_bwd(zx, pre, conv_w, dact, dzx, name):
    s = zx.shape[0]
    tm = _pick(s, (512, 256, 128))
    nt = s // tm
    ncb = SSM_CONV_DIM // SSM_COL_BLOCK
    nk = SSM_CONV_TAPS

    def body(w_ref, x_ref, pre_ref, da_ref, _, dx_ref, dw_ref, db_ref, carry_ref):
        first = pl.program_id(1) == 0
        for c0 in range(0, SSM_COL_BLOCK, FFN_COLS):
            cols = slice(c0, c0 + FFN_COLS)
            prev = pre_ref[:, cols].astype(F32)
            sig = _sigmoid(prev)
            dpre = da_ref[:, cols] * sig * (1.0 + prev * (1.0 - sig))
            nxt = jnp.where(first, 0.0, carry_ref[:, cols])
            ahead = [_shift_up(dpre, nxt, nk - 1 - k) for k in range(nk - 1)] + [dpre]
            dx_ref[:, cols] = sum(w_ref[k:k + 1, cols] * ahead[k] for k in range(nk)).astype(dx_ref.dtype)
            carry_ref[:, cols] = dpre[0:SUBLANES, :]
            x = x_ref[:, cols]
            dwp = jnp.concatenate([jnp.sum(ahead[k] * x, axis=0, keepdims=True) for k in range(nk)]
                                  + [jnp.zeros((SUBLANES - nk, FFN_COLS), F32)], axis=0)
            dbp = jnp.sum(dpre, axis=0, keepdims=True)

            @pl.when(first)
            def _():
                dw_ref[:, cols] = dwp
                db_ref[:, cols] = dbp

            @pl.when(jnp.logical_not(first))
            def _():
                dw_ref[:, cols] += dwp
                db_ref[:, cols] += dbp

    rev = lambda i: nt - 1 - i
    dzx, dw, db = pl.pallas_call(
        body, name=name,
        out_shape=[jax.ShapeDtypeStruct(dzx.shape, dzx.dtype), jax.ShapeDtypeStruct((SUBLANES, SSM_CONV_DIM), F32),
                   jax.ShapeDtypeStruct((1, SSM_CONV_DIM), F32)],
        grid=(ncb, nt),
        in_specs=[pl.BlockSpec((SUBLANES, SSM_COL_BLOCK), lambda j, i: (0, j)),
                  pl.BlockSpec((tm, SSM_COL_BLOCK), lambda j, i: (rev(i), j + 1)),
                  pl.BlockSpec((tm, SSM_COL_BLOCK), lambda j, i: (rev(i), j)),
                  pl.BlockSpec((tm, SSM_COL_BLOCK), lambda j, i: (rev(i), j)),
                  pl.BlockSpec(memory_space=pl.ANY)],
        out_specs=[pl.BlockSpec((tm, SSM_COL_BLOCK), lambda j, i: (rev(i), j + 1)),
                   pl.BlockSpec((SUBLANES, SSM_COL_BLOCK), lambda j, i: (0, j)),
                   pl.BlockSpec((1, SSM_COL_BLOCK), lambda j, i: (0, j))],
        scratch_shapes=[pltpu.VMEM((SUBLANES, SSM_COL_BLOCK), F32)],
        input_output_aliases={4: 0},
        compiler_params=_params("parallel", "arbitrary"),
    )(_pad_taps(conv_w), zx, pre, dact, dzx)
    return dzx, dw[:nk], db.reshape(SSM_CONV_DIM)


def _ssd_chunk(xs, bms, cms, dt_raw, dtb, alog, dsk, states):
    q = SSM_CHUNK
    lane = lax.broadcasted_iota(jnp.int32, (1, 128), 1)
    row = lax.broadcasted_iota(jnp.int32, (q, 1), 0)
    ii = lax.broadcasted_iota(jnp.int32, (q, q), 0)
    jj = lax.broadcasted_iota(jnp.int32, (q, q), 1)
    tril = ii >= jj
    left = lane < SSM_HEAD_DIM
    last_row = (row == q - 1).astype(F32)

    def lanes_of(mat, h):
        pick = (lane == h).astype(F32)
        return jnp.broadcast_to(jnp.sum(mat * pick, axis=1, keepdims=True), mat.shape)

    def rows_of(mat_t, h):
        pick = (row == h).astype(F32)
        return jnp.broadcast_to(jnp.sum(mat_t * pick, axis=0, keepdims=True), mat_t.shape)

    v = dt_raw + dtb
    dt = jnp.maximum(v, 0.0) + jnp.log(1.0 + jnp.exp(-jnp.abs(v)))
    adt = dt * (-jnp.exp(alog))
    acs = jnp.dot(tril.astype(F32), adt, precision=lax.Precision.HIGHEST, preferred_element_type=F32)
    acs_t = acs.T
    ys, new_states = [], []
    for pr in range(SSM_PAIRS):
        g = pr // 2
        if pr % 2 == 0:
            cb = _dot_nt(cms[g].astype(BF16), bms[g].astype(BF16))
        cols = [lanes_of(acs, 2 * pr + e) for e in range(2)]
        dts = [lanes_of(dt, 2 * pr + e) for e in range(2)]
        rws = [rows_of(acs_t, 2 * pr + e) for e in range(2)]
        lasts = [jnp.sum(c * last_row, axis=0, keepdims=True) for c in cols]
        xdt = xs[pr] * jnp.where(left, dts[0], dts[1])
        halves = [jnp.where(left, xdt, 0.0).astype(BF16), jnp.where(left, 0.0, xdt).astype(BF16)]
        y_diag, s_new = 0.0, 0.0
        for e in range(2):
            lmat = jnp.where(tril, jnp.exp(jnp.minimum(cols[e] - rws[e], 0.0)), 0.0)
            y_diag = y_diag + _dot((cb * lmat).astype(BF16), halves[e])
            decay = jnp.exp(lasts[e] - cols[e])
            s_new = s_new + _dot_tn((bms[g] * decay).astype(BF16), halves[e])
        y_off = _dot(cms[g].astype(BF16), states[pr].astype(BF16)) * jnp.where(left, jnp.exp(cols[0]), jnp.exp(cols[1]))
        skip = jnp.where(left, lanes_of(dsk, 2 * pr), lanes_of(dsk, 2 * pr + 1))
        ys.append(y_diag + y_off + xs[pr] * skip)
        new_states.append(states[pr] * jnp.where(left, jnp.exp(lasts[0]), jnp.exp(lasts[1])) + s_new)
    return tuple(ys), tuple(new_states)


def _ssm_vec(v):
    return jnp.pad(v.reshape(1, -1), ((0, 0), (0, 128 - v.shape[0])))


def _ssd_scan_fwd(act, zx, dtb, alog, dsk, name):
    s = act.shape[0]
    nc = s // SSM_CHUNK
    ng = SSM_GROUPS

    def body(act_ref, dt_ref, dtb_ref, alog_ref, dsk_ref, y_ref, st_out_ref, st_ref):
        @pl.when(pl.program_id(0) == 0)
        def _():
            st_ref[...] = jnp.zeros_like(st_ref)

        tile = lambda k: act_ref[:, k * 128:(k + 1) * 128]
        xs = [tile(k) for k in range(SSM_PAIRS)]
        bms = [tile(SSM_PAIRS + k) for k in range(ng)]
        cms = [tile(SSM_PAIRS + ng + k) for k in range(ng)]
        states = [st_ref[k] for k in range(SSM_PAIRS)]
        st_out_ref[0] = st_ref[...]
        ys, new_states = _ssd_chunk(xs, bms, cms, dt_ref[...], dtb_ref[...], alog_ref[...], dsk_ref[...], states)
        for k in range(SSM_PAIRS):
            y_ref[:, k * 128:(k + 1) * 128] = ys[k]
            st_ref[k] = new_states[k]

    vec = pl.BlockSpec((1, 128), lambda c: (0, 0))
    return pl.pallas_call(
        body, name=name,
        out_shape=[jax.ShapeDtypeStruct((s, SSM_D_INNER), F32),
                   jax.ShapeDtypeStruct((nc, SSM_PAIRS, SSM_STATE, 128), F32)],
        grid=(nc,),
        in_specs=[pl.BlockSpec((SSM_CHUNK, SSM_CONV_DIM), lambda c: (c, 0)),
                  pl.BlockSpec((SSM_CHUNK, 128), lambda c: (c, SSM_DT_BLOCK)), vec, vec, vec],
        out_specs=[pl.BlockSpec((SSM_CHUNK, SSM_D_INNER), lambda c: (c, 0)),
                   pl.BlockSpec((1, SSM_PAIRS, SSM_STATE, 128), lambda c: (c, 0, 0, 0))],
        scratch_shapes=[pltpu.VMEM((SSM_PAIRS, SSM_STATE, 128), F32)],
        compiler_params=_params("arbitrary"),
    )(act, zx, _ssm_vec(dtb), _ssm_vec(alog), _ssm_vec(dsk))


def _ssd_scan_bwd(act, zx, dtb, alog, dsk, st_in, dy, dzx, name):
    s = act.shape[0]
    nc = s // SSM_CHUNK
    ng = SSM_GROUPS
    dt_col = SSM_D_INNER + SSM_CONV_DIM
    tail = zx.shape[1] - dt_col
    assert tail % 128 == 0 and dt_col % tail == 0, tail

    def body(act_ref, dt_ref, dtb_ref, alog_ref, dsk_ref, st_ref, dy_ref, _, dact_ref, ddt_ref, dpar_ref, dst_ref):
        first = pl.program_id(0) == 0

        @pl.when(first)
        def _():
            dst_ref[...] = jnp.zeros_like(dst_ref)

        tile = lambda k: act_ref[:, k * 128:(k + 1) * 128]
        xs = [tile(k) for k in range(SSM_PAIRS)]
        bms = [tile(SSM_PAIRS + k) for k in range(ng)]
        cms = [tile(SSM_PAIRS + ng + k) for k in range(ng)]
        states = [st_ref[0, k] for k in range(SSM_PAIRS)]
        _, pullback = jax.vjp(_ssd_chunk, xs, bms, cms, dt_ref[...], dtb_ref[...], alog_ref[...], dsk_ref[...],
                              states)
        dys = tuple(dy_ref[:, k * 128:(k + 1) * 128] for k in range(SSM_PAIRS))
        dsts = tuple(dst_ref[k] for k in range(SSM_PAIRS))
        dxs, dbms, dcms, ddt, ddtb, dalog, ddsk, dstates = pullback((dys, dsts))
        for k, t in enumerate(list(dxs) + list(dbms) + list(dcms)):
            dact_ref[:, k * 128:(k + 1) * 128] = t
        ddt_ref[:, 0:128] = ddt.astype(ddt_ref.dtype)
        if tail > 128:
            ddt_ref[:, 128:] = jnp.zeros((SSM_CHUNK, tail - 128), ddt_ref.dtype)
        for k in range(SSM_PAIRS):
            dst_ref[k] = dstates[k]
        dpar = jnp.concatenate([ddtb, dalog, ddsk, jnp.zeros((SUBLANES - 3, 128), F32)], axis=0)
        _accumulate(dpar_ref, dpar, first)

    rev = lambda c: nc - 1 - c
    vec = pl.BlockSpec((1, 128), lambda c: (0, 0))
    dact, dzx, dpar = pl.pallas_call(
        body, name=name,
        out_shape=[jax.ShapeDtypeStruct((s, SSM_CONV_DIM), F32), jax.ShapeDtypeStruct(dzx.shape, dzx.dtype),
                   jax.ShapeDtypeStruct((SUBLANES, 128), F32)],
        grid=(nc,),
        in_specs=[pl.BlockSpec((SSM_CHUNK, SSM_CONV_DIM), lambda c: (rev(c), 0)),
                  pl.BlockSpec((SSM_CHUNK, 128), lambda c: (rev(c), SSM_DT_BLOCK)), vec, vec, vec,
                  pl.BlockSpec((1, SSM_PAIRS, SSM_STATE, 128), lambda c: (rev(c), 0, 0, 0)),
                  pl.BlockSpec((SSM_CHUNK, SSM_D_INNER), lambda c: (rev(c), 0)),
                  pl.BlockSpec(memory_space=pl.ANY)],
        out_specs=[pl.BlockSpec((SSM_CHUNK, SSM_CONV_DIM), lambda c: (rev(c), 0)),
                   pl.BlockSpec((SSM_CHUNK, tail), lambda c: (rev(c), dt_col // tail)),
                   pl.BlockSpec((SUBLANES, 128), lambda c: (0, 0))],
        scratch_shapes=[pltpu.VMEM((SSM_PAIRS, SSM_STATE, 128), F32)],
        input_output_aliases={7: 1},
        compiler_params=_params("arbitrary"),
    )(act, zx, _ssm_vec(dtb), _ssm_vec(alog), _ssm_vec(dsk), st_in, dy, dzx)
    return dact, dzx, dpar[0, :SSM_HEADS], dpar[1, :SSM_HEADS], dpar[2, :SSM_HEADS]


SSM_NORM_GROUP = SSM_D_INNER // SSM_GROUPS


def _gated_group(y, z, w):
    g = y * (z * _sigmoid(z))
    return g * lax.rsqrt(jnp.mean(g * g, axis=-1, keepdims=True) + NORM_EPS) * w


def _gated_norm_fwd(y, zx, w, name):
    s = y.shape[0]
    tm = _pick(s, (512, 256, 128))

    def body(y_ref, z_ref, w_ref, o_ref):
        for c0 in range(0, SSM_D_INNER, SSM_NORM_GROUP):
            cols = slice(c0, c0 + SSM_NORM_GROUP)
            o_ref[:, cols] = _gated_group(y_ref[:, cols], z_ref[:, cols], w_ref[:, cols]).astype(o_ref.dtype)

    blk = pl.BlockSpec((tm, SSM_D_INNER), lambda i: (i, 0))
    return pl.pallas_call(
        body, name=name, out_shape=jax.ShapeDtypeStruct((s, SSM_D_INNER), BF16), grid=(s // tm,),
        in_specs=[blk, blk, pl.BlockSpec((1, SSM_D_INNER), lambda i: (0, 0))], out_specs=blk,
        compiler_params=_params("parallel"),
    )(y, zx, w.reshape(1, SSM_D_INNER))


def _gated_norm_bwd(y, zx, w, dout, name):
    s = y.shape[0]
    tm = _pick(s, (512, 256, 128))

    def body(y_ref, z_ref, w_ref, do_ref, dy_ref, dz_ref, dw_ref):
        first = pl.program_id(0) == 0
        for c0 in range(0, SSM_D_INNER, SSM_NORM_GROUP):
            cols = slice(c0, c0 + SSM_NORM_GROUP)
            _, pullback = jax.vjp(_gated_group, y_ref[:, cols], z_ref[:, cols], w_ref[:, cols])
            dyv, dzv, dwv = pullback(do_ref[:, cols].astype(F32))
            dy_ref[:, cols] = dyv
            dz_ref[:, cols] = dzv.astype(dz_ref.dtype)

            @pl.when(first)
            def _():
                dw_ref[:, cols] = dwv

            @pl.when(jnp.logical_not(first))
            def _():
                dw_ref[:, cols] += dwv

    blk = pl.BlockSpec((tm, SSM_D_INNER), lambda i: (i, 0))
    vec = pl.BlockSpec((1, SSM_D_INNER), lambda i: (0, 0))
    dy, dz, dw = pl.pallas_call(
        body, name=name,
        out_shape=[jax.ShapeDtypeStruct((s, SSM_D_INNER), F32), jax.ShapeDtypeStruct((s, zx.shape[1]), BF16),
                   jax.ShapeDtypeStruct((1, SSM_D_INNER), F32)],
        grid=(s // tm,), in_specs=[blk, blk, vec, blk], out_specs=[blk, blk, vec],
        compiler_params=_params("arbitrary"),
    )(y, zx, w.reshape(1, SSM_D_INNER), dout)
    return dy, dz, dw.reshape(SSM_D_INNER)


def _ssm_core_fwd(zx, conv_w, conv_b, dtb, alog, dsk, norm_w, name):
    act, pre = _ssm_conv_fwd(zx, conv_w, conv_b, name + "_conv_fwd")
    y, st_in = _ssd_scan_fwd(act, zx, dtb, alog, dsk, name + "_scan_fwd")
    out = _gated_norm_fwd(y, zx, norm_w, name + "_gate_fwd")
    return out, (zx, conv_w, pre, dtb, alog, dsk, norm_w, act, y, st_in)


@functools.partial(jax.custom_vjp, nondiff_argnums=(7,))
def ssm_core(zx, conv_w, conv_b, dtb, alog, dsk, norm_w, name):
    return _ssm_core_fwd(zx, conv_w, conv_b, dtb, alog, dsk, norm_w, name)[0]


def _ssm_core_vjp_fwd(zx, conv_w, conv_b, dtb, alog, dsk, norm_w, name):
    return _ssm_core_fwd(zx, conv_w, conv_b, dtb, alog, dsk, norm_w, name)


def _ssm_core_vjp_bwd(name, res, dout):
    zx, conv_w, pre, dtb, alog, dsk, norm_w, act, y, st_in = res
    dy, dzx, dnorm_w = _gated_norm_bwd(y, zx, norm_w, dout, name + "_gate_bwd")
    dact, dzx, ddtb, dalog, ddsk = _ssd_scan_bwd(act, zx, dtb, alog, dsk, st_in, dy, dzx, name + "_scan_bwd")
    dzx, dconv_w, dconv_b = _ssm_conv_bwd(zx, pre, conv_w, dact, dzx, name + "_conv_bwd")
    return dzx, dconv_w, dconv_b, ddtb, dalog, ddsk, dnorm_w


ssm_core.defvjp(_ssm_core_vjp_fwd, _ssm_core_vjp_bwd)


def ssd_mixer_p(x, h, lin, j, carries, conv_w, conv_b, dtb, alog, dsk, norm_w, tag):
    zx = lin(h, ('ssm_w_in', j), F32, tag + "_in", carry=carries[0])
    return lin(ssm_core(zx, conv_w, conv_b, dtb, alog, dsk, norm_w, tag), ('ssm_w_out', j), F32, tag + "_out",
               carry=carries[1], res=x)


def _full_weight(n, parts):
    full = _join8(parts, SHARD_AXIS[n] - 1)
    if n == 'ssm_w_in':
        full = jnp.pad(full, ((0, 0), (0, SSM_IN_PAD - SSM_IN_WIDTH)))
    return full


def trunk(w, x):
    ready = {}

    def lin(a, key, out_dtype, name, cols=None, carry=None, res=None, rope=None):
        wt = ready[key] if cols is None else ready[key][:, cols]
        if carry is None:
            assert rope is None
            return linear(a, wt, res, out_dtype, name)
        y, parts = linear_x(a, wt, res, rope, w[carry[0]][carry[1]], out_dtype, name)
        ready[carry] = _full_weight(carry[0], parts)
        return y

    for i in range(DEPTH):
        x, h = norm(x, w['mix_norm_w'][i], f"mixnorm{i}")
        j = i // 2
        ffn_next = [('ffn_w_up', i), ('ffn_w_down', i)]
        if i == 0:
            x = first_attention_mixer(x, h, lin, w, ready, [('attn_w_o', 0)] + ffn_next, "attn0")
        elif i % 2 == 0:
            x = attention_mixer_p(x, h, lin, j, [('attn_w_o', j)] + ffn_next, f"attn{j}")
        else:
            x = ssd_mixer_p(x, h, lin, j, ffn_next, w['ssm_conv_w'][j], w['ssm_conv_b'][j], w['ssm_dt_bias'][j],
                            w['ssm_a_log'][j], w['ssm_d'][j], w['ssm_norm_w'][j], f"ssm{j}")
        if i + 1 == DEPTH:
            mixer_next = [None, None]
        elif i % 2 == 0:
            mixer_next = [('ssm_w_in', j), ('ssm_w_out', j)]
        else:
            mixer_next = [('attn_w_qkv', j + 1), None]
        x, h = norm(x, w['ffn_norm_w'][i], f"ffnnorm{i}")
        u0 = lin(h, ('ffn_w_up', i), F32, f"ffn{i}_up", carry=mixer_next[0])
        a = ffn_mid(u0, w['ffn_conv_w'][i], w['ffn_conv_b'][i], f"ffn{i}_mid")
        x = lin(a, ('ffn_w_down', i), F32, f"ffn{i}_down", carry=mixer_next[1], res=x)
    return x


def local_step(w, x, target):
    final_w = w['final_norm_w']
    trunk_w = {n: a for n, a in w.items() if n != 'final_norm_w'}
    xf, pullback = jax.vjp(trunk, trunk_w, x)
    loss, dxf, dfinal = loss_head(xf, final_w, target, "loss_head")
    gw, gx = pullback(dxf)
    gw['final_norm_w'] = dfinal
    return loss, gw, gx


def _adam_math(w, g, m, v):
    m = ADAM_B1 * m + (1.0 - ADAM_B1) * g
    v = ADAM_B2 * v + (1.0 - ADAM_B2) * (g * g)
    m_hat = m / (1.0 - ADAM_B1 ** ADAM_STEP)
    v_hat = v / (1.0 - ADAM_B2 ** ADAM_STEP)
    delta = -ADAM_LR * (m_hat / (jnp.sqrt(v_hat) + ADAM_EPS) + ADAM_WD * w)
    return delta, m, v


def _adamw_rows(g, w, m, v, name):
    r, c = w.shape
    tr = _pick(r, (256, 128, 64, 32, 16, 8))

    def body(g_ref, w_ref, m_ref, v_ref, d_out, m_out, v_out):
        delta, mm, vv = _adam_math(w_ref[...], g_ref[...], m_ref[...], v_ref[...])
        d_out[...] = delta
        m_out[...] = mm
        v_out[...] = vv

    blk = pl.BlockSpec((tr, c), lambda i: (i, 0))
    return pl.pallas_call(
        body, name=name, out_shape=[jax.ShapeDtypeStruct((r, c), F32)] * 3, grid=(r // tr,),
        in_specs=[blk] * 4, out_specs=[blk] * 3, compiler_params=_params("parallel"),
    )(g, w, m, v)


def _sum8(pieces, name):
    _, r, c = pieces.shape

    def body(p_ref, o_ref):
        g = p_ref[0]
        for j in range(1, N_DEV):
            g = g + p_ref[j]
        o_ref[...] = g

    return pl.pallas_call(
        body, name=name, out_shape=jax.ShapeDtypeStruct((r, c), F32),
        in_specs=[pl.BlockSpec(memory_space=pltpu.VMEM)], out_specs=pl.BlockSpec(memory_space=pltpu.VMEM),
    )(pieces)


def _adamw_plain(g, w, m, v, name):
    def body(g_ref, w_ref, m_ref, v_ref, d_out, m_out, v_out):
        delta, mm, vv = _adam_math(w_ref[...], g_ref[...], m_ref[...], v_ref[...])
        d_out[...] = delta
        m_out[...] = mm
        v_out[...] = vv

    vm = pl.BlockSpec(memory_space=pltpu.VMEM)
    return pl.pallas_call(
        body, name=name, out_shape=[jax.ShapeDtypeStruct(g.shape, F32)] * 3,
        in_specs=[vm] * 4, out_specs=[vm] * 3,
    )(g, w, m, v)


def _join8(parts, axis):
    t = jnp.moveaxis(parts, 0, axis)
    shp = t.shape
    return t.reshape(shp[:axis] + (shp[axis] * shp[axis + 1],) + shp[axis + 2:])


def _pack(arrs, lead, mult):
    flat = jnp.concatenate([a.reshape(a.shape[:lead] + (-1,)) for a in arrs], axis=-1)
    return _pad_rows(flat, mult)


def _unpack(buf, shapes, lead):
    flat = buf.reshape(buf.shape[:lead] + (-1,))
    out, off = [], 0
    for shp in shapes:
        n = math.prod(shp)
        out.append(flat[..., off:off + n].reshape(flat.shape[:lead] + tuple(shp)))
        off += n
    return out


def _own_shard(full, axis):
    size = full.shape[axis] // N_DEV
    return lax.dynamic_slice_in_dim(full, _my_index() * size, size, axis)


def kernel(x, mix_norm_w, attn_w_qkv, attn_w_o, ssm_w_in, ssm_conv_w, ssm_conv_b, ssm_dt_bias, ssm_a_log, ssm_d, ssm_norm_w, ssm_w_out, ffn_norm_w, ffn_w_up, ffn_conv_w, ffn_conv_b, ffn_w_down, final_norm_w, loss_target, m_mix_norm_w, m_attn_w_qkv, m_attn_w_o, m_ssm_w_in, m_ssm_conv_w, m_ssm_conv_b, m_ssm_dt_bias, m_ssm_a_log, m_ssm_d, m_ssm_norm_w, m_ssm_w_out, m_ffn_norm_w, m_ffn_w_up, m_ffn_conv_w, m_ffn_conv_b, m_ffn_w_down, m_final_norm_w, v_mix_norm_w, v_attn_w_qkv, v_attn_w_o, v_ssm_w_in, v_ssm_conv_w, v_ssm_conv_b, v_ssm_dt_bias, v_ssm_a_log, v_ssm_d, v_ssm_norm_w, v_ssm_w_out, v_ffn_norm_w, v_ffn_w_up, v_ffn_conv_w, v_ffn_conv_b, v_ffn_w_down, v_final_norm_w):
    args = dict(locals())
    w_sh = {n: args[n] for n in WEIGHT_NAMES}
    m_sh = {n: args["m_" + n] for n in WEIGHT_NAMES}
    v_sh = {n: args["v_" + n] for n in WEIGHT_NAMES}

    small_shapes = [w_sh[n].shape for n in SMALL_SHARDED]
    small = _exchange(_pack([w_sh[n] for n in SMALL_SHARDED], 0, 8), True, "gather_small")
    full = {n: w_sh[n] for n in SMALL if SHARD_AXIS[n] is None}
    for n, parts in zip(SMALL_SHARDED, _unpack(small, small_shapes, 1)):
        full[n] = _join8(parts, SHARD_AXIS[n])
    for n in BIG:
        full[n] = [w_sh[n][j] for j in range(w_sh[n].shape[0])]

    loss, gw, gx = local_step(full, x[0], loss_target[0])
    loss = lax.psum(loss, ("x", "y", "c"))
    for n in BIG:
        gw[n] = jnp.stack(gw[n])

    grads, deltas, new_m, new_v = {}, {}, {}, {}
    for n in BIG:
        shp = w_sh[n].shape
        two_d = (shp[0] * shp[1], shp[2])
        outs = _adamw_rows(*[t.reshape(two_d) for t in (gw[n], w_sh[n], m_sh[n], v_sh[n])], "adamw_" + n)
        grads[n] = gw[n]
        deltas[n], new_m[n], new_v[n] = [o.reshape(shp) for o in outs]

    small_full_shapes = [gw[n].shape for n in SMALL]
    gsmall = _exchange(_pack([gw[n] for n in SMALL], 0, 8), True, "gather_small_grads")
    gsmall = _unpack(_sum8(gsmall, "sum_small_grads"), small_full_shapes, 0)
    for n, g in zip(SMALL, gsmall):
        grads[n] = g if SHARD_AXIS[n] is None else _own_shard(g, SHARD_AXIS[n])
    shapes = [w_sh[n].shape for n in SMALL]
    outs = _adamw_plain(*[_pack([d[n] for n in SMALL], 0, 8) for d in (grads, w_sh, m_sh, v_sh)], "adamw_small")
    for d, buf in zip((deltas, new_m, new_v), outs):
        for n, a in zip(SMALL, _unpack(buf, shapes, 0)):
            d[n] = a

    return (loss, gx[None], *[grads[n] for n in WEIGHT_NAMES], *[deltas[n] for n in WEIGHT_NAMES],
            *[new_m[n] for n in WEIGHT_NAMES], *[new_v[n] for n in WEIGHT_NAMES])
```

```python
import functools
import math

import jax
import jax.numpy as jnp
from jax import lax
from jax.experimental import pallas as pl
from jax.experimental.pallas import tpu as pltpu

F32 = jnp.float32
BF16 = jnp.bfloat16
N_DEV = 8
MESH_ID = pl.DeviceIdType.MESH

DEPTH = 4
ATTN_HEADS = 8
ATTN_HEAD_DIM = 128
ATTN_DILATIONS = (1, 4, 16)
ATTN_BLOCK = 128
ROPE_THETA = 500000.0
ROPE_DIM = 32
ATTN_OUT_WIDTH = 1024
SSM_D_INNER = 2048
SSM_HEAD_DIM = 64
SSM_HEADS = 32
SSM_STATE = 128
SSM_GROUPS = 8
SSM_CHUNK = 128
SSM_CONV_DIM = 4096
SSM_IN_WIDTH = 6176
SSM_IN_PAD = 6400
NORM_EPS = 1e-5
ADAM_LR = 0.001
ADAM_B1 = 0.9
ADAM_B2 = 0.999
ADAM_EPS = 1e-08
ADAM_WD = 0.01
ADAM_STEP = 10

WEIGHT_NAMES = ['mix_norm_w', 'attn_w_qkv', 'attn_w_o', 'ssm_w_in', 'ssm_conv_w', 'ssm_conv_b', 'ssm_dt_bias',
                'ssm_a_log', 'ssm_d', 'ssm_norm_w', 'ssm_w_out', 'ffn_norm_w', 'ffn_w_up', 'ffn_conv_w',
                'ffn_conv_b', 'ffn_w_down', 'final_norm_w']
SHARD_AXIS = {'mix_norm_w': None, 'attn_w_qkv': 2, 'attn_w_o': 1, 'ssm_w_in': 2, 'ssm_conv_w': 2, 'ssm_conv_b': 1,
              'ssm_dt_bias': None, 'ssm_a_log': None, 'ssm_d': None, 'ssm_norm_w': 1, 'ssm_w_out': 1,
              'ffn_norm_w': None, 'ffn_w_up': 2, 'ffn_conv_w': 2, 'ffn_conv_b': None, 'ffn_w_down': 1,
              'final_norm_w': None}
BIG = ['attn_w_qkv', 'attn_w_o', 'ssm_w_in', 'ssm_w_out', 'ffn_w_up', 'ffn_w_down']
SMALL = [n for n in WEIGHT_NAMES if n not in BIG]
SMALL_SHARDED = [n for n in SMALL if SHARD_AXIS[n] is not None]
LANES = 1024


def _my_index():
    return 4 * lax.axis_index("x") + 2 * lax.axis_index("y") + lax.axis_index("c")


def _peer(k):
    x, y, c = lax.axis_index("x"), lax.axis_index("y"), lax.axis_index("c")
    return (x ^ ((k >> 2) & 1), y ^ ((k >> 1) & 1), c ^ (k & 1))


def _exchange(src, gather, name):
    def body(src_ref, out_ref, send_sems, recv_sems, local_sem):
        start, wait = _exchange_copies(src_ref, out_ref, send_sems, recv_sems, local_sem, gather)
        start()
        wait()

    return pl.pallas_call(
        body, name=name,
        out_shape=_exchange_out(src, gather),
        in_specs=[pl.BlockSpec(memory_space=pl.ANY)],
        out_specs=pl.BlockSpec(memory_space=pl.ANY),
        scratch_shapes=list(EXCHANGE_SEMS),
    )(src)


EXCHANGE_SEMS = (pltpu.SemaphoreType.DMA((N_DEV - 1,)), pltpu.SemaphoreType.DMA((N_DEV - 1,)),
                 pltpu.SemaphoreType.DMA)


def _exchange_out(src, gather):
    return jax.ShapeDtypeStruct((N_DEV,) + (src.shape if gather else src.shape[1:]), src.dtype)


def _exchange_copies(src_ref, out_ref, send_sems, recv_sems, local_sem, gather):
    me = _my_index()

    def piece(j):
        return src_ref if gather else src_ref.at[j]

    def remote(k, slab):
        return pltpu.make_async_remote_copy(
            src_ref=piece(me ^ k), dst_ref=out_ref.at[slab], send_sem=send_sems.at[k - 1],
            recv_sem=recv_sems.at[k - 1], device_id=_peer(k), device_id_type=MESH_ID)

    mine = pltpu.make_async_copy(piece(me), out_ref.at[me], local_sem)
    arrivals = {k: remote(k, me ^ k) for k in range(1, N_DEV)}
    if not gather:
        sends = [remote(k, me) for k in range(1, N_DEV)]

        def start():
            mine.start()
            for cp in sends:
                cp.start()

        def wait():
            for cp in arrivals.values():
                cp.wait_recv()
            for cp in sends:
                cp.wait_send()
            mine.wait()

        return start, wait

    far = (2, 4, 6)
    sends = [remote(k, me) for k in (1,) + far]
    passed_on = [pltpu.make_async_remote_copy(
        src_ref=out_ref.at[me ^ k], dst_ref=out_ref.at[me ^ k], send_sem=send_sems.at[k],
        recv_sem=recv_sems.at[k], device_id=_peer(1), device_id_type=MESH_ID) for k in far]

    def start():
        mine.start()
        for cp in sends:
            cp.start()

    def wait():
        for k, cp in zip(far, passed_on):
            arrivals[k].wait_recv()
            cp.start()
        for k in (1, 3, 5, 7):
            arrivals[k].wait_recv()
        for cp in sends + passed_on:
            cp.wait_send()
        mine.wait()

    return start, wait


def _pad_rows(flat, mult):
    n = flat.shape[-1]
    rows = -(-n // (LANES * mult)) * mult
    pad = rows * LANES - n
    flat = jnp.pad(flat, [(0, 0)] * (flat.ndim - 1) + [(0, pad)])
    return flat.reshape(flat.shape[:-1] + (rows, LANES))


MATMUL_VMEM_BUDGET = 48 * 1024 * 1024
MATMUL_TM = (1024, 1408, 512, 256, 128)
MATMUL_TN = (3200, 3072, 2816, 2048, 1536, 1408, 1280, 1024, 896, 512, 384, 256, 128)
MATMUL_TK = (3200, 3072, 2816, 2048, 1408, 1280, 1024, 896)
MATMUL_MIN_TK = 896


def _pick(n, cands):
    for c in cands:
        if n % c == 0:
            return c
    return n


def _matmul_tiles(m, n, k, a_bytes, b_bytes, out_bytes, has_res):
    tm = _pick(m, MATMUL_TM)

    def need(tn, tk):
        blocks = 2 * (tm * tk * a_bytes + tk * tn * b_bytes + tm * tn * out_bytes + has_res * tm * tn * 4)
        temps = tm * tn * 4 * (1 + (tk < k)) + (a_bytes > 2) * tm * tk * 2 + (b_bytes > 2) * tk * tn * 2
        return blocks + temps

    for tn in [c for c in MATMUL_TN if n % c == 0] + [n]:
        for tk in [k] + [c for c in MATMUL_TK if k % c == 0 and MATMUL_MIN_TK <= c < k]:
            if need(tn, tk) <= MATMUL_VMEM_BUDGET:
                return tm, tn, tk
    raise ValueError(f"no matmul tiling fits VMEM for {(m, n, k)}")


def _matmul(a, b, ta, tb, out_dtype, name, rider=None, res=None, rope=None):
    (m, k) = (a.shape[1], a.shape[0]) if ta else a.shape
    (k2, n) = (b.shape[1], b.shape[0]) if tb else b.shape
    assert k == k2, (a.shape, b.shape, ta, tb)
    tm, tn, tk = _matmul_tiles(m, n, k, a.dtype.itemsize, b.dtype.itemsize, jnp.dtype(out_dtype).itemsize,
                               res is not None)
    nk = k // tk
    dims = (((0 if ta else 1,), (1 if tb else 0,)), ((), ()))

    grid = (n // tn, m // tm, nk)

    n_in = 2 + (res is not None) + (rope is not None) + (rider is not None)
    if rope is not None:
        assert res is None and tn == n == QKV_GROUP, (tn, n)

    def body(*refs):
        ins, rest = refs[:n_in], refs[n_in:]
        a_ref, b_ref = ins[:2]
        res_ref = ins[2] if res is not None else None
        rope_ref = ins[2] if rope is not None else None
        o_ref, scratch = rest[0], rest[1:]
        if rider is not None:
            start, wait = _exchange_copies(ins[-1], rest[1], *scratch[-3:], rider[1])
            scratch = scratch[1:-3]
            at = [pl.program_id(ax) for ax in range(3)]
            pl.when(functools.reduce(jnp.logical_and, [p == 0 for p in at]))(start)
        part = lax.dot_general(a_ref[...].astype(BF16), b_ref[...].astype(BF16), dims,
                               preferred_element_type=F32)

        def finish(total):
            if rope_ref is not None:
                for c0 in range(0, tn, ATTN_HEAD_DIM):
                    head = total[:, c0:c0 + ATTN_HEAD_DIM]
                    if c0 < 2 * ATTN_OUT_WIDTH:
                        head = _rope(head, rope_ref[...], 1.0)
                    o_ref[:, c0:c0 + ATTN_HEAD_DIM] = head.astype(o_ref.dtype)
                return
            if res_ref is not None:
                total = total + res_ref[...]
            o_ref[...] = total.astype(o_ref.dtype)

        if nk == 1:
            finish(part)
        else:
            acc_ref, = scratch
            kk = pl.program_id(2)

            @pl.when(kk == 0)
            def _():
                acc_ref[...] = part

            @pl.when(kk > 0)
            def _():
                acc_ref[...] += part

            @pl.when(kk == nk - 1)
            def _():
                finish(acc_ref[...])
        if rider is not None:
            pl.when(functools.reduce(jnp.logical_and, [p == g - 1 for p, g in zip(at, grid)]))(wait)

    a_spec = (pl.BlockSpec((tk, tm), lambda j, i, kk: (kk, i)) if ta
              else pl.BlockSpec((tm, tk), lambda j, i, kk: (i, kk)))
    b_spec = (pl.BlockSpec((tn, tk), lambda j, i, kk: (j, kk)) if tb
              else pl.BlockSpec((tk, tn), lambda j, i, kk: (kk, j)))
    any_spec = pl.BlockSpec(memory_space=pl.ANY)
    tile_spec = pl.BlockSpec((tm, tn), lambda j, i, kk: (i, j))
    in_specs, operands = [a_spec, b_spec], [a, b]
    out_shape, out_specs = [jax.ShapeDtypeStruct((m, n), out_dtype)], [tile_spec]
    scratch = [] if nk == 1 else [pltpu.VMEM((tm, tn), F32)]
    sem = ("parallel", "parallel", "arbitrary")
    if res is not None:
        in_specs.append(tile_spec)
        operands.append(res)
    if rope is not None:
        in_specs.append(pl.BlockSpec((tm, rope.shape[1]), lambda j, i, kk: (i, 0)))
        operands.append(rope)
    if rider is not None:
        in_specs.append(any_spec)
        operands.append(rider[0])
        out_shape.append(_exchange_out(*rider))
        out_specs.append(any_spec)
        scratch += list(EXCHANGE_SEMS)
        sem = ("arbitrary",) * 3
    outs = pl.pallas_call(
        body, name=name, out_shape=out_shape, grid=grid, in_specs=in_specs, out_specs=out_specs,
        scratch_shapes=scratch, compiler_params=_params(*sem),
    )(*operands)
    return outs[0] if rider is None else tuple(outs)


@functools.partial(jax.custom_vjp, nondiff_argnums=(3, 4))
def linear(a, w, res, out_dtype, name):
    return _matmul(a, w, False, False, out_dtype, name + "_fwd", res=res)


def _linear_fwd(a, w, res, out_dtype, name):
    return _matmul(a, w, False, False, out_dtype, name + "_fwd", res=res), (a, w, res is not None)


def _linear_bwd(out_dtype, name, saved, dy):
    a, w, has_res = saved
    da = _matmul(dy, w, False, True, a.dtype, name + "_da")
    dw = _matmul(a, dy, True, False, w.dtype, name + "_dw")
    return da, dw, (dy if has_res else None)


linear.defvjp(_linear_fwd, _linear_bwd)


def _sum_pieces(pieces, name):
    shape = pieces.shape[1:]
    c = shape[-1]
    r = math.prod(shape[:-1])
    tr = _pick(r, (512, 256, 128, 64, 32, 16, 8))

    def body(p_ref, o_ref):
        g = p_ref[0].astype(F32)
        for j in range(1, N_DEV):
            g = g + p_ref[j].astype(F32)
        o_ref[...] = g

    return pl.pallas_call(
        body, name=name, out_shape=jax.ShapeDtypeStruct((r, c), F32), grid=(r // tr,),
        in_specs=[pl.BlockSpec((N_DEV, tr, c), lambda i: (0, i, 0))],
        out_specs=pl.BlockSpec((tr, c), lambda i: (i, 0)),
        compiler_params=_params("parallel"),
    )(pieces.reshape(N_DEV, r, c)).reshape(shape)


@functools.partial(jax.custom_vjp, nondiff_argnums=(5, 6))
def linear_x(a, w, res, rope, shard, out_dtype, name):
    return _matmul(a, w, False, False, out_dtype, name + "_fwd", rider=(shard.astype(BF16), True), res=res,
                   rope=rope)


def _linear_x_fwd(a, w, res, rope, shard, out_dtype, name):
    out = _matmul(a, w, False, False, out_dtype, name + "_fwd", rider=(shard.astype(BF16), True), res=res,
                  rope=rope)
    return out, (a, w, res is not None, rope)


def _linear_x_bwd(out_dtype, name, saved, cts):
    a, w, has_res, rope = saved
    dy, dparts = cts
    half = dparts.shape[1] // 2
    da, moved_a = _matmul(dy, w, False, True, a.dtype, name + "_da", rider=(dparts[:, :half], False))
    dw, moved_b = _matmul(a, dy, True, False, w.dtype, name + "_dw", rider=(dparts[:, half:], False))
    dshard = jnp.concatenate([_sum_pieces(moved_a, name + "_sum_a"), _sum_pieces(moved_b, name + "_sum_b")])
    return da, dw, (dy if has_res else None), (None if rope is None else jnp.zeros_like(rope)), dshard


linear_x.defvjp(_linear_x_fwd, _linear_x_bwd)

FIRST_ROW_SPLIT = (0, 384, 704, 1024)


def _qkv_first_fwd(hs, shard, ropes, carried, name):
    parts = _exchange(shard.astype(BF16), True, name + "_gather")
    wfull = _join8(parts, 1)
    ys, moved = [], []
    for g, (h, rope, carry) in enumerate(zip(hs, ropes, carried)):
        y, m = _matmul(h, wfull[:, g * QKV_GROUP:(g + 1) * QKV_GROUP], False, False, BF16, f"{name}{g}_fwd",
                       rider=(carry.astype(BF16), True), rope=rope)
        ys.append(y)
        moved.append(m)
    return (tuple(ys), tuple(moved)), (hs, wfull, ropes)


@functools.partial(jax.custom_vjp, nondiff_argnums=(4,))
def qkv_first(hs, shard, ropes, carried, name):
    return _qkv_first_fwd(hs, shard, ropes, carried, name)[0]


def _qkv_first_vjp_fwd(hs, shard, ropes, carried, name):
    return _qkv_first_fwd(hs, shard, ropes, carried, name)


def _qkv_first_vjp_bwd(name, saved, cts):
    hs, wfull, ropes = saved
    dys, dmoved = cts
    dws, dcarried = [], []
    for g in range(len(hs)):
        dw, back = _matmul(hs[g], dys[g], True, False, BF16, f"{name}{g}_dw", rider=(dmoved[g], False))
        dws.append(dw)
        dcarried.append(_sum_pieces(back, f"{name}{g}_carried_sum"))
    w_shape = wfull.shape
    dparts = jnp.moveaxis(jnp.concatenate(dws, axis=1).reshape(w_shape[0], N_DEV, w_shape[1] // N_DEV), 1, 0)
    das, dshard = [], []
    for g in range(len(hs)):
        rows = slice(FIRST_ROW_SPLIT[g], FIRST_ROW_SPLIT[g + 1])
        da, got = _matmul(dys[g], wfull[:, g * QKV_GROUP:(g + 1) * QKV_GROUP], False, True, BF16, f"{name}{g}_da",
                          rider=(dparts[:, rows], False))
        das.append(da)
        dshard.append(_sum_pieces(got, f"{name}{g}_own_sum"))
    return tuple(das), jnp.concatenate(dshard), tuple(jnp.zeros_like(r) for r in ropes), tuple(dcarried)


qkv_first.defvjp(_qkv_first_vjp_fwd, _qkv_first_vjp_bwd)


VMEM_LIMIT = 56 * 1024 * 1024
SUBLANES = 8


def _params(*sem):
    return pltpu.CompilerParams(dimension_semantics=sem, vmem_limit_bytes=VMEM_LIMIT)


def _sigmoid(x):
    return 0.5 * jnp.tanh(0.5 * x) + 0.5


def _rstd(xv):
    return lax.rsqrt(jnp.mean(xv * xv, axis=-1, keepdims=True) + NORM_EPS)


def _accumulate(ref, part, first):
    @pl.when(first)
    def _():
        ref[...] = part

    @pl.when(jnp.logical_not(first))
    def _():
        ref[...] += part


def _norm_fwd(x, w, name):
    s, d = x.shape
    tm = _pick(s, (1024, 512, 256, 128))

    def body(x_ref, w_ref, h_ref):
        xv = x_ref[...]
        h_ref[...] = (xv * _rstd(xv) * w_ref[...]).astype(h_ref.dtype)

    return pl.pallas_call(
        body, name=name, out_shape=jax.ShapeDtypeStruct((s, d), BF16), grid=(s // tm,),
        in_specs=[pl.BlockSpec((tm, d), lambda i: (i, 0)), pl.BlockSpec((1, d), lambda i: (0, 0))],
        out_specs=pl.BlockSpec((tm, d), lambda i: (i, 0)), compiler_params=_params("parallel"),
    )(x, w.reshape(1, d))


def _norm_bwd(x, w, dh, dskip, name):
    s, d = x.shape
    tm = _pick(s, (1024, 512, 256, 128))

    def body(x_ref, w_ref, dh_ref, ds_ref, dx_ref, dw_ref):
        xv = x_ref[...]
        r = _rstd(xv)
        y = xv * r
        dhv = dh_ref[...].astype(F32)
        dy = dhv * w_ref[...]
        dx_ref[...] = ds_ref[...] + r * (dy - y * jnp.mean(dy * y, axis=-1, keepdims=True))
        _accumulate(dw_ref, jnp.sum(dhv * y, axis=0, keepdims=True), pl.program_id(0) == 0)

    row = pl.BlockSpec((tm, d), lambda i: (i, 0))
    vec = pl.BlockSpec((1, d), lambda i: (0, 0))
    dx, dw = pl.pallas_call(
        body, name=name,
        out_shape=[jax.ShapeDtypeStruct((s, d), F32), jax.ShapeDtypeStruct((1, d), F32)], grid=(s // tm,),
        in_specs=[row, vec, row, row], out_specs=[row, vec], compiler_params=_params("arbitrary"),
    )(x, w.reshape(1, d), dh, dskip)
    return dx, dw.reshape(d)


@functools.partial(jax.custom_vjp, nondiff_argnums=(2,))
def norm(x, w, name):
    return x, _norm_fwd(x, w, name + "_fwd")


def _norm_vjp_fwd(x, w, name):
    return (x, _norm_fwd(x, w, name + "_fwd")), (x, w)


def _norm_vjp_bwd(name, saved, cts):
    x, w = saved
    dskip, dh = cts
    return _norm_bwd(x, w, dh, dskip, name + "_bwd")


norm.defvjp(_norm_vjp_fwd, _norm_vjp_bwd)


def loss_head(x, w, target, name):
    s, d = x.shape
    tm = _pick(s, (1024, 512, 256, 128))

    def body(x_ref, w_ref, t_ref, loss_ref, dx_ref, dw_ref):
        first = pl.program_id(0) == 0
        xv = x_ref[...]
        r = _rstd(xv)
        y = xv * r
        err = y * w_ref[...] - t_ref[...]
        part = 0.5 * jnp.sum(jnp.sum(err * err, axis=-1, keepdims=True), axis=0, keepdims=True) / d
        _accumulate(loss_ref, jnp.broadcast_to(part, loss_ref.shape), first)
        dout = err / d
        dy = dout * w_ref[...]
        dx_ref[...] = r * (dy - y * jnp.mean(dy * y, axis=-1, keepdims=True))
        _accumulate(dw_ref, jnp.sum(dout * y, axis=0, keepdims=True), first)

    row = pl.BlockSpec((tm, d), lambda i: (i, 0))
    vec = pl.BlockSpec((1, d), lambda i: (0, 0))
    loss, dx, dw = pl.pallas_call(
        body, name=name,
        out_shape=[jax.ShapeDtypeStruct((1, 128), F32), jax.ShapeDtypeStruct((s, d), F32),
                   jax.ShapeDtypeStruct((1, d), F32)],
        grid=(s // tm,), in_specs=[row, vec, row],
        out_specs=[pl.BlockSpec((1, 128), lambda i: (0, 0)), row, vec],
        compiler_params=_params("arbitrary"),
    )(x, w.reshape(1, d), target)
    return loss[0, 0], dx, dw.reshape(d)


FFN_ROWS = 256
FFN_COLS = 256


def _shift_down(cur, halo, k):
    out = pltpu.roll(cur, k, axis=0)
    row = lax.broadcasted_iota(jnp.int32, halo.shape, 0)
    top = out[0:SUBLANES]
    for j in range(k):
        top = jnp.where(row == j, halo[SUBLANES - k + j:SUBLANES - k + j + 1, :], top)
    return jnp.concatenate([top, out[SUBLANES:]], axis=0)


def _shift_up(cur, nxt, k):
    n = cur.shape[0]
    out = pltpu.roll(cur, n - k, axis=0)
    row = lax.broadcasted_iota(jnp.int32, nxt.shape, 0)
    bottom = out[n - SUBLANES:]
    for j in range(k):
        bottom = jnp.where(row == SUBLANES - k + j, nxt[j:j + 1, :], bottom)
    return jnp.concatenate([out[:n - SUBLANES], bottom], axis=0)


def _conv_taps(cur, halo, ntaps):
    return [_shift_down(cur, halo, ntaps - 1 - k) if k < ntaps - 1 else cur for k in range(ntaps)]


def _pad_taps(conv_w):
    return jnp.pad(conv_w, ((0, SUBLANES - conv_w.shape[0]), (0, 0)))


def _ffn_mid_fwd(u0, conv_w, conv_b, name):
    s, width = u0.shape
    half = width // 2
    tm = _pick(s, (FFN_ROWS, 128))
    per = tm // SUBLANES

    def body(w_ref, b_ref, u_ref, halo_ref, a_ref, pre_ref):
        keep = pl.program_id(0) > 0
        for c0 in range(0, half, FFN_COLS):
            vals = []
            for base in (c0, half + c0):
                cols = slice(base, base + FFN_COLS)
                halo = jnp.where(keep, halo_ref[:, cols], 0.0)
                taps = _conv_taps(u_ref[:, cols], halo, 3)
                vals.append(sum(w_ref[k:k + 1, cols] * taps[k] for k in range(3)) + b_ref[:, cols])
                pre_ref[:, cols] = vals[-1].astype(pre_ref.dtype)
            gate, up = vals
            a_ref[:, c0:c0 + FFN_COLS] = (gate * _sigmoid(gate) * up).astype(a_ref.dtype)

    return pl.pallas_call(
        body, name=name,
        out_shape=[jax.ShapeDtypeStruct((s, half), BF16), jax.ShapeDtypeStruct((s, width), BF16)], grid=(s // tm,),
        in_specs=[pl.BlockSpec((SUBLANES, width), lambda i: (0, 0)), pl.BlockSpec((1, width), lambda i: (0, 0)),
                  pl.BlockSpec((tm, width), lambda i: (i, 0)),
                  pl.BlockSpec((SUBLANES, width), lambda i: (jnp.maximum(i * per - 1, 0), 0))],
        out_specs=[pl.BlockSpec((tm, half), lambda i: (i, 0)), pl.BlockSpec((tm, width), lambda i: (i, 0))],
        compiler_params=_params("parallel"),
    )(_pad_taps(conv_w), conv_b.reshape(1, width), u0, u0)


def _ffn_mid_bwd(u0, pre, conv_w, da, name):
    s, width = u0.shape
    half = width // 2
    tm = _pick(s, (FFN_ROWS, 128))
    nt = s // tm

    def body(w_ref, u_ref, pre_ref, da_ref, du0_ref, dw_ref, db_ref, carry_ref):
        first = pl.program_id(0) == 0
        for c0 in range(0, half, FFN_COLS):
            dav = da_ref[:, c0:c0 + FFN_COLS].astype(F32)
            gate = pre_ref[:, c0:c0 + FFN_COLS].astype(F32)
            up = pre_ref[:, half + c0:half + c0 + FFN_COLS].astype(F32)
            sig = _sigmoid(gate)
            dus = [dav * up * sig * (1.0 + gate * (1.0 - sig)), dav * gate * sig]
            for base, du in zip((c0, half + c0), dus):
                cols = slice(base, base + FFN_COLS)
                nxt = jnp.where(first, 0.0, carry_ref[:, cols])
                ahead = [_shift_up(du, nxt, 2), _shift_up(du, nxt, 1), du]
                du0 = sum(w_ref[k:k + 1, cols] * ahead[k] for k in range(3))
                du0_ref[:, cols] = du0.astype(du0_ref.dtype)
                carry_ref[:, cols] = du[0:SUBLANES, :]
                x = u_ref[:, cols]
                dwp = jnp.concatenate([jnp.sum(ahead[k] * x, axis=0, keepdims=True) for k in range(3)]
                                      + [jnp.zeros((SUBLANES - 3, FFN_COLS), F32)], axis=0)
                dbp = jnp.sum(du, axis=0, keepdims=True)

                @pl.when(first)
                def _():
                    dw_ref[:, cols] = dwp
                    db_ref[:, cols] = dbp

                @pl.when(jnp.logical_not(first))
                def _():
                    dw_ref[:, cols] += dwp
                    db_ref[:, cols] += dbp

    rev = lambda i: nt - 1 - i
    du0, dw, db = pl.pallas_call(
        body, name=name,
        out_shape=[jax.ShapeDtypeStruct((s, width), BF16), jax.ShapeDtypeStruct((SUBLANES, width), F32),
                   jax.ShapeDtypeStruct((1, width), F32)],
        grid=(nt,),
        in_specs=[pl.BlockSpec((SUBLANES, width), lambda i: (0, 0)),
                  pl.BlockSpec((tm, width), lambda i: (rev(i), 0)), pl.BlockSpec((tm, width), lambda i: (rev(i), 0)),
                  pl.BlockSpec((tm, half), lambda i: (rev(i), 0))],
        out_specs=[pl.BlockSpec((tm, width), lambda i: (rev(i), 0)),
                   pl.BlockSpec((SUBLANES, width), lambda i: (0, 0)), pl.BlockSpec((1, width), lambda i: (0, 0))],
        scratch_shapes=[pltpu.VMEM((SUBLANES, width), F32)],
        compiler_params=_params("arbitrary"),
    )(_pad_taps(conv_w), u0, pre, da)
    return du0, dw[:3], db.reshape(width)


@functools.partial(jax.custom_vjp, nondiff_argnums=(3,))
def ffn_mid(u0, conv_w, conv_b, name):
    return _ffn_mid_fwd(u0, conv_w, conv_b, name + "_fwd")[0]


def _ffn_mid_vjp_fwd(u0, conv_w, conv_b, name):
    a, pre = _ffn_mid_fwd(u0, conv_w, conv_b, name + "_fwd")
    return a, (u0, pre, conv_w)


def _ffn_mid_vjp_bwd(name, res, da):
    u0, pre, conv_w = res
    return _ffn_mid_bwd(u0, pre, conv_w, da, name + "_bwd")


ffn_mid.defvjp(_ffn_mid_vjp_fwd, _ffn_mid_vjp_bwd)


NEG = -1e30
HEAD_SLICES = [slice(hh * ATTN_HEAD_DIM, (hh + 1) * ATTN_HEAD_DIM) for hh in range(ATTN_HEADS)]
ATTN_SCALE = ATTN_HEAD_DIM ** -0.5
QKV_GROUP = 3 * ATTN_OUT_WIDTH


def rope_table(seq, d):
    pos = (jnp.arange(seq // d, dtype=jnp.int32)[None, :] * d + jnp.arange(d, dtype=jnp.int32)[:, None])
    inv_freq = ROPE_THETA ** (-jnp.arange(0, ROPE_DIM, 2, dtype=F32) / ROPE_DIM)
    ang = pos.reshape(seq).astype(F32)[:, None] * inv_freq[None, :]
    cos, sin = jnp.cos(ang), jnp.sin(ang)
    half = ROPE_DIM // 2
    ones = jnp.ones((seq, ATTN_HEAD_DIM - ROPE_DIM), F32)
    zero = lambda n: jnp.zeros((seq, n), F32)
    return jnp.concatenate([cos, cos, ones, -sin, zero(ATTN_HEAD_DIM - half),
                            zero(half), sin, zero(ATTN_HEAD_DIM - ROPE_DIM)], axis=1)


def _rope(t, tab, sign):
    half = ROPE_DIM // 2
    return t * tab[:, 0:128] + sign * (pltpu.roll(t, ATTN_HEAD_DIM - half, axis=1) * tab[:, 128:256]
                                       + pltpu.roll(t, half, axis=1) * tab[:, 256:384])


def _to_dilated(a, d):
    s = a.shape[0]
    return a if d == 1 else a.reshape(s // d, d, -1).transpose(1, 0, 2).reshape(s, -1)


def _from_dilated(a, d):
    s = a.shape[0]
    return a if d == 1 else a.reshape(d, s // d, -1).transpose(1, 0, 2).reshape(s, -1)


def _dot_nt(a, b):
    return lax.dot_general(a, b, (((1,), (1,)), ((), ())), preferred_element_type=F32)


def _dot_tn(a, b):
    return lax.dot_general(a, b, (((0,), (0,)), ((), ())), preferred_element_type=F32)


def _dot(a, b):
    return jnp.dot(a, b, preferred_element_type=F32)


def _window_mask(has_prev):
    ii = lax.broadcasted_iota(jnp.int32, (ATTN_BLOCK, 2 * ATTN_BLOCK), 0)
    jj = lax.broadcasted_iota(jnp.int32, (ATTN_BLOCK, 2 * ATTN_BLOCK), 1)
    in_window = jnp.logical_and(jj >= ii, jj <= ii + ATTN_BLOCK)
    return jnp.logical_and(in_window, jnp.logical_or(jj >= ATTN_BLOCK, has_prev))


def _both(prev_ref, cur_ref, hs):
    return jnp.concatenate([prev_ref[:, hs], cur_ref[:, hs]], axis=0)


def _attn_group_fwd(qkv, d, name):
    s = qkv.shape[0]
    nb = s // d // ATTN_BLOCK

    def body(q_ref, kc_ref, kp_ref, vc_ref, vp_ref, o_ref, lse_ref):
        mask = _window_mask(pl.program_id(1) > 0)
        lane = lax.broadcasted_iota(jnp.int32, (ATTN_BLOCK, 128), 1)
        lse_tile = jnp.zeros((ATTN_BLOCK, 128), F32)
        for hh, hs in enumerate(HEAD_SLICES):
            sc = jnp.where(mask, _dot_nt(q_ref[:, hs], _both(kp_ref, kc_ref, hs)) * ATTN_SCALE, NEG)
            m = jnp.max(sc, axis=1, keepdims=True)
            p = jnp.exp(sc - m)
            den = jnp.sum(p, axis=1, keepdims=True)
            o_ref[:, hs] = (_dot(p.astype(BF16), _both(vp_ref, vc_ref, hs)) / den).astype(o_ref.dtype)
            lse_tile = jnp.where(lane == hh, m + jnp.log(den), lse_tile)
        lse_ref[...] = lse_tile

    cur = lambda t: pl.BlockSpec((ATTN_BLOCK, ATTN_OUT_WIDTH), lambda r, n: (r * nb + n, t))
    prv = lambda t: pl.BlockSpec((ATTN_BLOCK, ATTN_OUT_WIDTH), lambda r, n: (r * nb + jnp.maximum(n - 1, 0), t))
    return pl.pallas_call(
        body, name=name,
        out_shape=[jax.ShapeDtypeStruct((s, ATTN_OUT_WIDTH), BF16), jax.ShapeDtypeStruct((s, 128), F32)],
        grid=(d, nb), in_specs=[cur(0), cur(1), prv(1), cur(2), prv(2)],
        out_specs=[pl.BlockSpec((ATTN_BLOCK, ATTN_OUT_WIDTH), lambda r, n: (r * nb + n, 0)),
                   pl.BlockSpec((ATTN_BLOCK, 128), lambda r, n: (r * nb + n, 0))],
        compiler_params=_params("parallel", "parallel"),
    )(qkv, qkv, qkv, qkv, qkv)


def _attn_combine(os_, lses, name):
    s = os_[0].shape[0]
    tm = _pick(s, (512, 256, 128))
    ng = len(os_)

    def body(*refs):
        o_refs, l_refs, (o_ref, lse_ref) = refs[:ng], refs[ng:2 * ng], refs[2 * ng:]
        lane = lax.broadcasted_iota(jnp.int32, (tm, 128), 1)
        lse_tile = jnp.zeros((tm, 128), F32)
        for hh, hs in enumerate(HEAD_SLICES):
            ls = [l_ref[:, hh:hh + 1] for l_ref in l_refs]
            m = functools.reduce(jnp.maximum, ls)
            ws = [jnp.exp(l - m) for l in ls]
            tot = functools.reduce(lambda a, b: a + b, ws)
            acc = functools.reduce(lambda a, b: a + b, [o_r[:, hs].astype(F32) * w for o_r, w in zip(o_refs, ws)])
            o_ref[:, hs] = (acc / tot).astype(o_ref.dtype)
            lse_tile = jnp.where(lane == hh, m + jnp.log(tot), lse_tile)
        lse_ref[...] = lse_tile

    wide = pl.BlockSpec((tm, ATTN_OUT_WIDTH), lambda i: (i, 0))
    thin = pl.BlockSpec((tm, 128), lambda i: (i, 0))
    return pl.pallas_call(
        body, name=name,
        out_shape=[jax.ShapeDtypeStruct((s, ATTN_OUT_WIDTH), BF16), jax.ShapeDtypeStruct((s, 128), F32)],
        grid=(s // tm,), in_specs=[wide] * ng + [thin] * ng, out_specs=[wide, thin],
        compiler_params=_params("parallel"),
    )(*os_, *lses)


def _attn_delta(do, o, name):
    s = do.shape[0]
    tm = _pick(s, (512, 256, 128))

    def body(do_ref, o_ref, out_ref):
        lane = lax.broadcasted_iota(jnp.int32, (tm, 128), 1)
        tile = jnp.zeros((tm, 128), F32)
        for hh, hs in enumerate(HEAD_SLICES):
            prod = do_ref[:, hs].astype(F32) * o_ref[:, hs].astype(F32)
            tile = jnp.where(lane == hh, jnp.sum(prod, axis=1, keepdims=True), tile)
        out_ref[...] = tile

    wide = pl.BlockSpec((tm, ATTN_OUT_WIDTH), lambda i: (i, 0))
    return pl.pallas_call(
        body, name=name, out_shape=jax.ShapeDtypeStruct((s, 128), F32), grid=(s // tm,),
        in_specs=[wide, wide], out_specs=pl.BlockSpec((tm, 128), lambda i: (i, 0)),
        compiler_params=_params("parallel"),
    )(do, o)


def _attn_group_bwd(qkv, do, lse, delta, tab, d, name):
    s = qkv.shape[0]
    nb = s // d // ATTN_BLOCK
    wide = ATTN_OUT_WIDTH

    def body(q_ref, qn_ref, k_ref, v_ref, do_ref, don_ref, lse_ref, lsen_ref, dl_ref, dln_ref, tab_ref,
             out_ref, carry_ref):
        n = pl.program_id(1)

        @pl.when(n == 0)
        def _():
            carry_ref[...] = jnp.zeros_like(carry_ref)

        rows = lax.broadcasted_iota(jnp.int32, (2 * ATTN_BLOCK, ATTN_BLOCK), 0)
        keys = lax.broadcasted_iota(jnp.int32, (2 * ATTN_BLOCK, ATTN_BLOCK), 1)
        own = jnp.logical_and(rows < ATTN_BLOCK, keys <= rows)
        nxt = jnp.logical_and(rows >= ATTN_BLOCK, jnp.logical_and(keys >= rows - ATTN_BLOCK, n + 1 < nb))
        mask = jnp.logical_or(own, nxt)
        both = lambda a_ref, b_ref, cols: jnp.concatenate([a_ref[:, cols], b_ref[:, cols]], axis=0)
        for hh, hs in enumerate(HEAD_SLICES):
            one = slice(hh, hh + 1)
            q2, do2 = both(q_ref, qn_ref, hs), both(do_ref, don_ref, hs)
            kh, vh = k_ref[:, hs], v_ref[:, hs]
            sc = jnp.where(mask, _dot_nt(q2, kh) * ATTN_SCALE, NEG)
            p = jnp.exp(sc - both(lse_ref, lsen_ref, one))
            ds = (p * (_dot_nt(do2, vh) - both(dl_ref, dln_ref, one)) * ATTN_SCALE).astype(BF16)
            dq2 = _dot(ds, kh)
            dq = carry_ref[:, hs] + dq2[:ATTN_BLOCK]
            carry_ref[:, hs] = dq2[ATTN_BLOCK:]
            out_ref[:, hs] = _rope(dq, tab_ref[...], -1.0).astype(out_ref.dtype)
            out_ref[:, wide + hh * ATTN_HEAD_DIM:wide + (hh + 1) * ATTN_HEAD_DIM] = _rope(
                _dot_tn(ds, q2), tab_ref[...], -1.0).astype(out_ref.dtype)
            out_ref[:, 2 * wide + hh * ATTN_HEAD_DIM:2 * wide + (hh + 1) * ATTN_HEAD_DIM] = _dot_tn(
                p.astype(BF16), do2).astype(out_ref.dtype)

    def spec(width, row, col):
        return pl.BlockSpec((ATTN_BLOCK, width), lambda r, n: (r * nb + row(n), col))

    cur = lambda n: n
    nxt_block = lambda n: jnp.minimum(n + 1, nb - 1)
    return pl.pallas_call(
        body, name=name, out_shape=jax.ShapeDtypeStruct((s, QKV_GROUP), BF16), grid=(d, nb),
        in_specs=[spec(wide, cur, 0), spec(wide, nxt_block, 0), spec(wide, cur, 1), spec(wide, cur, 2),
                  spec(wide, cur, 0), spec(wide, nxt_block, 0), spec(128, cur, 0), spec(128, nxt_block, 0),
                  spec(128, cur, 0), spec(128, nxt_block, 0), spec(384, cur, 0)],
        out_specs=spec(QKV_GROUP, cur, 0),
        scratch_shapes=[pltpu.VMEM((ATTN_BLOCK, wide), F32)],
        compiler_params=_params("parallel", "arbitrary"),
    )(qkv, qkv, qkv, qkv, do, do, lse, lse, delta, delta, tab)


def _attn_core_fwd(qkvs, name):
    os_, lses = [], []
    for g, (qkv, d) in enumerate(zip(qkvs, ATTN_DILATIONS)):
        o_g, lse_g = _attn_group_fwd(qkv, d, f"{name}_fwd{g}")
        os_.append(_from_dilated(o_g, d))
        lses.append(_from_dilated(lse_g, d))
    o, lse = _attn_combine(os_, lses, name + "_combine")
    return o, (tuple(qkvs), o, lse)


@functools.partial(jax.custom_vjp, nondiff_argnums=(1,))
def attn_core(qkvs, name):
    return _attn_core_fwd(qkvs, name)[0]


def _attn_core_vjp_fwd(qkvs, name):
    return _attn_core_fwd(qkvs, name)


def _attn_core_vjp_bwd(name, res, do):
    rot, o, lse = res
    delta = _attn_delta(do, o, name + "_delta")
    out = []
    for g, (qkv, d) in enumerate(zip(rot, ATTN_DILATIONS)):
        out.append(_attn_group_bwd(qkv, _to_dilated(do, d), _to_dilated(lse, d), _to_dilated(delta, d),
                                   rope_table(o.shape[0], d), d, f"{name}_bwd{g}"))
    return (tuple(out),)


attn_core.defvjp(_attn_core_vjp_fwd, _attn_core_vjp_bwd)


def attention_mixer_p(x, h, lin, j, carries, tag):
    qkvs = tuple(lin(_to_dilated(h, d), ('attn_w_qkv', j), BF16, f"{tag}_qkv{g}",
                     cols=slice(g * QKV_GROUP, (g + 1) * QKV_GROUP), carry=carries[g],
                     rope=rope_table(h.shape[0], d))
                 for g, d in enumerate(ATTN_DILATIONS))
    return lin(attn_core(qkvs, tag), ('attn_w_o', j), F32, tag + "_o", res=x)


def first_attention_mixer(x, h, lin, w, ready, carries, tag):
    s = h.shape[0]
    qkvs, moved = qkv_first(tuple(_to_dilated(h, d) for d in ATTN_DILATIONS), w['attn_w_qkv'][0],
                            tuple(rope_table(s, d) for d in ATTN_DILATIONS),
                            tuple(w[n][j] for n, j in carries), tag + "_qkv")
    for (n, j), parts in zip(carries, moved):
        ready[(n, j)] = _full_weight(n, parts)
    return lin(attn_core(qkvs, tag), ('attn_w_o', 0), F32, tag + "_o", res=x)


SSM_CONV_TAPS = 4
SSM_COL_BLOCK = 2048
SSM_PAIRS = SSM_HEADS // 2
SSM_DT_BLOCK = (SSM_D_INNER + SSM_CONV_DIM) // 128


def _ssm_conv_fwd(zx, conv_w, conv_b, name):
    s = zx.shape[0]
    tm = _pick(s, (512, 256, 128))
    per = tm // SUBLANES
    ncb = SSM_CONV_DIM // SSM_COL_BLOCK

    def body(w_ref, b_ref, x_ref, halo_ref, o_ref, pre_ref):
        keep = pl.program_id(1) > 0
        for c0 in range(0, SSM_COL_BLOCK, FFN_COLS):
            cols = slice(c0, c0 + FFN_COLS)
            halo = jnp.where(keep, halo_ref[:, cols], 0.0)
            taps = _conv_taps(x_ref[:, cols], halo, SSM_CONV_TAPS)
            pre = sum(w_ref[k:k + 1, cols] * taps[k] for k in range(SSM_CONV_TAPS)) + b_ref[:, cols]
            pre_ref[:, cols] = pre.astype(pre_ref.dtype)
            o_ref[:, cols] = pre * _sigmoid(pre)

    blk = pl.BlockSpec((tm, SSM_COL_BLOCK), lambda j, i: (i, j))
    return pl.pallas_call(
        body, name=name,
        out_shape=[jax.ShapeDtypeStruct((s, SSM_CONV_DIM), F32), jax.ShapeDtypeStruct((s, SSM_CONV_DIM), BF16)],
        grid=(ncb, s // tm),
        in_specs=[pl.BlockSpec((SUBLANES, SSM_COL_BLOCK), lambda j, i: (0, j)),
                  pl.BlockSpec((1, SSM_COL_BLOCK), lambda j, i: (0, j)),
                  pl.BlockSpec((tm, SSM_COL_BLOCK), lambda j, i: (i, j + 1)),
                  pl.BlockSpec((SUBLANES, SSM_COL_BLOCK), lambda j, i: (jnp.maximum(i * per - 1, 0), j + 1))],
        out_specs=[blk, blk], compiler_params=_params("parallel", "parallel"),
    )(_pad_taps(conv_w), conv_b.reshape(1, SSM_CONV_DIM), zx, zx)


def _ssm_conv_bwd(zx, pre, conv_w, dact, dzx, name):
    s = zx.shape[0]
    tm = _pick(s, (512, 256, 128))
    nt = s // tm
    ncb = SSM_CONV_DIM // SSM_COL_BLOCK
    nk = SSM_CONV_TAPS

    def body(w_ref, x_ref, pre_ref, da_ref, _, dx_ref, dw_ref, db_ref, carry_ref):
        first = pl.program_id(1) == 0
        for c0 in range(0, SSM_COL_BLOCK, FFN_COLS):
            cols = slice(c0, c0 + FFN_COLS)
            prev = pre_ref[:, cols].astype(F32)
            sig = _sigmoid(prev)
            dpre = da_ref[:, cols] * sig * (1.0 + prev * (1.0 - sig))
            nxt = jnp.where(first, 0.0, carry_ref[:, cols])
            ahead = [_shift_up(dpre, nxt, nk - 1 - k) for k in range(nk - 1)] + [dpre]
            dx_ref[:, cols] = sum(w_ref[k:k + 1, cols] * ahead[k] for k in range(nk)).astype(dx_ref.dtype)
            carry_ref[:, cols] = dpre[0:SUBLANES, :]
            x = x_ref[:, cols]
            dwp = jnp.concatenate([jnp.sum(ahead[k] * x, axis=0, keepdims=True) for k in range(nk)]
                                  + [jnp.zeros((SUBLANES - nk, FFN_COLS), F32)], axis=0)
            dbp = jnp.sum(dpre, axis=0, keepdims=True)

            @pl.when(first)
            def _():
                dw_ref[:, cols] = dwp
                db_ref[:, cols] = dbp

            @pl.when(jnp.logical_not(first))
            def _():
                dw_ref[:, cols] += dwp
                db_ref[:, cols] += dbp

    rev = lambda i: nt - 1 - i
    dzx, dw, db = pl.pallas_call(
        body, name=name,
        out_shape=[jax.ShapeDtypeStruct(dzx.shape, dzx.dtype), jax.ShapeDtypeStruct((SUBLANES, SSM_CONV_DIM), F32),
                   jax.ShapeDtypeStruct((1, SSM_CONV_DIM), F32)],
        grid=(ncb, nt),
        in_specs=[pl.BlockSpec((SUBLANES, SSM_COL_BLOCK), lambda j, i: (0, j)),
                  pl.BlockSpec((tm, SSM_COL_BLOCK), lambda j, i: (rev(i), j + 1)),
                  pl.BlockSpec((tm, SSM_COL_BLOCK), lambda j, i: (rev(i), j)),
                  pl.BlockSpec((tm, SSM_COL_BLOCK), lambda j, i: (rev(i), j)),
                  pl.BlockSpec(memory_space=pl.ANY)],
        out_specs=[pl.BlockSpec((tm, SSM_COL_BLOCK), lambda j, i: (rev(i), j + 1)),
                   pl.BlockSpec((SUBLANES, SSM_COL_BLOCK), lambda j, i: (0, j)),
                   pl.BlockSpec((1, SSM_COL_BLOCK), lambda j, i: (0, j))],
        scratch_shapes=[pltpu.VMEM((SUBLANES, SSM_COL_BLOCK), F32)],
        input_output_aliases={4: 0},
        compiler_params=_params("parallel", "arbitrary"),
    )(_pad_taps(conv_w), zx, pre, dact, dzx)
    return dzx, dw[:nk], db.reshape(SSM_CONV_DIM)


def _ssd_chunk(xs, bms, cms, dt_raw, dtb, alog, dsk, states):
    q = SSM_CHUNK
    lane = lax.broadcasted_iota(jnp.int32, (1, 128), 1)
    row = lax.broadcasted_iota(jnp.int32, (q, 1), 0)
    ii = lax.broadcasted_iota(jnp.int32, (q, q), 0)
    jj = lax.broadcasted_iota(jnp.int32, (q, q), 1)
    tril = ii >= jj
    left = lane < SSM_HEAD_DIM
    last_row = (row == q - 1).astype(F32)

    def lanes_of(mat, h):
        pick = (lane == h).astype(F32)
        return jnp.broadcast_to(jnp.sum(mat * pick, axis=1, keepdims=True), mat.shape)

    def rows_of(mat_t, h):
        pick = (row == h).astype(F32)
        return jnp.broadcast_to(jnp.sum(mat_t * pick, axis=0, keepdims=True), mat_t.shape)

    v = dt_raw + dtb
    dt = jnp.maximum(v, 0.0) + jnp.log(1.0 + jnp.exp(-jnp.abs(v)))
    adt = dt * (-jnp.exp(alog))
    acs = jnp.dot(tril.astype(F32), adt, precision=lax.Precision.HIGHEST, preferred_element_type=F32)
    acs_t = acs.T
    ys, new_states = [], []
    for pr in range(SSM_PAIRS):
        g = pr // 2
        if pr % 2 == 0:
            cb = _dot_nt(cms[g].astype(BF16), bms[g].astype(BF16))
        cols = [lanes_of(acs, 2 * pr + e) for e in range(2)]
        dts = [lanes_of(dt, 2 * pr + e) for e in range(2)]
        rws = [rows_of(acs_t, 2 * pr + e) for e in range(2)]
        lasts = [jnp.sum(c * last_row, axis=0, keepdims=True) for c in cols]
        xdt = xs[pr] * jnp.where(left, dts[0], dts[1])
        halves = [jnp.where(left, xdt, 0.0).astype(BF16), jnp.where(left, 0.0, xdt).astype(BF16)]
        y_diag, s_new = 0.0, 0.0
        for e in range(2):
            lmat = jnp.where(tril, jnp.exp(jnp.minimum(cols[e] - rws[e], 0.0)), 0.0)
            y_diag = y_diag + _dot((cb * lmat).astype(BF16), halves[e])
            decay = jnp.exp(lasts[e] - cols[e])
            s_new = s_new + _dot_tn((bms[g] * decay).astype(BF16), halves[e])
        y_off = _dot(cms[g].astype(BF16), states[pr].astype(BF16)) * jnp.where(left, jnp.exp(cols[0]), jnp.exp(cols[1]))
        skip = jnp.where(left, lanes_of(dsk, 2 * pr), lanes_of(dsk, 2 * pr + 1))
        ys.append(y_diag + y_off + xs[pr] * skip)
        new_states.append(states[pr] * jnp.where(left, jnp.exp(lasts[0]), jnp.exp(lasts[1])) + s_new)
    return tuple(ys), tuple(new_states)


def _ssm_vec(v):
    return jnp.pad(v.reshape(1, -1), ((0, 0), (0, 128 - v.shape[0])))


def _ssd_scan_fwd(act, zx, dtb, alog, dsk, name):
    s = act.shape[0]
    nc = s // SSM_CHUNK
    ng = SSM_GROUPS

    def body(act_ref, dt_ref, dtb_ref, alog_ref, dsk_ref, y_ref, st_out_ref, st_ref):
        @pl.when(pl.program_id(0) == 0)
        def _():
            st_ref[...] = jnp.zeros_like(st_ref)

        tile = lambda k: act_ref[:, k * 128:(k + 1) * 128]
        xs = [tile(k) for k in range(SSM_PAIRS)]
        bms = [tile(SSM_PAIRS + k) for k in range(ng)]
        cms = [tile(SSM_PAIRS + ng + k) for k in range(ng)]
        states = [st_ref[k] for k in range(SSM_PAIRS)]
        st_out_ref[0] = st_ref[...]
        ys, new_states = _ssd_chunk(xs, bms, cms, dt_ref[...], dtb_ref[...], alog_ref[...], dsk_ref[...], states)
        for k in range(SSM_PAIRS):
            y_ref[:, k * 128:(k + 1) * 128] = ys[k]
            st_ref[k] = new_states[k]

    vec = pl.BlockSpec((1, 128), lambda c: (0, 0))
    return pl.pallas_call(
        body, name=name,
        out_shape=[jax.ShapeDtypeStruct((s, SSM_D_INNER), F32),
                   jax.ShapeDtypeStruct((nc, SSM_PAIRS, SSM_STATE, 128), F32)],
        grid=(nc,),
        in_specs=[pl.BlockSpec((SSM_CHUNK, SSM_CONV_DIM), lambda c: (c, 0)),
                  pl.BlockSpec((SSM_CHUNK, 128), lambda c: (c, SSM_DT_BLOCK)), vec, vec, vec],
        out_specs=[pl.BlockSpec((SSM_CHUNK, SSM_D_INNER), lambda c: (c, 0)),
                   pl.BlockSpec((1, SSM_PAIRS, SSM_STATE, 128), lambda c: (c, 0, 0, 0))],
        scratch_shapes=[pltpu.VMEM((SSM_PAIRS, SSM_STATE, 128), F32)],
        compiler_params=_params("arbitrary"),
    )(act, zx, _ssm_vec(dtb), _ssm_vec(alog), _ssm_vec(dsk))


def _ssd_scan_bwd(act, zx, dtb, alog, dsk, st_in, dy, dzx, name):
    s = act.shape[0]
    nc = s // SSM_CHUNK
    ng = SSM_GROUPS
    dt_col = SSM_D_INNER + SSM_CONV_DIM
    tail = zx.shape[1] - dt_col
    assert tail % 128 == 0 and dt_col % tail == 0, tail

    def body(act_ref, dt_ref, dtb_ref, alog_ref, dsk_ref, st_ref, dy_ref, _, dact_ref, ddt_ref, dpar_ref, dst_ref):
        first = pl.program_id(0) == 0

        @pl.when(first)
        def _():
            dst_ref[...] = jnp.zeros_like(dst_ref)

        tile = lambda k: act_ref[:, k * 128:(k + 1) * 128]
        xs = [tile(k) for k in range(SSM_PAIRS)]
        bms = [tile(SSM_PAIRS + k) for k in range(ng)]
        cms = [tile(SSM_PAIRS + ng + k) for k in range(ng)]
        states = [st_ref[0, k] for k in range(SSM_PAIRS)]
        _, pullback = jax.vjp(_ssd_chunk, xs, bms, cms, dt_ref[...], dtb_ref[...], alog_ref[...], dsk_ref[...],
                              states)
        dys = tuple(dy_ref[:, k * 128:(k + 1) * 128] for k in range(SSM_PAIRS))
        dsts = tuple(dst_ref[k] for k in range(SSM_PAIRS))
        dxs, dbms, dcms, ddt, ddtb, dalog, ddsk, dstates = pullback((dys, dsts))
        for k, t in enumerate(list(dxs) + list(dbms) + list(dcms)):
            dact_ref[:, k * 128:(k + 1) * 128] = t
        ddt_ref[:, 0:128] = ddt.astype(ddt_ref.dtype)
        if tail > 128:
            ddt_ref[:, 128:] = jnp.zeros((SSM_CHUNK, tail - 128), ddt_ref.dtype)
        for k in range(SSM_PAIRS):
            dst_ref[k] = dstates[k]
        dpar = jnp.concatenate([ddtb, dalog, ddsk, jnp.zeros((SUBLANES - 3, 128), F32)], axis=0)
        _accumulate(dpar_ref, dpar, first)

    rev = lambda c: nc - 1 - c
    vec = pl.BlockSpec((1, 128), lambda c: (0, 0))
    dact, dzx, dpar = pl.pallas_call(
        body, name=name,
        out_shape=[jax.ShapeDtypeStruct((s, SSM_CONV_DIM), F32), jax.ShapeDtypeStruct(dzx.shape, dzx.dtype),
                   jax.ShapeDtypeStruct((SUBLANES, 128), F32)],
        grid=(nc,),
        in_specs=[pl.BlockSpec((SSM_CHUNK, SSM_CONV_DIM), lambda c: (rev(c), 0)),
                  pl.BlockSpec((SSM_CHUNK, 128), lambda c: (rev(c), SSM_DT_BLOCK)), vec, vec, vec,
                  pl.BlockSpec((1, SSM_PAIRS, SSM_STATE, 128), lambda c: (rev(c), 0, 0, 0)),
                  pl.BlockSpec((SSM_CHUNK, SSM_D_INNER), lambda c: (rev(c), 0)),
                  pl.BlockSpec(memory_space=pl.ANY)],
        out_specs=[pl.BlockSpec((SSM_CHUNK, SSM_CONV_DIM), lambda c: (rev(c), 0)),
                   pl.BlockSpec((SSM_CHUNK, tail), lambda c: (rev(c), dt_col // tail)),
                   pl.BlockSpec((SUBLANES, 128), lambda c: (0, 0))],
        scratch_shapes=[pltpu.VMEM((SSM_PAIRS, SSM_STATE, 128), F32)],
        input_output_aliases={7: 1},
        compiler_params=_params("arbitrary"),
    )(act, zx, _ssm_vec(dtb), _ssm_vec(alog), _ssm_vec(dsk), st_in, dy, dzx)
    return dact, dzx, dpar[0, :SSM_HEADS], dpar[1, :SSM_HEADS], dpar[2, :SSM_HEADS]


SSM_NORM_GROUP = SSM_D_INNER // SSM_GROUPS


def _gated_group(y, z, w):
    g = y * (z * _sigmoid(z))
    return g * lax.rsqrt(jnp.mean(g * g, axis=-1, keepdims=True) + NORM_EPS) * w


def _gated_norm_fwd(y, zx, w, name):
    s = y.shape[0]
    tm = _pick(s, (512, 256, 128))

    def body(y_ref, z_ref, w_ref, o_ref):
        for c0 in range(0, SSM_D_INNER, SSM_NORM_GROUP):
            cols = slice(c0, c0 + SSM_NORM_GROUP)
            o_ref[:, cols] = _gated_group(y_ref[:, cols], z_ref[:, cols], w_ref[:, cols]).astype(o_ref.dtype)

    blk = pl.BlockSpec((tm, SSM_D_INNER), lambda i: (i, 0))
    return pl.pallas_call(
        body, name=name, out_shape=jax.ShapeDtypeStruct((s, SSM_D_INNER), BF16), grid=(s // tm,),
        in_specs=[blk, blk, pl.BlockSpec((1, SSM_D_INNER), lambda i: (0, 0))], out_specs=blk,
        compiler_params=_params("parallel"),
    )(y, zx, w.reshape(1, SSM_D_INNER))


def _gated_norm_bwd(y, zx, w, dout, name):
    s = y.shape[0]
    tm = _pick(s, (512, 256, 128))

    def body(y_ref, z_ref, w_ref, do_ref, dy_ref, dz_ref, dw_ref):
        first = pl.program_id(0) == 0
        for c0 in range(0, SSM_D_INNER, SSM_NORM_GROUP):
            cols = slice(c0, c0 + SSM_NORM_GROUP)
            _, pullback = jax.vjp(_gated_group, y_ref[:, cols], z_ref[:, cols], w_ref[:, cols])
            dyv, dzv, dwv = pullback(do_ref[:, cols].astype(F32))
            dy_ref[:, cols] = dyv
            dz_ref[:, cols] = dzv.astype(dz_ref.dtype)

            @pl.when(first)
            def _():
                dw_ref[:, cols] = dwv

            @pl.when(jnp.logical_not(first))
            def _():
                dw_ref[:, cols] += dwv

    blk = pl.BlockSpec((tm, SSM_D_INNER), lambda i: (i, 0))
    vec = pl.BlockSpec((1, SSM_D_INNER), lambda i: (0, 0))
    dy, dz, dw = pl.pallas_call(
        body, name=name,
        out_shape=[jax.ShapeDtypeStruct((s, SSM_D_INNER), F32), jax.ShapeDtypeStruct((s, zx.shape[1]), BF16),
                   jax.ShapeDtypeStruct((1, SSM_D_INNER), F32)],
        grid=(s // tm,), in_specs=[blk, blk, vec, blk], out_specs=[blk, blk, vec],
        compiler_params=_params("arbitrary"),
    )(y, zx, w.reshape(1, SSM_D_INNER), dout)
    return dy, dz, dw.reshape(SSM_D_INNER)


def _ssm_core_fwd(zx, conv_w, conv_b, dtb, alog, dsk, norm_w, name):
    act, pre = _ssm_conv_fwd(zx, conv_w, conv_b, name + "_conv_fwd")
    y, st_in = _ssd_scan_fwd(act, zx, dtb, alog, dsk, name + "_scan_fwd")
    out = _gated_norm_fwd(y, zx, norm_w, name + "_gate_fwd")
    return out, (zx, conv_w, pre, dtb, alog, dsk, norm_w, act, y, st_in)


@functools.partial(jax.custom_vjp, nondiff_argnums=(7,))
def ssm_core(zx, conv_w, conv_b, dtb, alog, dsk, norm_w, name):
    return _ssm_core_fwd(zx, conv_w, conv_b, dtb, alog, dsk, norm_w, name)[0]


def _ssm_core_vjp_fwd(zx, conv_w, conv_b, dtb, alog, dsk, norm_w, name):
    return _ssm_core_fwd(zx, conv_w, conv_b, dtb, alog, dsk, norm_w, name)


def _ssm_core_vjp_bwd(name, res, dout):
    zx, conv_w, pre, dtb, alog, dsk, norm_w, act, y, st_in = res
    dy, dzx, dnorm_w = _gated_norm_bwd(y, zx, norm_w, dout, name + "_gate_bwd")
    dact, dzx, ddtb, dalog, ddsk = _ssd_scan_bwd(act, zx, dtb, alog, dsk, st_in, dy, dzx, name + "_scan_bwd")
    dzx, dconv_w, dconv_b = _ssm_conv_bwd(zx, pre, conv_w, dact, dzx, name + "_conv_bwd")
    return dzx, dconv_w, dconv_b, ddtb, dalog, ddsk, dnorm_w


ssm_core.defvjp(_ssm_core_vjp_fwd, _ssm_core_vjp_bwd)


def ssd_mixer_p(x, h, lin, j, carries, conv_w, conv_b, dtb, alog, dsk, norm_w, tag):
    zx = lin(h, ('ssm_w_in', j), F32, tag + "_in", carry=carries[0])
    return lin(ssm_core(zx, conv_w, conv_b, dtb, alog, dsk, norm_w, tag), ('ssm_w_out', j), F32, tag + "_out",
               carry=carries[1], res=x)


def _full_weight(n, parts):
    if SHARD_AXIS[n] - 1 == 0:
        return _join8(parts, 0)
    pieces = [parts[k] for k in range(N_DEV)]
    if n == 'ssm_w_in':
        pieces.append(jnp.zeros((parts.shape[1], SSM_IN_PAD - SSM_IN_WIDTH), parts.dtype))
    return jnp.concatenate(pieces, axis=1)


def trunk(w, x):
    ready = {}

    def lin(a, key, out_dtype, name, cols=None, carry=None, res=None, rope=None):
        wt = ready[key] if cols is None else ready[key][:, cols]
        if carry is None:
            assert rope is None
            return linear(a, wt, res, out_dtype, name)
        y, parts = linear_x(a, wt, res, rope, w[carry[0]][carry[1]], out_dtype, name)
        ready[carry] = _full_weight(carry[0], parts)
        return y

    for i in range(DEPTH):
        x, h = norm(x, w['mix_norm_w'][i], f"mixnorm{i}")
        j = i // 2
        ffn_next = [('ffn_w_up', i), ('ffn_w_down', i)]
        if i == 0:
            x = first_attention_mixer(x, h, lin, w, ready, [('attn_w_o', 0)] + ffn_next, "attn0")
        elif i % 2 == 0:
            x = attention_mixer_p(x, h, lin, j, [('attn_w_o', j)] + ffn_next, f"attn{j}")
        else:
            x = ssd_mixer_p(x, h, lin, j, ffn_next, w['ssm_conv_w'][j], w['ssm_conv_b'][j], w['ssm_dt_bias'][j],
                            w['ssm_a_log'][j], w['ssm_d'][j], w['ssm_norm_w'][j], f"ssm{j}")
        if i + 1 == DEPTH:
            mixer_next = [None, None]
        elif i % 2 == 0:
            mixer_next = [('ssm_w_in', j), ('ssm_w_out', j)]
        else:
            mixer_next = [('attn_w_qkv', j + 1), None]
        x, h = norm(x, w['ffn_norm_w'][i], f"ffnnorm{i}")
        u0 = lin(h, ('ffn_w_up', i), F32, f"ffn{i}_up", carry=mixer_next[0])
        a = ffn_mid(u0, w['ffn_conv_w'][i], w['ffn_conv_b'][i], f"ffn{i}_mid")
        x = lin(a, ('ffn_w_down', i), F32, f"ffn{i}_down", carry=mixer_next[1], res=x)
    return x


def local_step(w, x, target):
    final_w = w['final_norm_w']
    trunk_w = {n: a for n, a in w.items() if n != 'final_norm_w'}
    xf, pullback = jax.vjp(trunk, trunk_w, x)
    loss, dxf, dfinal = loss_head(xf, final_w, target, "loss_head")
    gw, gx = pullback(dxf)
    gw['final_norm_w'] = dfinal
    return loss, gw, gx


def _adam_math(w, g, m, v):
    m = ADAM_B1 * m + (1.0 - ADAM_B1) * g
    v = ADAM_B2 * v + (1.0 - ADAM_B2) * (g * g)
    m_hat = m / (1.0 - ADAM_B1 ** ADAM_STEP)
    v_hat = v / (1.0 - ADAM_B2 ** ADAM_STEP)
    delta = -ADAM_LR * (m_hat / (jnp.sqrt(v_hat) + ADAM_EPS) + ADAM_WD * w)
    return delta, m, v


def _adamw_rows(g, w, m, v, name):
    r, c = w.shape
    tr = _pick(r, (256, 128, 64, 32, 16, 8))

    def body(g_ref, w_ref, m_ref, v_ref, d_out, m_out, v_out):
        delta, mm, vv = _adam_math(w_ref[...], g_ref[...], m_ref[...], v_ref[...])
        d_out[...] = delta
        m_out[...] = mm
        v_out[...] = vv

    blk = pl.BlockSpec((tr, c), lambda i: (i, 0))
    return pl.pallas_call(
        body, name=name, out_shape=[jax.ShapeDtypeStruct((r, c), F32)] * 3, grid=(r // tr,),
        in_specs=[blk] * 4, out_specs=[blk] * 3, compiler_params=_params("parallel"),
    )(g, w, m, v)


def _sum8(pieces, name):
    _, r, c = pieces.shape

    def body(p_ref, o_ref):
        g = p_ref[0]
        for j in range(1, N_DEV):
            g = g + p_ref[j]
        o_ref[...] = g

    return pl.pallas_call(
        body, name=name, out_shape=jax.ShapeDtypeStruct((r, c), F32),
        in_specs=[pl.BlockSpec(memory_space=pltpu.VMEM)], out_specs=pl.BlockSpec(memory_space=pltpu.VMEM),
    )(pieces)


def _adamw_plain(g, w, m, v, name):
    def body(g_ref, w_ref, m_ref, v_ref, d_out, m_out, v_out):
        delta, mm, vv = _adam_math(w_ref[...], g_ref[...], m_ref[...], v_ref[...])
        d_out[...] = delta
        m_out[...] = mm
        v_out[...] = vv

    vm = pl.BlockSpec(memory_space=pltpu.VMEM)
    return pl.pallas_call(
        body, name=name, out_shape=[jax.ShapeDtypeStruct(g.shape, F32)] * 3,
        in_specs=[vm] * 4, out_specs=[vm] * 3,
    )(g, w, m, v)


def _join8(parts, axis):
    t = jnp.moveaxis(parts, 0, axis)
    shp = t.shape
    return t.reshape(shp[:axis] + (shp[axis] * shp[axis + 1],) + shp[axis + 2:])


def _pack(arrs, lead, mult):
    flat = jnp.concatenate([a.reshape(a.shape[:lead] + (-1,)) for a in arrs], axis=-1)
    return _pad_rows(flat, mult)


def _unpack(buf, shapes, lead):
    flat = buf.reshape(buf.shape[:lead] + (-1,))
    out, off = [], 0
    for shp in shapes:
        n = math.prod(shp)
        out.append(flat[..., off:off + n].reshape(flat.shape[:lead] + tuple(shp)))
        off += n
    return out


def _own_shard(full, axis):
    size = full.shape[axis] // N_DEV
    return lax.dynamic_slice_in_dim(full, _my_index() * size, size, axis)


def kernel(x, mix_norm_w, attn_w_qkv, attn_w_o, ssm_w_in, ssm_conv_w, ssm_conv_b, ssm_dt_bias, ssm_a_log, ssm_d, ssm_norm_w, ssm_w_out, ffn_norm_w, ffn_w_up, ffn_conv_w, ffn_conv_b, ffn_w_down, final_norm_w, loss_target, m_mix_norm_w, m_attn_w_qkv, m_attn_w_o, m_ssm_w_in, m_ssm_conv_w, m_ssm_conv_b, m_ssm_dt_bias, m_ssm_a_log, m_ssm_d, m_ssm_norm_w, m_ssm_w_out, m_ffn_norm_w, m_ffn_w_up, m_ffn_conv_w, m_ffn_conv_b, m_ffn_w_down, m_final_norm_w, v_mix_norm_w, v_attn_w_qkv, v_attn_w_o, v_ssm_w_in, v_ssm_conv_w, v_ssm_conv_b, v_ssm_dt_bias, v_ssm_a_log, v_ssm_d, v_ssm_norm_w, v_ssm_w_out, v_ffn_norm_w, v_ffn_w_up, v_ffn_conv_w, v_ffn_conv_b, v_ffn_w_down, v_final_norm_w):
    args = dict(locals())
    w_sh = {n: args[n] for n in WEIGHT_NAMES}
    m_sh = {n: args["m_" + n] for n in WEIGHT_NAMES}
    v_sh = {n: args["v_" + n] for n in WEIGHT_NAMES}

    small_shapes = [w_sh[n].shape for n in SMALL_SHARDED]
    small = _exchange(_pack([w_sh[n] for n in SMALL_SHARDED], 0, 8), True, "gather_small")
    full = {n: w_sh[n] for n in SMALL if SHARD_AXIS[n] is None}
    for n, parts in zip(SMALL_SHARDED, _unpack(small, small_shapes, 1)):
        full[n] = _join8(parts, SHARD_AXIS[n])
    for n in BIG:
        full[n] = [w_sh[n][j] for j in range(w_sh[n].shape[0])]

    loss, gw, gx = local_step(full, x[0], loss_target[0])
    loss = lax.psum(loss, ("x", "y", "c"))
    for n in BIG:
        gw[n] = jnp.stack(gw[n])

    grads, deltas, new_m, new_v = {}, {}, {}, {}
    for n in BIG:
        shp = w_sh[n].shape
        two_d = (shp[0] * shp[1], shp[2])
        outs = _adamw_rows(*[t.reshape(two_d) for t in (gw[n], w_sh[n], m_sh[n], v_sh[n])], "adamw_" + n)
        grads[n] = gw[n]
        deltas[n], new_m[n], new_v[n] = [o.reshape(shp) for o in outs]

    small_full_shapes = [gw[n].shape for n in SMALL]
    gsmall = _exchange(_pack([gw[n] for n in SMALL], 0, 8), True, "gather_small_grads")
    gsmall = _unpack(_sum8(gsmall, "sum_small_grads"), small_full_shapes, 0)
    for n, g in zip(SMALL, gsmall):
        grads[n] = g if SHARD_AXIS[n] is None else _own_shard(g, SHARD_AXIS[n])
    shapes = [w_sh[n].shape for n in SMALL]
    outs = _adamw_plain(*[_pack([d[n] for n in SMALL], 0, 8) for d in (grads, w_sh, m_sh, v_sh)], "adamw_small")
    for d, buf in zip((deltas, new_m, new_v), outs):
        for n, a in zip(SMALL, _unpack(buf, shapes, 0)):
            d[n] = a

    return (loss, gx[None], *[grads[n] for n in WEIGHT_NAMES], *[deltas[n] for n in WEIGHT_NAMES],
            *[new_m[n] for n in WEIGHT_NAMES], *[new_v[n] for n in WEIGHT_NAMES])
```

```python
import functools
import math

import jax
import jax.numpy as jnp
from jax import lax
from jax.experimental import pallas as pl
from jax.experimental.pallas import tpu as pltpu

F32 = jnp.float32
BF16 = jnp.bfloat16
N_DEV = 8
MESH_ID = pl.DeviceIdType.MESH

DEPTH = 4
ATTN_HEADS = 8
ATTN_HEAD_DIM = 128
ATTN_DILATIONS = (1, 4, 16)
ATTN_BLOCK = 128
ROPE_THETA = 500000.0
ROPE_DIM = 32
ATTN_OUT_WIDTH = 1024
SSM_D_INNER = 2048
SSM_HEAD_DIM = 64
SSM_HEADS = 32
SSM_STATE = 128
SSM_GROUPS = 8
SSM_CHUNK = 128
SSM_CONV_DIM = 4096
SSM_IN_WIDTH = 6176
SSM_IN_PAD = 6400
NORM_EPS = 1e-5
ADAM_LR = 0.001
ADAM_B1 = 0.9
ADAM_B2 = 0.999
ADAM_EPS = 1e-08
ADAM_WD = 0.01
ADAM_STEP = 10

WEIGHT_NAMES = ['mix_norm_w', 'attn_w_qkv', 'attn_w_o', 'ssm_w_in', 'ssm_conv_w', 'ssm_conv_b', 'ssm_dt_bias',
                'ssm_a_log', 'ssm_d', 'ssm_norm_w', 'ssm_w_out', 'ffn_norm_w', 'ffn_w_up', 'ffn_conv_w',
                'ffn_conv_b', 'ffn_w_down', 'final_norm_w']
SHARD_AXIS = {'mix_norm_w': None, 'attn_w_qkv': 2, 'attn_w_o': 1, 'ssm_w_in': 2, 'ssm_conv_w': 2, 'ssm_conv_b': 1,
              'ssm_dt_bias': None, 'ssm_a_log': None, 'ssm_d': None, 'ssm_norm_w': 1, 'ssm_w_out': 1,
              'ffn_norm_w': None, 'ffn_w_up': 2, 'ffn_conv_w': 2, 'ffn_conv_b': None, 'ffn_w_down': 1,
              'final_norm_w': None}
BIG = ['attn_w_qkv', 'attn_w_o', 'ssm_w_in', 'ssm_w_out', 'ffn_w_up', 'ffn_w_down']
SMALL = [n for n in WEIGHT_NAMES if n not in BIG]
SMALL_SHARDED = [n for n in SMALL if SHARD_AXIS[n] is not None]
LANES = 1024


def _my_index():
    return 4 * lax.axis_index("x") + 2 * lax.axis_index("y") + lax.axis_index("c")


def _peer(k):
    x, y, c = lax.axis_index("x"), lax.axis_index("y"), lax.axis_index("c")
    return (x ^ ((k >> 2) & 1), y ^ ((k >> 1) & 1), c ^ (k & 1))


def _exchange(src, gather, name):
    def body(src_ref, out_ref, send_sems, recv_sems, local_sem):
        start, wait = _exchange_copies(src_ref, out_ref, send_sems, recv_sems, local_sem, gather)
        start()
        wait()

    return pl.pallas_call(
        body, name=name,
        out_shape=_exchange_out(src, gather),
        in_specs=[pl.BlockSpec(memory_space=pl.ANY)],
        out_specs=pl.BlockSpec(memory_space=pl.ANY),
        scratch_shapes=list(EXCHANGE_SEMS),
    )(src)


EXCHANGE_SEMS = (pltpu.SemaphoreType.DMA((N_DEV - 1,)), pltpu.SemaphoreType.DMA((N_DEV - 1,)),
                 pltpu.SemaphoreType.DMA)


def _exchange_out(src, gather):
    return jax.ShapeDtypeStruct((N_DEV,) + (src.shape if gather else src.shape[1:]), src.dtype)


def _exchange_copies(src_ref, out_ref, send_sems, recv_sems, local_sem, gather):
    me = _my_index()

    def piece(j):
        return src_ref if gather else src_ref.at[j]

    def remote(k, slab):
        return pltpu.make_async_remote_copy(
            src_ref=piece(me ^ k), dst_ref=out_ref.at[slab], send_sem=send_sems.at[k - 1],
            recv_sem=recv_sems.at[k - 1], device_id=_peer(k), device_id_type=MESH_ID)

    mine = pltpu.make_async_copy(piece(me), out_ref.at[me], local_sem)
    arrivals = {k: remote(k, me ^ k) for k in range(1, N_DEV)}
    if not gather:
        sends = [remote(k, me) for k in range(1, N_DEV)]

        def start():
            mine.start()
            for cp in sends:
                cp.start()

        def wait():
            for cp in arrivals.values():
                cp.wait_recv()
            for cp in sends:
                cp.wait_send()
            mine.wait()

        return start, wait

    far = (2, 4, 6)
    sends = [remote(k, me) for k in (1,) + far]
    passed_on = [pltpu.make_async_remote_copy(
        src_ref=out_ref.at[me ^ k], dst_ref=out_ref.at[me ^ k], send_sem=send_sems.at[k],
        recv_sem=recv_sems.at[k], device_id=_peer(1), device_id_type=MESH_ID) for k in far]

    def start():
        mine.start()
        for cp in sends:
            cp.start()

    def wait():
        for k, cp in zip(far, passed_on):
            arrivals[k].wait_recv()
            cp.start()
        for k in (1, 3, 5, 7):
            arrivals[k].wait_recv()
        for cp in sends + passed_on:
            cp.wait_send()
        mine.wait()

    return start, wait


def _pad_rows(flat, mult):
    n = flat.shape[-1]
    rows = -(-n // (LANES * mult)) * mult
    pad = rows * LANES - n
    flat = jnp.pad(flat, [(0, 0)] * (flat.ndim - 1) + [(0, pad)])
    return flat.reshape(flat.shape[:-1] + (rows, LANES))


MATMUL_VMEM_BUDGET = 48 * 1024 * 1024
MATMUL_TM = (1024, 1408, 512, 256, 128)
MATMUL_TN = (3200, 3072, 2816, 2048, 1536, 1408, 1280, 1024, 896, 512, 384, 256, 128)
MATMUL_TK = (3200, 3072, 2816, 2048, 1408, 1280, 1024, 896)
MATMUL_MIN_TK = 896


def _pick(n, cands):
    for c in cands:
        if n % c == 0:
            return c
    return n


def _matmul_tiles(m, n, k, a_bytes, b_bytes, out_bytes, has_res):
    tm = _pick(m, MATMUL_TM)

    def need(tn, tk):
        blocks = 2 * (tm * tk * a_bytes + tk * tn * b_bytes + tm * tn * out_bytes + has_res * tm * tn * 4)
        temps = tm * tn * 4 * (1 + (tk < k)) + (a_bytes > 2) * tm * tk * 2 + (b_bytes > 2) * tk * tn * 2
        return blocks + temps

    for tn in [c for c in MATMUL_TN if n % c == 0] + [n]:
        for tk in [k] + [c for c in MATMUL_TK if k % c == 0 and MATMUL_MIN_TK <= c < k]:
            if need(tn, tk) <= MATMUL_VMEM_BUDGET:
                return tm, tn, tk
    raise ValueError(f"no matmul tiling fits VMEM for {(m, n, k)}")


def _matmul(a, b, ta, tb, out_dtype, name, rider=None, res=None, rope=None):
    (m, k) = (a.shape[1], a.shape[0]) if ta else a.shape
    (k2, n) = (b.shape[1], b.shape[0]) if tb else b.shape
    assert k == k2, (a.shape, b.shape, ta, tb)
    tm, tn, tk = _matmul_tiles(m, n, k, a.dtype.itemsize, b.dtype.itemsize, jnp.dtype(out_dtype).itemsize,
                               res is not None)
    nk = k // tk
    dims = (((0 if ta else 1,), (1 if tb else 0,)), ((), ()))

    grid = (n // tn, m // tm, nk)

    n_in = 2 + (res is not None) + (rope is not None) + (rider is not None)
    if rope is not None:
        assert res is None and tn == n == QKV_GROUP, (tn, n)

    def body(*refs):
        ins, rest = refs[:n_in], refs[n_in:]
        a_ref, b_ref = ins[:2]
        res_ref = ins[2] if res is not None else None
        rope_ref = ins[2] if rope is not None else None
        o_ref, scratch = rest[0], rest[1:]
        if rider is not None:
            start, wait = _exchange_copies(ins[-1], rest[1], *scratch[-3:], rider[1])
            scratch = scratch[1:-3]
            at = [pl.program_id(ax) for ax in range(3)]
            pl.when(functools.reduce(jnp.logical_and, [p == 0 for p in at]))(start)
        part = lax.dot_general(a_ref[...].astype(BF16), b_ref[...].astype(BF16), dims,
                               preferred_element_type=F32)

        def finish(total):
            if rope_ref is not None:
                for c0 in range(0, tn, ATTN_HEAD_DIM):
                    head = total[:, c0:c0 + ATTN_HEAD_DIM]
                    if c0 < 2 * ATTN_OUT_WIDTH:
                        head = _rope(head, rope_ref[...], 1.0)
                    o_ref[:, c0:c0 + ATTN_HEAD_DIM] = head.astype(o_ref.dtype)
                return
            if res_ref is not None:
                total = total + res_ref[...]
            o_ref[...] = total.astype(o_ref.dtype)

        if nk == 1:
            finish(part)
        else:
            acc_ref, = scratch
            kk = pl.program_id(2)

            @pl.when(kk == 0)
            def _():
                acc_ref[...] = part

            @pl.when(kk > 0)
            def _():
                acc_ref[...] += part

            @pl.when(kk == nk - 1)
            def _():
                finish(acc_ref[...])
        if rider is not None:
            pl.when(functools.reduce(jnp.logical_and, [p == g - 1 for p, g in zip(at, grid)]))(wait)

    a_spec = (pl.BlockSpec((tk, tm), lambda j, i, kk: (kk, i)) if ta
              else pl.BlockSpec((tm, tk), lambda j, i, kk: (i, kk)))
    b_spec = (pl.BlockSpec((tn, tk), lambda j, i, kk: (j, kk)) if tb
              else pl.BlockSpec((tk, tn), lambda j, i, kk: (kk, j)))
    any_spec = pl.BlockSpec(memory_space=pl.ANY)
    tile_spec = pl.BlockSpec((tm, tn), lambda j, i, kk: (i, j))
    in_specs, operands = [a_spec, b_spec], [a, b]
    out_shape, out_specs = [jax.ShapeDtypeStruct((m, n), out_dtype)], [tile_spec]
    scratch = [] if nk == 1 else [pltpu.VMEM((tm, tn), F32)]
    sem = ("parallel", "parallel", "arbitrary")
    if res is not None:
        in_specs.append(tile_spec)
        operands.append(res)
    if rope is not None:
        in_specs.append(pl.BlockSpec((tm, rope.shape[1]), lambda j, i, kk: (i, 0)))
        operands.append(rope)
    if rider is not None:
        in_specs.append(any_spec)
        operands.append(rider[0])
        out_shape.append(_exchange_out(*rider))
        out_specs.append(any_spec)
        scratch += list(EXCHANGE_SEMS)
        sem = ("arbitrary",) * 3
    outs = pl.pallas_call(
        body, name=name, out_shape=out_shape, grid=grid, in_specs=in_specs, out_specs=out_specs,
        scratch_shapes=scratch, compiler_params=_params(*sem),
    )(*operands)
    return outs[0] if rider is None else tuple(outs)


@functools.partial(jax.custom_vjp, nondiff_argnums=(3, 4))
def linear(a, w, res, out_dtype, name):
    return _matmul(a, w, False, False, out_dtype, name + "_fwd", res=res)


def _linear_fwd(a, w, res, out_dtype, name):
    return _matmul(a, w, False, False, out_dtype, name + "_fwd", res=res), (a, w, res is not None)


def _linear_bwd(out_dtype, name, saved, dy):
    a, w, has_res = saved
    da = _matmul(dy, w, False, True, a.dtype, name + "_da")
    dw = _matmul(a, dy, True, False, w.dtype, name + "_dw")
    return da, dw, (dy if has_res else None)


linear.defvjp(_linear_fwd, _linear_bwd)


def _sum_pieces(pieces, name):
    shape = pieces.shape[1:]
    c = shape[-1]
    r = math.prod(shape[:-1])
    tr = _pick(r, (512, 256, 128, 64, 32, 16, 8))

    def body(p_ref, o_ref):
        g = p_ref[0].astype(F32)
        for j in range(1, N_DEV):
            g = g + p_ref[j].astype(F32)
        o_ref[...] = g

    return pl.pallas_call(
        body, name=name, out_shape=jax.ShapeDtypeStruct((r, c), F32), grid=(r // tr,),
        in_specs=[pl.BlockSpec((N_DEV, tr, c), lambda i: (0, i, 0))],
        out_specs=pl.BlockSpec((tr, c), lambda i: (i, 0)),
        compiler_params=_params("parallel"),
    )(pieces.reshape(N_DEV, r, c)).reshape(shape)


@functools.partial(jax.custom_vjp, nondiff_argnums=(5, 6))
def linear_x(a, w, res, rope, shard, out_dtype, name):
    return _matmul(a, w, False, False, out_dtype, name + "_fwd", rider=(shard.astype(BF16), True), res=res,
                   rope=rope)


def _linear_x_fwd(a, w, res, rope, shard, out_dtype, name):
    out = _matmul(a, w, False, False, out_dtype, name + "_fwd", rider=(shard.astype(BF16), True), res=res,
                  rope=rope)
    return out, (a, w, res is not None, rope)


def _linear_x_bwd(out_dtype, name, saved, cts):
    a, w, has_res, rope = saved
    dy, dparts = cts
    half = dparts.shape[1] // 2
    da, moved_a = _matmul(dy, w, False, True, a.dtype, name + "_da", rider=(dparts[:, :half], False))
    dw, moved_b = _matmul(a, dy, True, False, w.dtype, name + "_dw", rider=(dparts[:, half:], False))
    dshard = jnp.concatenate([_sum_pieces(moved_a, name + "_sum_a"), _sum_pieces(moved_b, name + "_sum_b")])
    return da, dw, (dy if has_res else None), (None if rope is None else jnp.zeros_like(rope)), dshard


linear_x.defvjp(_linear_x_fwd, _linear_x_bwd)

FIRST_ROW_SPLIT = (0, 384, 704, 1024)


def _qkv_first_fwd(hs, shard, ropes, carried, name):
    parts = _exchange(shard.astype(BF16), True, name + "_gather")
    wfull = _join8(parts, 1)
    ys, moved = [], []
    for g, (h, rope, carry) in enumerate(zip(hs, ropes, carried)):
        y, m = _matmul(h, wfull[:, g * QKV_GROUP:(g + 1) * QKV_GROUP], False, False, BF16, f"{name}{g}_fwd",
                       rider=(carry.astype(BF16), True), rope=rope)
        ys.append(y)
        moved.append(m)
    return (tuple(ys), tuple(moved)), (hs, wfull, ropes)


@functools.partial(jax.custom_vjp, nondiff_argnums=(4,))
def qkv_first(hs, shard, ropes, carried, name):
    return _qkv_first_fwd(hs, shard, ropes, carried, name)[0]


def _qkv_first_vjp_fwd(hs, shard, ropes, carried, name):
    return _qkv_first_fwd(hs, shard, ropes, carried, name)


def _qkv_first_vjp_bwd(name, saved, cts):
    hs, wfull, ropes = saved
    dys, dmoved = cts
    dws, dcarried = [], []
    for g in range(len(hs)):
        dw, back = _matmul(hs[g], dys[g], True, False, BF16, f"{name}{g}_dw", rider=(dmoved[g], False))
        dws.append(dw)
        dcarried.append(_sum_pieces(back, f"{name}{g}_carried_sum"))
    w_shape = wfull.shape
    dparts = jnp.moveaxis(jnp.concatenate(dws, axis=1).reshape(w_shape[0], N_DEV, w_shape[1] // N_DEV), 1, 0)
    das, dshard = [], []
    for g in range(len(hs)):
        rows = slice(FIRST_ROW_SPLIT[g], FIRST_ROW_SPLIT[g + 1])
        da, got = _matmul(dys[g], wfull[:, g * QKV_GROUP:(g + 1) * QKV_GROUP], False, True, BF16, f"{name}{g}_da",
                          rider=(dparts[:, rows], False))
        das.append(da)
        dshard.append(_sum_pieces(got, f"{name}{g}_own_sum"))
    return tuple(das), jnp.concatenate(dshard), tuple(jnp.zeros_like(r) for r in ropes), tuple(dcarried)


qkv_first.defvjp(_qkv_first_vjp_fwd, _qkv_first_vjp_bwd)


VMEM_LIMIT = 56 * 1024 * 1024
SUBLANES = 8


def _params(*sem):
    return pltpu.CompilerParams(dimension_semantics=sem, vmem_limit_bytes=VMEM_LIMIT)


def _sigmoid(x):
    return 0.5 * jnp.tanh(0.5 * x) + 0.5


def _rstd(xv):
    return lax.rsqrt(jnp.mean(xv * xv, axis=-1, keepdims=True) + NORM_EPS)


def _accumulate(ref, part, first):
    @pl.when(first)
    def _():
        ref[...] = part

    @pl.when(jnp.logical_not(first))
    def _():
        ref[...] += part


def _norm_fwd(x, w, name):
    s, d = x.shape
    tm = _pick(s, (1024, 512, 256, 128))

    def body(x_ref, w_ref, h_ref):
        xv = x_ref[...]
        h_ref[...] = (xv * _rstd(xv) * w_ref[...]).astype(h_ref.dtype)

    return pl.pallas_call(
        body, name=name, out_shape=jax.ShapeDtypeStruct((s, d), BF16), grid=(s // tm,),
        in_specs=[pl.BlockSpec((tm, d), lambda i: (i, 0)), pl.BlockSpec((1, d), lambda i: (0, 0))],
        out_specs=pl.BlockSpec((tm, d), lambda i: (i, 0)), compiler_params=_params("parallel"),
    )(x, w.reshape(1, d))


def _norm_bwd(x, w, dh, dskip, name):
    s, d = x.shape
    tm = _pick(s, (1024, 512, 256, 128))

    def body(x_ref, w_ref, dh_ref, ds_ref, dx_ref, dw_ref):
        xv = x_ref[...]
        r = _rstd(xv)
        y = xv * r
        dhv = dh_ref[...].astype(F32)
        dy = dhv * w_ref[...]
        dx_ref[...] = ds_ref[...] + r * (dy - y * jnp.mean(dy * y, axis=-1, keepdims=True))
        _accumulate(dw_ref, jnp.sum(dhv * y, axis=0, keepdims=True), pl.program_id(0) == 0)

    row = pl.BlockSpec((tm, d), lambda i: (i, 0))
    vec = pl.BlockSpec((1, d), lambda i: (0, 0))
    dx, dw = pl.pallas_call(
        body, name=name,
        out_shape=[jax.ShapeDtypeStruct((s, d), F32), jax.ShapeDtypeStruct((1, d), F32)], grid=(s // tm,),
        in_specs=[row, vec, row, row], out_specs=[row, vec], compiler_params=_params("arbitrary"),
    )(x, w.reshape(1, d), dh, dskip)
    return dx, dw.reshape(d)


@functools.partial(jax.custom_vjp, nondiff_argnums=(2,))
def norm(x, w, name):
    return x, _norm_fwd(x, w, name + "_fwd")


def _norm_vjp_fwd(x, w, name):
    return (x, _norm_fwd(x, w, name + "_fwd")), (x, w)


def _norm_vjp_bwd(name, saved, cts):
    x, w = saved
    dskip, dh = cts
    return _norm_bwd(x, w, dh, dskip, name + "_bwd")


norm.defvjp(_norm_vjp_fwd, _norm_vjp_bwd)


def loss_head(x, w, target, name):
    s, d = x.shape
    tm = _pick(s, (1024, 512, 256, 128))

    def body(x_ref, w_ref, t_ref, loss_ref, dx_ref, dw_ref):
        first = pl.program_id(0) == 0
        xv = x_ref[...]
        r = _rstd(xv)
        y = xv * r
        err = y * w_ref[...] - t_ref[...]
        part = 0.5 * jnp.sum(jnp.sum(err * err, axis=-1, keepdims=True), axis=0, keepdims=True) / d
        _accumulate(loss_ref, jnp.broadcast_to(part, loss_ref.shape), first)
        dout = err / d
        dy = dout * w_ref[...]
        dx_ref[...] = r * (dy - y * jnp.mean(dy * y, axis=-1, keepdims=True))
        _accumulate(dw_ref, jnp.sum(dout * y, axis=0, keepdims=True), first)

    row = pl.BlockSpec((tm, d), lambda i: (i, 0))
    vec = pl.BlockSpec((1, d), lambda i: (0, 0))
    loss, dx, dw = pl.pallas_call(
        body, name=name,
        out_shape=[jax.ShapeDtypeStruct((1, 128), F32), jax.ShapeDtypeStruct((s, d), F32),
                   jax.ShapeDtypeStruct((1, d), F32)],
        grid=(s // tm,), in_specs=[row, vec, row],
        out_specs=[pl.BlockSpec((1, 128), lambda i: (0, 0)), row, vec],
        compiler_params=_params("arbitrary"),
    )(x, w.reshape(1, d), target)
    return loss[0, 0], dx, dw.reshape(d)


FFN_ROWS = 256
FFN_COLS = 128


def _shift_down(cur, halo, k):
    out = pltpu.roll(cur, k, axis=0)
    row = lax.broadcasted_iota(jnp.int32, halo.shape, 0)
    top = out[0:SUBLANES]
    for j in range(k):
        top = jnp.where(row == j, halo[SUBLANES - k + j:SUBLANES - k + j + 1, :], top)
    return jnp.concatenate([top, out[SUBLANES:]], axis=0)


def _shift_up(cur, nxt, k):
    n = cur.shape[0]
    out = pltpu.roll(cur, n - k, axis=0)
    row = lax.broadcasted_iota(jnp.int32, nxt.shape, 0)
    bottom = out[n - SUBLANES:]
    for j in range(k):
        bottom = jnp.where(row == SUBLANES - k + j, nxt[j:j + 1, :], bottom)
    return jnp.concatenate([out[:n - SUBLANES], bottom], axis=0)


def _conv_taps(cur, halo, ntaps):
    return [_shift_down(cur, halo, ntaps - 1 - k) if k < ntaps - 1 else cur for k in range(ntaps)]


def _pad_taps(conv_w):
    return jnp.pad(conv_w, ((0, SUBLANES - conv_w.shape[0]), (0, 0)))


def _ffn_mid_fwd(u0, conv_w, conv_b, name):
    s, width = u0.shape
    half = width // 2
    tm = _pick(s, (FFN_ROWS, 128))
    per = tm // SUBLANES

    def body(w_ref, b_ref, u_ref, halo_ref, a_ref, pre_ref):
        keep = pl.program_id(0) > 0
        for c0 in range(0, half, FFN_COLS):
            vals = []
            for base in (c0, half + c0):
                cols = slice(base, base + FFN_COLS)
                halo = jnp.where(keep, halo_ref[:, cols], 0.0)
                taps = _conv_taps(u_ref[:, cols], halo, 3)
                vals.append(sum(w_ref[k:k + 1, cols] * taps[k] for k in range(3)) + b_ref[:, cols])
                pre_ref[:, cols] = vals[-1].astype(pre_ref.dtype)
            gate, up = vals
            a_ref[:, c0:c0 + FFN_COLS] = (gate * _sigmoid(gate) * up).astype(a_ref.dtype)

    return pl.pallas_call(
        body, name=name,
        out_shape=[jax.ShapeDtypeStruct((s, half), BF16), jax.ShapeDtypeStruct((s, width), BF16)], grid=(s // tm,),
        in_specs=[pl.BlockSpec((SUBLANES, width), lambda i: (0, 0)), pl.BlockSpec((1, width), lambda i: (0, 0)),
                  pl.BlockSpec((tm, width), lambda i: (i, 0)),
                  pl.BlockSpec((SUBLANES, width), lambda i: (jnp.maximum(i * per - 1, 0), 0))],
        out_specs=[pl.BlockSpec((tm, half), lambda i: (i, 0)), pl.BlockSpec((tm, width), lambda i: (i, 0))],
        compiler_params=_params("parallel"),
    )(_pad_taps(conv_w), conv_b.reshape(1, width), u0, u0)


def _ffn_mid_bwd(u0, pre, conv_w, da, name):
    s, width = u0.shape
    half = width // 2
    tm = _pick(s, (FFN_ROWS, 128))
    nt = s // tm

    def body(w_ref, u_ref, pre_ref, da_ref, du0_ref, dw_ref, db_ref, carry_ref):
        first = pl.program_id(0) == 0
        for c0 in range(0, half, FFN_COLS):
            dav = da_ref[:, c0:c0 + FFN_COLS].astype(F32)
            gate = pre_ref[:, c0:c0 + FFN_COLS].astype(F32)
            up = pre_ref[:, half + c0:half + c0 + FFN_COLS].astype(F32)
            sig = _sigmoid(gate)
            scaled = dav * sig
            dus = [scaled * up * (1.0 + gate * (1.0 - sig)), scaled * gate]
            for base, du in zip((c0, half + c0), dus):
                cols = slice(base, base + FFN_COLS)
                nxt = jnp.where(first, 0.0, carry_ref[:, cols])
                ahead = [_shift_up(du, nxt, 2), _shift_up(du, nxt, 1), du]
                du0 = sum(w_ref[k:k + 1, cols] * ahead[k] for k in range(3))
                du0_ref[:, cols] = du0.astype(du0_ref.dtype)
                carry_ref[:, cols] = du[0:SUBLANES, :]
                x = u_ref[:, cols]
                dwp = jnp.concatenate([jnp.sum(ahead[k] * x, axis=0, keepdims=True) for k in range(3)]
                                      + [jnp.zeros((SUBLANES - 3, FFN_COLS), F32)], axis=0)
                dbp = jnp.sum(du, axis=0, keepdims=True)

                @pl.when(first)
                def _():
                    dw_ref[:, cols] = dwp
                    db_ref[:, cols] = dbp

                @pl.when(jnp.logical_not(first))
                def _():
                    dw_ref[:, cols] += dwp
                    db_ref[:, cols] += dbp

    rev = lambda i: nt - 1 - i
    du0, dw, db = pl.pallas_call(
        body, name=name,
        out_shape=[jax.ShapeDtypeStruct((s, width), BF16), jax.ShapeDtypeStruct((SUBLANES, width), F32),
                   jax.ShapeDtypeStruct((1, width), F32)],
        grid=(nt,),
        in_specs=[pl.BlockSpec((SUBLANES, width), lambda i: (0, 0)),
                  pl.BlockSpec((tm, width), lambda i: (rev(i), 0)), pl.BlockSpec((tm, width), lambda i: (rev(i), 0)),
                  pl.BlockSpec((tm, half), lambda i: (rev(i), 0))],
        out_specs=[pl.BlockSpec((tm, width), lambda i: (rev(i), 0)),
                   pl.BlockSpec((SUBLANES, width), lambda i: (0, 0)), pl.BlockSpec((1, width), lambda i: (0, 0))],
        scratch_shapes=[pltpu.VMEM((SUBLANES, width), F32)],
        compiler_params=_params("arbitrary"),
    )(_pad_taps(conv_w), u0, pre, da)
    return du0, dw[:3], db.reshape(width)


@functools.partial(jax.custom_vjp, nondiff_argnums=(3,))
def ffn_mid(u0, conv_w, conv_b, name):
    return _ffn_mid_fwd(u0, conv_w, conv_b, name + "_fwd")[0]


def _ffn_mid_vjp_fwd(u0, conv_w, conv_b, name):
    a, pre = _ffn_mid_fwd(u0, conv_w, conv_b, name + "_fwd")
    return a, (u0, pre, conv_w)


def _ffn_mid_vjp_bwd(name, res, da):
    u0, pre, conv_w = res
    return _ffn_mid_bwd(u0, pre, conv_w, da, name + "_bwd")


ffn_mid.defvjp(_ffn_mid_vjp_fwd, _ffn_mid_vjp_bwd)


NEG = -1e30
HEAD_SLICES = [slice(hh * ATTN_HEAD_DIM, (hh + 1) * ATTN_HEAD_DIM) for hh in range(ATTN_HEADS)]
ATTN_SCALE = ATTN_HEAD_DIM ** -0.5
QKV_GROUP = 3 * ATTN_OUT_WIDTH


def rope_table(seq, d):
    pos = (jnp.arange(seq // d, dtype=jnp.int32)[None, :] * d + jnp.arange(d, dtype=jnp.int32)[:, None])
    inv_freq = ROPE_THETA ** (-jnp.arange(0, ROPE_DIM, 2, dtype=F32) / ROPE_DIM)
    ang = pos.reshape(seq).astype(F32)[:, None] * inv_freq[None, :]
    cos, sin = jnp.cos(ang), jnp.sin(ang)
    half = ROPE_DIM // 2
    ones = jnp.ones((seq, ATTN_HEAD_DIM - ROPE_DIM), F32)
    zero = lambda n: jnp.zeros((seq, n), F32)
    return jnp.concatenate([cos, cos, ones, -sin, zero(ATTN_HEAD_DIM - half),
                            zero(half), sin, zero(ATTN_HEAD_DIM - ROPE_DIM)], axis=1)


def _rope(t, tab, sign):
    half = ROPE_DIM // 2
    return t * tab[:, 0:128] + sign * (pltpu.roll(t, ATTN_HEAD_DIM - half, axis=1) * tab[:, 128:256]
                                       + pltpu.roll(t, half, axis=1) * tab[:, 256:384])


def _to_dilated(a, d):
    s = a.shape[0]
    return a if d == 1 else a.reshape(s // d, d, -1).transpose(1, 0, 2).reshape(s, -1)


def _from_dilated(a, d):
    s = a.shape[0]
    return a if d == 1 else a.reshape(d, s // d, -1).transpose(1, 0, 2).reshape(s, -1)


def _dot_nt(a, b):
    return lax.dot_general(a, b, (((1,), (1,)), ((), ())), preferred_element_type=F32)


def _dot_tn(a, b):
    return lax.dot_general(a, b, (((0,), (0,)), ((), ())), preferred_element_type=F32)


def _dot(a, b):
    return jnp.dot(a, b, preferred_element_type=F32)


def _window_mask(has_prev):
    ii = lax.broadcasted_iota(jnp.int32, (ATTN_BLOCK, 2 * ATTN_BLOCK), 0)
    jj = lax.broadcasted_iota(jnp.int32, (ATTN_BLOCK, 2 * ATTN_BLOCK), 1)
    in_window = jnp.logical_and(jj >= ii, jj <= ii + ATTN_BLOCK)
    return jnp.logical_and(in_window, jnp.logical_or(jj >= ATTN_BLOCK, has_prev))


def _both(prev_ref, cur_ref, hs):
    return jnp.concatenate([prev_ref[:, hs], cur_ref[:, hs]], axis=0)


def _attn_group_fwd(qkv, d, name):
    s = qkv.shape[0]
    nb = s // d // ATTN_BLOCK

    def body(q_ref, kc_ref, kp_ref, vc_ref, vp_ref, o_ref, lse_ref):
        mask = _window_mask(pl.program_id(1) > 0)
        lane = lax.broadcasted_iota(jnp.int32, (ATTN_BLOCK, 128), 1)
        lse_tile = jnp.zeros((ATTN_BLOCK, 128), F32)
        for hh, hs in enumerate(HEAD_SLICES):
            sc = jnp.where(mask, _dot_nt(q_ref[:, hs], _both(kp_ref, kc_ref, hs)) * ATTN_SCALE, NEG)
            m = jnp.max(sc, axis=1, keepdims=True)
            p = jnp.exp(sc - m)
            den = jnp.sum(p, axis=1, keepdims=True)
            o_ref[:, hs] = (_dot(p.astype(BF16), _both(vp_ref, vc_ref, hs)) / den).astype(o_ref.dtype)
            lse_tile = jnp.where(lane == hh, m + jnp.log(den), lse_tile)
        lse_ref[...] = lse_tile

    cur = lambda t: pl.BlockSpec((ATTN_BLOCK, ATTN_OUT_WIDTH), lambda r, n: (r * nb + n, t))
    prv = lambda t: pl.BlockSpec((ATTN_BLOCK, ATTN_OUT_WIDTH), lambda r, n: (r * nb + jnp.maximum(n - 1, 0), t))
    return pl.pallas_call(
        body, name=name,
        out_shape=[jax.ShapeDtypeStruct((s, ATTN_OUT_WIDTH), BF16), jax.ShapeDtypeStruct((s, 128), F32)],
        grid=(d, nb), in_specs=[cur(0), cur(1), prv(1), cur(2), prv(2)],
        out_specs=[pl.BlockSpec((ATTN_BLOCK, ATTN_OUT_WIDTH), lambda r, n: (r * nb + n, 0)),
                   pl.BlockSpec((ATTN_BLOCK, 128), lambda r, n: (r * nb + n, 0))],
        compiler_params=_params("parallel", "parallel"),
    )(qkv, qkv, qkv, qkv, qkv)


def _attn_combine(os_, lses, name):
    s = os_[0].shape[0]
    tm = _pick(s, (512, 256, 128))
    ng = len(os_)

    def body(*refs):
        o_refs, l_refs, (o_ref, lse_ref) = refs[:ng], refs[ng:2 * ng], refs[2 * ng:]
        lane = lax.broadcasted_iota(jnp.int32, (tm, 128), 1)
        lse_tile = jnp.zeros((tm, 128), F32)
        for hh, hs in enumerate(HEAD_SLICES):
            ls = [l_ref[:, hh:hh + 1] for l_ref in l_refs]
            m = functools.reduce(jnp.maximum, ls)
            ws = [jnp.exp(l - m) for l in ls]
            tot = functools.reduce(lambda a, b: a + b, ws)
            acc = functools.reduce(lambda a, b: a + b, [o_r[:, hs].astype(F32) * w for o_r, w in zip(o_refs, ws)])
            o_ref[:, hs] = (acc / tot).astype(o_ref.dtype)
            lse_tile = jnp.where(lane == hh, m + jnp.log(tot), lse_tile)
        lse_ref[...] = lse_tile

    wide = pl.BlockSpec((tm, ATTN_OUT_WIDTH), lambda i: (i, 0))
    thin = pl.BlockSpec((tm, 128), lambda i: (i, 0))
    return pl.pallas_call(
        body, name=name,
        out_shape=[jax.ShapeDtypeStruct((s, ATTN_OUT_WIDTH), BF16), jax.ShapeDtypeStruct((s, 128), F32)],
        grid=(s // tm,), in_specs=[wide] * ng + [thin] * ng, out_specs=[wide, thin],
        compiler_params=_params("parallel"),
    )(*os_, *lses)


def _attn_delta(do, o, name):
    s = do.shape[0]
    tm = _pick(s, (512, 256, 128))

    def body(do_ref, o_ref, out_ref):
        lane = lax.broadcasted_iota(jnp.int32, (tm, 128), 1)
        tile = jnp.zeros((tm, 128), F32)
        for hh, hs in enumerate(HEAD_SLICES):
            prod = do_ref[:, hs].astype(F32) * o_ref[:, hs].astype(F32)
            tile = jnp.where(lane == hh, jnp.sum(prod, axis=1, keepdims=True), tile)
        out_ref[...] = tile

    wide = pl.BlockSpec((tm, ATTN_OUT_WIDTH), lambda i: (i, 0))
    return pl.pallas_call(
        body, name=name, out_shape=jax.ShapeDtypeStruct((s, 128), F32), grid=(s // tm,),
        in_specs=[wide, wide], out_specs=pl.BlockSpec((tm, 128), lambda i: (i, 0)),
        compiler_params=_params("parallel"),
    )(do, o)


def _attn_group_bwd(qkv, do, lse, delta, tab, d, name):
    s = qkv.shape[0]
    nb = s // d // ATTN_BLOCK
    wide = ATTN_OUT_WIDTH

    def body(q_ref, qn_ref, k_ref, v_ref, do_ref, don_ref, lse_ref, lsen_ref, dl_ref, dln_ref, tab_ref,
             out_ref, carry_ref):
        n = pl.program_id(1)

        @pl.when(n == 0)
        def _():
            carry_ref[...] = jnp.zeros_like(carry_ref)

        rows = lax.broadcasted_iota(jnp.int32, (2 * ATTN_BLOCK, ATTN_BLOCK), 0)
        keys = lax.broadcasted_iota(jnp.int32, (2 * ATTN_BLOCK, ATTN_BLOCK), 1)
        own = jnp.logical_and(rows < ATTN_BLOCK, keys <= rows)
        nxt = jnp.logical_and(rows >= ATTN_BLOCK, jnp.logical_and(keys >= rows - ATTN_BLOCK, n + 1 < nb))
        mask = jnp.logical_or(own, nxt)
        both = lambda a_ref, b_ref, cols: jnp.concatenate([a_ref[:, cols], b_ref[:, cols]], axis=0)
        for hh, hs in enumerate(HEAD_SLICES):
            one = slice(hh, hh + 1)
            q2, do2 = both(q_ref, qn_ref, hs), both(do_ref, don_ref, hs)
            kh, vh = k_ref[:, hs], v_ref[:, hs]
            sc = jnp.where(mask, _dot_nt(q2, kh) * ATTN_SCALE, NEG)
            p = jnp.exp(sc - both(lse_ref, lsen_ref, one))
            ds = (p * (_dot_nt(do2, vh) - both(dl_ref, dln_ref, one)) * ATTN_SCALE).astype(BF16)
            dq2 = _dot(ds, kh)
            dq = carry_ref[:, hs] + dq2[:ATTN_BLOCK]
            carry_ref[:, hs] = dq2[ATTN_BLOCK:]
            out_ref[:, hs] = _rope(dq, tab_ref[...], -1.0).astype(out_ref.dtype)
            out_ref[:, wide + hh * ATTN_HEAD_DIM:wide + (hh + 1) * ATTN_HEAD_DIM] = _rope(
                _dot_tn(ds, q2), tab_ref[...], -1.0).astype(out_ref.dtype)
            out_ref[:, 2 * wide + hh * ATTN_HEAD_DIM:2 * wide + (hh + 1) * ATTN_HEAD_DIM] = _dot_tn(
                p.astype(BF16), do2).astype(out_ref.dtype)

    def spec(width, row, col):
        return pl.BlockSpec((ATTN_BLOCK, width), lambda r, n: (r * nb + row(n), col))

    cur = lambda n: n
    nxt_block = lambda n: jnp.minimum(n + 1, nb - 1)
    return pl.pallas_call(
        body, name=name, out_shape=jax.ShapeDtypeStruct((s, QKV_GROUP), BF16), grid=(d, nb),
        in_specs=[spec(wide, cur, 0), spec(wide, nxt_block, 0), spec(wide, cur, 1), spec(wide, cur, 2),
                  spec(wide, cur, 0), spec(wide, nxt_block, 0), spec(128, cur, 0), spec(128, nxt_block, 0),
                  spec(128, cur, 0), spec(128, nxt_block, 0), spec(384, cur, 0)],
        out_specs=spec(QKV_GROUP, cur, 0),
        scratch_shapes=[pltpu.VMEM((ATTN_BLOCK, wide), F32)],
        compiler_params=_params("parallel", "arbitrary"),
    )(qkv, qkv, qkv, qkv, do, do, lse, lse, delta, delta, tab)


def _attn_core_fwd(qkvs, name):
    os_, lses = [], []
    for g, (qkv, d) in enumerate(zip(qkvs, ATTN_DILATIONS)):
        o_g, lse_g = _attn_group_fwd(qkv, d, f"{name}_fwd{g}")
        os_.append(_from_dilated(o_g, d))
        lses.append(_from_dilated(lse_g, d))
    o, lse = _attn_combine(os_, lses, name + "_combine")
    return o, (tuple(qkvs), o, lse)


@functools.partial(jax.custom_vjp, nondiff_argnums=(1,))
def attn_core(qkvs, name):
    return _attn_core_fwd(qkvs, name)[0]


def _attn_core_vjp_fwd(qkvs, name):
    return _attn_core_fwd(qkvs, name)


def _attn_core_vjp_bwd(name, res, do):
    rot, o, lse = res
    delta = _attn_delta(do, o, name + "_delta")
    out = []
    for g, (qkv, d) in enumerate(zip(rot, ATTN_DILATIONS)):
        out.append(_attn_group_bwd(qkv, _to_dilated(do, d), _to_dilated(lse, d), _to_dilated(delta, d),
                                   rope_table(o.shape[0], d), d, f"{name}_bwd{g}"))
    return (tuple(out),)


attn_core.defvjp(_attn_core_vjp_fwd, _attn_core_vjp_bwd)


def attention_mixer_p(x, h, lin, j, carries, tag):
    qkvs = tuple(lin(_to_dilated(h, d), ('attn_w_qkv', j), BF16, f"{tag}_qkv{g}",
                     cols=slice(g * QKV_GROUP, (g + 1) * QKV_GROUP), carry=carries[g],
                     rope=rope_table(h.shape[0], d))
                 for g, d in enumerate(ATTN_DILATIONS))
    return lin(attn_core(qkvs, tag), ('attn_w_o', j), F32, tag + "_o", res=x)


def first_attention_mixer(x, h, lin, w, ready, carries, tag):
    s = h.shape[0]
    qkvs, moved = qkv_first(tuple(_to_dilated(h, d) for d in ATTN_DILATIONS), w['attn_w_qkv'][0],
                            tuple(rope_table(s, d) for d in ATTN_DILATIONS),
                            tuple(w[n][j] for n, j in carries), tag + "_qkv")
    for (n, j), parts in zip(carries, moved):
        ready[(n, j)] = _full_weight(n, parts)
    return lin(attn_core(qkvs, tag), ('attn_w_o', 0), F32, tag + "_o", res=x)


SSM_CONV_TAPS = 4
SSM_COL_BLOCK = 2048
SSM_PAIRS = SSM_HEADS // 2
SSM_DT_BLOCK = (SSM_D_INNER + SSM_CONV_DIM) // 128


def _ssm_conv_fwd(zx, conv_w, conv_b, name):
    s = zx.shape[0]
    tm = _pick(s, (512, 256, 128))
    per = tm // SUBLANES
    ncb = SSM_CONV_DIM // SSM_COL_BLOCK

    def body(w_ref, b_ref, x_ref, halo_ref, o_ref, pre_ref):
        keep = pl.program_id(1) > 0
        for c0 in range(0, SSM_COL_BLOCK, FFN_COLS):
            cols = slice(c0, c0 + FFN_COLS)
            halo = jnp.where(keep, halo_ref[:, cols], 0.0)
            taps = _conv_taps(x_ref[:, cols], halo, SSM_CONV_TAPS)
            pre = sum(w_ref[k:k + 1, cols] * taps[k] for k in range(SSM_CONV_TAPS)) + b_ref[:, cols]
            pre_ref[:, cols] = pre.astype(pre_ref.dtype)
            o_ref[:, cols] = pre * _sigmoid(pre)

    blk = pl.BlockSpec((tm, SSM_COL_BLOCK), lambda j, i: (i, j))
    return pl.pallas_call(
        body, name=name,
        out_shape=[jax.ShapeDtypeStruct((s, SSM_CONV_DIM), F32), jax.ShapeDtypeStruct((s, SSM_CONV_DIM), BF16)],
        grid=(ncb, s // tm),
        in_specs=[pl.BlockSpec((SUBLANES, SSM_COL_BLOCK), lambda j, i: (0, j)),
                  pl.BlockSpec((1, SSM_COL_BLOCK), lambda j, i: (0, j)),
                  pl.BlockSpec((tm, SSM_COL_BLOCK), lambda j, i: (i, j + 1)),
                  pl.BlockSpec((SUBLANES, SSM_COL_BLOCK), lambda j, i: (jnp.maximum(i * per - 1, 0), j + 1))],
        out_specs=[blk, blk], compiler_params=_params("parallel", "parallel"),
    )(_pad_taps(conv_w), conv_b.reshape(1, SSM_CONV_DIM), zx, zx)


def _ssm_conv_bwd(zx, pre, conv_w, dact, dzx, name):
    s = zx.shape[0]
    tm = _pick(s, (512, 256, 128))
    nt = s // tm
    ncb = SSM_CONV_DIM // SSM_COL_BLOCK
    nk = SSM_CONV_TAPS

    def body(w_ref, x_ref, pre_ref, da_ref, _, dx_ref, dw_ref, db_ref, carry_ref):
        first = pl.program_id(1) == 0
        for c0 in range(0, SSM_COL_BLOCK, FFN_COLS):
            cols = slice(c0, c0 + FFN_COLS)
            prev = pre_ref[:, cols].astype(F32)
            sig = _sigmoid(prev)
            dpre = da_ref[:, cols] * sig * (1.0 + prev * (1.0 - sig))
            nxt = jnp.where(first, 0.0, carry_ref[:, cols])
            ahead = [_shift_up(dpre, nxt, nk - 1 - k) for k in range(nk - 1)] + [dpre]
            dx_ref[:, cols] = sum(w_ref[k:k + 1, cols] * ahead[k] for k in range(nk)).astype(dx_ref.dtype)
            carry_ref[:, cols] = dpre[0:SUBLANES, :]
            x = x_ref[:, cols]
            dwp = jnp.concatenate([jnp.sum(ahead[k] * x, axis=0, keepdims=True) for k in range(nk)]
                                  + [jnp.zeros((SUBLANES - nk, FFN_COLS), F32)], axis=0)
            dbp = jnp.sum(dpre, axis=0, keepdims=True)

            @pl.when(first)
            def _():
                dw_ref[:, cols] = dwp
                db_ref[:, cols] = dbp

            @pl.when(jnp.logical_not(first))
            def _():
                dw_ref[:, cols] += dwp
                db_ref[:, cols] += dbp

    rev = lambda i: nt - 1 - i
    dzx, dw, db = pl.pallas_call(
        body, name=name,
        out_shape=[jax.ShapeDtypeStruct(dzx.shape, dzx.dtype), jax.ShapeDtypeStruct((SUBLANES, SSM_CONV_DIM), F32),
                   jax.ShapeDtypeStruct((1, SSM_CONV_DIM), F32)],
        grid=(ncb, nt),
        in_specs=[pl.BlockSpec((SUBLANES, SSM_COL_BLOCK), lambda j, i: (0, j)),
                  pl.BlockSpec((tm, SSM_COL_BLOCK), lambda j, i: (rev(i), j + 1)),
                  pl.BlockSpec((tm, SSM_COL_BLOCK), lambda j, i: (rev(i), j)),
                  pl.BlockSpec((tm, SSM_COL_BLOCK), lambda j, i: (rev(i), j)),
                  pl.BlockSpec(memory_space=pl.ANY)],
        out_specs=[pl.BlockSpec((tm, SSM_COL_BLOCK), lambda j, i: (rev(i), j + 1)),
                   pl.BlockSpec((SUBLANES, SSM_COL_BLOCK), lambda j, i: (0, j)),
                   pl.BlockSpec((1, SSM_COL_BLOCK), lambda j, i: (0, j))],
        scratch_shapes=[pltpu.VMEM((SUBLANES, SSM_COL_BLOCK), F32)],
        input_output_aliases={4: 0},
        compiler_params=_params("parallel", "arbitrary"),
    )(_pad_taps(conv_w), zx, pre, dact, dzx)
    return dzx, dw[:nk], db.reshape(SSM_CONV_DIM)


def _ssd_chunk(xs, bms, cms, dt_raw, dtb, alog, dsk, states):
    q = SSM_CHUNK
    lane = lax.broadcasted_iota(jnp.int32, (1, 128), 1)
    row = lax.broadcasted_iota(jnp.int32, (q, 1), 0)
    ii = lax.broadcasted_iota(jnp.int32, (q, q), 0)
    jj = lax.broadcasted_iota(jnp.int32, (q, q), 1)
    tril = ii >= jj
    left = lane < SSM_HEAD_DIM
    last_row = (row == q - 1).astype(F32)

    def lanes_of(mat, h):
        pick = (lane == h).astype(F32)
        return jnp.broadcast_to(jnp.sum(mat * pick, axis=1, keepdims=True), mat.shape)

    def rows_of(mat_t, h):
        pick = (row == h).astype(F32)
        return jnp.broadcast_to(jnp.sum(mat_t * pick, axis=0, keepdims=True), mat_t.shape)

    v = dt_raw + dtb
    dt = jnp.maximum(v, 0.0) + jnp.log(1.0 + jnp.exp(-jnp.abs(v)))
    adt = dt * (-jnp.exp(alog))
    acs = jnp.dot(tril.astype(F32), adt, precision=lax.Precision.HIGHEST, preferred_element_type=F32)
    acs_t = acs.T
    ys, new_states = [], []
    for pr in range(SSM_PAIRS):
        g = pr // 2
        if pr % 2 == 0:
            cb = _dot_nt(cms[g].astype(BF16), bms[g].astype(BF16))
        cols = [lanes_of(acs, 2 * pr + e) for e in range(2)]
        dts = [lanes_of(dt, 2 * pr + e) for e in range(2)]
        rws = [rows_of(acs_t, 2 * pr + e) for e in range(2)]
        lasts = [jnp.sum(c * last_row, axis=0, keepdims=True) for c in cols]
        xdt = xs[pr] * jnp.where(left, dts[0], dts[1])
        halves = [jnp.where(left, xdt, 0.0).astype(BF16), jnp.where(left, 0.0, xdt).astype(BF16)]
        y_diag, s_new = 0.0, 0.0
        for e in range(2):
            lmat = jnp.where(tril, jnp.exp(jnp.minimum(cols[e] - rws[e], 0.0)), 0.0)
            y_diag = y_diag + _dot((cb * lmat).astype(BF16), halves[e])
            decay = jnp.exp(lasts[e] - cols[e])
            s_new = s_new + _dot_tn((bms[g] * decay).astype(BF16), halves[e])
        y_off = _dot(cms[g].astype(BF16), states[pr].astype(BF16)) * jnp.where(left, jnp.exp(cols[0]), jnp.exp(cols[1]))
        skip = jnp.where(left, lanes_of(dsk, 2 * pr), lanes_of(dsk, 2 * pr + 1))
        ys.append(y_diag + y_off + xs[pr] * skip)
        new_states.append(states[pr] * jnp.where(left, jnp.exp(lasts[0]), jnp.exp(lasts[1])) + s_new)
    return tuple(ys), tuple(new_states)


def _ssm_vec(v):
    return jnp.pad(v.reshape(1, -1), ((0, 0), (0, 128 - v.shape[0])))


def _ssd_scan_fwd(act, zx, dtb, alog, dsk, name):
    s = act.shape[0]
    nc = s // SSM_CHUNK
    ng = SSM_GROUPS

    def body(act_ref, dt_ref, dtb_ref, alog_ref, dsk_ref, y_ref, st_out_ref, st_ref):
        @pl.when(pl.program_id(0) == 0)
        def _():
            st_ref[...] = jnp.zeros_like(st_ref)

        tile = lambda k: act_ref[:, k * 128:(k + 1) * 128]
        xs = [tile(k) for k in range(SSM_PAIRS)]
        bms = [tile(SSM_PAIRS + k) for k in range(ng)]
        cms = [tile(SSM_PAIRS + ng + k) for k in range(ng)]
        states = [st_ref[k] for k in range(SSM_PAIRS)]
        st_out_ref[0] = st_ref[...]
        ys, new_states = _ssd_chunk(xs, bms, cms, dt_ref[...], dtb_ref[...], alog_ref[...], dsk_ref[...], states)
        for k in range(SSM_PAIRS):
            y_ref[:, k * 128:(k + 1) * 128] = ys[k]
            st_ref[k] = new_states[k]

    vec = pl.BlockSpec((1, 128), lambda c: (0, 0))
    return pl.pallas_call(
        body, name=name,
        out_shape=[jax.ShapeDtypeStruct((s, SSM_D_INNER), F32),
                   jax.ShapeDtypeStruct((nc, SSM_PAIRS, SSM_STATE, 128), F32)],
        grid=(nc,),
        in_specs=[pl.BlockSpec((SSM_CHUNK, SSM_CONV_DIM), lambda c: (c, 0)),
                  pl.BlockSpec((SSM_CHUNK, 128), lambda c: (c, SSM_DT_BLOCK)), vec, vec, vec],
        out_specs=[pl.BlockSpec((SSM_CHUNK, SSM_D_INNER), lambda c: (c, 0)),
                   pl.BlockSpec((1, SSM_PAIRS, SSM_STATE, 128), lambda c: (c, 0, 0, 0))],
        scratch_shapes=[pltpu.VMEM((SSM_PAIRS, SSM_STATE, 128), F32)],
        compiler_params=_params("arbitrary"),
    )(act, zx, _ssm_vec(dtb), _ssm_vec(alog), _ssm_vec(dsk))


def _ssd_scan_bwd(act, zx, dtb, alog, dsk, st_in, dy, dzx, name):
    s = act.shape[0]
    nc = s // SSM_CHUNK
    ng = SSM_GROUPS
    dt_col = SSM_D_INNER + SSM_CONV_DIM
    tail = zx.shape[1] - dt_col
    assert tail % 128 == 0 and dt_col % tail == 0, tail

    def body(act_ref, dt_ref, dtb_ref, alog_ref, dsk_ref, st_ref, dy_ref, _, dact_ref, ddt_ref, dpar_ref, dst_ref):
        first = pl.program_id(0) == 0

        @pl.when(first)
        def _():
            dst_ref[...] = jnp.zeros_like(dst_ref)

        tile = lambda k: act_ref[:, k * 128:(k + 1) * 128]
        xs = [tile(k) for k in range(SSM_PAIRS)]
        bms = [tile(SSM_PAIRS + k) for k in range(ng)]
        cms = [tile(SSM_PAIRS + ng + k) for k in range(ng)]
        states = [st_ref[0, k] for k in range(SSM_PAIRS)]
        _, pullback = jax.vjp(_ssd_chunk, xs, bms, cms, dt_ref[...], dtb_ref[...], alog_ref[...], dsk_ref[...],
                              states)
        dys = tuple(dy_ref[:, k * 128:(k + 1) * 128] for k in range(SSM_PAIRS))
        dsts = tuple(dst_ref[k] for k in range(SSM_PAIRS))
        dxs, dbms, dcms, ddt, ddtb, dalog, ddsk, dstates = pullback((dys, dsts))
        for k, t in enumerate(list(dxs) + list(dbms) + list(dcms)):
            dact_ref[:, k * 128:(k + 1) * 128] = t
        ddt_ref[:, 0:128] = ddt.astype(ddt_ref.dtype)
        if tail > 128:
            ddt_ref[:, 128:] = jnp.zeros((SSM_CHUNK, tail - 128), ddt_ref.dtype)
        for k in range(SSM_PAIRS):
            dst_ref[k] = dstates[k]
        dpar = jnp.concatenate([ddtb, dalog, ddsk, jnp.zeros((SUBLANES - 3, 128), F32)], axis=0)
        _accumulate(dpar_ref, dpar, first)

    rev = lambda c: nc - 1 - c
    vec = pl.BlockSpec((1, 128), lambda c: (0, 0))
    dact, dzx, dpar = pl.pallas_call(
        body, name=name,
        out_shape=[jax.ShapeDtypeStruct((s, SSM_CONV_DIM), F32), jax.ShapeDtypeStruct(dzx.shape, dzx.dtype),
                   jax.ShapeDtypeStruct((SUBLANES, 128), F32)],
        grid=(nc,),
        in_specs=[pl.BlockSpec((SSM_CHUNK, SSM_CONV_DIM), lambda c: (rev(c), 0)),
                  pl.BlockSpec((SSM_CHUNK, 128), lambda c: (rev(c), SSM_DT_BLOCK)), vec, vec, vec,
                  pl.BlockSpec((1, SSM_PAIRS, SSM_STATE, 128), lambda c: (rev(c), 0, 0, 0)),
                  pl.BlockSpec((SSM_CHUNK, SSM_D_INNER), lambda c: (rev(c), 0)),
                  pl.BlockSpec(memory_space=pl.ANY)],
        out_specs=[pl.BlockSpec((SSM_CHUNK, SSM_CONV_DIM), lambda c: (rev(c), 0)),
                   pl.BlockSpec((SSM_CHUNK, tail), lambda c: (rev(c), dt_col // tail)),
                   pl.BlockSpec((SUBLANES, 128), lambda c: (0, 0))],
        scratch_shapes=[pltpu.VMEM((SSM_PAIRS, SSM_STATE, 128), F32)],
        input_output_aliases={7: 1},
        compiler_params=_params("arbitrary"),
    )(act, zx, _ssm_vec(dtb), _ssm_vec(alog), _ssm_vec(dsk), st_in, dy, dzx)
    return dact, dzx, dpar[0, :SSM_HEADS], dpar[1, :SSM_HEADS], dpar[2, :SSM_HEADS]


SSM_NORM_GROUP = SSM_D_INNER // SSM_GROUPS


def _gated_group(y, z, w):
    g = y * (z * _sigmoid(z))
    return g * lax.rsqrt(jnp.mean(g * g, axis=-1, keepdims=True) + NORM_EPS) * w


def _gated_norm_fwd(y, zx, w, name):
    s = y.shape[0]
    tm = _pick(s, (512, 256, 128))

    def body(y_ref, z_ref, w_ref, o_ref):
        for c0 in range(0, SSM_D_INNER, SSM_NORM_GROUP):
            cols = slice(c0, c0 + SSM_NORM_GROUP)
            o_ref[:, cols] = _gated_group(y_ref[:, cols], z_ref[:, cols], w_ref[:, cols]).astype(o_ref.dtype)

    blk = pl.BlockSpec((tm, SSM_D_INNER), lambda i: (i, 0))
    return pl.pallas_call(
        body, name=name, out_shape=jax.ShapeDtypeStruct((s, SSM_D_INNER), BF16), grid=(s // tm,),
        in_specs=[blk, blk, pl.BlockSpec((1, SSM_D_INNER), lambda i: (0, 0))], out_specs=blk,
        compiler_params=_params("parallel"),
    )(y, zx, w.reshape(1, SSM_D_INNER))


def _gated_norm_bwd(y, zx, w, dout, name):
    s = y.shape[0]
    tm = _pick(s, (512, 256, 128))

    def body(y_ref, z_ref, w_ref, do_ref, dy_ref, dz_ref, dw_ref):
        first = pl.program_id(0) == 0
        for c0 in range(0, SSM_D_INNER, SSM_NORM_GROUP):
            cols = slice(c0, c0 + SSM_NORM_GROUP)
            _, pullback = jax.vjp(_gated_group, y_ref[:, cols], z_ref[:, cols], w_ref[:, cols])
            dyv, dzv, dwv = pullback(do_ref[:, cols].astype(F32))
            dy_ref[:, cols] = dyv
            dz_ref[:, cols] = dzv.astype(dz_ref.dtype)

            @pl.when(first)
            def _():
                dw_ref[:, cols] = dwv

            @pl.when(jnp.logical_not(first))
            def _():
                dw_ref[:, cols] += dwv

    blk = pl.BlockSpec((tm, SSM_D_INNER), lambda i: (i, 0))
    vec = pl.BlockSpec((1, SSM_D_INNER), lambda i: (0, 0))
    dy, dz, dw = pl.pallas_call(
        body, name=name,
        out_shape=[jax.ShapeDtypeStruct((s, SSM_D_INNER), F32), jax.ShapeDtypeStruct((s, zx.shape[1]), BF16),
                   jax.ShapeDtypeStruct((1, SSM_D_INNER), F32)],
        grid=(s // tm,), in_specs=[blk, blk, vec, blk], out_specs=[blk, blk, vec],
        compiler_params=_params("arbitrary"),
    )(y, zx, w.reshape(1, SSM_D_INNER), dout)
    return dy, dz, dw.reshape(SSM_D_INNER)


def _ssm_core_fwd(zx, conv_w, conv_b, dtb, alog, dsk, norm_w, name):
    act, pre = _ssm_conv_fwd(zx, conv_w, conv_b, name + "_conv_fwd")
    y, st_in = _ssd_scan_fwd(act, zx, dtb, alog, dsk, name + "_scan_fwd")
    out = _gated_norm_fwd(y, zx, norm_w, name + "_gate_fwd")
    return out, (zx, conv_w, pre, dtb, alog, dsk, norm_w, act, y, st_in)


@functools.partial(jax.custom_vjp, nondiff_argnums=(7,))
def ssm_core(zx, conv_w, conv_b, dtb, alog, dsk, norm_w, name):
    return _ssm_core_fwd(zx, conv_w, conv_b, dtb, alog, dsk, norm_w, name)[0]


def _ssm_core_vjp_fwd(zx, conv_w, conv_b, dtb, alog, dsk, norm_w, name):
    return _ssm_core_fwd(zx, conv_w, conv_b, dtb, alog, dsk, norm_w, name)


def _ssm_core_vjp_bwd(name, res, dout):
    zx, conv_w, pre, dtb, alog, dsk, norm_w, act, y, st_in = res
    dy, dzx, dnorm_w = _gated_norm_bwd(y, zx, norm_w, dout, name + "_gate_bwd")
    dact, dzx, ddtb, dalog, ddsk = _ssd_scan_bwd(act, zx, dtb, alog, dsk, st_in, dy, dzx, name + "_scan_bwd")
    dzx, dconv_w, dconv_b = _ssm_conv_bwd(zx, pre, conv_w, dact, dzx, name + "_conv_bwd")
    return dzx, dconv_w, dconv_b, ddtb, dalog, ddsk, dnorm_w


ssm_core.defvjp(_ssm_core_vjp_fwd, _ssm_core_vjp_bwd)


def ssd_mixer_p(x, h, lin, j, carries, conv_w, conv_b, dtb, alog, dsk, norm_w, tag):
    zx = lin(h, ('ssm_w_in', j), F32, tag + "_in", carry=carries[0])
    return lin(ssm_core(zx, conv_w, conv_b, dtb, alog, dsk, norm_w, tag), ('ssm_w_out', j), F32, tag + "_out",
               carry=carries[1], res=x)


def _full_weight(n, parts):
    if SHARD_AXIS[n] - 1 == 0:
        return _join8(parts, 0)
    pieces = [parts[k] for k in range(N_DEV)]
    if n == 'ssm_w_in':
        pieces.append(jnp.zeros((parts.shape[1], SSM_IN_PAD - SSM_IN_WIDTH), parts.dtype))
    return jnp.concatenate(pieces, axis=1)


def trunk(w, x):
    ready = {}

    def lin(a, key, out_dtype, name, cols=None, carry=None, res=None, rope=None):
        wt = ready[key] if cols is None else ready[key][:, cols]
        if carry is None:
            assert rope is None
            return linear(a, wt, res, out_dtype, name)
        y, parts = linear_x(a, wt, res, rope, w[carry[0]][carry[1]], out_dtype, name)
        ready[carry] = _full_weight(carry[0], parts)
        return y

    for i in range(DEPTH):
        x, h = norm(x, w['mix_norm_w'][i], f"mixnorm{i}")
        j = i // 2
        ffn_next = [('ffn_w_up', i), ('ffn_w_down', i)]
        if i == 0:
            x = first_attention_mixer(x, h, lin, w, ready, [('attn_w_o', 0)] + ffn_next, "attn0")
        elif i % 2 == 0:
            x = attention_mixer_p(x, h, lin, j, [('attn_w_o', j)] + ffn_next, f"attn{j}")
        else:
            x = ssd_mixer_p(x, h, lin, j, ffn_next, w['ssm_conv_w'][j], w['ssm_conv_b'][j], w['ssm_dt_bias'][j],
                            w['ssm_a_log'][j], w['ssm_d'][j], w['ssm_norm_w'][j], f"ssm{j}")
        if i + 1 == DEPTH:
            mixer_next = [None, None]
        elif i % 2 == 0:
            mixer_next = [('ssm_w_in', j), ('ssm_w_out', j)]
        else:
            mixer_next = [('attn_w_qkv', j + 1), None]
        x, h = norm(x, w['ffn_norm_w'][i], f"ffnnorm{i}")
        u0 = lin(h, ('ffn_w_up', i), F32, f"ffn{i}_up", carry=mixer_next[0])
        a = ffn_mid(u0, w['ffn_conv_w'][i], w['ffn_conv_b'][i], f"ffn{i}_mid")
        x = lin(a, ('ffn_w_down', i), F32, f"ffn{i}_down", carry=mixer_next[1], res=x)
    return x


def local_step(w, x, target):
    final_w = w['final_norm_w']
    trunk_w = {n: a for n, a in w.items() if n != 'final_norm_w'}
    xf, pullback = jax.vjp(trunk, trunk_w, x)
    loss, dxf, dfinal = loss_head(xf, final_w, target, "loss_head")
    gw, gx = pullback(dxf)
    gw['final_norm_w'] = dfinal
    return loss, gw, gx


def _adam_math(w, g, m, v):
    m = ADAM_B1 * m + (1.0 - ADAM_B1) * g
    v = ADAM_B2 * v + (1.0 - ADAM_B2) * (g * g)
    m_hat = m / (1.0 - ADAM_B1 ** ADAM_STEP)
    v_hat = v / (1.0 - ADAM_B2 ** ADAM_STEP)
    delta = -ADAM_LR * (m_hat / (jnp.sqrt(v_hat) + ADAM_EPS) + ADAM_WD * w)
    return delta, m, v


def _adamw_rows(g, w, m, v, name):
    r, c = w.shape
    tr = _pick(r, (256, 128, 64, 32, 16, 8))

    def body(g_ref, w_ref, m_ref, v_ref, d_out, m_out, v_out):
        delta, mm, vv = _adam_math(w_ref[...], g_ref[...], m_ref[...], v_ref[...])
        d_out[...] = delta
        m_out[...] = mm
        v_out[...] = vv

    blk = pl.BlockSpec((tr, c), lambda i: (i, 0))
    return pl.pallas_call(
        body, name=name, out_shape=[jax.ShapeDtypeStruct((r, c), F32)] * 3, grid=(r // tr,),
        in_specs=[blk] * 4, out_specs=[blk] * 3, compiler_params=_params("parallel"),
    )(g, w, m, v)


def _sum8(pieces, name):
    _, r, c = pieces.shape

    def body(p_ref, o_ref):
        g = p_ref[0]
        for j in range(1, N_DEV):
            g = g + p_ref[j]
        o_ref[...] = g

    return pl.pallas_call(
        body, name=name, out_shape=jax.ShapeDtypeStruct((r, c), F32),
        in_specs=[pl.BlockSpec(memory_space=pltpu.VMEM)], out_specs=pl.BlockSpec(memory_space=pltpu.VMEM),
    )(pieces)


def _adamw_plain(g, w, m, v, name):
    def body(g_ref, w_ref, m_ref, v_ref, d_out, m_out, v_out):
        delta, mm, vv = _adam_math(w_ref[...], g_ref[...], m_ref[...], v_ref[...])
        d_out[...] = delta
        m_out[...] = mm
        v_out[...] = vv

    vm = pl.BlockSpec(memory_space=pltpu.VMEM)
    return pl.pallas_call(
        body, name=name, out_shape=[jax.ShapeDtypeStruct(g.shape, F32)] * 3,
        in_specs=[vm] * 4, out_specs=[vm] * 3,
    )(g, w, m, v)


def _join8(parts, axis):
    t = jnp.moveaxis(parts, 0, axis)
    shp = t.shape
    return t.reshape(shp[:axis] + (shp[axis] * shp[axis + 1],) + shp[axis + 2:])


def _pack(arrs, lead, mult):
    flat = jnp.concatenate([a.reshape(a.shape[:lead] + (-1,)) for a in arrs], axis=-1)
    return _pad_rows(flat, mult)


def _unpack(buf, shapes, lead):
    flat = buf.reshape(buf.shape[:lead] + (-1,))
    out, off = [], 0
    for shp in shapes:
        n = math.prod(shp)
        out.append(flat[..., off:off + n].reshape(flat.shape[:lead] + tuple(shp)))
        off += n
    return out


def _own_shard(full, axis):
    size = full.shape[axis] // N_DEV
    return lax.dynamic_slice_in_dim(full, _my_index() * size, size, axis)


def kernel(x, mix_norm_w, attn_w_qkv, attn_w_o, ssm_w_in, ssm_conv_w, ssm_conv_b, ssm_dt_bias, ssm_a_log, ssm_d, ssm_norm_w, ssm_w_out, ffn_norm_w, ffn_w_up, ffn_conv_w, ffn_conv_b, ffn_w_down, final_norm_w, loss_target, m_mix_norm_w, m_attn_w_qkv, m_attn_w_o, m_ssm_w_in, m_ssm_conv_w, m_ssm_conv_b, m_ssm_dt_bias, m_ssm_a_log, m_ssm_d, m_ssm_norm_w, m_ssm_w_out, m_ffn_norm_w, m_ffn_w_up, m_ffn_conv_w, m_ffn_conv_b, m_ffn_w_down, m_final_norm_w, v_mix_norm_w, v_attn_w_qkv, v_attn_w_o, v_ssm_w_in, v_ssm_conv_w, v_ssm_conv_b, v_ssm_dt_bias, v_ssm_a_log, v_ssm_d, v_ssm_norm_w, v_ssm_w_out, v_ffn_norm_w, v_ffn_w_up, v_ffn_conv_w, v_ffn_conv_b, v_ffn_w_down, v_final_norm_w):
    args = dict(locals())
    w_sh = {n: args[n] for n in WEIGHT_NAMES}
    m_sh = {n: args["m_" + n] for n in WEIGHT_NAMES}
    v_sh = {n: args["v_" + n] for n in WEIGHT_NAMES}

    small_shapes = [w_sh[n].shape for n in SMALL_SHARDED]
    small = _exchange(_pack([w_sh[n] for n in SMALL_SHARDED], 0, 8), True, "gather_small")
    full = {n: w_sh[n] for n in SMALL if SHARD_AXIS[n] is None}
    for n, parts in zip(SMALL_SHARDED, _unpack(small, small_shapes, 1)):
        full[n] = _join8(parts, SHARD_AXIS[n])
    for n in BIG:
        full[n] = [w_sh[n][j] for j in range(w_sh[n].shape[0])]

    loss, gw, gx = local_step(full, x[0], loss_target[0])
    loss = lax.psum(loss, ("x", "y", "c"))
    for n in BIG:
        gw[n] = jnp.stack(gw[n])

    grads, deltas, new_m, new_v = {}, {}, {}, {}
    for n in BIG:
        shp = w_sh[n].shape
        two_d = (shp[0] * shp[1], shp[2])
        outs = _adamw_rows(*[t.reshape(two_d) for t in (gw[n], w_sh[n], m_sh[n], v_sh[n])], "adamw_" + n)
        grads[n] = gw[n]
        deltas[n], new_m[n], new_v[n] = [o.reshape(shp) for o in outs]

    small_full_shapes = [gw[n].shape for n in SMALL]
    gsmall = _exchange(_pack([gw[n] for n in SMALL], 0, 8), True, "gather_small_grads")
    gsmall = _unpack(_sum8(gsmall, "sum_small_grads"), small_full_shapes, 0)
    for n, g in zip(SMALL, gsmall):
        grads[n] = g if SHARD_AXIS[n] is None else _own_shard(g, SHARD_AXIS[n])
    shapes = [w_sh[n].shape for n in SMALL]
    outs = _adamw_plain(*[_pack([d[n] for n in SMALL], 0, 8) for d in (grads, w_sh, m_sh, v_sh)], "adamw_small")
    for d, buf in zip((deltas, new_m, new_v), outs):
        for n, a in zip(SMALL, _unpack(buf, shapes, 0)):
            d[n] = a

    return (loss, gx[None], *[grads[n] for n in WEIGHT_NAMES], *[deltas[n] for n in WEIGHT_NAMES],
            *[new_m[n] for n in WEIGHT_NAMES], *[new_v[n] for n in WEIGHT_NAMES])
```
